```python
import math
import jax, jax.numpy as jnp
from jax import lax
import numpy as np


D_MODEL = 1024
BATCH = 8
SEQ = 2048
DEPTH = 1
DEC_BATCH = 8
DEC_SEQ = 4096
PAST_LEN = 128

HEAD_DIM = 64
NA_HEADS = D_MODEL // (2 * HEAD_DIM)
NA_WIDTH = NA_HEADS * HEAD_DIM
DIFF_HEADS = D_MODEL // (4 * HEAD_DIM)
DIFF_WIDTH = DIFF_HEADS * 2 * HEAD_DIM
MIX_WIDTH = NA_WIDTH + DIFF_WIDTH
QKV_COLS = 3 * NA_WIDTH + 3 * DIFF_WIDTH
GRID_W = 64
NA_WIN_ROWS = 8
NA_WIN_COLS = 16
ROPE_THETA = 10000.0
Q_BLOCK = 128
N_EXPERTS = 32
TOP_K = 4
D_FF = D_MODEL
SWIGLU_LIMIT = 7.0
SWIGLU_ALPHA = 1.702
EXPERT_BLOCK = 512
EPS = 1e-5
NEG_INF = -1e30

kernel_name = 'hybrid_na_diffattn_moe_encoder'


def lambda_init_fn(layer):
    return 0.8 - 0.6 * math.exp(-0.3 * layer)


def rms_norm(x, g):
    xf = x.astype(jnp.float32)
    y = xf * lax.rsqrt(jnp.mean(xf * xf, axis=-1, keepdims=True) + EPS)
    return (y * g.astype(jnp.float32)).astype(x.dtype)


def rope(x, pos):
    half = HEAD_DIM // 2
    inv = ROPE_THETA ** (-jnp.arange(half, dtype=jnp.float32) / half)
    ang = pos.astype(jnp.float32)[:, None] * inv[None, :]
    cos, sin = jnp.cos(ang), jnp.sin(ang)
    xf = x.astype(jnp.float32)
    x1, x2 = xf[..., :half], xf[..., half:]
    return jnp.concatenate([x1 * cos - x2 * sin, x2 * cos + x1 * sin], axis=-1).astype(x.dtype)


def neighbourhood_attention(q, k, v, rpb):
    B, H, T, Dh = q.shape
    rows = T // GRID_W
    kh = min(NA_WIN_ROWS, rows)
    qg = q.reshape(B, H, rows, GRID_W, Dh)
    kg = k.reshape(B, H, rows, GRID_W, Dh)
    vg = v.reshape(B, H, rows, GRID_W, Dh)
    cols = jnp.arange(GRID_W, dtype=jnp.int32)
    c_start = jnp.clip(cols - NA_WIN_COLS // 2, 0, GRID_W - NA_WIN_COLS)
    col_mask = (cols[None, :] >= c_start[:, None]) & (cols[None, :] < c_start[:, None] + NA_WIN_COLS)
    col_idx = jnp.clip(cols[None, :] - cols[:, None], -(NA_WIN_COLS - 1), NA_WIN_COLS - 1) + NA_WIN_COLS - 1
    scale = Dh ** -0.5

    def row_block(r):
        r_start = jnp.clip(r - kh // 2, 0, rows - kh)
        q_r = lax.dynamic_index_in_dim(qg, r, axis=2, keepdims=False)
        k_r = lax.dynamic_slice_in_dim(kg, r_start, kh, axis=2)
        v_r = lax.dynamic_slice_in_dim(vg, r_start, kh, axis=2)
        row_idx = r_start + jnp.arange(kh, dtype=jnp.int32) - r + NA_WIN_ROWS - 1
        bias = rpb[:, row_idx[None, :, None], col_idx[:, None, :]]
        s = jnp.einsum('bhqd,bhrkd->bhqrk', q_r, k_r).astype(jnp.float32) * scale + bias.astype(jnp.float32)[None]
        s = jnp.where(col_mask[:, None, :], s, NEG_INF)
        p = jax.nn.softmax(s.reshape(B, H, GRID_W, kh * GRID_W), axis=-1)
        p = p.reshape(B, H, GRID_W, kh, GRID_W).astype(v.dtype)
        return jnp.einsum('bhqrk,bhrkd->bhqd', p, v_r)

    out = lax.map(row_block, jnp.arange(rows, dtype=jnp.int32))
    return jnp.moveaxis(out, 0, 2).reshape(B, H, T, Dh)


def differential_attention(q, k, v, lam, subln_g, lambda_init):
    B, H, _, T, Dh = q.shape
    nb = T // Q_BLOCK
    qb = jnp.moveaxis(q.reshape(B, H, 2, nb, Q_BLOCK, Dh), 3, 0)
    scale = Dh ** -0.5

    def block(q_blk):
        s = jnp.einsum('bhcqd,bhckd->bhcqk', q_blk, k).astype(jnp.float32) * scale
        p = jax.nn.softmax(s, axis=-1)
        a = p[:, :, 0] - lam * p[:, :, 1]
        return jnp.einsum('bhqk,bhkd->bhqd', a.astype(v.dtype), v)

    o = lax.map(block, qb)
    o = jnp.moveaxis(o, 0, 2).reshape(B, H, T, 2 * Dh)
    return rms_norm(o, subln_g) * (1.0 - lambda_init)


def mixer(h, pos, lambda_init, w_qkv, na_q_norm, na_k_norm, na_rpb, diff_q_norm, diff_k_norm,
          lambda_q1, lambda_k1, lambda_q2, lambda_k2, diff_subln, w_o):
    B, T, _ = h.shape
    qkv = h @ w_qkv
    cuts = [NA_WIDTH, 2 * NA_WIDTH, 3 * NA_WIDTH, 3 * NA_WIDTH + DIFF_WIDTH, 3 * NA_WIDTH + 2 * DIFF_WIDTH]
    na_q, na_k, na_v, df_q, df_k, df_v = jnp.split(qkv, cuts, axis=-1)

    def heads(t, n, d):
        return t.reshape(B, T, n, d).transpose(0, 2, 1, 3)

    def pair_heads(t):
        return t.reshape(B, T, DIFF_HEADS, 2, HEAD_DIM).transpose(0, 2, 3, 1, 4)

    na_q = rms_norm(heads(na_q, NA_HEADS, HEAD_DIM), na_q_norm)
    na_k = rms_norm(heads(na_k, NA_HEADS, HEAD_DIM), na_k_norm)
    o_na = neighbourhood_attention(na_q, na_k, heads(na_v, NA_HEADS, HEAD_DIM), na_rpb)

    df_q = rope(rms_norm(pair_heads(df_q), diff_q_norm), pos)
    df_k = rope(rms_norm(pair_heads(df_k), diff_k_norm), pos)
    f32 = jnp.float32
    lam = (jnp.exp(jnp.sum(lambda_q1.astype(f32) * lambda_k1.astype(f32)))
           - jnp.exp(jnp.sum(lambda_q2.astype(f32) * lambda_k2.astype(f32))) + lambda_init)
    o_df = differential_attention(df_q, df_k, heads(df_v, DIFF_HEADS, 2 * HEAD_DIM), lam, diff_subln, lambda_init)

    o = jnp.concatenate([o_na.transpose(0, 2, 1, 3).reshape(B, T, NA_WIDTH),
                         o_df.transpose(0, 2, 1, 3).reshape(B, T, DIFF_WIDTH)], axis=-1)
    return o @ w_o


def moe_ffn(h, w_router, b_router, w_gate, b_gate, w_up, b_up, w_down, b_down):
    B, T, D = h.shape
    xt = h.reshape(-1, D)
    n = xt.shape[0]
    logits = (xt @ w_router + b_router).astype(jnp.float32)
    top_val, top_idx = lax.top_k(logits, TOP_K)
    top_w = jax.nn.softmax(top_val, axis=-1)
    nk = n * TOP_K
    flat_e = top_idx.reshape(-1).astype(jnp.int32)
    flat_tok = jnp.arange(nk, dtype=jnp.int32) // TOP_K
    flat_w = top_w.reshape(-1)
    order = jnp.argsort(flat_e, stable=True)
    sorted_e = flat_e[order]
    counts = jnp.bincount(flat_e, length=N_EXPERTS).astype(jnp.int32)
    padded = (counts + EXPERT_BLOCK - 1) // EXPERT_BLOCK * EXPERT_BLOCK
    start = jnp.cumsum(counts) - counts
    pad_end = jnp.cumsum(padded)
    pad_start = pad_end - padded
    dest = pad_start[sorted_e] + jnp.arange(nk, dtype=jnp.int32) - start[sorted_e]
    n_blocks = -(-nk // EXPERT_BLOCK) + N_EXPERTS
    cap = n_blocks * EXPERT_BLOCK
    buf_tok = jnp.full((cap,), n, jnp.int32).at[dest].set(flat_tok[order])
    buf_w = jnp.zeros((cap,), jnp.float32).at[dest].set(flat_w[order])
    block_e = jnp.minimum(jnp.searchsorted(pad_end, jnp.arange(n_blocks, dtype=jnp.int32) * EXPERT_BLOCK,
                                           side='right'), N_EXPERTS - 1)
    xt_pad = jnp.concatenate([xt, jnp.zeros((1, D), xt.dtype)], axis=0)

    def expert_block(y, blk):
        tok, wts, e = blk
        xb = xt_pad[tok]
        g = jnp.minimum(xb @ w_gate[e] + b_gate[e], SWIGLU_LIMIT)
        u = jnp.clip(xb @ w_up[e] + b_up[e], -SWIGLU_LIMIT, SWIGLU_LIMIT)
        act = g * jax.nn.sigmoid(SWIGLU_ALPHA * g) * (u + 1)
        out = act @ w_down[e] + b_down[e]
        return y.at[tok].add(out * wts[:, None].astype(out.dtype)), None

    y0 = jnp.zeros((n + 1, D), h.dtype)
    y, _ = lax.scan(expert_block, y0, (buf_tok.reshape(n_blocks, EXPERT_BLOCK),
                                       buf_w.reshape(n_blocks, EXPERT_BLOCK), block_e))
    return y[:n].reshape(B, T, D)


def encode(x, c, w_ada, b_ada, g_attn_norm, w_qkv, na_q_norm, na_k_norm, na_rpb, diff_q_norm, diff_k_norm,
           lambda_q1, lambda_k1, lambda_q2, lambda_k2, diff_subln, w_o, g_ffn_norm, w_router, b_router,
           w_gate, b_gate, w_up, b_up, w_down, b_down):
    pos = jnp.arange(x.shape[1], dtype=jnp.int32)
    for l in range(DEPTH):
        ada = jax.nn.silu(c) @ w_ada[l] + b_ada[l]
        sh1, sc1, gt1, sh2, sc2, gt2 = jnp.split(ada[:, None, :], 6, axis=-1)
        h = rms_norm(x, g_attn_norm[l]) * (1 + sc1) + sh1
        x = x + gt1 * mixer(h, pos, lambda_init_fn(l), w_qkv[l], na_q_norm[l], na_k_norm[l], na_rpb[l],
                            diff_q_norm[l], diff_k_norm[l], lambda_q1[l], lambda_k1[l], lambda_q2[l],
                            lambda_k2[l], diff_subln[l], w_o[l])
        h = rms_norm(x, g_ffn_norm[l]) * (1 + sc2) + sh2
        x = x + gt2 * moe_ffn(h, w_router[l], b_router[l], w_gate[l], b_gate[l], w_up[l], b_up[l],
                              w_down[l], b_down[l])
    return x


def setup_inputs(seed: int = 0) -> dict:
    key = jax.random.key(seed)
    ks = jax.random.split(key, 28)
    f32 = jnp.float32

    def nrm(k, shape, scale):
        return jax.random.normal(k, shape, f32) * scale

    L, D, E, F, Dh = DEPTH, D_MODEL, N_EXPERTS, D_FF, HEAD_DIM
    return {
        'x_prompt': nrm(ks[0], (BATCH, SEQ, D), 1.0),
        'x_sample': nrm(ks[1], (DEC_BATCH, DEC_SEQ, D), 1.0),
        'c_prompt': nrm(ks[2], (BATCH, D), 1.0),
        'c_sample': nrm(ks[3], (DEC_BATCH, D), 1.0),
        'w_ada': nrm(ks[4], (L, D, 6 * D), 0.5 * D ** -0.5),
        'b_ada': nrm(ks[5], (L, 6 * D), 0.02),
        'g_attn_norm': 1.0 + nrm(ks[6], (L, D), 0.02),
        'w_qkv': nrm(ks[7], (L, D, QKV_COLS), D ** -0.5),
        'na_q_norm': 1.0 + nrm(ks[8], (L, Dh), 0.02),
        'na_k_norm': 1.0 + nrm(ks[9], (L, Dh), 0.02),
        'na_rpb': nrm(ks[10], (L, NA_HEADS, 2 * NA_WIN_ROWS - 1, 2 * NA_WIN_COLS - 1), 0.1),
        'diff_q_norm': 1.0 + nrm(ks[11], (L, Dh), 0.02),
        'diff_k_norm': 1.0 + nrm(ks[12], (L, Dh), 0.02),
        'lambda_q1': nrm(ks[13], (L, Dh), 0.1),
        'lambda_k1': nrm(ks[14], (L, Dh), 0.1),
        'lambda_q2': nrm(ks[15], (L, Dh), 0.1),
        'lambda_k2': nrm(ks[16], (L, Dh), 0.1),
        'diff_subln': 1.0 + nrm(ks[17], (L, 2 * Dh), 0.02),
        'w_o': nrm(ks[18], (L, MIX_WIDTH, D), MIX_WIDTH ** -0.5),
        'g_ffn_norm': 1.0 + nrm(ks[19], (L, D), 0.02),
        'w_router': nrm(ks[20], (L, D, E), D ** -0.5),
        'b_router': nrm(ks[21], (L, E), 0.01),
        'w_gate': nrm(ks[22], (L, E, D, F), D ** -0.5),
        'b_gate': nrm(ks[23], (L, E, F), 0.02),
        'w_up': nrm(ks[24], (L, E, D, F), D ** -0.5),
        'b_up': nrm(ks[25], (L, E, F), 0.02),
        'w_down': nrm(ks[26], (L, E, F, D), F ** -0.5),
        'b_down': nrm(ks[27], (L, E, D), 0.02),
    }


def reference(x_prompt, x_sample, c_prompt, c_sample, w_ada, b_ada, g_attn_norm, w_qkv, na_q_norm, na_k_norm,
              na_rpb, diff_q_norm, diff_k_norm, lambda_q1, lambda_k1, lambda_q2, lambda_k2, diff_subln, w_o,
              g_ffn_norm, w_router, b_router, w_gate, b_gate, w_up, b_up, w_down, b_down):
    y_prompt = encode(x_prompt, c_prompt, w_ada, b_ada, g_attn_norm, w_qkv, na_q_norm, na_k_norm, na_rpb,
                      diff_q_norm, diff_k_norm, lambda_q1, lambda_k1, lambda_q2, lambda_k2, diff_subln, w_o,
                      g_ffn_norm, w_router, b_router, w_gate, b_gate, w_up, b_up, w_down, b_down)
    y_sample = encode(x_sample, c_sample, w_ada, b_ada, g_attn_norm, w_qkv, na_q_norm, na_k_norm, na_rpb,
                      diff_q_norm, diff_k_norm, lambda_q1, lambda_k1, lambda_q2, lambda_k2, diff_subln, w_o,
                      g_ffn_norm, w_router, b_router, w_gate, b_gate, w_up, b_up, w_down, b_down)
    return (y_prompt, y_sample)
```

```python
import functools
import math

import jax
import jax.numpy as jnp
from jax import lax
from jax.experimental import pallas as pl
from jax.experimental.pallas import tpu as pltpu

F32 = jnp.float32
BF16 = jnp.bfloat16

D_MODEL = 1024
HEAD_DIM = 64
NA_HEADS = 8
NA_WIDTH = 512
DIFF_HEADS = 4
DIFF_WIDTH = 512
QKV_COLS = 3072
GRID_W = 64
NA_WIN_ROWS = 8
NA_WIN_COLS = 16
ROPE_THETA = 10000.0
N_EXPERTS = 32
TOP_K = 4
SWIGLU_LIMIT = 7.0
SWIGLU_ALPHA = 1.702
EPS = 1e-5
NEG_INF = -1e30
LAMBDA_INIT = 0.8 - 0.6 * math.exp(-0.3 * 0)
LOG2E = 1.4426950408889634

LANES = 128
MXU_DIM = 256
VMEM_LIMIT = 56 * 1024 * 1024

ROW_BLOCK = 512
Q_BLOCK = 256
EXPERT_BLOCK = 512


def _params(*sem):
    return pltpu.CompilerParams(dimension_semantics=sem, vmem_limit_bytes=VMEM_LIMIT)


def _ada_kernel(c_ref, w_ref, b_ref, o_ref):
    c = c_ref[...]
    s = c * jax.nn.sigmoid(c)
    o_ref[...] = jnp.dot(s, w_ref[...], preferred_element_type=F32,
                         precision=lax.Precision.HIGHEST) + b_ref[...]


def _ada(c_all, w_ada, b_ada):
    nb = c_all.shape[0]
    n_out = w_ada.shape[1]
    blk = D_MODEL
    return pl.pallas_call(
        _ada_kernel,
        grid=(n_out // blk,),
        in_specs=[pl.BlockSpec((nb, D_MODEL), lambda j: (0, 0)),
                  pl.BlockSpec((D_MODEL, blk), lambda j: (0, j)),
                  pl.BlockSpec((1, blk), lambda j: (0, j))],
        out_specs=pl.BlockSpec((nb, blk), lambda j: (0, j)),
        out_shape=jax.ShapeDtypeStruct((nb, n_out), F32),
        compiler_params=_params("arbitrary"),
        name="ada",
    )(c_all, w_ada, b_ada.reshape(1, n_out))


def _head_sumsq(y, bd):
    sq = (y * y).astype(BF16)
    parts = [jnp.dot(sq[:, c:c + MXU_DIM], bd, preferred_element_type=F32)
             for c in range(0, y.shape[1], MXU_DIM)]
    return jnp.concatenate(parts, axis=1)


def _qkv_kernel(x_ref, ada_ref, g_ref, w_ref, gain_ref, cos_ref, sin_ref, bd_ref, o_ref):
    x = x_ref[...]
    ms = jnp.mean(x * x, axis=-1, keepdims=True)
    xn = x * lax.rsqrt(ms + EPS) * g_ref[...]
    sh = ada_ref[0, 0:1, :]
    sc = ada_ref[0, 1:2, :]
    h = (xn * (1.0 + sc) + sh).astype(BF16)
    bd = bd_ref[...]
    lane = lax.broadcasted_iota(jnp.int32, (x.shape[0], NA_WIDTH), 1)
    first_half = (lane & (HEAD_DIM // 2)) == 0
    for grp in range(6):
        cols = slice(grp * 512, (grp + 1) * 512)
        acc = jnp.dot(h, w_ref[:, cols], preferred_element_type=F32)
        if grp in (2, 5):
            o_ref[:, cols] = acc.astype(BF16)
            continue
        gi = {0: 0, 1: 1, 3: 2, 4: 3}[grp]
        ss = _head_sumsq(acc, bd)
        y = acc * lax.rsqrt(ss * (1.0 / HEAD_DIM) + EPS) * gain_ref[gi:gi + 1, :]
        if grp in (3, 4):
            partner = jnp.where(first_half,
                                pltpu.roll(y, NA_WIDTH - HEAD_DIM // 2, axis=1),
                                pltpu.roll(y, HEAD_DIM // 2, axis=1))
            y = y * cos_ref[...] + partner * sin_ref[...]
        o_ref[:, cols] = y.astype(BF16)


def _qkv(x2d, ada_g, g_attn, w_qkv_bf, gains, cos_t, sin_t, bd, seq):
    n = x2d.shape[0]
    tm = ROW_BLOCK
    per_seq = seq // tm
    return pl.pallas_call(
        _qkv_kernel,
        grid=(n // tm,),
        in_specs=[pl.BlockSpec((tm, D_MODEL), lambda i: (i, 0)),
                  pl.BlockSpec((1, 6, D_MODEL), lambda i: (i // per_seq, 0, 0)),
                  pl.BlockSpec((1, D_MODEL), lambda i: (0, 0)),
                  pl.BlockSpec((D_MODEL, QKV_COLS), lambda i: (0, 0)),
                  pl.BlockSpec((4, NA_WIDTH), lambda i: (0, 0)),
                  pl.BlockSpec((tm, DIFF_WIDTH), lambda i: (i % per_seq, 0)),
                  pl.BlockSpec((tm, DIFF_WIDTH), lambda i: (i % per_seq, 0)),
                  pl.BlockSpec((MXU_DIM, MXU_DIM), lambda i: (0, 0))],
        out_specs=pl.BlockSpec((tm, QKV_COLS), lambda i: (i, 0)),
        out_shape=jax.ShapeDtypeStruct((n, QKV_COLS), BF16),
        compiler_params=_params("arbitrary"),
        name="qkv",
    )(x2d, ada_g, g_attn, w_qkv_bf, gains, cos_t, sin_t, bd)


def _na_kernel(q_ref, k_ref, v_ref, bias_ref, o_ref, *, rows):
    lane = lax.broadcasted_iota(jnp.int32, (GRID_W, LANES), 1)
    head0 = lane < HEAD_DIM
    win = NA_WIN_ROWS * GRID_W

    def body(r, carry):
        r_start = jnp.clip(r - NA_WIN_ROWS // 2, 0, rows - NA_WIN_ROWS)
        delta = r - r_start
        q = q_ref[pl.ds(pl.multiple_of(r * GRID_W, GRID_W), GRID_W), :]
        k0 = pl.multiple_of(r_start * GRID_W, GRID_W)
        kw = k_ref[pl.ds(k0, win), :]
        vw = v_ref[pl.ds(k0, win), :]
        zero = jnp.zeros_like(q)
        qm = jnp.concatenate([jnp.where(head0, q, zero), jnp.where(head0, zero, q)], axis=0)
        s = lax.dot_general(kw, qm, (((1,), (1,)), ((), ())), preferred_element_type=F32)
        s = s + bias_ref[0, delta]
        m = jnp.max(s, axis=0, keepdims=True)
        p = jnp.exp2(s - m)
        l = jnp.sum(p, axis=0, keepdims=True)
        p = (p * (1.0 / l)).astype(BF16)
        o2 = lax.dot_general(p, vw, (((0,), (0,)), ((), ())), preferred_element_type=F32)
        o = jnp.where(head0, o2[:GRID_W], o2[GRID_W:])
        o_ref[pl.ds(pl.multiple_of(r * GRID_W, GRID_W), GRID_W), :] = o.astype(BF16)
        return carry

    lax.fori_loop(0, rows, body, 0)


def _na(qkv, bias_t, batch, seq):
    rows = seq // GRID_W
    n_pairs = NA_HEADS // 2
    return pl.pallas_call(
        functools.partial(_na_kernel, rows=rows),
        grid=(batch, n_pairs),
        in_specs=[pl.BlockSpec((seq, LANES), lambda b, hp: (b, hp)),
                  pl.BlockSpec((seq, LANES), lambda b, hp: (b, n_pairs + hp)),
                  pl.BlockSpec((seq, LANES), lambda b, hp: (b, 2 * n_pairs + hp)),
                  pl.BlockSpec((1, NA_WIN_ROWS, NA_WIN_ROWS * GRID_W, LANES), lambda b, hp: (hp, 0, 0, 0))],
        out_specs=pl.BlockSpec((seq, LANES), lambda b, hp: (b, hp)),
        out_shape=jax.ShapeDtypeStruct((batch * seq, NA_WIDTH), BF16),
        compiler_params=_params("arbitrary", "arbitrary"),
        name="na_attn",
    )(qkv, qkv, qkv, bias_t)


def _diff_kernel(q_ref, k_ref, v_ref, lam_ref, g_ref, o_ref, vt_ref):
    @pl.when(pl.program_id(2) == 0)
    def _():
        vt_ref[...] = v_ref[...].astype(F32).T.astype(BF16)

    lq1 = lam_ref[0:1, :]
    lk1 = lam_ref[1:2, :]
    lq2 = lam_ref[2:3, :]
    lk2 = lam_ref[3:4, :]
    lam = (jnp.exp(jnp.sum(lq1 * lk1, axis=-1, keepdims=True))
           - jnp.exp(jnp.sum(lq2 * lk2, axis=-1, keepdims=True)) + LAMBDA_INIT)

    q = q_ref[...]
    k = k_ref[...]
    vt = vt_ref[...]
    lane = lax.broadcasted_iota(jnp.int32, q.shape, 1)
    zero = jnp.zeros_like(q)
    outs = []
    for comp in range(2):
        sel = (lane < HEAD_DIM) if comp == 0 else (lane >= HEAD_DIM)
        qc = jnp.where(sel, q, zero)
        s = lax.dot_general(k, qc, (((1,), (1,)), ((), ())), preferred_element_type=F32)
        m = jnp.max(s, axis=0, keepdims=True)
        p = jnp.exp2(s - m)
        l = jnp.sum(p, axis=0, keepdims=True)
        o = jnp.dot(vt, p.astype(BF16), preferred_element_type=F32)
        outs.append(o * (1.0 / l))
    o = outs[0] - lam * outs[1]
    ms = jnp.mean(o * o, axis=0, keepdims=True)
    y = o * lax.rsqrt(ms + EPS) * g_ref[...]
    o_ref[...] = y.T.astype(BF16)


def _diff(qkv, lam_vecs, subln_col, batch, seq):
    tq = Q_BLOCK
    nq = seq // tq
    base = 3 * NA_WIDTH // LANES
    nh = DIFF_HEADS
    return pl.pallas_call(
        _diff_kernel,
        grid=(batch, nh, nq),
        in_specs=[pl.BlockSpec((tq, LANES), lambda b, h, i: (b * nq + i, base + h)),
                  pl.BlockSpec((seq, LANES), lambda b, h, i: (b, base + nh + h)),
                  pl.BlockSpec((seq, LANES), lambda b, h, i: (b, base + 2 * nh + h)),
                  pl.BlockSpec((4, HEAD_DIM), lambda b, h, i: (0, 0)),
                  pl.BlockSpec((LANES, 1), lambda b, h, i: (0, 0))],
        out_specs=pl.BlockSpec((tq, LANES), lambda b, h, i: (b * nq + i, h)),
        out_shape=jax.ShapeDtypeStruct((batch * seq, DIFF_WIDTH), BF16),
        scratch_shapes=[pltpu.VMEM((LANES, seq), BF16)],
        compiler_params=_params("arbitrary", "arbitrary", "arbitrary"),
        name="diff_attn",
    )(qkv, qkv, qkv, lam_vecs, subln_col)


def _wo_kernel(ona_ref, odf_ref, x_ref, ada_ref, wo_ref, g_ref, wrh_ref, wrl_ref, br_ref,
               x1_ref, h2_ref, ti_ref, tw_ref):
    mix = (jnp.dot(ona_ref[...], wo_ref[:NA_WIDTH, :], preferred_element_type=F32)
           + jnp.dot(odf_ref[...], wo_ref[NA_WIDTH:, :], preferred_element_type=F32))
    gt1 = ada_ref[0, 2:3, :]
    sh2 = ada_ref[0, 3:4, :]
    sc2 = ada_ref[0, 4:5, :]
    x1 = x_ref[...] + gt1 * mix
    x1_ref[...] = x1
    ms = jnp.mean(x1 * x1, axis=-1, keepdims=True)
    h2 = x1 * lax.rsqrt(ms + EPS) * g_ref[...] * (1.0 + sc2) + sh2
    hi = h2.astype(BF16)
    h2_ref[...] = hi
    lo = (h2 - hi.astype(F32)).astype(BF16)
    logits = (jnp.dot(hi, wrh_ref[...], preferred_element_type=F32)
              + jnp.dot(hi, wrl_ref[...], preferred_element_type=F32)
              + jnp.dot(lo, wrh_ref[...], preferred_element_type=F32)) + br_ref[...]
    lane = lax.broadcasted_iota(jnp.int32, logits.shape, 1).astype(F32)
    vals = []
    idxs = []
    cur = logits
    for _ in range(TOP_K):
        m = jnp.max(cur, axis=-1, keepdims=True)
        idx = jnp.min(jnp.where(cur == m, lane, float(LANES)), axis=-1, keepdims=True)
        vals.append(m)
        idxs.append(idx)
        cur = jnp.where(lane == idx, -jnp.inf, cur)
    es = [jnp.exp(v - vals[0]) for v in vals]
    inv = 1.0 / (es[0] + es[1] + es[2] + es[3])
    ti = jnp.zeros(logits.shape, F32)
    tw = jnp.zeros(logits.shape, F32)
    for j in range(TOP_K):
        ti = jnp.where(lane == float(j), idxs[j], ti)
        tw = jnp.where(lane == float(j), es[j] * inv, tw)
    ti_ref[...] = ti.astype(jnp.int32)
    tw_ref[...] = tw


def _wo(o_na, o_df, x2d, ada_g, w_o_bf, g_ffn, wr_hi, wr_lo, br_pad, seq):
    n = x2d.shape[0]
    tm = ROW_BLOCK
    per_seq = seq // tm
    row = lambda i: (i, 0)
    const = lambda i: (0, 0)
    return pl.pallas_call(
        _wo_kernel,
        grid=(n // tm,),
        in_specs=[pl.BlockSpec((tm, NA_WIDTH), row),
                  pl.BlockSpec((tm, DIFF_WIDTH), row),
                  pl.BlockSpec((tm, D_MODEL), row),
                  pl.BlockSpec((1, 6, D_MODEL), lambda i: (i // per_seq, 0, 0)),
                  pl.BlockSpec((D_MODEL, D_MODEL), const),
                  pl.BlockSpec((1, D_MODEL), const),
                  pl.BlockSpec((D_MODEL, LANES), const),
                  pl.BlockSpec((D_MODEL, LANES), const),
                  pl.BlockSpec((1, LANES), const)],
        out_specs=[pl.BlockSpec((tm, D_MODEL), row),
                   pl.BlockSpec((tm, D_MODEL), row),
                   pl.BlockSpec((tm, LANES), row),
                   pl.BlockSpec((tm, LANES), row)],
        out_shape=[jax.ShapeDtypeStruct((n, D_MODEL), F32),
                   jax.ShapeDtypeStruct((n, D_MODEL), BF16),
                   jax.ShapeDtypeStruct((n, LANES), jnp.int32),
                   jax.ShapeDtypeStruct((n, LANES), F32)],
        compiler_params=_params("arbitrary"),
        name="wo_router",
    )(o_na, o_df, x2d, ada_g, w_o_bf, g_ffn, wr_hi, wr_lo, br_pad)


def _expert_kernel(be_ref, na_ref, xs_ref, wg_ref, bg_ref, wu_ref, bu_ref, wd_ref, bd_ref, o_ref):
    i = pl.program_id(0)

    @pl.when(i < na_ref[0])
    def _():
        xb = xs_ref[...]
        g = jnp.minimum(jnp.dot(xb, wg_ref[0], preferred_element_type=F32) + bg_ref[0], SWIGLU_LIMIT)
        u = jnp.clip(jnp.dot(xb, wu_ref[0], preferred_element_type=F32) + bu_ref[0],
                     -SWIGLU_LIMIT, SWIGLU_LIMIT)
        act = g * jax.nn.sigmoid(SWIGLU_ALPHA * g) * (u + 1.0)
        out = jnp.dot(act.astype(BF16), wd_ref[0], preferred_element_type=F32) + bd_ref[0]
        o_ref[...] = out.astype(BF16)

    @pl.when(i >= na_ref[0])
    def _():
        o_ref[...] = jnp.zeros_like(o_ref)


def _experts(block_e, n_active, xs, wg, bg, wu, bu, wd, bd):
    cap = xs.shape[0]
    n_blocks = cap // EXPERT_BLOCK
    xmap = lambda i, be, na: (jnp.minimum(i, na[0] - 1), 0)
    wmap = lambda i, be, na: (be[i], 0, 0)
    grid_spec = pltpu.PrefetchScalarGridSpec(
        num_scalar_prefetch=2,
        grid=(n_blocks,),
        in_specs=[pl.BlockSpec((EXPERT_BLOCK, D_MODEL), xmap),
                  pl.BlockSpec((1, D_MODEL, D_MODEL), wmap),
                  pl.BlockSpec((1, 1, D_MODEL), wmap),
                  pl.BlockSpec((1, D_MODEL, D_MODEL), wmap),
                  pl.BlockSpec((1, 1, D_MODEL), wmap),
                  pl.BlockSpec((1, D_MODEL, D_MODEL), wmap),
                  pl.BlockSpec((1, 1, D_MODEL), wmap)],
        out_specs=pl.BlockSpec((EXPERT_BLOCK, D_MODEL), lambda i, be, na: (i, 0)),
    )
    return pl.pallas_call(
        _expert_kernel,
        grid_spec=grid_spec,
        out_shape=jax.ShapeDtypeStruct((cap, D_MODEL), BF16),
        compiler_params=_params("arbitrary"),
        name="experts",
    )(block_e, n_active, xs, wg, bg, wu, bu, wd, bd)


def _combine_kernel(x1_ref, ys_ref, tw_ref, ada_ref, o_ref):
    tw = tw_ref[...]
    acc = jnp.zeros(x1_ref.shape, F32)
    for j in range(TOP_K):
        acc = acc + tw[:, j:j + 1] * ys_ref[:, j * D_MODEL:(j + 1) * D_MODEL].astype(F32)
    o_ref[...] = x1_ref[...] + ada_ref[0, 5:6, :] * acc


def _combine(x1, ys, tw, ada_g, seq, row_off):
    n = x1.shape[0]
    tm = ROW_BLOCK
    per_seq = seq // tm
    off = row_off // tm
    return pl.pallas_call(
        _combine_kernel,
        grid=(n // tm,),
        in_specs=[pl.BlockSpec((tm, D_MODEL), lambda i: (i, 0)),
                  pl.BlockSpec((tm, TOP_K * D_MODEL), lambda i: (i + off, 0)),
                  pl.BlockSpec((tm, LANES), lambda i: (i + off, 0)),
                  pl.BlockSpec((1, 6, D_MODEL), lambda i: (i // per_seq, 0, 0))],
        out_specs=pl.BlockSpec((tm, D_MODEL), lambda i: (i, 0)),
        out_shape=jax.ShapeDtypeStruct((n, D_MODEL), F32),
        compiler_params=_params("arbitrary"),
        name="combine",
    )(x1, ys, tw, ada_g)


def _rope_tables(seq):
    half = HEAD_DIM // 2
    inv = ROPE_THETA ** (-jnp.arange(half, dtype=F32) / half)
    ang = jnp.arange(seq, dtype=F32)[:, None] * inv[None, :]
    cos, sin = jnp.cos(ang), jnp.sin(ang)
    cos_h = jnp.concatenate([cos, cos], axis=-1)
    sin_h = jnp.concatenate([-sin, sin], axis=-1)
    reps = DIFF_WIDTH // HEAD_DIM
    return jnp.tile(cos_h, (1, reps)), jnp.tile(sin_h, (1, reps))


def _na_bias_table(rpb):
    cols = jnp.arange(GRID_W, dtype=jnp.int32)
    c_start = jnp.clip(cols - NA_WIN_COLS // 2, 0, GRID_W - NA_WIN_COLS)
    col_mask = (cols[None, :] >= c_start[:, None]) & (cols[None, :] < c_start[:, None] + NA_WIN_COLS)
    col_idx = jnp.clip(cols[None, :] - cols[:, None], -(NA_WIN_COLS - 1), NA_WIN_COLS - 1) + NA_WIN_COLS - 1
    delta = jnp.arange(NA_WIN_ROWS, dtype=jnp.int32)
    j = jnp.arange(NA_WIN_ROWS, dtype=jnp.int32)
    row_idx = j[None, :] - delta[:, None] + NA_WIN_ROWS - 1
    bias = rpb.astype(F32)[:, row_idx[:, :, None, None], col_idx[None, None, :, :]]
    bias = jnp.where(col_mask[None, None, None], bias * LOG2E, NEG_INF)
    bias = bias.transpose(0, 1, 2, 4, 3).reshape(NA_HEADS, NA_WIN_ROWS, NA_WIN_ROWS * GRID_W, GRID_W)
    bias = bias.reshape(NA_HEADS // 2, 2, NA_WIN_ROWS, NA_WIN_ROWS * GRID_W, GRID_W)
    return jnp.concatenate([bias[:, 0], bias[:, 1]], axis=-1)


def _routing(top_idx, n):
    nk = n * TOP_K
    n_blocks = nk // EXPERT_BLOCK + N_EXPERTS
    cap = n_blocks * EXPERT_BLOCK
    onehot = (top_idx[:, :, None] == jnp.arange(N_EXPERTS, dtype=jnp.int32)[None, None, :]).astype(jnp.int32).sum(1)
    csum = jnp.cumsum(onehot, axis=0)
    rank = csum - onehot
    counts = csum[-1]
    padded = (counts + EXPERT_BLOCK - 1) // EXPERT_BLOCK * EXPERT_BLOCK
    pad_end = jnp.cumsum(padded)
    pad_start = pad_end - padded
    dest = pad_start[top_idx] + jnp.take_along_axis(rank, top_idx, axis=1)
    tok = jnp.broadcast_to(jnp.arange(n, dtype=jnp.int32)[:, None], (n, TOP_K))
    buf_tok = jnp.full((cap,), n, jnp.int32).at[dest.reshape(-1)].set(tok.reshape(-1))
    block_e = jnp.minimum(jnp.searchsorted(pad_end, jnp.arange(n_blocks, dtype=jnp.int32) * EXPERT_BLOCK,
                                           side='right'), N_EXPERTS - 1).astype(jnp.int32)
    n_active = (pad_end[-1] // EXPERT_BLOCK).astype(jnp.int32).reshape(1)
    return dest, buf_tok, block_e, n_active


def kernel(x_prompt, x_sample, c_prompt, c_sample, w_ada, b_ada, g_attn_norm, w_qkv, na_q_norm, na_k_norm, na_rpb, diff_q_norm, diff_k_norm, lambda_q1, lambda_k1, lambda_q2, lambda_k2, diff_subln, w_o, g_ffn_norm, w_router, b_router, w_gate, b_gate, w_up, b_up, w_down, b_down):
    l = 0
    groups = [(x_prompt, c_prompt), (x_sample, c_sample)]
    nb = [x.shape[0] for x, _ in groups]

    ada_all = _ada(jnp.concatenate([c for _, c in groups], axis=0), w_ada[l], b_ada[l])
    ada_all = ada_all.reshape(sum(nb), 6, D_MODEL)

    w_qkv_bf = w_qkv[l].astype(BF16)
    w_o_bf = w_o[l].astype(BF16)
    scale = HEAD_DIM ** -0.5
    reps = NA_WIDTH // HEAD_DIM
    gains = jnp.stack([jnp.tile(na_q_norm[l], reps) * (scale * LOG2E),
                       jnp.tile(na_k_norm[l], reps),
                       jnp.tile(diff_q_norm[l], reps) * (scale * LOG2E),
                       jnp.tile(diff_k_norm[l], reps)]).astype(F32)
    head_id = jnp.arange(MXU_DIM, dtype=jnp.int32) // HEAD_DIM
    bd = (head_id[:, None] == head_id[None, :]).astype(BF16)
    bias_t = _na_bias_table(na_rpb[l])
    lam_vecs = jnp.stack([lambda_q1[l], lambda_k1[l], lambda_q2[l], lambda_k2[l]]).astype(F32)
    subln_col = (diff_subln[l].astype(F32) * (1.0 - LAMBDA_INIT)).reshape(LANES, 1)
    wr = w_router[l].astype(F32)
    wr_pad = jnp.zeros((D_MODEL, LANES), F32).at[:, :N_EXPERTS].set(wr)
    wr_hi = wr_pad.astype(BF16)
    wr_lo = (wr_pad - wr_hi.astype(F32)).astype(BF16)
    br_pad = jnp.full((1, LANES), NEG_INF, F32).at[0, :N_EXPERTS].set(b_router[l].astype(F32))
    g_attn = g_attn_norm[l].reshape(1, D_MODEL).astype(F32)
    g_ffn = g_ffn_norm[l].reshape(1, D_MODEL).astype(F32)
    max_seq = max(x.shape[1] for x, _ in groups)
    cos_t, sin_t = _rope_tables(max_seq)

    x1s, h2s, tis, tws = [], [], [], []
    b0 = 0
    for (x, _), b in zip(groups, nb):
        seq = x.shape[1]
        x2d = x.reshape(b * seq, D_MODEL)
        ada_g = ada_all[b0:b0 + b]
        b0 += b
        qkv = _qkv(x2d, ada_g, g_attn, w_qkv_bf, gains, cos_t, sin_t, bd, seq)
        o_na = _na(qkv, bias_t, b, seq)
        o_df = _diff(qkv, lam_vecs, subln_col, b, seq)
        x1, h2, ti, tw = _wo(o_na, o_df, x2d, ada_g, w_o_bf, g_ffn, wr_hi, wr_lo, br_pad, seq)
        x1s.append(x1)
        h2s.append(h2)
        tis.append(ti)
        tws.append(tw)

    h2_all = jnp.concatenate(h2s, axis=0)
    ti_all = jnp.concatenate(tis, axis=0)
    tw_all = jnp.concatenate(tws, axis=0)
    n = h2_all.shape[0]
    top_idx = ti_all[:, :TOP_K]
    dest, buf_tok, block_e, n_active = _routing(top_idx, n)

    h2_pad = jnp.concatenate([h2_all, jnp.zeros((1, D_MODEL), BF16)], axis=0)
    xs = h2_pad[buf_tok]
    ys_sorted = _experts(block_e, n_active, xs,
                         w_gate[l].astype(BF16), b_gate[l].reshape(N_EXPERTS, 1, D_MODEL).astype(F32),
                         w_up[l].astype(BF16), b_up[l].reshape(N_EXPERTS, 1, D_MODEL).astype(F32),
                         w_down[l].astype(BF16), b_down[l].reshape(N_EXPERTS, 1, D_MODEL).astype(F32))
    ys = ys_sorted[dest.reshape(-1)].reshape(n, TOP_K * D_MODEL)

    outs = []
    b0 = 0
    row_off = 0
    for (x, _), b, x1 in zip(groups, nb, x1s):
        seq = x.shape[1]
        ada_g = ada_all[b0:b0 + b]
        b0 += b
        y = _combine(x1, ys, tw_all, ada_g, seq, row_off)
        row_off += b * seq
        outs.append(y.reshape(b, seq, D_MODEL))
    return tuple(outs)
```

```python
import functools
import math

import jax
import jax.numpy as jnp
from jax import lax
from jax.experimental import pallas as pl
from jax.experimental.pallas import tpu as pltpu
from jax.experimental.pallas import tpu_sc as plsc

F32 = jnp.float32
BF16 = jnp.bfloat16
U32 = jnp.uint32

D_MODEL = 1024
HEAD_DIM = 64
NA_HEADS = 8
NA_WIDTH = 512
DIFF_HEADS = 4
DIFF_WIDTH = 512
QKV_COLS = 3072
GRID_W = 64
NA_WIN_ROWS = 8
NA_WIN_COLS = 16
ROPE_THETA = 10000.0
N_EXPERTS = 32
TOP_K = 4
SWIGLU_LIMIT = 7.0
SWIGLU_ALPHA = 1.702
EPS = 1e-5
NEG_INF = -1e30
LAMBDA_INIT = 0.8 - 0.6 * math.exp(-0.3 * 0)
LOG2E = 1.4426950408889634

LANES = 128
MXU_DIM = 256
VMEM_LIMIT = 56 * 1024 * 1024

ROW_BLOCK = 512
Q_BLOCK = 256
EXPERT_BLOCK = 512


PACKED = D_MODEL // 2
SC_CORES = 2
SC_SUBCORES = 16
SC_WORKERS = SC_CORES * SC_SUBCORES
GATHER_ROWS = 64


def _params(*sem):
    return pltpu.CompilerParams(dimension_semantics=sem, vmem_limit_bytes=VMEM_LIMIT)


def _pack_halves(x):
    w = x.shape[1] // 2
    bits = lax.bitcast_convert_type(x, U32)
    return (bits[:, :w] >> 16) | bits[:, w:]


def _unpack_halves(word):
    lo = lax.bitcast_convert_type(word << 16, F32)
    hi = lax.bitcast_convert_type(word & jnp.uint32(0xFFFF0000), F32)
    return lo, hi


def _ada_kernel(c_ref, w_ref, b_ref, o_ref):
    c = c_ref[...]
    s = c * jax.nn.sigmoid(c)
    o_ref[...] = jnp.dot(s, w_ref[...], preferred_element_type=F32,
                         precision=lax.Precision.HIGHEST) + b_ref[...]


def _ada(c_all, w_ada, b_ada):
    nb = c_all.shape[0]
    n_out = w_ada.shape[1]
    blk = D_MODEL
    return pl.pallas_call(
        _ada_kernel,
        grid=(n_out // blk,),
        in_specs=[pl.BlockSpec((nb, D_MODEL), lambda j: (0, 0)),
                  pl.BlockSpec((D_MODEL, blk), lambda j: (0, j)),
                  pl.BlockSpec((1, blk), lambda j: (0, j))],
        out_specs=pl.BlockSpec((nb, blk), lambda j: (0, j)),
        out_shape=jax.ShapeDtypeStruct((nb, n_out), F32),
        compiler_params=_params("arbitrary"),
        name="ada",
    )(c_all, w_ada, b_ada.reshape(1, n_out))


def _head_sumsq(y, bd):
    sq = (y * y).astype(BF16)
    parts = [jnp.dot(sq[:, c:c + MXU_DIM], bd, preferred_element_type=F32)
             for c in range(0, y.shape[1], MXU_DIM)]
    return jnp.concatenate(parts, axis=1)


def _qkv_kernel(x_ref, ada_ref, g_ref, w_ref, gain_ref, cos_ref, sin_ref, bd_ref, o_ref):
    x = x_ref[...]
    ms = jnp.mean(x * x, axis=-1, keepdims=True)
    xn = x * lax.rsqrt(ms + EPS) * g_ref[...]
    sh = ada_ref[0, 0:1, :]
    sc = ada_ref[0, 1:2, :]
    h = (xn * (1.0 + sc) + sh).astype(BF16)
    bd = bd_ref[...]
    lane = lax.broadcasted_iota(jnp.int32, (x.shape[0], NA_WIDTH), 1)
    first_half = (lane & (HEAD_DIM // 2)) == 0
    for grp in range(6):
        cols = slice(grp * 512, (grp + 1) * 512)
        acc = jnp.dot(h, w_ref[:, cols], preferred_element_type=F32)
        if grp in (2, 5):
            o_ref[:, cols] = acc.astype(BF16)
            continue
        gi = {0: 0, 1: 1, 3: 2, 4: 3}[grp]
        ss = _head_sumsq(acc, bd)
        y = acc * lax.rsqrt(ss * (1.0 / HEAD_DIM) + EPS) * gain_ref[gi:gi + 1, :]
        if grp in (3, 4):
            partner = jnp.where(first_half,
                                pltpu.roll(y, NA_WIDTH - HEAD_DIM // 2, axis=1),
                                pltpu.roll(y, HEAD_DIM // 2, axis=1))
            y = y * cos_ref[...] + partner * sin_ref[...]
        o_ref[:, cols] = y.astype(BF16)


def _qkv(x2d, ada_g, g_attn, w_qkv_bf, gains, cos_t, sin_t, bd, seq):
    n = x2d.shape[0]
    tm = ROW_BLOCK
    per_seq = seq // tm
    return pl.pallas_call(
        _qkv_kernel,
        grid=(n // tm,),
        in_specs=[pl.BlockSpec((tm, D_MODEL), lambda i: (i, 0)),
                  pl.BlockSpec((1, 6, D_MODEL), lambda i: (i // per_seq, 0, 0)),
                  pl.BlockSpec((1, D_MODEL), lambda i: (0, 0)),
                  pl.BlockSpec((D_MODEL, QKV_COLS), lambda i: (0, 0)),
                  pl.BlockSpec((4, NA_WIDTH), lambda i: (0, 0)),
                  pl.BlockSpec((tm, DIFF_WIDTH), lambda i: (i % per_seq, 0)),
                  pl.BlockSpec((tm, DIFF_WIDTH), lambda i: (i % per_seq, 0)),
                  pl.BlockSpec((MXU_DIM, MXU_DIM), lambda i: (0, 0))],
        out_specs=pl.BlockSpec((tm, QKV_COLS), lambda i: (i, 0)),
        out_shape=jax.ShapeDtypeStruct((n, QKV_COLS), BF16),
        compiler_params=_params("arbitrary"),
        name="qkv",
    )(x2d, ada_g, g_attn, w_qkv_bf, gains, cos_t, sin_t, bd)


def _na_kernel(q_ref, k_ref, v_ref, bias_ref, o_ref, *, rows):
    lane = lax.broadcasted_iota(jnp.int32, (GRID_W, LANES), 1)
    head0 = lane < HEAD_DIM
    win = NA_WIN_ROWS * GRID_W

    def body(r, carry):
        r_start = jnp.clip(r - NA_WIN_ROWS // 2, 0, rows - NA_WIN_ROWS)
        delta = r - r_start
        q = q_ref[pl.ds(pl.multiple_of(r * GRID_W, GRID_W), GRID_W), :]
        k0 = pl.multiple_of(r_start * GRID_W, GRID_W)
        kw = k_ref[pl.ds(k0, win), :]
        vw = v_ref[pl.ds(k0, win), :]
        zero = jnp.zeros_like(q)
        qm = jnp.concatenate([jnp.where(head0, q, zero), jnp.where(head0, zero, q)], axis=0)
        s = lax.dot_general(kw, qm, (((1,), (1,)), ((), ())), preferred_element_type=F32)
        s = s + bias_ref[0, delta]
        m = jnp.max(s, axis=0, keepdims=True)
        p = jnp.exp2(s - m)
        l = jnp.sum(p, axis=0, keepdims=True)
        p = (p * (1.0 / l)).astype(BF16)
        o2 = lax.dot_general(p, vw, (((0,), (0,)), ((), ())), preferred_element_type=F32)
        o = jnp.where(head0, o2[:GRID_W], o2[GRID_W:])
        o_ref[pl.ds(pl.multiple_of(r * GRID_W, GRID_W), GRID_W), :] = o.astype(BF16)
        return carry

    lax.fori_loop(0, rows, body, 0)


def _na(qkv, bias_t, batch, seq):
    rows = seq // GRID_W
    n_pairs = NA_HEADS // 2
    return pl.pallas_call(
        functools.partial(_na_kernel, rows=rows),
        grid=(batch, n_pairs),
        in_specs=[pl.BlockSpec((seq, LANES), lambda b, hp: (b, hp)),
                  pl.BlockSpec((seq, LANES), lambda b, hp: (b, n_pairs + hp)),
                  pl.BlockSpec((seq, LANES), lambda b, hp: (b, 2 * n_pairs + hp)),
                  pl.BlockSpec((1, NA_WIN_ROWS, NA_WIN_ROWS * GRID_W, LANES), lambda b, hp: (hp, 0, 0, 0))],
        out_specs=pl.BlockSpec((seq, LANES), lambda b, hp: (b, hp)),
        out_shape=jax.ShapeDtypeStruct((batch * seq, NA_WIDTH), BF16),
        compiler_params=_params("arbitrary", "arbitrary"),
        name="na_attn",
    )(qkv, qkv, qkv, bias_t)


def _diff_kernel(q_ref, k_ref, v_ref, lam_ref, g_ref, o_ref, vt_ref):
    @pl.when(pl.program_id(2) == 0)
    def _():
        vt_ref[...] = v_ref[...].astype(F32).T.astype(BF16)

    lq1 = lam_ref[0:1, :]
    lk1 = lam_ref[1:2, :]
    lq2 = lam_ref[2:3, :]
    lk2 = lam_ref[3:4, :]
    lam = (jnp.exp(jnp.sum(lq1 * lk1, axis=-1, keepdims=True))
           - jnp.exp(jnp.sum(lq2 * lk2, axis=-1, keepdims=True)) + LAMBDA_INIT)

    q = q_ref[...]
    k = k_ref[...]
    vt = vt_ref[...]
    lane = lax.broadcasted_iota(jnp.int32, q.shape, 1)
    zero = jnp.zeros_like(q)
    outs = []
    for comp in range(2):
        sel = (lane < HEAD_DIM) if comp == 0 else (lane >= HEAD_DIM)
        qc = jnp.where(sel, q, zero)
        s = lax.dot_general(k, qc, (((1,), (1,)), ((), ())), preferred_element_type=F32)
        m = jnp.max(s, axis=0, keepdims=True)
        p = jnp.exp2(s - m)
        l = jnp.sum(p, axis=0, keepdims=True)
        o = jnp.dot(vt, p.astype(BF16), preferred_element_type=F32)
        outs.append(o * (1.0 / l))
    o = outs[0] - lam * outs[1]
    ms = jnp.mean(o * o, axis=0, keepdims=True)
    y = o * lax.rsqrt(ms + EPS) * g_ref[...]
    o_ref[...] = y.T.astype(BF16)


def _diff(qkv, lam_vecs, subln_col, batch, seq):
    tq = Q_BLOCK
    nq = seq // tq
    base = 3 * NA_WIDTH // LANES
    nh = DIFF_HEADS
    return pl.pallas_call(
        _diff_kernel,
        grid=(batch, nh, nq),
        in_specs=[pl.BlockSpec((tq, LANES), lambda b, h, i: (b * nq + i, base + h)),
                  pl.BlockSpec((seq, LANES), lambda b, h, i: (b, base + nh + h)),
                  pl.BlockSpec((seq, LANES), lambda b, h, i: (b, base + 2 * nh + h)),
                  pl.BlockSpec((4, HEAD_DIM), lambda b, h, i: (0, 0)),
                  pl.BlockSpec((LANES, 1), lambda b, h, i: (0, 0))],
        out_specs=pl.BlockSpec((tq, LANES), lambda b, h, i: (b * nq + i, h)),
        out_shape=jax.ShapeDtypeStruct((batch * seq, DIFF_WIDTH), BF16),
        scratch_shapes=[pltpu.VMEM((LANES, seq), BF16)],
        compiler_params=_params("arbitrary", "arbitrary", "arbitrary"),
        name="diff_attn",
    )(qkv, qkv, qkv, lam_vecs, subln_col)


def _wo_kernel(ona_ref, odf_ref, x_ref, ada_ref, wo_ref, g_ref, wrh_ref, wrl_ref, br_ref,
               x1_ref, h2_ref, ti_ref, tw_ref):
    mix = (jnp.dot(ona_ref[...], wo_ref[:NA_WIDTH, :], preferred_element_type=F32)
           + jnp.dot(odf_ref[...], wo_ref[NA_WIDTH:, :], preferred_element_type=F32))
    gt1 = ada_ref[0, 2:3, :]
    sh2 = ada_ref[0, 3:4, :]
    sc2 = ada_ref[0, 4:5, :]
    x1 = x_ref[...] + gt1 * mix
    x1_ref[...] = x1
    ms = jnp.mean(x1 * x1, axis=-1, keepdims=True)
    h2 = x1 * lax.rsqrt(ms + EPS) * g_ref[...] * (1.0 + sc2) + sh2
    hi = h2.astype(BF16)
    h2_ref[...] = _pack_halves(hi.astype(F32))
    lo = (h2 - hi.astype(F32)).astype(BF16)
    logits = (jnp.dot(hi, wrh_ref[...], preferred_element_type=F32)
              + jnp.dot(hi, wrl_ref[...], preferred_element_type=F32)
              + jnp.dot(lo, wrh_ref[...], preferred_element_type=F32)) + br_ref[...]
    lane = lax.broadcasted_iota(jnp.int32, logits.shape, 1).astype(F32)
    vals = []
    idxs = []
    cur = logits
    for _ in range(TOP_K):
        m = jnp.max(cur, axis=-1, keepdims=True)
        idx = jnp.min(jnp.where(cur == m, lane, float(LANES)), axis=-1, keepdims=True)
        vals.append(m)
        idxs.append(idx)
        cur = jnp.where(lane == idx, -jnp.inf, cur)
    es = [jnp.exp(v - vals[0]) for v in vals]
    inv = 1.0 / (es[0] + es[1] + es[2] + es[3])
    ti = jnp.zeros(logits.shape, F32)
    tw = jnp.zeros(logits.shape, F32)
    for j in range(TOP_K):
        ti = jnp.where(lane == float(j), idxs[j], ti)
        tw = jnp.where(lane == float(j), es[j] * inv, tw)
    ti_ref[...] = ti.astype(jnp.int32)
    tw_ref[...] = tw


def _wo(o_na, o_df, x2d, ada_g, w_o_bf, g_ffn, wr_hi, wr_lo, br_pad, seq):
    n = x2d.shape[0]
    tm = ROW_BLOCK
    per_seq = seq // tm
    row = lambda i: (i, 0)
    const = lambda i: (0, 0)
    return pl.pallas_call(
        _wo_kernel,
        grid=(n // tm,),
        in_specs=[pl.BlockSpec((tm, NA_WIDTH), row),
                  pl.BlockSpec((tm, DIFF_WIDTH), row),
                  pl.BlockSpec((tm, D_MODEL), row),
                  pl.BlockSpec((1, 6, D_MODEL), lambda i: (i // per_seq, 0, 0)),
                  pl.BlockSpec((D_MODEL, D_MODEL), const),
                  pl.BlockSpec((1, D_MODEL), const),
                  pl.BlockSpec((D_MODEL, LANES), const),
                  pl.BlockSpec((D_MODEL, LANES), const),
                  pl.BlockSpec((1, LANES), const)],
        out_specs=[pl.BlockSpec((tm, D_MODEL), row),
                   pl.BlockSpec((tm, PACKED), row),
                   pl.BlockSpec((tm, LANES), row),
                   pl.BlockSpec((tm, LANES), row)],
        out_shape=[jax.ShapeDtypeStruct((n, D_MODEL), F32),
                   jax.ShapeDtypeStruct((n, PACKED), U32),
                   jax.ShapeDtypeStruct((n, LANES), jnp.int32),
                   jax.ShapeDtypeStruct((n, LANES), F32)],
        compiler_params=_params("arbitrary"),
        name="wo_router",
    )(o_na, o_df, x2d, ada_g, w_o_bf, g_ffn, wr_hi, wr_lo, br_pad)


def _expert_kernel(be_ref, na_ref, xs_ref, wg_ref, bg_ref, wu_ref, bu_ref, wd_ref, bd_ref, o_ref):
    i = pl.program_id(0)

    @pl.when(i < na_ref[0])
    def _():
        x_lo, x_hi = _unpack_halves(xs_ref[...])
        x_lo = x_lo.astype(BF16)
        x_hi = x_hi.astype(BF16)

        def proj(w_ref):
            return (jnp.dot(x_lo, w_ref[0, :PACKED, :], preferred_element_type=F32)
                    + jnp.dot(x_hi, w_ref[0, PACKED:, :], preferred_element_type=F32))

        g = jnp.minimum(proj(wg_ref) + bg_ref[0], SWIGLU_LIMIT)
        u = jnp.clip(proj(wu_ref) + bu_ref[0], -SWIGLU_LIMIT, SWIGLU_LIMIT)
        act = g * jax.nn.sigmoid(SWIGLU_ALPHA * g) * (u + 1.0)
        out = jnp.dot(act.astype(BF16), wd_ref[0], preferred_element_type=F32) + bd_ref[0]
        o_ref[...] = _pack_halves(out.astype(BF16).astype(F32))

    @pl.when(i >= na_ref[0])
    def _():
        o_ref[...] = jnp.zeros_like(o_ref)


def _experts(block_e, n_active, xs, wg, bg, wu, bu, wd, bd):
    cap = xs.shape[0]
    n_blocks = cap // EXPERT_BLOCK
    xmap = lambda i, be, na: (jnp.minimum(i, na[0] - 1), 0)
    wmap = lambda i, be, na: (be[i], 0, 0)
    grid_spec = pltpu.PrefetchScalarGridSpec(
        num_scalar_prefetch=2,
        grid=(n_blocks,),
        in_specs=[pl.BlockSpec((EXPERT_BLOCK, PACKED), xmap),
                  pl.BlockSpec((1, D_MODEL, D_MODEL), wmap),
                  pl.BlockSpec((1, 1, D_MODEL), wmap),
                  pl.BlockSpec((1, D_MODEL, D_MODEL), wmap),
                  pl.BlockSpec((1, 1, D_MODEL), wmap),
                  pl.BlockSpec((1, D_MODEL, D_MODEL), wmap),
                  pl.BlockSpec((1, 1, D_MODEL), wmap)],
        out_specs=pl.BlockSpec((EXPERT_BLOCK, PACKED), lambda i, be, na: (i, 0)),
    )
    return pl.pallas_call(
        _expert_kernel,
        grid_spec=grid_spec,
        out_shape=jax.ShapeDtypeStruct((cap, PACKED), U32),
        compiler_params=_params("arbitrary"),
        name="experts",
    )(block_e, n_active, xs, wg, bg, wu, bu, wd, bd)


def _sc_gather(table, idx):
    n_out = idx.shape[0]
    width = table.shape[1]
    per_worker = n_out // SC_WORKERS
    n_chunks = per_worker // GATHER_ROWS
    assert per_worker * SC_WORKERS == n_out and n_chunks * GATHER_ROWS == per_worker and n_chunks % 2 == 0
    idx3 = idx.reshape(SC_WORKERS, n_chunks, GATHER_ROWS)
    mesh = plsc.VectorSubcoreMesh(core_axis_name="core", subcore_axis_name="subcore")

    @functools.partial(
        pl.kernel, mesh=mesh,
        out_type=jax.ShapeDtypeStruct((n_out, width), table.dtype),
        scratch_types=[pltpu.VMEM((n_chunks, GATHER_ROWS), jnp.int32),
                       pltpu.VMEM((2, GATHER_ROWS, width), table.dtype),
                       pltpu.SemaphoreType.DMA((2,)),
                       pltpu.SemaphoreType.DMA((2,))])
    def gather_kernel(table_hbm, idx_hbm, out_hbm, idx_v, rows_v, gsem, wsem):
        wid = lax.axis_index("subcore") * SC_CORES + lax.axis_index("core")
        base = wid * per_worker
        pltpu.sync_copy(idx_hbm.at[wid], idx_v)

        def gather(j, slot):
            return pltpu.make_async_copy(table_hbm.at[idx_v.at[j]], rows_v.at[slot], gsem.at[slot])

        def write(j, slot):
            dst = out_hbm.at[pl.ds(pl.multiple_of(base + j * GATHER_ROWS, GATHER_ROWS), GATHER_ROWS)]
            return pltpu.make_async_copy(rows_v.at[slot], dst, wsem.at[slot])

        gather(0, 0).start()

        @pl.loop(0, n_chunks, step=2)
        def _(j):
            for slot in range(2):
                jj = j + slot
                gather(jj, slot).wait()

                @pl.when(jj >= 1)
                def _():
                    write(jj - 1, 1 - slot).wait()

                @pl.when(jj + 1 < n_chunks)
                def _():
                    gather(jj + 1, 1 - slot).start()

                write(jj, slot).start()

        write(n_chunks - 1, 1).wait()

    return gather_kernel(table, idx3)


def _combine_kernel(x1_ref, y0_ref, y1_ref, y2_ref, y3_ref, tw_ref, ada_ref, o_ref):
    tw = tw_ref[...]
    acc_lo = jnp.zeros((x1_ref.shape[0], PACKED), F32)
    acc_hi = jnp.zeros((x1_ref.shape[0], PACKED), F32)
    for j, y_ref in enumerate((y0_ref, y1_ref, y2_ref, y3_ref)):
        lo, hi = _unpack_halves(y_ref[...])
        acc_lo = acc_lo + tw[:, j:j + 1] * lo
        acc_hi = acc_hi + tw[:, j:j + 1] * hi
    o_ref[:, :PACKED] = x1_ref[:, :PACKED] + ada_ref[0, 5:6, :PACKED] * acc_lo
    o_ref[:, PACKED:] = x1_ref[:, PACKED:] + ada_ref[0, 5:6, PACKED:] * acc_hi


def _combine(x1, ys, tw, ada_g, seq, row_off):
    n = x1.shape[0]
    tm = ROW_BLOCK
    per_seq = seq // tm
    off = row_off // tm
    per_choice = tw.shape[0] // tm
    y_specs = [pl.BlockSpec((tm, PACKED), functools.partial(lambda i, j: (j * per_choice + off + i, 0), j=j))
               for j in range(TOP_K)]
    return pl.pallas_call(
        _combine_kernel,
        grid=(n // tm,),
        in_specs=[pl.BlockSpec((tm, D_MODEL), lambda i: (i, 0)),
                  *y_specs,
                  pl.BlockSpec((tm, LANES), lambda i: (i + off, 0)),
                  pl.BlockSpec((1, 6, D_MODEL), lambda i: (i // per_seq, 0, 0))],
        out_specs=pl.BlockSpec((tm, D_MODEL), lambda i: (i, 0)),
        out_shape=jax.ShapeDtypeStruct((n, D_MODEL), F32),
        compiler_params=_params("arbitrary"),
        name="combine",
    )(x1, ys, ys, ys, ys, tw, ada_g)


def _rope_tables(seq):
    half = HEAD_DIM // 2
    inv = ROPE_THETA ** (-jnp.arange(half, dtype=F32) / half)
    ang = jnp.arange(seq, dtype=F32)[:, None] * inv[None, :]
    cos, sin = jnp.cos(ang), jnp.sin(ang)
    cos_h = jnp.concatenate([cos, cos], axis=-1)
    sin_h = jnp.concatenate([-sin, sin], axis=-1)
    reps = DIFF_WIDTH // HEAD_DIM
    return jnp.tile(cos_h, (1, reps)), jnp.tile(sin_h, (1, reps))


def _na_bias_table(rpb):
    cols = jnp.arange(GRID_W, dtype=jnp.int32)
    c_start = jnp.clip(cols - NA_WIN_COLS // 2, 0, GRID_W - NA_WIN_COLS)
    col_mask = (cols[None, :] >= c_start[:, None]) & (cols[None, :] < c_start[:, None] + NA_WIN_COLS)
    col_idx = jnp.clip(cols[None, :] - cols[:, None], -(NA_WIN_COLS - 1), NA_WIN_COLS - 1) + NA_WIN_COLS - 1
    delta = jnp.arange(NA_WIN_ROWS, dtype=jnp.int32)
    j = jnp.arange(NA_WIN_ROWS, dtype=jnp.int32)
    row_idx = j[None, :] - delta[:, None] + NA_WIN_ROWS - 1
    bias = rpb.astype(F32)[:, row_idx[:, :, None, None], col_idx[None, None, :, :]]
    bias = jnp.where(col_mask[None, None, None], bias * LOG2E, NEG_INF)
    bias = bias.transpose(0, 1, 2, 4, 3).reshape(NA_HEADS, NA_WIN_ROWS, NA_WIN_ROWS * GRID_W, GRID_W)
    bias = bias.reshape(NA_HEADS // 2, 2, NA_WIN_ROWS, NA_WIN_ROWS * GRID_W, GRID_W)
    return jnp.concatenate([bias[:, 0], bias[:, 1]], axis=-1)


def _routing(top_idx, n):
    nk = n * TOP_K
    n_blocks = nk // EXPERT_BLOCK + N_EXPERTS
    cap = n_blocks * EXPERT_BLOCK
    onehot = (top_idx[:, :, None] == jnp.arange(N_EXPERTS, dtype=jnp.int32)[None, None, :]).astype(jnp.int32).sum(1)
    csum = jnp.cumsum(onehot, axis=0)
    rank = csum - onehot
    counts = csum[-1]
    padded = (counts + EXPERT_BLOCK - 1) // EXPERT_BLOCK * EXPERT_BLOCK
    pad_end = jnp.cumsum(padded)
    pad_start = pad_end - padded
    dest = pad_start[top_idx] + jnp.take_along_axis(rank, top_idx, axis=1)
    tok = jnp.broadcast_to(jnp.arange(n, dtype=jnp.int32)[:, None], (n, TOP_K))
    buf_tok = (jnp.arange(cap, dtype=jnp.int32) % n).at[dest.reshape(-1)].set(tok.reshape(-1))
    block_e = jnp.minimum(jnp.searchsorted(pad_end, jnp.arange(n_blocks, dtype=jnp.int32) * EXPERT_BLOCK,
                                           side='right'), N_EXPERTS - 1).astype(jnp.int32)
    n_active = (pad_end[-1] // EXPERT_BLOCK).astype(jnp.int32).reshape(1)
    return dest, buf_tok, block_e, n_active


def kernel(x_prompt, x_sample, c_prompt, c_sample, w_ada, b_ada, g_attn_norm, w_qkv, na_q_norm, na_k_norm, na_rpb, diff_q_norm, diff_k_norm, lambda_q1, lambda_k1, lambda_q2, lambda_k2, diff_subln, w_o, g_ffn_norm, w_router, b_router, w_gate, b_gate, w_up, b_up, w_down, b_down):
    l = 0
    groups = [(x_prompt, c_prompt), (x_sample, c_sample)]
    nb = [x.shape[0] for x, _ in groups]

    ada_all = _ada(jnp.concatenate([c for _, c in groups], axis=0), w_ada[l], b_ada[l])
    ada_all = ada_all.reshape(sum(nb), 6, D_MODEL)

    w_qkv_bf = w_qkv[l].astype(BF16)
    w_o_bf = w_o[l].astype(BF16)
    scale = HEAD_DIM ** -0.5
    reps = NA_WIDTH // HEAD_DIM
    gains = jnp.stack([jnp.tile(na_q_norm[l], reps) * (scale * LOG2E),
                       jnp.tile(na_k_norm[l], reps),
                       jnp.tile(diff_q_norm[l], reps) * (scale * LOG2E),
                       jnp.tile(diff_k_norm[l], reps)]).astype(F32)
    head_id = jnp.arange(MXU_DIM, dtype=jnp.int32) // HEAD_DIM
    bd = (head_id[:, None] == head_id[None, :]).astype(BF16)
    bias_t = _na_bias_table(na_rpb[l])
    lam_vecs = jnp.stack([lambda_q1[l], lambda_k1[l], lambda_q2[l], lambda_k2[l]]).astype(F32)
    subln_col = (diff_subln[l].astype(F32) * (1.0 - LAMBDA_INIT)).reshape(LANES, 1)
    wr = w_router[l].astype(F32)
    wr_pad = jnp.zeros((D_MODEL, LANES), F32).at[:, :N_EXPERTS].set(wr)
    wr_hi = wr_pad.astype(BF16)
    wr_lo = (wr_pad - wr_hi.astype(F32)).astype(BF16)
    br_pad = jnp.full((1, LANES), NEG_INF, F32).at[0, :N_EXPERTS].set(b_router[l].astype(F32))
    g_attn = g_attn_norm[l].reshape(1, D_MODEL).astype(F32)
    g_ffn = g_ffn_norm[l].reshape(1, D_MODEL).astype(F32)
    max_seq = max(x.shape[1] for x, _ in groups)
    cos_t, sin_t = _rope_tables(max_seq)

    x1s, h2s, tis, tws = [], [], [], []
    b0 = 0
    for (x, _), b in zip(groups, nb):
        seq = x.shape[1]
        x2d = x.reshape(b * seq, D_MODEL)
        ada_g = ada_all[b0:b0 + b]
        b0 += b
        qkv = _qkv(x2d, ada_g, g_attn, w_qkv_bf, gains, cos_t, sin_t, bd, seq)
        o_na = _na(qkv, bias_t, b, seq)
        o_df = _diff(qkv, lam_vecs, subln_col, b, seq)
        x1, h2, ti, tw = _wo(o_na, o_df, x2d, ada_g, w_o_bf, g_ffn, wr_hi, wr_lo, br_pad, seq)
        x1s.append(x1)
        h2s.append(h2)
        tis.append(ti)
        tws.append(tw)

    h2_all = jnp.concatenate(h2s, axis=0)
    ti_all = jnp.concatenate(tis, axis=0)
    tw_all = jnp.concatenate(tws, axis=0)
    n = h2_all.shape[0]
    top_idx = ti_all[:, :TOP_K]
    dest, buf_tok, block_e, n_active = _routing(top_idx, n)

    xs = _sc_gather(h2_all, buf_tok)
    ys_sorted = _experts(block_e, n_active, xs,
                         w_gate[l].astype(BF16), b_gate[l].reshape(N_EXPERTS, 1, D_MODEL).astype(F32),
                         w_up[l].astype(BF16), b_up[l].reshape(N_EXPERTS, 1, D_MODEL).astype(F32),
                         w_down[l].astype(BF16), b_down[l].reshape(N_EXPERTS, 1, D_MODEL).astype(F32))
    ys = _sc_gather(ys_sorted, dest.T.reshape(-1))

    outs = []
    b0 = 0
    row_off = 0
    for (x, _), b, x1 in zip(groups, nb, x1s):
        seq = x.shape[1]
        ada_g = ada_all[b0:b0 + b]
        b0 += b
        y = _combine(x1, ys, tw_all, ada_g, seq, row_off)
        row_off += b * seq
        outs.append(y.reshape(b, seq, D_MODEL))
    return tuple(outs)
```

```python
import functools
import math

import jax
import jax.numpy as jnp
from jax import lax
from jax.experimental import pallas as pl
from jax.experimental.pallas import tpu as pltpu
from jax.experimental.pallas import tpu_sc as plsc

F32 = jnp.float32
BF16 = jnp.bfloat16
U32 = jnp.uint32

D_MODEL = 1024
HEAD_DIM = 64
NA_HEADS = 8
NA_WIDTH = 512
DIFF_HEADS = 4
DIFF_WIDTH = 512
QKV_COLS = 3072
GRID_W = 64
NA_WIN_ROWS = 8
NA_WIN_COLS = 16
ROPE_THETA = 10000.0
N_EXPERTS = 32
TOP_K = 4
SWIGLU_LIMIT = 7.0
SWIGLU_ALPHA = 1.702
EPS = 1e-5
NEG_INF = -1e30
LAMBDA_INIT = 0.8 - 0.6 * math.exp(-0.3 * 0)
LOG2E = 1.4426950408889634

LANES = 128
MXU_DIM = 256
VMEM_LIMIT = 56 * 1024 * 1024

ROW_BLOCK = 512
Q_BLOCK = 256
EXPERT_BLOCK = 512


PACKED = D_MODEL // 2
SC_CORES = 2
SC_SUBCORES = 16
SC_WORKERS = SC_CORES * SC_SUBCORES
GATHER_ROWS = 64


def _params(*sem):
    return pltpu.CompilerParams(dimension_semantics=sem, vmem_limit_bytes=VMEM_LIMIT)


def _pack_halves(x):
    w = x.shape[1] // 2
    bits = lax.bitcast_convert_type(x, U32)
    return (bits[:, :w] >> 16) | bits[:, w:]


def _unpack_halves(word):
    lo = lax.bitcast_convert_type(word << 16, F32)
    hi = lax.bitcast_convert_type(word & jnp.uint32(0xFFFF0000), F32)
    return lo, hi


def _ada_kernel(c_ref, w_ref, b_ref, o_ref):
    c = c_ref[...]
    s = c * jax.nn.sigmoid(c)
    o_ref[...] = jnp.dot(s, w_ref[...], preferred_element_type=F32,
                         precision=lax.Precision.HIGHEST) + b_ref[...]


def _ada(c_all, w_ada, b_ada):
    nb = c_all.shape[0]
    n_out = w_ada.shape[1]
    blk = D_MODEL
    return pl.pallas_call(
        _ada_kernel,
        grid=(n_out // blk,),
        in_specs=[pl.BlockSpec((nb, D_MODEL), lambda j: (0, 0)),
                  pl.BlockSpec((D_MODEL, blk), lambda j: (0, j)),
                  pl.BlockSpec((1, blk), lambda j: (0, j))],
        out_specs=pl.BlockSpec((nb, blk), lambda j: (0, j)),
        out_shape=jax.ShapeDtypeStruct((nb, n_out), F32),
        compiler_params=_params("arbitrary"),
        name="ada",
    )(c_all, w_ada, b_ada.reshape(1, n_out))


def _head_sumsq(y, bd):
    sq = (y * y).astype(BF16)
    parts = [jnp.dot(sq[:, c:c + MXU_DIM], bd, preferred_element_type=F32)
             for c in range(0, y.shape[1], MXU_DIM)]
    return jnp.concatenate(parts, axis=1)


def _qkv_kernel(x_ref, ada_ref, g_ref, w_ref, gain_ref, cos_ref, sin_ref, bd_ref, o_ref):
    x = x_ref[...]
    ms = jnp.mean(x * x, axis=-1, keepdims=True)
    xn = x * lax.rsqrt(ms + EPS) * g_ref[...]
    sh = ada_ref[0, 0:1, :]
    sc = ada_ref[0, 1:2, :]
    h = (xn * (1.0 + sc) + sh).astype(BF16)
    bd = bd_ref[...]
    lane = lax.broadcasted_iota(jnp.int32, (x.shape[0], NA_WIDTH), 1)
    first_half = (lane & (HEAD_DIM // 2)) == 0
    for grp in range(6):
        cols = slice(grp * 512, (grp + 1) * 512)
        acc = jnp.dot(h, w_ref[:, cols], preferred_element_type=F32)
        if grp in (2, 5):
            o_ref[:, cols] = acc.astype(BF16)
            continue
        gi = {0: 0, 1: 1, 3: 2, 4: 3}[grp]
        ss = _head_sumsq(acc, bd)
        y = acc * lax.rsqrt(ss * (1.0 / HEAD_DIM) + EPS) * gain_ref[gi:gi + 1, :]
        if grp in (3, 4):
            partner = jnp.where(first_half,
                                pltpu.roll(y, NA_WIDTH - HEAD_DIM // 2, axis=1),
                                pltpu.roll(y, HEAD_DIM // 2, axis=1))
            y = y * cos_ref[...] + partner * sin_ref[...]
        o_ref[:, cols] = y.astype(BF16)


def _qkv(x2d, ada_g, g_attn, w_qkv_bf, gains, cos_t, sin_t, bd, seq):
    n = x2d.shape[0]
    tm = ROW_BLOCK
    per_seq = seq // tm
    return pl.pallas_call(
        _qkv_kernel,
        grid=(n // tm,),
        in_specs=[pl.BlockSpec((tm, D_MODEL), lambda i: (i, 0)),
                  pl.BlockSpec((1, 6, D_MODEL), lambda i: (i // per_seq, 0, 0)),
                  pl.BlockSpec((1, D_MODEL), lambda i: (0, 0)),
                  pl.BlockSpec((D_MODEL, QKV_COLS), lambda i: (0, 0)),
                  pl.BlockSpec((4, NA_WIDTH), lambda i: (0, 0)),
                  pl.BlockSpec((tm, DIFF_WIDTH), lambda i: (i % per_seq, 0)),
                  pl.BlockSpec((tm, DIFF_WIDTH), lambda i: (i % per_seq, 0)),
                  pl.BlockSpec((MXU_DIM, MXU_DIM), lambda i: (0, 0))],
        out_specs=pl.BlockSpec((tm, QKV_COLS), lambda i: (i, 0)),
        out_shape=jax.ShapeDtypeStruct((n, QKV_COLS), BF16),
        compiler_params=_params("arbitrary"),
        name="qkv",
    )(x2d, ada_g, g_attn, w_qkv_bf, gains, cos_t, sin_t, bd)


def _na_kernel(q_ref, k_ref, v_ref, bias_ref, o_ref, *, rows):
    lane = lax.broadcasted_iota(jnp.int32, (GRID_W, LANES), 1)
    head0 = lane < HEAD_DIM
    win = NA_WIN_ROWS * GRID_W

    def body(r, carry):
        r_start = jnp.clip(r - NA_WIN_ROWS // 2, 0, rows - NA_WIN_ROWS)
        delta = r - r_start
        q = q_ref[pl.ds(pl.multiple_of(r * GRID_W, GRID_W), GRID_W), :]
        k0 = pl.multiple_of(r_start * GRID_W, GRID_W)
        kw = k_ref[pl.ds(k0, win), :]
        vw = v_ref[pl.ds(k0, win), :]
        zero = jnp.zeros_like(q)
        qm = jnp.concatenate([jnp.where(head0, q, zero), jnp.where(head0, zero, q)], axis=0)
        s = lax.dot_general(kw, qm, (((1,), (1,)), ((), ())), preferred_element_type=F32)
        s = s + bias_ref[0, delta]
        m = jnp.max(s, axis=0, keepdims=True)
        p = jnp.exp2(s - m)
        l = jnp.sum(p, axis=0, keepdims=True)
        p = (p * (1.0 / l)).astype(BF16)
        o2 = lax.dot_general(p, vw, (((0,), (0,)), ((), ())), preferred_element_type=F32)
        o = jnp.where(head0, o2[:GRID_W], o2[GRID_W:])
        o_ref[pl.ds(pl.multiple_of(r * GRID_W, GRID_W), GRID_W), :] = o.astype(BF16)
        return carry

    lax.fori_loop(0, rows, body, 0)


def _na(qkv, bias_t, batch, seq):
    rows = seq // GRID_W
    n_pairs = NA_HEADS // 2
    return pl.pallas_call(
        functools.partial(_na_kernel, rows=rows),
        grid=(batch, n_pairs),
        in_specs=[pl.BlockSpec((seq, LANES), lambda b, hp: (b, hp)),
                  pl.BlockSpec((seq, LANES), lambda b, hp: (b, n_pairs + hp)),
                  pl.BlockSpec((seq, LANES), lambda b, hp: (b, 2 * n_pairs + hp)),
                  pl.BlockSpec((1, NA_WIN_ROWS, NA_WIN_ROWS * GRID_W, LANES), lambda b, hp: (hp, 0, 0, 0))],
        out_specs=pl.BlockSpec((seq, LANES), lambda b, hp: (b, hp)),
        out_shape=jax.ShapeDtypeStruct((batch * seq, NA_WIDTH), BF16),
        compiler_params=_params("arbitrary", "arbitrary"),
        name="na_attn",
    )(qkv, qkv, qkv, bias_t)


def _diff_kernel(q_ref, k_ref, v_ref, lam_ref, g_ref, o_ref, vt_ref):
    @pl.when(pl.program_id(2) == 0)
    def _():
        vt_ref[...] = v_ref[...].astype(F32).T.astype(BF16)

    lq1 = lam_ref[0:1, :]
    lk1 = lam_ref[1:2, :]
    lq2 = lam_ref[2:3, :]
    lk2 = lam_ref[3:4, :]
    lam = (jnp.exp(jnp.sum(lq1 * lk1, axis=-1, keepdims=True))
           - jnp.exp(jnp.sum(lq2 * lk2, axis=-1, keepdims=True)) + LAMBDA_INIT)

    q = q_ref[...]
    k = k_ref[...]
    vt = vt_ref[...]
    lane = lax.broadcasted_iota(jnp.int32, q.shape, 1)
    zero = jnp.zeros_like(q)
    outs = []
    for comp in range(2):
        sel = (lane < HEAD_DIM) if comp == 0 else (lane >= HEAD_DIM)
        qc = jnp.where(sel, q, zero)
        s = lax.dot_general(k, qc, (((1,), (1,)), ((), ())), preferred_element_type=F32)
        m = jnp.max(s, axis=0, keepdims=True)
        p = jnp.exp2(s - m)
        l = jnp.sum(p, axis=0, keepdims=True)
        o = jnp.dot(vt, p.astype(BF16), preferred_element_type=F32)
        outs.append(o * (1.0 / l))
    o = outs[0] - lam * outs[1]
    ms = jnp.mean(o * o, axis=0, keepdims=True)
    y = o * lax.rsqrt(ms + EPS) * g_ref[...]
    o_ref[...] = y.T.astype(BF16)


def _diff(qkv, lam_vecs, subln_col, batch, seq):
    tq = Q_BLOCK
    nq = seq // tq
    base = 3 * NA_WIDTH // LANES
    nh = DIFF_HEADS
    return pl.pallas_call(
        _diff_kernel,
        grid=(batch, nh, nq),
        in_specs=[pl.BlockSpec((tq, LANES), lambda b, h, i: (b * nq + i, base + h)),
                  pl.BlockSpec((seq, LANES), lambda b, h, i: (b, base + nh + h)),
                  pl.BlockSpec((seq, LANES), lambda b, h, i: (b, base + 2 * nh + h)),
                  pl.BlockSpec((4, HEAD_DIM), lambda b, h, i: (0, 0)),
                  pl.BlockSpec((LANES, 1), lambda b, h, i: (0, 0))],
        out_specs=pl.BlockSpec((tq, LANES), lambda b, h, i: (b * nq + i, h)),
        out_shape=jax.ShapeDtypeStruct((batch * seq, DIFF_WIDTH), BF16),
        scratch_shapes=[pltpu.VMEM((LANES, seq), BF16)],
        compiler_params=_params("arbitrary", "arbitrary", "arbitrary"),
        name="diff_attn",
    )(qkv, qkv, qkv, lam_vecs, subln_col)


def _wo_kernel(ona_ref, odf_ref, x_ref, ada_ref, wo_ref, g_ref, wrh_ref, wrl_ref, br_ref, tri_ref, cnt0_ref,
               x1_ref, h2_ref, ti_ref, tw_ref, cnt_ref):
    mix = (jnp.dot(ona_ref[...], wo_ref[:NA_WIDTH, :], preferred_element_type=F32)
           + jnp.dot(odf_ref[...], wo_ref[NA_WIDTH:, :], preferred_element_type=F32))
    gt1 = ada_ref[0, 2:3, :]
    sh2 = ada_ref[0, 3:4, :]
    sc2 = ada_ref[0, 4:5, :]
    x1 = x_ref[...] + gt1 * mix
    x1_ref[...] = x1
    ms = jnp.mean(x1 * x1, axis=-1, keepdims=True)
    h2 = x1 * lax.rsqrt(ms + EPS) * g_ref[...] * (1.0 + sc2) + sh2
    hi = h2.astype(BF16)
    h2_ref[...] = _pack_halves(hi.astype(F32))
    lo = (h2 - hi.astype(F32)).astype(BF16)
    logits = (jnp.dot(hi, wrh_ref[...], preferred_element_type=F32)
              + jnp.dot(hi, wrl_ref[...], preferred_element_type=F32)
              + jnp.dot(lo, wrh_ref[...], preferred_element_type=F32)) + br_ref[...]
    lane = lax.broadcasted_iota(jnp.int32, logits.shape, 1).astype(F32)
    vals = []
    idxs = []
    cur = logits
    for _ in range(TOP_K):
        m = jnp.max(cur, axis=-1, keepdims=True)
        idx = jnp.min(jnp.where(cur == m, lane, float(LANES)), axis=-1, keepdims=True)
        vals.append(m)
        idxs.append(idx)
        cur = jnp.where(lane == idx, -jnp.inf, cur)
    es = [jnp.exp(v - vals[0]) for v in vals]
    inv = 1.0 / (es[0] + es[1] + es[2] + es[3])

    @pl.when(pl.program_id(0) == 0)
    def _():
        cnt_ref[...] = cnt0_ref[...]

    sel = jnp.zeros(logits.shape, F32)
    for j in range(TOP_K):
        sel = sel + jnp.where(lane == idxs[j], 1.0, 0.0)
    before = jnp.dot(tri_ref[...], sel.astype(BF16), preferred_element_type=F32) + cnt_ref[...]
    cnt_ref[...] = cnt_ref[...] + jnp.sum(sel, axis=0, keepdims=True)
    ranks = [jnp.sum(jnp.where(lane == idxs[j], before, 0.0), axis=-1, keepdims=True) for j in range(TOP_K)]

    ti = jnp.zeros(logits.shape, F32)
    tw = jnp.zeros(logits.shape, F32)
    for j in range(TOP_K):
        ti = jnp.where(lane == float(j), idxs[j], ti)
        ti = jnp.where(lane == float(TOP_K + j), ranks[j], ti)
        tw = jnp.where(lane == float(j), es[j] * inv, tw)
    ti_ref[...] = ti.astype(jnp.int32)
    tw_ref[...] = tw


def _wo(o_na, o_df, x2d, ada_g, w_o_bf, g_ffn, wr_hi, wr_lo, br_pad, tri, cnt0, seq):
    n = x2d.shape[0]
    tm = ROW_BLOCK
    per_seq = seq // tm
    row = lambda i: (i, 0)
    const = lambda i: (0, 0)
    return pl.pallas_call(
        _wo_kernel,
        grid=(n // tm,),
        in_specs=[pl.BlockSpec((tm, NA_WIDTH), row),
                  pl.BlockSpec((tm, DIFF_WIDTH), row),
                  pl.BlockSpec((tm, D_MODEL), row),
                  pl.BlockSpec((1, 6, D_MODEL), lambda i: (i // per_seq, 0, 0)),
                  pl.BlockSpec((D_MODEL, D_MODEL), const),
                  pl.BlockSpec((1, D_MODEL), const),
                  pl.BlockSpec((D_MODEL, LANES), const),
                  pl.BlockSpec((D_MODEL, LANES), const),
                  pl.BlockSpec((1, LANES), const),
                  pl.BlockSpec((tm, tm), const),
                  pl.BlockSpec((1, LANES), const)],
        out_specs=[pl.BlockSpec((tm, D_MODEL), row),
                   pl.BlockSpec((tm, PACKED), row),
                   pl.BlockSpec((tm, LANES), row),
                   pl.BlockSpec((tm, LANES), row),
                   pl.BlockSpec((1, LANES), const)],
        out_shape=[jax.ShapeDtypeStruct((n, D_MODEL), F32),
                   jax.ShapeDtypeStruct((n, PACKED), U32),
                   jax.ShapeDtypeStruct((n, LANES), jnp.int32),
                   jax.ShapeDtypeStruct((n, LANES), F32),
                   jax.ShapeDtypeStruct((1, LANES), F32)],
        compiler_params=_params("arbitrary"),
        name="wo_router",
    )(o_na, o_df, x2d, ada_g, w_o_bf, g_ffn, wr_hi, wr_lo, br_pad, tri, cnt0)


def _expert_kernel(be_ref, na_ref, xs_ref, wg_ref, bg_ref, wu_ref, bu_ref, wd_ref, bd_ref, o_ref,
                   wg_bf, wu_bf, wd_bf):
    i = pl.program_id(0)
    active = i < na_ref[0]
    new_expert = jnp.logical_or(i == 0, be_ref[i] != be_ref[jnp.maximum(i - 1, 0)])

    @pl.when(jnp.logical_and(active, new_expert))
    def _():
        wg_bf[...] = wg_ref[0].astype(BF16)
        wu_bf[...] = wu_ref[0].astype(BF16)
        wd_bf[...] = wd_ref[0].astype(BF16)

    @pl.when(active)
    def _():
        x_lo, x_hi = _unpack_halves(xs_ref[...])
        x_lo = x_lo.astype(BF16)
        x_hi = x_hi.astype(BF16)

        def proj(w_bf):
            return (jnp.dot(x_lo, w_bf[:PACKED, :], preferred_element_type=F32)
                    + jnp.dot(x_hi, w_bf[PACKED:, :], preferred_element_type=F32))

        g = jnp.minimum(proj(wg_bf) + bg_ref[0], SWIGLU_LIMIT)
        u = jnp.clip(proj(wu_bf) + bu_ref[0], -SWIGLU_LIMIT, SWIGLU_LIMIT)
        act = g * jax.nn.sigmoid(SWIGLU_ALPHA * g) * (u + 1.0)
        out = jnp.dot(act.astype(BF16), wd_bf[...], preferred_element_type=F32) + bd_ref[0]
        o_ref[...] = _pack_halves(out.astype(BF16).astype(F32))

    @pl.when(i >= na_ref[0])
    def _():
        o_ref[...] = jnp.zeros_like(o_ref)


def _experts(block_e, n_active, xs, wg, bg, wu, bu, wd, bd):
    cap = xs.shape[0]
    n_blocks = cap // EXPERT_BLOCK
    xmap = lambda i, be, na: (jnp.minimum(i, na[0] - 1), 0)
    wmap = lambda i, be, na: (be[i], 0, 0)
    grid_spec = pltpu.PrefetchScalarGridSpec(
        num_scalar_prefetch=2,
        grid=(n_blocks,),
        in_specs=[pl.BlockSpec((EXPERT_BLOCK, PACKED), xmap),
                  pl.BlockSpec((1, D_MODEL, D_MODEL), wmap),
                  pl.BlockSpec((1, 1, D_MODEL), wmap),
                  pl.BlockSpec((1, D_MODEL, D_MODEL), wmap),
                  pl.BlockSpec((1, 1, D_MODEL), wmap),
                  pl.BlockSpec((1, D_MODEL, D_MODEL), wmap),
                  pl.BlockSpec((1, 1, D_MODEL), wmap)],
        out_specs=pl.BlockSpec((EXPERT_BLOCK, PACKED), lambda i, be, na: (i, 0)),
        scratch_shapes=[pltpu.VMEM((D_MODEL, D_MODEL), BF16)] * 3,
    )
    return pl.pallas_call(
        _expert_kernel,
        grid_spec=grid_spec,
        out_shape=jax.ShapeDtypeStruct((cap, PACKED), U32),
        compiler_params=_params("arbitrary"),
        name="experts",
    )(block_e, n_active, xs, wg, bg, wu, bu, wd, bd)


def _sc_gather(table, idx):
    n_out = idx.shape[0]
    width = table.shape[1]
    per_worker = n_out // SC_WORKERS
    n_chunks = per_worker // GATHER_ROWS
    assert per_worker * SC_WORKERS == n_out and n_chunks * GATHER_ROWS == per_worker and n_chunks % 2 == 0
    idx3 = idx.reshape(SC_WORKERS, n_chunks, GATHER_ROWS)
    mesh = plsc.VectorSubcoreMesh(core_axis_name="core", subcore_axis_name="subcore")

    @functools.partial(
        pl.kernel, mesh=mesh,
        out_type=jax.ShapeDtypeStruct((n_out, width), table.dtype),
        scratch_types=[pltpu.VMEM((n_chunks, GATHER_ROWS), jnp.int32),
                       pltpu.VMEM((2, GATHER_ROWS, width), table.dtype),
                       pltpu.SemaphoreType.DMA((2,)),
                       pltpu.SemaphoreType.DMA((2,))])
    def gather_kernel(table_hbm, idx_hbm, out_hbm, idx_v, rows_v, gsem, wsem):
        wid = lax.axis_index("subcore") * SC_CORES + lax.axis_index("core")
        base = wid * per_worker
        pltpu.sync_copy(idx_hbm.at[wid], idx_v)

        def gather(j, slot):
            return pltpu.make_async_copy(table_hbm.at[idx_v.at[j]], rows_v.at[slot], gsem.at[slot])

        def write(j, slot):
            dst = out_hbm.at[pl.ds(pl.multiple_of(base + j * GATHER_ROWS, GATHER_ROWS), GATHER_ROWS)]
            return pltpu.make_async_copy(rows_v.at[slot], dst, wsem.at[slot])

        gather(0, 0).start()

        @pl.loop(0, n_chunks, step=2)
        def _(j):
            for slot in range(2):
                jj = j + slot
                gather(jj, slot).wait()

                @pl.when(jj >= 1)
                def _():
                    write(jj - 1, 1 - slot).wait()

                @pl.when(jj + 1 < n_chunks)
                def _():
                    gather(jj + 1, 1 - slot).start()

                write(jj, slot).start()

        write(n_chunks - 1, 1).wait()

    return gather_kernel(table, idx3)


def _sc_scatter(rows, idx, n_out):
    n_src, width = rows.shape
    n_idx = idx.shape[0]
    per_worker = n_idx // SC_WORKERS
    n_chunks = per_worker // GATHER_ROWS
    assert per_worker * SC_WORKERS == n_idx and n_chunks * GATHER_ROWS == per_worker and n_chunks % 2 == 0
    assert n_src % per_worker == 0
    idx3 = idx.reshape(SC_WORKERS, n_chunks, GATHER_ROWS)
    mesh = plsc.VectorSubcoreMesh(core_axis_name="core", subcore_axis_name="subcore")

    @functools.partial(
        pl.kernel, mesh=mesh,
        out_type=jax.ShapeDtypeStruct((n_out, width), rows.dtype),
        scratch_types=[pltpu.VMEM((n_chunks, GATHER_ROWS), jnp.int32),
                       pltpu.VMEM((2, GATHER_ROWS, width), rows.dtype),
                       pltpu.SemaphoreType.DMA((2,)),
                       pltpu.SemaphoreType.DMA((2,))])
    def scatter_kernel(rows_hbm, idx_hbm, out_hbm, idx_v, rows_v, rsem, wsem):
        wid = lax.axis_index("subcore") * SC_CORES + lax.axis_index("core")
        base = lax.rem(wid * per_worker, n_src)
        pltpu.sync_copy(idx_hbm.at[wid], idx_v)

        def read(j, slot):
            src = rows_hbm.at[pl.ds(pl.multiple_of(base + j * GATHER_ROWS, GATHER_ROWS), GATHER_ROWS)]
            return pltpu.make_async_copy(src, rows_v.at[slot], rsem.at[slot])

        def write(j, slot):
            return pltpu.make_async_copy(rows_v.at[slot], out_hbm.at[idx_v.at[j]], wsem.at[slot])

        read(0, 0).start()

        @pl.loop(0, n_chunks, step=2)
        def _(j):
            for slot in range(2):
                jj = j + slot
                read(jj, slot).wait()

                @pl.when(jj >= 1)
                def _():
                    write(jj - 1, 1 - slot).wait()

                @pl.when(jj + 1 < n_chunks)
                def _():
                    read(jj + 1, 1 - slot).start()

                write(jj, slot).start()

        write(n_chunks - 1, 1).wait()

    return scatter_kernel(rows, idx3)


def _combine_kernel(x1_ref, y0_ref, y1_ref, y2_ref, y3_ref, tw_ref, ada_ref, o_ref):
    tw = tw_ref[...]
    acc_lo = jnp.zeros((x1_ref.shape[0], PACKED), F32)
    acc_hi = jnp.zeros((x1_ref.shape[0], PACKED), F32)
    for j, y_ref in enumerate((y0_ref, y1_ref, y2_ref, y3_ref)):
        lo, hi = _unpack_halves(y_ref[...])
        acc_lo = acc_lo + tw[:, j:j + 1] * lo
        acc_hi = acc_hi + tw[:, j:j + 1] * hi
    o_ref[:, :PACKED] = x1_ref[:, :PACKED] + ada_ref[0, 5:6, :PACKED] * acc_lo
    o_ref[:, PACKED:] = x1_ref[:, PACKED:] + ada_ref[0, 5:6, PACKED:] * acc_hi


def _combine(x1, ys, tw, ada_g, seq, row_off):
    n = x1.shape[0]
    tm = ROW_BLOCK
    per_seq = seq // tm
    off = row_off // tm
    per_choice = tw.shape[0] // tm
    y_specs = [pl.BlockSpec((tm, PACKED), functools.partial(lambda i, j: (j * per_choice + off + i, 0), j=j))
               for j in range(TOP_K)]
    return pl.pallas_call(
        _combine_kernel,
        grid=(n // tm,),
        in_specs=[pl.BlockSpec((tm, D_MODEL), lambda i: (i, 0)),
                  *y_specs,
                  pl.BlockSpec((tm, LANES), lambda i: (i + off, 0)),
                  pl.BlockSpec((1, 6, D_MODEL), lambda i: (i // per_seq, 0, 0))],
        out_specs=pl.BlockSpec((tm, D_MODEL), lambda i: (i, 0)),
        out_shape=jax.ShapeDtypeStruct((n, D_MODEL), F32),
        compiler_params=_params("arbitrary"),
        name="combine",
    )(x1, ys, ys, ys, ys, tw, ada_g)


def _rope_tables(seq):
    half = HEAD_DIM // 2
    inv = ROPE_THETA ** (-jnp.arange(half, dtype=F32) / half)
    ang = jnp.arange(seq, dtype=F32)[:, None] * inv[None, :]
    cos, sin = jnp.cos(ang), jnp.sin(ang)
    cos_h = jnp.concatenate([cos, cos], axis=-1)
    sin_h = jnp.concatenate([-sin, sin], axis=-1)
    reps = DIFF_WIDTH // HEAD_DIM
    return jnp.tile(cos_h, (1, reps)), jnp.tile(sin_h, (1, reps))


def _na_bias_table(rpb):
    cols = jnp.arange(GRID_W, dtype=jnp.int32)
    c_start = jnp.clip(cols - NA_WIN_COLS // 2, 0, GRID_W - NA_WIN_COLS)
    col_mask = (cols[None, :] >= c_start[:, None]) & (cols[None, :] < c_start[:, None] + NA_WIN_COLS)
    col_idx = jnp.clip(cols[None, :] - cols[:, None], -(NA_WIN_COLS - 1), NA_WIN_COLS - 1) + NA_WIN_COLS - 1
    delta = jnp.arange(NA_WIN_ROWS, dtype=jnp.int32)
    j = jnp.arange(NA_WIN_ROWS, dtype=jnp.int32)
    row_idx = j[None, :] - delta[:, None] + NA_WIN_ROWS - 1
    row_hot = (row_idx[:, :, None] == jnp.arange(2 * NA_WIN_ROWS - 1, dtype=jnp.int32)).astype(F32)
    col_hot = (col_idx[:, :, None] == jnp.arange(2 * NA_WIN_COLS - 1, dtype=jnp.int32)).astype(F32)
    bias = jnp.einsum('djr,hrc,qkc->hdjqk', row_hot, rpb.astype(F32), col_hot,
                      precision=lax.Precision.HIGHEST)
    bias = jnp.where(col_mask[None, None, None], bias * LOG2E, NEG_INF)
    bias = bias.transpose(0, 1, 2, 4, 3).reshape(NA_HEADS, NA_WIN_ROWS, NA_WIN_ROWS * GRID_W, GRID_W)
    bias = bias.reshape(NA_HEADS // 2, 2, NA_WIN_ROWS, NA_WIN_ROWS * GRID_W, GRID_W)
    return jnp.concatenate([bias[:, 0], bias[:, 1]], axis=-1)


def _routing(top_idx, rank, counts, n):
    n_blocks = n * TOP_K // EXPERT_BLOCK + N_EXPERTS
    experts = jnp.arange(N_EXPERTS, dtype=jnp.int32)
    padded = (counts + EXPERT_BLOCK - 1) // EXPERT_BLOCK * EXPERT_BLOCK
    pad_end = jnp.cumsum(padded)
    pad_start = pad_end - padded
    start_of = jnp.sum(jnp.where(top_idx[:, :, None] == experts, pad_start, 0), axis=-1)
    dest = (start_of + rank).T.reshape(-1)
    block_lo = jnp.arange(n_blocks, dtype=jnp.int32) * EXPERT_BLOCK
    block_e = jnp.minimum(jnp.sum((pad_end[None, :] <= block_lo[:, None]).astype(jnp.int32), axis=1),
                          N_EXPERTS - 1).astype(jnp.int32)
    n_active = (pad_end[-1] // EXPERT_BLOCK).astype(jnp.int32).reshape(1)
    return dest, block_e, n_active, n_blocks * EXPERT_BLOCK


def kernel(x_prompt, x_sample, c_prompt, c_sample, w_ada, b_ada, g_attn_norm, w_qkv, na_q_norm, na_k_norm, na_rpb, diff_q_norm, diff_k_norm, lambda_q1, lambda_k1, lambda_q2, lambda_k2, diff_subln, w_o, g_ffn_norm, w_router, b_router, w_gate, b_gate, w_up, b_up, w_down, b_down):
    l = 0
    groups = [(x_prompt, c_prompt), (x_sample, c_sample)]
    nb = [x.shape[0] for x, _ in groups]

    ada_all = _ada(jnp.concatenate([c for _, c in groups], axis=0), w_ada[l], b_ada[l])
    ada_all = ada_all.reshape(sum(nb), 6, D_MODEL)

    w_qkv_bf = w_qkv[l].astype(BF16)
    w_o_bf = w_o[l].astype(BF16)
    scale = HEAD_DIM ** -0.5
    reps = NA_WIDTH // HEAD_DIM
    gains = jnp.stack([jnp.tile(na_q_norm[l], reps) * (scale * LOG2E),
                       jnp.tile(na_k_norm[l], reps),
                       jnp.tile(diff_q_norm[l], reps) * (scale * LOG2E),
                       jnp.tile(diff_k_norm[l], reps)]).astype(F32)
    head_id = jnp.arange(MXU_DIM, dtype=jnp.int32) // HEAD_DIM
    bd = (head_id[:, None] == head_id[None, :]).astype(BF16)
    bias_t = _na_bias_table(na_rpb[l])
    lam_vecs = jnp.stack([lambda_q1[l], lambda_k1[l], lambda_q2[l], lambda_k2[l]]).astype(F32)
    subln_col = (diff_subln[l].astype(F32) * (1.0 - LAMBDA_INIT)).reshape(LANES, 1)
    wr = w_router[l].astype(F32)
    wr_pad = jnp.zeros((D_MODEL, LANES), F32).at[:, :N_EXPERTS].set(wr)
    wr_hi = wr_pad.astype(BF16)
    wr_lo = (wr_pad - wr_hi.astype(F32)).astype(BF16)
    br_pad = jnp.full((1, LANES), NEG_INF, F32).at[0, :N_EXPERTS].set(b_router[l].astype(F32))
    g_attn = g_attn_norm[l].reshape(1, D_MODEL).astype(F32)
    g_ffn = g_ffn_norm[l].reshape(1, D_MODEL).astype(F32)
    max_seq = max(x.shape[1] for x, _ in groups)
    cos_t, sin_t = _rope_tables(max_seq)

    rows = lax.broadcasted_iota(jnp.int32, (ROW_BLOCK, ROW_BLOCK), 0)
    cols = lax.broadcasted_iota(jnp.int32, (ROW_BLOCK, ROW_BLOCK), 1)
    tri = (cols < rows).astype(BF16)
    cnt = jnp.zeros((1, LANES), F32)

    x1s, h2s, tis, tws = [], [], [], []
    b0 = 0
    for (x, _), b in zip(groups, nb):
        seq = x.shape[1]
        x2d = x.reshape(b * seq, D_MODEL)
        ada_g = ada_all[b0:b0 + b]
        b0 += b
        qkv = _qkv(x2d, ada_g, g_attn, w_qkv_bf, gains, cos_t, sin_t, bd, seq)
        o_na = _na(qkv, bias_t, b, seq)
        o_df = _diff(qkv, lam_vecs, subln_col, b, seq)
        x1, h2, ti, tw, cnt = _wo(o_na, o_df, x2d, ada_g, w_o_bf, g_ffn, wr_hi, wr_lo, br_pad, tri, cnt, seq)
        x1s.append(x1)
        h2s.append(h2)
        tis.append(ti)
        tws.append(tw)

    h2_all = jnp.concatenate(h2s, axis=0)
    ti_all = jnp.concatenate(tis, axis=0)
    tw_all = jnp.concatenate(tws, axis=0)
    n = h2_all.shape[0]
    counts = cnt[0, :N_EXPERTS].astype(jnp.int32)
    dest, block_e, n_active, cap = _routing(ti_all[:, :TOP_K], ti_all[:, TOP_K:2 * TOP_K], counts, n)

    xs = _sc_scatter(h2_all, dest, cap)
    ys_sorted = _experts(block_e, n_active, xs,
                         w_gate[l], b_gate[l].reshape(N_EXPERTS, 1, D_MODEL).astype(F32),
                         w_up[l], b_up[l].reshape(N_EXPERTS, 1, D_MODEL).astype(F32),
                         w_down[l], b_down[l].reshape(N_EXPERTS, 1, D_MODEL).astype(F32))
    ys = _sc_gather(ys_sorted, dest)

    outs = []
    b0 = 0
    row_off = 0
    for (x, _), b, x1 in zip(groups, nb, x1s):
        seq = x.shape[1]
        ada_g = ada_all[b0:b0 + b]
        b0 += b
        y = _combine(x1, ys, tw_all, ada_g, seq, row_off)
        row_off += b * seq
        outs.append(y.reshape(b, seq, D_MODEL))
    return tuple(outs)
```

```python
import functools
import math

import jax
import jax.numpy as jnp
from jax import lax
from jax.experimental import pallas as pl
from jax.experimental.pallas import tpu as pltpu
from jax.experimental.pallas import tpu_sc as plsc

F32 = jnp.float32
BF16 = jnp.bfloat16
U32 = jnp.uint32

D_MODEL = 1024
HEAD_DIM = 64
NA_HEADS = 8
NA_WIDTH = 512
DIFF_HEADS = 4
DIFF_WIDTH = 512
QKV_COLS = 3072
GRID_W = 64
NA_WIN_ROWS = 8
NA_WIN_COLS = 16
ROPE_THETA = 10000.0
N_EXPERTS = 32
TOP_K = 4
SWIGLU_LIMIT = 7.0
SWIGLU_ALPHA = 1.702
EPS = 1e-5
NEG_INF = -1e30
LAMBDA_INIT = 0.8 - 0.6 * math.exp(-0.3 * 0)
LOG2E = 1.4426950408889634

LANES = 128
MXU_DIM = 256
VMEM_LIMIT = 56 * 1024 * 1024

ROW_BLOCK = 512
Q_BLOCK = 256
EXPERT_BLOCK = 512
NA_ROWS_PER_TRIP = 4
KV_CHUNK = 512


PACKED = D_MODEL // 2
SC_CORES = 2
SC_SUBCORES = 16
SC_WORKERS = SC_CORES * SC_SUBCORES
GATHER_ROWS = 64


def _params(*sem):
    return pltpu.CompilerParams(dimension_semantics=sem, vmem_limit_bytes=VMEM_LIMIT)


def _pack_halves(x):
    w = x.shape[1] // 2
    bits = lax.bitcast_convert_type(x, U32)
    return (bits[:, :w] >> 16) | bits[:, w:]


def _unpack_halves(word):
    lo = lax.bitcast_convert_type(word << 16, F32)
    hi = lax.bitcast_convert_type(word & jnp.uint32(0xFFFF0000), F32)
    return lo, hi


def _ada_kernel(c_ref, w_ref, b_ref, o_ref):
    c = c_ref[...]
    s = c * jax.nn.sigmoid(c)
    o_ref[...] = jnp.dot(s, w_ref[...], preferred_element_type=F32,
                         precision=lax.Precision.HIGHEST) + b_ref[...]


def _ada(c_all, w_ada, b_ada):
    nb = c_all.shape[0]
    n_out = w_ada.shape[1]
    blk = D_MODEL
    return pl.pallas_call(
        _ada_kernel,
        grid=(n_out // blk,),
        in_specs=[pl.BlockSpec((nb, D_MODEL), lambda j: (0, 0)),
                  pl.BlockSpec((D_MODEL, blk), lambda j: (0, j)),
                  pl.BlockSpec((1, blk), lambda j: (0, j))],
        out_specs=pl.BlockSpec((nb, blk), lambda j: (0, j)),
        out_shape=jax.ShapeDtypeStruct((nb, n_out), F32),
        compiler_params=_params("arbitrary"),
        name="ada",
    )(c_all, w_ada, b_ada.reshape(1, n_out))


def _head_sumsq(y, bd):
    sq = (y * y).astype(BF16)
    parts = [jnp.dot(sq[:, c:c + MXU_DIM], bd, preferred_element_type=F32)
             for c in range(0, y.shape[1], MXU_DIM)]
    return jnp.concatenate(parts, axis=1)


def _qkv_kernel(x_ref, ada_ref, g_ref, w_ref, gain_ref, cos_ref, sin_ref, bd_ref, o_ref):
    x = x_ref[...]
    ms = jnp.mean(x * x, axis=-1, keepdims=True)
    xn = x * lax.rsqrt(ms + EPS) * g_ref[...]
    sh = ada_ref[0, 0:1, :]
    sc = ada_ref[0, 1:2, :]
    h = (xn * (1.0 + sc) + sh).astype(BF16)
    bd = bd_ref[...]
    lane = lax.broadcasted_iota(jnp.int32, (x.shape[0], NA_WIDTH), 1)
    first_half = (lane & (HEAD_DIM // 2)) == 0
    for grp in range(6):
        cols = slice(grp * 512, (grp + 1) * 512)
        acc = jnp.dot(h, w_ref[:, cols], preferred_element_type=F32)
        if grp in (2, 5):
            o_ref[:, cols] = acc.astype(BF16)
            continue
        gi = {0: 0, 1: 1, 3: 2, 4: 3}[grp]
        ss = _head_sumsq(acc, bd)
        y = acc * lax.rsqrt(ss * (1.0 / HEAD_DIM) + EPS) * gain_ref[gi:gi + 1, :]
        if grp in (3, 4):
            partner = jnp.where(first_half,
                                pltpu.roll(y, NA_WIDTH - HEAD_DIM // 2, axis=1),
                                pltpu.roll(y, HEAD_DIM // 2, axis=1))
            y = y * cos_ref[...] + partner * sin_ref[...]
        o_ref[:, cols] = y.astype(BF16)


def _qkv(x2d, ada_g, g_attn, w_qkv_bf, gains, cos_t, sin_t, bd, seq):
    n = x2d.shape[0]
    tm = ROW_BLOCK
    per_seq = seq // tm
    return pl.pallas_call(
        _qkv_kernel,
        grid=(n // tm,),
        in_specs=[pl.BlockSpec((tm, D_MODEL), lambda i: (i, 0)),
                  pl.BlockSpec((1, 6, D_MODEL), lambda i: (i // per_seq, 0, 0)),
                  pl.BlockSpec((1, D_MODEL), lambda i: (0, 0)),
                  pl.BlockSpec((D_MODEL, QKV_COLS), lambda i: (0, 0)),
                  pl.BlockSpec((4, NA_WIDTH), lambda i: (0, 0)),
                  pl.BlockSpec((tm, DIFF_WIDTH), lambda i: (i % per_seq, 0)),
                  pl.BlockSpec((tm, DIFF_WIDTH), lambda i: (i % per_seq, 0)),
                  pl.BlockSpec((MXU_DIM, MXU_DIM), lambda i: (0, 0))],
        out_specs=pl.BlockSpec((tm, QKV_COLS), lambda i: (i, 0)),
        out_shape=jax.ShapeDtypeStruct((n, QKV_COLS), BF16),
        compiler_params=_params("arbitrary"),
        name="qkv",
    )(x2d, ada_g, g_attn, w_qkv_bf, gains, cos_t, sin_t, bd)


def _na_kernel(q_ref, k_ref, v_ref, bias_ref, o_ref, *, rows):
    lane = lax.broadcasted_iota(jnp.int32, (GRID_W, LANES), 1)
    head0 = lane < HEAD_DIM
    win = NA_WIN_ROWS * GRID_W

    def one_row(r):
        r_start = jnp.clip(r - NA_WIN_ROWS // 2, 0, rows - NA_WIN_ROWS)
        delta = r - r_start
        q = q_ref[pl.ds(pl.multiple_of(r * GRID_W, GRID_W), GRID_W), :]
        k0 = pl.multiple_of(r_start * GRID_W, GRID_W)
        kw = k_ref[pl.ds(k0, win), :]
        vw = v_ref[pl.ds(k0, win), :]
        zero = jnp.zeros_like(q)
        qm = jnp.concatenate([jnp.where(head0, q, zero), jnp.where(head0, zero, q)], axis=0)
        s = lax.dot_general(kw, qm, (((1,), (1,)), ((), ())), preferred_element_type=F32)
        s = s + bias_ref[0, delta]
        m = jnp.max(s, axis=0, keepdims=True)
        p = jnp.exp2(s - m)
        l = jnp.sum(p, axis=0, keepdims=True)
        p = (p * (1.0 / l)).astype(BF16)
        o2 = lax.dot_general(p, vw, (((0,), (0,)), ((), ())), preferred_element_type=F32)
        o = jnp.where(head0, o2[:GRID_W], o2[GRID_W:])
        o_ref[pl.ds(pl.multiple_of(r * GRID_W, GRID_W), GRID_W), :] = o.astype(BF16)

    def body(i, carry):
        for u in range(NA_ROWS_PER_TRIP):
            one_row(i * NA_ROWS_PER_TRIP + u)
        return carry

    lax.fori_loop(0, rows // NA_ROWS_PER_TRIP, body, 0)


def _na(qkv, bias_t, batch, seq):
    rows = seq // GRID_W
    n_pairs = NA_HEADS // 2
    return pl.pallas_call(
        functools.partial(_na_kernel, rows=rows),
        grid=(batch, n_pairs),
        in_specs=[pl.BlockSpec((seq, LANES), lambda b, hp: (b, hp)),
                  pl.BlockSpec((seq, LANES), lambda b, hp: (b, n_pairs + hp)),
                  pl.BlockSpec((seq, LANES), lambda b, hp: (b, 2 * n_pairs + hp)),
                  pl.BlockSpec((1, NA_WIN_ROWS, NA_WIN_ROWS * GRID_W, LANES), lambda b, hp: (hp, 0, 0, 0))],
        out_specs=pl.BlockSpec((seq, LANES), lambda b, hp: (b, hp)),
        out_shape=jax.ShapeDtypeStruct((batch * seq, NA_WIDTH), BF16),
        compiler_params=_params("arbitrary", "arbitrary"),
        name="na_attn",
    )(qkv, qkv, qkv, bias_t)


def _diff_kernel(q_ref, k_ref, v_ref, lam_ref, g_ref, o_ref, vt_ref):
    @pl.when(pl.program_id(2) == 0)
    def _():
        vt_ref[...] = v_ref[...].astype(F32).T.astype(BF16)

    lq1 = lam_ref[0:1, :]
    lk1 = lam_ref[1:2, :]
    lq2 = lam_ref[2:3, :]
    lk2 = lam_ref[3:4, :]
    lam = (jnp.exp(jnp.sum(lq1 * lk1, axis=-1, keepdims=True))
           - jnp.exp(jnp.sum(lq2 * lk2, axis=-1, keepdims=True)) + LAMBDA_INIT)

    q = q_ref[...]
    lane = lax.broadcasted_iota(jnp.int32, q.shape, 1)
    zero = jnp.zeros_like(q)
    qc = [jnp.where(lane < HEAD_DIM, q, zero), jnp.where(lane < HEAD_DIM, zero, q)]
    seq = k_ref.shape[0]
    m = [None, None]
    l = [None, None]
    acc = [None, None]
    for c in range(seq // KV_CHUNK):
        kc = k_ref[c * KV_CHUNK:(c + 1) * KV_CHUNK, :]
        vtc = vt_ref[:, c * KV_CHUNK:(c + 1) * KV_CHUNK]
        for comp in range(2):
            s = lax.dot_general(kc, qc[comp], (((1,), (1,)), ((), ())), preferred_element_type=F32)
            mc = jnp.max(s, axis=0, keepdims=True)
            m_new = mc if c == 0 else jnp.maximum(m[comp], mc)
            p = jnp.exp2(s - m_new)
            ps = jnp.sum(p, axis=0, keepdims=True)
            pv = jnp.dot(vtc, p.astype(BF16), preferred_element_type=F32)
            if c == 0:
                l[comp], acc[comp] = ps, pv
            else:
                alpha = jnp.exp2(m[comp] - m_new)
                l[comp] = alpha * l[comp] + ps
                acc[comp] = alpha * acc[comp] + pv
            m[comp] = m_new
    o = acc[0] * (1.0 / l[0]) - (lam / l[1]) * acc[1]
    ms = jnp.mean(o * o, axis=0, keepdims=True)
    y = o * lax.rsqrt(ms + EPS) * g_ref[...]
    o_ref[...] = y.T.astype(BF16)


def _diff(qkv, lam_vecs, subln_col, batch, seq):
    tq = Q_BLOCK
    nq = seq // tq
    base = 3 * NA_WIDTH // LANES
    nh = DIFF_HEADS
    return pl.pallas_call(
        _diff_kernel,
        grid=(batch, nh, nq),
        in_specs=[pl.BlockSpec((tq, LANES), lambda b, h, i: (b * nq + i, base + h)),
                  pl.BlockSpec((seq, LANES), lambda b, h, i: (b, base + nh + h)),
                  pl.BlockSpec((seq, LANES), lambda b, h, i: (b, base + 2 * nh + h)),
                  pl.BlockSpec((4, HEAD_DIM), lambda b, h, i: (0, 0)),
                  pl.BlockSpec((LANES, 1), lambda b, h, i: (0, 0))],
        out_specs=pl.BlockSpec((tq, LANES), lambda b, h, i: (b * nq + i, h)),
        out_shape=jax.ShapeDtypeStruct((batch * seq, DIFF_WIDTH), BF16),
        scratch_shapes=[pltpu.VMEM((LANES, seq), BF16)],
        compiler_params=_params("arbitrary", "arbitrary", "arbitrary"),
        name="diff_attn",
    )(qkv, qkv, qkv, lam_vecs, subln_col)


def _wo_kernel(ona_ref, odf_ref, x_ref, ada_ref, wo_ref, g_ref, wrh_ref, wrl_ref, br_ref, tri_ref, cnt0_ref,
               x1_ref, h2_ref, ti_ref, tw_ref, cnt_ref):
    mix = (jnp.dot(ona_ref[...], wo_ref[:NA_WIDTH, :], preferred_element_type=F32)
           + jnp.dot(odf_ref[...], wo_ref[NA_WIDTH:, :], preferred_element_type=F32))
    gt1 = ada_ref[0, 2:3, :]
    sh2 = ada_ref[0, 3:4, :]
    sc2 = ada_ref[0, 4:5, :]
    x1 = x_ref[...] + gt1 * mix
    x1_ref[...] = x1
    ms = jnp.mean(x1 * x1, axis=-1, keepdims=True)
    h2 = x1 * lax.rsqrt(ms + EPS) * g_ref[...] * (1.0 + sc2) + sh2
    hi = h2.astype(BF16)
    h2_ref[...] = _pack_halves(hi.astype(F32))
    lo = (h2 - hi.astype(F32)).astype(BF16)
    logits = (jnp.dot(hi, wrh_ref[...], preferred_element_type=F32)
              + jnp.dot(hi, wrl_ref[...], preferred_element_type=F32)
              + jnp.dot(lo, wrh_ref[...], preferred_element_type=F32)) + br_ref[...]
    lane = lax.broadcasted_iota(jnp.int32, logits.shape, 1).astype(F32)
    vals = []
    idxs = []
    cur = logits
    for _ in range(TOP_K):
        m = jnp.max(cur, axis=-1, keepdims=True)
        idx = jnp.min(jnp.where(cur == m, lane, float(LANES)), axis=-1, keepdims=True)
        vals.append(m)
        idxs.append(idx)
        cur = jnp.where(lane == idx, -jnp.inf, cur)
    es = [jnp.exp(v - vals[0]) for v in vals]
    inv = 1.0 / (es[0] + es[1] + es[2] + es[3])

    @pl.when(pl.program_id(0) == 0)
    def _():
        cnt_ref[...] = cnt0_ref[...]

    sel = jnp.zeros(logits.shape, F32)
    for j in range(TOP_K):
        sel = sel + jnp.where(lane == idxs[j], 1.0, 0.0)
    before = jnp.dot(tri_ref[...], sel.astype(BF16), preferred_element_type=F32) + cnt_ref[...]
    cnt_ref[...] = cnt_ref[...] + jnp.sum(sel, axis=0, keepdims=True)
    ranks = [jnp.sum(jnp.where(lane == idxs[j], before, 0.0), axis=-1, keepdims=True) for j in range(TOP_K)]

    ti = jnp.zeros(logits.shape, F32)
    tw = jnp.zeros(logits.shape, F32)
    for j in range(TOP_K):
        ti = jnp.where(lane == float(j), idxs[j], ti)
        ti = jnp.where(lane == float(TOP_K + j), ranks[j], ti)
        tw = jnp.where(lane == float(j), es[j] * inv, tw)
    ti_ref[...] = ti.astype(jnp.int32)
    tw_ref[...] = tw


def _wo(o_na, o_df, x2d, ada_g, w_o_bf, g_ffn, wr_hi, wr_lo, br_pad, tri, cnt0, seq):
    n = x2d.shape[0]
    tm = ROW_BLOCK
    per_seq = seq // tm
    row = lambda i: (i, 0)
    const = lambda i: (0, 0)
    return pl.pallas_call(
        _wo_kernel,
        grid=(n // tm,),
        in_specs=[pl.BlockSpec((tm, NA_WIDTH), row),
                  pl.BlockSpec((tm, DIFF_WIDTH), row),
                  pl.BlockSpec((tm, D_MODEL), row),
                  pl.BlockSpec((1, 6, D_MODEL), lambda i: (i // per_seq, 0, 0)),
                  pl.BlockSpec((D_MODEL, D_MODEL), const),
                  pl.BlockSpec((1, D_MODEL), const),
                  pl.BlockSpec((D_MODEL, LANES), const),
                  pl.BlockSpec((D_MODEL, LANES), const),
                  pl.BlockSpec((1, LANES), const),
                  pl.BlockSpec((tm, tm), const),
                  pl.BlockSpec((1, LANES), const)],
        out_specs=[pl.BlockSpec((tm, D_MODEL), row),
                   pl.BlockSpec((tm, PACKED), row),
                   pl.BlockSpec((tm, LANES), row),
                   pl.BlockSpec((tm, LANES), row),
                   pl.BlockSpec((1, LANES), const)],
        out_shape=[jax.ShapeDtypeStruct((n, D_MODEL), F32),
                   jax.ShapeDtypeStruct((n, PACKED), U32),
                   jax.ShapeDtypeStruct((n, LANES), jnp.int32),
                   jax.ShapeDtypeStruct((n, LANES), F32),
                   jax.ShapeDtypeStruct((1, LANES), F32)],
        compiler_params=_params("arbitrary"),
        name="wo_router",
    )(o_na, o_df, x2d, ada_g, w_o_bf, g_ffn, wr_hi, wr_lo, br_pad, tri, cnt0)


def _expert_kernel(be_ref, na_ref, xs_ref, wg_ref, bg_ref, wu_ref, bu_ref, wd_ref, bd_ref, o_ref,
                   wg_bf, wu_bf, wd_bf):
    i = pl.program_id(0)
    active = i < na_ref[0]
    new_expert = jnp.logical_or(i == 0, be_ref[i] != be_ref[jnp.maximum(i - 1, 0)])

    @pl.when(jnp.logical_and(active, new_expert))
    def _():
        wg_bf[...] = wg_ref[0].astype(BF16)
        wu_bf[...] = wu_ref[0].astype(BF16)
        wd_bf[...] = wd_ref[0].astype(BF16)

    @pl.when(active)
    def _():
        x_lo, x_hi = _unpack_halves(xs_ref[...])
        x_lo = x_lo.astype(BF16)
        x_hi = x_hi.astype(BF16)

        def proj(w_bf):
            return (jnp.dot(x_lo, w_bf[:PACKED, :], preferred_element_type=F32)
                    + jnp.dot(x_hi, w_bf[PACKED:, :], preferred_element_type=F32))

        g = jnp.minimum(proj(wg_bf) + bg_ref[0], SWIGLU_LIMIT)
        u = jnp.clip(proj(wu_bf) + bu_ref[0], -SWIGLU_LIMIT, SWIGLU_LIMIT)
        act = g * jax.nn.sigmoid(SWIGLU_ALPHA * g) * (u + 1.0)
        out = jnp.dot(act.astype(BF16), wd_bf[...], preferred_element_type=F32) + bd_ref[0]
        o_ref[...] = _pack_halves(out.astype(BF16).astype(F32))

    @pl.when(i >= na_ref[0])
    def _():
        o_ref[...] = jnp.zeros_like(o_ref)


def _experts(block_e, n_active, xs, wg, bg, wu, bu, wd, bd):
    cap = xs.shape[0]
    n_blocks = cap // EXPERT_BLOCK
    xmap = lambda i, be, na: (jnp.minimum(i, na[0] - 1), 0)
    wmap = lambda i, be, na: (be[i], 0, 0)
    grid_spec = pltpu.PrefetchScalarGridSpec(
        num_scalar_prefetch=2,
        grid=(n_blocks,),
        in_specs=[pl.BlockSpec((EXPERT_BLOCK, PACKED), xmap),
                  pl.BlockSpec((1, D_MODEL, D_MODEL), wmap),
                  pl.BlockSpec((1, 1, D_MODEL), wmap),
                  pl.BlockSpec((1, D_MODEL, D_MODEL), wmap),
                  pl.BlockSpec((1, 1, D_MODEL), wmap),
                  pl.BlockSpec((1, D_MODEL, D_MODEL), wmap),
                  pl.BlockSpec((1, 1, D_MODEL), wmap)],
        out_specs=pl.BlockSpec((EXPERT_BLOCK, PACKED), lambda i, be, na: (i, 0)),
        scratch_shapes=[pltpu.VMEM((D_MODEL, D_MODEL), BF16)] * 3,
    )
    return pl.pallas_call(
        _expert_kernel,
        grid_spec=grid_spec,
        out_shape=jax.ShapeDtypeStruct((cap, PACKED), U32),
        compiler_params=_params("arbitrary"),
        name="experts",
    )(block_e, n_active, xs, wg, bg, wu, bu, wd, bd)


def _sc_gather(table, idx):
    n_out = idx.shape[0]
    width = table.shape[1]
    per_worker = n_out // SC_WORKERS
    n_chunks = per_worker // GATHER_ROWS
    assert per_worker * SC_WORKERS == n_out and n_chunks * GATHER_ROWS == per_worker and n_chunks % 2 == 0
    idx3 = idx.reshape(SC_WORKERS, n_chunks, GATHER_ROWS)
    mesh = plsc.VectorSubcoreMesh(core_axis_name="core", subcore_axis_name="subcore")

    @functools.partial(
        pl.kernel, mesh=mesh,
        out_type=jax.ShapeDtypeStruct((n_out, width), table.dtype),
        scratch_types=[pltpu.VMEM((n_chunks, GATHER_ROWS), jnp.int32),
                       pltpu.VMEM((2, GATHER_ROWS, width), table.dtype),
                       pltpu.SemaphoreType.DMA((2,)),
                       pltpu.SemaphoreType.DMA((2,))])
    def gather_kernel(table_hbm, idx_hbm, out_hbm, idx_v, rows_v, gsem, wsem):
        wid = lax.axis_index("subcore") * SC_CORES + lax.axis_index("core")
        base = wid * per_worker
        pltpu.sync_copy(idx_hbm.at[wid], idx_v)

        def gather(j, slot):
            return pltpu.make_async_copy(table_hbm.at[idx_v.at[j]], rows_v.at[slot], gsem.at[slot])

        def write(j, slot):
            dst = out_hbm.at[pl.ds(pl.multiple_of(base + j * GATHER_ROWS, GATHER_ROWS), GATHER_ROWS)]
            return pltpu.make_async_copy(rows_v.at[slot], dst, wsem.at[slot])

        gather(0, 0).start()

        @pl.loop(0, n_chunks, step=2)
        def _(j):
            for slot in range(2):
                jj = j + slot
                gather(jj, slot).wait()

                @pl.when(jj >= 1)
                def _():
                    write(jj - 1, 1 - slot).wait()

                @pl.when(jj + 1 < n_chunks)
                def _():
                    gather(jj + 1, 1 - slot).start()

                write(jj, slot).start()

        write(n_chunks - 1, 1).wait()

    return gather_kernel(table, idx3)


def _sc_scatter(rows, idx, n_out):
    n_src, width = rows.shape
    n_idx = idx.shape[0]
    per_worker = n_idx // SC_WORKERS
    n_chunks = per_worker // GATHER_ROWS
    assert per_worker * SC_WORKERS == n_idx and n_chunks * GATHER_ROWS == per_worker and n_chunks % 2 == 0
    assert n_src % per_worker == 0
    idx3 = idx.reshape(SC_WORKERS, n_chunks, GATHER_ROWS)
    mesh = plsc.VectorSubcoreMesh(core_axis_name="core", subcore_axis_name="subcore")

    @functools.partial(
        pl.kernel, mesh=mesh,
        out_type=jax.ShapeDtypeStruct((n_out, width), rows.dtype),
        scratch_types=[pltpu.VMEM((n_chunks, GATHER_ROWS), jnp.int32),
                       pltpu.VMEM((2, GATHER_ROWS, width), rows.dtype),
                       pltpu.SemaphoreType.DMA((2,)),
                       pltpu.SemaphoreType.DMA((2,))])
    def scatter_kernel(rows_hbm, idx_hbm, out_hbm, idx_v, rows_v, rsem, wsem):
        wid = lax.axis_index("subcore") * SC_CORES + lax.axis_index("core")
        base = lax.rem(wid * per_worker, n_src)
        pltpu.sync_copy(idx_hbm.at[wid], idx_v)

        def read(j, slot):
            src = rows_hbm.at[pl.ds(pl.multiple_of(base + j * GATHER_ROWS, GATHER_ROWS), GATHER_ROWS)]
            return pltpu.make_async_copy(src, rows_v.at[slot], rsem.at[slot])

        def write(j, slot):
            return pltpu.make_async_copy(rows_v.at[slot], out_hbm.at[idx_v.at[j]], wsem.at[slot])

        read(0, 0).start()

        @pl.loop(0, n_chunks, step=2)
        def _(j):
            for slot in range(2):
                jj = j + slot
                read(jj, slot).wait()

                @pl.when(jj >= 1)
                def _():
                    write(jj - 1, 1 - slot).wait()

                @pl.when(jj + 1 < n_chunks)
                def _():
                    read(jj + 1, 1 - slot).start()

                write(jj, slot).start()

        write(n_chunks - 1, 1).wait()

    return scatter_kernel(rows, idx3)


def _combine_kernel(x1_ref, y0_ref, y1_ref, y2_ref, y3_ref, tw_ref, ada_ref, o_ref):
    tw = tw_ref[...]
    acc_lo = jnp.zeros((x1_ref.shape[0], PACKED), F32)
    acc_hi = jnp.zeros((x1_ref.shape[0], PACKED), F32)
    for j, y_ref in enumerate((y0_ref, y1_ref, y2_ref, y3_ref)):
        lo, hi = _unpack_halves(y_ref[...])
        acc_lo = acc_lo + tw[:, j:j + 1] * lo
        acc_hi = acc_hi + tw[:, j:j + 1] * hi
    o_ref[:, :PACKED] = x1_ref[:, :PACKED] + ada_ref[0, 5:6, :PACKED] * acc_lo
    o_ref[:, PACKED:] = x1_ref[:, PACKED:] + ada_ref[0, 5:6, PACKED:] * acc_hi


def _combine(x1, ys, tw, ada_g, seq, row_off):
    n = x1.shape[0]
    tm = ROW_BLOCK
    per_seq = seq // tm
    off = row_off // tm
    per_choice = tw.shape[0] // tm
    y_specs = [pl.BlockSpec((tm, PACKED), functools.partial(lambda i, j: (j * per_choice + off + i, 0), j=j))
               for j in range(TOP_K)]
    return pl.pallas_call(
        _combine_kernel,
        grid=(n // tm,),
        in_specs=[pl.BlockSpec((tm, D_MODEL), lambda i: (i, 0)),
                  *y_specs,
                  pl.BlockSpec((tm, LANES), lambda i: (i + off, 0)),
                  pl.BlockSpec((1, 6, D_MODEL), lambda i: (i // per_seq, 0, 0))],
        out_specs=pl.BlockSpec((tm, D_MODEL), lambda i: (i, 0)),
        out_shape=jax.ShapeDtypeStruct((n, D_MODEL), F32),
        compiler_params=_params("arbitrary"),
        name="combine",
    )(x1, ys, ys, ys, ys, tw, ada_g)


def _rope_tables(seq):
    half = HEAD_DIM // 2
    inv = ROPE_THETA ** (-jnp.arange(half, dtype=F32) / half)
    ang = jnp.arange(seq, dtype=F32)[:, None] * inv[None, :]
    cos, sin = jnp.cos(ang), jnp.sin(ang)
    cos_h = jnp.concatenate([cos, cos], axis=-1)
    sin_h = jnp.concatenate([-sin, sin], axis=-1)
    reps = DIFF_WIDTH // HEAD_DIM
    return jnp.tile(cos_h, (1, reps)), jnp.tile(sin_h, (1, reps))


def _na_bias_table(rpb):
    cols = jnp.arange(GRID_W, dtype=jnp.int32)
    c_start = jnp.clip(cols - NA_WIN_COLS // 2, 0, GRID_W - NA_WIN_COLS)
    col_mask = (cols[None, :] >= c_start[:, None]) & (cols[None, :] < c_start[:, None] + NA_WIN_COLS)
    col_idx = jnp.clip(cols[None, :] - cols[:, None], -(NA_WIN_COLS - 1), NA_WIN_COLS - 1) + NA_WIN_COLS - 1
    delta = jnp.arange(NA_WIN_ROWS, dtype=jnp.int32)
    j = jnp.arange(NA_WIN_ROWS, dtype=jnp.int32)
    row_idx = j[None, :] - delta[:, None] + NA_WIN_ROWS - 1
    row_hot = (row_idx[:, :, None] == jnp.arange(2 * NA_WIN_ROWS - 1, dtype=jnp.int32)).astype(F32)
    col_hot = (col_idx[:, :, None] == jnp.arange(2 * NA_WIN_COLS - 1, dtype=jnp.int32)).astype(F32)
    bias = jnp.einsum('djr,hrc,qkc->hdjqk', row_hot, rpb.astype(F32), col_hot,
                      precision=lax.Precision.HIGHEST)
    bias = jnp.where(col_mask[None, None, None], bias * LOG2E, NEG_INF)
    bias = bias.transpose(0, 1, 2, 4, 3).reshape(NA_HEADS, NA_WIN_ROWS, NA_WIN_ROWS * GRID_W, GRID_W)
    bias = bias.reshape(NA_HEADS // 2, 2, NA_WIN_ROWS, NA_WIN_ROWS * GRID_W, GRID_W)
    return jnp.concatenate([bias[:, 0], bias[:, 1]], axis=-1)


def _routing(top_idx, rank, counts, n):
    n_blocks = n * TOP_K // EXPERT_BLOCK + N_EXPERTS
    experts = jnp.arange(N_EXPERTS, dtype=jnp.int32)
    padded = (counts + EXPERT_BLOCK - 1) // EXPERT_BLOCK * EXPERT_BLOCK
    pad_end = jnp.cumsum(padded)
    pad_start = pad_end - padded
    start_of = jnp.sum(jnp.where(top_idx[:, :, None] == experts, pad_start, 0), axis=-1)
    dest = (start_of + rank).T.reshape(-1)
    block_lo = jnp.arange(n_blocks, dtype=jnp.int32) * EXPERT_BLOCK
    block_e = jnp.minimum(jnp.sum((pad_end[None, :] <= block_lo[:, None]).astype(jnp.int32), axis=1),
                          N_EXPERTS - 1).astype(jnp.int32)
    n_active = (pad_end[-1] // EXPERT_BLOCK).astype(jnp.int32).reshape(1)
    return dest, block_e, n_active, n_blocks * EXPERT_BLOCK


def kernel(x_prompt, x_sample, c_prompt, c_sample, w_ada, b_ada, g_attn_norm, w_qkv, na_q_norm, na_k_norm, na_rpb, diff_q_norm, diff_k_norm, lambda_q1, lambda_k1, lambda_q2, lambda_k2, diff_subln, w_o, g_ffn_norm, w_router, b_router, w_gate, b_gate, w_up, b_up, w_down, b_down):
    l = 0
    groups = [(x_prompt, c_prompt), (x_sample, c_sample)]
    nb = [x.shape[0] for x, _ in groups]

    ada_all = _ada(jnp.concatenate([c for _, c in groups], axis=0), w_ada[l], b_ada[l])
    ada_all = ada_all.reshape(sum(nb), 6, D_MODEL)

    w_qkv_bf = w_qkv[l].astype(BF16)
    w_o_bf = w_o[l].astype(BF16)
    scale = HEAD_DIM ** -0.5
    reps = NA_WIDTH // HEAD_DIM
    gains = jnp.stack([jnp.tile(na_q_norm[l], reps) * (scale * LOG2E),
                       jnp.tile(na_k_norm[l], reps),
                       jnp.tile(diff_q_norm[l], reps) * (scale * LOG2E),
                       jnp.tile(diff_k_norm[l], reps)]).astype(F32)
    head_id = jnp.arange(MXU_DIM, dtype=jnp.int32) // HEAD_DIM
    bd = (head_id[:, None] == head_id[None, :]).astype(BF16)
    bias_t = _na_bias_table(na_rpb[l])
    lam_vecs = jnp.stack([lambda_q1[l], lambda_k1[l], lambda_q2[l], lambda_k2[l]]).astype(F32)
    subln_col = (diff_subln[l].astype(F32) * (1.0 - LAMBDA_INIT)).reshape(LANES, 1)
    wr = w_router[l].astype(F32)
    wr_pad = jnp.zeros((D_MODEL, LANES), F32).at[:, :N_EXPERTS].set(wr)
    wr_hi = wr_pad.astype(BF16)
    wr_lo = (wr_pad - wr_hi.astype(F32)).astype(BF16)
    br_pad = jnp.full((1, LANES), NEG_INF, F32).at[0, :N_EXPERTS].set(b_router[l].astype(F32))
    g_attn = g_attn_norm[l].reshape(1, D_MODEL).astype(F32)
    g_ffn = g_ffn_norm[l].reshape(1, D_MODEL).astype(F32)
    max_seq = max(x.shape[1] for x, _ in groups)
    cos_t, sin_t = _rope_tables(max_seq)

    rows = lax.broadcasted_iota(jnp.int32, (ROW_BLOCK, ROW_BLOCK), 0)
    cols = lax.broadcasted_iota(jnp.int32, (ROW_BLOCK, ROW_BLOCK), 1)
    tri = (cols < rows).astype(BF16)
    cnt = jnp.zeros((1, LANES), F32)

    x1s, h2s, tis, tws = [], [], [], []
    b0 = 0
    for (x, _), b in zip(groups, nb):
        seq = x.shape[1]
        x2d = x.reshape(b * seq, D_MODEL)
        ada_g = ada_all[b0:b0 + b]
        b0 += b
        qkv = _qkv(x2d, ada_g, g_attn, w_qkv_bf, gains, cos_t, sin_t, bd, seq)
        o_na = _na(qkv, bias_t, b, seq)
        o_df = _diff(qkv, lam_vecs, subln_col, b, seq)
        x1, h2, ti, tw, cnt = _wo(o_na, o_df, x2d, ada_g, w_o_bf, g_ffn, wr_hi, wr_lo, br_pad, tri, cnt, seq)
        x1s.append(x1)
        h2s.append(h2)
        tis.append(ti)
        tws.append(tw)

    h2_all = jnp.concatenate(h2s, axis=0)
    ti_all = jnp.concatenate(tis, axis=0)
    tw_all = jnp.concatenate(tws, axis=0)
    n = h2_all.shape[0]
    counts = cnt[0, :N_EXPERTS].astype(jnp.int32)
    dest, block_e, n_active, cap = _routing(ti_all[:, :TOP_K], ti_all[:, TOP_K:2 * TOP_K], counts, n)

    xs = _sc_scatter(h2_all, dest, cap)
    ys_sorted = _experts(block_e, n_active, xs,
                         w_gate[l], b_gate[l].reshape(N_EXPERTS, 1, D_MODEL).astype(F32),
                         w_up[l], b_up[l].reshape(N_EXPERTS, 1, D_MODEL).astype(F32),
                         w_down[l], b_down[l].reshape(N_EXPERTS, 1, D_MODEL).astype(F32))
    ys = _sc_gather(ys_sorted, dest)

    outs = []
    b0 = 0
    row_off = 0
    for (x, _), b, x1 in zip(groups, nb, x1s):
        seq = x.shape[1]
        ada_g = ada_all[b0:b0 + b]
        b0 += b
        y = _combine(x1, ys, tw_all, ada_g, seq, row_off)
        row_off += b * seq
        outs.append(y.reshape(b, seq, D_MODEL))
    return tuple(outs)
```

```python
import functools
import math

import jax
import jax.numpy as jnp
from jax import lax
from jax.experimental import pallas as pl
from jax.experimental.pallas import tpu as pltpu
from jax.experimental.pallas import tpu_sc as plsc

F32 = jnp.float32
BF16 = jnp.bfloat16
U32 = jnp.uint32

D_MODEL = 1024
HEAD_DIM = 64
NA_HEADS = 8
NA_WIDTH = 512
DIFF_HEADS = 4
DIFF_WIDTH = 512
QKV_COLS = 3072
GRID_W = 64
NA_WIN_ROWS = 8
NA_WIN_COLS = 16
ROPE_THETA = 10000.0
N_EXPERTS = 32
TOP_K = 4
SWIGLU_LIMIT = 7.0
SWIGLU_ALPHA = 1.702
EPS = 1e-5
NEG_INF = -1e30
LAMBDA_INIT = 0.8 - 0.6 * math.exp(-0.3 * 0)
LOG2E = 1.4426950408889634

LANES = 128
MXU_DIM = 256
VMEM_LIMIT = 56 * 1024 * 1024

ROW_BLOCK = 512
Q_BLOCK = 256
EXPERT_BLOCK = 512
NA_ROWS_PER_TRIP = 4
VT_ROWS = LANES + 16


PACKED = D_MODEL // 2
SC_CORES = 2
SC_SUBCORES = 16
SC_WORKERS = SC_CORES * SC_SUBCORES
GATHER_ROWS = 64


def _params(*sem):
    return pltpu.CompilerParams(dimension_semantics=sem, vmem_limit_bytes=VMEM_LIMIT)


def _pack_halves(x):
    w = x.shape[1] // 2
    bits = lax.bitcast_convert_type(x, U32)
    return (bits[:, :w] >> 16) | bits[:, w:]


def _col_reduce(x, op):
    while x.shape[0] >= 64:
        x = op(x.reshape(8, x.shape[0] // 8, x.shape[1]), axis=0)
    return op(x, axis=0, keepdims=True)


def _unpack_halves(word):
    lo = lax.bitcast_convert_type(word << 16, F32)
    hi = lax.bitcast_convert_type(word & jnp.uint32(0xFFFF0000), F32)
    return lo, hi


def _ada_kernel(c_ref, w_ref, b_ref, o_ref):
    c = c_ref[...]
    s = c * jax.nn.sigmoid(c)
    o_ref[...] = jnp.dot(s, w_ref[...], preferred_element_type=F32,
                         precision=lax.Precision.HIGHEST) + b_ref[...]


def _ada(c_all, w_ada, b_ada):
    nb = c_all.shape[0]
    n_out = w_ada.shape[1]
    blk = D_MODEL
    return pl.pallas_call(
        _ada_kernel,
        grid=(n_out // blk,),
        in_specs=[pl.BlockSpec((nb, D_MODEL), lambda j: (0, 0)),
                  pl.BlockSpec((D_MODEL, blk), lambda j: (0, j)),
                  pl.BlockSpec((1, blk), lambda j: (0, j))],
        out_specs=pl.BlockSpec((nb, blk), lambda j: (0, j)),
        out_shape=jax.ShapeDtypeStruct((nb, n_out), F32),
        compiler_params=_params("arbitrary"),
        name="ada",
    )(c_all, w_ada, b_ada.reshape(1, n_out))


def _head_sumsq(y, bd):
    sq = (y * y).astype(BF16)
    parts = [jnp.dot(sq[:, c:c + MXU_DIM], bd, preferred_element_type=F32)
             for c in range(0, y.shape[1], MXU_DIM)]
    return jnp.concatenate(parts, axis=1)


def _qkv_kernel(x_ref, ada_ref, g_ref, w_ref, gain_ref, cos_ref, sin_ref, bd_ref, o_ref):
    x = x_ref[...]
    ms = jnp.mean(x * x, axis=-1, keepdims=True)
    xn = x * lax.rsqrt(ms + EPS) * g_ref[...]
    sh = ada_ref[0, 0:1, :]
    sc = ada_ref[0, 1:2, :]
    h = (xn * (1.0 + sc) + sh).astype(BF16)
    bd = bd_ref[...]
    lane = lax.broadcasted_iota(jnp.int32, (x.shape[0], NA_WIDTH), 1)
    first_half = (lane & (HEAD_DIM // 2)) == 0
    for grp in range(6):
        cols = slice(grp * 512, (grp + 1) * 512)
        acc = jnp.dot(h, w_ref[:, cols], preferred_element_type=F32)
        if grp in (2, 5):
            o_ref[:, cols] = acc.astype(BF16)
            continue
        gi = {0: 0, 1: 1, 3: 2, 4: 3}[grp]
        ss = _head_sumsq(acc, bd)
        y = acc * lax.rsqrt(ss * (1.0 / HEAD_DIM) + EPS) * gain_ref[gi:gi + 1, :]
        if grp in (3, 4):
            partner = jnp.where(first_half,
                                pltpu.roll(y, NA_WIDTH - HEAD_DIM // 2, axis=1),
                                pltpu.roll(y, HEAD_DIM // 2, axis=1))
            y = y * cos_ref[...] + partner * sin_ref[...]
        o_ref[:, cols] = y.astype(BF16)


def _qkv(x2d, ada_g, g_attn, w_qkv_bf, gains, cos_t, sin_t, bd, seq):
    n = x2d.shape[0]
    tm = ROW_BLOCK
    per_seq = seq // tm
    return pl.pallas_call(
        _qkv_kernel,
        grid=(n // tm,),
        in_specs=[pl.BlockSpec((tm, D_MODEL), lambda i: (i, 0)),
                  pl.BlockSpec((1, 6, D_MODEL), lambda i: (i // per_seq, 0, 0)),
                  pl.BlockSpec((1, D_MODEL), lambda i: (0, 0)),
                  pl.BlockSpec((D_MODEL, QKV_COLS), lambda i: (0, 0)),
                  pl.BlockSpec((4, NA_WIDTH), lambda i: (0, 0)),
                  pl.BlockSpec((tm, DIFF_WIDTH), lambda i: (i % per_seq, 0)),
                  pl.BlockSpec((tm, DIFF_WIDTH), lambda i: (i % per_seq, 0)),
                  pl.BlockSpec((MXU_DIM, MXU_DIM), lambda i: (0, 0))],
        out_specs=pl.BlockSpec((tm, QKV_COLS), lambda i: (i, 0)),
        out_shape=jax.ShapeDtypeStruct((n, QKV_COLS), BF16),
        compiler_params=_params("arbitrary"),
        name="qkv",
    )(x2d, ada_g, g_attn, w_qkv_bf, gains, cos_t, sin_t, bd)


def _na_kernel(q_ref, k_ref, v_ref, bias_ref, o_ref, *, rows):
    lane = lax.broadcasted_iota(jnp.int32, (GRID_W, LANES), 1)
    head0 = lane < HEAD_DIM
    win = NA_WIN_ROWS * GRID_W

    def one_row(r):
        r_start = jnp.clip(r - NA_WIN_ROWS // 2, 0, rows - NA_WIN_ROWS)
        delta = r - r_start
        q = q_ref[pl.ds(pl.multiple_of(r * GRID_W, GRID_W), GRID_W), :]
        k0 = pl.multiple_of(r_start * GRID_W, GRID_W)
        kw = k_ref[pl.ds(k0, win), :]
        vw = v_ref[pl.ds(k0, win), :]
        zero = jnp.zeros_like(q)
        qm = jnp.concatenate([jnp.where(head0, q, zero), jnp.where(head0, zero, q)], axis=0)
        s = lax.dot_general(kw, qm, (((1,), (1,)), ((), ())), preferred_element_type=F32)
        s = s + bias_ref[0, delta]
        m = _col_reduce(s, jnp.max)
        p = jnp.exp2(s - m)
        l = _col_reduce(p, jnp.sum)
        p = (p * (1.0 / l)).astype(BF16)
        o2 = lax.dot_general(p, vw, (((0,), (0,)), ((), ())), preferred_element_type=F32)
        o = jnp.where(head0, o2[:GRID_W], o2[GRID_W:])
        o_ref[pl.ds(pl.multiple_of(r * GRID_W, GRID_W), GRID_W), :] = o.astype(BF16)

    def body(i, carry):
        for u in range(NA_ROWS_PER_TRIP):
            one_row(i * NA_ROWS_PER_TRIP + u)
        return carry

    lax.fori_loop(0, rows // NA_ROWS_PER_TRIP, body, 0)


def _na(qkv, bias_t, batch, seq):
    rows = seq // GRID_W
    n_pairs = NA_HEADS // 2
    return pl.pallas_call(
        functools.partial(_na_kernel, rows=rows),
        grid=(batch, n_pairs),
        in_specs=[pl.BlockSpec((seq, LANES), lambda b, hp: (b, hp)),
                  pl.BlockSpec((seq, LANES), lambda b, hp: (b, n_pairs + hp)),
                  pl.BlockSpec((seq, LANES), lambda b, hp: (b, 2 * n_pairs + hp)),
                  pl.BlockSpec((1, NA_WIN_ROWS, NA_WIN_ROWS * GRID_W, LANES), lambda b, hp: (hp, 0, 0, 0))],
        out_specs=pl.BlockSpec((seq, LANES), lambda b, hp: (b, hp)),
        out_shape=jax.ShapeDtypeStruct((batch * seq, NA_WIDTH), BF16),
        compiler_params=_params("arbitrary", "arbitrary"),
        name="na_attn",
    )(qkv, qkv, qkv, bias_t)


def _diff_kernel(q_ref, k_ref, v_ref, lam_ref, g_ref, o_ref, vt_ref):
    @pl.when(pl.program_id(2) == 0)
    def _():
        vt_ref[:LANES, :] = v_ref[...].astype(F32).T.astype(BF16)
        ones_row = lax.broadcasted_iota(jnp.int32, (VT_ROWS - LANES, v_ref.shape[0]), 0) == 0
        vt_ref[LANES:, :] = jnp.where(ones_row, 1.0, 0.0).astype(BF16)

    lq1 = lam_ref[0:1, :]
    lk1 = lam_ref[1:2, :]
    lq2 = lam_ref[2:3, :]
    lk2 = lam_ref[3:4, :]
    lam = (jnp.exp(jnp.sum(lq1 * lk1, axis=-1, keepdims=True))
           - jnp.exp(jnp.sum(lq2 * lk2, axis=-1, keepdims=True)) + LAMBDA_INIT)

    q = q_ref[...]
    lane = lax.broadcasted_iota(jnp.int32, q.shape, 1)
    zero = jnp.zeros_like(q)
    qc = [jnp.where(lane < HEAD_DIM, q, zero), jnp.where(lane < HEAD_DIM, zero, q)]
    k = k_ref[...]
    vt = vt_ref[...]
    seq, tq = k.shape[0], q.shape[0]
    outs = []
    for comp in range(2):
        s = lax.dot_general(k, qc[comp], (((1,), (1,)), ((), ())), preferred_element_type=F32)
        m = _col_reduce(s, jnp.max)
        p = jnp.exp2((s - m).astype(BF16))
        oa = jnp.dot(vt, p, preferred_element_type=F32)
        outs.append((oa[:LANES], oa[LANES:LANES + 1]))
    (o0, l0), (o1, l1) = outs
    o = o0 * (1.0 / l0) - (lam / l1) * o1
    ms = jnp.mean(o * o, axis=0, keepdims=True)
    y = o * lax.rsqrt(ms + EPS) * g_ref[...]
    o_ref[...] = y.T.astype(BF16)


def _diff(qkv, lam_vecs, subln_col, batch, seq):
    tq = Q_BLOCK
    nq = seq // tq
    base = 3 * NA_WIDTH // LANES
    nh = DIFF_HEADS
    return pl.pallas_call(
        _diff_kernel,
        grid=(batch, nh, nq),
        in_specs=[pl.BlockSpec((tq, LANES), lambda b, h, i: (b * nq + i, base + h)),
                  pl.BlockSpec((seq, LANES), lambda b, h, i: (b, base + nh + h)),
                  pl.BlockSpec((seq, LANES), lambda b, h, i: (b, base + 2 * nh + h)),
                  pl.BlockSpec((4, HEAD_DIM), lambda b, h, i: (0, 0)),
                  pl.BlockSpec((LANES, 1), lambda b, h, i: (0, 0))],
        out_specs=pl.BlockSpec((tq, LANES), lambda b, h, i: (b * nq + i, h)),
        out_shape=jax.ShapeDtypeStruct((batch * seq, DIFF_WIDTH), BF16),
        scratch_shapes=[pltpu.VMEM((VT_ROWS, seq), BF16)],
        compiler_params=_params("arbitrary", "arbitrary", "arbitrary"),
        name="diff_attn",
    )(qkv, qkv, qkv, lam_vecs, subln_col)


def _wo_kernel(ona_ref, odf_ref, x_ref, ada_ref, wo_ref, g_ref, wrh_ref, wrl_ref, br_ref, tri_ref, cnt0_ref,
               x1_ref, h2_ref, ti_ref, tw_ref, cnt_ref):
    mix = (jnp.dot(ona_ref[...], wo_ref[:NA_WIDTH, :], preferred_element_type=F32)
           + jnp.dot(odf_ref[...], wo_ref[NA_WIDTH:, :], preferred_element_type=F32))
    gt1 = ada_ref[0, 2:3, :]
    sh2 = ada_ref[0, 3:4, :]
    sc2 = ada_ref[0, 4:5, :]
    x1 = x_ref[...] + gt1 * mix
    x1_ref[...] = x1
    ms = jnp.mean(x1 * x1, axis=-1, keepdims=True)
    h2 = x1 * lax.rsqrt(ms + EPS) * g_ref[...] * (1.0 + sc2) + sh2
    hi = h2.astype(BF16)
    h2_ref[...] = _pack_halves(hi.astype(F32))
    lo = (h2 - hi.astype(F32)).astype(BF16)
    logits = (jnp.dot(hi, wrh_ref[...], preferred_element_type=F32)
              + jnp.dot(hi, wrl_ref[...], preferred_element_type=F32)
              + jnp.dot(lo, wrh_ref[...], preferred_element_type=F32)) + br_ref[...]
    lane = lax.broadcasted_iota(jnp.int32, logits.shape, 1).astype(F32)
    vals = []
    idxs = []
    cur = logits
    for _ in range(TOP_K):
        m = jnp.max(cur, axis=-1, keepdims=True)
        idx = jnp.min(jnp.where(cur == m, lane, float(LANES)), axis=-1, keepdims=True)
        vals.append(m)
        idxs.append(idx)
        cur = jnp.where(lane == idx, -jnp.inf, cur)
    es = [jnp.exp(v - vals[0]) for v in vals]
    inv = 1.0 / (es[0] + es[1] + es[2] + es[3])

    @pl.when(pl.program_id(0) == 0)
    def _():
        cnt_ref[...] = cnt0_ref[...]

    sel = jnp.zeros(logits.shape, F32)
    for j in range(TOP_K):
        sel = sel + jnp.where(lane == idxs[j], 1.0, 0.0)
    before = jnp.dot(tri_ref[...], sel.astype(BF16), preferred_element_type=F32) + cnt_ref[...]
    cnt_ref[...] = cnt_ref[...] + jnp.sum(sel, axis=0, keepdims=True)
    ranks = [jnp.sum(jnp.where(lane == idxs[j], before, 0.0), axis=-1, keepdims=True) for j in range(TOP_K)]

    ti = jnp.zeros(logits.shape, F32)
    tw = jnp.zeros(logits.shape, F32)
    for j in range(TOP_K):
        ti = jnp.where(lane == float(j), idxs[j], ti)
        ti = jnp.where(lane == float(TOP_K + j), ranks[j], ti)
        tw = jnp.where(lane == float(j), es[j] * inv, tw)
    ti_ref[...] = ti.astype(jnp.int32)
    tw_ref[...] = tw


def _wo(o_na, o_df, x2d, ada_g, w_o_bf, g_ffn, wr_hi, wr_lo, br_pad, tri, cnt0, seq):
    n = x2d.shape[0]
    tm = ROW_BLOCK
    per_seq = seq // tm
    row = lambda i: (i, 0)
    const = lambda i: (0, 0)
    return pl.pallas_call(
        _wo_kernel,
        grid=(n // tm,),
        in_specs=[pl.BlockSpec((tm, NA_WIDTH), row),
                  pl.BlockSpec((tm, DIFF_WIDTH), row),
                  pl.BlockSpec((tm, D_MODEL), row),
                  pl.BlockSpec((1, 6, D_MODEL), lambda i: (i // per_seq, 0, 0)),
                  pl.BlockSpec((D_MODEL, D_MODEL), const),
                  pl.BlockSpec((1, D_MODEL), const),
                  pl.BlockSpec((D_MODEL, LANES), const),
                  pl.BlockSpec((D_MODEL, LANES), const),
                  pl.BlockSpec((1, LANES), const),
                  pl.BlockSpec((tm, tm), const),
                  pl.BlockSpec((1, LANES), const)],
        out_specs=[pl.BlockSpec((tm, D_MODEL), row),
                   pl.BlockSpec((tm, PACKED), row),
                   pl.BlockSpec((tm, LANES), row),
                   pl.BlockSpec((tm, LANES), row),
                   pl.BlockSpec((1, LANES), const)],
        out_shape=[jax.ShapeDtypeStruct((n, D_MODEL), F32),
                   jax.ShapeDtypeStruct((n, PACKED), U32),
                   jax.ShapeDtypeStruct((n, LANES), jnp.int32),
                   jax.ShapeDtypeStruct((n, LANES), F32),
                   jax.ShapeDtypeStruct((1, LANES), F32)],
        compiler_params=_params("arbitrary"),
        name="wo_router",
    )(o_na, o_df, x2d, ada_g, w_o_bf, g_ffn, wr_hi, wr_lo, br_pad, tri, cnt0)


def _expert_kernel(be_ref, na_ref, xs_ref, wg_ref, bg_ref, wu_ref, bu_ref, wd_ref, bd_ref, o_ref,
                   wg_bf, wu_bf, wd_bf):
    i = pl.program_id(0)
    active = i < na_ref[0]
    new_expert = jnp.logical_or(i == 0, be_ref[i] != be_ref[jnp.maximum(i - 1, 0)])

    @pl.when(jnp.logical_and(active, new_expert))
    def _():
        wg_bf[...] = wg_ref[0].astype(BF16)
        wu_bf[...] = wu_ref[0].astype(BF16)
        wd_bf[...] = wd_ref[0].astype(BF16)

    @pl.when(active)
    def _():
        x_lo, x_hi = _unpack_halves(xs_ref[...])
        x_lo = x_lo.astype(BF16)
        x_hi = x_hi.astype(BF16)

        def proj(w_bf):
            return (jnp.dot(x_lo, w_bf[:PACKED, :], preferred_element_type=F32)
                    + jnp.dot(x_hi, w_bf[PACKED:, :], preferred_element_type=F32))

        g = jnp.minimum(proj(wg_bf) + bg_ref[0], SWIGLU_LIMIT)
        u = jnp.clip(proj(wu_bf) + bu_ref[0], -SWIGLU_LIMIT, SWIGLU_LIMIT)
        act = g * jax.nn.sigmoid(SWIGLU_ALPHA * g) * (u + 1.0)
        out = jnp.dot(act.astype(BF16), wd_bf[...], preferred_element_type=F32) + bd_ref[0]
        o_ref[...] = _pack_halves(out.astype(BF16).astype(F32))

    @pl.when(i >= na_ref[0])
    def _():
        o_ref[...] = jnp.zeros_like(o_ref)


def _experts(block_e, n_active, xs, wg, bg, wu, bu, wd, bd):
    cap = xs.shape[0]
    n_blocks = cap // EXPERT_BLOCK
    xmap = lambda i, be, na: (jnp.minimum(i, na[0] - 1), 0)
    wmap = lambda i, be, na: (be[i], 0, 0)
    grid_spec = pltpu.PrefetchScalarGridSpec(
        num_scalar_prefetch=2,
        grid=(n_blocks,),
        in_specs=[pl.BlockSpec((EXPERT_BLOCK, PACKED), xmap),
                  pl.BlockSpec((1, D_MODEL, D_MODEL), wmap),
                  pl.BlockSpec((1, 1, D_MODEL), wmap),
                  pl.BlockSpec((1, D_MODEL, D_MODEL), wmap),
                  pl.BlockSpec((1, 1, D_MODEL), wmap),
                  pl.BlockSpec((1, D_MODEL, D_MODEL), wmap),
                  pl.BlockSpec((1, 1, D_MODEL), wmap)],
        out_specs=pl.BlockSpec((EXPERT_BLOCK, PACKED), lambda i, be, na: (i, 0)),
        scratch_shapes=[pltpu.VMEM((D_MODEL, D_MODEL), BF16)] * 3,
    )
    return pl.pallas_call(
        _expert_kernel,
        grid_spec=grid_spec,
        out_shape=jax.ShapeDtypeStruct((cap, PACKED), U32),
        compiler_params=_params("arbitrary"),
        name="experts",
    )(block_e, n_active, xs, wg, bg, wu, bu, wd, bd)


def _sc_gather(table, idx):
    n_out = idx.shape[0]
    width = table.shape[1]
    per_worker = n_out // SC_WORKERS
    n_chunks = per_worker // GATHER_ROWS
    assert per_worker * SC_WORKERS == n_out and n_chunks * GATHER_ROWS == per_worker and n_chunks % 2 == 0
    idx3 = idx.reshape(SC_WORKERS, n_chunks, GATHER_ROWS)
    mesh = plsc.VectorSubcoreMesh(core_axis_name="core", subcore_axis_name="subcore")

    @functools.partial(
        pl.kernel, mesh=mesh,
        out_type=jax.ShapeDtypeStruct((n_out, width), table.dtype),
        scratch_types=[pltpu.VMEM((n_chunks, GATHER_ROWS), jnp.int32),
                       pltpu.VMEM((2, GATHER_ROWS, width), table.dtype),
                       pltpu.SemaphoreType.DMA((2,)),
                       pltpu.SemaphoreType.DMA((2,))])
    def gather_kernel(table_hbm, idx_hbm, out_hbm, idx_v, rows_v, gsem, wsem):
        wid = lax.axis_index("subcore") * SC_CORES + lax.axis_index("core")
        base = wid * per_worker
        pltpu.sync_copy(idx_hbm.at[wid], idx_v)

        def gather(j, slot):
            return pltpu.make_async_copy(table_hbm.at[idx_v.at[j]], rows_v.at[slot], gsem.at[slot])

        def write(j, slot):
            dst = out_hbm.at[pl.ds(pl.multiple_of(base + j * GATHER_ROWS, GATHER_ROWS), GATHER_ROWS)]
            return pltpu.make_async_copy(rows_v.at[slot], dst, wsem.at[slot])

        gather(0, 0).start()

        @pl.loop(0, n_chunks, step=2)
        def _(j):
            for slot in range(2):
                jj = j + slot
                gather(jj, slot).wait()

                @pl.when(jj >= 1)
                def _():
                    write(jj - 1, 1 - slot).wait()

                @pl.when(jj + 1 < n_chunks)
                def _():
                    gather(jj + 1, 1 - slot).start()

                write(jj, slot).start()

        write(n_chunks - 1, 1).wait()

    return gather_kernel(table, idx3)


def _sc_scatter(rows, idx, n_out):
    n_src, width = rows.shape
    n_idx = idx.shape[0]
    per_worker = n_idx // SC_WORKERS
    n_chunks = per_worker // GATHER_ROWS
    assert per_worker * SC_WORKERS == n_idx and n_chunks * GATHER_ROWS == per_worker and n_chunks % 2 == 0
    assert n_src % per_worker == 0
    idx3 = idx.reshape(SC_WORKERS, n_chunks, GATHER_ROWS)
    mesh = plsc.VectorSubcoreMesh(core_axis_name="core", subcore_axis_name="subcore")

    @functools.partial(
        pl.kernel, mesh=mesh,
        out_type=jax.ShapeDtypeStruct((n_out, width), rows.dtype),
        scratch_types=[pltpu.VMEM((n_chunks, GATHER_ROWS), jnp.int32),
                       pltpu.VMEM((2, GATHER_ROWS, width), rows.dtype),
                       pltpu.SemaphoreType.DMA((2,)),
                       pltpu.SemaphoreType.DMA((2,))])
    def scatter_kernel(rows_hbm, idx_hbm, out_hbm, idx_v, rows_v, rsem, wsem):
        wid = lax.axis_index("subcore") * SC_CORES + lax.axis_index("core")
        base = lax.rem(wid * per_worker, n_src)
        pltpu.sync_copy(idx_hbm.at[wid], idx_v)

        def read(j, slot):
            src = rows_hbm.at[pl.ds(pl.multiple_of(base + j * GATHER_ROWS, GATHER_ROWS), GATHER_ROWS)]
            return pltpu.make_async_copy(src, rows_v.at[slot], rsem.at[slot])

        def write(j, slot):
            return pltpu.make_async_copy(rows_v.at[slot], out_hbm.at[idx_v.at[j]], wsem.at[slot])

        read(0, 0).start()

        @pl.loop(0, n_chunks, step=2)
        def _(j):
            for slot in range(2):
                jj = j + slot
                read(jj, slot).wait()

                @pl.when(jj >= 1)
                def _():
                    write(jj - 1, 1 - slot).wait()

                @pl.when(jj + 1 < n_chunks)
                def _():
                    read(jj + 1, 1 - slot).start()

                write(jj, slot).start()

        write(n_chunks - 1, 1).wait()

    return scatter_kernel(rows, idx3)


def _combine_kernel(x1_ref, y0_ref, y1_ref, y2_ref, y3_ref, tw_ref, ada_ref, o_ref):
    tw = tw_ref[...]
    acc_lo = jnp.zeros((x1_ref.shape[0], PACKED), F32)
    acc_hi = jnp.zeros((x1_ref.shape[0], PACKED), F32)
    for j, y_ref in enumerate((y0_ref, y1_ref, y2_ref, y3_ref)):
        lo, hi = _unpack_halves(y_ref[...])
        acc_lo = acc_lo + tw[:, j:j + 1] * lo
        acc_hi = acc_hi + tw[:, j:j + 1] * hi
    o_ref[:, :PACKED] = x1_ref[:, :PACKED] + ada_ref[0, 5:6, :PACKED] * acc_lo
    o_ref[:, PACKED:] = x1_ref[:, PACKED:] + ada_ref[0, 5:6, PACKED:] * acc_hi


def _combine(x1, ys, tw, ada_g, seq, row_off):
    n = x1.shape[0]
    tm = ROW_BLOCK
    per_seq = seq // tm
    off = row_off // tm
    per_choice = tw.shape[0] // tm
    y_specs = [pl.BlockSpec((tm, PACKED), functools.partial(lambda i, j: (j * per_choice + off + i, 0), j=j))
               for j in range(TOP_K)]
    return pl.pallas_call(
        _combine_kernel,
        grid=(n // tm,),
        in_specs=[pl.BlockSpec((tm, D_MODEL), lambda i: (i, 0)),
                  *y_specs,
                  pl.BlockSpec((tm, LANES), lambda i: (i + off, 0)),
                  pl.BlockSpec((1, 6, D_MODEL), lambda i: (i // per_seq, 0, 0))],
        out_specs=pl.BlockSpec((tm, D_MODEL), lambda i: (i, 0)),
        out_shape=jax.ShapeDtypeStruct((n, D_MODEL), F32),
        compiler_params=_params("arbitrary"),
        name="combine",
    )(x1, ys, ys, ys, ys, tw, ada_g)


def _rope_tables(seq):
    half = HEAD_DIM // 2
    inv = ROPE_THETA ** (-jnp.arange(half, dtype=F32) / half)
    ang = jnp.arange(seq, dtype=F32)[:, None] * inv[None, :]
    cos, sin = jnp.cos(ang), jnp.sin(ang)
    cos_h = jnp.concatenate([cos, cos], axis=-1)
    sin_h = jnp.concatenate([-sin, sin], axis=-1)
    reps = DIFF_WIDTH // HEAD_DIM
    return jnp.tile(cos_h, (1, reps)), jnp.tile(sin_h, (1, reps))


def _na_bias_table(rpb):
    cols = jnp.arange(GRID_W, dtype=jnp.int32)
    c_start = jnp.clip(cols - NA_WIN_COLS // 2, 0, GRID_W - NA_WIN_COLS)
    col_mask = (cols[None, :] >= c_start[:, None]) & (cols[None, :] < c_start[:, None] + NA_WIN_COLS)
    col_idx = jnp.clip(cols[None, :] - cols[:, None], -(NA_WIN_COLS - 1), NA_WIN_COLS - 1) + NA_WIN_COLS - 1
    delta = jnp.arange(NA_WIN_ROWS, dtype=jnp.int32)
    j = jnp.arange(NA_WIN_ROWS, dtype=jnp.int32)
    row_idx = j[None, :] - delta[:, None] + NA_WIN_ROWS - 1
    row_hot = (row_idx[:, :, None] == jnp.arange(2 * NA_WIN_ROWS - 1, dtype=jnp.int32)).astype(F32)
    col_hot = (col_idx[:, :, None] == jnp.arange(2 * NA_WIN_COLS - 1, dtype=jnp.int32)).astype(F32)
    bias = jnp.einsum('djr,hrc,qkc->hdjqk', row_hot, rpb.astype(F32), col_hot,
                      precision=lax.Precision.HIGHEST)
    bias = jnp.where(col_mask[None, None, None], bias * LOG2E, NEG_INF)
    bias = bias.transpose(0, 1, 2, 4, 3).reshape(NA_HEADS, NA_WIN_ROWS, NA_WIN_ROWS * GRID_W, GRID_W)
    bias = bias.reshape(NA_HEADS // 2, 2, NA_WIN_ROWS, NA_WIN_ROWS * GRID_W, GRID_W)
    return jnp.concatenate([bias[:, 0], bias[:, 1]], axis=-1)


def _routing(top_idx, rank, counts, n):
    n_blocks = n * TOP_K // EXPERT_BLOCK + N_EXPERTS
    experts = jnp.arange(N_EXPERTS, dtype=jnp.int32)
    padded = (counts + EXPERT_BLOCK - 1) // EXPERT_BLOCK * EXPERT_BLOCK
    pad_end = jnp.cumsum(padded)
    pad_start = pad_end - padded
    start_of = jnp.sum(jnp.where(top_idx[:, :, None] == experts, pad_start, 0), axis=-1)
    dest = (start_of + rank).T.reshape(-1)
    block_lo = jnp.arange(n_blocks, dtype=jnp.int32) * EXPERT_BLOCK
    block_e = jnp.minimum(jnp.sum((pad_end[None, :] <= block_lo[:, None]).astype(jnp.int32), axis=1),
                          N_EXPERTS - 1).astype(jnp.int32)
    n_active = (pad_end[-1] // EXPERT_BLOCK).astype(jnp.int32).reshape(1)
    return dest, block_e, n_active, n_blocks * EXPERT_BLOCK


def kernel(x_prompt, x_sample, c_prompt, c_sample, w_ada, b_ada, g_attn_norm, w_qkv, na_q_norm, na_k_norm, na_rpb, diff_q_norm, diff_k_norm, lambda_q1, lambda_k1, lambda_q2, lambda_k2, diff_subln, w_o, g_ffn_norm, w_router, b_router, w_gate, b_gate, w_up, b_up, w_down, b_down):
    l = 0
    groups = [(x_prompt, c_prompt), (x_sample, c_sample)]
    nb = [x.shape[0] for x, _ in groups]

    ada_all = _ada(jnp.concatenate([c for _, c in groups], axis=0), w_ada[l], b_ada[l])
    ada_all = ada_all.reshape(sum(nb), 6, D_MODEL)

    w_qkv_bf = w_qkv[l].astype(BF16)
    w_o_bf = w_o[l].astype(BF16)
    scale = HEAD_DIM ** -0.5
    reps = NA_WIDTH // HEAD_DIM
    gains = jnp.stack([jnp.tile(na_q_norm[l], reps) * (scale * LOG2E),
                       jnp.tile(na_k_norm[l], reps),
                       jnp.tile(diff_q_norm[l], reps) * (scale * LOG2E),
                       jnp.tile(diff_k_norm[l], reps)]).astype(F32)
    head_id = jnp.arange(MXU_DIM, dtype=jnp.int32) // HEAD_DIM
    bd = (head_id[:, None] == head_id[None, :]).astype(BF16)
    bias_t = _na_bias_table(na_rpb[l])
    lam_vecs = jnp.stack([lambda_q1[l], lambda_k1[l], lambda_q2[l], lambda_k2[l]]).astype(F32)
    subln_col = (diff_subln[l].astype(F32) * (1.0 - LAMBDA_INIT)).reshape(LANES, 1)
    wr = w_router[l].astype(F32)
    wr_pad = jnp.zeros((D_MODEL, LANES), F32).at[:, :N_EXPERTS].set(wr)
    wr_hi = wr_pad.astype(BF16)
    wr_lo = (wr_pad - wr_hi.astype(F32)).astype(BF16)
    br_pad = jnp.full((1, LANES), NEG_INF, F32).at[0, :N_EXPERTS].set(b_router[l].astype(F32))
    g_attn = g_attn_norm[l].reshape(1, D_MODEL).astype(F32)
    g_ffn = g_ffn_norm[l].reshape(1, D_MODEL).astype(F32)
    max_seq = max(x.shape[1] for x, _ in groups)
    cos_t, sin_t = _rope_tables(max_seq)

    rows = lax.broadcasted_iota(jnp.int32, (ROW_BLOCK, ROW_BLOCK), 0)
    cols = lax.broadcasted_iota(jnp.int32, (ROW_BLOCK, ROW_BLOCK), 1)
    tri = (cols < rows).astype(BF16)
    cnt = jnp.zeros((1, LANES), F32)

    x1s, h2s, tis, tws = [], [], [], []
    b0 = 0
    for (x, _), b in zip(groups, nb):
        seq = x.shape[1]
        x2d = x.reshape(b * seq, D_MODEL)
        ada_g = ada_all[b0:b0 + b]
        b0 += b
        qkv = _qkv(x2d, ada_g, g_attn, w_qkv_bf, gains, cos_t, sin_t, bd, seq)
        o_na = _na(qkv, bias_t, b, seq)
        o_df = _diff(qkv, lam_vecs, subln_col, b, seq)
        x1, h2, ti, tw, cnt = _wo(o_na, o_df, x2d, ada_g, w_o_bf, g_ffn, wr_hi, wr_lo, br_pad, tri, cnt, seq)
        x1s.append(x1)
        h2s.append(h2)
        tis.append(ti)
        tws.append(tw)

    h2_all = jnp.concatenate(h2s, axis=0)
    ti_all = jnp.concatenate(tis, axis=0)
    tw_all = jnp.concatenate(tws, axis=0)
    n = h2_all.shape[0]
    counts = cnt[0, :N_EXPERTS].astype(jnp.int32)
    dest, block_e, n_active, cap = _routing(ti_all[:, :TOP_K], ti_all[:, TOP_K:2 * TOP_K], counts, n)

    xs = _sc_scatter(h2_all, dest, cap)
    ys_sorted = _experts(block_e, n_active, xs,
                         w_gate[l], b_gate[l].reshape(N_EXPERTS, 1, D_MODEL).astype(F32),
                         w_up[l], b_up[l].reshape(N_EXPERTS, 1, D_MODEL).astype(F32),
                         w_down[l], b_down[l].reshape(N_EXPERTS, 1, D_MODEL).astype(F32))
    ys = _sc_gather(ys_sorted, dest)

    outs = []
    b0 = 0
    row_off = 0
    for (x, _), b, x1 in zip(groups, nb, x1s):
        seq = x.shape[1]
        ada_g = ada_all[b0:b0 + b]
        b0 += b
        y = _combine(x1, ys, tw_all, ada_g, seq, row_off)
        row_off += b * seq
        outs.append(y.reshape(b, seq, D_MODEL))
    return tuple(outs)
```

```python
import functools
import math

import jax
import jax.numpy as jnp
from jax import lax
from jax.experimental import pallas as pl
from jax.experimental.pallas import tpu as pltpu
from jax.experimental.pallas import tpu_sc as plsc

F32 = jnp.float32
BF16 = jnp.bfloat16
U32 = jnp.uint32

D_MODEL = 1024
HEAD_DIM = 64
NA_HEADS = 8
NA_WIDTH = 512
DIFF_HEADS = 4
DIFF_WIDTH = 512
QKV_COLS = 3072
GRID_W = 64
NA_WIN_ROWS = 8
NA_WIN_COLS = 16
ROPE_THETA = 10000.0
N_EXPERTS = 32
TOP_K = 4
SWIGLU_LIMIT = 7.0
SWIGLU_ALPHA = 1.702
EPS = 1e-5
NEG_INF = -1e30
LAMBDA_INIT = 0.8 - 0.6 * math.exp(-0.3 * 0)
LOG2E = 1.4426950408889634

LANES = 128
MXU_DIM = 256
VMEM_LIMIT = 56 * 1024 * 1024

ROW_BLOCK = 512
Q_BLOCK = 256
EXPERT_BLOCK = 512
NA_ROWS_PER_TRIP = 4
VT_ROWS = LANES + 16


PACKED = D_MODEL // 2
SC_CORES = 2
SC_SUBCORES = 16
SC_WORKERS = SC_CORES * SC_SUBCORES
GATHER_ROWS = 64


def _params(*sem):
    return pltpu.CompilerParams(dimension_semantics=sem, vmem_limit_bytes=VMEM_LIMIT)


def _pack_halves(x):
    w = x.shape[1] // 2
    bits = lax.bitcast_convert_type(x, U32)
    return (bits[:, :w] >> 16) | bits[:, w:]


def _col_reduce(x, op):
    while x.shape[0] >= 64:
        x = op(x.reshape(8, x.shape[0] // 8, x.shape[1]), axis=0)
    return op(x, axis=0, keepdims=True)


def _unpack_halves(word):
    lo = lax.bitcast_convert_type(word << 16, F32)
    hi = lax.bitcast_convert_type(word & jnp.uint32(0xFFFF0000), F32)
    return lo, hi


def _ada_kernel(c_ref, w_ref, b_ref, o_ref):
    c = c_ref[...]
    s = c * jax.nn.sigmoid(c)
    o_ref[...] = jnp.dot(s, w_ref[...], preferred_element_type=F32,
                         precision=lax.Precision.HIGHEST) + b_ref[...]


def _ada(c_all, w_ada, b_ada):
    nb = c_all.shape[0]
    n_out = w_ada.shape[1]
    blk = D_MODEL
    return pl.pallas_call(
        _ada_kernel,
        grid=(n_out // blk,),
        in_specs=[pl.BlockSpec((nb, D_MODEL), lambda j: (0, 0)),
                  pl.BlockSpec((D_MODEL, blk), lambda j: (0, j)),
                  pl.BlockSpec((1, blk), lambda j: (0, j))],
        out_specs=pl.BlockSpec((nb, blk), lambda j: (0, j)),
        out_shape=jax.ShapeDtypeStruct((nb, n_out), F32),
        compiler_params=_params("arbitrary"),
        name="ada",
    )(c_all, w_ada, b_ada.reshape(1, n_out))


def _head_sumsq(y, bd):
    sq = (y * y).astype(BF16)
    parts = [jnp.dot(sq[:, c:c + MXU_DIM], bd, preferred_element_type=F32)
             for c in range(0, y.shape[1], MXU_DIM)]
    return jnp.concatenate(parts, axis=1)


def _qkv_kernel(x_ref, ada_ref, g_ref, w_ref, gain_ref, cos_ref, sin_ref, bd_ref, o_ref):
    x = x_ref[...]
    ms = jnp.mean(x * x, axis=-1, keepdims=True)
    xn = x * lax.rsqrt(ms + EPS) * g_ref[...]
    sh = ada_ref[0, 0:1, :]
    sc = ada_ref[0, 1:2, :]
    h = (xn * (1.0 + sc) + sh).astype(BF16)
    bd = bd_ref[...]
    lane = lax.broadcasted_iota(jnp.int32, (x.shape[0], NA_WIDTH), 1)
    first_half = (lane & (HEAD_DIM // 2)) == 0
    for grp in range(6):
        cols = slice(grp * 512, (grp + 1) * 512)
        acc = jnp.dot(h, w_ref[:, cols], preferred_element_type=F32)
        if grp in (2, 5):
            o_ref[:, cols] = acc.astype(BF16)
            continue
        gi = {0: 0, 1: 1, 3: 2, 4: 3}[grp]
        ss = _head_sumsq(acc, bd)
        y = acc * lax.rsqrt(ss * (1.0 / HEAD_DIM) + EPS) * gain_ref[gi:gi + 1, :]
        if grp in (3, 4):
            partner = jnp.where(first_half,
                                pltpu.roll(y, NA_WIDTH - HEAD_DIM // 2, axis=1),
                                pltpu.roll(y, HEAD_DIM // 2, axis=1))
            y = y * cos_ref[...] + partner * sin_ref[...]
        o_ref[:, cols] = y.astype(BF16)


def _qkv(x2d, ada_g, g_attn, w_qkv_bf, gains, cos_t, sin_t, bd, seq):
    n = x2d.shape[0]
    tm = ROW_BLOCK
    per_seq = seq // tm
    return pl.pallas_call(
        _qkv_kernel,
        grid=(n // tm,),
        in_specs=[pl.BlockSpec((tm, D_MODEL), lambda i: (i, 0)),
                  pl.BlockSpec((1, 6, D_MODEL), lambda i: (i // per_seq, 0, 0)),
                  pl.BlockSpec((1, D_MODEL), lambda i: (0, 0)),
                  pl.BlockSpec((D_MODEL, QKV_COLS), lambda i: (0, 0)),
                  pl.BlockSpec((4, NA_WIDTH), lambda i: (0, 0)),
                  pl.BlockSpec((tm, DIFF_WIDTH), lambda i: (i % per_seq, 0)),
                  pl.BlockSpec((tm, DIFF_WIDTH), lambda i: (i % per_seq, 0)),
                  pl.BlockSpec((MXU_DIM, MXU_DIM), lambda i: (0, 0))],
        out_specs=pl.BlockSpec((tm, QKV_COLS), lambda i: (i, 0)),
        out_shape=jax.ShapeDtypeStruct((n, QKV_COLS), BF16),
        compiler_params=_params("arbitrary"),
        name="qkv",
    )(x2d, ada_g, g_attn, w_qkv_bf, gains, cos_t, sin_t, bd)


def _na_kernel(q_ref, k_ref, v_ref, bias_ref, o_ref, *, rows):
    lane = lax.broadcasted_iota(jnp.int32, (GRID_W, LANES), 1)
    head0 = lane < HEAD_DIM
    win = NA_WIN_ROWS * GRID_W

    def one_row(r):
        r_start = jnp.clip(r - NA_WIN_ROWS // 2, 0, rows - NA_WIN_ROWS)
        delta = r - r_start
        q = q_ref[pl.ds(pl.multiple_of(r * GRID_W, GRID_W), GRID_W), :]
        k0 = pl.multiple_of(r_start * GRID_W, GRID_W)
        kw = k_ref[pl.ds(k0, win), :]
        vw = v_ref[pl.ds(k0, win), :]
        zero = jnp.zeros_like(q)
        qm = jnp.concatenate([jnp.where(head0, q, zero), jnp.where(head0, zero, q)], axis=0)
        s = lax.dot_general(kw, qm, (((1,), (1,)), ((), ())), preferred_element_type=F32)
        s = s + bias_ref[0, delta]
        m = _col_reduce(s, jnp.max)
        p = jnp.exp2(s - m)
        l = _col_reduce(p, jnp.sum)
        p = (p * (1.0 / l)).astype(BF16)
        o2 = lax.dot_general(p, vw, (((0,), (0,)), ((), ())), preferred_element_type=F32)
        o = jnp.where(head0, o2[:GRID_W], o2[GRID_W:])
        o_ref[pl.ds(pl.multiple_of(r * GRID_W, GRID_W), GRID_W), :] = o.astype(BF16)

    def body(i, carry):
        for u in range(NA_ROWS_PER_TRIP):
            one_row(i * NA_ROWS_PER_TRIP + u)
        return carry

    lax.fori_loop(0, rows // NA_ROWS_PER_TRIP, body, 0)


def _na(qkv, bias_t, batch, seq):
    rows = seq // GRID_W
    n_pairs = NA_HEADS // 2
    return pl.pallas_call(
        functools.partial(_na_kernel, rows=rows),
        grid=(batch, n_pairs),
        in_specs=[pl.BlockSpec((seq, LANES), lambda b, hp: (b, hp)),
                  pl.BlockSpec((seq, LANES), lambda b, hp: (b, n_pairs + hp)),
                  pl.BlockSpec((seq, LANES), lambda b, hp: (b, 2 * n_pairs + hp)),
                  pl.BlockSpec((1, NA_WIN_ROWS, NA_WIN_ROWS * GRID_W, LANES), lambda b, hp: (hp, 0, 0, 0))],
        out_specs=pl.BlockSpec((seq, LANES), lambda b, hp: (b, hp)),
        out_shape=jax.ShapeDtypeStruct((batch * seq, NA_WIDTH), BF16),
        compiler_params=_params("arbitrary", "arbitrary"),
        name="na_attn",
    )(qkv, qkv, qkv, bias_t)


def _diff_kernel(q_ref, k_ref, v_ref, lam_ref, g_ref, o_ref, vt_ref):
    @pl.when(pl.program_id(2) == 0)
    def _():
        vt_ref[:LANES, :] = v_ref[...].astype(F32).T.astype(BF16)
        ones_row = lax.broadcasted_iota(jnp.int32, (VT_ROWS - LANES, v_ref.shape[0]), 0) == 0
        vt_ref[LANES:, :] = jnp.where(ones_row, 1.0, 0.0).astype(BF16)

    lq1 = lam_ref[0:1, :]
    lk1 = lam_ref[1:2, :]
    lq2 = lam_ref[2:3, :]
    lk2 = lam_ref[3:4, :]
    lam = (jnp.exp(jnp.sum(lq1 * lk1, axis=-1, keepdims=True))
           - jnp.exp(jnp.sum(lq2 * lk2, axis=-1, keepdims=True)) + LAMBDA_INIT)

    q = q_ref[...]
    lane = lax.broadcasted_iota(jnp.int32, q.shape, 1)
    zero = jnp.zeros_like(q)
    tq = q.shape[0]
    qcat = jnp.concatenate([jnp.where(lane < HEAD_DIM, q, zero), jnp.where(lane < HEAD_DIM, zero, q)], axis=0)
    s = lax.dot_general(k_ref[...], qcat, (((1,), (1,)), ((), ())), preferred_element_type=F32)
    m = _col_reduce(s, jnp.max)
    p = jnp.exp2((s - m).astype(BF16))
    oa = jnp.dot(vt_ref[...], p, preferred_element_type=F32)
    o0, l0 = oa[:LANES, :tq], oa[LANES:LANES + 1, :tq]
    o1, l1 = oa[:LANES, tq:], oa[LANES:LANES + 1, tq:]
    o = o0 * (1.0 / l0) - (lam / l1) * o1
    ms = jnp.mean(o * o, axis=0, keepdims=True)
    y = o * lax.rsqrt(ms + EPS) * g_ref[...]
    o_ref[...] = y.T.astype(BF16)


def _diff(qkv, lam_vecs, subln_col, batch, seq):
    tq = Q_BLOCK
    nq = seq // tq
    base = 3 * NA_WIDTH // LANES
    nh = DIFF_HEADS
    return pl.pallas_call(
        _diff_kernel,
        grid=(batch, nh, nq),
        in_specs=[pl.BlockSpec((tq, LANES), lambda b, h, i: (b * nq + i, base + h)),
                  pl.BlockSpec((seq, LANES), lambda b, h, i: (b, base + nh + h)),
                  pl.BlockSpec((seq, LANES), lambda b, h, i: (b, base + 2 * nh + h)),
                  pl.BlockSpec((4, HEAD_DIM), lambda b, h, i: (0, 0)),
                  pl.BlockSpec((LANES, 1), lambda b, h, i: (0, 0))],
        out_specs=pl.BlockSpec((tq, LANES), lambda b, h, i: (b * nq + i, h)),
        out_shape=jax.ShapeDtypeStruct((batch * seq, DIFF_WIDTH), BF16),
        scratch_shapes=[pltpu.VMEM((VT_ROWS, seq), BF16)],
        compiler_params=_params("arbitrary", "arbitrary", "arbitrary"),
        name="diff_attn",
    )(qkv, qkv, qkv, lam_vecs, subln_col)


def _wo_kernel(ona_ref, odf_ref, x_ref, ada_ref, wo_ref, g_ref, wrh_ref, wrl_ref, br_ref, tri_ref, cnt0_ref,
               x1_ref, h2_ref, ti_ref, tw_ref, cnt_ref):
    mix = (jnp.dot(ona_ref[...], wo_ref[:NA_WIDTH, :], preferred_element_type=F32)
           + jnp.dot(odf_ref[...], wo_ref[NA_WIDTH:, :], preferred_element_type=F32))
    gt1 = ada_ref[0, 2:3, :]
    sh2 = ada_ref[0, 3:4, :]
    sc2 = ada_ref[0, 4:5, :]
    x1 = x_ref[...] + gt1 * mix
    x1_ref[...] = x1
    ms = jnp.mean(x1 * x1, axis=-1, keepdims=True)
    h2 = x1 * lax.rsqrt(ms + EPS) * g_ref[...] * (1.0 + sc2) + sh2
    hi = h2.astype(BF16)
    h2_ref[...] = _pack_halves(hi.astype(F32))
    lo = (h2 - hi.astype(F32)).astype(BF16)
    logits = (jnp.dot(hi, wrh_ref[...], preferred_element_type=F32)
              + jnp.dot(hi, wrl_ref[...], preferred_element_type=F32)
              + jnp.dot(lo, wrh_ref[...], preferred_element_type=F32)) + br_ref[...]
    lane = lax.broadcasted_iota(jnp.int32, logits.shape, 1).astype(F32)
    vals = []
    idxs = []
    cur = logits
    for _ in range(TOP_K):
        m = jnp.max(cur, axis=-1, keepdims=True)
        idx = jnp.min(jnp.where(cur == m, lane, float(LANES)), axis=-1, keepdims=True)
        vals.append(m)
        idxs.append(idx)
        cur = jnp.where(lane == idx, -jnp.inf, cur)
    es = [jnp.exp(v - vals[0]) for v in vals]
    inv = 1.0 / (es[0] + es[1] + es[2] + es[3])

    @pl.when(pl.program_id(0) == 0)
    def _():
        cnt_ref[...] = cnt0_ref[...]

    sel = jnp.zeros(logits.shape, F32)
    for j in range(TOP_K):
        sel = sel + jnp.where(lane == idxs[j], 1.0, 0.0)
    before = jnp.dot(tri_ref[...], sel.astype(BF16), preferred_element_type=F32) + cnt_ref[...]
    cnt_ref[...] = cnt_ref[...] + jnp.sum(sel, axis=0, keepdims=True)
    ranks = [jnp.sum(jnp.where(lane == idxs[j], before, 0.0), axis=-1, keepdims=True) for j in range(TOP_K)]

    ti = jnp.zeros(logits.shape, F32)
    tw = jnp.zeros(logits.shape, F32)
    for j in range(TOP_K):
        ti = jnp.where(lane == float(j), idxs[j], ti)
        ti = jnp.where(lane == float(TOP_K + j), ranks[j], ti)
        tw = jnp.where(lane == float(j), es[j] * inv, tw)
    ti_ref[...] = ti.astype(jnp.int32)
    tw_ref[...] = tw


def _wo(o_na, o_df, x2d, ada_g, w_o_bf, g_ffn, wr_hi, wr_lo, br_pad, tri, cnt0, seq):
    n = x2d.shape[0]
    tm = ROW_BLOCK
    per_seq = seq // tm
    row = lambda i: (i, 0)
    const = lambda i: (0, 0)
    return pl.pallas_call(
        _wo_kernel,
        grid=(n // tm,),
        in_specs=[pl.BlockSpec((tm, NA_WIDTH), row),
                  pl.BlockSpec((tm, DIFF_WIDTH), row),
                  pl.BlockSpec((tm, D_MODEL), row),
                  pl.BlockSpec((1, 6, D_MODEL), lambda i: (i // per_seq, 0, 0)),
                  pl.BlockSpec((D_MODEL, D_MODEL), const),
                  pl.BlockSpec((1, D_MODEL), const),
                  pl.BlockSpec((D_MODEL, LANES), const),
                  pl.BlockSpec((D_MODEL, LANES), const),
                  pl.BlockSpec((1, LANES), const),
                  pl.BlockSpec((tm, tm), const),
                  pl.BlockSpec((1, LANES), const)],
        out_specs=[pl.BlockSpec((tm, D_MODEL), row),
                   pl.BlockSpec((tm, PACKED), row),
                   pl.BlockSpec((tm, LANES), row),
                   pl.BlockSpec((tm, LANES), row),
                   pl.BlockSpec((1, LANES), const)],
        out_shape=[jax.ShapeDtypeStruct((n, D_MODEL), F32),
                   jax.ShapeDtypeStruct((n, PACKED), U32),
                   jax.ShapeDtypeStruct((n, LANES), jnp.int32),
                   jax.ShapeDtypeStruct((n, LANES), F32),
                   jax.ShapeDtypeStruct((1, LANES), F32)],
        compiler_params=_params("arbitrary"),
        name="wo_router",
    )(o_na, o_df, x2d, ada_g, w_o_bf, g_ffn, wr_hi, wr_lo, br_pad, tri, cnt0)


def _expert_kernel(be_ref, na_ref, xs_ref, wg_ref, bg_ref, wu_ref, bu_ref, wd_ref, bd_ref, o_ref,
                   wg_bf, wu_bf, wd_bf):
    i = pl.program_id(0)
    active = i < na_ref[0]
    new_expert = jnp.logical_or(i == 0, be_ref[i] != be_ref[jnp.maximum(i - 1, 0)])

    @pl.when(jnp.logical_and(active, new_expert))
    def _():
        wg_bf[...] = wg_ref[0].astype(BF16)
        wu_bf[...] = wu_ref[0].astype(BF16)
        wd_bf[...] = wd_ref[0].astype(BF16)

    @pl.when(active)
    def _():
        x_lo, x_hi = _unpack_halves(xs_ref[...])
        x_lo = x_lo.astype(BF16)
        x_hi = x_hi.astype(BF16)

        def proj(w_bf):
            return (jnp.dot(x_lo, w_bf[:PACKED, :], preferred_element_type=F32)
                    + jnp.dot(x_hi, w_bf[PACKED:, :], preferred_element_type=F32))

        g = jnp.minimum(proj(wg_bf) + bg_ref[0], SWIGLU_LIMIT)
        u = jnp.clip(proj(wu_bf) + bu_ref[0], -SWIGLU_LIMIT, SWIGLU_LIMIT)
        act = g * jax.nn.sigmoid(SWIGLU_ALPHA * g) * (u + 1.0)
        out = jnp.dot(act.astype(BF16), wd_bf[...], preferred_element_type=F32) + bd_ref[0]
        o_ref[...] = _pack_halves(out.astype(BF16).astype(F32))

    @pl.when(i >= na_ref[0])
    def _():
        o_ref[...] = jnp.zeros_like(o_ref)


def _experts(block_e, n_active, xs, wg, bg, wu, bu, wd, bd):
    cap = xs.shape[0]
    n_blocks = cap // EXPERT_BLOCK
    xmap = lambda i, be, na: (jnp.minimum(i, na[0] - 1), 0)
    wmap = lambda i, be, na: (be[i], 0, 0)
    grid_spec = pltpu.PrefetchScalarGridSpec(
        num_scalar_prefetch=2,
        grid=(n_blocks,),
        in_specs=[pl.BlockSpec((EXPERT_BLOCK, PACKED), xmap),
                  pl.BlockSpec((1, D_MODEL, D_MODEL), wmap),
                  pl.BlockSpec((1, 1, D_MODEL), wmap),
                  pl.BlockSpec((1, D_MODEL, D_MODEL), wmap),
                  pl.BlockSpec((1, 1, D_MODEL), wmap),
                  pl.BlockSpec((1, D_MODEL, D_MODEL), wmap),
                  pl.BlockSpec((1, 1, D_MODEL), wmap)],
        out_specs=pl.BlockSpec((EXPERT_BLOCK, PACKED), lambda i, be, na: (i, 0)),
        scratch_shapes=[pltpu.VMEM((D_MODEL, D_MODEL), BF16)] * 3,
    )
    return pl.pallas_call(
        _expert_kernel,
        grid_spec=grid_spec,
        out_shape=jax.ShapeDtypeStruct((cap, PACKED), U32),
        compiler_params=_params("arbitrary"),
        name="experts",
    )(block_e, n_active, xs, wg, bg, wu, bu, wd, bd)


def _sc_gather(table, idx):
    n_out = idx.shape[0]
    width = table.shape[1]
    per_worker = n_out // SC_WORKERS
    n_chunks = per_worker // GATHER_ROWS
    assert per_worker * SC_WORKERS == n_out and n_chunks * GATHER_ROWS == per_worker and n_chunks % 2 == 0
    idx3 = idx.reshape(SC_WORKERS, n_chunks, GATHER_ROWS)
    mesh = plsc.VectorSubcoreMesh(core_axis_name="core", subcore_axis_name="subcore")

    @functools.partial(
        pl.kernel, mesh=mesh,
        out_type=jax.ShapeDtypeStruct((n_out, width), table.dtype),
        scratch_types=[pltpu.VMEM((n_chunks, GATHER_ROWS), jnp.int32),
                       pltpu.VMEM((2, GATHER_ROWS, width), table.dtype),
                       pltpu.SemaphoreType.DMA((2,)),
                       pltpu.SemaphoreType.DMA((2,))])
    def gather_kernel(table_hbm, idx_hbm, out_hbm, idx_v, rows_v, gsem, wsem):
        wid = lax.axis_index("subcore") * SC_CORES + lax.axis_index("core")
        base = wid * per_worker
        pltpu.sync_copy(idx_hbm.at[wid], idx_v)

        def gather(j, slot):
            return pltpu.make_async_copy(table_hbm.at[idx_v.at[j]], rows_v.at[slot], gsem.at[slot])

        def write(j, slot):
            dst = out_hbm.at[pl.ds(pl.multiple_of(base + j * GATHER_ROWS, GATHER_ROWS), GATHER_ROWS)]
            return pltpu.make_async_copy(rows_v.at[slot], dst, wsem.at[slot])

        gather(0, 0).start()

        @pl.loop(0, n_chunks, step=2)
        def _(j):
            for slot in range(2):
                jj = j + slot
                gather(jj, slot).wait()

                @pl.when(jj >= 1)
                def _():
                    write(jj - 1, 1 - slot).wait()

                @pl.when(jj + 1 < n_chunks)
                def _():
                    gather(jj + 1, 1 - slot).start()

                write(jj, slot).start()

        write(n_chunks - 1, 1).wait()

    return gather_kernel(table, idx3)


def _sc_scatter(rows, idx, n_out):
    n_src, width = rows.shape
    n_idx = idx.shape[0]
    per_worker = n_idx // SC_WORKERS
    n_chunks = per_worker // GATHER_ROWS
    assert per_worker * SC_WORKERS == n_idx and n_chunks * GATHER_ROWS == per_worker and n_chunks % 2 == 0
    assert n_src % per_worker == 0
    idx3 = idx.reshape(SC_WORKERS, n_chunks, GATHER_ROWS)
    mesh = plsc.VectorSubcoreMesh(core_axis_name="core", subcore_axis_name="subcore")

    @functools.partial(
        pl.kernel, mesh=mesh,
        out_type=jax.ShapeDtypeStruct((n_out, width), rows.dtype),
        scratch_types=[pltpu.VMEM((n_chunks, GATHER_ROWS), jnp.int32),
                       pltpu.VMEM((2, GATHER_ROWS, width), rows.dtype),
                       pltpu.SemaphoreType.DMA((2,)),
                       pltpu.SemaphoreType.DMA((2,))])
    def scatter_kernel(rows_hbm, idx_hbm, out_hbm, idx_v, rows_v, rsem, wsem):
        wid = lax.axis_index("subcore") * SC_CORES + lax.axis_index("core")
        base = lax.rem(wid * per_worker, n_src)
        pltpu.sync_copy(idx_hbm.at[wid], idx_v)

        def read(j, slot):
            src = rows_hbm.at[pl.ds(pl.multiple_of(base + j * GATHER_ROWS, GATHER_ROWS), GATHER_ROWS)]
            return pltpu.make_async_copy(src, rows_v.at[slot], rsem.at[slot])

        def write(j, slot):
            return pltpu.make_async_copy(rows_v.at[slot], out_hbm.at[idx_v.at[j]], wsem.at[slot])

        read(0, 0).start()

        @pl.loop(0, n_chunks, step=2)
        def _(j):
            for slot in range(2):
                jj = j + slot
                read(jj, slot).wait()

                @pl.when(jj >= 1)
                def _():
                    write(jj - 1, 1 - slot).wait()

                @pl.when(jj + 1 < n_chunks)
                def _():
                    read(jj + 1, 1 - slot).start()

                write(jj, slot).start()

        write(n_chunks - 1, 1).wait()

    return scatter_kernel(rows, idx3)


def _combine_kernel(x1_ref, y0_ref, y1_ref, y2_ref, y3_ref, tw_ref, ada_ref, o_ref):
    tw = tw_ref[...]
    acc_lo = jnp.zeros((x1_ref.shape[0], PACKED), F32)
    acc_hi = jnp.zeros((x1_ref.shape[0], PACKED), F32)
    for j, y_ref in enumerate((y0_ref, y1_ref, y2_ref, y3_ref)):
        lo, hi = _unpack_halves(y_ref[...])
        acc_lo = acc_lo + tw[:, j:j + 1] * lo
        acc_hi = acc_hi + tw[:, j:j + 1] * hi
    o_ref[:, :PACKED] = x1_ref[:, :PACKED] + ada_ref[0, 5:6, :PACKED] * acc_lo
    o_ref[:, PACKED:] = x1_ref[:, PACKED:] + ada_ref[0, 5:6, PACKED:] * acc_hi


def _combine(x1, ys, tw, ada_g, seq, row_off):
    n = x1.shape[0]
    tm = ROW_BLOCK
    per_seq = seq // tm
    off = row_off // tm
    per_choice = tw.shape[0] // tm
    y_specs = [pl.BlockSpec((tm, PACKED), functools.partial(lambda i, j: (j * per_choice + off + i, 0), j=j))
               for j in range(TOP_K)]
    return pl.pallas_call(
        _combine_kernel,
        grid=(n // tm,),
        in_specs=[pl.BlockSpec((tm, D_MODEL), lambda i: (i, 0)),
                  *y_specs,
                  pl.BlockSpec((tm, LANES), lambda i: (i + off, 0)),
                  pl.BlockSpec((1, 6, D_MODEL), lambda i: (i // per_seq, 0, 0))],
        out_specs=pl.BlockSpec((tm, D_MODEL), lambda i: (i, 0)),
        out_shape=jax.ShapeDtypeStruct((n, D_MODEL), F32),
        compiler_params=_params("arbitrary"),
        name="combine",
    )(x1, ys, ys, ys, ys, tw, ada_g)


def _rope_tables(seq):
    half = HEAD_DIM // 2
    inv = ROPE_THETA ** (-jnp.arange(half, dtype=F32) / half)
    ang = jnp.arange(seq, dtype=F32)[:, None] * inv[None, :]
    cos, sin = jnp.cos(ang), jnp.sin(ang)
    cos_h = jnp.concatenate([cos, cos], axis=-1)
    sin_h = jnp.concatenate([-sin, sin], axis=-1)
    reps = DIFF_WIDTH // HEAD_DIM
    return jnp.tile(cos_h, (1, reps)), jnp.tile(sin_h, (1, reps))


def _na_bias_table(rpb):
    cols = jnp.arange(GRID_W, dtype=jnp.int32)
    c_start = jnp.clip(cols - NA_WIN_COLS // 2, 0, GRID_W - NA_WIN_COLS)
    col_mask = (cols[None, :] >= c_start[:, None]) & (cols[None, :] < c_start[:, None] + NA_WIN_COLS)
    col_idx = jnp.clip(cols[None, :] - cols[:, None], -(NA_WIN_COLS - 1), NA_WIN_COLS - 1) + NA_WIN_COLS - 1
    delta = jnp.arange(NA_WIN_ROWS, dtype=jnp.int32)
    j = jnp.arange(NA_WIN_ROWS, dtype=jnp.int32)
    row_idx = j[None, :] - delta[:, None] + NA_WIN_ROWS - 1
    row_hot = (row_idx[:, :, None] == jnp.arange(2 * NA_WIN_ROWS - 1, dtype=jnp.int32)).astype(F32)
    col_hot = (col_idx[:, :, None] == jnp.arange(2 * NA_WIN_COLS - 1, dtype=jnp.int32)).astype(F32)
    bias = jnp.einsum('djr,hrc,qkc->hdjqk', row_hot, rpb.astype(F32), col_hot,
                      precision=lax.Precision.HIGHEST)
    bias = jnp.where(col_mask[None, None, None], bias * LOG2E, NEG_INF)
    bias = bias.transpose(0, 1, 2, 4, 3).reshape(NA_HEADS, NA_WIN_ROWS, NA_WIN_ROWS * GRID_W, GRID_W)
    bias = bias.reshape(NA_HEADS // 2, 2, NA_WIN_ROWS, NA_WIN_ROWS * GRID_W, GRID_W)
    return jnp.concatenate([bias[:, 0], bias[:, 1]], axis=-1)


def _routing(top_idx, rank, counts, n):
    n_blocks = n * TOP_K // EXPERT_BLOCK + N_EXPERTS
    experts = jnp.arange(N_EXPERTS, dtype=jnp.int32)
    padded = (counts + EXPERT_BLOCK - 1) // EXPERT_BLOCK * EXPERT_BLOCK
    pad_end = jnp.cumsum(padded)
    pad_start = pad_end - padded
    start_of = jnp.sum(jnp.where(top_idx[:, :, None] == experts, pad_start, 0), axis=-1)
    dest = (start_of + rank).T.reshape(-1)
    block_lo = jnp.arange(n_blocks, dtype=jnp.int32) * EXPERT_BLOCK
    block_e = jnp.minimum(jnp.sum((pad_end[None, :] <= block_lo[:, None]).astype(jnp.int32), axis=1),
                          N_EXPERTS - 1).astype(jnp.int32)
    n_active = (pad_end[-1] // EXPERT_BLOCK).astype(jnp.int32).reshape(1)
    return dest, block_e, n_active, n_blocks * EXPERT_BLOCK


def kernel(x_prompt, x_sample, c_prompt, c_sample, w_ada, b_ada, g_attn_norm, w_qkv, na_q_norm, na_k_norm, na_rpb, diff_q_norm, diff_k_norm, lambda_q1, lambda_k1, lambda_q2, lambda_k2, diff_subln, w_o, g_ffn_norm, w_router, b_router, w_gate, b_gate, w_up, b_up, w_down, b_down):
    l = 0
    groups = [(x_prompt, c_prompt), (x_sample, c_sample)]
    nb = [x.shape[0] for x, _ in groups]

    ada_all = _ada(jnp.concatenate([c for _, c in groups], axis=0), w_ada[l], b_ada[l])
    ada_all = ada_all.reshape(sum(nb), 6, D_MODEL)

    w_qkv_bf = w_qkv[l].astype(BF16)
    w_o_bf = w_o[l].astype(BF16)
    scale = HEAD_DIM ** -0.5
    reps = NA_WIDTH // HEAD_DIM
    gains = jnp.stack([jnp.tile(na_q_norm[l], reps) * (scale * LOG2E),
                       jnp.tile(na_k_norm[l], reps),
                       jnp.tile(diff_q_norm[l], reps) * (scale * LOG2E),
                       jnp.tile(diff_k_norm[l], reps)]).astype(F32)
    head_id = jnp.arange(MXU_DIM, dtype=jnp.int32) // HEAD_DIM
    bd = (head_id[:, None] == head_id[None, :]).astype(BF16)
    bias_t = _na_bias_table(na_rpb[l])
    lam_vecs = jnp.stack([lambda_q1[l], lambda_k1[l], lambda_q2[l], lambda_k2[l]]).astype(F32)
    subln_col = (diff_subln[l].astype(F32) * (1.0 - LAMBDA_INIT)).reshape(LANES, 1)
    wr = w_router[l].astype(F32)
    wr_pad = jnp.zeros((D_MODEL, LANES), F32).at[:, :N_EXPERTS].set(wr)
    wr_hi = wr_pad.astype(BF16)
    wr_lo = (wr_pad - wr_hi.astype(F32)).astype(BF16)
    br_pad = jnp.full((1, LANES), NEG_INF, F32).at[0, :N_EXPERTS].set(b_router[l].astype(F32))
    g_attn = g_attn_norm[l].reshape(1, D_MODEL).astype(F32)
    g_ffn = g_ffn_norm[l].reshape(1, D_MODEL).astype(F32)
    max_seq = max(x.shape[1] for x, _ in groups)
    cos_t, sin_t = _rope_tables(max_seq)

    rows = lax.broadcasted_iota(jnp.int32, (ROW_BLOCK, ROW_BLOCK), 0)
    cols = lax.broadcasted_iota(jnp.int32, (ROW_BLOCK, ROW_BLOCK), 1)
    tri = (cols < rows).astype(BF16)
    cnt = jnp.zeros((1, LANES), F32)

    x1s, h2s, tis, tws = [], [], [], []
    b0 = 0
    for (x, _), b in zip(groups, nb):
        seq = x.shape[1]
        x2d = x.reshape(b * seq, D_MODEL)
        ada_g = ada_all[b0:b0 + b]
        b0 += b
        qkv = _qkv(x2d, ada_g, g_attn, w_qkv_bf, gains, cos_t, sin_t, bd, seq)
        o_na = _na(qkv, bias_t, b, seq)
        o_df = _diff(qkv, lam_vecs, subln_col, b, seq)
        x1, h2, ti, tw, cnt = _wo(o_na, o_df, x2d, ada_g, w_o_bf, g_ffn, wr_hi, wr_lo, br_pad, tri, cnt, seq)
        x1s.append(x1)
        h2s.append(h2)
        tis.append(ti)
        tws.append(tw)

    h2_all = jnp.concatenate(h2s, axis=0)
    ti_all = jnp.concatenate(tis, axis=0)
    tw_all = jnp.concatenate(tws, axis=0)
    n = h2_all.shape[0]
    counts = cnt[0, :N_EXPERTS].astype(jnp.int32)
    dest, block_e, n_active, cap = _routing(ti_all[:, :TOP_K], ti_all[:, TOP_K:2 * TOP_K], counts, n)

    xs = _sc_scatter(h2_all, dest, cap)
    ys_sorted = _experts(block_e, n_active, xs,
                         w_gate[l], b_gate[l].reshape(N_EXPERTS, 1, D_MODEL).astype(F32),
                         w_up[l], b_up[l].reshape(N_EXPERTS, 1, D_MODEL).astype(F32),
                         w_down[l], b_down[l].reshape(N_EXPERTS, 1, D_MODEL).astype(F32))
    ys = _sc_gather(ys_sorted, dest)

    outs = []
    b0 = 0
    row_off = 0
    for (x, _), b, x1 in zip(groups, nb, x1s):
        seq = x.shape[1]
        ada_g = ada_all[b0:b0 + b]
        b0 += b
        y = _combine(x1, ys, tw_all, ada_g, seq, row_off)
        row_off += b * seq
        outs.append(y.reshape(b, seq, D_MODEL))
    return tuple(outs)
```

```python
import functools
import math

import jax
import jax.numpy as jnp
from jax import lax
from jax.experimental import pallas as pl
from jax.experimental.pallas import tpu as pltpu
from jax.experimental.pallas import tpu_sc as plsc

F32 = jnp.float32
BF16 = jnp.bfloat16
U32 = jnp.uint32

D_MODEL = 1024
HEAD_DIM = 64
NA_HEADS = 8
NA_WIDTH = 512
DIFF_HEADS = 4
DIFF_WIDTH = 512
QKV_COLS = 3072
GRID_W = 64
NA_WIN_ROWS = 8
NA_WIN_COLS = 16
ROPE_THETA = 10000.0
N_EXPERTS = 32
TOP_K = 4
SWIGLU_LIMIT = 7.0
SWIGLU_ALPHA = 1.702
EPS = 1e-5
NEG_INF = -1e30
LAMBDA_INIT = 0.8 - 0.6 * math.exp(-0.3 * 0)
LOG2E = 1.4426950408889634

LANES = 128
MXU_DIM = 256
VMEM_LIMIT = 56 * 1024 * 1024

ROW_BLOCK = 512
Q_BLOCK = 256
EXPERT_BLOCK = 512
NA_ROWS_PER_TRIP = 4
VT_ROWS = LANES + 16
KV_CHUNK = 512


PACKED = D_MODEL // 2
SC_CORES = 2
SC_SUBCORES = 16
SC_WORKERS = SC_CORES * SC_SUBCORES
GATHER_ROWS = 64


def _params(*sem):
    return pltpu.CompilerParams(dimension_semantics=sem, vmem_limit_bytes=VMEM_LIMIT)


def _pack_halves(x):
    w = x.shape[1] // 2
    bits = lax.bitcast_convert_type(x, U32)
    return (bits[:, :w] >> 16) | bits[:, w:]


def _col_reduce(x, op):
    while x.shape[0] >= 64:
        x = op(x.reshape(8, x.shape[0] // 8, x.shape[1]), axis=0)
    return op(x, axis=0, keepdims=True)


def _unpack_halves(word):
    lo = lax.bitcast_convert_type(word << 16, F32)
    hi = lax.bitcast_convert_type(word & jnp.uint32(0xFFFF0000), F32)
    return lo, hi


def _ada_kernel(c_ref, w_ref, b_ref, o_ref):
    c = c_ref[...]
    s = c * jax.nn.sigmoid(c)
    o_ref[...] = jnp.dot(s, w_ref[...], preferred_element_type=F32,
                         precision=lax.Precision.HIGHEST) + b_ref[...]


def _ada(c_all, w_ada, b_ada):
    nb = c_all.shape[0]
    n_out = w_ada.shape[1]
    blk = D_MODEL
    return pl.pallas_call(
        _ada_kernel,
        grid=(n_out // blk,),
        in_specs=[pl.BlockSpec((nb, D_MODEL), lambda j: (0, 0)),
                  pl.BlockSpec((D_MODEL, blk), lambda j: (0, j)),
                  pl.BlockSpec((1, blk), lambda j: (0, j))],
        out_specs=pl.BlockSpec((nb, blk), lambda j: (0, j)),
        out_shape=jax.ShapeDtypeStruct((nb, n_out), F32),
        compiler_params=_params("arbitrary"),
        name="ada",
    )(c_all, w_ada, b_ada.reshape(1, n_out))


def _head_sumsq(y, bd):
    sq = (y * y).astype(BF16)
    parts = [jnp.dot(sq[:, c:c + MXU_DIM], bd, preferred_element_type=F32)
             for c in range(0, y.shape[1], MXU_DIM)]
    return jnp.concatenate(parts, axis=1)


def _qkv_kernel(x_ref, ada_ref, g_ref, w_ref, gain_ref, cos_ref, sin_ref, bd_ref, o_ref):
    x = x_ref[...]
    ms = jnp.mean(x * x, axis=-1, keepdims=True)
    xn = x * lax.rsqrt(ms + EPS) * g_ref[...]
    sh = ada_ref[0, 0:1, :]
    sc = ada_ref[0, 1:2, :]
    h = (xn * (1.0 + sc) + sh).astype(BF16)
    bd = bd_ref[...]
    lane = lax.broadcasted_iota(jnp.int32, (x.shape[0], NA_WIDTH), 1)
    first_half = (lane & (HEAD_DIM // 2)) == 0
    for grp in range(6):
        cols = slice(grp * 512, (grp + 1) * 512)
        acc = jnp.dot(h, w_ref[:, cols], preferred_element_type=F32)
        if grp in (2, 5):
            o_ref[:, cols] = acc.astype(BF16)
            continue
        gi = {0: 0, 1: 1, 3: 2, 4: 3}[grp]
        ss = _head_sumsq(acc, bd)
        y = acc * lax.rsqrt(ss * (1.0 / HEAD_DIM) + EPS) * gain_ref[gi:gi + 1, :]
        if grp in (3, 4):
            partner = jnp.where(first_half,
                                pltpu.roll(y, NA_WIDTH - HEAD_DIM // 2, axis=1),
                                pltpu.roll(y, HEAD_DIM // 2, axis=1))
            y = y * cos_ref[...] + partner * sin_ref[...]
        o_ref[:, cols] = y.astype(BF16)


def _qkv(x2d, ada_g, g_attn, w_qkv_bf, gains, cos_t, sin_t, bd, seq):
    n = x2d.shape[0]
    tm = ROW_BLOCK
    per_seq = seq // tm
    return pl.pallas_call(
        _qkv_kernel,
        grid=(n // tm,),
        in_specs=[pl.BlockSpec((tm, D_MODEL), lambda i: (i, 0)),
                  pl.BlockSpec((1, 6, D_MODEL), lambda i: (i // per_seq, 0, 0)),
                  pl.BlockSpec((1, D_MODEL), lambda i: (0, 0)),
                  pl.BlockSpec((D_MODEL, QKV_COLS), lambda i: (0, 0)),
                  pl.BlockSpec((4, NA_WIDTH), lambda i: (0, 0)),
                  pl.BlockSpec((tm, DIFF_WIDTH), lambda i: (i % per_seq, 0)),
                  pl.BlockSpec((tm, DIFF_WIDTH), lambda i: (i % per_seq, 0)),
                  pl.BlockSpec((MXU_DIM, MXU_DIM), lambda i: (0, 0))],
        out_specs=pl.BlockSpec((tm, QKV_COLS), lambda i: (i, 0)),
        out_shape=jax.ShapeDtypeStruct((n, QKV_COLS), BF16),
        compiler_params=_params("arbitrary"),
        name="qkv",
    )(x2d, ada_g, g_attn, w_qkv_bf, gains, cos_t, sin_t, bd)


def _na_kernel(q_ref, k_ref, v_ref, bias_ref, o_ref, *, rows):
    lane = lax.broadcasted_iota(jnp.int32, (GRID_W, LANES), 1)
    head0 = lane < HEAD_DIM
    win = NA_WIN_ROWS * GRID_W

    def window_start(r):
        return jnp.clip(r - NA_WIN_ROWS // 2, 0, rows - NA_WIN_ROWS)

    def scores(r):
        r_start = window_start(r)
        q = q_ref[pl.ds(pl.multiple_of(r * GRID_W, GRID_W), GRID_W), :]
        kw = k_ref[pl.ds(pl.multiple_of(r_start * GRID_W, GRID_W), win), :]
        zero = jnp.zeros_like(q)
        qm = jnp.concatenate([jnp.where(head0, q, zero), jnp.where(head0, zero, q)], axis=0)
        s = lax.dot_general(kw, qm, (((1,), (1,)), ((), ())), preferred_element_type=F32)
        return s + bias_ref[0, r - r_start]

    def finish(r, s):
        vw = v_ref[pl.ds(pl.multiple_of(window_start(r) * GRID_W, GRID_W), win), :]
        m = _col_reduce(s, jnp.max)
        p = jnp.exp2(s - m)
        l = _col_reduce(p, jnp.sum)
        p = (p * (1.0 / l)).astype(BF16)
        o2 = lax.dot_general(p, vw, (((0,), (0,)), ((), ())), preferred_element_type=F32)
        o = jnp.where(head0, o2[:GRID_W], o2[GRID_W:])
        o_ref[pl.ds(pl.multiple_of(r * GRID_W, GRID_W), GRID_W), :] = o.astype(BF16)

    def body(i, carry):
        trip_rows = [i * NA_ROWS_PER_TRIP + u for u in range(NA_ROWS_PER_TRIP)]
        trip_scores = [scores(r) for r in trip_rows]
        for r, s in zip(trip_rows, trip_scores):
            finish(r, s)
        return carry

    lax.fori_loop(0, rows // NA_ROWS_PER_TRIP, body, 0)


def _na(qkv, bias_t, batch, seq):
    rows = seq // GRID_W
    n_pairs = NA_HEADS // 2
    return pl.pallas_call(
        functools.partial(_na_kernel, rows=rows),
        grid=(batch, n_pairs),
        in_specs=[pl.BlockSpec((seq, LANES), lambda b, hp: (b, hp)),
                  pl.BlockSpec((seq, LANES), lambda b, hp: (b, n_pairs + hp)),
                  pl.BlockSpec((seq, LANES), lambda b, hp: (b, 2 * n_pairs + hp)),
                  pl.BlockSpec((1, NA_WIN_ROWS, NA_WIN_ROWS * GRID_W, LANES), lambda b, hp: (hp, 0, 0, 0))],
        out_specs=pl.BlockSpec((seq, LANES), lambda b, hp: (b, hp)),
        out_shape=jax.ShapeDtypeStruct((batch * seq, NA_WIDTH), BF16),
        compiler_params=_params("arbitrary", "arbitrary"),
        name="na_attn",
    )(qkv, qkv, qkv, bias_t)


def _diff_kernel(q_ref, k_ref, v_ref, lam_ref, g_ref, o_ref, vt_ref):
    @pl.when(pl.program_id(2) == 0)
    def _():
        vt_ref[:LANES, :] = v_ref[...].astype(F32).T.astype(BF16)
        ones_row = lax.broadcasted_iota(jnp.int32, (VT_ROWS - LANES, v_ref.shape[0]), 0) == 0
        vt_ref[LANES:, :] = jnp.where(ones_row, 1.0, 0.0).astype(BF16)

    lq1 = lam_ref[0:1, :]
    lk1 = lam_ref[1:2, :]
    lq2 = lam_ref[2:3, :]
    lk2 = lam_ref[3:4, :]
    lam = (jnp.exp(jnp.sum(lq1 * lk1, axis=-1, keepdims=True))
           - jnp.exp(jnp.sum(lq2 * lk2, axis=-1, keepdims=True)) + LAMBDA_INIT)

    q = q_ref[...]
    lane = lax.broadcasted_iota(jnp.int32, q.shape, 1)
    zero = jnp.zeros_like(q)
    tq = q.shape[0]
    qcat = jnp.concatenate([jnp.where(lane < HEAD_DIM, q, zero), jnp.where(lane < HEAD_DIM, zero, q)], axis=0)
    def scores(c):
        kc = k_ref[c * KV_CHUNK:(c + 1) * KV_CHUNK, :]
        return lax.dot_general(kc, qcat, (((1,), (1,)), ((), ())), preferred_element_type=F32)

    n_chunks = k_ref.shape[0] // KV_CHUNK
    s_next = scores(0)
    m = None
    oa = None
    for c in range(n_chunks):
        s = s_next
        if c + 1 < n_chunks:
            s_next = scores(c + 1)
        mc = _col_reduce(s, jnp.max)
        m_new = mc if c == 0 else jnp.maximum(m, mc)
        p = jnp.exp2((s - m_new).astype(BF16))
        pv = jnp.dot(vt_ref[:, c * KV_CHUNK:(c + 1) * KV_CHUNK], p, preferred_element_type=F32)
        oa = pv if c == 0 else jnp.exp2(m - m_new) * oa + pv
        m = m_new
    o0, l0 = oa[:LANES, :tq], oa[LANES:LANES + 1, :tq]
    o1, l1 = oa[:LANES, tq:], oa[LANES:LANES + 1, tq:]
    o = o0 * (1.0 / l0) - (lam / l1) * o1
    ms = jnp.mean(o * o, axis=0, keepdims=True)
    y = o * lax.rsqrt(ms + EPS) * g_ref[...]
    o_ref[...] = y.T.astype(BF16)


def _diff(qkv, lam_vecs, subln_col, batch, seq):
    tq = Q_BLOCK
    nq = seq // tq
    base = 3 * NA_WIDTH // LANES
    nh = DIFF_HEADS
    return pl.pallas_call(
        _diff_kernel,
        grid=(batch, nh, nq),
        in_specs=[pl.BlockSpec((tq, LANES), lambda b, h, i: (b * nq + i, base + h)),
                  pl.BlockSpec((seq, LANES), lambda b, h, i: (b, base + nh + h)),
                  pl.BlockSpec((seq, LANES), lambda b, h, i: (b, base + 2 * nh + h)),
                  pl.BlockSpec((4, HEAD_DIM), lambda b, h, i: (0, 0)),
                  pl.BlockSpec((LANES, 1), lambda b, h, i: (0, 0))],
        out_specs=pl.BlockSpec((tq, LANES), lambda b, h, i: (b * nq + i, h)),
        out_shape=jax.ShapeDtypeStruct((batch * seq, DIFF_WIDTH), BF16),
        scratch_shapes=[pltpu.VMEM((VT_ROWS, seq), BF16)],
        compiler_params=_params("arbitrary", "arbitrary", "arbitrary"),
        name="diff_attn",
    )(qkv, qkv, qkv, lam_vecs, subln_col)


def _wo_kernel(ona_ref, odf_ref, x_ref, ada_ref, wo_ref, g_ref, wrh_ref, wrl_ref, br_ref, tri_ref, cnt0_ref,
               x1_ref, h2_ref, ti_ref, tw_ref, cnt_ref):
    mix = (jnp.dot(ona_ref[...], wo_ref[:NA_WIDTH, :], preferred_element_type=F32)
           + jnp.dot(odf_ref[...], wo_ref[NA_WIDTH:, :], preferred_element_type=F32))
    gt1 = ada_ref[0, 2:3, :]
    sh2 = ada_ref[0, 3:4, :]
    sc2 = ada_ref[0, 4:5, :]
    x1 = x_ref[...] + gt1 * mix
    x1_ref[...] = x1
    ms = jnp.mean(x1 * x1, axis=-1, keepdims=True)
    h2 = x1 * lax.rsqrt(ms + EPS) * g_ref[...] * (1.0 + sc2) + sh2
    hi = h2.astype(BF16)
    h2_ref[...] = _pack_halves(hi.astype(F32))
    lo = (h2 - hi.astype(F32)).astype(BF16)
    logits = (jnp.dot(hi, wrh_ref[...], preferred_element_type=F32)
              + jnp.dot(hi, wrl_ref[...], preferred_element_type=F32)
              + jnp.dot(lo, wrh_ref[...], preferred_element_type=F32)) + br_ref[...]
    lane = lax.broadcasted_iota(jnp.int32, logits.shape, 1).astype(F32)
    vals = []
    idxs = []
    cur = logits
    for _ in range(TOP_K):
        m = jnp.max(cur, axis=-1, keepdims=True)
        idx = jnp.min(jnp.where(cur == m, lane, float(LANES)), axis=-1, keepdims=True)
        vals.append(m)
        idxs.append(idx)
        cur = jnp.where(lane == idx, -jnp.inf, cur)
    es = [jnp.exp(v - vals[0]) for v in vals]
    inv = 1.0 / (es[0] + es[1] + es[2] + es[3])

    @pl.when(pl.program_id(0) == 0)
    def _():
        cnt_ref[...] = cnt0_ref[...]

    sel = jnp.zeros(logits.shape, F32)
    for j in range(TOP_K):
        sel = sel + jnp.where(lane == idxs[j], 1.0, 0.0)
    before = jnp.dot(tri_ref[...], sel.astype(BF16), preferred_element_type=F32) + cnt_ref[...]
    cnt_ref[...] = cnt_ref[...] + jnp.sum(sel, axis=0, keepdims=True)
    ranks = [jnp.sum(jnp.where(lane == idxs[j], before, 0.0), axis=-1, keepdims=True) for j in range(TOP_K)]

    ti = jnp.zeros(logits.shape, F32)
    tw = jnp.zeros(logits.shape, F32)
    for j in range(TOP_K):
        ti = jnp.where(lane == float(j), idxs[j], ti)
        ti = jnp.where(lane == float(TOP_K + j), ranks[j], ti)
        tw = jnp.where(lane == float(j), es[j] * inv, tw)
    ti_ref[...] = ti.astype(jnp.int32)
    tw_ref[...] = tw


def _wo(o_na, o_df, x2d, ada_g, w_o_bf, g_ffn, wr_hi, wr_lo, br_pad, tri, cnt0, seq):
    n = x2d.shape[0]
    tm = ROW_BLOCK
    per_seq = seq // tm
    row = lambda i: (i, 0)
    const = lambda i: (0, 0)
    return pl.pallas_call(
        _wo_kernel,
        grid=(n // tm,),
        in_specs=[pl.BlockSpec((tm, NA_WIDTH), row),
                  pl.BlockSpec((tm, DIFF_WIDTH), row),
                  pl.BlockSpec((tm, D_MODEL), row),
                  pl.BlockSpec((1, 6, D_MODEL), lambda i: (i // per_seq, 0, 0)),
                  pl.BlockSpec((D_MODEL, D_MODEL), const),
                  pl.BlockSpec((1, D_MODEL), const),
                  pl.BlockSpec((D_MODEL, LANES), const),
                  pl.BlockSpec((D_MODEL, LANES), const),
                  pl.BlockSpec((1, LANES), const),
                  pl.BlockSpec((tm, tm), const),
                  pl.BlockSpec((1, LANES), const)],
        out_specs=[pl.BlockSpec((tm, D_MODEL), row),
                   pl.BlockSpec((tm, PACKED), row),
                   pl.BlockSpec((tm, LANES), row),
                   pl.BlockSpec((tm, LANES), row),
                   pl.BlockSpec((1, LANES), const)],
        out_shape=[jax.ShapeDtypeStruct((n, D_MODEL), F32),
                   jax.ShapeDtypeStruct((n, PACKED), U32),
                   jax.ShapeDtypeStruct((n, LANES), jnp.int32),
                   jax.ShapeDtypeStruct((n, LANES), F32),
                   jax.ShapeDtypeStruct((1, LANES), F32)],
        compiler_params=_params("arbitrary"),
        name="wo_router",
    )(o_na, o_df, x2d, ada_g, w_o_bf, g_ffn, wr_hi, wr_lo, br_pad, tri, cnt0)


def _expert_kernel(be_ref, na_ref, xs_ref, wg_ref, bg_ref, wu_ref, bu_ref, wd_ref, bd_ref, o_ref,
                   wg_bf, wu_bf, wd_bf):
    i = pl.program_id(0)
    active = i < na_ref[0]
    new_expert = jnp.logical_or(i == 0, be_ref[i] != be_ref[jnp.maximum(i - 1, 0)])

    @pl.when(jnp.logical_and(active, new_expert))
    def _():
        wg_bf[...] = wg_ref[0].astype(BF16)
        wu_bf[...] = wu_ref[0].astype(BF16)
        wd_bf[...] = wd_ref[0].astype(BF16)

    @pl.when(active)
    def _():
        x_lo, x_hi = _unpack_halves(xs_ref[...])
        x_lo = x_lo.astype(BF16)
        x_hi = x_hi.astype(BF16)

        def proj(w_bf):
            return (jnp.dot(x_lo, w_bf[:PACKED, :], preferred_element_type=F32)
                    + jnp.dot(x_hi, w_bf[PACKED:, :], preferred_element_type=F32))

        g = jnp.minimum(proj(wg_bf) + bg_ref[0], SWIGLU_LIMIT)
        u = jnp.clip(proj(wu_bf) + bu_ref[0], -SWIGLU_LIMIT, SWIGLU_LIMIT)
        act = g * jax.nn.sigmoid(SWIGLU_ALPHA * g) * (u + 1.0)
        out = jnp.dot(act.astype(BF16), wd_bf[...], preferred_element_type=F32) + bd_ref[0]
        o_ref[...] = _pack_halves(out.astype(BF16).astype(F32))

    @pl.when(i >= na_ref[0])
    def _():
        o_ref[...] = jnp.zeros_like(o_ref)


def _experts(block_e, n_active, xs, wg, bg, wu, bu, wd, bd):
    cap = xs.shape[0]
    n_blocks = cap // EXPERT_BLOCK
    xmap = lambda i, be, na: (jnp.minimum(i, na[0] - 1), 0)
    wmap = lambda i, be, na: (be[i], 0, 0)
    grid_spec = pltpu.PrefetchScalarGridSpec(
        num_scalar_prefetch=2,
        grid=(n_blocks,),
        in_specs=[pl.BlockSpec((EXPERT_BLOCK, PACKED), xmap),
                  pl.BlockSpec((1, D_MODEL, D_MODEL), wmap),
                  pl.BlockSpec((1, 1, D_MODEL), wmap),
                  pl.BlockSpec((1, D_MODEL, D_MODEL), wmap),
                  pl.BlockSpec((1, 1, D_MODEL), wmap),
                  pl.BlockSpec((1, D_MODEL, D_MODEL), wmap),
                  pl.BlockSpec((1, 1, D_MODEL), wmap)],
        out_specs=pl.BlockSpec((EXPERT_BLOCK, PACKED), lambda i, be, na: (i, 0)),
        scratch_shapes=[pltpu.VMEM((D_MODEL, D_MODEL), BF16)] * 3,
    )
    return pl.pallas_call(
        _expert_kernel,
        grid_spec=grid_spec,
        out_shape=jax.ShapeDtypeStruct((cap, PACKED), U32),
        compiler_params=_params("arbitrary"),
        name="experts",
    )(block_e, n_active, xs, wg, bg, wu, bu, wd, bd)


def _sc_gather(table, idx):
    n_out = idx.shape[0]
    width = table.shape[1]
    per_worker = n_out // SC_WORKERS
    n_chunks = per_worker // GATHER_ROWS
    assert per_worker * SC_WORKERS == n_out and n_chunks * GATHER_ROWS == per_worker and n_chunks % 2 == 0
    idx3 = idx.reshape(SC_WORKERS, n_chunks, GATHER_ROWS)
    mesh = plsc.VectorSubcoreMesh(core_axis_name="core", subcore_axis_name="subcore")

    @functools.partial(
        pl.kernel, mesh=mesh,
        out_type=jax.ShapeDtypeStruct((n_out, width), table.dtype),
        scratch_types=[pltpu.VMEM((n_chunks, GATHER_ROWS), jnp.int32),
                       pltpu.VMEM((2, GATHER_ROWS, width), table.dtype),
                       pltpu.SemaphoreType.DMA((2,)),
                       pltpu.SemaphoreType.DMA((2,))])
    def gather_kernel(table_hbm, idx_hbm, out_hbm, idx_v, rows_v, gsem, wsem):
        wid = lax.axis_index("subcore") * SC_CORES + lax.axis_index("core")
        base = wid * per_worker
        pltpu.sync_copy(idx_hbm.at[wid], idx_v)

        def gather(j, slot):
            return pltpu.make_async_copy(table_hbm.at[idx_v.at[j]], rows_v.at[slot], gsem.at[slot])

        def write(j, slot):
            dst = out_hbm.at[pl.ds(pl.multiple_of(base + j * GATHER_ROWS, GATHER_ROWS), GATHER_ROWS)]
            return pltpu.make_async_copy(rows_v.at[slot], dst, wsem.at[slot])

        gather(0, 0).start()

        @pl.loop(0, n_chunks, step=2)
        def _(j):
            for slot in range(2):
                jj = j + slot
                gather(jj, slot).wait()

                @pl.when(jj >= 1)
                def _():
                    write(jj - 1, 1 - slot).wait()

                @pl.when(jj + 1 < n_chunks)
                def _():
                    gather(jj + 1, 1 - slot).start()

                write(jj, slot).start()

        write(n_chunks - 1, 1).wait()

    return gather_kernel(table, idx3)


def _sc_scatter(rows, idx, n_out):
    n_src, width = rows.shape
    n_idx = idx.shape[0]
    per_worker = n_idx // SC_WORKERS
    n_chunks = per_worker // GATHER_ROWS
    assert per_worker * SC_WORKERS == n_idx and n_chunks * GATHER_ROWS == per_worker and n_chunks % 2 == 0
    assert n_src % per_worker == 0
    idx3 = idx.reshape(SC_WORKERS, n_chunks, GATHER_ROWS)
    mesh = plsc.VectorSubcoreMesh(core_axis_name="core", subcore_axis_name="subcore")

    @functools.partial(
        pl.kernel, mesh=mesh,
        out_type=jax.ShapeDtypeStruct((n_out, width), rows.dtype),
        scratch_types=[pltpu.VMEM((n_chunks, GATHER_ROWS), jnp.int32),
                       pltpu.VMEM((2, GATHER_ROWS, width), rows.dtype),
                       pltpu.SemaphoreType.DMA((2,)),
                       pltpu.SemaphoreType.DMA((2,))])
    def scatter_kernel(rows_hbm, idx_hbm, out_hbm, idx_v, rows_v, rsem, wsem):
        wid = lax.axis_index("subcore") * SC_CORES + lax.axis_index("core")
        base = lax.rem(wid * per_worker, n_src)
        pltpu.sync_copy(idx_hbm.at[wid], idx_v)

        def read(j, slot):
            src = rows_hbm.at[pl.ds(pl.multiple_of(base + j * GATHER_ROWS, GATHER_ROWS), GATHER_ROWS)]
            return pltpu.make_async_copy(src, rows_v.at[slot], rsem.at[slot])

        def write(j, slot):
            return pltpu.make_async_copy(rows_v.at[slot], out_hbm.at[idx_v.at[j]], wsem.at[slot])

        read(0, 0).start()

        @pl.loop(0, n_chunks, step=2)
        def _(j):
            for slot in range(2):
                jj = j + slot
                read(jj, slot).wait()

                @pl.when(jj >= 1)
                def _():
                    write(jj - 1, 1 - slot).wait()

                @pl.when(jj + 1 < n_chunks)
                def _():
                    read(jj + 1, 1 - slot).start()

                write(jj, slot).start()

        write(n_chunks - 1, 1).wait()

    return scatter_kernel(rows, idx3)


def _combine_kernel(x1_ref, y0_ref, y1_ref, y2_ref, y3_ref, tw_ref, ada_ref, o_ref):
    tw = tw_ref[...]
    acc_lo = jnp.zeros((x1_ref.shape[0], PACKED), F32)
    acc_hi = jnp.zeros((x1_ref.shape[0], PACKED), F32)
    for j, y_ref in enumerate((y0_ref, y1_ref, y2_ref, y3_ref)):
        lo, hi = _unpack_halves(y_ref[...])
        acc_lo = acc_lo + tw[:, j:j + 1] * lo
        acc_hi = acc_hi + tw[:, j:j + 1] * hi
    o_ref[:, :PACKED] = x1_ref[:, :PACKED] + ada_ref[0, 5:6, :PACKED] * acc_lo
    o_ref[:, PACKED:] = x1_ref[:, PACKED:] + ada_ref[0, 5:6, PACKED:] * acc_hi


def _combine(x1, ys, tw, ada_g, seq, row_off):
    n = x1.shape[0]
    tm = ROW_BLOCK
    per_seq = seq // tm
    off = row_off // tm
    per_choice = tw.shape[0] // tm
    y_specs = [pl.BlockSpec((tm, PACKED), functools.partial(lambda i, j: (j * per_choice + off + i, 0), j=j))
               for j in range(TOP_K)]
    return pl.pallas_call(
        _combine_kernel,
        grid=(n // tm,),
        in_specs=[pl.BlockSpec((tm, D_MODEL), lambda i: (i, 0)),
                  *y_specs,
                  pl.BlockSpec((tm, LANES), lambda i: (i + off, 0)),
                  pl.BlockSpec((1, 6, D_MODEL), lambda i: (i // per_seq, 0, 0))],
        out_specs=pl.BlockSpec((tm, D_MODEL), lambda i: (i, 0)),
        out_shape=jax.ShapeDtypeStruct((n, D_MODEL), F32),
        compiler_params=_params("arbitrary"),
        name="combine",
    )(x1, ys, ys, ys, ys, tw, ada_g)


def _rope_tables(seq):
    half = HEAD_DIM // 2
    inv = ROPE_THETA ** (-jnp.arange(half, dtype=F32) / half)
    ang = jnp.arange(seq, dtype=F32)[:, None] * inv[None, :]
    cos, sin = jnp.cos(ang), jnp.sin(ang)
    cos_h = jnp.concatenate([cos, cos], axis=-1)
    sin_h = jnp.concatenate([-sin, sin], axis=-1)
    reps = DIFF_WIDTH // HEAD_DIM
    return jnp.tile(cos_h, (1, reps)), jnp.tile(sin_h, (1, reps))


def _na_bias_table(rpb):
    cols = jnp.arange(GRID_W, dtype=jnp.int32)
    c_start = jnp.clip(cols - NA_WIN_COLS // 2, 0, GRID_W - NA_WIN_COLS)
    col_mask = (cols[None, :] >= c_start[:, None]) & (cols[None, :] < c_start[:, None] + NA_WIN_COLS)
    col_idx = jnp.clip(cols[None, :] - cols[:, None], -(NA_WIN_COLS - 1), NA_WIN_COLS - 1) + NA_WIN_COLS - 1
    delta = jnp.arange(NA_WIN_ROWS, dtype=jnp.int32)
    j = jnp.arange(NA_WIN_ROWS, dtype=jnp.int32)
    row_idx = j[None, :] - delta[:, None] + NA_WIN_ROWS - 1
    row_hot = (row_idx[:, :, None] == jnp.arange(2 * NA_WIN_ROWS - 1, dtype=jnp.int32)).astype(F32)
    col_hot = (col_idx[:, :, None] == jnp.arange(2 * NA_WIN_COLS - 1, dtype=jnp.int32)).astype(F32)
    bias = jnp.einsum('djr,hrc,qkc->hdjqk', row_hot, rpb.astype(F32), col_hot,
                      precision=lax.Precision.HIGHEST)
    bias = jnp.where(col_mask[None, None, None], bias * LOG2E, NEG_INF)
    bias = bias.transpose(0, 1, 2, 4, 3).reshape(NA_HEADS, NA_WIN_ROWS, NA_WIN_ROWS * GRID_W, GRID_W)
    bias = bias.reshape(NA_HEADS // 2, 2, NA_WIN_ROWS, NA_WIN_ROWS * GRID_W, GRID_W)
    return jnp.concatenate([bias[:, 0], bias[:, 1]], axis=-1)


def _routing(top_idx, rank, counts, n):
    n_blocks = n * TOP_K // EXPERT_BLOCK + N_EXPERTS
    experts = jnp.arange(N_EXPERTS, dtype=jnp.int32)
    padded = (counts + EXPERT_BLOCK - 1) // EXPERT_BLOCK * EXPERT_BLOCK
    pad_end = jnp.cumsum(padded)
    pad_start = pad_end - padded
    start_of = jnp.sum(jnp.where(top_idx[:, :, None] == experts, pad_start, 0), axis=-1)
    dest = (start_of + rank).T.reshape(-1)
    block_lo = jnp.arange(n_blocks, dtype=jnp.int32) * EXPERT_BLOCK
    block_e = jnp.minimum(jnp.sum((pad_end[None, :] <= block_lo[:, None]).astype(jnp.int32), axis=1),
                          N_EXPERTS - 1).astype(jnp.int32)
    n_active = (pad_end[-1] // EXPERT_BLOCK).astype(jnp.int32).reshape(1)
    return dest, block_e, n_active, n_blocks * EXPERT_BLOCK


def kernel(x_prompt, x_sample, c_prompt, c_sample, w_ada, b_ada, g_attn_norm, w_qkv, na_q_norm, na_k_norm, na_rpb, diff_q_norm, diff_k_norm, lambda_q1, lambda_k1, lambda_q2, lambda_k2, diff_subln, w_o, g_ffn_norm, w_router, b_router, w_gate, b_gate, w_up, b_up, w_down, b_down):
    l = 0
    groups = [(x_prompt, c_prompt), (x_sample, c_sample)]
    nb = [x.shape[0] for x, _ in groups]

    ada_all = _ada(jnp.concatenate([c for _, c in groups], axis=0), w_ada[l], b_ada[l])
    ada_all = ada_all.reshape(sum(nb), 6, D_MODEL)

    w_qkv_bf = w_qkv[l].astype(BF16)
    w_o_bf = w_o[l].astype(BF16)
    scale = HEAD_DIM ** -0.5
    reps = NA_WIDTH // HEAD_DIM
    gains = jnp.stack([jnp.tile(na_q_norm[l], reps) * (scale * LOG2E),
                       jnp.tile(na_k_norm[l], reps),
                       jnp.tile(diff_q_norm[l], reps) * (scale * LOG2E),
                       jnp.tile(diff_k_norm[l], reps)]).astype(F32)
    head_id = jnp.arange(MXU_DIM, dtype=jnp.int32) // HEAD_DIM
    bd = (head_id[:, None] == head_id[None, :]).astype(BF16)
    bias_t = _na_bias_table(na_rpb[l])
    lam_vecs = jnp.stack([lambda_q1[l], lambda_k1[l], lambda_q2[l], lambda_k2[l]]).astype(F32)
    subln_col = (diff_subln[l].astype(F32) * (1.0 - LAMBDA_INIT)).reshape(LANES, 1)
    wr = w_router[l].astype(F32)
    wr_pad = jnp.zeros((D_MODEL, LANES), F32).at[:, :N_EXPERTS].set(wr)
    wr_hi = wr_pad.astype(BF16)
    wr_lo = (wr_pad - wr_hi.astype(F32)).astype(BF16)
    br_pad = jnp.full((1, LANES), NEG_INF, F32).at[0, :N_EXPERTS].set(b_router[l].astype(F32))
    g_attn = g_attn_norm[l].reshape(1, D_MODEL).astype(F32)
    g_ffn = g_ffn_norm[l].reshape(1, D_MODEL).astype(F32)
    max_seq = max(x.shape[1] for x, _ in groups)
    cos_t, sin_t = _rope_tables(max_seq)

    rows = lax.broadcasted_iota(jnp.int32, (ROW_BLOCK, ROW_BLOCK), 0)
    cols = lax.broadcasted_iota(jnp.int32, (ROW_BLOCK, ROW_BLOCK), 1)
    tri = (cols < rows).astype(BF16)
    cnt = jnp.zeros((1, LANES), F32)

    x1s, h2s, tis, tws = [], [], [], []
    b0 = 0
    for (x, _), b in zip(groups, nb):
        seq = x.shape[1]
        x2d = x.reshape(b * seq, D_MODEL)
        ada_g = ada_all[b0:b0 + b]
        b0 += b
        qkv = _qkv(x2d, ada_g, g_attn, w_qkv_bf, gains, cos_t, sin_t, bd, seq)
        o_na = _na(qkv, bias_t, b, seq)
        o_df = _diff(qkv, lam_vecs, subln_col, b, seq)
        x1, h2, ti, tw, cnt = _wo(o_na, o_df, x2d, ada_g, w_o_bf, g_ffn, wr_hi, wr_lo, br_pad, tri, cnt, seq)
        x1s.append(x1)
        h2s.append(h2)
        tis.append(ti)
        tws.append(tw)

    h2_all = jnp.concatenate(h2s, axis=0)
    ti_all = jnp.concatenate(tis, axis=0)
    tw_all = jnp.concatenate(tws, axis=0)
    n = h2_all.shape[0]
    counts = cnt[0, :N_EXPERTS].astype(jnp.int32)
    dest, block_e, n_active, cap = _routing(ti_all[:, :TOP_K], ti_all[:, TOP_K:2 * TOP_K], counts, n)

    xs = _sc_scatter(h2_all, dest, cap)
    ys_sorted = _experts(block_e, n_active, xs,
                         w_gate[l], b_gate[l].reshape(N_EXPERTS, 1, D_MODEL).astype(F32),
                         w_up[l], b_up[l].reshape(N_EXPERTS, 1, D_MODEL).astype(F32),
                         w_down[l], b_down[l].reshape(N_EXPERTS, 1, D_MODEL).astype(F32))
    ys = _sc_gather(ys_sorted, dest)

    outs = []
    b0 = 0
    row_off = 0
    for (x, _), b, x1 in zip(groups, nb, x1s):
        seq = x.shape[1]
        ada_g = ada_all[b0:b0 + b]
        b0 += b
        y = _combine(x1, ys, tw_all, ada_g, seq, row_off)
        row_off += b * seq
        outs.append(y.reshape(b, seq, D_MODEL))
    return tuple(outs)
```

```python
import functools
import math

import jax
import jax.numpy as jnp
from jax import lax
from jax.experimental import pallas as pl
from jax.experimental.pallas import tpu as pltpu
from jax.experimental.pallas import tpu_sc as plsc

F32 = jnp.float32
BF16 = jnp.bfloat16
U32 = jnp.uint32

D_MODEL = 1024
HEAD_DIM = 64
NA_HEADS = 8
NA_WIDTH = 512
DIFF_HEADS = 4
DIFF_WIDTH = 512
QKV_COLS = 3072
GRID_W = 64
NA_WIN_ROWS = 8
NA_WIN_COLS = 16
ROPE_THETA = 10000.0
N_EXPERTS = 32
TOP_K = 4
SWIGLU_LIMIT = 7.0
SWIGLU_ALPHA = 1.702
EPS = 1e-5
NEG_INF = -1e30
LAMBDA_INIT = 0.8 - 0.6 * math.exp(-0.3 * 0)
LOG2E = 1.4426950408889634

LANES = 128
MXU_DIM = 256
VMEM_LIMIT = 56 * 1024 * 1024

ROW_BLOCK = 512
Q_BLOCK = 512
EXPERT_BLOCK = 512
NA_ROWS_PER_TRIP = 4
VT_ROWS = LANES + 16
KV_CHUNK = 256
NORM_SLACK = 1.01
MAX_SAFE_BOUND = 60.0


PACKED = D_MODEL // 2
SC_CORES = 2
SC_SUBCORES = 16
SC_WORKERS = SC_CORES * SC_SUBCORES
GATHER_ROWS = 64


def _params(*sem):
    return pltpu.CompilerParams(dimension_semantics=sem, vmem_limit_bytes=VMEM_LIMIT)


def _pack_halves(x):
    w = x.shape[1] // 2
    bits = lax.bitcast_convert_type(x, U32)
    return (bits[:, :w] >> 16) | bits[:, w:]


def _col_reduce(x, op):
    while x.shape[0] >= 64:
        x = op(x.reshape(8, x.shape[0] // 8, x.shape[1]), axis=0)
    return op(x, axis=0, keepdims=True)


def _unpack_halves(word):
    lo = lax.bitcast_convert_type(word << 16, F32)
    hi = lax.bitcast_convert_type(word & jnp.uint32(0xFFFF0000), F32)
    return lo, hi


def _ada_kernel(c_ref, w_ref, b_ref, o_ref):
    c = c_ref[...]
    s = c * jax.nn.sigmoid(c)
    o_ref[...] = jnp.dot(s, w_ref[...], preferred_element_type=F32,
                         precision=lax.Precision.HIGHEST) + b_ref[...]


def _ada(c_all, w_ada, b_ada):
    nb = c_all.shape[0]
    n_out = w_ada.shape[1]
    blk = D_MODEL
    return pl.pallas_call(
        _ada_kernel,
        grid=(n_out // blk,),
        in_specs=[pl.BlockSpec((nb, D_MODEL), lambda j: (0, 0)),
                  pl.BlockSpec((D_MODEL, blk), lambda j: (0, j)),
                  pl.BlockSpec((1, blk), lambda j: (0, j))],
        out_specs=pl.BlockSpec((nb, blk), lambda j: (0, j)),
        out_shape=jax.ShapeDtypeStruct((nb, n_out), F32),
        compiler_params=_params("arbitrary"),
        name="ada",
    )(c_all, w_ada, b_ada.reshape(1, n_out))


def _head_sumsq(y, bd):
    sq = (y * y).astype(BF16)
    parts = [jnp.dot(sq[:, c:c + MXU_DIM], bd, preferred_element_type=F32)
             for c in range(0, y.shape[1], MXU_DIM)]
    return jnp.concatenate(parts, axis=1)


def _qkv_kernel(x_ref, ada_ref, g_ref, w_ref, gain_ref, cos_ref, sin_ref, bd_ref, o_ref):
    x = x_ref[...]
    ms = jnp.mean(x * x, axis=-1, keepdims=True)
    xn = x * lax.rsqrt(ms + EPS) * g_ref[...]
    sh = ada_ref[0, 0:1, :]
    sc = ada_ref[0, 1:2, :]
    h = (xn * (1.0 + sc) + sh).astype(BF16)
    bd = bd_ref[...]
    lane = lax.broadcasted_iota(jnp.int32, (x.shape[0], NA_WIDTH), 1)
    first_half = (lane & (HEAD_DIM // 2)) == 0
    for grp in range(6):
        cols = slice(grp * 512, (grp + 1) * 512)
        acc = jnp.dot(h, w_ref[:, cols], preferred_element_type=F32)
        if grp in (2, 5):
            o_ref[:, cols] = acc.astype(BF16)
            continue
        gi = {0: 0, 1: 1, 3: 2, 4: 3}[grp]
        ss = _head_sumsq(acc, bd)
        y = acc * lax.rsqrt(ss * (1.0 / HEAD_DIM) + EPS) * gain_ref[gi:gi + 1, :]
        if grp in (3, 4):
            partner = jnp.where(first_half,
                                pltpu.roll(y, NA_WIDTH - HEAD_DIM // 2, axis=1),
                                pltpu.roll(y, HEAD_DIM // 2, axis=1))
            y = y * cos_ref[...] + partner * sin_ref[...]
        o_ref[:, cols] = y.astype(BF16)


def _qkv(x2d, ada_g, g_attn, w_qkv_bf, gains, cos_t, sin_t, bd, seq):
    n = x2d.shape[0]
    tm = ROW_BLOCK
    per_seq = seq // tm
    return pl.pallas_call(
        _qkv_kernel,
        grid=(n // tm,),
        in_specs=[pl.BlockSpec((tm, D_MODEL), lambda i: (i, 0)),
                  pl.BlockSpec((1, 6, D_MODEL), lambda i: (i // per_seq, 0, 0)),
                  pl.BlockSpec((1, D_MODEL), lambda i: (0, 0)),
                  pl.BlockSpec((D_MODEL, QKV_COLS), lambda i: (0, 0)),
                  pl.BlockSpec((4, NA_WIDTH), lambda i: (0, 0)),
                  pl.BlockSpec((tm, DIFF_WIDTH), lambda i: (i % per_seq, 0)),
                  pl.BlockSpec((tm, DIFF_WIDTH), lambda i: (i % per_seq, 0)),
                  pl.BlockSpec((MXU_DIM, MXU_DIM), lambda i: (0, 0))],
        out_specs=pl.BlockSpec((tm, QKV_COLS), lambda i: (i, 0)),
        out_shape=jax.ShapeDtypeStruct((n, QKV_COLS), BF16),
        compiler_params=_params("arbitrary"),
        name="qkv",
    )(x2d, ada_g, g_attn, w_qkv_bf, gains, cos_t, sin_t, bd)


def _na_kernel(q_ref, k_ref, v_ref, bias_ref, o_ref, *, rows):
    lane = lax.broadcasted_iota(jnp.int32, (GRID_W, LANES), 1)
    head0 = lane < HEAD_DIM
    win = NA_WIN_ROWS * GRID_W

    def window_start(r):
        return jnp.clip(r - NA_WIN_ROWS // 2, 0, rows - NA_WIN_ROWS)

    def scores(r):
        r_start = window_start(r)
        q = q_ref[pl.ds(pl.multiple_of(r * GRID_W, GRID_W), GRID_W), :]
        kw = k_ref[pl.ds(pl.multiple_of(r_start * GRID_W, GRID_W), win), :]
        zero = jnp.zeros_like(q)
        qm = jnp.concatenate([jnp.where(head0, q, zero), jnp.where(head0, zero, q)], axis=0)
        s = lax.dot_general(kw, qm, (((1,), (1,)), ((), ())), preferred_element_type=F32)
        return s + bias_ref[0, r - r_start]

    def finish(r, s):
        vw = v_ref[pl.ds(pl.multiple_of(window_start(r) * GRID_W, GRID_W), win), :]
        m = _col_reduce(s, jnp.max)
        p = jnp.exp2(s - m)
        l = _col_reduce(p, jnp.sum)
        p = (p * (1.0 / l)).astype(BF16)
        o2 = lax.dot_general(p, vw, (((0,), (0,)), ((), ())), preferred_element_type=F32)
        o = jnp.where(head0, o2[:GRID_W], o2[GRID_W:])
        o_ref[pl.ds(pl.multiple_of(r * GRID_W, GRID_W), GRID_W), :] = o.astype(BF16)

    def body(i, carry):
        trip_rows = [i * NA_ROWS_PER_TRIP + u for u in range(NA_ROWS_PER_TRIP)]
        trip_scores = [scores(r) for r in trip_rows]
        for r, s in zip(trip_rows, trip_scores):
            finish(r, s)
        return carry

    lax.fori_loop(0, rows // NA_ROWS_PER_TRIP, body, 0)


def _na(qkv, bias_t, batch, seq):
    rows = seq // GRID_W
    n_pairs = NA_HEADS // 2
    return pl.pallas_call(
        functools.partial(_na_kernel, rows=rows),
        grid=(batch, n_pairs),
        in_specs=[pl.BlockSpec((seq, LANES), lambda b, hp: (b, hp)),
                  pl.BlockSpec((seq, LANES), lambda b, hp: (b, n_pairs + hp)),
                  pl.BlockSpec((seq, LANES), lambda b, hp: (b, 2 * n_pairs + hp)),
                  pl.BlockSpec((1, NA_WIN_ROWS, NA_WIN_ROWS * GRID_W, LANES), lambda b, hp: (hp, 0, 0, 0))],
        out_specs=pl.BlockSpec((seq, LANES), lambda b, hp: (b, hp)),
        out_shape=jax.ShapeDtypeStruct((batch * seq, NA_WIDTH), BF16),
        compiler_params=_params("arbitrary", "arbitrary"),
        name="na_attn",
    )(qkv, qkv, qkv, bias_t)


def _diff_kernel(q_ref, k_ref, v_ref, lam_ref, g_ref, o_ref, vt_ref, kn_ref, oa_ref):
    @pl.when(pl.program_id(2) == 0)
    def _():
        vt_ref[:LANES, :] = v_ref[...].astype(F32).T.astype(BF16)
        ones_row = lax.broadcasted_iota(jnp.int32, (VT_ROWS - LANES, v_ref.shape[0]), 0) == 0
        vt_ref[LANES:, :] = jnp.where(ones_row, 1.0, 0.0).astype(BF16)
        kf = k_ref[...].astype(F32)
        d_id = lax.broadcasted_iota(jnp.int32, (LANES, LANES), 0) // HEAD_DIM
        c_id = lax.broadcasted_iota(jnp.int32, (LANES, LANES), 1)
        comp_sel = jnp.where(d_id == c_id, 1.0, 0.0).astype(BF16)
        kn2 = jnp.dot((kf * kf).astype(BF16), comp_sel, preferred_element_type=F32)
        kn_ref[...] = jnp.sqrt(_col_reduce(kn2, jnp.max)) * NORM_SLACK

    lq1 = lam_ref[0:1, :]
    lk1 = lam_ref[1:2, :]
    lq2 = lam_ref[2:3, :]
    lk2 = lam_ref[3:4, :]
    lam = (jnp.exp(jnp.sum(lq1 * lk1, axis=-1, keepdims=True))
           - jnp.exp(jnp.sum(lq2 * lk2, axis=-1, keepdims=True)) + LAMBDA_INIT)

    q = q_ref[...]
    lane = lax.broadcasted_iota(jnp.int32, q.shape, 1)
    zero = jnp.zeros_like(q)
    tq = q.shape[0]
    qcat = jnp.concatenate([jnp.where(lane < HEAD_DIM, q, zero), jnp.where(lane < HEAD_DIM, zero, q)], axis=0)
    def scores(c, chunk):
        kc = k_ref[c * chunk:(c + 1) * chunk, :]
        return lax.dot_general(kc, qcat, (((1,), (1,)), ((), ())), preferred_element_type=F32)

    def chunked(chunk, step):
        n_chunks = k_ref.shape[0] // chunk
        s_next = scores(0, chunk)
        state = None
        for c in range(n_chunks):
            s = s_next
            if c + 1 < n_chunks:
                s_next = scores(c + 1, chunk)
            state = step(c, s, vt_ref[:, c * chunk:(c + 1) * chunk], state)
        return state

    qf = qcat.astype(F32)
    ones8 = jnp.ones((8, LANES), BF16)
    qn2 = lax.dot_general(ones8, (qf * qf).astype(BF16), (((1,), (1,)), ((), ())), preferred_element_type=F32)
    col = lax.broadcasted_iota(jnp.int32, (1, 2 * tq), 1)
    bound = jnp.sqrt(qn2[0:1, :]) * NORM_SLACK * jnp.where(col < tq, kn_ref[0:1, 0:1], kn_ref[0:1, 1:2])
    in_range = jnp.max(bound) <= MAX_SAFE_BOUND

    @pl.when(in_range)
    def _():
        def step(c, s, vtc, acc):
            pv = jnp.dot(vtc, jnp.exp2(s - bound).astype(BF16), preferred_element_type=F32)
            return pv if c == 0 else acc + pv
        oa_ref[...] = chunked(KV_CHUNK, step)

    @pl.when(jnp.logical_not(in_range))
    def _():
        def step(c, s, vtc, state):
            mc = _col_reduce(s, jnp.max)
            m_new = mc if c == 0 else jnp.maximum(state[0], mc)
            pv = jnp.dot(vtc, jnp.exp2(s - m_new).astype(BF16), preferred_element_type=F32)
            return (m_new, pv if c == 0 else jnp.exp2(state[0] - m_new) * state[1] + pv)
        oa_ref[...] = chunked(KV_CHUNK, step)[1]

    oa = oa_ref[...]
    o0, l0 = oa[:LANES, :tq], oa[LANES:LANES + 1, :tq]
    o1, l1 = oa[:LANES, tq:], oa[LANES:LANES + 1, tq:]
    o = o0 * (1.0 / l0) - (lam / l1) * o1
    ms = jnp.mean(o * o, axis=0, keepdims=True)
    y = o * lax.rsqrt(ms + EPS) * g_ref[...]
    o_ref[...] = y.T.astype(BF16)


def _diff(qkv, lam_vecs, subln_col, batch, seq):
    tq = Q_BLOCK
    nq = seq // tq
    base = 3 * NA_WIDTH // LANES
    nh = DIFF_HEADS
    return pl.pallas_call(
        _diff_kernel,
        grid=(batch, nh, nq),
        in_specs=[pl.BlockSpec((tq, LANES), lambda b, h, i: (b * nq + i, base + h)),
                  pl.BlockSpec((seq, LANES), lambda b, h, i: (b, base + nh + h)),
                  pl.BlockSpec((seq, LANES), lambda b, h, i: (b, base + 2 * nh + h)),
                  pl.BlockSpec((4, HEAD_DIM), lambda b, h, i: (0, 0)),
                  pl.BlockSpec((LANES, 1), lambda b, h, i: (0, 0))],
        out_specs=pl.BlockSpec((tq, LANES), lambda b, h, i: (b * nq + i, h)),
        out_shape=jax.ShapeDtypeStruct((batch * seq, DIFF_WIDTH), BF16),
        scratch_shapes=[pltpu.VMEM((VT_ROWS, seq), BF16),
                        pltpu.VMEM((1, LANES), F32),
                        pltpu.VMEM((VT_ROWS, 2 * tq), F32)],
        compiler_params=_params("arbitrary", "arbitrary", "arbitrary"),
        name="diff_attn",
    )(qkv, qkv, qkv, lam_vecs, subln_col)


def _wo_kernel(ona_ref, odf_ref, x_ref, ada_ref, wo_ref, g_ref, wrh_ref, wrl_ref, br_ref, tri_ref, cnt0_ref,
               x1_ref, h2_ref, ti_ref, tw_ref, cnt_ref):
    mix = (jnp.dot(ona_ref[...], wo_ref[:NA_WIDTH, :], preferred_element_type=F32)
           + jnp.dot(odf_ref[...], wo_ref[NA_WIDTH:, :], preferred_element_type=F32))
    gt1 = ada_ref[0, 2:3, :]
    sh2 = ada_ref[0, 3:4, :]
    sc2 = ada_ref[0, 4:5, :]
    x1 = x_ref[...] + gt1 * mix
    x1_ref[...] = x1
    ms = jnp.mean(x1 * x1, axis=-1, keepdims=True)
    h2 = x1 * lax.rsqrt(ms + EPS) * g_ref[...] * (1.0 + sc2) + sh2
    hi = h2.astype(BF16)
    h2_ref[...] = _pack_halves(hi.astype(F32))
    lo = (h2 - hi.astype(F32)).astype(BF16)
    logits = (jnp.dot(hi, wrh_ref[...], preferred_element_type=F32)
              + jnp.dot(hi, wrl_ref[...], preferred_element_type=F32)
              + jnp.dot(lo, wrh_ref[...], preferred_element_type=F32)) + br_ref[...]
    lane = lax.broadcasted_iota(jnp.int32, logits.shape, 1).astype(F32)
    vals = []
    idxs = []
    cur = logits
    for _ in range(TOP_K):
        m = jnp.max(cur, axis=-1, keepdims=True)
        idx = jnp.min(jnp.where(cur == m, lane, float(LANES)), axis=-1, keepdims=True)
        vals.append(m)
        idxs.append(idx)
        cur = jnp.where(lane == idx, -jnp.inf, cur)
    es = [jnp.exp(v - vals[0]) for v in vals]
    inv = 1.0 / (es[0] + es[1] + es[2] + es[3])

    @pl.when(pl.program_id(0) == 0)
    def _():
        cnt_ref[...] = cnt0_ref[...]

    sel = jnp.zeros(logits.shape, F32)
    for j in range(TOP_K):
        sel = sel + jnp.where(lane == idxs[j], 1.0, 0.0)
    before = jnp.dot(tri_ref[...], sel.astype(BF16), preferred_element_type=F32) + cnt_ref[...]
    cnt_ref[...] = cnt_ref[...] + jnp.sum(sel, axis=0, keepdims=True)
    ranks = [jnp.sum(jnp.where(lane == idxs[j], before, 0.0), axis=-1, keepdims=True) for j in range(TOP_K)]

    ti = jnp.zeros(logits.shape, F32)
    tw = jnp.zeros(logits.shape, F32)
    for j in range(TOP_K):
        ti = jnp.where(lane == float(j), idxs[j], ti)
        ti = jnp.where(lane == float(TOP_K + j), ranks[j], ti)
        tw = jnp.where(lane == float(j), es[j] * inv, tw)
    ti_ref[...] = ti.astype(jnp.int32)
    tw_ref[...] = tw


def _wo(o_na, o_df, x2d, ada_g, w_o_bf, g_ffn, wr_hi, wr_lo, br_pad, tri, cnt0, seq):
    n = x2d.shape[0]
    tm = ROW_BLOCK
    per_seq = seq // tm
    row = lambda i: (i, 0)
    const = lambda i: (0, 0)
    return pl.pallas_call(
        _wo_kernel,
        grid=(n // tm,),
        in_specs=[pl.BlockSpec((tm, NA_WIDTH), row),
                  pl.BlockSpec((tm, DIFF_WIDTH), row),
                  pl.BlockSpec((tm, D_MODEL), row),
                  pl.BlockSpec((1, 6, D_MODEL), lambda i: (i // per_seq, 0, 0)),
                  pl.BlockSpec((D_MODEL, D_MODEL), const),
                  pl.BlockSpec((1, D_MODEL), const),
                  pl.BlockSpec((D_MODEL, LANES), const),
                  pl.BlockSpec((D_MODEL, LANES), const),
                  pl.BlockSpec((1, LANES), const),
                  pl.BlockSpec((tm, tm), const),
                  pl.BlockSpec((1, LANES), const)],
        out_specs=[pl.BlockSpec((tm, D_MODEL), row),
                   pl.BlockSpec((tm, PACKED), row),
                   pl.BlockSpec((tm, LANES), row),
                   pl.BlockSpec((tm, LANES), row),
                   pl.BlockSpec((1, LANES), const)],
        out_shape=[jax.ShapeDtypeStruct((n, D_MODEL), F32),
                   jax.ShapeDtypeStruct((n, PACKED), U32),
                   jax.ShapeDtypeStruct((n, LANES), jnp.int32),
                   jax.ShapeDtypeStruct((n, LANES), F32),
                   jax.ShapeDtypeStruct((1, LANES), F32)],
        compiler_params=_params("arbitrary"),
        name="wo_router",
    )(o_na, o_df, x2d, ada_g, w_o_bf, g_ffn, wr_hi, wr_lo, br_pad, tri, cnt0)


def _expert_kernel(be_ref, na_ref, xs_ref, wg_ref, bg_ref, wu_ref, bu_ref, wd_ref, bd_ref, o_ref,
                   wg_bf, wu_bf, wd_bf):
    i = pl.program_id(0)
    active = i < na_ref[0]
    new_expert = jnp.logical_or(i == 0, be_ref[i] != be_ref[jnp.maximum(i - 1, 0)])

    @pl.when(jnp.logical_and(active, new_expert))
    def _():
        wg_bf[...] = wg_ref[0].astype(BF16)
        wu_bf[...] = wu_ref[0].astype(BF16)
        wd_bf[...] = wd_ref[0].astype(BF16)

    @pl.when(active)
    def _():
        x_lo, x_hi = _unpack_halves(xs_ref[...])
        x_lo = x_lo.astype(BF16)
        x_hi = x_hi.astype(BF16)

        def proj(w_bf):
            return (jnp.dot(x_lo, w_bf[:PACKED, :], preferred_element_type=F32)
                    + jnp.dot(x_hi, w_bf[PACKED:, :], preferred_element_type=F32))

        g = jnp.minimum(proj(wg_bf) + bg_ref[0], SWIGLU_LIMIT)
        u = jnp.clip(proj(wu_bf) + bu_ref[0], -SWIGLU_LIMIT, SWIGLU_LIMIT)
        act = g * jax.nn.sigmoid(SWIGLU_ALPHA * g) * (u + 1.0)
        out = jnp.dot(act.astype(BF16), wd_bf[...], preferred_element_type=F32) + bd_ref[0]
        o_ref[...] = _pack_halves(out.astype(BF16).astype(F32))

    @pl.when(i >= na_ref[0])
    def _():
        o_ref[...] = jnp.zeros_like(o_ref)


def _experts(block_e, n_active, xs, wg, bg, wu, bu, wd, bd):
    cap = xs.shape[0]
    n_blocks = cap // EXPERT_BLOCK
    xmap = lambda i, be, na: (jnp.minimum(i, na[0] - 1), 0)
    wmap = lambda i, be, na: (be[i], 0, 0)
    grid_spec = pltpu.PrefetchScalarGridSpec(
        num_scalar_prefetch=2,
        grid=(n_blocks,),
        in_specs=[pl.BlockSpec((EXPERT_BLOCK, PACKED), xmap),
                  pl.BlockSpec((1, D_MODEL, D_MODEL), wmap),
                  pl.BlockSpec((1, 1, D_MODEL), wmap),
                  pl.BlockSpec((1, D_MODEL, D_MODEL), wmap),
                  pl.BlockSpec((1, 1, D_MODEL), wmap),
                  pl.BlockSpec((1, D_MODEL, D_MODEL), wmap),
                  pl.BlockSpec((1, 1, D_MODEL), wmap)],
        out_specs=pl.BlockSpec((EXPERT_BLOCK, PACKED), lambda i, be, na: (i, 0)),
        scratch_shapes=[pltpu.VMEM((D_MODEL, D_MODEL), BF16)] * 3,
    )
    return pl.pallas_call(
        _expert_kernel,
        grid_spec=grid_spec,
        out_shape=jax.ShapeDtypeStruct((cap, PACKED), U32),
        compiler_params=_params("arbitrary"),
        name="experts",
    )(block_e, n_active, xs, wg, bg, wu, bu, wd, bd)


def _sc_gather(table, idx):
    n_out = idx.shape[0]
    width = table.shape[1]
    per_worker = n_out // SC_WORKERS
    n_chunks = per_worker // GATHER_ROWS
    assert per_worker * SC_WORKERS == n_out and n_chunks * GATHER_ROWS == per_worker and n_chunks % 2 == 0
    idx3 = idx.reshape(SC_WORKERS, n_chunks, GATHER_ROWS)
    mesh = plsc.VectorSubcoreMesh(core_axis_name="core", subcore_axis_name="subcore")

    @functools.partial(
        pl.kernel, mesh=mesh,
        out_type=jax.ShapeDtypeStruct((n_out, width), table.dtype),
        scratch_types=[pltpu.VMEM((n_chunks, GATHER_ROWS), jnp.int32),
                       pltpu.VMEM((2, GATHER_ROWS, width), table.dtype),
                       pltpu.SemaphoreType.DMA((2,)),
                       pltpu.SemaphoreType.DMA((2,))])
    def gather_kernel(table_hbm, idx_hbm, out_hbm, idx_v, rows_v, gsem, wsem):
        wid = lax.axis_index("subcore") * SC_CORES + lax.axis_index("core")
        base = wid * per_worker
        pltpu.sync_copy(idx_hbm.at[wid], idx_v)

        def gather(j, slot):
            return pltpu.make_async_copy(table_hbm.at[idx_v.at[j]], rows_v.at[slot], gsem.at[slot])

        def write(j, slot):
            dst = out_hbm.at[pl.ds(pl.multiple_of(base + j * GATHER_ROWS, GATHER_ROWS), GATHER_ROWS)]
            return pltpu.make_async_copy(rows_v.at[slot], dst, wsem.at[slot])

        gather(0, 0).start()

        @pl.loop(0, n_chunks, step=2)
        def _(j):
            for slot in range(2):
                jj = j + slot
                gather(jj, slot).wait()

                @pl.when(jj >= 1)
                def _():
                    write(jj - 1, 1 - slot).wait()

                @pl.when(jj + 1 < n_chunks)
                def _():
                    gather(jj + 1, 1 - slot).start()

                write(jj, slot).start()

        write(n_chunks - 1, 1).wait()

    return gather_kernel(table, idx3)


def _sc_scatter(rows, idx, n_out):
    n_src, width = rows.shape
    n_idx = idx.shape[0]
    per_worker = n_idx // SC_WORKERS
    n_chunks = per_worker // GATHER_ROWS
    assert per_worker * SC_WORKERS == n_idx and n_chunks * GATHER_ROWS == per_worker and n_chunks % 2 == 0
    assert n_src % per_worker == 0
    idx3 = idx.reshape(SC_WORKERS, n_chunks, GATHER_ROWS)
    mesh = plsc.VectorSubcoreMesh(core_axis_name="core", subcore_axis_name="subcore")

    @functools.partial(
        pl.kernel, mesh=mesh,
        out_type=jax.ShapeDtypeStruct((n_out, width), rows.dtype),
        scratch_types=[pltpu.VMEM((n_chunks, GATHER_ROWS), jnp.int32),
                       pltpu.VMEM((2, GATHER_ROWS, width), rows.dtype),
                       pltpu.SemaphoreType.DMA((2,)),
                       pltpu.SemaphoreType.DMA((2,))])
    def scatter_kernel(rows_hbm, idx_hbm, out_hbm, idx_v, rows_v, rsem, wsem):
        wid = lax.axis_index("subcore") * SC_CORES + lax.axis_index("core")
        base = lax.rem(wid * per_worker, n_src)
        pltpu.sync_copy(idx_hbm.at[wid], idx_v)

        def read(j, slot):
            src = rows_hbm.at[pl.ds(pl.multiple_of(base + j * GATHER_ROWS, GATHER_ROWS), GATHER_ROWS)]
            return pltpu.make_async_copy(src, rows_v.at[slot], rsem.at[slot])

        def write(j, slot):
            return pltpu.make_async_copy(rows_v.at[slot], out_hbm.at[idx_v.at[j]], wsem.at[slot])

        read(0, 0).start()

        @pl.loop(0, n_chunks, step=2)
        def _(j):
            for slot in range(2):
                jj = j + slot
                read(jj, slot).wait()

                @pl.when(jj >= 1)
                def _():
                    write(jj - 1, 1 - slot).wait()

                @pl.when(jj + 1 < n_chunks)
                def _():
                    read(jj + 1, 1 - slot).start()

                write(jj, slot).start()

        write(n_chunks - 1, 1).wait()

    return scatter_kernel(rows, idx3)


def _combine_kernel(x1_ref, y0_ref, y1_ref, y2_ref, y3_ref, tw_ref, ada_ref, o_ref):
    tw = tw_ref[...]
    acc_lo = jnp.zeros((x1_ref.shape[0], PACKED), F32)
    acc_hi = jnp.zeros((x1_ref.shape[0], PACKED), F32)
    for j, y_ref in enumerate((y0_ref, y1_ref, y2_ref, y3_ref)):
        lo, hi = _unpack_halves(y_ref[...])
        acc_lo = acc_lo + tw[:, j:j + 1] * lo
        acc_hi = acc_hi + tw[:, j:j + 1] * hi
    o_ref[:, :PACKED] = x1_ref[:, :PACKED] + ada_ref[0, 5:6, :PACKED] * acc_lo
    o_ref[:, PACKED:] = x1_ref[:, PACKED:] + ada_ref[0, 5:6, PACKED:] * acc_hi


def _combine(x1, ys, tw, ada_g, seq, row_off):
    n = x1.shape[0]
    tm = ROW_BLOCK
    per_seq = seq // tm
    off = row_off // tm
    per_choice = tw.shape[0] // tm
    y_specs = [pl.BlockSpec((tm, PACKED), functools.partial(lambda i, j: (j * per_choice + off + i, 0), j=j))
               for j in range(TOP_K)]
    return pl.pallas_call(
        _combine_kernel,
        grid=(n // tm,),
        in_specs=[pl.BlockSpec((tm, D_MODEL), lambda i: (i, 0)),
                  *y_specs,
                  pl.BlockSpec((tm, LANES), lambda i: (i + off, 0)),
                  pl.BlockSpec((1, 6, D_MODEL), lambda i: (i // per_seq, 0, 0))],
        out_specs=pl.BlockSpec((tm, D_MODEL), lambda i: (i, 0)),
        out_shape=jax.ShapeDtypeStruct((n, D_MODEL), F32),
        compiler_params=_params("arbitrary"),
        name="combine",
    )(x1, ys, ys, ys, ys, tw, ada_g)


def _rope_tables(seq):
    half = HEAD_DIM // 2
    inv = ROPE_THETA ** (-jnp.arange(half, dtype=F32) / half)
    ang = jnp.arange(seq, dtype=F32)[:, None] * inv[None, :]
    cos, sin = jnp.cos(ang), jnp.sin(ang)
    cos_h = jnp.concatenate([cos, cos], axis=-1)
    sin_h = jnp.concatenate([-sin, sin], axis=-1)
    reps = DIFF_WIDTH // HEAD_DIM
    return jnp.tile(cos_h, (1, reps)), jnp.tile(sin_h, (1, reps))


def _na_bias_table(rpb):
    cols = jnp.arange(GRID_W, dtype=jnp.int32)
    c_start = jnp.clip(cols - NA_WIN_COLS // 2, 0, GRID_W - NA_WIN_COLS)
    col_mask = (cols[None, :] >= c_start[:, None]) & (cols[None, :] < c_start[:, None] + NA_WIN_COLS)
    col_idx = jnp.clip(cols[None, :] - cols[:, None], -(NA_WIN_COLS - 1), NA_WIN_COLS - 1) + NA_WIN_COLS - 1
    delta = jnp.arange(NA_WIN_ROWS, dtype=jnp.int32)
    j = jnp.arange(NA_WIN_ROWS, dtype=jnp.int32)
    row_idx = j[None, :] - delta[:, None] + NA_WIN_ROWS - 1
    row_hot = (row_idx[:, :, None] == jnp.arange(2 * NA_WIN_ROWS - 1, dtype=jnp.int32)).astype(F32)
    col_hot = (col_idx[:, :, None] == jnp.arange(2 * NA_WIN_COLS - 1, dtype=jnp.int32)).astype(F32)
    bias = jnp.einsum('djr,hrc,qkc->hdjqk', row_hot, rpb.astype(F32), col_hot,
                      precision=lax.Precision.HIGHEST)
    bias = jnp.where(col_mask[None, None, None], bias * LOG2E, NEG_INF)
    bias = bias.transpose(0, 1, 2, 4, 3).reshape(NA_HEADS, NA_WIN_ROWS, NA_WIN_ROWS * GRID_W, GRID_W)
    bias = bias.reshape(NA_HEADS // 2, 2, NA_WIN_ROWS, NA_WIN_ROWS * GRID_W, GRID_W)
    return jnp.concatenate([bias[:, 0], bias[:, 1]], axis=-1)


def _routing(top_idx, rank, counts, n):
    n_blocks = n * TOP_K // EXPERT_BLOCK + N_EXPERTS
    experts = jnp.arange(N_EXPERTS, dtype=jnp.int32)
    padded = (counts + EXPERT_BLOCK - 1) // EXPERT_BLOCK * EXPERT_BLOCK
    pad_end = jnp.cumsum(padded)
    pad_start = pad_end - padded
    start_of = jnp.sum(jnp.where(top_idx[:, :, None] == experts, pad_start, 0), axis=-1)
    dest = (start_of + rank).T.reshape(-1)
    block_lo = jnp.arange(n_blocks, dtype=jnp.int32) * EXPERT_BLOCK
    block_e = jnp.minimum(jnp.sum((pad_end[None, :] <= block_lo[:, None]).astype(jnp.int32), axis=1),
                          N_EXPERTS - 1).astype(jnp.int32)
    n_active = (pad_end[-1] // EXPERT_BLOCK).astype(jnp.int32).reshape(1)
    return dest, block_e, n_active, n_blocks * EXPERT_BLOCK


def kernel(x_prompt, x_sample, c_prompt, c_sample, w_ada, b_ada, g_attn_norm, w_qkv, na_q_norm, na_k_norm, na_rpb, diff_q_norm, diff_k_norm, lambda_q1, lambda_k1, lambda_q2, lambda_k2, diff_subln, w_o, g_ffn_norm, w_router, b_router, w_gate, b_gate, w_up, b_up, w_down, b_down):
    l = 0
    groups = [(x_prompt, c_prompt), (x_sample, c_sample)]
    nb = [x.shape[0] for x, _ in groups]

    ada_all = _ada(jnp.concatenate([c for _, c in groups], axis=0), w_ada[l], b_ada[l])
    ada_all = ada_all.reshape(sum(nb), 6, D_MODEL)

    w_qkv_bf = w_qkv[l].astype(BF16)
    w_o_bf = w_o[l].astype(BF16)
    scale = HEAD_DIM ** -0.5
    reps = NA_WIDTH // HEAD_DIM
    gains = jnp.stack([jnp.tile(na_q_norm[l], reps) * (scale * LOG2E),
                       jnp.tile(na_k_norm[l], reps),
                       jnp.tile(diff_q_norm[l], reps) * (scale * LOG2E),
                       jnp.tile(diff_k_norm[l], reps)]).astype(F32)
    head_id = jnp.arange(MXU_DIM, dtype=jnp.int32) // HEAD_DIM
    bd = (head_id[:, None] == head_id[None, :]).astype(BF16)
    bias_t = _na_bias_table(na_rpb[l])
    lam_vecs = jnp.stack([lambda_q1[l], lambda_k1[l], lambda_q2[l], lambda_k2[l]]).astype(F32)
    subln_col = (diff_subln[l].astype(F32) * (1.0 - LAMBDA_INIT)).reshape(LANES, 1)
    wr = w_router[l].astype(F32)
    wr_pad = jnp.zeros((D_MODEL, LANES), F32).at[:, :N_EXPERTS].set(wr)
    wr_hi = wr_pad.astype(BF16)
    wr_lo = (wr_pad - wr_hi.astype(F32)).astype(BF16)
    br_pad = jnp.full((1, LANES), NEG_INF, F32).at[0, :N_EXPERTS].set(b_router[l].astype(F32))
    g_attn = g_attn_norm[l].reshape(1, D_MODEL).astype(F32)
    g_ffn = g_ffn_norm[l].reshape(1, D_MODEL).astype(F32)
    max_seq = max(x.shape[1] for x, _ in groups)
    cos_t, sin_t = _rope_tables(max_seq)

    rows = lax.broadcasted_iota(jnp.int32, (ROW_BLOCK, ROW_BLOCK), 0)
    cols = lax.broadcasted_iota(jnp.int32, (ROW_BLOCK, ROW_BLOCK), 1)
    tri = (cols < rows).astype(BF16)
    cnt = jnp.zeros((1, LANES), F32)

    x1s, h2s, tis, tws = [], [], [], []
    b0 = 0
    for (x, _), b in zip(groups, nb):
        seq = x.shape[1]
        x2d = x.reshape(b * seq, D_MODEL)
        ada_g = ada_all[b0:b0 + b]
        b0 += b
        qkv = _qkv(x2d, ada_g, g_attn, w_qkv_bf, gains, cos_t, sin_t, bd, seq)
        o_na = _na(qkv, bias_t, b, seq)
        o_df = _diff(qkv, lam_vecs, subln_col, b, seq)
        x1, h2, ti, tw, cnt = _wo(o_na, o_df, x2d, ada_g, w_o_bf, g_ffn, wr_hi, wr_lo, br_pad, tri, cnt, seq)
        x1s.append(x1)
        h2s.append(h2)
        tis.append(ti)
        tws.append(tw)

    h2_all = jnp.concatenate(h2s, axis=0)
    ti_all = jnp.concatenate(tis, axis=0)
    tw_all = jnp.concatenate(tws, axis=0)
    n = h2_all.shape[0]
    counts = cnt[0, :N_EXPERTS].astype(jnp.int32)
    dest, block_e, n_active, cap = _routing(ti_all[:, :TOP_K], ti_all[:, TOP_K:2 * TOP_K], counts, n)

    xs = _sc_scatter(h2_all, dest, cap)
    ys_sorted = _experts(block_e, n_active, xs,
                         w_gate[l], b_gate[l].reshape(N_EXPERTS, 1, D_MODEL).astype(F32),
                         w_up[l], b_up[l].reshape(N_EXPERTS, 1, D_MODEL).astype(F32),
                         w_down[l], b_down[l].reshape(N_EXPERTS, 1, D_MODEL).astype(F32))
    ys = _sc_gather(ys_sorted, dest)

    outs = []
    b0 = 0
    row_off = 0
    for (x, _), b, x1 in zip(groups, nb, x1s):
        seq = x.shape[1]
        ada_g = ada_all[b0:b0 + b]
        b0 += b
        y = _combine(x1, ys, tw_all, ada_g, seq, row_off)
        row_off += b * seq
        outs.append(y.reshape(b, seq, D_MODEL))
    return tuple(outs)
```

```python
import functools
import math

import jax
import jax.numpy as jnp
from jax import lax
from jax.experimental import pallas as pl
from jax.experimental.pallas import tpu as pltpu
from jax.experimental.pallas import tpu_sc as plsc

F32 = jnp.float32
BF16 = jnp.bfloat16
U32 = jnp.uint32

D_MODEL = 1024
HEAD_DIM = 64
NA_HEADS = 8
NA_WIDTH = 512
DIFF_HEADS = 4
DIFF_WIDTH = 512
QKV_COLS = 3072
GRID_W = 64
NA_WIN_ROWS = 8
NA_WIN_COLS = 16
ROPE_THETA = 10000.0
N_EXPERTS = 32
TOP_K = 4
SWIGLU_LIMIT = 7.0
SWIGLU_ALPHA = 1.702
EPS = 1e-5
NEG_INF = -1e30
LAMBDA_INIT = 0.8 - 0.6 * math.exp(-0.3 * 0)
LOG2E = 1.4426950408889634

LANES = 128
MXU_DIM = 256
VMEM_LIMIT = 56 * 1024 * 1024

ROW_BLOCK = 512
Q_BLOCK = 512
EXPERT_BLOCK = 512
EXPERT_SUB = 256
NA_ROWS_PER_TRIP = 4
VT_ROWS = LANES + 16
KV_CHUNK = 256
NORM_SLACK = 1.01
MAX_SAFE_BOUND = 60.0


PACKED = D_MODEL // 2
SC_CORES = 2
SC_SUBCORES = 16
SC_WORKERS = SC_CORES * SC_SUBCORES
GATHER_ROWS = 64


def _params(*sem):
    return pltpu.CompilerParams(dimension_semantics=sem, vmem_limit_bytes=VMEM_LIMIT)


def _pack_halves(x):
    w = x.shape[1] // 2
    bits = lax.bitcast_convert_type(x, U32)
    return (bits[:, :w] >> 16) | bits[:, w:]


def _col_reduce(x, op):
    while x.shape[0] >= 64:
        x = op(x.reshape(8, x.shape[0] // 8, x.shape[1]), axis=0)
    return op(x, axis=0, keepdims=True)


def _unpack_halves(word):
    lo = lax.bitcast_convert_type(word << 16, F32)
    hi = lax.bitcast_convert_type(word & jnp.uint32(0xFFFF0000), F32)
    return lo, hi


def _ada_kernel(c_ref, w_ref, b_ref, o_ref):
    c = c_ref[...]
    s = c * jax.nn.sigmoid(c)
    o_ref[...] = jnp.dot(s, w_ref[...], preferred_element_type=F32,
                         precision=lax.Precision.HIGHEST) + b_ref[...]


def _ada(c_all, w_ada, b_ada):
    nb = c_all.shape[0]
    n_out = w_ada.shape[1]
    blk = D_MODEL
    return pl.pallas_call(
        _ada_kernel,
        grid=(n_out // blk,),
        in_specs=[pl.BlockSpec((nb, D_MODEL), lambda j: (0, 0)),
                  pl.BlockSpec((D_MODEL, blk), lambda j: (0, j)),
                  pl.BlockSpec((1, blk), lambda j: (0, j))],
        out_specs=pl.BlockSpec((nb, blk), lambda j: (0, j)),
        out_shape=jax.ShapeDtypeStruct((nb, n_out), F32),
        compiler_params=_params("arbitrary"),
        name="ada",
    )(c_all, w_ada, b_ada.reshape(1, n_out))


def _head_sumsq(y, bd):
    sq = (y * y).astype(BF16)
    parts = [jnp.dot(sq[:, c:c + MXU_DIM], bd, preferred_element_type=F32)
             for c in range(0, y.shape[1], MXU_DIM)]
    return jnp.concatenate(parts, axis=1)


def _qkv_kernel(x_ref, ada_ref, g_ref, w_ref, gain_ref, cos_ref, sin_ref, bd_ref, o_ref):
    x = x_ref[...]
    ms = jnp.mean(x * x, axis=-1, keepdims=True)
    xn = x * lax.rsqrt(ms + EPS) * g_ref[...]
    sh = ada_ref[0, 0:1, :]
    sc = ada_ref[0, 1:2, :]
    h = (xn * (1.0 + sc) + sh).astype(BF16)
    bd = bd_ref[...]
    lane = lax.broadcasted_iota(jnp.int32, (x.shape[0], NA_WIDTH), 1)
    first_half = (lane & (HEAD_DIM // 2)) == 0
    for grp in range(6):
        cols = slice(grp * 512, (grp + 1) * 512)
        acc = jnp.dot(h, w_ref[:, cols], preferred_element_type=F32)
        if grp in (2, 5):
            o_ref[:, cols] = acc.astype(BF16)
            continue
        gi = {0: 0, 1: 1, 3: 2, 4: 3}[grp]
        ss = _head_sumsq(acc, bd)
        y = acc * lax.rsqrt(ss * (1.0 / HEAD_DIM) + EPS) * gain_ref[gi:gi + 1, :]
        if grp in (3, 4):
            partner = jnp.where(first_half,
                                pltpu.roll(y, NA_WIDTH - HEAD_DIM // 2, axis=1),
                                pltpu.roll(y, HEAD_DIM // 2, axis=1))
            y = y * cos_ref[...] + partner * sin_ref[...]
        o_ref[:, cols] = y.astype(BF16)


def _qkv(x2d, ada_g, g_attn, w_qkv_bf, gains, cos_t, sin_t, bd, seq):
    n = x2d.shape[0]
    tm = ROW_BLOCK
    per_seq = seq // tm
    return pl.pallas_call(
        _qkv_kernel,
        grid=(n // tm,),
        in_specs=[pl.BlockSpec((tm, D_MODEL), lambda i: (i, 0)),
                  pl.BlockSpec((1, 6, D_MODEL), lambda i: (i // per_seq, 0, 0)),
                  pl.BlockSpec((1, D_MODEL), lambda i: (0, 0)),
                  pl.BlockSpec((D_MODEL, QKV_COLS), lambda i: (0, 0)),
                  pl.BlockSpec((4, NA_WIDTH), lambda i: (0, 0)),
                  pl.BlockSpec((tm, DIFF_WIDTH), lambda i: (i % per_seq, 0)),
                  pl.BlockSpec((tm, DIFF_WIDTH), lambda i: (i % per_seq, 0)),
                  pl.BlockSpec((MXU_DIM, MXU_DIM), lambda i: (0, 0))],
        out_specs=pl.BlockSpec((tm, QKV_COLS), lambda i: (i, 0)),
        out_shape=jax.ShapeDtypeStruct((n, QKV_COLS), BF16),
        compiler_params=_params("arbitrary"),
        name="qkv",
    )(x2d, ada_g, g_attn, w_qkv_bf, gains, cos_t, sin_t, bd)


def _na_kernel(q_ref, k_ref, v_ref, bias_ref, o_ref, *, rows):
    lane = lax.broadcasted_iota(jnp.int32, (GRID_W, LANES), 1)
    head0 = lane < HEAD_DIM
    win = NA_WIN_ROWS * GRID_W

    def window_start(r):
        return jnp.clip(r - NA_WIN_ROWS // 2, 0, rows - NA_WIN_ROWS)

    def scores(r):
        r_start = window_start(r)
        q = q_ref[pl.ds(pl.multiple_of(r * GRID_W, GRID_W), GRID_W), :]
        kw = k_ref[pl.ds(pl.multiple_of(r_start * GRID_W, GRID_W), win), :]
        zero = jnp.zeros_like(q)
        qm = jnp.concatenate([jnp.where(head0, q, zero), jnp.where(head0, zero, q)], axis=0)
        s = lax.dot_general(kw, qm, (((1,), (1,)), ((), ())), preferred_element_type=F32)
        return s + bias_ref[0, r - r_start]

    def finish(r, s):
        vw = v_ref[pl.ds(pl.multiple_of(window_start(r) * GRID_W, GRID_W), win), :]
        m = _col_reduce(s, jnp.max)
        p = jnp.exp2(s - m)
        l = _col_reduce(p, jnp.sum)
        p = (p * (1.0 / l)).astype(BF16)
        o2 = lax.dot_general(p, vw, (((0,), (0,)), ((), ())), preferred_element_type=F32)
        o = jnp.where(head0, o2[:GRID_W], o2[GRID_W:])
        o_ref[pl.ds(pl.multiple_of(r * GRID_W, GRID_W), GRID_W), :] = o.astype(BF16)

    def body(i, carry):
        trip_rows = [i * NA_ROWS_PER_TRIP + u for u in range(NA_ROWS_PER_TRIP)]
        trip_scores = [scores(r) for r in trip_rows]
        for r, s in zip(trip_rows, trip_scores):
            finish(r, s)
        return carry

    lax.fori_loop(0, rows // NA_ROWS_PER_TRIP, body, 0)


def _na(qkv, bias_t, batch, seq):
    rows = seq // GRID_W
    n_pairs = NA_HEADS // 2
    return pl.pallas_call(
        functools.partial(_na_kernel, rows=rows),
        grid=(batch, n_pairs),
        in_specs=[pl.BlockSpec((seq, LANES), lambda b, hp: (b, hp)),
                  pl.BlockSpec((seq, LANES), lambda b, hp: (b, n_pairs + hp)),
                  pl.BlockSpec((seq, LANES), lambda b, hp: (b, 2 * n_pairs + hp)),
                  pl.BlockSpec((1, NA_WIN_ROWS, NA_WIN_ROWS * GRID_W, LANES), lambda b, hp: (hp, 0, 0, 0))],
        out_specs=pl.BlockSpec((seq, LANES), lambda b, hp: (b, hp)),
        out_shape=jax.ShapeDtypeStruct((batch * seq, NA_WIDTH), BF16),
        compiler_params=_params("arbitrary", "arbitrary"),
        name="na_attn",
    )(qkv, qkv, qkv, bias_t)


def _diff_kernel(q_ref, k_ref, v_ref, lam_ref, g_ref, o_ref, vt_ref, kn_ref, oa_ref):
    @pl.when(pl.program_id(2) == 0)
    def _():
        vt_ref[:LANES, :] = v_ref[...].astype(F32).T.astype(BF16)
        ones_row = lax.broadcasted_iota(jnp.int32, (VT_ROWS - LANES, v_ref.shape[0]), 0) == 0
        vt_ref[LANES:, :] = jnp.where(ones_row, 1.0, 0.0).astype(BF16)
        kf = k_ref[...].astype(F32)
        d_id = lax.broadcasted_iota(jnp.int32, (LANES, LANES), 0) // HEAD_DIM
        c_id = lax.broadcasted_iota(jnp.int32, (LANES, LANES), 1)
        comp_sel = jnp.where(d_id == c_id, 1.0, 0.0).astype(BF16)
        kn2 = jnp.dot((kf * kf).astype(BF16), comp_sel, preferred_element_type=F32)
        kn_ref[...] = jnp.sqrt(_col_reduce(kn2, jnp.max)) * NORM_SLACK

    lq1 = lam_ref[0:1, :]
    lk1 = lam_ref[1:2, :]
    lq2 = lam_ref[2:3, :]
    lk2 = lam_ref[3:4, :]
    lam = (jnp.exp(jnp.sum(lq1 * lk1, axis=-1, keepdims=True))
           - jnp.exp(jnp.sum(lq2 * lk2, axis=-1, keepdims=True)) + LAMBDA_INIT)

    q = q_ref[...]
    lane = lax.broadcasted_iota(jnp.int32, q.shape, 1)
    zero = jnp.zeros_like(q)
    tq = q.shape[0]
    qcat = jnp.concatenate([jnp.where(lane < HEAD_DIM, q, zero), jnp.where(lane < HEAD_DIM, zero, q)], axis=0)
    def scores(c, chunk):
        kc = k_ref[c * chunk:(c + 1) * chunk, :]
        return lax.dot_general(kc, qcat, (((1,), (1,)), ((), ())), preferred_element_type=F32)

    def chunked(chunk, step):
        n_chunks = k_ref.shape[0] // chunk
        s_next = scores(0, chunk)
        state = None
        for c in range(n_chunks):
            s = s_next
            if c + 1 < n_chunks:
                s_next = scores(c + 1, chunk)
            state = step(c, s, vt_ref[:, c * chunk:(c + 1) * chunk], state)
        return state

    qf = qcat.astype(F32)
    ones8 = jnp.ones((8, LANES), BF16)
    qn2 = lax.dot_general(ones8, (qf * qf).astype(BF16), (((1,), (1,)), ((), ())), preferred_element_type=F32)
    col = lax.broadcasted_iota(jnp.int32, (1, 2 * tq), 1)
    bound = jnp.sqrt(qn2[0:1, :]) * NORM_SLACK * jnp.where(col < tq, kn_ref[0:1, 0:1], kn_ref[0:1, 1:2])
    in_range = jnp.max(bound) <= MAX_SAFE_BOUND

    @pl.when(in_range)
    def _():
        def step(c, s, vtc, acc):
            pv = jnp.dot(vtc, jnp.exp2(s - bound).astype(BF16), preferred_element_type=F32)
            return pv if c == 0 else acc + pv
        oa_ref[...] = chunked(KV_CHUNK, step)

    @pl.when(jnp.logical_not(in_range))
    def _():
        def step(c, s, vtc, state):
            mc = _col_reduce(s, jnp.max)
            m_new = mc if c == 0 else jnp.maximum(state[0], mc)
            pv = jnp.dot(vtc, jnp.exp2(s - m_new).astype(BF16), preferred_element_type=F32)
            return (m_new, pv if c == 0 else jnp.exp2(state[0] - m_new) * state[1] + pv)
        oa_ref[...] = chunked(KV_CHUNK, step)[1]

    oa = oa_ref[...]
    o0, l0 = oa[:LANES, :tq], oa[LANES:LANES + 1, :tq]
    o1, l1 = oa[:LANES, tq:], oa[LANES:LANES + 1, tq:]
    o = o0 * (1.0 / l0) - (lam / l1) * o1
    ms = jnp.mean(o * o, axis=0, keepdims=True)
    y = o * lax.rsqrt(ms + EPS) * g_ref[...]
    o_ref[...] = y.T.astype(BF16)


def _diff(qkv, lam_vecs, subln_col, batch, seq):
    tq = Q_BLOCK
    nq = seq // tq
    base = 3 * NA_WIDTH // LANES
    nh = DIFF_HEADS
    return pl.pallas_call(
        _diff_kernel,
        grid=(batch, nh, nq),
        in_specs=[pl.BlockSpec((tq, LANES), lambda b, h, i: (b * nq + i, base + h)),
                  pl.BlockSpec((seq, LANES), lambda b, h, i: (b, base + nh + h)),
                  pl.BlockSpec((seq, LANES), lambda b, h, i: (b, base + 2 * nh + h)),
                  pl.BlockSpec((4, HEAD_DIM), lambda b, h, i: (0, 0)),
                  pl.BlockSpec((LANES, 1), lambda b, h, i: (0, 0))],
        out_specs=pl.BlockSpec((tq, LANES), lambda b, h, i: (b * nq + i, h)),
        out_shape=jax.ShapeDtypeStruct((batch * seq, DIFF_WIDTH), BF16),
        scratch_shapes=[pltpu.VMEM((VT_ROWS, seq), BF16),
                        pltpu.VMEM((1, LANES), F32),
                        pltpu.VMEM((VT_ROWS, 2 * tq), F32)],
        compiler_params=_params("arbitrary", "arbitrary", "arbitrary"),
        name="diff_attn",
    )(qkv, qkv, qkv, lam_vecs, subln_col)


def _wo_kernel(ona_ref, odf_ref, x_ref, ada_ref, wo_ref, g_ref, wrh_ref, wrl_ref, br_ref, tri_ref, cnt0_ref,
               x1_ref, h2_ref, ti_ref, tw_ref, cnt_ref):
    mix = (jnp.dot(ona_ref[...], wo_ref[:NA_WIDTH, :], preferred_element_type=F32)
           + jnp.dot(odf_ref[...], wo_ref[NA_WIDTH:, :], preferred_element_type=F32))
    gt1 = ada_ref[0, 2:3, :]
    sh2 = ada_ref[0, 3:4, :]
    sc2 = ada_ref[0, 4:5, :]
    x1 = x_ref[...] + gt1 * mix
    x1_ref[...] = x1
    ms = jnp.mean(x1 * x1, axis=-1, keepdims=True)
    h2 = x1 * lax.rsqrt(ms + EPS) * g_ref[...] * (1.0 + sc2) + sh2
    hi = h2.astype(BF16)
    h2_ref[...] = _pack_halves(hi.astype(F32))
    lo = (h2 - hi.astype(F32)).astype(BF16)
    logits = (jnp.dot(hi, wrh_ref[...], preferred_element_type=F32)
              + jnp.dot(hi, wrl_ref[...], preferred_element_type=F32)
              + jnp.dot(lo, wrh_ref[...], preferred_element_type=F32)) + br_ref[...]
    lane = lax.broadcasted_iota(jnp.int32, logits.shape, 1).astype(F32)
    vals = []
    idxs = []
    cur = logits
    for _ in range(TOP_K):
        m = jnp.max(cur, axis=-1, keepdims=True)
        idx = jnp.min(jnp.where(cur == m, lane, float(LANES)), axis=-1, keepdims=True)
        vals.append(m)
        idxs.append(idx)
        cur = jnp.where(lane == idx, -jnp.inf, cur)
    es = [jnp.exp(v - vals[0]) for v in vals]
    inv = 1.0 / (es[0] + es[1] + es[2] + es[3])

    @pl.when(pl.program_id(0) == 0)
    def _():
        cnt_ref[...] = cnt0_ref[...]

    sel = jnp.zeros(logits.shape, F32)
    for j in range(TOP_K):
        sel = sel + jnp.where(lane == idxs[j], 1.0, 0.0)
    before = jnp.dot(tri_ref[...], sel.astype(BF16), preferred_element_type=F32) + cnt_ref[...]
    cnt_ref[...] = cnt_ref[...] + jnp.sum(sel, axis=0, keepdims=True)
    ranks = [jnp.sum(jnp.where(lane == idxs[j], before, 0.0), axis=-1, keepdims=True) for j in range(TOP_K)]

    ti = jnp.zeros(logits.shape, F32)
    tw = jnp.zeros(logits.shape, F32)
    for j in range(TOP_K):
        ti = jnp.where(lane == float(j), idxs[j], ti)
        ti = jnp.where(lane == float(TOP_K + j), ranks[j], ti)
        tw = jnp.where(lane == float(j), es[j] * inv, tw)
    ti_ref[...] = ti.astype(jnp.int32)
    tw_ref[...] = tw


def _wo(o_na, o_df, x2d, ada_g, w_o_bf, g_ffn, wr_hi, wr_lo, br_pad, tri, cnt0, seq):
    n = x2d.shape[0]
    tm = ROW_BLOCK
    per_seq = seq // tm
    row = lambda i: (i, 0)
    const = lambda i: (0, 0)
    return pl.pallas_call(
        _wo_kernel,
        grid=(n // tm,),
        in_specs=[pl.BlockSpec((tm, NA_WIDTH), row),
                  pl.BlockSpec((tm, DIFF_WIDTH), row),
                  pl.BlockSpec((tm, D_MODEL), row),
                  pl.BlockSpec((1, 6, D_MODEL), lambda i: (i // per_seq, 0, 0)),
                  pl.BlockSpec((D_MODEL, D_MODEL), const),
                  pl.BlockSpec((1, D_MODEL), const),
                  pl.BlockSpec((D_MODEL, LANES), const),
                  pl.BlockSpec((D_MODEL, LANES), const),
                  pl.BlockSpec((1, LANES), const),
                  pl.BlockSpec((tm, tm), const),
                  pl.BlockSpec((1, LANES), const)],
        out_specs=[pl.BlockSpec((tm, D_MODEL), row),
                   pl.BlockSpec((tm, PACKED), row),
                   pl.BlockSpec((tm, LANES), row),
                   pl.BlockSpec((tm, LANES), row),
                   pl.BlockSpec((1, LANES), const)],
        out_shape=[jax.ShapeDtypeStruct((n, D_MODEL), F32),
                   jax.ShapeDtypeStruct((n, PACKED), U32),
                   jax.ShapeDtypeStruct((n, LANES), jnp.int32),
                   jax.ShapeDtypeStruct((n, LANES), F32),
                   jax.ShapeDtypeStruct((1, LANES), F32)],
        compiler_params=_params("arbitrary"),
        name="wo_router",
    )(o_na, o_df, x2d, ada_g, w_o_bf, g_ffn, wr_hi, wr_lo, br_pad, tri, cnt0)


def _expert_kernel(be_ref, na_ref, xs_ref, wg_ref, bg_ref, wu_ref, bu_ref, wd_ref, bd_ref, o_ref,
                   wg_bf, wu_bf, wd_bf):
    i = pl.program_id(0)
    active = i < na_ref[0]
    new_expert = jnp.logical_or(i == 0, be_ref[i] != be_ref[jnp.maximum(i - 1, 0)])

    @pl.when(jnp.logical_and(active, new_expert))
    def _():
        wg_bf[...] = wg_ref[0].astype(BF16)
        wu_bf[...] = wu_ref[0].astype(BF16)
        wd_bf[...] = wd_ref[0].astype(BF16)

    @pl.when(active)
    def _():
        def gate_up(rows):
            x_lo, x_hi = _unpack_halves(xs_ref[rows, :])
            x_lo = x_lo.astype(BF16)
            x_hi = x_hi.astype(BF16)

            def proj(w_bf):
                return (jnp.dot(x_lo, w_bf[:PACKED, :], preferred_element_type=F32)
                        + jnp.dot(x_hi, w_bf[PACKED:, :], preferred_element_type=F32))

            return proj(wg_bf), proj(wu_bf)

        def act_down(rows, gu):
            g = jnp.minimum(gu[0] + bg_ref[0], SWIGLU_LIMIT)
            u = jnp.clip(gu[1] + bu_ref[0], -SWIGLU_LIMIT, SWIGLU_LIMIT)
            act = g * jax.nn.sigmoid(SWIGLU_ALPHA * g) * (u + 1.0)
            out = jnp.dot(act.astype(BF16), wd_bf[...], preferred_element_type=F32) + bd_ref[0]
            o_ref[rows, :] = _pack_halves(out.astype(BF16).astype(F32))

        sub = [pl.ds(r, EXPERT_SUB) for r in range(0, EXPERT_BLOCK, EXPERT_SUB)]
        gu_next = gate_up(sub[0])
        for j, rows in enumerate(sub):
            gu = gu_next
            if j + 1 < len(sub):
                gu_next = gate_up(sub[j + 1])
            act_down(rows, gu)

    @pl.when(i >= na_ref[0])
    def _():
        o_ref[...] = jnp.zeros_like(o_ref)


def _experts(block_e, n_active, xs, wg, bg, wu, bu, wd, bd):
    cap = xs.shape[0]
    n_blocks = cap // EXPERT_BLOCK
    xmap = lambda i, be, na: (jnp.minimum(i, na[0] - 1), 0)
    wmap = lambda i, be, na: (be[i], 0, 0)
    grid_spec = pltpu.PrefetchScalarGridSpec(
        num_scalar_prefetch=2,
        grid=(n_blocks,),
        in_specs=[pl.BlockSpec((EXPERT_BLOCK, PACKED), xmap),
                  pl.BlockSpec((1, D_MODEL, D_MODEL), wmap),
                  pl.BlockSpec((1, 1, D_MODEL), wmap),
                  pl.BlockSpec((1, D_MODEL, D_MODEL), wmap),
                  pl.BlockSpec((1, 1, D_MODEL), wmap),
                  pl.BlockSpec((1, D_MODEL, D_MODEL), wmap),
                  pl.BlockSpec((1, 1, D_MODEL), wmap)],
        out_specs=pl.BlockSpec((EXPERT_BLOCK, PACKED), lambda i, be, na: (i, 0)),
        scratch_shapes=[pltpu.VMEM((D_MODEL, D_MODEL), BF16)] * 3,
    )
    return pl.pallas_call(
        _expert_kernel,
        grid_spec=grid_spec,
        out_shape=jax.ShapeDtypeStruct((cap, PACKED), U32),
        compiler_params=_params("arbitrary"),
        name="experts",
    )(block_e, n_active, xs, wg, bg, wu, bu, wd, bd)


def _sc_gather(table, idx):
    n_out = idx.shape[0]
    width = table.shape[1]
    per_worker = n_out // SC_WORKERS
    n_chunks = per_worker // GATHER_ROWS
    assert per_worker * SC_WORKERS == n_out and n_chunks * GATHER_ROWS == per_worker and n_chunks % 2 == 0
    idx3 = idx.reshape(SC_WORKERS, n_chunks, GATHER_ROWS)
    mesh = plsc.VectorSubcoreMesh(core_axis_name="core", subcore_axis_name="subcore")

    @functools.partial(
        pl.kernel, mesh=mesh,
        out_type=jax.ShapeDtypeStruct((n_out, width), table.dtype),
        scratch_types=[pltpu.VMEM((n_chunks, GATHER_ROWS), jnp.int32),
                       pltpu.VMEM((2, GATHER_ROWS, width), table.dtype),
                       pltpu.SemaphoreType.DMA((2,)),
                       pltpu.SemaphoreType.DMA((2,))])
    def gather_kernel(table_hbm, idx_hbm, out_hbm, idx_v, rows_v, gsem, wsem):
        wid = lax.axis_index("subcore") * SC_CORES + lax.axis_index("core")
        base = wid * per_worker
        pltpu.sync_copy(idx_hbm.at[wid], idx_v)

        def gather(j, slot):
            return pltpu.make_async_copy(table_hbm.at[idx_v.at[j]], rows_v.at[slot], gsem.at[slot])

        def write(j, slot):
            dst = out_hbm.at[pl.ds(pl.multiple_of(base + j * GATHER_ROWS, GATHER_ROWS), GATHER_ROWS)]
            return pltpu.make_async_copy(rows_v.at[slot], dst, wsem.at[slot])

        gather(0, 0).start()

        @pl.loop(0, n_chunks, step=2)
        def _(j):
            for slot in range(2):
                jj = j + slot
                gather(jj, slot).wait()

                @pl.when(jj >= 1)
                def _():
                    write(jj - 1, 1 - slot).wait()

                @pl.when(jj + 1 < n_chunks)
                def _():
                    gather(jj + 1, 1 - slot).start()

                write(jj, slot).start()

        write(n_chunks - 1, 1).wait()

    return gather_kernel(table, idx3)


def _sc_scatter(rows, idx, n_out):
    n_src, width = rows.shape
    n_idx = idx.shape[0]
    per_worker = n_idx // SC_WORKERS
    n_chunks = per_worker // GATHER_ROWS
    assert per_worker * SC_WORKERS == n_idx and n_chunks * GATHER_ROWS == per_worker and n_chunks % 2 == 0
    assert n_src % per_worker == 0
    idx3 = idx.reshape(SC_WORKERS, n_chunks, GATHER_ROWS)
    mesh = plsc.VectorSubcoreMesh(core_axis_name="core", subcore_axis_name="subcore")

    @functools.partial(
        pl.kernel, mesh=mesh,
        out_type=jax.ShapeDtypeStruct((n_out, width), rows.dtype),
        scratch_types=[pltpu.VMEM((n_chunks, GATHER_ROWS), jnp.int32),
                       pltpu.VMEM((2, GATHER_ROWS, width), rows.dtype),
                       pltpu.SemaphoreType.DMA((2,)),
                       pltpu.SemaphoreType.DMA((2,))])
    def scatter_kernel(rows_hbm, idx_hbm, out_hbm, idx_v, rows_v, rsem, wsem):
        wid = lax.axis_index("subcore") * SC_CORES + lax.axis_index("core")
        base = lax.rem(wid * per_worker, n_src)
        pltpu.sync_copy(idx_hbm.at[wid], idx_v)

        def read(j, slot):
            src = rows_hbm.at[pl.ds(pl.multiple_of(base + j * GATHER_ROWS, GATHER_ROWS), GATHER_ROWS)]
            return pltpu.make_async_copy(src, rows_v.at[slot], rsem.at[slot])

        def write(j, slot):
            return pltpu.make_async_copy(rows_v.at[slot], out_hbm.at[idx_v.at[j]], wsem.at[slot])

        read(0, 0).start()

        @pl.loop(0, n_chunks, step=2)
        def _(j):
            for slot in range(2):
                jj = j + slot
                read(jj, slot).wait()

                @pl.when(jj >= 1)
                def _():
                    write(jj - 1, 1 - slot).wait()

                @pl.when(jj + 1 < n_chunks)
                def _():
                    read(jj + 1, 1 - slot).start()

                write(jj, slot).start()

        write(n_chunks - 1, 1).wait()

    return scatter_kernel(rows, idx3)


def _combine_kernel(x1_ref, y0_ref, y1_ref, y2_ref, y3_ref, tw_ref, ada_ref, o_ref):
    tw = tw_ref[...]
    acc_lo = jnp.zeros((x1_ref.shape[0], PACKED), F32)
    acc_hi = jnp.zeros((x1_ref.shape[0], PACKED), F32)
    for j, y_ref in enumerate((y0_ref, y1_ref, y2_ref, y3_ref)):
        lo, hi = _unpack_halves(y_ref[...])
        acc_lo = acc_lo + tw[:, j:j + 1] * lo
        acc_hi = acc_hi + tw[:, j:j + 1] * hi
    o_ref[:, :PACKED] = x1_ref[:, :PACKED] + ada_ref[0, 5:6, :PACKED] * acc_lo
    o_ref[:, PACKED:] = x1_ref[:, PACKED:] + ada_ref[0, 5:6, PACKED:] * acc_hi


def _combine(x1, ys, tw, ada_g, seq, row_off):
    n = x1.shape[0]
    tm = ROW_BLOCK
    per_seq = seq // tm
    off = row_off // tm
    per_choice = tw.shape[0] // tm
    y_specs = [pl.BlockSpec((tm, PACKED), functools.partial(lambda i, j: (j * per_choice + off + i, 0), j=j))
               for j in range(TOP_K)]
    return pl.pallas_call(
        _combine_kernel,
        grid=(n // tm,),
        in_specs=[pl.BlockSpec((tm, D_MODEL), lambda i: (i, 0)),
                  *y_specs,
                  pl.BlockSpec((tm, LANES), lambda i: (i + off, 0)),
                  pl.BlockSpec((1, 6, D_MODEL), lambda i: (i // per_seq, 0, 0))],
        out_specs=pl.BlockSpec((tm, D_MODEL), lambda i: (i, 0)),
        out_shape=jax.ShapeDtypeStruct((n, D_MODEL), F32),
        compiler_params=_params("arbitrary"),
        name="combine",
    )(x1, ys, ys, ys, ys, tw, ada_g)


def _rope_tables(seq):
    half = HEAD_DIM // 2
    inv = ROPE_THETA ** (-jnp.arange(half, dtype=F32) / half)
    ang = jnp.arange(seq, dtype=F32)[:, None] * inv[None, :]
    cos, sin = jnp.cos(ang), jnp.sin(ang)
    cos_h = jnp.concatenate([cos, cos], axis=-1)
    sin_h = jnp.concatenate([-sin, sin], axis=-1)
    reps = DIFF_WIDTH // HEAD_DIM
    return jnp.tile(cos_h, (1, reps)), jnp.tile(sin_h, (1, reps))


def _na_bias_table(rpb):
    cols = jnp.arange(GRID_W, dtype=jnp.int32)
    c_start = jnp.clip(cols - NA_WIN_COLS // 2, 0, GRID_W - NA_WIN_COLS)
    col_mask = (cols[None, :] >= c_start[:, None]) & (cols[None, :] < c_start[:, None] + NA_WIN_COLS)
    col_idx = jnp.clip(cols[None, :] - cols[:, None], -(NA_WIN_COLS - 1), NA_WIN_COLS - 1) + NA_WIN_COLS - 1
    delta = jnp.arange(NA_WIN_ROWS, dtype=jnp.int32)
    j = jnp.arange(NA_WIN_ROWS, dtype=jnp.int32)
    row_idx = j[None, :] - delta[:, None] + NA_WIN_ROWS - 1
    row_hot = (row_idx[:, :, None] == jnp.arange(2 * NA_WIN_ROWS - 1, dtype=jnp.int32)).astype(F32)
    col_hot = (col_idx[:, :, None] == jnp.arange(2 * NA_WIN_COLS - 1, dtype=jnp.int32)).astype(F32)
    bias = jnp.einsum('djr,hrc,qkc->hdjqk', row_hot, rpb.astype(F32), col_hot,
                      precision=lax.Precision.HIGHEST)
    bias = jnp.where(col_mask[None, None, None], bias * LOG2E, NEG_INF)
    bias = bias.transpose(0, 1, 2, 4, 3).reshape(NA_HEADS, NA_WIN_ROWS, NA_WIN_ROWS * GRID_W, GRID_W)
    bias = bias.reshape(NA_HEADS // 2, 2, NA_WIN_ROWS, NA_WIN_ROWS * GRID_W, GRID_W)
    return jnp.concatenate([bias[:, 0], bias[:, 1]], axis=-1)


def _routing(top_idx, rank, counts, n):
    n_blocks = n * TOP_K // EXPERT_BLOCK + N_EXPERTS
    experts = jnp.arange(N_EXPERTS, dtype=jnp.int32)
    padded = (counts + EXPERT_BLOCK - 1) // EXPERT_BLOCK * EXPERT_BLOCK
    pad_end = jnp.cumsum(padded)
    pad_start = pad_end - padded
    start_of = jnp.sum(jnp.where(top_idx[:, :, None] == experts, pad_start, 0), axis=-1)
    dest = (start_of + rank).T.reshape(-1)
    block_lo = jnp.arange(n_blocks, dtype=jnp.int32) * EXPERT_BLOCK
    block_e = jnp.minimum(jnp.sum((pad_end[None, :] <= block_lo[:, None]).astype(jnp.int32), axis=1),
                          N_EXPERTS - 1).astype(jnp.int32)
    n_active = (pad_end[-1] // EXPERT_BLOCK).astype(jnp.int32).reshape(1)
    return dest, block_e, n_active, n_blocks * EXPERT_BLOCK


def kernel(x_prompt, x_sample, c_prompt, c_sample, w_ada, b_ada, g_attn_norm, w_qkv, na_q_norm, na_k_norm, na_rpb, diff_q_norm, diff_k_norm, lambda_q1, lambda_k1, lambda_q2, lambda_k2, diff_subln, w_o, g_ffn_norm, w_router, b_router, w_gate, b_gate, w_up, b_up, w_down, b_down):
    l = 0
    groups = [(x_prompt, c_prompt), (x_sample, c_sample)]
    nb = [x.shape[0] for x, _ in groups]

    ada_all = _ada(jnp.concatenate([c for _, c in groups], axis=0), w_ada[l], b_ada[l])
    ada_all = ada_all.reshape(sum(nb), 6, D_MODEL)

    w_qkv_bf = w_qkv[l].astype(BF16)
    w_o_bf = w_o[l].astype(BF16)
    scale = HEAD_DIM ** -0.5
    reps = NA_WIDTH // HEAD_DIM
    gains = jnp.stack([jnp.tile(na_q_norm[l], reps) * (scale * LOG2E),
                       jnp.tile(na_k_norm[l], reps),
                       jnp.tile(diff_q_norm[l], reps) * (scale * LOG2E),
                       jnp.tile(diff_k_norm[l], reps)]).astype(F32)
    head_id = jnp.arange(MXU_DIM, dtype=jnp.int32) // HEAD_DIM
    bd = (head_id[:, None] == head_id[None, :]).astype(BF16)
    bias_t = _na_bias_table(na_rpb[l])
    lam_vecs = jnp.stack([lambda_q1[l], lambda_k1[l], lambda_q2[l], lambda_k2[l]]).astype(F32)
    subln_col = (diff_subln[l].astype(F32) * (1.0 - LAMBDA_INIT)).reshape(LANES, 1)
    wr = w_router[l].astype(F32)
    wr_pad = jnp.zeros((D_MODEL, LANES), F32).at[:, :N_EXPERTS].set(wr)
    wr_hi = wr_pad.astype(BF16)
    wr_lo = (wr_pad - wr_hi.astype(F32)).astype(BF16)
    br_pad = jnp.full((1, LANES), NEG_INF, F32).at[0, :N_EXPERTS].set(b_router[l].astype(F32))
    g_attn = g_attn_norm[l].reshape(1, D_MODEL).astype(F32)
    g_ffn = g_ffn_norm[l].reshape(1, D_MODEL).astype(F32)
    max_seq = max(x.shape[1] for x, _ in groups)
    cos_t, sin_t = _rope_tables(max_seq)

    rows = lax.broadcasted_iota(jnp.int32, (ROW_BLOCK, ROW_BLOCK), 0)
    cols = lax.broadcasted_iota(jnp.int32, (ROW_BLOCK, ROW_BLOCK), 1)
    tri = (cols < rows).astype(BF16)
    cnt0 = jnp.zeros((1, LANES), F32)

    bg = b_gate[l].reshape(N_EXPERTS, 1, D_MODEL).astype(F32)
    bu = b_up[l].reshape(N_EXPERTS, 1, D_MODEL).astype(F32)
    bdn = b_down[l].reshape(N_EXPERTS, 1, D_MODEL).astype(F32)
    ada_groups = [ada_all[:nb[0]], ada_all[nb[0]:]]

    order = sorted(range(len(groups)), key=lambda g: -groups[g][0].shape[1])
    staged = {}
    for g in order:
        x = groups[g][0]
        b, seq = x.shape[0], x.shape[1]
        n = b * seq
        x2d = x.reshape(n, D_MODEL)
        ada_g = ada_groups[g]
        qkv = _qkv(x2d, ada_g, g_attn, w_qkv_bf, gains, cos_t, sin_t, bd, seq)
        o_na = _na(qkv, bias_t, b, seq)
        o_df = _diff(qkv, lam_vecs, subln_col, b, seq)
        x1, h2, ti, tw, cnt = _wo(o_na, o_df, x2d, ada_g, w_o_bf, g_ffn, wr_hi, wr_lo, br_pad, tri, cnt0, seq)
        counts = cnt[0, :N_EXPERTS].astype(jnp.int32)
        dest, block_e, n_active, cap = _routing(ti[:, :TOP_K], ti[:, TOP_K:2 * TOP_K], counts, n)
        xs = _sc_scatter(h2, dest, cap)
        staged[g] = (x1, tw, ada_g, dest, block_e, n_active, xs, seq, b)

    sorted_out = {}
    for g in order:
        x1, tw, ada_g, dest, block_e, n_active, xs, seq, b = staged[g]
        sorted_out[g] = _experts(block_e, n_active, xs, w_gate[l], bg, w_up[l], bu, w_down[l], bdn)

    outs = [None] * len(groups)
    for g in order:
        x1, tw, ada_g, dest, block_e, n_active, xs, seq, b = staged[g]
        ys = _sc_gather(sorted_out[g], dest)
        outs[g] = _combine(x1, ys, tw, ada_g, seq, 0).reshape(b, seq, D_MODEL)
    return tuple(outs)
```

```python
import functools
import math

import jax
import jax.numpy as jnp
from jax import lax
from jax.experimental import pallas as pl
from jax.experimental.pallas import tpu as pltpu
from jax.experimental.pallas import tpu_sc as plsc

F32 = jnp.float32
BF16 = jnp.bfloat16
U32 = jnp.uint32

D_MODEL = 1024
HEAD_DIM = 64
NA_HEADS = 8
NA_WIDTH = 512
DIFF_HEADS = 4
DIFF_WIDTH = 512
QKV_COLS = 3072
GRID_W = 64
NA_WIN_ROWS = 8
NA_WIN_COLS = 16
ROPE_THETA = 10000.0
N_EXPERTS = 32
TOP_K = 4
SWIGLU_LIMIT = 7.0
SWIGLU_ALPHA = 1.702
EPS = 1e-5
NEG_INF = -1e30
LAMBDA_INIT = 0.8 - 0.6 * math.exp(-0.3 * 0)
LOG2E = 1.4426950408889634

LANES = 128
MXU_DIM = 256
VMEM_LIMIT = 56 * 1024 * 1024

ROW_BLOCK = 512
Q_BLOCK = 512
EXPERT_BLOCK = 512
EXPERT_SUB = 256
NA_ROWS_PER_TRIP = 4
VT_ROWS = LANES + 16
KV_CHUNK = 256
NORM_SLACK = 1.01
MAX_SAFE_BOUND = 60.0


PACKED = D_MODEL // 2
SC_CORES = 2
SC_SUBCORES = 16
SC_WORKERS = SC_CORES * SC_SUBCORES
GATHER_ROWS = 64


def _params(*sem):
    return pltpu.CompilerParams(dimension_semantics=sem, vmem_limit_bytes=VMEM_LIMIT)


def _pack_halves(x):
    w = x.shape[1] // 2
    bits = lax.bitcast_convert_type(x, U32)
    return (bits[:, :w] >> 16) | bits[:, w:]


def _col_reduce(x, op):
    while x.shape[0] >= 64:
        x = op(x.reshape(8, x.shape[0] // 8, x.shape[1]), axis=0)
    return op(x, axis=0, keepdims=True)


def _unpack_halves(word):
    lo = lax.bitcast_convert_type(word << 16, F32)
    hi = lax.bitcast_convert_type(word & jnp.uint32(0xFFFF0000), F32)
    return lo, hi


def _ada_kernel(c_ref, w_ref, b_ref, o_ref):
    c = c_ref[...]
    s = c * jax.nn.sigmoid(c)
    o_ref[...] = jnp.dot(s, w_ref[...], preferred_element_type=F32,
                         precision=lax.Precision.HIGHEST) + b_ref[...]


def _ada(c_all, w_ada, b_ada):
    nb = c_all.shape[0]
    n_out = w_ada.shape[1]
    blk = D_MODEL
    return pl.pallas_call(
        _ada_kernel,
        grid=(n_out // blk,),
        in_specs=[pl.BlockSpec((nb, D_MODEL), lambda j: (0, 0)),
                  pl.BlockSpec((D_MODEL, blk), lambda j: (0, j)),
                  pl.BlockSpec((1, blk), lambda j: (0, j))],
        out_specs=pl.BlockSpec((nb, blk), lambda j: (0, j)),
        out_shape=jax.ShapeDtypeStruct((nb, n_out), F32),
        compiler_params=_params("arbitrary"),
        name="ada",
    )(c_all, w_ada, b_ada.reshape(1, n_out))


def _head_sumsq(y, bd):
    sq = (y * y).astype(BF16)
    parts = [jnp.dot(sq[:, c:c + MXU_DIM], bd, preferred_element_type=F32)
             for c in range(0, y.shape[1], MXU_DIM)]
    return jnp.concatenate(parts, axis=1)


def _qkv_kernel(x_ref, ada_ref, g_ref, w_ref, gain_ref, cos_ref, sin_ref, bd_ref, o_ref):
    x = x_ref[...]
    ms = jnp.mean(x * x, axis=-1, keepdims=True)
    xn = x * lax.rsqrt(ms + EPS) * g_ref[...]
    sh = ada_ref[0, 0:1, :]
    sc = ada_ref[0, 1:2, :]
    h = (xn * (1.0 + sc) + sh).astype(BF16)
    bd = bd_ref[...]
    lane = lax.broadcasted_iota(jnp.int32, (x.shape[0], NA_WIDTH), 1)
    first_half = (lane & (HEAD_DIM // 2)) == 0
    for grp in range(6):
        cols = slice(grp * 512, (grp + 1) * 512)
        acc = jnp.dot(h, w_ref[:, cols], preferred_element_type=F32)
        if grp in (2, 5):
            o_ref[:, cols] = acc.astype(BF16)
            continue
        gi = {0: 0, 1: 1, 3: 2, 4: 3}[grp]
        ss = _head_sumsq(acc, bd)
        y = acc * lax.rsqrt(ss * (1.0 / HEAD_DIM) + EPS) * gain_ref[gi:gi + 1, :]
        if grp in (3, 4):
            partner = jnp.where(first_half,
                                pltpu.roll(y, NA_WIDTH - HEAD_DIM // 2, axis=1),
                                pltpu.roll(y, HEAD_DIM // 2, axis=1))
            y = y * cos_ref[...] + partner * sin_ref[...]
        o_ref[:, cols] = y.astype(BF16)


def _qkv(x2d, ada_g, g_attn, w_qkv_bf, gains, cos_t, sin_t, bd, seq):
    n = x2d.shape[0]
    tm = ROW_BLOCK
    per_seq = seq // tm
    return pl.pallas_call(
        _qkv_kernel,
        grid=(n // tm,),
        in_specs=[pl.BlockSpec((tm, D_MODEL), lambda i: (i, 0)),
                  pl.BlockSpec((1, 6, D_MODEL), lambda i: (i // per_seq, 0, 0)),
                  pl.BlockSpec((1, D_MODEL), lambda i: (0, 0)),
                  pl.BlockSpec((D_MODEL, QKV_COLS), lambda i: (0, 0)),
                  pl.BlockSpec((4, NA_WIDTH), lambda i: (0, 0)),
                  pl.BlockSpec((tm, DIFF_WIDTH), lambda i: (i % per_seq, 0)),
                  pl.BlockSpec((tm, DIFF_WIDTH), lambda i: (i % per_seq, 0)),
                  pl.BlockSpec((MXU_DIM, MXU_DIM), lambda i: (0, 0))],
        out_specs=pl.BlockSpec((tm, QKV_COLS), lambda i: (i, 0)),
        out_shape=jax.ShapeDtypeStruct((n, QKV_COLS), BF16),
        compiler_params=_params("arbitrary"),
        name="qkv",
    )(x2d, ada_g, g_attn, w_qkv_bf, gains, cos_t, sin_t, bd)


def _na_kernel(q_ref, k_ref, v_ref, bias_ref, o_ref, *, rows):
    lane = lax.broadcasted_iota(jnp.int32, (GRID_W, LANES), 1)
    head0 = lane < HEAD_DIM
    win = NA_WIN_ROWS * GRID_W

    def window_start(r):
        return jnp.clip(r - NA_WIN_ROWS // 2, 0, rows - NA_WIN_ROWS)

    def scores(r):
        r_start = window_start(r)
        q = q_ref[pl.ds(pl.multiple_of(r * GRID_W, GRID_W), GRID_W), :]
        kw = k_ref[pl.ds(pl.multiple_of(r_start * GRID_W, GRID_W), win), :]
        zero = jnp.zeros_like(q)
        qm = jnp.concatenate([jnp.where(head0, q, zero), jnp.where(head0, zero, q)], axis=0)
        s = lax.dot_general(kw, qm, (((1,), (1,)), ((), ())), preferred_element_type=F32)
        return s + bias_ref[0, r - r_start]

    def finish(r, s):
        vw = v_ref[pl.ds(pl.multiple_of(window_start(r) * GRID_W, GRID_W), win), :]
        m = _col_reduce(s, jnp.max)
        p = jnp.exp2(s - m)
        l = _col_reduce(p, jnp.sum)
        p = (p * (1.0 / l)).astype(BF16)
        o2 = lax.dot_general(p, vw, (((0,), (0,)), ((), ())), preferred_element_type=F32)
        o = jnp.where(head0, o2[:GRID_W], o2[GRID_W:])
        o_ref[pl.ds(pl.multiple_of(r * GRID_W, GRID_W), GRID_W), :] = o.astype(BF16)

    def body(i, carry):
        trip_rows = [i * NA_ROWS_PER_TRIP + u for u in range(NA_ROWS_PER_TRIP)]
        trip_scores = [scores(r) for r in trip_rows]
        for r, s in zip(trip_rows, trip_scores):
            finish(r, s)
        return carry

    lax.fori_loop(0, rows // NA_ROWS_PER_TRIP, body, 0)


def _na(qkv, bias_t, batch, seq):
    rows = seq // GRID_W
    n_pairs = NA_HEADS // 2
    return pl.pallas_call(
        functools.partial(_na_kernel, rows=rows),
        grid=(batch, n_pairs),
        in_specs=[pl.BlockSpec((seq, LANES), lambda b, hp: (b, hp)),
                  pl.BlockSpec((seq, LANES), lambda b, hp: (b, n_pairs + hp)),
                  pl.BlockSpec((seq, LANES), lambda b, hp: (b, 2 * n_pairs + hp)),
                  pl.BlockSpec((1, NA_WIN_ROWS, NA_WIN_ROWS * GRID_W, LANES), lambda b, hp: (hp, 0, 0, 0))],
        out_specs=pl.BlockSpec((seq, LANES), lambda b, hp: (b, hp)),
        out_shape=jax.ShapeDtypeStruct((batch * seq, NA_WIDTH), BF16),
        compiler_params=_params("arbitrary", "arbitrary"),
        name="na_attn",
    )(qkv, qkv, qkv, bias_t)


def _diff_kernel(q_ref, k_ref, v_ref, lam_ref, g_ref, o_ref, vt_ref, kn_ref, oa_ref):
    @pl.when(pl.program_id(2) == 0)
    def _():
        vt_ref[:LANES, :] = v_ref[...].astype(F32).T.astype(BF16)
        ones_row = lax.broadcasted_iota(jnp.int32, (VT_ROWS - LANES, v_ref.shape[0]), 0) == 0
        vt_ref[LANES:, :] = jnp.where(ones_row, 1.0, 0.0).astype(BF16)
        kf = k_ref[...].astype(F32)
        d_id = lax.broadcasted_iota(jnp.int32, (LANES, LANES), 0) // HEAD_DIM
        c_id = lax.broadcasted_iota(jnp.int32, (LANES, LANES), 1)
        comp_sel = jnp.where(d_id == c_id, 1.0, 0.0).astype(BF16)
        kn2 = jnp.dot((kf * kf).astype(BF16), comp_sel, preferred_element_type=F32)
        kn_ref[...] = jnp.sqrt(_col_reduce(kn2, jnp.max)) * NORM_SLACK

    lq1 = lam_ref[0:1, :]
    lk1 = lam_ref[1:2, :]
    lq2 = lam_ref[2:3, :]
    lk2 = lam_ref[3:4, :]
    lam = (jnp.exp(jnp.sum(lq1 * lk1, axis=-1, keepdims=True))
           - jnp.exp(jnp.sum(lq2 * lk2, axis=-1, keepdims=True)) + LAMBDA_INIT)

    q = q_ref[...]
    lane = lax.broadcasted_iota(jnp.int32, q.shape, 1)
    zero = jnp.zeros_like(q)
    tq = q.shape[0]
    qcat = jnp.concatenate([jnp.where(lane < HEAD_DIM, q, zero), jnp.where(lane < HEAD_DIM, zero, q)], axis=0)
    def scores(c, chunk):
        kc = k_ref[c * chunk:(c + 1) * chunk, :]
        return lax.dot_general(kc, qcat, (((1,), (1,)), ((), ())), preferred_element_type=F32)

    def chunked(chunk, step):
        n_chunks = k_ref.shape[0] // chunk
        s_next = scores(0, chunk)
        state = None
        for c in range(n_chunks):
            s = s_next
            if c + 1 < n_chunks:
                s_next = scores(c + 1, chunk)
            state = step(c, s, vt_ref[:, c * chunk:(c + 1) * chunk], state)
        return state

    qf = qcat.astype(F32)
    ones8 = jnp.ones((8, LANES), BF16)
    qn2 = lax.dot_general(ones8, (qf * qf).astype(BF16), (((1,), (1,)), ((), ())), preferred_element_type=F32)
    col = lax.broadcasted_iota(jnp.int32, (1, 2 * tq), 1)
    bound = jnp.sqrt(qn2[0:1, :]) * NORM_SLACK * jnp.where(col < tq, kn_ref[0:1, 0:1], kn_ref[0:1, 1:2])
    in_range = jnp.max(bound) <= MAX_SAFE_BOUND

    @pl.when(in_range)
    def _():
        def step(c, s, vtc, acc):
            pv = jnp.dot(vtc, jnp.exp2(s - bound).astype(BF16), preferred_element_type=F32)
            return pv if c == 0 else acc + pv
        oa_ref[...] = chunked(KV_CHUNK, step)

    @pl.when(jnp.logical_not(in_range))
    def _():
        def step(c, s, vtc, state):
            mc = _col_reduce(s, jnp.max)
            m_new = mc if c == 0 else jnp.maximum(state[0], mc)
            pv = jnp.dot(vtc, jnp.exp2(s - m_new).astype(BF16), preferred_element_type=F32)
            return (m_new, pv if c == 0 else jnp.exp2(state[0] - m_new) * state[1] + pv)
        oa_ref[...] = chunked(KV_CHUNK, step)[1]

    oa = oa_ref[...]
    o0, l0 = oa[:LANES, :tq], oa[LANES:LANES + 1, :tq]
    o1, l1 = oa[:LANES, tq:], oa[LANES:LANES + 1, tq:]
    o = o0 * (1.0 / l0) - (lam / l1) * o1
    ms = jnp.mean(o * o, axis=0, keepdims=True)
    y = o * lax.rsqrt(ms + EPS) * g_ref[...]
    o_ref[...] = y.T.astype(BF16)


def _diff(qkv, lam_vecs, subln_col, batch, seq):
    tq = Q_BLOCK
    nq = seq // tq
    base = 3 * NA_WIDTH // LANES
    nh = DIFF_HEADS
    return pl.pallas_call(
        _diff_kernel,
        grid=(batch, nh, nq),
        in_specs=[pl.BlockSpec((tq, LANES), lambda b, h, i: (b * nq + i, base + h)),
                  pl.BlockSpec((seq, LANES), lambda b, h, i: (b, base + nh + h)),
                  pl.BlockSpec((seq, LANES), lambda b, h, i: (b, base + 2 * nh + h)),
                  pl.BlockSpec((4, HEAD_DIM), lambda b, h, i: (0, 0)),
                  pl.BlockSpec((LANES, 1), lambda b, h, i: (0, 0))],
        out_specs=pl.BlockSpec((tq, LANES), lambda b, h, i: (b * nq + i, h)),
        out_shape=jax.ShapeDtypeStruct((batch * seq, DIFF_WIDTH), BF16),
        scratch_shapes=[pltpu.VMEM((VT_ROWS, seq), BF16),
                        pltpu.VMEM((1, LANES), F32),
                        pltpu.VMEM((VT_ROWS, 2 * tq), F32)],
        compiler_params=_params("arbitrary", "arbitrary", "arbitrary"),
        name="diff_attn",
    )(qkv, qkv, qkv, lam_vecs, subln_col)


def _wo_kernel(ona_ref, odf_ref, x_ref, ada_ref, wo_ref, g_ref, wrh_ref, wrl_ref, br_ref, tri_ref, cnt0_ref,
               x1_ref, h2_ref, ti_ref, tw_ref, cnt_ref):
    mix = (jnp.dot(ona_ref[...], wo_ref[:NA_WIDTH, :], preferred_element_type=F32)
           + jnp.dot(odf_ref[...], wo_ref[NA_WIDTH:, :], preferred_element_type=F32))
    gt1 = ada_ref[0, 2:3, :]
    sh2 = ada_ref[0, 3:4, :]
    sc2 = ada_ref[0, 4:5, :]
    x1 = x_ref[...] + gt1 * mix
    x1_ref[...] = x1
    ms = jnp.mean(x1 * x1, axis=-1, keepdims=True)
    h2 = x1 * lax.rsqrt(ms + EPS) * g_ref[...] * (1.0 + sc2) + sh2
    hi = h2.astype(BF16)
    h2_ref[...] = _pack_halves(hi.astype(F32))
    lo = (h2 - hi.astype(F32)).astype(BF16)
    logits = (jnp.dot(hi, wrh_ref[...], preferred_element_type=F32)
              + jnp.dot(hi, wrl_ref[...], preferred_element_type=F32)
              + jnp.dot(lo, wrh_ref[...], preferred_element_type=F32)) + br_ref[...]
    lane = lax.broadcasted_iota(jnp.int32, logits.shape, 1).astype(F32)
    vals = []
    idxs = []
    cur = logits
    for _ in range(TOP_K):
        m = jnp.max(cur, axis=-1, keepdims=True)
        idx = jnp.min(jnp.where(cur == m, lane, float(LANES)), axis=-1, keepdims=True)
        vals.append(m)
        idxs.append(idx)
        cur = jnp.where(lane == idx, -jnp.inf, cur)
    es = [jnp.exp(v - vals[0]) for v in vals]
    inv = 1.0 / (es[0] + es[1] + es[2] + es[3])

    @pl.when(pl.program_id(0) == 0)
    def _():
        cnt_ref[...] = cnt0_ref[...]

    sel = jnp.zeros(logits.shape, F32)
    for j in range(TOP_K):
        sel = sel + jnp.where(lane == idxs[j], 1.0, 0.0)
    before = jnp.dot(tri_ref[...], sel.astype(BF16), preferred_element_type=F32) + cnt_ref[...]
    cnt_ref[...] = cnt_ref[...] + jnp.sum(sel, axis=0, keepdims=True)
    ranks = [jnp.sum(jnp.where(lane == idxs[j], before, 0.0), axis=-1, keepdims=True) for j in range(TOP_K)]

    ti = jnp.zeros(logits.shape, F32)
    tw = jnp.zeros(logits.shape, F32)
    for j in range(TOP_K):
        ti = jnp.where(lane == float(j), idxs[j], ti)
        ti = jnp.where(lane == float(TOP_K + j), ranks[j], ti)
        tw = jnp.where(lane == float(j), es[j] * inv, tw)
    ti_ref[...] = ti.astype(jnp.int32)
    tw_ref[...] = tw


def _wo(o_na, o_df, x2d, ada_g, w_o_bf, g_ffn, wr_hi, wr_lo, br_pad, tri, cnt0, seq):
    n = x2d.shape[0]
    tm = ROW_BLOCK
    per_seq = seq // tm
    row = lambda i: (i, 0)
    const = lambda i: (0, 0)
    return pl.pallas_call(
        _wo_kernel,
        grid=(n // tm,),
        in_specs=[pl.BlockSpec((tm, NA_WIDTH), row),
                  pl.BlockSpec((tm, DIFF_WIDTH), row),
                  pl.BlockSpec((tm, D_MODEL), row),
                  pl.BlockSpec((1, 6, D_MODEL), lambda i: (i // per_seq, 0, 0)),
                  pl.BlockSpec((D_MODEL, D_MODEL), const),
                  pl.BlockSpec((1, D_MODEL), const),
                  pl.BlockSpec((D_MODEL, LANES), const),
                  pl.BlockSpec((D_MODEL, LANES), const),
                  pl.BlockSpec((1, LANES), const),
                  pl.BlockSpec((tm, tm), const),
                  pl.BlockSpec((1, LANES), const)],
        out_specs=[pl.BlockSpec((tm, D_MODEL), row),
                   pl.BlockSpec((tm, PACKED), row),
                   pl.BlockSpec((tm, LANES), row),
                   pl.BlockSpec((tm, LANES), row),
                   pl.BlockSpec((1, LANES), const)],
        out_shape=[jax.ShapeDtypeStruct((n, D_MODEL), F32),
                   jax.ShapeDtypeStruct((n, PACKED), U32),
                   jax.ShapeDtypeStruct((n, LANES), jnp.int32),
                   jax.ShapeDtypeStruct((n, LANES), F32),
                   jax.ShapeDtypeStruct((1, LANES), F32)],
        compiler_params=_params("arbitrary"),
        name="wo_router",
    )(o_na, o_df, x2d, ada_g, w_o_bf, g_ffn, wr_hi, wr_lo, br_pad, tri, cnt0)


def _expert_kernel(be_ref, nxt_ref, na_ref, xs_ref, wg_hbm, bg_ref, wu_hbm, bu_ref, wd_hbm, bd_ref, o_ref,
                   w_f32, w_bf, sems, slot_ref):
    i = pl.program_id(0)
    active = i < na_ref[0]
    expert = be_ref[i]
    new_expert = jnp.logical_or(i == 0, expert != be_ref[jnp.maximum(i - 1, 0)])
    wg_bf, wu_bf, wd_bf = w_bf.at[0], w_bf.at[1], w_bf.at[2]

    def weight_copies(src_expert, slot):
        return [pltpu.make_async_copy(w_hbm.at[src_expert], w_f32.at[slot, j], sems.at[slot, j])
                for j, w_hbm in enumerate((wg_hbm, wu_hbm, wd_hbm))]

    @pl.when(i == 0)
    def _():
        slot_ref[0] = 0
        for cp in weight_copies(expert, 0):
            cp.start()

    @pl.when(jnp.logical_and(active, new_expert))
    def _():
        slot = slot_ref[0]
        nxt = nxt_ref[i]
        for s in range(2):
            @pl.when(slot == s)
            def _():
                @pl.when(nxt >= 0)
                def _():
                    for cp in weight_copies(nxt, 1 - s):
                        cp.start()
                for j, cp in enumerate(weight_copies(expert, s)):
                    cp.wait()
                    w_bf[j] = w_f32[s, j].astype(BF16)
        slot_ref[0] = 1 - slot

    @pl.when(active)
    def _():
        def gate_up(rows):
            x_lo, x_hi = _unpack_halves(xs_ref[rows, :])
            x_lo = x_lo.astype(BF16)
            x_hi = x_hi.astype(BF16)

            def proj(w_bf):
                return (jnp.dot(x_lo, w_bf[:PACKED, :], preferred_element_type=F32)
                        + jnp.dot(x_hi, w_bf[PACKED:, :], preferred_element_type=F32))

            return proj(wg_bf), proj(wu_bf)

        def act_down(rows, gu):
            g = jnp.minimum(gu[0] + bg_ref[0], SWIGLU_LIMIT)
            u = jnp.clip(gu[1] + bu_ref[0], -SWIGLU_LIMIT, SWIGLU_LIMIT)
            act = g * jax.nn.sigmoid(SWIGLU_ALPHA * g) * (u + 1.0)
            out = jnp.dot(act.astype(BF16), wd_bf[...], preferred_element_type=F32) + bd_ref[0]
            o_ref[rows, :] = _pack_halves(out.astype(BF16).astype(F32))

        sub = [pl.ds(r, EXPERT_SUB) for r in range(0, EXPERT_BLOCK, EXPERT_SUB)]
        gu_next = gate_up(sub[0])
        for j, rows in enumerate(sub):
            gu = gu_next
            if j + 1 < len(sub):
                gu_next = gate_up(sub[j + 1])
            act_down(rows, gu)

    @pl.when(i >= na_ref[0])
    def _():
        o_ref[...] = jnp.zeros_like(o_ref)


def _experts(block_e, next_e, n_active, xs, wg, bg, wu, bu, wd, bd):
    cap = xs.shape[0]
    n_blocks = cap // EXPERT_BLOCK
    xmap = lambda i, be, nx, na: (jnp.minimum(i, na[0] - 1), 0)
    bmap = lambda i, be, nx, na: (be[i], 0, 0)
    hbm = pl.BlockSpec(memory_space=pl.ANY)
    grid_spec = pltpu.PrefetchScalarGridSpec(
        num_scalar_prefetch=3,
        grid=(n_blocks,),
        in_specs=[pl.BlockSpec((EXPERT_BLOCK, PACKED), xmap),
                  hbm, pl.BlockSpec((1, 1, D_MODEL), bmap),
                  hbm, pl.BlockSpec((1, 1, D_MODEL), bmap),
                  hbm, pl.BlockSpec((1, 1, D_MODEL), bmap)],
        out_specs=pl.BlockSpec((EXPERT_BLOCK, PACKED), lambda i, be, nx, na: (i, 0)),
        scratch_shapes=[pltpu.VMEM((2, 3, D_MODEL, D_MODEL), F32),
                        pltpu.VMEM((3, D_MODEL, D_MODEL), BF16),
                        pltpu.SemaphoreType.DMA((2, 3)),
                        pltpu.SMEM((1,), jnp.int32)],
    )
    return pl.pallas_call(
        _expert_kernel,
        grid_spec=grid_spec,
        out_shape=jax.ShapeDtypeStruct((cap, PACKED), U32),
        compiler_params=_params("arbitrary"),
        name="experts",
    )(block_e, next_e, n_active, xs, wg, bg, wu, bu, wd, bd)


def _sc_gather(table, idx):
    n_out = idx.shape[0]
    width = table.shape[1]
    per_worker = n_out // SC_WORKERS
    n_chunks = per_worker // GATHER_ROWS
    assert per_worker * SC_WORKERS == n_out and n_chunks * GATHER_ROWS == per_worker and n_chunks % 2 == 0
    idx3 = idx.reshape(SC_WORKERS, n_chunks, GATHER_ROWS)
    mesh = plsc.VectorSubcoreMesh(core_axis_name="core", subcore_axis_name="subcore")

    @functools.partial(
        pl.kernel, mesh=mesh,
        out_type=jax.ShapeDtypeStruct((n_out, width), table.dtype),
        scratch_types=[pltpu.VMEM((n_chunks, GATHER_ROWS), jnp.int32),
                       pltpu.VMEM((2, GATHER_ROWS, width), table.dtype),
                       pltpu.SemaphoreType.DMA((2,)),
                       pltpu.SemaphoreType.DMA((2,))])
    def gather_kernel(table_hbm, idx_hbm, out_hbm, idx_v, rows_v, gsem, wsem):
        wid = lax.axis_index("subcore") * SC_CORES + lax.axis_index("core")
        base = wid * per_worker
        pltpu.sync_copy(idx_hbm.at[wid], idx_v)

        def gather(j, slot):
            return pltpu.make_async_copy(table_hbm.at[idx_v.at[j]], rows_v.at[slot], gsem.at[slot])

        def write(j, slot):
            dst = out_hbm.at[pl.ds(pl.multiple_of(base + j * GATHER_ROWS, GATHER_ROWS), GATHER_ROWS)]
            return pltpu.make_async_copy(rows_v.at[slot], dst, wsem.at[slot])

        gather(0, 0).start()

        @pl.loop(0, n_chunks, step=2)
        def _(j):
            for slot in range(2):
                jj = j + slot
                gather(jj, slot).wait()

                @pl.when(jj >= 1)
                def _():
                    write(jj - 1, 1 - slot).wait()

                @pl.when(jj + 1 < n_chunks)
                def _():
                    gather(jj + 1, 1 - slot).start()

                write(jj, slot).start()

        write(n_chunks - 1, 1).wait()

    return gather_kernel(table, idx3)


def _sc_scatter(rows, idx, n_out):
    n_src, width = rows.shape
    n_idx = idx.shape[0]
    per_worker = n_idx // SC_WORKERS
    n_chunks = per_worker // GATHER_ROWS
    assert per_worker * SC_WORKERS == n_idx and n_chunks * GATHER_ROWS == per_worker and n_chunks % 2 == 0
    assert n_src % per_worker == 0
    idx3 = idx.reshape(SC_WORKERS, n_chunks, GATHER_ROWS)
    mesh = plsc.VectorSubcoreMesh(core_axis_name="core", subcore_axis_name="subcore")

    @functools.partial(
        pl.kernel, mesh=mesh,
        out_type=jax.ShapeDtypeStruct((n_out, width), rows.dtype),
        scratch_types=[pltpu.VMEM((n_chunks, GATHER_ROWS), jnp.int32),
                       pltpu.VMEM((2, GATHER_ROWS, width), rows.dtype),
                       pltpu.SemaphoreType.DMA((2,)),
                       pltpu.SemaphoreType.DMA((2,))])
    def scatter_kernel(rows_hbm, idx_hbm, out_hbm, idx_v, rows_v, rsem, wsem):
        wid = lax.axis_index("subcore") * SC_CORES + lax.axis_index("core")
        base = lax.rem(wid * per_worker, n_src)
        pltpu.sync_copy(idx_hbm.at[wid], idx_v)

        def read(j, slot):
            src = rows_hbm.at[pl.ds(pl.multiple_of(base + j * GATHER_ROWS, GATHER_ROWS), GATHER_ROWS)]
            return pltpu.make_async_copy(src, rows_v.at[slot], rsem.at[slot])

        def write(j, slot):
            return pltpu.make_async_copy(rows_v.at[slot], out_hbm.at[idx_v.at[j]], wsem.at[slot])

        read(0, 0).start()

        @pl.loop(0, n_chunks, step=2)
        def _(j):
            for slot in range(2):
                jj = j + slot
                read(jj, slot).wait()

                @pl.when(jj >= 1)
                def _():
                    write(jj - 1, 1 - slot).wait()

                @pl.when(jj + 1 < n_chunks)
                def _():
                    read(jj + 1, 1 - slot).start()

                write(jj, slot).start()

        write(n_chunks - 1, 1).wait()

    return scatter_kernel(rows, idx3)


def _combine_kernel(x1_ref, y0_ref, y1_ref, y2_ref, y3_ref, tw_ref, ada_ref, o_ref):
    tw = tw_ref[...]
    acc_lo = jnp.zeros((x1_ref.shape[0], PACKED), F32)
    acc_hi = jnp.zeros((x1_ref.shape[0], PACKED), F32)
    for j, y_ref in enumerate((y0_ref, y1_ref, y2_ref, y3_ref)):
        lo, hi = _unpack_halves(y_ref[...])
        acc_lo = acc_lo + tw[:, j:j + 1] * lo
        acc_hi = acc_hi + tw[:, j:j + 1] * hi
    o_ref[:, :PACKED] = x1_ref[:, :PACKED] + ada_ref[0, 5:6, :PACKED] * acc_lo
    o_ref[:, PACKED:] = x1_ref[:, PACKED:] + ada_ref[0, 5:6, PACKED:] * acc_hi


def _combine(x1, ys, tw, ada_g, seq, row_off):
    n = x1.shape[0]
    tm = ROW_BLOCK
    per_seq = seq // tm
    off = row_off // tm
    per_choice = tw.shape[0] // tm
    y_specs = [pl.BlockSpec((tm, PACKED), functools.partial(lambda i, j: (j * per_choice + off + i, 0), j=j))
               for j in range(TOP_K)]
    return pl.pallas_call(
        _combine_kernel,
        grid=(n // tm,),
        in_specs=[pl.BlockSpec((tm, D_MODEL), lambda i: (i, 0)),
                  *y_specs,
                  pl.BlockSpec((tm, LANES), lambda i: (i + off, 0)),
                  pl.BlockSpec((1, 6, D_MODEL), lambda i: (i // per_seq, 0, 0))],
        out_specs=pl.BlockSpec((tm, D_MODEL), lambda i: (i, 0)),
        out_shape=jax.ShapeDtypeStruct((n, D_MODEL), F32),
        compiler_params=_params("arbitrary"),
        name="combine",
    )(x1, ys, ys, ys, ys, tw, ada_g)


def _rope_tables(seq):
    half = HEAD_DIM // 2
    inv = ROPE_THETA ** (-jnp.arange(half, dtype=F32) / half)
    ang = jnp.arange(seq, dtype=F32)[:, None] * inv[None, :]
    cos, sin = jnp.cos(ang), jnp.sin(ang)
    cos_h = jnp.concatenate([cos, cos], axis=-1)
    sin_h = jnp.concatenate([-sin, sin], axis=-1)
    reps = DIFF_WIDTH // HEAD_DIM
    return jnp.tile(cos_h, (1, reps)), jnp.tile(sin_h, (1, reps))


def _na_bias_table(rpb):
    cols = jnp.arange(GRID_W, dtype=jnp.int32)
    c_start = jnp.clip(cols - NA_WIN_COLS // 2, 0, GRID_W - NA_WIN_COLS)
    col_mask = (cols[None, :] >= c_start[:, None]) & (cols[None, :] < c_start[:, None] + NA_WIN_COLS)
    col_idx = jnp.clip(cols[None, :] - cols[:, None], -(NA_WIN_COLS - 1), NA_WIN_COLS - 1) + NA_WIN_COLS - 1
    delta = jnp.arange(NA_WIN_ROWS, dtype=jnp.int32)
    j = jnp.arange(NA_WIN_ROWS, dtype=jnp.int32)
    row_idx = j[None, :] - delta[:, None] + NA_WIN_ROWS - 1
    row_hot = (row_idx[:, :, None] == jnp.arange(2 * NA_WIN_ROWS - 1, dtype=jnp.int32)).astype(F32)
    col_hot = (col_idx[:, :, None] == jnp.arange(2 * NA_WIN_COLS - 1, dtype=jnp.int32)).astype(F32)
    bias = jnp.einsum('djr,hrc,qkc->hdjqk', row_hot, rpb.astype(F32), col_hot,
                      precision=lax.Precision.HIGHEST)
    bias = jnp.where(col_mask[None, None, None], bias * LOG2E, NEG_INF)
    bias = bias.transpose(0, 1, 2, 4, 3).reshape(NA_HEADS, NA_WIN_ROWS, NA_WIN_ROWS * GRID_W, GRID_W)
    bias = bias.reshape(NA_HEADS // 2, 2, NA_WIN_ROWS, NA_WIN_ROWS * GRID_W, GRID_W)
    return jnp.concatenate([bias[:, 0], bias[:, 1]], axis=-1)


def _routing(top_idx, rank, counts, n):
    n_blocks = n * TOP_K // EXPERT_BLOCK + N_EXPERTS
    experts = jnp.arange(N_EXPERTS, dtype=jnp.int32)
    padded = (counts + EXPERT_BLOCK - 1) // EXPERT_BLOCK * EXPERT_BLOCK
    pad_end = jnp.cumsum(padded)
    pad_start = pad_end - padded
    start_of = jnp.sum(jnp.where(top_idx[:, :, None] == experts, pad_start, 0), axis=-1)
    dest = (start_of + rank).T.reshape(-1)
    block_lo = jnp.arange(n_blocks, dtype=jnp.int32) * EXPERT_BLOCK
    block_e = jnp.minimum(jnp.sum((pad_end[None, :] <= block_lo[:, None]).astype(jnp.int32), axis=1),
                          N_EXPERTS - 1).astype(jnp.int32)
    n_active = (pad_end[-1] // EXPERT_BLOCK).astype(jnp.int32).reshape(1)
    later = jnp.where((experts[None, :] > experts[:, None]) & (padded[None, :] > 0), experts[None, :], N_EXPERTS)
    next_nonempty = jnp.min(later, axis=1)
    next_nonempty = jnp.where(next_nonempty == N_EXPERTS, -1, next_nonempty)
    next_e = jnp.sum(jnp.where(block_e[:, None] == experts[None, :], next_nonempty[None, :], 0), axis=1)
    return dest, block_e, next_e.astype(jnp.int32), n_active, n_blocks * EXPERT_BLOCK


def kernel(x_prompt, x_sample, c_prompt, c_sample, w_ada, b_ada, g_attn_norm, w_qkv, na_q_norm, na_k_norm, na_rpb, diff_q_norm, diff_k_norm, lambda_q1, lambda_k1, lambda_q2, lambda_k2, diff_subln, w_o, g_ffn_norm, w_router, b_router, w_gate, b_gate, w_up, b_up, w_down, b_down):
    l = 0
    groups = [(x_prompt, c_prompt), (x_sample, c_sample)]
    nb = [x.shape[0] for x, _ in groups]

    ada_all = _ada(jnp.concatenate([c for _, c in groups], axis=0), w_ada[l], b_ada[l])
    ada_all = ada_all.reshape(sum(nb), 6, D_MODEL)

    w_qkv_bf = w_qkv[l].astype(BF16)
    w_o_bf = w_o[l].astype(BF16)
    scale = HEAD_DIM ** -0.5
    reps = NA_WIDTH // HEAD_DIM
    gains = jnp.stack([jnp.tile(na_q_norm[l], reps) * (scale * LOG2E),
                       jnp.tile(na_k_norm[l], reps),
                       jnp.tile(diff_q_norm[l], reps) * (scale * LOG2E),
                       jnp.tile(diff_k_norm[l], reps)]).astype(F32)
    head_id = jnp.arange(MXU_DIM, dtype=jnp.int32) // HEAD_DIM
    bd = (head_id[:, None] == head_id[None, :]).astype(BF16)
    bias_t = _na_bias_table(na_rpb[l])
    lam_vecs = jnp.stack([lambda_q1[l], lambda_k1[l], lambda_q2[l], lambda_k2[l]]).astype(F32)
    subln_col = (diff_subln[l].astype(F32) * (1.0 - LAMBDA_INIT)).reshape(LANES, 1)
    wr = w_router[l].astype(F32)
    wr_pad = jnp.zeros((D_MODEL, LANES), F32).at[:, :N_EXPERTS].set(wr)
    wr_hi = wr_pad.astype(BF16)
    wr_lo = (wr_pad - wr_hi.astype(F32)).astype(BF16)
    br_pad = jnp.full((1, LANES), NEG_INF, F32).at[0, :N_EXPERTS].set(b_router[l].astype(F32))
    g_attn = g_attn_norm[l].reshape(1, D_MODEL).astype(F32)
    g_ffn = g_ffn_norm[l].reshape(1, D_MODEL).astype(F32)
    max_seq = max(x.shape[1] for x, _ in groups)
    cos_t, sin_t = _rope_tables(max_seq)

    rows = lax.broadcasted_iota(jnp.int32, (ROW_BLOCK, ROW_BLOCK), 0)
    cols = lax.broadcasted_iota(jnp.int32, (ROW_BLOCK, ROW_BLOCK), 1)
    tri = (cols < rows).astype(BF16)
    cnt0 = jnp.zeros((1, LANES), F32)

    bg = b_gate[l].reshape(N_EXPERTS, 1, D_MODEL).astype(F32)
    bu = b_up[l].reshape(N_EXPERTS, 1, D_MODEL).astype(F32)
    bdn = b_down[l].reshape(N_EXPERTS, 1, D_MODEL).astype(F32)
    ada_groups = [ada_all[:nb[0]], ada_all[nb[0]:]]

    order = sorted(range(len(groups)), key=lambda g: -groups[g][0].shape[1])
    staged = {}
    for g in order:
        x = groups[g][0]
        b, seq = x.shape[0], x.shape[1]
        n = b * seq
        x2d = x.reshape(n, D_MODEL)
        ada_g = ada_groups[g]
        qkv = _qkv(x2d, ada_g, g_attn, w_qkv_bf, gains, cos_t, sin_t, bd, seq)
        o_na = _na(qkv, bias_t, b, seq)
        o_df = _diff(qkv, lam_vecs, subln_col, b, seq)
        x1, h2, ti, tw, cnt = _wo(o_na, o_df, x2d, ada_g, w_o_bf, g_ffn, wr_hi, wr_lo, br_pad, tri, cnt0, seq)
        counts = cnt[0, :N_EXPERTS].astype(jnp.int32)
        dest, block_e, next_e, n_active, cap = _routing(ti[:, :TOP_K], ti[:, TOP_K:2 * TOP_K], counts, n)
        xs = _sc_scatter(h2, dest, cap)
        staged[g] = (x1, tw, ada_g, dest, block_e, next_e, n_active, xs, seq, b)

    sorted_out = {}
    for g in order:
        x1, tw, ada_g, dest, block_e, next_e, n_active, xs, seq, b = staged[g]
        sorted_out[g] = _experts(block_e, next_e, n_active, xs, w_gate[l], bg, w_up[l], bu, w_down[l], bdn)

    outs = [None] * len(groups)
    for g in order:
        x1, tw, ada_g, dest, block_e, next_e, n_active, xs, seq, b = staged[g]
        ys = _sc_gather(sorted_out[g], dest)
        outs[g] = _combine(x1, ys, tw, ada_g, seq, 0).reshape(b, seq, D_MODEL)
    return tuple(outs)
```

```python
import functools
import math

import jax
import jax.numpy as jnp
from jax import lax
from jax.experimental import pallas as pl
from jax.experimental.pallas import tpu as pltpu
from jax.experimental.pallas import tpu_sc as plsc

F32 = jnp.float32
BF16 = jnp.bfloat16
U32 = jnp.uint32

D_MODEL = 1024
HEAD_DIM = 64
NA_HEADS = 8
NA_WIDTH = 512
DIFF_HEADS = 4
DIFF_WIDTH = 512
QKV_COLS = 3072
GRID_W = 64
NA_WIN_ROWS = 8
NA_WIN_COLS = 16
ROPE_THETA = 10000.0
N_EXPERTS = 32
TOP_K = 4
SWIGLU_LIMIT = 7.0
SWIGLU_ALPHA = 1.702
EPS = 1e-5
NEG_INF = -1e30
LAMBDA_INIT = 0.8 - 0.6 * math.exp(-0.3 * 0)
LOG2E = 1.4426950408889634

LANES = 128
MXU_DIM = 256
VMEM_LIMIT = 56 * 1024 * 1024

ROW_BLOCK = 512
Q_BLOCK = 512
EXPERT_BLOCK = 512
EXPERT_SUB = 256
NA_ROWS_PER_TRIP = 4
VT_ROWS = LANES + 16
KV_CHUNK = 256
NORM_SLACK = 1.01
MAX_SAFE_BOUND = 60.0


PACKED = D_MODEL // 2
SC_CORES = 2
SC_SUBCORES = 16
SC_WORKERS = SC_CORES * SC_SUBCORES
GATHER_ROWS = 64


def _params(*sem):
    return pltpu.CompilerParams(dimension_semantics=sem, vmem_limit_bytes=VMEM_LIMIT)


def _pack_halves(x):
    w = x.shape[1] // 2
    bits = lax.bitcast_convert_type(x, U32)
    return (bits[:, :w] >> 16) | bits[:, w:]


def _col_reduce(x, op):
    while x.shape[0] >= 64:
        x = op(x.reshape(8, x.shape[0] // 8, x.shape[1]), axis=0)
    return op(x, axis=0, keepdims=True)


def _unpack_halves(word):
    lo = lax.bitcast_convert_type(word << 16, F32)
    hi = lax.bitcast_convert_type(word & jnp.uint32(0xFFFF0000), F32)
    return lo, hi


def _ada_kernel(c_ref, w_ref, b_ref, o_ref):
    c = c_ref[...]
    s = c * jax.nn.sigmoid(c)
    o_ref[...] = jnp.dot(s, w_ref[...], preferred_element_type=F32,
                         precision=lax.Precision.HIGHEST) + b_ref[...]


def _ada(c_all, w_ada, b_ada):
    nb = c_all.shape[0]
    n_out = w_ada.shape[1]
    blk = D_MODEL
    return pl.pallas_call(
        _ada_kernel,
        grid=(n_out // blk,),
        in_specs=[pl.BlockSpec((nb, D_MODEL), lambda j: (0, 0)),
                  pl.BlockSpec((D_MODEL, blk), lambda j: (0, j)),
                  pl.BlockSpec((1, blk), lambda j: (0, j))],
        out_specs=pl.BlockSpec((nb, blk), lambda j: (0, j)),
        out_shape=jax.ShapeDtypeStruct((nb, n_out), F32),
        compiler_params=_params("arbitrary"),
        name="ada",
    )(c_all, w_ada, b_ada.reshape(1, n_out))


def _head_sumsq(y, bd):
    sq = (y * y).astype(BF16)
    parts = [jnp.dot(sq[:, c:c + MXU_DIM], bd, preferred_element_type=F32)
             for c in range(0, y.shape[1], MXU_DIM)]
    return jnp.concatenate(parts, axis=1)


def _qkv_kernel(x_ref, ada_ref, g_ref, w_ref, gain_ref, cos_ref, sin_ref, bd_ref, o_ref):
    x = x_ref[...]
    ms = jnp.mean(x * x, axis=-1, keepdims=True)
    xn = x * lax.rsqrt(ms + EPS) * g_ref[...]
    sh = ada_ref[0, 0:1, :]
    sc = ada_ref[0, 1:2, :]
    h = (xn * (1.0 + sc) + sh).astype(BF16)
    bd = bd_ref[...]
    lane = lax.broadcasted_iota(jnp.int32, (x.shape[0], NA_WIDTH), 1)
    first_half = (lane & (HEAD_DIM // 2)) == 0
    for grp in range(6):
        cols = slice(grp * 512, (grp + 1) * 512)
        acc = jnp.dot(h, w_ref[:, cols], preferred_element_type=F32)
        if grp in (2, 5):
            o_ref[:, cols] = acc.astype(BF16)
            continue
        gi = {0: 0, 1: 1, 3: 2, 4: 3}[grp]
        ss = _head_sumsq(acc, bd)
        y = acc * lax.rsqrt(ss * (1.0 / HEAD_DIM) + EPS) * gain_ref[gi:gi + 1, :]
        if grp in (3, 4):
            partner = jnp.where(first_half,
                                pltpu.roll(y, NA_WIDTH - HEAD_DIM // 2, axis=1),
                                pltpu.roll(y, HEAD_DIM // 2, axis=1))
            y = y * cos_ref[...] + partner * sin_ref[...]
        o_ref[:, cols] = y.astype(BF16)


def _qkv(x2d, ada_g, g_attn, w_qkv_bf, gains, cos_t, sin_t, bd, seq):
    n = x2d.shape[0]
    tm = ROW_BLOCK
    per_seq = seq // tm
    return pl.pallas_call(
        _qkv_kernel,
        grid=(n // tm,),
        in_specs=[pl.BlockSpec((tm, D_MODEL), lambda i: (i, 0)),
                  pl.BlockSpec((1, 6, D_MODEL), lambda i: (i // per_seq, 0, 0)),
                  pl.BlockSpec((1, D_MODEL), lambda i: (0, 0)),
                  pl.BlockSpec((D_MODEL, QKV_COLS), lambda i: (0, 0)),
                  pl.BlockSpec((4, NA_WIDTH), lambda i: (0, 0)),
                  pl.BlockSpec((tm, DIFF_WIDTH), lambda i: (i % per_seq, 0)),
                  pl.BlockSpec((tm, DIFF_WIDTH), lambda i: (i % per_seq, 0)),
                  pl.BlockSpec((MXU_DIM, MXU_DIM), lambda i: (0, 0))],
        out_specs=pl.BlockSpec((tm, QKV_COLS), lambda i: (i, 0)),
        out_shape=jax.ShapeDtypeStruct((n, QKV_COLS), BF16),
        compiler_params=_params("arbitrary"),
        name="qkv",
    )(x2d, ada_g, g_attn, w_qkv_bf, gains, cos_t, sin_t, bd)


def _na_kernel(q_ref, k_ref, v_ref, bias_ref, o_ref, *, rows):
    lane = lax.broadcasted_iota(jnp.int32, (GRID_W, LANES), 1)
    head0 = lane < HEAD_DIM
    win = NA_WIN_ROWS * GRID_W

    def window_start(r):
        return jnp.clip(r - NA_WIN_ROWS // 2, 0, rows - NA_WIN_ROWS)

    def scores(r):
        r_start = window_start(r)
        q = q_ref[pl.ds(pl.multiple_of(r * GRID_W, GRID_W), GRID_W), :]
        kw = k_ref[pl.ds(pl.multiple_of(r_start * GRID_W, GRID_W), win), :]
        zero = jnp.zeros_like(q)
        qm = jnp.concatenate([jnp.where(head0, q, zero), jnp.where(head0, zero, q)], axis=0)
        s = lax.dot_general(kw, qm, (((1,), (1,)), ((), ())), preferred_element_type=F32)
        return s + bias_ref[0, r - r_start]

    def finish(r, s):
        vw = v_ref[pl.ds(pl.multiple_of(window_start(r) * GRID_W, GRID_W), win), :]
        m = _col_reduce(s, jnp.max)
        p = jnp.exp2(s - m)
        l = _col_reduce(p, jnp.sum)
        p = (p * (1.0 / l)).astype(BF16)
        o2 = lax.dot_general(p, vw, (((0,), (0,)), ((), ())), preferred_element_type=F32)
        o = jnp.where(head0, o2[:GRID_W], o2[GRID_W:])
        o_ref[pl.ds(pl.multiple_of(r * GRID_W, GRID_W), GRID_W), :] = o.astype(BF16)

    def body(i, carry):
        trip_rows = [i * NA_ROWS_PER_TRIP + u for u in range(NA_ROWS_PER_TRIP)]
        trip_scores = [scores(r) for r in trip_rows]
        for r, s in zip(trip_rows, trip_scores):
            finish(r, s)
        return carry

    lax.fori_loop(0, rows // NA_ROWS_PER_TRIP, body, 0)


def _na(qkv, bias_t, batch, seq):
    rows = seq // GRID_W
    n_pairs = NA_HEADS // 2
    return pl.pallas_call(
        functools.partial(_na_kernel, rows=rows),
        grid=(batch, n_pairs),
        in_specs=[pl.BlockSpec((seq, LANES), lambda b, hp: (b, hp)),
                  pl.BlockSpec((seq, LANES), lambda b, hp: (b, n_pairs + hp)),
                  pl.BlockSpec((seq, LANES), lambda b, hp: (b, 2 * n_pairs + hp)),
                  pl.BlockSpec((1, NA_WIN_ROWS, NA_WIN_ROWS * GRID_W, LANES), lambda b, hp: (hp, 0, 0, 0))],
        out_specs=pl.BlockSpec((seq, LANES), lambda b, hp: (b, hp)),
        out_shape=jax.ShapeDtypeStruct((batch * seq, NA_WIDTH), BF16),
        compiler_params=_params("arbitrary", "arbitrary"),
        name="na_attn",
    )(qkv, qkv, qkv, bias_t)


def _diff_kernel(q_ref, k_ref, v_ref, lam_ref, g_ref, o_ref, vt_ref, kn_ref, oa_ref):
    @pl.when(pl.program_id(2) == 0)
    def _():
        vt_ref[:LANES, :] = v_ref[...].astype(F32).T.astype(BF16)
        ones_row = lax.broadcasted_iota(jnp.int32, (VT_ROWS - LANES, v_ref.shape[0]), 0) == 0
        vt_ref[LANES:, :] = jnp.where(ones_row, 1.0, 0.0).astype(BF16)
        kf = k_ref[...].astype(F32)
        d_id = lax.broadcasted_iota(jnp.int32, (LANES, LANES), 0) // HEAD_DIM
        c_id = lax.broadcasted_iota(jnp.int32, (LANES, LANES), 1)
        comp_sel = jnp.where(d_id == c_id, 1.0, 0.0).astype(BF16)
        kn2 = jnp.dot((kf * kf).astype(BF16), comp_sel, preferred_element_type=F32)
        kn_ref[...] = jnp.sqrt(_col_reduce(kn2, jnp.max)) * NORM_SLACK

    lq1 = lam_ref[0:1, :]
    lk1 = lam_ref[1:2, :]
    lq2 = lam_ref[2:3, :]
    lk2 = lam_ref[3:4, :]
    lam = (jnp.exp(jnp.sum(lq1 * lk1, axis=-1, keepdims=True))
           - jnp.exp(jnp.sum(lq2 * lk2, axis=-1, keepdims=True)) + LAMBDA_INIT)

    q = q_ref[...]
    lane = lax.broadcasted_iota(jnp.int32, q.shape, 1)
    zero = jnp.zeros_like(q)
    tq = q.shape[0]
    qcat = jnp.concatenate([jnp.where(lane < HEAD_DIM, q, zero), jnp.where(lane < HEAD_DIM, zero, q)], axis=0)
    def scores(c, chunk):
        kc = k_ref[c * chunk:(c + 1) * chunk, :]
        return lax.dot_general(kc, qcat, (((1,), (1,)), ((), ())), preferred_element_type=F32)

    def chunked(chunk, step):
        n_chunks = k_ref.shape[0] // chunk
        s_next = scores(0, chunk)
        state = None
        for c in range(n_chunks):
            s = s_next
            if c + 1 < n_chunks:
                s_next = scores(c + 1, chunk)
            state = step(c, s, vt_ref[:, c * chunk:(c + 1) * chunk], state)
        return state

    qf = qcat.astype(F32)
    ones8 = jnp.ones((8, LANES), BF16)
    qn2 = lax.dot_general(ones8, (qf * qf).astype(BF16), (((1,), (1,)), ((), ())), preferred_element_type=F32)
    col = lax.broadcasted_iota(jnp.int32, (1, 2 * tq), 1)
    bound = jnp.sqrt(qn2[0:1, :]) * NORM_SLACK * jnp.where(col < tq, kn_ref[0:1, 0:1], kn_ref[0:1, 1:2])
    in_range = jnp.max(bound) <= MAX_SAFE_BOUND

    @pl.when(in_range)
    def _():
        def step(c, s, vtc, acc):
            pv = jnp.dot(vtc, jnp.exp2(s - bound).astype(BF16), preferred_element_type=F32)
            return pv if c == 0 else acc + pv
        oa_ref[...] = chunked(KV_CHUNK, step)

    @pl.when(jnp.logical_not(in_range))
    def _():
        def step(c, s, vtc, state):
            mc = _col_reduce(s, jnp.max)
            m_new = mc if c == 0 else jnp.maximum(state[0], mc)
            pv = jnp.dot(vtc, jnp.exp2(s - m_new).astype(BF16), preferred_element_type=F32)
            return (m_new, pv if c == 0 else jnp.exp2(state[0] - m_new) * state[1] + pv)
        oa_ref[...] = chunked(KV_CHUNK, step)[1]

    oa = oa_ref[...]
    o0, l0 = oa[:LANES, :tq], oa[LANES:LANES + 1, :tq]
    o1, l1 = oa[:LANES, tq:], oa[LANES:LANES + 1, tq:]
    o = o0 * (1.0 / l0) - (lam / l1) * o1
    ms = jnp.mean(o * o, axis=0, keepdims=True)
    y = o * lax.rsqrt(ms + EPS) * g_ref[...]
    o_ref[...] = y.T.astype(BF16)


def _diff(qkv, lam_vecs, subln_col, batch, seq):
    tq = Q_BLOCK
    nq = seq // tq
    base = 3 * NA_WIDTH // LANES
    nh = DIFF_HEADS
    return pl.pallas_call(
        _diff_kernel,
        grid=(batch, nh, nq),
        in_specs=[pl.BlockSpec((tq, LANES), lambda b, h, i: (b * nq + i, base + h)),
                  pl.BlockSpec((seq, LANES), lambda b, h, i: (b, base + nh + h)),
                  pl.BlockSpec((seq, LANES), lambda b, h, i: (b, base + 2 * nh + h)),
                  pl.BlockSpec((4, HEAD_DIM), lambda b, h, i: (0, 0)),
                  pl.BlockSpec((LANES, 1), lambda b, h, i: (0, 0))],
        out_specs=pl.BlockSpec((tq, LANES), lambda b, h, i: (b * nq + i, h)),
        out_shape=jax.ShapeDtypeStruct((batch * seq, DIFF_WIDTH), BF16),
        scratch_shapes=[pltpu.VMEM((VT_ROWS, seq), BF16),
                        pltpu.VMEM((1, LANES), F32),
                        pltpu.VMEM((VT_ROWS, 2 * tq), F32)],
        compiler_params=_params("arbitrary", "arbitrary", "arbitrary"),
        name="diff_attn",
    )(qkv, qkv, qkv, lam_vecs, subln_col)


def _wo_kernel(ona_ref, odf_ref, x_ref, ada_ref, wo_ref, g_ref, wrh_ref, wrl_ref, br_ref, tri_ref, cnt0_ref,
               x1_ref, h2_ref, ti_ref, tw_ref, cnt_ref):
    mix = (jnp.dot(ona_ref[...], wo_ref[:NA_WIDTH, :], preferred_element_type=F32)
           + jnp.dot(odf_ref[...], wo_ref[NA_WIDTH:, :], preferred_element_type=F32))
    gt1 = ada_ref[0, 2:3, :]
    sh2 = ada_ref[0, 3:4, :]
    sc2 = ada_ref[0, 4:5, :]
    x1 = x_ref[...] + gt1 * mix
    x1_ref[...] = x1
    ms = jnp.mean(x1 * x1, axis=-1, keepdims=True)
    h2 = x1 * lax.rsqrt(ms + EPS) * g_ref[...] * (1.0 + sc2) + sh2
    hi = h2.astype(BF16)
    h2_ref[...] = _pack_halves(hi.astype(F32))
    lo = (h2 - hi.astype(F32)).astype(BF16)
    logits = (jnp.dot(hi, wrh_ref[...], preferred_element_type=F32)
              + jnp.dot(hi, wrl_ref[...], preferred_element_type=F32)
              + jnp.dot(lo, wrh_ref[...], preferred_element_type=F32)) + br_ref[...]
    lane = lax.broadcasted_iota(jnp.int32, logits.shape, 1).astype(F32)
    vals = []
    idxs = []
    cur = logits
    for _ in range(TOP_K):
        m = jnp.max(cur, axis=-1, keepdims=True)
        idx = jnp.min(jnp.where(cur == m, lane, float(LANES)), axis=-1, keepdims=True)
        vals.append(m)
        idxs.append(idx)
        cur = jnp.where(lane == idx, -jnp.inf, cur)
    es = [jnp.exp(v - vals[0]) for v in vals]
    inv = 1.0 / (es[0] + es[1] + es[2] + es[3])

    @pl.when(pl.program_id(0) == 0)
    def _():
        cnt_ref[...] = cnt0_ref[...]

    sel = jnp.zeros(logits.shape, F32)
    for j in range(TOP_K):
        sel = sel + jnp.where(lane == idxs[j], 1.0, 0.0)
    before = jnp.dot(tri_ref[...], sel.astype(BF16), preferred_element_type=F32) + cnt_ref[...]
    cnt_ref[...] = cnt_ref[...] + jnp.sum(sel, axis=0, keepdims=True)
    ranks = [jnp.sum(jnp.where(lane == idxs[j], before, 0.0), axis=-1, keepdims=True) for j in range(TOP_K)]

    ti = jnp.zeros(logits.shape, F32)
    tw = jnp.zeros(logits.shape, F32)
    for j in range(TOP_K):
        ti = jnp.where(lane == float(j), idxs[j], ti)
        ti = jnp.where(lane == float(TOP_K + j), ranks[j], ti)
        tw = jnp.where(lane == float(j), es[j] * inv, tw)
    ti_ref[...] = ti.astype(jnp.int32)
    tw_ref[...] = tw


def _wo(o_na, o_df, x2d, ada_g, w_o_bf, g_ffn, wr_hi, wr_lo, br_pad, tri, cnt0, seq):
    n = x2d.shape[0]
    tm = ROW_BLOCK
    per_seq = seq // tm
    row = lambda i: (i, 0)
    const = lambda i: (0, 0)
    return pl.pallas_call(
        _wo_kernel,
        grid=(n // tm,),
        in_specs=[pl.BlockSpec((tm, NA_WIDTH), row),
                  pl.BlockSpec((tm, DIFF_WIDTH), row),
                  pl.BlockSpec((tm, D_MODEL), row),
                  pl.BlockSpec((1, 6, D_MODEL), lambda i: (i // per_seq, 0, 0)),
                  pl.BlockSpec((D_MODEL, D_MODEL), const),
                  pl.BlockSpec((1, D_MODEL), const),
                  pl.BlockSpec((D_MODEL, LANES), const),
                  pl.BlockSpec((D_MODEL, LANES), const),
                  pl.BlockSpec((1, LANES), const),
                  pl.BlockSpec((tm, tm), const),
                  pl.BlockSpec((1, LANES), const)],
        out_specs=[pl.BlockSpec((tm, D_MODEL), row),
                   pl.BlockSpec((tm, PACKED), row),
                   pl.BlockSpec((tm, LANES), row),
                   pl.BlockSpec((tm, LANES), row),
                   pl.BlockSpec((1, LANES), const)],
        out_shape=[jax.ShapeDtypeStruct((n, D_MODEL), F32),
                   jax.ShapeDtypeStruct((n, PACKED), U32),
                   jax.ShapeDtypeStruct((n, LANES), jnp.int32),
                   jax.ShapeDtypeStruct((n, LANES), F32),
                   jax.ShapeDtypeStruct((1, LANES), F32)],
        compiler_params=_params("arbitrary"),
        name="wo_router",
    )(o_na, o_df, x2d, ada_g, w_o_bf, g_ffn, wr_hi, wr_lo, br_pad, tri, cnt0)


def _expert_kernel(be_ref, nxt_ref, na_ref, xs_ref, wg_hbm, bg_ref, wu_hbm, bu_ref, wd_hbm, bd_ref, o_ref,
                   w_f32, w_bf, sems, slot_ref):
    i = pl.program_id(0)
    active = i < na_ref[0]
    expert = be_ref[i]
    new_expert = jnp.logical_or(i == 0, expert != be_ref[jnp.maximum(i - 1, 0)])
    wg_bf, wu_bf, wd_bf = w_bf.at[0], w_bf.at[1], w_bf.at[2]

    def weight_copies(src_expert, slot):
        return [pltpu.make_async_copy(w_hbm.at[src_expert], w_f32.at[slot, j], sems.at[slot, j])
                for j, w_hbm in enumerate((wg_hbm, wu_hbm, wd_hbm))]

    @pl.when(i == 0)
    def _():
        slot_ref[0] = 0
        for cp in weight_copies(expert, 0):
            cp.start()

    @pl.when(jnp.logical_and(active, new_expert))
    def _():
        slot = slot_ref[0]
        nxt = nxt_ref[i]
        for s in range(2):
            @pl.when(slot == s)
            def _():
                @pl.when(nxt >= 0)
                def _():
                    for cp in weight_copies(nxt, 1 - s):
                        cp.start(priority=1)
                for j, cp in enumerate(weight_copies(expert, s)):
                    cp.wait()
                    w_bf[j] = w_f32[s, j].astype(BF16)
        slot_ref[0] = 1 - slot

    @pl.when(active)
    def _():
        def gate_up(rows):
            x_lo, x_hi = _unpack_halves(xs_ref[rows, :])
            x_lo = x_lo.astype(BF16)
            x_hi = x_hi.astype(BF16)

            def proj(w_bf):
                return (jnp.dot(x_lo, w_bf[:PACKED, :], preferred_element_type=F32)
                        + jnp.dot(x_hi, w_bf[PACKED:, :], preferred_element_type=F32))

            return proj(wg_bf), proj(wu_bf)

        def act_down(rows, gu):
            g = jnp.minimum(gu[0] + bg_ref[0], SWIGLU_LIMIT)
            u = jnp.clip(gu[1] + bu_ref[0], -SWIGLU_LIMIT, SWIGLU_LIMIT)
            act = g * jax.nn.sigmoid(SWIGLU_ALPHA * g) * (u + 1.0)
            out = jnp.dot(act.astype(BF16), wd_bf[...], preferred_element_type=F32) + bd_ref[0]
            o_ref[rows, :] = _pack_halves(out.astype(BF16).astype(F32))

        sub = [pl.ds(r, EXPERT_SUB) for r in range(0, EXPERT_BLOCK, EXPERT_SUB)]
        gu_next = gate_up(sub[0])
        for j, rows in enumerate(sub):
            gu = gu_next
            if j + 1 < len(sub):
                gu_next = gate_up(sub[j + 1])
            act_down(rows, gu)

    @pl.when(i >= na_ref[0])
    def _():
        o_ref[...] = jnp.zeros_like(o_ref)


def _experts(block_e, next_e, n_active, xs, wg, bg, wu, bu, wd, bd):
    cap = xs.shape[0]
    n_blocks = cap // EXPERT_BLOCK
    xmap = lambda i, be, nx, na: (jnp.minimum(i, na[0] - 1), 0)
    bmap = lambda i, be, nx, na: (be[i], 0, 0)
    hbm = pl.BlockSpec(memory_space=pl.ANY)
    grid_spec = pltpu.PrefetchScalarGridSpec(
        num_scalar_prefetch=3,
        grid=(n_blocks,),
        in_specs=[pl.BlockSpec((EXPERT_BLOCK, PACKED), xmap),
                  hbm, pl.BlockSpec((1, 1, D_MODEL), bmap),
                  hbm, pl.BlockSpec((1, 1, D_MODEL), bmap),
                  hbm, pl.BlockSpec((1, 1, D_MODEL), bmap)],
        out_specs=pl.BlockSpec((EXPERT_BLOCK, PACKED), lambda i, be, nx, na: (i, 0)),
        scratch_shapes=[pltpu.VMEM((2, 3, D_MODEL, D_MODEL), F32),
                        pltpu.VMEM((3, D_MODEL, D_MODEL), BF16),
                        pltpu.SemaphoreType.DMA((2, 3)),
                        pltpu.SMEM((1,), jnp.int32)],
    )
    return pl.pallas_call(
        _expert_kernel,
        grid_spec=grid_spec,
        out_shape=jax.ShapeDtypeStruct((cap, PACKED), U32),
        compiler_params=_params("arbitrary"),
        name="experts",
    )(block_e, next_e, n_active, xs, wg, bg, wu, bu, wd, bd)


def _sc_gather(table, idx):
    n_out = idx.shape[0]
    width = table.shape[1]
    per_worker = n_out // SC_WORKERS
    n_chunks = per_worker // GATHER_ROWS
    assert per_worker * SC_WORKERS == n_out and n_chunks * GATHER_ROWS == per_worker and n_chunks % 2 == 0
    idx3 = idx.reshape(SC_WORKERS, n_chunks, GATHER_ROWS)
    mesh = plsc.VectorSubcoreMesh(core_axis_name="core", subcore_axis_name="subcore")

    @functools.partial(
        pl.kernel, mesh=mesh,
        out_type=jax.ShapeDtypeStruct((n_out, width), table.dtype),
        scratch_types=[pltpu.VMEM((n_chunks, GATHER_ROWS), jnp.int32),
                       pltpu.VMEM((2, GATHER_ROWS, width), table.dtype),
                       pltpu.SemaphoreType.DMA((2,)),
                       pltpu.SemaphoreType.DMA((2,))])
    def gather_kernel(table_hbm, idx_hbm, out_hbm, idx_v, rows_v, gsem, wsem):
        wid = lax.axis_index("subcore") * SC_CORES + lax.axis_index("core")
        base = wid * per_worker
        pltpu.sync_copy(idx_hbm.at[wid], idx_v)

        def gather(j, slot):
            return pltpu.make_async_copy(table_hbm.at[idx_v.at[j]], rows_v.at[slot], gsem.at[slot])

        def write(j, slot):
            dst = out_hbm.at[pl.ds(pl.multiple_of(base + j * GATHER_ROWS, GATHER_ROWS), GATHER_ROWS)]
            return pltpu.make_async_copy(rows_v.at[slot], dst, wsem.at[slot])

        gather(0, 0).start()

        @pl.loop(0, n_chunks, step=2)
        def _(j):
            for slot in range(2):
                jj = j + slot
                gather(jj, slot).wait()

                @pl.when(jj >= 1)
                def _():
                    write(jj - 1, 1 - slot).wait()

                @pl.when(jj + 1 < n_chunks)
                def _():
                    gather(jj + 1, 1 - slot).start()

                write(jj, slot).start()

        write(n_chunks - 1, 1).wait()

    return gather_kernel(table, idx3)


def _sc_scatter(rows, idx, n_out):
    n_src, width = rows.shape
    n_idx = idx.shape[0]
    per_worker = n_idx // SC_WORKERS
    n_chunks = per_worker // GATHER_ROWS
    assert per_worker * SC_WORKERS == n_idx and n_chunks * GATHER_ROWS == per_worker and n_chunks % 2 == 0
    assert n_src % per_worker == 0
    idx3 = idx.reshape(SC_WORKERS, n_chunks, GATHER_ROWS)
    mesh = plsc.VectorSubcoreMesh(core_axis_name="core", subcore_axis_name="subcore")

    @functools.partial(
        pl.kernel, mesh=mesh,
        out_type=jax.ShapeDtypeStruct((n_out, width), rows.dtype),
        scratch_types=[pltpu.VMEM((n_chunks, GATHER_ROWS), jnp.int32),
                       pltpu.VMEM((2, GATHER_ROWS, width), rows.dtype),
                       pltpu.SemaphoreType.DMA((2,)),
                       pltpu.SemaphoreType.DMA((2,))])
    def scatter_kernel(rows_hbm, idx_hbm, out_hbm, idx_v, rows_v, rsem, wsem):
        wid = lax.axis_index("subcore") * SC_CORES + lax.axis_index("core")
        base = lax.rem(wid * per_worker, n_src)
        pltpu.sync_copy(idx_hbm.at[wid], idx_v)

        def read(j, slot):
            src = rows_hbm.at[pl.ds(pl.multiple_of(base + j * GATHER_ROWS, GATHER_ROWS), GATHER_ROWS)]
            return pltpu.make_async_copy(src, rows_v.at[slot], rsem.at[slot])

        def write(j, slot):
            return pltpu.make_async_copy(rows_v.at[slot], out_hbm.at[idx_v.at[j]], wsem.at[slot])

        read(0, 0).start()

        @pl.loop(0, n_chunks, step=2)
        def _(j):
            for slot in range(2):
                jj = j + slot
                read(jj, slot).wait()

                @pl.when(jj >= 1)
                def _():
                    write(jj - 1, 1 - slot).wait()

                @pl.when(jj + 1 < n_chunks)
                def _():
                    read(jj + 1, 1 - slot).start()

                write(jj, slot).start()

        write(n_chunks - 1, 1).wait()

    return scatter_kernel(rows, idx3)


def _combine_kernel(x1_ref, y0_ref, y1_ref, y2_ref, y3_ref, tw_ref, ada_ref, o_ref):
    tw = tw_ref[...]
    acc_lo = jnp.zeros((x1_ref.shape[0], PACKED), F32)
    acc_hi = jnp.zeros((x1_ref.shape[0], PACKED), F32)
    for j, y_ref in enumerate((y0_ref, y1_ref, y2_ref, y3_ref)):
        lo, hi = _unpack_halves(y_ref[...])
        acc_lo = acc_lo + tw[:, j:j + 1] * lo
        acc_hi = acc_hi + tw[:, j:j + 1] * hi
    o_ref[:, :PACKED] = x1_ref[:, :PACKED] + ada_ref[0, 5:6, :PACKED] * acc_lo
    o_ref[:, PACKED:] = x1_ref[:, PACKED:] + ada_ref[0, 5:6, PACKED:] * acc_hi


def _combine(x1, ys, tw, ada_g, seq, row_off):
    n = x1.shape[0]
    tm = ROW_BLOCK
    per_seq = seq // tm
    off = row_off // tm
    per_choice = tw.shape[0] // tm
    y_specs = [pl.BlockSpec((tm, PACKED), functools.partial(lambda i, j: (j * per_choice + off + i, 0), j=j))
               for j in range(TOP_K)]
    return pl.pallas_call(
        _combine_kernel,
        grid=(n // tm,),
        in_specs=[pl.BlockSpec((tm, D_MODEL), lambda i: (i, 0)),
                  *y_specs,
                  pl.BlockSpec((tm, LANES), lambda i: (i + off, 0)),
                  pl.BlockSpec((1, 6, D_MODEL), lambda i: (i // per_seq, 0, 0))],
        out_specs=pl.BlockSpec((tm, D_MODEL), lambda i: (i, 0)),
        out_shape=jax.ShapeDtypeStruct((n, D_MODEL), F32),
        compiler_params=_params("arbitrary"),
        name="combine",
    )(x1, ys, ys, ys, ys, tw, ada_g)


def _rope_tables(seq):
    half = HEAD_DIM // 2
    inv = ROPE_THETA ** (-jnp.arange(half, dtype=F32) / half)
    ang = jnp.arange(seq, dtype=F32)[:, None] * inv[None, :]
    cos, sin = jnp.cos(ang), jnp.sin(ang)
    cos_h = jnp.concatenate([cos, cos], axis=-1)
    sin_h = jnp.concatenate([-sin, sin], axis=-1)
    reps = DIFF_WIDTH // HEAD_DIM
    return jnp.tile(cos_h, (1, reps)), jnp.tile(sin_h, (1, reps))


def _na_bias_table(rpb):
    cols = jnp.arange(GRID_W, dtype=jnp.int32)
    c_start = jnp.clip(cols - NA_WIN_COLS // 2, 0, GRID_W - NA_WIN_COLS)
    col_mask = (cols[None, :] >= c_start[:, None]) & (cols[None, :] < c_start[:, None] + NA_WIN_COLS)
    col_idx = jnp.clip(cols[None, :] - cols[:, None], -(NA_WIN_COLS - 1), NA_WIN_COLS - 1) + NA_WIN_COLS - 1
    delta = jnp.arange(NA_WIN_ROWS, dtype=jnp.int32)
    j = jnp.arange(NA_WIN_ROWS, dtype=jnp.int32)
    row_idx = j[None, :] - delta[:, None] + NA_WIN_ROWS - 1
    row_hot = (row_idx[:, :, None] == jnp.arange(2 * NA_WIN_ROWS - 1, dtype=jnp.int32)).astype(F32)
    col_hot = (col_idx[:, :, None] == jnp.arange(2 * NA_WIN_COLS - 1, dtype=jnp.int32)).astype(F32)
    bias = jnp.einsum('djr,hrc,qkc->hdjqk', row_hot, rpb.astype(F32), col_hot,
                      precision=lax.Precision.HIGHEST)
    bias = jnp.where(col_mask[None, None, None], bias * LOG2E, NEG_INF)
    bias = bias.transpose(0, 1, 2, 4, 3).reshape(NA_HEADS, NA_WIN_ROWS, NA_WIN_ROWS * GRID_W, GRID_W)
    bias = bias.reshape(NA_HEADS // 2, 2, NA_WIN_ROWS, NA_WIN_ROWS * GRID_W, GRID_W)
    return jnp.concatenate([bias[:, 0], bias[:, 1]], axis=-1)


def _routing(top_idx, rank, counts, n):
    n_blocks = n * TOP_K // EXPERT_BLOCK + N_EXPERTS
    experts = jnp.arange(N_EXPERTS, dtype=jnp.int32)
    padded = (counts + EXPERT_BLOCK - 1) // EXPERT_BLOCK * EXPERT_BLOCK
    pad_end = jnp.cumsum(padded)
    pad_start = pad_end - padded
    start_of = jnp.sum(jnp.where(top_idx[:, :, None] == experts, pad_start, 0), axis=-1)
    dest = (start_of + rank).T.reshape(-1)
    block_lo = jnp.arange(n_blocks, dtype=jnp.int32) * EXPERT_BLOCK
    block_e = jnp.minimum(jnp.sum((pad_end[None, :] <= block_lo[:, None]).astype(jnp.int32), axis=1),
                          N_EXPERTS - 1).astype(jnp.int32)
    n_active = (pad_end[-1] // EXPERT_BLOCK).astype(jnp.int32).reshape(1)
    later = jnp.where((experts[None, :] > experts[:, None]) & (padded[None, :] > 0), experts[None, :], N_EXPERTS)
    next_nonempty = jnp.min(later, axis=1)
    next_nonempty = jnp.where(next_nonempty == N_EXPERTS, -1, next_nonempty)
    next_e = jnp.sum(jnp.where(block_e[:, None] == experts[None, :], next_nonempty[None, :], 0), axis=1)
    return dest, block_e, next_e.astype(jnp.int32), n_active, n_blocks * EXPERT_BLOCK


def kernel(x_prompt, x_sample, c_prompt, c_sample, w_ada, b_ada, g_attn_norm, w_qkv, na_q_norm, na_k_norm, na_rpb, diff_q_norm, diff_k_norm, lambda_q1, lambda_k1, lambda_q2, lambda_k2, diff_subln, w_o, g_ffn_norm, w_router, b_router, w_gate, b_gate, w_up, b_up, w_down, b_down):
    l = 0
    groups = [(x_prompt, c_prompt), (x_sample, c_sample)]
    nb = [x.shape[0] for x, _ in groups]

    ada_all = _ada(jnp.concatenate([c for _, c in groups], axis=0), w_ada[l], b_ada[l])
    ada_all = ada_all.reshape(sum(nb), 6, D_MODEL)

    w_qkv_bf = w_qkv[l].astype(BF16)
    w_o_bf = w_o[l].astype(BF16)
    scale = HEAD_DIM ** -0.5
    reps = NA_WIDTH // HEAD_DIM
    gains = jnp.stack([jnp.tile(na_q_norm[l], reps) * (scale * LOG2E),
                       jnp.tile(na_k_norm[l], reps),
                       jnp.tile(diff_q_norm[l], reps) * (scale * LOG2E),
                       jnp.tile(diff_k_norm[l], reps)]).astype(F32)
    head_id = jnp.arange(MXU_DIM, dtype=jnp.int32) // HEAD_DIM
    bd = (head_id[:, None] == head_id[None, :]).astype(BF16)
    bias_t = _na_bias_table(na_rpb[l])
    lam_vecs = jnp.stack([lambda_q1[l], lambda_k1[l], lambda_q2[l], lambda_k2[l]]).astype(F32)
    subln_col = (diff_subln[l].astype(F32) * (1.0 - LAMBDA_INIT)).reshape(LANES, 1)
    wr = w_router[l].astype(F32)
    wr_pad = jnp.zeros((D_MODEL, LANES), F32).at[:, :N_EXPERTS].set(wr)
    wr_hi = wr_pad.astype(BF16)
    wr_lo = (wr_pad - wr_hi.astype(F32)).astype(BF16)
    br_pad = jnp.full((1, LANES), NEG_INF, F32).at[0, :N_EXPERTS].set(b_router[l].astype(F32))
    g_attn = g_attn_norm[l].reshape(1, D_MODEL).astype(F32)
    g_ffn = g_ffn_norm[l].reshape(1, D_MODEL).astype(F32)
    max_seq = max(x.shape[1] for x, _ in groups)
    cos_t, sin_t = _rope_tables(max_seq)

    rows = lax.broadcasted_iota(jnp.int32, (ROW_BLOCK, ROW_BLOCK), 0)
    cols = lax.broadcasted_iota(jnp.int32, (ROW_BLOCK, ROW_BLOCK), 1)
    tri = (cols < rows).astype(BF16)
    cnt0 = jnp.zeros((1, LANES), F32)

    bg = b_gate[l].reshape(N_EXPERTS, 1, D_MODEL).astype(F32)
    bu = b_up[l].reshape(N_EXPERTS, 1, D_MODEL).astype(F32)
    bdn = b_down[l].reshape(N_EXPERTS, 1, D_MODEL).astype(F32)
    ada_groups = [ada_all[:nb[0]], ada_all[nb[0]:]]

    order = sorted(range(len(groups)), key=lambda g: -groups[g][0].shape[1])
    staged = {}
    for g in order:
        x = groups[g][0]
        b, seq = x.shape[0], x.shape[1]
        n = b * seq
        x2d = x.reshape(n, D_MODEL)
        ada_g = ada_groups[g]
        qkv = _qkv(x2d, ada_g, g_attn, w_qkv_bf, gains, cos_t, sin_t, bd, seq)
        o_na = _na(qkv, bias_t, b, seq)
        o_df = _diff(qkv, lam_vecs, subln_col, b, seq)
        x1, h2, ti, tw, cnt = _wo(o_na, o_df, x2d, ada_g, w_o_bf, g_ffn, wr_hi, wr_lo, br_pad, tri, cnt0, seq)
        counts = cnt[0, :N_EXPERTS].astype(jnp.int32)
        dest, block_e, next_e, n_active, cap = _routing(ti[:, :TOP_K], ti[:, TOP_K:2 * TOP_K], counts, n)
        xs = _sc_scatter(h2, dest, cap)
        staged[g] = (x1, tw, ada_g, dest, block_e, next_e, n_active, xs, seq, b)

    sorted_out = {}
    for g in order:
        x1, tw, ada_g, dest, block_e, next_e, n_active, xs, seq, b = staged[g]
        sorted_out[g] = _experts(block_e, next_e, n_active, xs, w_gate[l], bg, w_up[l], bu, w_down[l], bdn)

    outs = [None] * len(groups)
    for g in order:
        x1, tw, ada_g, dest, block_e, next_e, n_active, xs, seq, b = staged[g]
        ys = _sc_gather(sorted_out[g], dest)
        outs[g] = _combine(x1, ys, tw, ada_g, seq, 0).reshape(b, seq, D_MODEL)
    return tuple(outs)
```

```python
import functools
import math

import jax
import jax.numpy as jnp
from jax import lax
from jax.experimental import pallas as pl
from jax.experimental.pallas import tpu as pltpu
from jax.experimental.pallas import tpu_sc as plsc

F32 = jnp.float32
BF16 = jnp.bfloat16
U32 = jnp.uint32

D_MODEL = 1024
HEAD_DIM = 64
NA_HEADS = 8
NA_WIDTH = 512
DIFF_HEADS = 4
DIFF_WIDTH = 512
QKV_COLS = 3072
GRID_W = 64
NA_WIN_ROWS = 8
NA_WIN_COLS = 16
ROPE_THETA = 10000.0
N_EXPERTS = 32
TOP_K = 4
SWIGLU_LIMIT = 7.0
SWIGLU_ALPHA = 1.702
EPS = 1e-5
NEG_INF = -1e30
LAMBDA_INIT = 0.8 - 0.6 * math.exp(-0.3 * 0)
LOG2E = 1.4426950408889634

LANES = 128
MXU_DIM = 256
VMEM_LIMIT = 56 * 1024 * 1024

ROW_BLOCK = 512
Q_BLOCK = 512
EXPERT_BLOCK = 512
EXPERT_SUB = 256
NA_ROWS_PER_TRIP = 8
VT_ROWS = LANES + 16
KV_CHUNK = 256
NORM_SLACK = 1.01
MAX_SAFE_BOUND = 60.0


PACKED = D_MODEL // 2
SC_CORES = 2
SC_SUBCORES = 16
SC_WORKERS = SC_CORES * SC_SUBCORES
GATHER_ROWS = 64


def _params(*sem):
    return pltpu.CompilerParams(dimension_semantics=sem, vmem_limit_bytes=VMEM_LIMIT)


def _pack_halves(x):
    w = x.shape[1] // 2
    bits = lax.bitcast_convert_type(x, U32)
    return (bits[:, :w] >> 16) | bits[:, w:]


def _col_reduce(x, op):
    while x.shape[0] >= 64:
        x = op(x.reshape(8, x.shape[0] // 8, x.shape[1]), axis=0)
    return op(x, axis=0, keepdims=True)


def _unpack_halves(word):
    lo = lax.bitcast_convert_type(word << 16, F32)
    hi = lax.bitcast_convert_type(word & jnp.uint32(0xFFFF0000), F32)
    return lo, hi


def _ada_kernel(c_ref, w_ref, b_ref, o_ref):
    c = c_ref[...]
    s = c * jax.nn.sigmoid(c)
    o_ref[...] = jnp.dot(s, w_ref[...], preferred_element_type=F32,
                         precision=lax.Precision.HIGHEST) + b_ref[...]


def _ada(c_all, w_ada, b_ada):
    nb = c_all.shape[0]
    n_out = w_ada.shape[1]
    blk = D_MODEL
    return pl.pallas_call(
        _ada_kernel,
        grid=(n_out // blk,),
        in_specs=[pl.BlockSpec((nb, D_MODEL), lambda j: (0, 0)),
                  pl.BlockSpec((D_MODEL, blk), lambda j: (0, j)),
                  pl.BlockSpec((1, blk), lambda j: (0, j))],
        out_specs=pl.BlockSpec((nb, blk), lambda j: (0, j)),
        out_shape=jax.ShapeDtypeStruct((nb, n_out), F32),
        compiler_params=_params("arbitrary"),
        name="ada",
    )(c_all, w_ada, b_ada.reshape(1, n_out))


def _head_sumsq(y, bd):
    sq = (y * y).astype(BF16)
    parts = [jnp.dot(sq[:, c:c + MXU_DIM], bd, preferred_element_type=F32)
             for c in range(0, y.shape[1], MXU_DIM)]
    return jnp.concatenate(parts, axis=1)


def _qkv_kernel(x_ref, ada_ref, g_ref, w_ref, gain_ref, cos_ref, sin_ref, bd_ref, o_ref):
    x = x_ref[...]
    ms = jnp.mean(x * x, axis=-1, keepdims=True)
    xn = x * lax.rsqrt(ms + EPS) * g_ref[...]
    sh = ada_ref[0, 0:1, :]
    sc = ada_ref[0, 1:2, :]
    h = (xn * (1.0 + sc) + sh).astype(BF16)
    bd = bd_ref[...]
    lane = lax.broadcasted_iota(jnp.int32, (x.shape[0], NA_WIDTH), 1)
    first_half = (lane & (HEAD_DIM // 2)) == 0
    for grp in range(6):
        cols = slice(grp * 512, (grp + 1) * 512)
        acc = jnp.dot(h, w_ref[:, cols], preferred_element_type=F32)
        if grp in (2, 5):
            o_ref[:, cols] = acc.astype(BF16)
            continue
        gi = {0: 0, 1: 1, 3: 2, 4: 3}[grp]
        ss = _head_sumsq(acc, bd)
        y = acc * lax.rsqrt(ss * (1.0 / HEAD_DIM) + EPS) * gain_ref[gi:gi + 1, :]
        if grp in (3, 4):
            partner = jnp.where(first_half,
                                pltpu.roll(y, NA_WIDTH - HEAD_DIM // 2, axis=1),
                                pltpu.roll(y, HEAD_DIM // 2, axis=1))
            y = y * cos_ref[...] + partner * sin_ref[...]
        o_ref[:, cols] = y.astype(BF16)


def _qkv(x2d, ada_g, g_attn, w_qkv_bf, gains, cos_t, sin_t, bd, seq):
    n = x2d.shape[0]
    tm = ROW_BLOCK
    per_seq = seq // tm
    return pl.pallas_call(
        _qkv_kernel,
        grid=(n // tm,),
        in_specs=[pl.BlockSpec((tm, D_MODEL), lambda i: (i, 0)),
                  pl.BlockSpec((1, 6, D_MODEL), lambda i: (i // per_seq, 0, 0)),
                  pl.BlockSpec((1, D_MODEL), lambda i: (0, 0)),
                  pl.BlockSpec((D_MODEL, QKV_COLS), lambda i: (0, 0)),
                  pl.BlockSpec((4, NA_WIDTH), lambda i: (0, 0)),
                  pl.BlockSpec((tm, DIFF_WIDTH), lambda i: (i % per_seq, 0)),
                  pl.BlockSpec((tm, DIFF_WIDTH), lambda i: (i % per_seq, 0)),
                  pl.BlockSpec((MXU_DIM, MXU_DIM), lambda i: (0, 0))],
        out_specs=pl.BlockSpec((tm, QKV_COLS), lambda i: (i, 0)),
        out_shape=jax.ShapeDtypeStruct((n, QKV_COLS), BF16),
        compiler_params=_params("arbitrary"),
        name="qkv",
    )(x2d, ada_g, g_attn, w_qkv_bf, gains, cos_t, sin_t, bd)


def _na_kernel(q_ref, k_ref, v_ref, bias_ref, o_ref, vaug_ref, *, rows):
    lane = lax.broadcasted_iota(jnp.int32, (GRID_W, LANES), 1)
    head0 = lane < HEAD_DIM
    win = NA_WIN_ROWS * GRID_W

    def window_start(r):
        return jnp.clip(r - NA_WIN_ROWS // 2, 0, rows - NA_WIN_ROWS)

    def scores(r):
        r_start = window_start(r)
        q = q_ref[pl.ds(pl.multiple_of(r * GRID_W, GRID_W), GRID_W), :]
        kw = k_ref[pl.ds(pl.multiple_of(r_start * GRID_W, GRID_W), win), :]
        zero = jnp.zeros_like(q)
        qm = jnp.concatenate([jnp.where(head0, q, zero), jnp.where(head0, zero, q)], axis=0)
        s = lax.dot_general(kw, qm, (((1,), (1,)), ((), ())), preferred_element_type=F32)
        return s + bias_ref[0, r - r_start]

    vaug_ref[:, :LANES] = v_ref[...]
    vaug_ref[:, LANES:] = jnp.ones((v_ref.shape[0], LANES), BF16)

    def finish(r, s):
        vw = vaug_ref[pl.ds(pl.multiple_of(window_start(r) * GRID_W, GRID_W), win), :]
        m = _col_reduce(s, jnp.max)
        p = jnp.exp2(s - m).astype(BF16)
        o2 = lax.dot_general(p, vw, (((0,), (0,)), ((), ())), preferred_element_type=F32)
        o2 = o2[:, :LANES] * (1.0 / o2[:, LANES:])
        o = jnp.where(head0, o2[:GRID_W], o2[GRID_W:])
        o_ref[pl.ds(pl.multiple_of(r * GRID_W, GRID_W), GRID_W), :] = o.astype(BF16)

    def body(i, carry):
        trip_rows = [i * NA_ROWS_PER_TRIP + u for u in range(NA_ROWS_PER_TRIP)]
        trip_scores = [scores(r) for r in trip_rows]
        for r, s in zip(trip_rows, trip_scores):
            finish(r, s)
        return carry

    lax.fori_loop(0, rows // NA_ROWS_PER_TRIP, body, 0)


def _na(qkv, bias_t, batch, seq):
    rows = seq // GRID_W
    n_pairs = NA_HEADS // 2
    return pl.pallas_call(
        functools.partial(_na_kernel, rows=rows),
        grid=(batch, n_pairs),
        in_specs=[pl.BlockSpec((seq, LANES), lambda b, hp: (b, hp)),
                  pl.BlockSpec((seq, LANES), lambda b, hp: (b, n_pairs + hp)),
                  pl.BlockSpec((seq, LANES), lambda b, hp: (b, 2 * n_pairs + hp)),
                  pl.BlockSpec((1, NA_WIN_ROWS, NA_WIN_ROWS * GRID_W, LANES), lambda b, hp: (hp, 0, 0, 0))],
        out_specs=pl.BlockSpec((seq, LANES), lambda b, hp: (b, hp)),
        out_shape=jax.ShapeDtypeStruct((batch * seq, NA_WIDTH), BF16),
        scratch_shapes=[pltpu.VMEM((seq, 2 * LANES), BF16)],
        compiler_params=_params("arbitrary", "arbitrary"),
        name="na_attn",
    )(qkv, qkv, qkv, bias_t)


def _diff_kernel(q_ref, k_ref, v_ref, lam_ref, g_ref, o_ref, vt_ref, kn_ref, oa_ref):
    @pl.when(pl.program_id(2) == 0)
    def _():
        vt_ref[:LANES, :] = v_ref[...].astype(F32).T.astype(BF16)
        ones_row = lax.broadcasted_iota(jnp.int32, (VT_ROWS - LANES, v_ref.shape[0]), 0) == 0
        vt_ref[LANES:, :] = jnp.where(ones_row, 1.0, 0.0).astype(BF16)
        kf = k_ref[...].astype(F32)
        d_id = lax.broadcasted_iota(jnp.int32, (LANES, LANES), 0) // HEAD_DIM
        c_id = lax.broadcasted_iota(jnp.int32, (LANES, LANES), 1)
        comp_sel = jnp.where(d_id == c_id, 1.0, 0.0).astype(BF16)
        kn2 = jnp.dot((kf * kf).astype(BF16), comp_sel, preferred_element_type=F32)
        kn_ref[...] = jnp.sqrt(_col_reduce(kn2, jnp.max)) * NORM_SLACK

    lq1 = lam_ref[0:1, :]
    lk1 = lam_ref[1:2, :]
    lq2 = lam_ref[2:3, :]
    lk2 = lam_ref[3:4, :]
    lam = (jnp.exp(jnp.sum(lq1 * lk1, axis=-1, keepdims=True))
           - jnp.exp(jnp.sum(lq2 * lk2, axis=-1, keepdims=True)) + LAMBDA_INIT)

    q = q_ref[...]
    lane = lax.broadcasted_iota(jnp.int32, q.shape, 1)
    zero = jnp.zeros_like(q)
    tq = q.shape[0]
    qcat = jnp.concatenate([jnp.where(lane < HEAD_DIM, q, zero), jnp.where(lane < HEAD_DIM, zero, q)], axis=0)
    def scores(c, chunk):
        kc = k_ref[c * chunk:(c + 1) * chunk, :]
        return lax.dot_general(kc, qcat, (((1,), (1,)), ((), ())), preferred_element_type=F32)

    def chunked(chunk, step):
        n_chunks = k_ref.shape[0] // chunk
        s_next = scores(0, chunk)
        state = None
        for c in range(n_chunks):
            s = s_next
            if c + 1 < n_chunks:
                s_next = scores(c + 1, chunk)
            state = step(c, s, vt_ref[:, c * chunk:(c + 1) * chunk], state)
        return state

    qf = qcat.astype(F32)
    ones8 = jnp.ones((8, LANES), BF16)
    qn2 = lax.dot_general(ones8, (qf * qf).astype(BF16), (((1,), (1,)), ((), ())), preferred_element_type=F32)
    col = lax.broadcasted_iota(jnp.int32, (1, 2 * tq), 1)
    bound = jnp.sqrt(qn2[0:1, :]) * NORM_SLACK * jnp.where(col < tq, kn_ref[0:1, 0:1], kn_ref[0:1, 1:2])
    in_range = jnp.max(bound) <= MAX_SAFE_BOUND

    @pl.when(in_range)
    def _():
        def step(c, s, vtc, acc):
            pv = jnp.dot(vtc, jnp.exp2(s - bound).astype(BF16), preferred_element_type=F32)
            return pv if c == 0 else acc + pv
        oa_ref[...] = chunked(KV_CHUNK, step)

    @pl.when(jnp.logical_not(in_range))
    def _():
        def step(c, s, vtc, state):
            mc = _col_reduce(s, jnp.max)
            m_new = mc if c == 0 else jnp.maximum(state[0], mc)
            pv = jnp.dot(vtc, jnp.exp2(s - m_new).astype(BF16), preferred_element_type=F32)
            return (m_new, pv if c == 0 else jnp.exp2(state[0] - m_new) * state[1] + pv)
        oa_ref[...] = chunked(KV_CHUNK, step)[1]

    oa = oa_ref[...]
    o0, l0 = oa[:LANES, :tq], oa[LANES:LANES + 1, :tq]
    o1, l1 = oa[:LANES, tq:], oa[LANES:LANES + 1, tq:]
    o = o0 * (1.0 / l0) - (lam / l1) * o1
    ms = jnp.mean(o * o, axis=0, keepdims=True)
    y = o * lax.rsqrt(ms + EPS) * g_ref[...]
    o_ref[...] = y.T.astype(BF16)


def _diff(qkv, lam_vecs, subln_col, batch, seq):
    tq = Q_BLOCK
    nq = seq // tq
    base = 3 * NA_WIDTH // LANES
    nh = DIFF_HEADS
    return pl.pallas_call(
        _diff_kernel,
        grid=(batch, nh, nq),
        in_specs=[pl.BlockSpec((tq, LANES), lambda b, h, i: (b * nq + i, base + h)),
                  pl.BlockSpec((seq, LANES), lambda b, h, i: (b, base + nh + h)),
                  pl.BlockSpec((seq, LANES), lambda b, h, i: (b, base + 2 * nh + h)),
                  pl.BlockSpec((4, HEAD_DIM), lambda b, h, i: (0, 0)),
                  pl.BlockSpec((LANES, 1), lambda b, h, i: (0, 0))],
        out_specs=pl.BlockSpec((tq, LANES), lambda b, h, i: (b * nq + i, h)),
        out_shape=jax.ShapeDtypeStruct((batch * seq, DIFF_WIDTH), BF16),
        scratch_shapes=[pltpu.VMEM((VT_ROWS, seq), BF16),
                        pltpu.VMEM((1, LANES), F32),
                        pltpu.VMEM((VT_ROWS, 2 * tq), F32)],
        compiler_params=_params("arbitrary", "arbitrary", "arbitrary"),
        name="diff_attn",
    )(qkv, qkv, qkv, lam_vecs, subln_col)


def _wo_kernel(ona_ref, odf_ref, x_ref, ada_ref, wo_ref, g_ref, wrh_ref, wrl_ref, br_ref, tri_ref, cnt0_ref,
               x1_ref, h2_ref, ti_ref, tw_ref, cnt_ref):
    mix = (jnp.dot(ona_ref[...], wo_ref[:NA_WIDTH, :], preferred_element_type=F32)
           + jnp.dot(odf_ref[...], wo_ref[NA_WIDTH:, :], preferred_element_type=F32))
    gt1 = ada_ref[0, 2:3, :]
    sh2 = ada_ref[0, 3:4, :]
    sc2 = ada_ref[0, 4:5, :]
    x1 = x_ref[...] + gt1 * mix
    x1_ref[...] = x1
    ms = jnp.mean(x1 * x1, axis=-1, keepdims=True)
    h2 = x1 * lax.rsqrt(ms + EPS) * g_ref[...] * (1.0 + sc2) + sh2
    hi = h2.astype(BF16)
    h2_ref[...] = _pack_halves(hi.astype(F32))
    lo = (h2 - hi.astype(F32)).astype(BF16)
    both = jnp.dot(hi, wrl_ref[...], preferred_element_type=F32)
    logits = (both + pltpu.roll(both, LANES - N_EXPERTS, axis=1)
              + jnp.dot(lo, wrh_ref[...], preferred_element_type=F32)) + br_ref[...]
    lane = lax.broadcasted_iota(jnp.int32, logits.shape, 1).astype(F32)
    vals = []
    idxs = []
    cur = logits
    for _ in range(TOP_K):
        m = jnp.max(cur, axis=-1, keepdims=True)
        idx = jnp.min(jnp.where(cur == m, lane, float(LANES)), axis=-1, keepdims=True)
        vals.append(m)
        idxs.append(idx)
        cur = jnp.where(lane == idx, -jnp.inf, cur)
    es = [jnp.exp(v - vals[0]) for v in vals]
    inv = 1.0 / (es[0] + es[1] + es[2] + es[3])

    @pl.when(pl.program_id(0) == 0)
    def _():
        cnt_ref[...] = cnt0_ref[...]

    sel = jnp.zeros(logits.shape, F32)
    for j in range(TOP_K):
        sel = sel + jnp.where(lane == idxs[j], 1.0, 0.0)
    before = jnp.dot(tri_ref[...], sel.astype(BF16), preferred_element_type=F32) + cnt_ref[...]
    cnt_ref[...] = cnt_ref[...] + jnp.sum(sel, axis=0, keepdims=True)
    ranks = [jnp.sum(jnp.where(lane == idxs[j], before, 0.0), axis=-1, keepdims=True) for j in range(TOP_K)]

    ti = jnp.zeros(logits.shape, F32)
    tw = jnp.zeros(logits.shape, F32)
    for j in range(TOP_K):
        ti = jnp.where(lane == float(j), idxs[j], ti)
        ti = jnp.where(lane == float(TOP_K + j), ranks[j], ti)
        tw = jnp.where(lane == float(j), es[j] * inv, tw)
    ti_ref[...] = ti.astype(jnp.int32)
    tw_ref[...] = tw


def _wo(o_na, o_df, x2d, ada_g, w_o_bf, g_ffn, wr_hi, wr_lo, br_pad, tri, cnt0, seq):
    n = x2d.shape[0]
    tm = ROW_BLOCK
    per_seq = seq // tm
    row = lambda i: (i, 0)
    const = lambda i: (0, 0)
    return pl.pallas_call(
        _wo_kernel,
        grid=(n // tm,),
        in_specs=[pl.BlockSpec((tm, NA_WIDTH), row),
                  pl.BlockSpec((tm, DIFF_WIDTH), row),
                  pl.BlockSpec((tm, D_MODEL), row),
                  pl.BlockSpec((1, 6, D_MODEL), lambda i: (i // per_seq, 0, 0)),
                  pl.BlockSpec((D_MODEL, D_MODEL), const),
                  pl.BlockSpec((1, D_MODEL), const),
                  pl.BlockSpec((D_MODEL, LANES), const),
                  pl.BlockSpec((D_MODEL, LANES), const),
                  pl.BlockSpec((1, LANES), const),
                  pl.BlockSpec((tm, tm), const),
                  pl.BlockSpec((1, LANES), const)],
        out_specs=[pl.BlockSpec((tm, D_MODEL), row),
                   pl.BlockSpec((tm, PACKED), row),
                   pl.BlockSpec((tm, LANES), row),
                   pl.BlockSpec((tm, LANES), row),
                   pl.BlockSpec((1, LANES), const)],
        out_shape=[jax.ShapeDtypeStruct((n, D_MODEL), F32),
                   jax.ShapeDtypeStruct((n, PACKED), U32),
                   jax.ShapeDtypeStruct((n, LANES), jnp.int32),
                   jax.ShapeDtypeStruct((n, LANES), F32),
                   jax.ShapeDtypeStruct((1, LANES), F32)],
        compiler_params=_params("arbitrary"),
        name="wo_router",
    )(o_na, o_df, x2d, ada_g, w_o_bf, g_ffn, wr_hi, wr_lo, br_pad, tri, cnt0)


def _expert_kernel(be_ref, nxt_ref, na_ref, xs_ref, wg_hbm, bg_ref, wu_hbm, bu_ref, wd_hbm, bd_ref, o_ref,
                   w_f32, w_bf, sems, slot_ref):
    i = pl.program_id(0)
    active = i < na_ref[0]
    expert = be_ref[i]
    new_expert = jnp.logical_or(i == 0, expert != be_ref[jnp.maximum(i - 1, 0)])
    wg_bf, wu_bf, wd_bf = w_bf.at[0], w_bf.at[1], w_bf.at[2]

    def weight_copies(src_expert, slot):
        return [pltpu.make_async_copy(w_hbm.at[src_expert], w_f32.at[slot, j], sems.at[slot, j])
                for j, w_hbm in enumerate((wg_hbm, wu_hbm, wd_hbm))]

    @pl.when(i == 0)
    def _():
        slot_ref[0] = 0
        for cp in weight_copies(expert, 0):
            cp.start()

    @pl.when(jnp.logical_and(active, new_expert))
    def _():
        slot = slot_ref[0]
        nxt = nxt_ref[i]
        for s in range(2):
            @pl.when(slot == s)
            def _():
                @pl.when(nxt >= 0)
                def _():
                    for cp in weight_copies(nxt, 1 - s):
                        cp.start(priority=1)
                for j, cp in enumerate(weight_copies(expert, s)):
                    cp.wait()
                    w_bf[j] = w_f32[s, j].astype(BF16)
        slot_ref[0] = 1 - slot

    @pl.when(active)
    def _():
        def gate_up(rows):
            x_lo, x_hi = _unpack_halves(xs_ref[rows, :])
            x_lo = x_lo.astype(BF16)
            x_hi = x_hi.astype(BF16)

            def proj(w_bf):
                return (jnp.dot(x_lo, w_bf[:PACKED, :], preferred_element_type=F32)
                        + jnp.dot(x_hi, w_bf[PACKED:, :], preferred_element_type=F32))

            return proj(wg_bf), proj(wu_bf)

        def act_down(rows, gu):
            g = jnp.minimum(gu[0] + bg_ref[0], SWIGLU_LIMIT)
            u = jnp.clip(gu[1] + bu_ref[0], -SWIGLU_LIMIT, SWIGLU_LIMIT)
            act = g * jax.nn.sigmoid(SWIGLU_ALPHA * g) * (u + 1.0)
            out = jnp.dot(act.astype(BF16), wd_bf[...], preferred_element_type=F32) + bd_ref[0]
            o_ref[rows, :] = _pack_halves(out.astype(BF16).astype(F32))

        sub = [pl.ds(r, EXPERT_SUB) for r in range(0, EXPERT_BLOCK, EXPERT_SUB)]
        gu_next = gate_up(sub[0])
        for j, rows in enumerate(sub):
            gu = gu_next
            if j + 1 < len(sub):
                gu_next = gate_up(sub[j + 1])
            act_down(rows, gu)

    @pl.when(i >= na_ref[0])
    def _():
        o_ref[...] = jnp.zeros_like(o_ref)


def _experts(block_e, next_e, n_active, xs, wg, bg, wu, bu, wd, bd):
    cap = xs.shape[0]
    n_blocks = cap // EXPERT_BLOCK
    xmap = lambda i, be, nx, na: (jnp.minimum(i, na[0] - 1), 0)
    bmap = lambda i, be, nx, na: (be[i], 0, 0)
    hbm = pl.BlockSpec(memory_space=pl.ANY)
    grid_spec = pltpu.PrefetchScalarGridSpec(
        num_scalar_prefetch=3,
        grid=(n_blocks,),
        in_specs=[pl.BlockSpec((EXPERT_BLOCK, PACKED), xmap),
                  hbm, pl.BlockSpec((1, 1, D_MODEL), bmap),
                  hbm, pl.BlockSpec((1, 1, D_MODEL), bmap),
                  hbm, pl.BlockSpec((1, 1, D_MODEL), bmap)],
        out_specs=pl.BlockSpec((EXPERT_BLOCK, PACKED), lambda i, be, nx, na: (i, 0)),
        scratch_shapes=[pltpu.VMEM((2, 3, D_MODEL, D_MODEL), F32),
                        pltpu.VMEM((3, D_MODEL, D_MODEL), BF16),
                        pltpu.SemaphoreType.DMA((2, 3)),
                        pltpu.SMEM((1,), jnp.int32)],
    )
    return pl.pallas_call(
        _expert_kernel,
        grid_spec=grid_spec,
        out_shape=jax.ShapeDtypeStruct((cap, PACKED), U32),
        compiler_params=_params("arbitrary"),
        name="experts",
    )(block_e, next_e, n_active, xs, wg, bg, wu, bu, wd, bd)


def _sc_gather(table, idx):
    n_out = idx.shape[0]
    width = table.shape[1]
    per_worker = n_out // SC_WORKERS
    n_chunks = per_worker // GATHER_ROWS
    assert per_worker * SC_WORKERS == n_out and n_chunks * GATHER_ROWS == per_worker and n_chunks % 2 == 0
    idx3 = idx.reshape(SC_WORKERS, n_chunks, GATHER_ROWS)
    mesh = plsc.VectorSubcoreMesh(core_axis_name="core", subcore_axis_name="subcore")

    @functools.partial(
        pl.kernel, mesh=mesh,
        out_type=jax.ShapeDtypeStruct((n_out, width), table.dtype),
        scratch_types=[pltpu.VMEM((n_chunks, GATHER_ROWS), jnp.int32),
                       pltpu.VMEM((2, GATHER_ROWS, width), table.dtype),
                       pltpu.SemaphoreType.DMA((2,)),
                       pltpu.SemaphoreType.DMA((2,))])
    def gather_kernel(table_hbm, idx_hbm, out_hbm, idx_v, rows_v, gsem, wsem):
        wid = lax.axis_index("subcore") * SC_CORES + lax.axis_index("core")
        base = wid * per_worker
        pltpu.sync_copy(idx_hbm.at[wid], idx_v)

        def gather(j, slot):
            return pltpu.make_async_copy(table_hbm.at[idx_v.at[j]], rows_v.at[slot], gsem.at[slot])

        def write(j, slot):
            dst = out_hbm.at[pl.ds(pl.multiple_of(base + j * GATHER_ROWS, GATHER_ROWS), GATHER_ROWS)]
            return pltpu.make_async_copy(rows_v.at[slot], dst, wsem.at[slot])

        gather(0, 0).start()

        @pl.loop(0, n_chunks, step=2)
        def _(j):
            for slot in range(2):
                jj = j + slot
                gather(jj, slot).wait()

                @pl.when(jj >= 1)
                def _():
                    write(jj - 1, 1 - slot).wait()

                @pl.when(jj + 1 < n_chunks)
                def _():
                    gather(jj + 1, 1 - slot).start()

                write(jj, slot).start()

        write(n_chunks - 1, 1).wait()

    return gather_kernel(table, idx3)


def _sc_scatter(rows, idx, n_out):
    n_src, width = rows.shape
    n_idx = idx.shape[0]
    per_worker = n_idx // SC_WORKERS
    n_chunks = per_worker // GATHER_ROWS
    assert per_worker * SC_WORKERS == n_idx and n_chunks * GATHER_ROWS == per_worker and n_chunks % 2 == 0
    assert n_src % per_worker == 0
    idx3 = idx.reshape(SC_WORKERS, n_chunks, GATHER_ROWS)
    mesh = plsc.VectorSubcoreMesh(core_axis_name="core", subcore_axis_name="subcore")

    @functools.partial(
        pl.kernel, mesh=mesh,
        out_type=jax.ShapeDtypeStruct((n_out, width), rows.dtype),
        scratch_types=[pltpu.VMEM((n_chunks, GATHER_ROWS), jnp.int32),
                       pltpu.VMEM((2, GATHER_ROWS, width), rows.dtype),
                       pltpu.SemaphoreType.DMA((2,)),
                       pltpu.SemaphoreType.DMA((2,))])
    def scatter_kernel(rows_hbm, idx_hbm, out_hbm, idx_v, rows_v, rsem, wsem):
        wid = lax.axis_index("subcore") * SC_CORES + lax.axis_index("core")
        base = lax.rem(wid * per_worker, n_src)
        pltpu.sync_copy(idx_hbm.at[wid], idx_v)

        def read(j, slot):
            src = rows_hbm.at[pl.ds(pl.multiple_of(base + j * GATHER_ROWS, GATHER_ROWS), GATHER_ROWS)]
            return pltpu.make_async_copy(src, rows_v.at[slot], rsem.at[slot])

        def write(j, slot):
            return pltpu.make_async_copy(rows_v.at[slot], out_hbm.at[idx_v.at[j]], wsem.at[slot])

        read(0, 0).start()

        @pl.loop(0, n_chunks, step=2)
        def _(j):
            for slot in range(2):
                jj = j + slot
                read(jj, slot).wait()

                @pl.when(jj >= 1)
                def _():
                    write(jj - 1, 1 - slot).wait()

                @pl.when(jj + 1 < n_chunks)
                def _():
                    read(jj + 1, 1 - slot).start()

                write(jj, slot).start()

        write(n_chunks - 1, 1).wait()

    return scatter_kernel(rows, idx3)


def _combine_kernel(x1_ref, y0_ref, y1_ref, y2_ref, y3_ref, tw_ref, ada_ref, o_ref):
    tw = tw_ref[...]
    acc_lo = jnp.zeros((x1_ref.shape[0], PACKED), F32)
    acc_hi = jnp.zeros((x1_ref.shape[0], PACKED), F32)
    for j, y_ref in enumerate((y0_ref, y1_ref, y2_ref, y3_ref)):
        lo, hi = _unpack_halves(y_ref[...])
        acc_lo = acc_lo + tw[:, j:j + 1] * lo
        acc_hi = acc_hi + tw[:, j:j + 1] * hi
    o_ref[:, :PACKED] = x1_ref[:, :PACKED] + ada_ref[0, 5:6, :PACKED] * acc_lo
    o_ref[:, PACKED:] = x1_ref[:, PACKED:] + ada_ref[0, 5:6, PACKED:] * acc_hi


def _combine(x1, ys, tw, ada_g, seq, row_off):
    n = x1.shape[0]
    tm = ROW_BLOCK
    per_seq = seq // tm
    off = row_off // tm
    per_choice = tw.shape[0] // tm
    y_specs = [pl.BlockSpec((tm, PACKED), functools.partial(lambda i, j: (j * per_choice + off + i, 0), j=j))
               for j in range(TOP_K)]
    return pl.pallas_call(
        _combine_kernel,
        grid=(n // tm,),
        in_specs=[pl.BlockSpec((tm, D_MODEL), lambda i: (i, 0)),
                  *y_specs,
                  pl.BlockSpec((tm, LANES), lambda i: (i + off, 0)),
                  pl.BlockSpec((1, 6, D_MODEL), lambda i: (i // per_seq, 0, 0))],
        out_specs=pl.BlockSpec((tm, D_MODEL), lambda i: (i, 0)),
        out_shape=jax.ShapeDtypeStruct((n, D_MODEL), F32),
        compiler_params=_params("arbitrary"),
        name="combine",
    )(x1, ys, ys, ys, ys, tw, ada_g)


def _rope_tables(seq):
    half = HEAD_DIM // 2
    inv = ROPE_THETA ** (-jnp.arange(half, dtype=F32) / half)
    ang = jnp.arange(seq, dtype=F32)[:, None] * inv[None, :]
    cos, sin = jnp.cos(ang), jnp.sin(ang)
    cos_h = jnp.concatenate([cos, cos], axis=-1)
    sin_h = jnp.concatenate([-sin, sin], axis=-1)
    reps = DIFF_WIDTH // HEAD_DIM
    return jnp.tile(cos_h, (1, reps)), jnp.tile(sin_h, (1, reps))


def _na_bias_table(rpb):
    cols = jnp.arange(GRID_W, dtype=jnp.int32)
    c_start = jnp.clip(cols - NA_WIN_COLS // 2, 0, GRID_W - NA_WIN_COLS)
    col_mask = (cols[None, :] >= c_start[:, None]) & (cols[None, :] < c_start[:, None] + NA_WIN_COLS)
    col_idx = jnp.clip(cols[None, :] - cols[:, None], -(NA_WIN_COLS - 1), NA_WIN_COLS - 1) + NA_WIN_COLS - 1
    delta = jnp.arange(NA_WIN_ROWS, dtype=jnp.int32)
    j = jnp.arange(NA_WIN_ROWS, dtype=jnp.int32)
    row_idx = j[None, :] - delta[:, None] + NA_WIN_ROWS - 1
    row_hot = (row_idx[:, :, None] == jnp.arange(2 * NA_WIN_ROWS - 1, dtype=jnp.int32)).astype(F32)
    col_hot = (col_idx[:, :, None] == jnp.arange(2 * NA_WIN_COLS - 1, dtype=jnp.int32)).astype(F32)
    bias = jnp.einsum('djr,hrc,qkc->hdjqk', row_hot, rpb.astype(F32), col_hot,
                      precision=lax.Precision.HIGHEST)
    bias = jnp.where(col_mask[None, None, None], bias * LOG2E, NEG_INF)
    bias = bias.transpose(0, 1, 2, 4, 3).reshape(NA_HEADS, NA_WIN_ROWS, NA_WIN_ROWS * GRID_W, GRID_W)
    bias = bias.reshape(NA_HEADS // 2, 2, NA_WIN_ROWS, NA_WIN_ROWS * GRID_W, GRID_W)
    return jnp.concatenate([bias[:, 0], bias[:, 1]], axis=-1)


def _routing(top_idx, rank, counts, n):
    n_blocks = n * TOP_K // EXPERT_BLOCK + N_EXPERTS
    experts = jnp.arange(N_EXPERTS, dtype=jnp.int32)
    padded = (counts + EXPERT_BLOCK - 1) // EXPERT_BLOCK * EXPERT_BLOCK
    pad_end = jnp.cumsum(padded)
    pad_start = pad_end - padded
    start_of = jnp.sum(jnp.where(top_idx[:, :, None] == experts, pad_start, 0), axis=-1)
    dest = (start_of + rank).T.reshape(-1)
    block_lo = jnp.arange(n_blocks, dtype=jnp.int32) * EXPERT_BLOCK
    block_e = jnp.minimum(jnp.sum((pad_end[None, :] <= block_lo[:, None]).astype(jnp.int32), axis=1),
                          N_EXPERTS - 1).astype(jnp.int32)
    n_active = (pad_end[-1] // EXPERT_BLOCK).astype(jnp.int32).reshape(1)
    later = jnp.where((experts[None, :] > experts[:, None]) & (padded[None, :] > 0), experts[None, :], N_EXPERTS)
    next_nonempty = jnp.min(later, axis=1)
    next_nonempty = jnp.where(next_nonempty == N_EXPERTS, -1, next_nonempty)
    next_e = jnp.sum(jnp.where(block_e[:, None] == experts[None, :], next_nonempty[None, :], 0), axis=1)
    return dest, block_e, next_e.astype(jnp.int32), n_active, n_blocks * EXPERT_BLOCK


def kernel(x_prompt, x_sample, c_prompt, c_sample, w_ada, b_ada, g_attn_norm, w_qkv, na_q_norm, na_k_norm, na_rpb, diff_q_norm, diff_k_norm, lambda_q1, lambda_k1, lambda_q2, lambda_k2, diff_subln, w_o, g_ffn_norm, w_router, b_router, w_gate, b_gate, w_up, b_up, w_down, b_down):
    l = 0
    groups = [(x_prompt, c_prompt), (x_sample, c_sample)]
    nb = [x.shape[0] for x, _ in groups]

    ada_all = _ada(jnp.concatenate([c for _, c in groups], axis=0), w_ada[l], b_ada[l])
    ada_all = ada_all.reshape(sum(nb), 6, D_MODEL)

    w_qkv_bf = w_qkv[l].astype(BF16)
    w_o_bf = w_o[l].astype(BF16)
    scale = HEAD_DIM ** -0.5
    reps = NA_WIDTH // HEAD_DIM
    gains = jnp.stack([jnp.tile(na_q_norm[l], reps) * (scale * LOG2E),
                       jnp.tile(na_k_norm[l], reps),
                       jnp.tile(diff_q_norm[l], reps) * (scale * LOG2E),
                       jnp.tile(diff_k_norm[l], reps)]).astype(F32)
    head_id = jnp.arange(MXU_DIM, dtype=jnp.int32) // HEAD_DIM
    bd = (head_id[:, None] == head_id[None, :]).astype(BF16)
    bias_t = _na_bias_table(na_rpb[l])
    lam_vecs = jnp.stack([lambda_q1[l], lambda_k1[l], lambda_q2[l], lambda_k2[l]]).astype(F32)
    subln_col = (diff_subln[l].astype(F32) * (1.0 - LAMBDA_INIT)).reshape(LANES, 1)
    wr = w_router[l].astype(F32)
    wr_pad = jnp.zeros((D_MODEL, LANES), F32).at[:, :N_EXPERTS].set(wr)
    wr_hi = wr_pad.astype(BF16)
    wr_lo = (wr_pad - wr_hi.astype(F32)).astype(BF16)
    wr_lo = wr_hi.at[:, N_EXPERTS:2 * N_EXPERTS].set(wr_lo[:, :N_EXPERTS])
    br_pad = jnp.full((1, LANES), NEG_INF, F32).at[0, :N_EXPERTS].set(b_router[l].astype(F32))
    g_attn = g_attn_norm[l].reshape(1, D_MODEL).astype(F32)
    g_ffn = g_ffn_norm[l].reshape(1, D_MODEL).astype(F32)
    max_seq = max(x.shape[1] for x, _ in groups)
    cos_t, sin_t = _rope_tables(max_seq)

    rows = lax.broadcasted_iota(jnp.int32, (ROW_BLOCK, ROW_BLOCK), 0)
    cols = lax.broadcasted_iota(jnp.int32, (ROW_BLOCK, ROW_BLOCK), 1)
    tri = (cols < rows).astype(BF16)
    cnt0 = jnp.zeros((1, LANES), F32)

    bg = b_gate[l].reshape(N_EXPERTS, 1, D_MODEL).astype(F32)
    bu = b_up[l].reshape(N_EXPERTS, 1, D_MODEL).astype(F32)
    bdn = b_down[l].reshape(N_EXPERTS, 1, D_MODEL).astype(F32)
    ada_groups = [ada_all[:nb[0]], ada_all[nb[0]:]]

    order = sorted(range(len(groups)), key=lambda g: -groups[g][0].shape[1])
    staged = {}
    for g in order:
        x = groups[g][0]
        b, seq = x.shape[0], x.shape[1]
        n = b * seq
        x2d = x.reshape(n, D_MODEL)
        ada_g = ada_groups[g]
        qkv = _qkv(x2d, ada_g, g_attn, w_qkv_bf, gains, cos_t, sin_t, bd, seq)
        o_na = _na(qkv, bias_t, b, seq)
        o_df = _diff(qkv, lam_vecs, subln_col, b, seq)
        x1, h2, ti, tw, cnt = _wo(o_na, o_df, x2d, ada_g, w_o_bf, g_ffn, wr_hi, wr_lo, br_pad, tri, cnt0, seq)
        counts = cnt[0, :N_EXPERTS].astype(jnp.int32)
        dest, block_e, next_e, n_active, cap = _routing(ti[:, :TOP_K], ti[:, TOP_K:2 * TOP_K], counts, n)
        xs = _sc_scatter(h2, dest, cap)
        staged[g] = (x1, tw, ada_g, dest, block_e, next_e, n_active, xs, seq, b)

    sorted_out = {}
    for g in order:
        x1, tw, ada_g, dest, block_e, next_e, n_active, xs, seq, b = staged[g]
        sorted_out[g] = _experts(block_e, next_e, n_active, xs, w_gate[l], bg, w_up[l], bu, w_down[l], bdn)

    outs = [None] * len(groups)
    for g in order:
        x1, tw, ada_g, dest, block_e, next_e, n_active, xs, seq, b = staged[g]
        ys = _sc_gather(sorted_out[g], dest)
        outs[g] = _combine(x1, ys, tw, ada_g, seq, 0).reshape(b, seq, D_MODEL)
    return tuple(outs)
```

```python
import functools
import math

import jax
import jax.numpy as jnp
from jax import lax
from jax.experimental import pallas as pl
from jax.experimental.pallas import tpu as pltpu
from jax.experimental.pallas import tpu_sc as plsc

F32 = jnp.float32
BF16 = jnp.bfloat16
U32 = jnp.uint32

D_MODEL = 1024
HEAD_DIM = 64
NA_HEADS = 8
NA_WIDTH = 512
DIFF_HEADS = 4
DIFF_WIDTH = 512
QKV_COLS = 3072
GRID_W = 64
NA_WIN_ROWS = 8
NA_WIN_COLS = 16
ROPE_THETA = 10000.0
N_EXPERTS = 32
TOP_K = 4
SWIGLU_LIMIT = 7.0
SWIGLU_ALPHA = 1.702
EPS = 1e-5
NEG_INF = -1e30
LAMBDA_INIT = 0.8 - 0.6 * math.exp(-0.3 * 0)
LOG2E = 1.4426950408889634

LANES = 128
MXU_DIM = 256
VMEM_LIMIT = 56 * 1024 * 1024

ROW_BLOCK = 512
Q_BLOCK = 1024
EXPERT_BLOCK = 512
EXPERT_SUB = 256
NA_ROWS_PER_TRIP = 8
VT_ROWS = LANES + 16
KV_CHUNK = 256
NORM_SLACK = 1.01
MAX_SAFE_BOUND = 60.0


PACKED = D_MODEL // 2
SC_CORES = 2
SC_SUBCORES = 16
SC_WORKERS = SC_CORES * SC_SUBCORES
GATHER_ROWS = 64


def _params(*sem):
    return pltpu.CompilerParams(dimension_semantics=sem, vmem_limit_bytes=VMEM_LIMIT)


def _pack_halves(x):
    w = x.shape[1] // 2
    bits = lax.bitcast_convert_type(x, U32)
    return (bits[:, :w] >> 16) | bits[:, w:]


def _col_reduce(x, op):
    while x.shape[0] >= 64:
        x = op(x.reshape(8, x.shape[0] // 8, x.shape[1]), axis=0)
    return op(x, axis=0, keepdims=True)


def _unpack_halves(word):
    lo = lax.bitcast_convert_type(word << 16, F32)
    hi = lax.bitcast_convert_type(word & jnp.uint32(0xFFFF0000), F32)
    return lo, hi


def _ada_kernel(c_ref, w_ref, b_ref, o_ref):
    c = c_ref[...]
    s = c * jax.nn.sigmoid(c)
    o_ref[...] = jnp.dot(s, w_ref[...], preferred_element_type=F32,
                         precision=lax.Precision.HIGHEST) + b_ref[...]


def _ada(c_all, w_ada, b_ada):
    nb = c_all.shape[0]
    n_out = w_ada.shape[1]
    blk = D_MODEL
    return pl.pallas_call(
        _ada_kernel,
        grid=(n_out // blk,),
        in_specs=[pl.BlockSpec((nb, D_MODEL), lambda j: (0, 0)),
                  pl.BlockSpec((D_MODEL, blk), lambda j: (0, j)),
                  pl.BlockSpec((1, blk), lambda j: (0, j))],
        out_specs=pl.BlockSpec((nb, blk), lambda j: (0, j)),
        out_shape=jax.ShapeDtypeStruct((nb, n_out), F32),
        compiler_params=_params("arbitrary"),
        name="ada",
    )(c_all, w_ada, b_ada.reshape(1, n_out))


def _head_sumsq(y, bd):
    sq = (y * y).astype(BF16)
    parts = [jnp.dot(sq[:, c:c + MXU_DIM], bd, preferred_element_type=F32)
             for c in range(0, y.shape[1], MXU_DIM)]
    return jnp.concatenate(parts, axis=1)


def _qkv_kernel(x_ref, ada_ref, g_ref, w_ref, gain_ref, cos_ref, sin_ref, bd_ref, o_ref):
    x = x_ref[...]
    ms = jnp.mean(x * x, axis=-1, keepdims=True)
    xn = x * lax.rsqrt(ms + EPS) * g_ref[...]
    sh = ada_ref[0, 0:1, :]
    sc = ada_ref[0, 1:2, :]
    h = (xn * (1.0 + sc) + sh).astype(BF16)
    bd = bd_ref[...]
    lane = lax.broadcasted_iota(jnp.int32, (x.shape[0], NA_WIDTH), 1)
    first_half = (lane & (HEAD_DIM // 2)) == 0
    for grp in range(6):
        cols = slice(grp * 512, (grp + 1) * 512)
        acc = jnp.dot(h, w_ref[:, cols], preferred_element_type=F32)
        if grp in (2, 5):
            o_ref[:, cols] = acc.astype(BF16)
            continue
        gi = {0: 0, 1: 1, 3: 2, 4: 3}[grp]
        ss = _head_sumsq(acc, bd)
        y = acc * lax.rsqrt(ss * (1.0 / HEAD_DIM) + EPS) * gain_ref[gi:gi + 1, :]
        if grp in (3, 4):
            partner = jnp.where(first_half,
                                pltpu.roll(y, NA_WIDTH - HEAD_DIM // 2, axis=1),
                                pltpu.roll(y, HEAD_DIM // 2, axis=1))
            y = y * cos_ref[...] + partner * sin_ref[...]
        o_ref[:, cols] = y.astype(BF16)


def _qkv(x2d, ada_g, g_attn, w_qkv_bf, gains, cos_t, sin_t, bd, seq):
    n = x2d.shape[0]
    tm = ROW_BLOCK
    per_seq = seq // tm
    return pl.pallas_call(
        _qkv_kernel,
        grid=(n // tm,),
        in_specs=[pl.BlockSpec((tm, D_MODEL), lambda i: (i, 0)),
                  pl.BlockSpec((1, 6, D_MODEL), lambda i: (i // per_seq, 0, 0)),
                  pl.BlockSpec((1, D_MODEL), lambda i: (0, 0)),
                  pl.BlockSpec((D_MODEL, QKV_COLS), lambda i: (0, 0)),
                  pl.BlockSpec((4, NA_WIDTH), lambda i: (0, 0)),
                  pl.BlockSpec((tm, DIFF_WIDTH), lambda i: (i % per_seq, 0)),
                  pl.BlockSpec((tm, DIFF_WIDTH), lambda i: (i % per_seq, 0)),
                  pl.BlockSpec((MXU_DIM, MXU_DIM), lambda i: (0, 0))],
        out_specs=pl.BlockSpec((tm, QKV_COLS), lambda i: (i, 0)),
        out_shape=jax.ShapeDtypeStruct((n, QKV_COLS), BF16),
        compiler_params=_params("arbitrary"),
        name="qkv",
    )(x2d, ada_g, g_attn, w_qkv_bf, gains, cos_t, sin_t, bd)


def _na_kernel(q_ref, k_ref, v_ref, bias_ref, o_ref, vaug_ref, *, rows):
    lane = lax.broadcasted_iota(jnp.int32, (GRID_W, LANES), 1)
    head0 = lane < HEAD_DIM
    win = NA_WIN_ROWS * GRID_W

    def window_start(r):
        return jnp.clip(r - NA_WIN_ROWS // 2, 0, rows - NA_WIN_ROWS)

    def scores(r):
        r_start = window_start(r)
        q = q_ref[pl.ds(pl.multiple_of(r * GRID_W, GRID_W), GRID_W), :]
        kw = k_ref[pl.ds(pl.multiple_of(r_start * GRID_W, GRID_W), win), :]
        zero = jnp.zeros_like(q)
        qm = jnp.concatenate([jnp.where(head0, q, zero), jnp.where(head0, zero, q)], axis=0)
        s = lax.dot_general(kw, qm, (((1,), (1,)), ((), ())), preferred_element_type=F32)
        return s + bias_ref[0, r - r_start]

    vaug_ref[:, :LANES] = v_ref[...]
    vaug_ref[:, LANES:] = jnp.ones((v_ref.shape[0], LANES), BF16)

    def finish(r, s):
        vw = vaug_ref[pl.ds(pl.multiple_of(window_start(r) * GRID_W, GRID_W), win), :]
        m = _col_reduce(s, jnp.max)
        p = jnp.exp2(s - m).astype(BF16)
        o2 = lax.dot_general(p, vw, (((0,), (0,)), ((), ())), preferred_element_type=F32)
        o2 = o2[:, :LANES] * (1.0 / o2[:, LANES:])
        o = jnp.where(head0, o2[:GRID_W], o2[GRID_W:])
        o_ref[pl.ds(pl.multiple_of(r * GRID_W, GRID_W), GRID_W), :] = o.astype(BF16)

    def body(i, carry):
        trip_rows = [i * NA_ROWS_PER_TRIP + u for u in range(NA_ROWS_PER_TRIP)]
        trip_scores = [scores(r) for r in trip_rows]
        for r, s in zip(trip_rows, trip_scores):
            finish(r, s)
        return carry

    lax.fori_loop(0, rows // NA_ROWS_PER_TRIP, body, 0)


def _na(qkv, bias_t, batch, seq):
    rows = seq // GRID_W
    n_pairs = NA_HEADS // 2
    return pl.pallas_call(
        functools.partial(_na_kernel, rows=rows),
        grid=(batch, n_pairs),
        in_specs=[pl.BlockSpec((seq, LANES), lambda b, hp: (b, hp)),
                  pl.BlockSpec((seq, LANES), lambda b, hp: (b, n_pairs + hp)),
                  pl.BlockSpec((seq, LANES), lambda b, hp: (b, 2 * n_pairs + hp)),
                  pl.BlockSpec((1, NA_WIN_ROWS, NA_WIN_ROWS * GRID_W, LANES), lambda b, hp: (hp, 0, 0, 0))],
        out_specs=pl.BlockSpec((seq, LANES), lambda b, hp: (b, hp)),
        out_shape=jax.ShapeDtypeStruct((batch * seq, NA_WIDTH), BF16),
        scratch_shapes=[pltpu.VMEM((seq, 2 * LANES), BF16)],
        compiler_params=_params("arbitrary", "arbitrary"),
        name="na_attn",
    )(qkv, qkv, qkv, bias_t)


def _diff_kernel(q_ref, k_ref, v_ref, lam_ref, g_ref, o_ref, vt_ref, kn_ref, oa_ref):
    @pl.when(pl.program_id(2) == 0)
    def _():
        vt_ref[:LANES, :] = v_ref[...].astype(F32).T.astype(BF16)
        ones_row = lax.broadcasted_iota(jnp.int32, (VT_ROWS - LANES, v_ref.shape[0]), 0) == 0
        vt_ref[LANES:, :] = jnp.where(ones_row, 1.0, 0.0).astype(BF16)
        kf = k_ref[...].astype(F32)
        d_id = lax.broadcasted_iota(jnp.int32, (LANES, LANES), 0) // HEAD_DIM
        c_id = lax.broadcasted_iota(jnp.int32, (LANES, LANES), 1)
        comp_sel = jnp.where(d_id == c_id, 1.0, 0.0).astype(BF16)
        kn2 = jnp.dot((kf * kf).astype(BF16), comp_sel, preferred_element_type=F32)
        kn_ref[...] = jnp.sqrt(_col_reduce(kn2, jnp.max)) * NORM_SLACK

    lq1 = lam_ref[0:1, :]
    lk1 = lam_ref[1:2, :]
    lq2 = lam_ref[2:3, :]
    lk2 = lam_ref[3:4, :]
    lam = (jnp.exp(jnp.sum(lq1 * lk1, axis=-1, keepdims=True))
           - jnp.exp(jnp.sum(lq2 * lk2, axis=-1, keepdims=True)) + LAMBDA_INIT)

    q = q_ref[...]
    lane = lax.broadcasted_iota(jnp.int32, q.shape, 1)
    zero = jnp.zeros_like(q)
    tq = q.shape[0]
    qcat = jnp.concatenate([jnp.where(lane < HEAD_DIM, q, zero), jnp.where(lane < HEAD_DIM, zero, q)], axis=0)
    def scores(c, chunk):
        kc = k_ref[c * chunk:(c + 1) * chunk, :]
        return lax.dot_general(kc, qcat, (((1,), (1,)), ((), ())), preferred_element_type=F32)

    def chunked(chunk, step):
        n_chunks = k_ref.shape[0] // chunk
        s_next = scores(0, chunk)
        state = None
        for c in range(n_chunks):
            s = s_next
            if c + 1 < n_chunks:
                s_next = scores(c + 1, chunk)
            state = step(c, s, vt_ref[:, c * chunk:(c + 1) * chunk], state)
        return state

    qf = qcat.astype(F32)
    ones8 = jnp.ones((8, LANES), BF16)
    qn2 = lax.dot_general(ones8, (qf * qf).astype(BF16), (((1,), (1,)), ((), ())), preferred_element_type=F32)
    col = lax.broadcasted_iota(jnp.int32, (1, 2 * tq), 1)
    bound = jnp.sqrt(qn2[0:1, :]) * NORM_SLACK * jnp.where(col < tq, kn_ref[0:1, 0:1], kn_ref[0:1, 1:2])
    in_range = jnp.max(bound) <= MAX_SAFE_BOUND

    @pl.when(in_range)
    def _():
        def step(c, s, vtc, acc):
            pv = jnp.dot(vtc, jnp.exp2(s - bound).astype(BF16), preferred_element_type=F32)
            return pv if c == 0 else acc + pv
        oa_ref[...] = chunked(KV_CHUNK, step)

    @pl.when(jnp.logical_not(in_range))
    def _():
        def step(c, s, vtc, state):
            mc = _col_reduce(s, jnp.max)
            m_new = mc if c == 0 else jnp.maximum(state[0], mc)
            pv = jnp.dot(vtc, jnp.exp2(s - m_new).astype(BF16), preferred_element_type=F32)
            return (m_new, pv if c == 0 else jnp.exp2(state[0] - m_new) * state[1] + pv)
        oa_ref[...] = chunked(KV_CHUNK, step)[1]

    oa = oa_ref[...]
    o0, l0 = oa[:LANES, :tq], oa[LANES:LANES + 1, :tq]
    o1, l1 = oa[:LANES, tq:], oa[LANES:LANES + 1, tq:]
    o = o0 * (1.0 / l0) - (lam / l1) * o1
    ms = jnp.mean(o * o, axis=0, keepdims=True)
    y = o * lax.rsqrt(ms + EPS) * g_ref[...]
    o_ref[...] = y.T.astype(BF16)


def _diff(qkv, lam_vecs, subln_col, batch, seq):
    tq = Q_BLOCK
    nq = seq // tq
    base = 3 * NA_WIDTH // LANES
    nh = DIFF_HEADS
    return pl.pallas_call(
        _diff_kernel,
        grid=(batch, nh, nq),
        in_specs=[pl.BlockSpec((tq, LANES), lambda b, h, i: (b * nq + i, base + h)),
                  pl.BlockSpec((seq, LANES), lambda b, h, i: (b, base + nh + h)),
                  pl.BlockSpec((seq, LANES), lambda b, h, i: (b, base + 2 * nh + h)),
                  pl.BlockSpec((4, HEAD_DIM), lambda b, h, i: (0, 0)),
                  pl.BlockSpec((LANES, 1), lambda b, h, i: (0, 0))],
        out_specs=pl.BlockSpec((tq, LANES), lambda b, h, i: (b * nq + i, h)),
        out_shape=jax.ShapeDtypeStruct((batch * seq, DIFF_WIDTH), BF16),
        scratch_shapes=[pltpu.VMEM((VT_ROWS, seq), BF16),
                        pltpu.VMEM((1, LANES), F32),
                        pltpu.VMEM((VT_ROWS, 2 * tq), F32)],
        compiler_params=_params("arbitrary", "arbitrary", "arbitrary"),
        name="diff_attn",
    )(qkv, qkv, qkv, lam_vecs, subln_col)


def _wo_kernel(ona_ref, odf_ref, x_ref, ada_ref, wo_ref, g_ref, wrh_ref, wrl_ref, br_ref, tri_ref, cnt0_ref,
               x1_ref, h2_ref, ti_ref, tw_ref, cnt_ref):
    mix = (jnp.dot(ona_ref[...], wo_ref[:NA_WIDTH, :], preferred_element_type=F32)
           + jnp.dot(odf_ref[...], wo_ref[NA_WIDTH:, :], preferred_element_type=F32))
    gt1 = ada_ref[0, 2:3, :]
    sh2 = ada_ref[0, 3:4, :]
    sc2 = ada_ref[0, 4:5, :]
    x1 = x_ref[...] + gt1 * mix
    x1_ref[...] = x1
    ms = jnp.mean(x1 * x1, axis=-1, keepdims=True)
    h2 = x1 * lax.rsqrt(ms + EPS) * g_ref[...] * (1.0 + sc2) + sh2
    hi = h2.astype(BF16)
    h2_ref[...] = _pack_halves(hi.astype(F32))
    lo = (h2 - hi.astype(F32)).astype(BF16)
    both = jnp.dot(hi, wrl_ref[...], preferred_element_type=F32)
    logits = (both + pltpu.roll(both, LANES - N_EXPERTS, axis=1)
              + jnp.dot(lo, wrh_ref[...], preferred_element_type=F32)) + br_ref[...]
    lane = lax.broadcasted_iota(jnp.int32, logits.shape, 1).astype(F32)
    vals = []
    idxs = []
    cur = logits
    for _ in range(TOP_K):
        m = jnp.max(cur, axis=-1, keepdims=True)
        idx = jnp.min(jnp.where(cur == m, lane, float(LANES)), axis=-1, keepdims=True)
        vals.append(m)
        idxs.append(idx)
        cur = jnp.where(lane == idx, -jnp.inf, cur)
    es = [jnp.exp(v - vals[0]) for v in vals]
    inv = 1.0 / (es[0] + es[1] + es[2] + es[3])

    @pl.when(pl.program_id(0) == 0)
    def _():
        cnt_ref[...] = cnt0_ref[...]

    sel = jnp.zeros(logits.shape, F32)
    for j in range(TOP_K):
        sel = sel + jnp.where(lane == idxs[j], 1.0, 0.0)
    before = jnp.dot(tri_ref[...], sel.astype(BF16), preferred_element_type=F32) + cnt_ref[...]
    cnt_ref[...] = cnt_ref[...] + jnp.sum(sel, axis=0, keepdims=True)
    ranks = [jnp.sum(jnp.where(lane == idxs[j], before, 0.0), axis=-1, keepdims=True) for j in range(TOP_K)]

    ti = jnp.zeros(logits.shape, F32)
    tw = jnp.zeros(logits.shape, F32)
    for j in range(TOP_K):
        ti = jnp.where(lane == float(j), idxs[j], ti)
        ti = jnp.where(lane == float(TOP_K + j), ranks[j], ti)
        tw = jnp.where(lane == float(j), es[j] * inv, tw)
    ti_ref[...] = ti.astype(jnp.int32)
    tw_ref[...] = tw


def _wo(o_na, o_df, x2d, ada_g, w_o_bf, g_ffn, wr_hi, wr_lo, br_pad, tri, cnt0, seq):
    n = x2d.shape[0]
    tm = ROW_BLOCK
    per_seq = seq // tm
    row = lambda i: (i, 0)
    const = lambda i: (0, 0)
    return pl.pallas_call(
        _wo_kernel,
        grid=(n // tm,),
        in_specs=[pl.BlockSpec((tm, NA_WIDTH), row),
                  pl.BlockSpec((tm, DIFF_WIDTH), row),
                  pl.BlockSpec((tm, D_MODEL), row),
                  pl.BlockSpec((1, 6, D_MODEL), lambda i: (i // per_seq, 0, 0)),
                  pl.BlockSpec((D_MODEL, D_MODEL), const),
                  pl.BlockSpec((1, D_MODEL), const),
                  pl.BlockSpec((D_MODEL, LANES), const),
                  pl.BlockSpec((D_MODEL, LANES), const),
                  pl.BlockSpec((1, LANES), const),
                  pl.BlockSpec((tm, tm), const),
                  pl.BlockSpec((1, LANES), const)],
        out_specs=[pl.BlockSpec((tm, D_MODEL), row),
                   pl.BlockSpec((tm, PACKED), row),
                   pl.BlockSpec((tm, LANES), row),
                   pl.BlockSpec((tm, LANES), row),
                   pl.BlockSpec((1, LANES), const)],
        out_shape=[jax.ShapeDtypeStruct((n, D_MODEL), F32),
                   jax.ShapeDtypeStruct((n, PACKED), U32),
                   jax.ShapeDtypeStruct((n, LANES), jnp.int32),
                   jax.ShapeDtypeStruct((n, LANES), F32),
                   jax.ShapeDtypeStruct((1, LANES), F32)],
        compiler_params=_params("arbitrary"),
        name="wo_router",
    )(o_na, o_df, x2d, ada_g, w_o_bf, g_ffn, wr_hi, wr_lo, br_pad, tri, cnt0)


def _expert_kernel(be_ref, nxt_ref, na_ref, xs_ref, wg_hbm, bg_ref, wu_hbm, bu_ref, wd_hbm, bd_ref, o_ref,
                   w_f32, w_bf, sems, slot_ref):
    i = pl.program_id(0)
    active = i < na_ref[0]
    expert = be_ref[i]
    new_expert = jnp.logical_or(i == 0, expert != be_ref[jnp.maximum(i - 1, 0)])
    wg_bf, wu_bf, wd_bf = w_bf.at[0], w_bf.at[1], w_bf.at[2]

    def weight_copies(src_expert, slot):
        return [pltpu.make_async_copy(w_hbm.at[src_expert], w_f32.at[slot, j], sems.at[slot, j])
                for j, w_hbm in enumerate((wg_hbm, wu_hbm, wd_hbm))]

    @pl.when(i == 0)
    def _():
        slot_ref[0] = 0
        for cp in weight_copies(expert, 0):
            cp.start()

    @pl.when(jnp.logical_and(active, new_expert))
    def _():
        slot = slot_ref[0]
        nxt = nxt_ref[i]
        for s in range(2):
            @pl.when(slot == s)
            def _():
                @pl.when(nxt >= 0)
                def _():
                    for cp in weight_copies(nxt, 1 - s):
                        cp.start(priority=1)
                for j, cp in enumerate(weight_copies(expert, s)):
                    cp.wait()
                    w_bf[j] = w_f32[s, j].astype(BF16)
        slot_ref[0] = 1 - slot

    @pl.when(active)
    def _():
        def gate_up(rows):
            x_lo, x_hi = _unpack_halves(xs_ref[rows, :])
            x_lo = x_lo.astype(BF16)
            x_hi = x_hi.astype(BF16)

            def proj(w_bf):
                return (jnp.dot(x_lo, w_bf[:PACKED, :], preferred_element_type=F32)
                        + jnp.dot(x_hi, w_bf[PACKED:, :], preferred_element_type=F32))

            return proj(wg_bf), proj(wu_bf)

        def act_down(rows, gu):
            g = jnp.minimum(gu[0] + bg_ref[0], SWIGLU_LIMIT)
            u = jnp.clip(gu[1] + bu_ref[0], -SWIGLU_LIMIT, SWIGLU_LIMIT)
            act = g * jax.nn.sigmoid(SWIGLU_ALPHA * g) * (u + 1.0)
            out = jnp.dot(act.astype(BF16), wd_bf[...], preferred_element_type=F32) + bd_ref[0]
            o_ref[rows, :] = _pack_halves(out.astype(BF16).astype(F32))

        sub = [pl.ds(r, EXPERT_SUB) for r in range(0, EXPERT_BLOCK, EXPERT_SUB)]
        gu_next = gate_up(sub[0])
        for j, rows in enumerate(sub):
            gu = gu_next
            if j + 1 < len(sub):
                gu_next = gate_up(sub[j + 1])
            act_down(rows, gu)

    @pl.when(i >= na_ref[0])
    def _():
        o_ref[...] = jnp.zeros_like(o_ref)


def _experts(block_e, next_e, n_active, xs, wg, bg, wu, bu, wd, bd):
    cap = xs.shape[0]
    n_blocks = cap // EXPERT_BLOCK
    xmap = lambda i, be, nx, na: (jnp.minimum(i, na[0] - 1), 0)
    bmap = lambda i, be, nx, na: (be[i], 0, 0)
    hbm = pl.BlockSpec(memory_space=pl.ANY)
    grid_spec = pltpu.PrefetchScalarGridSpec(
        num_scalar_prefetch=3,
        grid=(n_blocks,),
        in_specs=[pl.BlockSpec((EXPERT_BLOCK, PACKED), xmap),
                  hbm, pl.BlockSpec((1, 1, D_MODEL), bmap),
                  hbm, pl.BlockSpec((1, 1, D_MODEL), bmap),
                  hbm, pl.BlockSpec((1, 1, D_MODEL), bmap)],
        out_specs=pl.BlockSpec((EXPERT_BLOCK, PACKED), lambda i, be, nx, na: (i, 0)),
        scratch_shapes=[pltpu.VMEM((2, 3, D_MODEL, D_MODEL), F32),
                        pltpu.VMEM((3, D_MODEL, D_MODEL), BF16),
                        pltpu.SemaphoreType.DMA((2, 3)),
                        pltpu.SMEM((1,), jnp.int32)],
    )
    return pl.pallas_call(
        _expert_kernel,
        grid_spec=grid_spec,
        out_shape=jax.ShapeDtypeStruct((cap, PACKED), U32),
        compiler_params=_params("arbitrary"),
        name="experts",
    )(block_e, next_e, n_active, xs, wg, bg, wu, bu, wd, bd)


def _sc_gather(table, idx):
    n_out = idx.shape[0]
    width = table.shape[1]
    per_worker = n_out // SC_WORKERS
    n_chunks = per_worker // GATHER_ROWS
    assert per_worker * SC_WORKERS == n_out and n_chunks * GATHER_ROWS == per_worker and n_chunks % 2 == 0
    idx3 = idx.reshape(SC_WORKERS, n_chunks, GATHER_ROWS)
    mesh = plsc.VectorSubcoreMesh(core_axis_name="core", subcore_axis_name="subcore")

    @functools.partial(
        pl.kernel, mesh=mesh,
        out_type=jax.ShapeDtypeStruct((n_out, width), table.dtype),
        scratch_types=[pltpu.VMEM((n_chunks, GATHER_ROWS), jnp.int32),
                       pltpu.VMEM((2, GATHER_ROWS, width), table.dtype),
                       pltpu.SemaphoreType.DMA((2,)),
                       pltpu.SemaphoreType.DMA((2,))])
    def gather_kernel(table_hbm, idx_hbm, out_hbm, idx_v, rows_v, gsem, wsem):
        wid = lax.axis_index("subcore") * SC_CORES + lax.axis_index("core")
        base = wid * per_worker
        pltpu.sync_copy(idx_hbm.at[wid], idx_v)

        def gather(j, slot):
            return pltpu.make_async_copy(table_hbm.at[idx_v.at[j]], rows_v.at[slot], gsem.at[slot])

        def write(j, slot):
            dst = out_hbm.at[pl.ds(pl.multiple_of(base + j * GATHER_ROWS, GATHER_ROWS), GATHER_ROWS)]
            return pltpu.make_async_copy(rows_v.at[slot], dst, wsem.at[slot])

        gather(0, 0).start()

        @pl.loop(0, n_chunks, step=2)
        def _(j):
            for slot in range(2):
                jj = j + slot
                gather(jj, slot).wait()

                @pl.when(jj >= 1)
                def _():
                    write(jj - 1, 1 - slot).wait()

                @pl.when(jj + 1 < n_chunks)
                def _():
                    gather(jj + 1, 1 - slot).start()

                write(jj, slot).start()

        write(n_chunks - 1, 1).wait()

    return gather_kernel(table, idx3)


def _sc_scatter(rows, idx, n_out):
    n_src, width = rows.shape
    n_idx = idx.shape[0]
    per_worker = n_idx // SC_WORKERS
    n_chunks = per_worker // GATHER_ROWS
    assert per_worker * SC_WORKERS == n_idx and n_chunks * GATHER_ROWS == per_worker and n_chunks % 2 == 0
    assert n_src % per_worker == 0
    idx3 = idx.reshape(SC_WORKERS, n_chunks, GATHER_ROWS)
    mesh = plsc.VectorSubcoreMesh(core_axis_name="core", subcore_axis_name="subcore")

    @functools.partial(
        pl.kernel, mesh=mesh,
        out_type=jax.ShapeDtypeStruct((n_out, width), rows.dtype),
        scratch_types=[pltpu.VMEM((n_chunks, GATHER_ROWS), jnp.int32),
                       pltpu.VMEM((2, GATHER_ROWS, width), rows.dtype),
                       pltpu.SemaphoreType.DMA((2,)),
                       pltpu.SemaphoreType.DMA((2,))])
    def scatter_kernel(rows_hbm, idx_hbm, out_hbm, idx_v, rows_v, rsem, wsem):
        wid = lax.axis_index("subcore") * SC_CORES + lax.axis_index("core")
        base = lax.rem(wid * per_worker, n_src)
        pltpu.sync_copy(idx_hbm.at[wid], idx_v)

        def read(j, slot):
            src = rows_hbm.at[pl.ds(pl.multiple_of(base + j * GATHER_ROWS, GATHER_ROWS), GATHER_ROWS)]
            return pltpu.make_async_copy(src, rows_v.at[slot], rsem.at[slot])

        def write(j, slot):
            return pltpu.make_async_copy(rows_v.at[slot], out_hbm.at[idx_v.at[j]], wsem.at[slot])

        read(0, 0).start()

        @pl.loop(0, n_chunks, step=2)
        def _(j):
            for slot in range(2):
                jj = j + slot
                read(jj, slot).wait()

                @pl.when(jj >= 1)
                def _():
                    write(jj - 1, 1 - slot).wait()

                @pl.when(jj + 1 < n_chunks)
                def _():
                    read(jj + 1, 1 - slot).start()

                write(jj, slot).start()

        write(n_chunks - 1, 1).wait()

    return scatter_kernel(rows, idx3)


def _combine_kernel(x1_ref, y0_ref, y1_ref, y2_ref, y3_ref, tw_ref, ada_ref, o_ref):
    tw = tw_ref[...]
    acc_lo = jnp.zeros((x1_ref.shape[0], PACKED), F32)
    acc_hi = jnp.zeros((x1_ref.shape[0], PACKED), F32)
    for j, y_ref in enumerate((y0_ref, y1_ref, y2_ref, y3_ref)):
        lo, hi = _unpack_halves(y_ref[...])
        acc_lo = acc_lo + tw[:, j:j + 1] * lo
        acc_hi = acc_hi + tw[:, j:j + 1] * hi
    o_ref[:, :PACKED] = x1_ref[:, :PACKED] + ada_ref[0, 5:6, :PACKED] * acc_lo
    o_ref[:, PACKED:] = x1_ref[:, PACKED:] + ada_ref[0, 5:6, PACKED:] * acc_hi


def _combine(x1, ys, tw, ada_g, seq, row_off):
    n = x1.shape[0]
    tm = ROW_BLOCK
    per_seq = seq // tm
    off = row_off // tm
    per_choice = tw.shape[0] // tm
    y_specs = [pl.BlockSpec((tm, PACKED), functools.partial(lambda i, j: (j * per_choice + off + i, 0), j=j))
               for j in range(TOP_K)]
    return pl.pallas_call(
        _combine_kernel,
        grid=(n // tm,),
        in_specs=[pl.BlockSpec((tm, D_MODEL), lambda i: (i, 0)),
                  *y_specs,
                  pl.BlockSpec((tm, LANES), lambda i: (i + off, 0)),
                  pl.BlockSpec((1, 6, D_MODEL), lambda i: (i // per_seq, 0, 0))],
        out_specs=pl.BlockSpec((tm, D_MODEL), lambda i: (i, 0)),
        out_shape=jax.ShapeDtypeStruct((n, D_MODEL), F32),
        compiler_params=_params("arbitrary"),
        name="combine",
    )(x1, ys, ys, ys, ys, tw, ada_g)


def _rope_tables(seq):
    half = HEAD_DIM // 2
    inv = ROPE_THETA ** (-jnp.arange(half, dtype=F32) / half)
    ang = jnp.arange(seq, dtype=F32)[:, None] * inv[None, :]
    cos, sin = jnp.cos(ang), jnp.sin(ang)
    cos_h = jnp.concatenate([cos, cos], axis=-1)
    sin_h = jnp.concatenate([-sin, sin], axis=-1)
    reps = DIFF_WIDTH // HEAD_DIM
    return jnp.tile(cos_h, (1, reps)), jnp.tile(sin_h, (1, reps))


def _na_bias_table(rpb):
    cols = jnp.arange(GRID_W, dtype=jnp.int32)
    c_start = jnp.clip(cols - NA_WIN_COLS // 2, 0, GRID_W - NA_WIN_COLS)
    col_mask = (cols[None, :] >= c_start[:, None]) & (cols[None, :] < c_start[:, None] + NA_WIN_COLS)
    col_idx = jnp.clip(cols[None, :] - cols[:, None], -(NA_WIN_COLS - 1), NA_WIN_COLS - 1) + NA_WIN_COLS - 1
    delta = jnp.arange(NA_WIN_ROWS, dtype=jnp.int32)
    j = jnp.arange(NA_WIN_ROWS, dtype=jnp.int32)
    row_idx = j[None, :] - delta[:, None] + NA_WIN_ROWS - 1
    row_hot = (row_idx[:, :, None] == jnp.arange(2 * NA_WIN_ROWS - 1, dtype=jnp.int32)).astype(F32)
    col_hot = (col_idx[:, :, None] == jnp.arange(2 * NA_WIN_COLS - 1, dtype=jnp.int32)).astype(F32)
    bias = jnp.einsum('djr,hrc,qkc->hdjqk', row_hot, rpb.astype(F32), col_hot,
                      precision=lax.Precision.HIGHEST)
    bias = jnp.where(col_mask[None, None, None], bias * LOG2E, NEG_INF)
    bias = bias.transpose(0, 1, 2, 4, 3).reshape(NA_HEADS, NA_WIN_ROWS, NA_WIN_ROWS * GRID_W, GRID_W)
    bias = bias.reshape(NA_HEADS // 2, 2, NA_WIN_ROWS, NA_WIN_ROWS * GRID_W, GRID_W)
    return jnp.concatenate([bias[:, 0], bias[:, 1]], axis=-1)


def _routing(top_idx, rank, counts, n):
    n_blocks = n * TOP_K // EXPERT_BLOCK + N_EXPERTS
    experts = jnp.arange(N_EXPERTS, dtype=jnp.int32)
    padded = (counts + EXPERT_BLOCK - 1) // EXPERT_BLOCK * EXPERT_BLOCK
    pad_end = jnp.cumsum(padded)
    pad_start = pad_end - padded
    start_of = jnp.sum(jnp.where(top_idx[:, :, None] == experts, pad_start, 0), axis=-1)
    dest = (start_of + rank).T.reshape(-1)
    block_lo = jnp.arange(n_blocks, dtype=jnp.int32) * EXPERT_BLOCK
    block_e = jnp.minimum(jnp.sum((pad_end[None, :] <= block_lo[:, None]).astype(jnp.int32), axis=1),
                          N_EXPERTS - 1).astype(jnp.int32)
    n_active = (pad_end[-1] // EXPERT_BLOCK).astype(jnp.int32).reshape(1)
    later = jnp.where((experts[None, :] > experts[:, None]) & (padded[None, :] > 0), experts[None, :], N_EXPERTS)
    next_nonempty = jnp.min(later, axis=1)
    next_nonempty = jnp.where(next_nonempty == N_EXPERTS, -1, next_nonempty)
    next_e = jnp.sum(jnp.where(block_e[:, None] == experts[None, :], next_nonempty[None, :], 0), axis=1)
    return dest, block_e, next_e.astype(jnp.int32), n_active, n_blocks * EXPERT_BLOCK


def kernel(x_prompt, x_sample, c_prompt, c_sample, w_ada, b_ada, g_attn_norm, w_qkv, na_q_norm, na_k_norm, na_rpb, diff_q_norm, diff_k_norm, lambda_q1, lambda_k1, lambda_q2, lambda_k2, diff_subln, w_o, g_ffn_norm, w_router, b_router, w_gate, b_gate, w_up, b_up, w_down, b_down):
    l = 0
    groups = [(x_prompt, c_prompt), (x_sample, c_sample)]
    nb = [x.shape[0] for x, _ in groups]

    ada_all = _ada(jnp.concatenate([c for _, c in groups], axis=0), w_ada[l], b_ada[l])
    ada_all = ada_all.reshape(sum(nb), 6, D_MODEL)

    w_qkv_bf = w_qkv[l].astype(BF16)
    w_o_bf = w_o[l].astype(BF16)
    scale = HEAD_DIM ** -0.5
    reps = NA_WIDTH // HEAD_DIM
    gains = jnp.stack([jnp.tile(na_q_norm[l], reps) * (scale * LOG2E),
                       jnp.tile(na_k_norm[l], reps),
                       jnp.tile(diff_q_norm[l], reps) * (scale * LOG2E),
                       jnp.tile(diff_k_norm[l], reps)]).astype(F32)
    head_id = jnp.arange(MXU_DIM, dtype=jnp.int32) // HEAD_DIM
    bd = (head_id[:, None] == head_id[None, :]).astype(BF16)
    bias_t = _na_bias_table(na_rpb[l])
    lam_vecs = jnp.stack([lambda_q1[l], lambda_k1[l], lambda_q2[l], lambda_k2[l]]).astype(F32)
    subln_col = (diff_subln[l].astype(F32) * (1.0 - LAMBDA_INIT)).reshape(LANES, 1)
    wr = w_router[l].astype(F32)
    wr_pad = jnp.zeros((D_MODEL, LANES), F32).at[:, :N_EXPERTS].set(wr)
    wr_hi = wr_pad.astype(BF16)
    wr_lo = (wr_pad - wr_hi.astype(F32)).astype(BF16)
    wr_lo = wr_hi.at[:, N_EXPERTS:2 * N_EXPERTS].set(wr_lo[:, :N_EXPERTS])
    br_pad = jnp.full((1, LANES), NEG_INF, F32).at[0, :N_EXPERTS].set(b_router[l].astype(F32))
    g_attn = g_attn_norm[l].reshape(1, D_MODEL).astype(F32)
    g_ffn = g_ffn_norm[l].reshape(1, D_MODEL).astype(F32)
    max_seq = max(x.shape[1] for x, _ in groups)
    cos_t, sin_t = _rope_tables(max_seq)

    rows = lax.broadcasted_iota(jnp.int32, (ROW_BLOCK, ROW_BLOCK), 0)
    cols = lax.broadcasted_iota(jnp.int32, (ROW_BLOCK, ROW_BLOCK), 1)
    tri = (cols < rows).astype(BF16)
    cnt0 = jnp.zeros((1, LANES), F32)

    bg = b_gate[l].reshape(N_EXPERTS, 1, D_MODEL).astype(F32)
    bu = b_up[l].reshape(N_EXPERTS, 1, D_MODEL).astype(F32)
    bdn = b_down[l].reshape(N_EXPERTS, 1, D_MODEL).astype(F32)
    ada_groups = [ada_all[:nb[0]], ada_all[nb[0]:]]

    order = sorted(range(len(groups)), key=lambda g: -groups[g][0].shape[1])
    staged = {}
    for g in order:
        x = groups[g][0]
        b, seq = x.shape[0], x.shape[1]
        n = b * seq
        x2d = x.reshape(n, D_MODEL)
        ada_g = ada_groups[g]
        qkv = _qkv(x2d, ada_g, g_attn, w_qkv_bf, gains, cos_t, sin_t, bd, seq)
        o_na = _na(qkv, bias_t, b, seq)
        o_df = _diff(qkv, lam_vecs, subln_col, b, seq)
        x1, h2, ti, tw, cnt = _wo(o_na, o_df, x2d, ada_g, w_o_bf, g_ffn, wr_hi, wr_lo, br_pad, tri, cnt0, seq)
        counts = cnt[0, :N_EXPERTS].astype(jnp.int32)
        dest, block_e, next_e, n_active, cap = _routing(ti[:, :TOP_K], ti[:, TOP_K:2 * TOP_K], counts, n)
        xs = _sc_scatter(h2, dest, cap)
        staged[g] = (x1, tw, ada_g, dest, block_e, next_e, n_active, xs, seq, b)

    sorted_out = {}
    for g in order:
        x1, tw, ada_g, dest, block_e, next_e, n_active, xs, seq, b = staged[g]
        sorted_out[g] = _experts(block_e, next_e, n_active, xs, w_gate[l], bg, w_up[l], bu, w_down[l], bdn)

    outs = [None] * len(groups)
    for g in order:
        x1, tw, ada_g, dest, block_e, next_e, n_active, xs, seq, b = staged[g]
        ys = _sc_gather(sorted_out[g], dest)
        outs[g] = _combine(x1, ys, tw, ada_g, seq, 0).reshape(b, seq, D_MODEL)
    return tuple(outs)
```

```python
import functools
import math

import jax
import jax.numpy as jnp
from jax import lax
from jax.experimental import pallas as pl
from jax.experimental.pallas import tpu as pltpu
from jax.experimental.pallas import tpu_sc as plsc

F32 = jnp.float32
BF16 = jnp.bfloat16
U32 = jnp.uint32

D_MODEL = 1024
HEAD_DIM = 64
NA_HEADS = 8
NA_WIDTH = 512
DIFF_HEADS = 4
DIFF_WIDTH = 512
QKV_COLS = 3072
GRID_W = 64
NA_WIN_ROWS = 8
NA_WIN_COLS = 16
ROPE_THETA = 10000.0
N_EXPERTS = 32
TOP_K = 4
SWIGLU_LIMIT = 7.0
SWIGLU_ALPHA = 1.702
EPS = 1e-5
NEG_INF = -1e30
LAMBDA_INIT = 0.8 - 0.6 * math.exp(-0.3 * 0)
LOG2E = 1.4426950408889634

LANES = 128
MXU_DIM = 256
VMEM_LIMIT = 56 * 1024 * 1024

ROW_BLOCK = 512
Q_BLOCK = 1024
EXPERT_BLOCK = 512
EXPERT_SUB = 256
WO_BLOCK = 1024
WO_SUB = 256
NA_ROWS_PER_TRIP = 8
VT_ROWS = LANES + 16
KV_CHUNK = 256
NORM_SLACK = 1.01
MAX_SAFE_BOUND = 60.0


PACKED = D_MODEL // 2
SC_CORES = 2
SC_SUBCORES = 16
SC_WORKERS = SC_CORES * SC_SUBCORES
GATHER_ROWS = 64


def _params(*sem):
    return pltpu.CompilerParams(dimension_semantics=sem, vmem_limit_bytes=VMEM_LIMIT)


def _pack_halves(x):
    w = x.shape[1] // 2
    bits = lax.bitcast_convert_type(x, U32)
    return (bits[:, :w] >> 16) | bits[:, w:]


def _col_reduce(x, op):
    while x.shape[0] >= 64:
        x = op(x.reshape(8, x.shape[0] // 8, x.shape[1]), axis=0)
    return op(x, axis=0, keepdims=True)


def _unpack_halves(word):
    lo = lax.bitcast_convert_type(word << 16, F32)
    hi = lax.bitcast_convert_type(word & jnp.uint32(0xFFFF0000), F32)
    return lo, hi


def _ada_kernel(c_ref, w_ref, b_ref, o_ref):
    c = c_ref[...]
    s = c * jax.nn.sigmoid(c)
    o_ref[...] = jnp.dot(s, w_ref[...], preferred_element_type=F32,
                         precision=lax.Precision.HIGHEST) + b_ref[...]


def _ada(c_all, w_ada, b_ada):
    nb = c_all.shape[0]
    n_out = w_ada.shape[1]
    blk = D_MODEL
    return pl.pallas_call(
        _ada_kernel,
        grid=(n_out // blk,),
        in_specs=[pl.BlockSpec((nb, D_MODEL), lambda j: (0, 0)),
                  pl.BlockSpec((D_MODEL, blk), lambda j: (0, j)),
                  pl.BlockSpec((1, blk), lambda j: (0, j))],
        out_specs=pl.BlockSpec((nb, blk), lambda j: (0, j)),
        out_shape=jax.ShapeDtypeStruct((nb, n_out), F32),
        compiler_params=_params("arbitrary"),
        name="ada",
    )(c_all, w_ada, b_ada.reshape(1, n_out))


def _head_sumsq(y, bd):
    sq = (y * y).astype(BF16)
    parts = [jnp.dot(sq[:, c:c + MXU_DIM], bd, preferred_element_type=F32)
             for c in range(0, y.shape[1], MXU_DIM)]
    return jnp.concatenate(parts, axis=1)


def _qkv_kernel(x_ref, ada_ref, g_ref, w_ref, gain_ref, cos_ref, sin_ref, bd_ref, o_ref):
    x = x_ref[...]
    ms = jnp.mean(x * x, axis=-1, keepdims=True)
    xn = x * lax.rsqrt(ms + EPS) * g_ref[...]
    sh = ada_ref[0, 0:1, :]
    sc = ada_ref[0, 1:2, :]
    h = (xn * (1.0 + sc) + sh).astype(BF16)
    bd = bd_ref[...]
    lane = lax.broadcasted_iota(jnp.int32, (x.shape[0], NA_WIDTH), 1)
    first_half = (lane & (HEAD_DIM // 2)) == 0
    for grp in range(6):
        cols = slice(grp * 512, (grp + 1) * 512)
        acc = jnp.dot(h, w_ref[:, cols], preferred_element_type=F32)
        if grp in (2, 5):
            o_ref[:, cols] = acc.astype(BF16)
            continue
        gi = {0: 0, 1: 1, 3: 2, 4: 3}[grp]
        ss = _head_sumsq(acc, bd)
        y = acc * lax.rsqrt(ss * (1.0 / HEAD_DIM) + EPS) * gain_ref[gi:gi + 1, :]
        if grp in (3, 4):
            partner = jnp.where(first_half,
                                pltpu.roll(y, NA_WIDTH - HEAD_DIM // 2, axis=1),
                                pltpu.roll(y, HEAD_DIM // 2, axis=1))
            y = y * cos_ref[...] + partner * sin_ref[...]
        o_ref[:, cols] = y.astype(BF16)


def _qkv(x2d, ada_g, g_attn, w_qkv_bf, gains, cos_t, sin_t, bd, seq):
    n = x2d.shape[0]
    tm = ROW_BLOCK
    per_seq = seq // tm
    return pl.pallas_call(
        _qkv_kernel,
        grid=(n // tm,),
        in_specs=[pl.BlockSpec((tm, D_MODEL), lambda i: (i, 0)),
                  pl.BlockSpec((1, 6, D_MODEL), lambda i: (i // per_seq, 0, 0)),
                  pl.BlockSpec((1, D_MODEL), lambda i: (0, 0)),
                  pl.BlockSpec((D_MODEL, QKV_COLS), lambda i: (0, 0)),
                  pl.BlockSpec((4, NA_WIDTH), lambda i: (0, 0)),
                  pl.BlockSpec((tm, DIFF_WIDTH), lambda i: (i % per_seq, 0)),
                  pl.BlockSpec((tm, DIFF_WIDTH), lambda i: (i % per_seq, 0)),
                  pl.BlockSpec((MXU_DIM, MXU_DIM), lambda i: (0, 0))],
        out_specs=pl.BlockSpec((tm, QKV_COLS), lambda i: (i, 0)),
        out_shape=jax.ShapeDtypeStruct((n, QKV_COLS), BF16),
        compiler_params=_params("arbitrary"),
        name="qkv",
    )(x2d, ada_g, g_attn, w_qkv_bf, gains, cos_t, sin_t, bd)


def _na_kernel(q_ref, k_ref, v_ref, bias_ref, o_ref, vaug_ref, *, rows):
    lane = lax.broadcasted_iota(jnp.int32, (GRID_W, LANES), 1)
    head0 = lane < HEAD_DIM
    win = NA_WIN_ROWS * GRID_W

    def window_start(r):
        return jnp.clip(r - NA_WIN_ROWS // 2, 0, rows - NA_WIN_ROWS)

    def scores(r):
        r_start = window_start(r)
        q = q_ref[pl.ds(pl.multiple_of(r * GRID_W, GRID_W), GRID_W), :]
        kw = k_ref[pl.ds(pl.multiple_of(r_start * GRID_W, GRID_W), win), :]
        zero = jnp.zeros_like(q)
        qm = jnp.concatenate([jnp.where(head0, q, zero), jnp.where(head0, zero, q)], axis=0)
        s = lax.dot_general(kw, qm, (((1,), (1,)), ((), ())), preferred_element_type=F32)
        return s + bias_ref[0, r - r_start]

    vaug_ref[:, :LANES] = v_ref[...]
    vaug_ref[:, LANES:] = jnp.ones((v_ref.shape[0], LANES), BF16)

    def finish(r, s):
        vw = vaug_ref[pl.ds(pl.multiple_of(window_start(r) * GRID_W, GRID_W), win), :]
        m = _col_reduce(s, jnp.max)
        p = jnp.exp2(s - m).astype(BF16)
        o2 = lax.dot_general(p, vw, (((0,), (0,)), ((), ())), preferred_element_type=F32)
        o2 = o2[:, :LANES] * (1.0 / o2[:, LANES:])
        o = jnp.where(head0, o2[:GRID_W], o2[GRID_W:])
        o_ref[pl.ds(pl.multiple_of(r * GRID_W, GRID_W), GRID_W), :] = o.astype(BF16)

    def body(i, carry):
        trip_rows = [i * NA_ROWS_PER_TRIP + u for u in range(NA_ROWS_PER_TRIP)]
        trip_scores = [scores(r) for r in trip_rows]
        for r, s in zip(trip_rows, trip_scores):
            finish(r, s)
        return carry

    lax.fori_loop(0, rows // NA_ROWS_PER_TRIP, body, 0)


def _na(qkv, bias_t, batch, seq):
    rows = seq // GRID_W
    n_pairs = NA_HEADS // 2
    return pl.pallas_call(
        functools.partial(_na_kernel, rows=rows),
        grid=(batch, n_pairs),
        in_specs=[pl.BlockSpec((seq, LANES), lambda b, hp: (b, hp)),
                  pl.BlockSpec((seq, LANES), lambda b, hp: (b, n_pairs + hp)),
                  pl.BlockSpec((seq, LANES), lambda b, hp: (b, 2 * n_pairs + hp)),
                  pl.BlockSpec((1, NA_WIN_ROWS, NA_WIN_ROWS * GRID_W, LANES), lambda b, hp: (hp, 0, 0, 0))],
        out_specs=pl.BlockSpec((seq, LANES), lambda b, hp: (b, hp)),
        out_shape=jax.ShapeDtypeStruct((batch * seq, NA_WIDTH), BF16),
        scratch_shapes=[pltpu.VMEM((seq, 2 * LANES), BF16)],
        compiler_params=_params("arbitrary", "arbitrary"),
        name="na_attn",
    )(qkv, qkv, qkv, bias_t)


def _diff_kernel(q_ref, k_ref, v_ref, lam_ref, g_ref, o_ref, vt_ref, kn_ref, oa_ref):
    @pl.when(pl.program_id(2) == 0)
    def _():
        vt_ref[:LANES, :] = v_ref[...].astype(F32).T.astype(BF16)
        ones_row = lax.broadcasted_iota(jnp.int32, (VT_ROWS - LANES, v_ref.shape[0]), 0) == 0
        vt_ref[LANES:, :] = jnp.where(ones_row, 1.0, 0.0).astype(BF16)
        kf = k_ref[...].astype(F32)
        d_id = lax.broadcasted_iota(jnp.int32, (LANES, LANES), 0) // HEAD_DIM
        c_id = lax.broadcasted_iota(jnp.int32, (LANES, LANES), 1)
        comp_sel = jnp.where(d_id == c_id, 1.0, 0.0).astype(BF16)
        kn2 = jnp.dot((kf * kf).astype(BF16), comp_sel, preferred_element_type=F32)
        kn_ref[...] = jnp.sqrt(_col_reduce(kn2, jnp.max)) * NORM_SLACK

    lq1 = lam_ref[0:1, :]
    lk1 = lam_ref[1:2, :]
    lq2 = lam_ref[2:3, :]
    lk2 = lam_ref[3:4, :]
    lam = (jnp.exp(jnp.sum(lq1 * lk1, axis=-1, keepdims=True))
           - jnp.exp(jnp.sum(lq2 * lk2, axis=-1, keepdims=True)) + LAMBDA_INIT)

    q = q_ref[...]
    lane = lax.broadcasted_iota(jnp.int32, q.shape, 1)
    zero = jnp.zeros_like(q)
    tq = q.shape[0]
    qcat = jnp.concatenate([jnp.where(lane < HEAD_DIM, q, zero), jnp.where(lane < HEAD_DIM, zero, q)], axis=0)
    def scores(c, chunk):
        kc = k_ref[c * chunk:(c + 1) * chunk, :]
        return lax.dot_general(kc, qcat, (((1,), (1,)), ((), ())), preferred_element_type=F32)

    def chunked(chunk, step):
        n_chunks = k_ref.shape[0] // chunk
        s_next = scores(0, chunk)
        state = None
        for c in range(n_chunks):
            s = s_next
            if c + 1 < n_chunks:
                s_next = scores(c + 1, chunk)
            state = step(c, s, vt_ref[:, c * chunk:(c + 1) * chunk], state)
        return state

    qf = qcat.astype(F32)
    ones8 = jnp.ones((8, LANES), BF16)
    qn2 = lax.dot_general(ones8, (qf * qf).astype(BF16), (((1,), (1,)), ((), ())), preferred_element_type=F32)
    col = lax.broadcasted_iota(jnp.int32, (1, 2 * tq), 1)
    bound = jnp.sqrt(qn2[0:1, :]) * NORM_SLACK * jnp.where(col < tq, kn_ref[0:1, 0:1], kn_ref[0:1, 1:2])
    in_range = jnp.max(bound) <= MAX_SAFE_BOUND

    @pl.when(in_range)
    def _():
        def step(c, s, vtc, acc):
            pv = jnp.dot(vtc, jnp.exp2(s - bound).astype(BF16), preferred_element_type=F32)
            return pv if c == 0 else acc + pv
        oa_ref[...] = chunked(KV_CHUNK, step)

    @pl.when(jnp.logical_not(in_range))
    def _():
        def step(c, s, vtc, state):
            mc = _col_reduce(s, jnp.max)
            m_new = mc if c == 0 else jnp.maximum(state[0], mc)
            pv = jnp.dot(vtc, jnp.exp2(s - m_new).astype(BF16), preferred_element_type=F32)
            return (m_new, pv if c == 0 else jnp.exp2(state[0] - m_new) * state[1] + pv)
        oa_ref[...] = chunked(KV_CHUNK, step)[1]

    oa = oa_ref[...]
    o0, l0 = oa[:LANES, :tq], oa[LANES:LANES + 1, :tq]
    o1, l1 = oa[:LANES, tq:], oa[LANES:LANES + 1, tq:]
    o = o0 * (1.0 / l0) - (lam / l1) * o1
    ms = jnp.mean(o * o, axis=0, keepdims=True)
    y = o * lax.rsqrt(ms + EPS) * g_ref[...]
    o_ref[...] = y.T.astype(BF16)


def _diff(qkv, lam_vecs, subln_col, batch, seq):
    tq = Q_BLOCK
    nq = seq // tq
    base = 3 * NA_WIDTH // LANES
    nh = DIFF_HEADS
    return pl.pallas_call(
        _diff_kernel,
        grid=(batch, nh, nq),
        in_specs=[pl.BlockSpec((tq, LANES), lambda b, h, i: (b * nq + i, base + h)),
                  pl.BlockSpec((seq, LANES), lambda b, h, i: (b, base + nh + h)),
                  pl.BlockSpec((seq, LANES), lambda b, h, i: (b, base + 2 * nh + h)),
                  pl.BlockSpec((4, HEAD_DIM), lambda b, h, i: (0, 0)),
                  pl.BlockSpec((LANES, 1), lambda b, h, i: (0, 0))],
        out_specs=pl.BlockSpec((tq, LANES), lambda b, h, i: (b * nq + i, h)),
        out_shape=jax.ShapeDtypeStruct((batch * seq, DIFF_WIDTH), BF16),
        scratch_shapes=[pltpu.VMEM((VT_ROWS, seq), BF16),
                        pltpu.VMEM((1, LANES), F32),
                        pltpu.VMEM((VT_ROWS, 2 * tq), F32)],
        compiler_params=_params("arbitrary", "arbitrary", "arbitrary"),
        name="diff_attn",
    )(qkv, qkv, qkv, lam_vecs, subln_col)


def _wo_kernel(ona_ref, odf_ref, x_ref, ada_ref, wo_ref, g_ref, wrh_ref, wrl_ref, br_ref, tri_ref, cnt0_ref,
               x1_ref, h2_ref, ti_ref, tw_ref, cnt_ref):
    gt1 = ada_ref[0, 2:3, :]
    sh2 = ada_ref[0, 3:4, :]
    sc2 = ada_ref[0, 4:5, :]
    lane = lax.broadcasted_iota(jnp.int32, (WO_SUB, LANES), 1).astype(F32)

    def mix_of(rows):
        return (jnp.dot(ona_ref[rows, :], wo_ref[:NA_WIDTH, :], preferred_element_type=F32)
                + jnp.dot(odf_ref[rows, :], wo_ref[NA_WIDTH:, :], preferred_element_type=F32))

    def route(rows, mix):
        x1 = x_ref[rows, :] + gt1 * mix
        x1_ref[rows, :] = x1
        ms = jnp.mean(x1 * x1, axis=-1, keepdims=True)
        h2 = x1 * lax.rsqrt(ms + EPS) * g_ref[...] * (1.0 + sc2) + sh2
        hi = h2.astype(BF16)
        h2_ref[rows, :] = _pack_halves(hi.astype(F32))
        lo = (h2 - hi.astype(F32)).astype(BF16)
        both = jnp.dot(hi, wrl_ref[...], preferred_element_type=F32)
        cur = (both + pltpu.roll(both, LANES - N_EXPERTS, axis=1)
               + jnp.dot(lo, wrh_ref[...], preferred_element_type=F32)) + br_ref[...]
        vals = []
        idxs = []
        for _ in range(TOP_K):
            m = jnp.max(cur, axis=-1, keepdims=True)
            idx = jnp.min(jnp.where(cur == m, lane, float(LANES)), axis=-1, keepdims=True)
            vals.append(m)
            idxs.append(idx)
            cur = jnp.where(lane == idx, -jnp.inf, cur)
        es = [jnp.exp(v - vals[0]) for v in vals]
        inv = 1.0 / (es[0] + es[1] + es[2] + es[3])
        sel = jnp.zeros((WO_SUB, LANES), F32)
        for j in range(TOP_K):
            sel = sel + jnp.where(lane == idxs[j], 1.0, 0.0)
        return idxs, [e * inv for e in es], sel

    subs = [pl.ds(r, WO_SUB) for r in range(0, x_ref.shape[0], WO_SUB)]
    routed = []
    mix_next = mix_of(subs[0])
    for j, rows in enumerate(subs):
        mix = mix_next
        if j + 1 < len(subs):
            mix_next = mix_of(subs[j + 1])
        routed.append(route(rows, mix))

    @pl.when(pl.program_id(0) == 0)
    def _():
        cnt_ref[...] = cnt0_ref[...]

    sel_all = jnp.concatenate([r[2] for r in routed], axis=0)
    before = jnp.dot(tri_ref[...], sel_all.astype(BF16), preferred_element_type=F32) + cnt_ref[...]
    cnt_ref[...] = cnt_ref[...] + jnp.sum(sel_all, axis=0, keepdims=True)

    for s, rows in enumerate(subs):
        idxs, wts, _ = routed[s]
        bef = before[s * WO_SUB:(s + 1) * WO_SUB]
        ti = jnp.zeros((WO_SUB, LANES), F32)
        tw = jnp.zeros((WO_SUB, LANES), F32)
        for j in range(TOP_K):
            rank = jnp.sum(jnp.where(lane == idxs[j], bef, 0.0), axis=-1, keepdims=True)
            ti = jnp.where(lane == float(j), idxs[j], ti)
            ti = jnp.where(lane == float(TOP_K + j), rank, ti)
            tw = jnp.where(lane == float(j), wts[j], tw)
        ti_ref[rows, :] = ti.astype(jnp.int32)
        tw_ref[rows, :] = tw


def _wo(o_na, o_df, x2d, ada_g, w_o_bf, g_ffn, wr_hi, wr_lo, br_pad, tri, cnt0, seq):
    n = x2d.shape[0]
    tm = WO_BLOCK
    per_seq = seq // tm
    row = lambda i: (i, 0)
    const = lambda i: (0, 0)
    return pl.pallas_call(
        _wo_kernel,
        grid=(n // tm,),
        in_specs=[pl.BlockSpec((tm, NA_WIDTH), row),
                  pl.BlockSpec((tm, DIFF_WIDTH), row),
                  pl.BlockSpec((tm, D_MODEL), row),
                  pl.BlockSpec((1, 6, D_MODEL), lambda i: (i // per_seq, 0, 0)),
                  pl.BlockSpec((D_MODEL, D_MODEL), const),
                  pl.BlockSpec((1, D_MODEL), const),
                  pl.BlockSpec((D_MODEL, LANES), const),
                  pl.BlockSpec((D_MODEL, LANES), const),
                  pl.BlockSpec((1, LANES), const),
                  pl.BlockSpec((tm, tm), const),
                  pl.BlockSpec((1, LANES), const)],
        out_specs=[pl.BlockSpec((tm, D_MODEL), row),
                   pl.BlockSpec((tm, PACKED), row),
                   pl.BlockSpec((tm, LANES), row),
                   pl.BlockSpec((tm, LANES), row),
                   pl.BlockSpec((1, LANES), const)],
        out_shape=[jax.ShapeDtypeStruct((n, D_MODEL), F32),
                   jax.ShapeDtypeStruct((n, PACKED), U32),
                   jax.ShapeDtypeStruct((n, LANES), jnp.int32),
                   jax.ShapeDtypeStruct((n, LANES), F32),
                   jax.ShapeDtypeStruct((1, LANES), F32)],
        compiler_params=_params("arbitrary"),
        name="wo_router",
    )(o_na, o_df, x2d, ada_g, w_o_bf, g_ffn, wr_hi, wr_lo, br_pad, tri, cnt0)


def _expert_kernel(be_ref, nxt_ref, na_ref, xs_ref, wg_hbm, bg_ref, wu_hbm, bu_ref, wd_hbm, bd_ref, o_ref,
                   w_f32, w_bf, sems, slot_ref):
    i = pl.program_id(0)
    active = i < na_ref[0]
    expert = be_ref[i]
    new_expert = jnp.logical_or(i == 0, expert != be_ref[jnp.maximum(i - 1, 0)])
    wg_bf, wu_bf, wd_bf = w_bf.at[0], w_bf.at[1], w_bf.at[2]

    def weight_copies(src_expert, slot):
        return [pltpu.make_async_copy(w_hbm.at[src_expert], w_f32.at[slot, j], sems.at[slot, j])
                for j, w_hbm in enumerate((wg_hbm, wu_hbm, wd_hbm))]

    @pl.when(i == 0)
    def _():
        slot_ref[0] = 0
        for cp in weight_copies(expert, 0):
            cp.start()

    @pl.when(jnp.logical_and(active, new_expert))
    def _():
        slot = slot_ref[0]
        nxt = nxt_ref[i]
        for s in range(2):
            @pl.when(slot == s)
            def _():
                @pl.when(nxt >= 0)
                def _():
                    for cp in weight_copies(nxt, 1 - s):
                        cp.start(priority=1)
                for j, cp in enumerate(weight_copies(expert, s)):
                    cp.wait()
                    w_bf[j] = w_f32[s, j].astype(BF16)
        slot_ref[0] = 1 - slot

    @pl.when(active)
    def _():
        def gate_up(rows):
            x_lo, x_hi = _unpack_halves(xs_ref[rows, :])
            x_lo = x_lo.astype(BF16)
            x_hi = x_hi.astype(BF16)

            def proj(w_bf):
                return (jnp.dot(x_lo, w_bf[:PACKED, :], preferred_element_type=F32)
                        + jnp.dot(x_hi, w_bf[PACKED:, :], preferred_element_type=F32))

            return proj(wg_bf), proj(wu_bf)

        def act_down(rows, gu):
            g = jnp.minimum(gu[0] + bg_ref[0], SWIGLU_LIMIT)
            u = jnp.clip(gu[1] + bu_ref[0], -SWIGLU_LIMIT, SWIGLU_LIMIT)
            act = g * jax.nn.sigmoid(SWIGLU_ALPHA * g) * (u + 1.0)
            out = jnp.dot(act.astype(BF16), wd_bf[...], preferred_element_type=F32) + bd_ref[0]
            o_ref[rows, :] = _pack_halves(out.astype(BF16).astype(F32))

        sub = [pl.ds(r, EXPERT_SUB) for r in range(0, EXPERT_BLOCK, EXPERT_SUB)]
        gu_next = gate_up(sub[0])
        for j, rows in enumerate(sub):
            gu = gu_next
            if j + 1 < len(sub):
                gu_next = gate_up(sub[j + 1])
            act_down(rows, gu)

    @pl.when(i >= na_ref[0])
    def _():
        o_ref[...] = jnp.zeros_like(o_ref)


def _experts(block_e, next_e, n_active, xs, wg, bg, wu, bu, wd, bd):
    cap = xs.shape[0]
    n_blocks = cap // EXPERT_BLOCK
    xmap = lambda i, be, nx, na: (jnp.minimum(i, na[0] - 1), 0)
    bmap = lambda i, be, nx, na: (be[i], 0, 0)
    hbm = pl.BlockSpec(memory_space=pl.ANY)
    grid_spec = pltpu.PrefetchScalarGridSpec(
        num_scalar_prefetch=3,
        grid=(n_blocks,),
        in_specs=[pl.BlockSpec((EXPERT_BLOCK, PACKED), xmap),
                  hbm, pl.BlockSpec((1, 1, D_MODEL), bmap),
                  hbm, pl.BlockSpec((1, 1, D_MODEL), bmap),
                  hbm, pl.BlockSpec((1, 1, D_MODEL), bmap)],
        out_specs=pl.BlockSpec((EXPERT_BLOCK, PACKED), lambda i, be, nx, na: (i, 0)),
        scratch_shapes=[pltpu.VMEM((2, 3, D_MODEL, D_MODEL), F32),
                        pltpu.VMEM((3, D_MODEL, D_MODEL), BF16),
                        pltpu.SemaphoreType.DMA((2, 3)),
                        pltpu.SMEM((1,), jnp.int32)],
    )
    return pl.pallas_call(
        _expert_kernel,
        grid_spec=grid_spec,
        out_shape=jax.ShapeDtypeStruct((cap, PACKED), U32),
        compiler_params=_params("arbitrary"),
        name="experts",
    )(block_e, next_e, n_active, xs, wg, bg, wu, bu, wd, bd)


def _sc_gather(table, idx):
    n_out = idx.shape[0]
    width = table.shape[1]
    per_worker = n_out // SC_WORKERS
    n_chunks = per_worker // GATHER_ROWS
    assert per_worker * SC_WORKERS == n_out and n_chunks * GATHER_ROWS == per_worker and n_chunks % 2 == 0
    idx3 = idx.reshape(SC_WORKERS, n_chunks, GATHER_ROWS)
    mesh = plsc.VectorSubcoreMesh(core_axis_name="core", subcore_axis_name="subcore")

    @functools.partial(
        pl.kernel, mesh=mesh,
        out_type=jax.ShapeDtypeStruct((n_out, width), table.dtype),
        scratch_types=[pltpu.VMEM((n_chunks, GATHER_ROWS), jnp.int32),
                       pltpu.VMEM((2, GATHER_ROWS, width), table.dtype),
                       pltpu.SemaphoreType.DMA((2,)),
                       pltpu.SemaphoreType.DMA((2,))])
    def gather_kernel(table_hbm, idx_hbm, out_hbm, idx_v, rows_v, gsem, wsem):
        wid = lax.axis_index("subcore") * SC_CORES + lax.axis_index("core")
        base = wid * per_worker
        pltpu.sync_copy(idx_hbm.at[wid], idx_v)

        def gather(j, slot):
            return pltpu.make_async_copy(table_hbm.at[idx_v.at[j]], rows_v.at[slot], gsem.at[slot])

        def write(j, slot):
            dst = out_hbm.at[pl.ds(pl.multiple_of(base + j * GATHER_ROWS, GATHER_ROWS), GATHER_ROWS)]
            return pltpu.make_async_copy(rows_v.at[slot], dst, wsem.at[slot])

        gather(0, 0).start()

        @pl.loop(0, n_chunks, step=2)
        def _(j):
            for slot in range(2):
                jj = j + slot
                gather(jj, slot).wait()

                @pl.when(jj >= 1)
                def _():
                    write(jj - 1, 1 - slot).wait()

                @pl.when(jj + 1 < n_chunks)
                def _():
                    gather(jj + 1, 1 - slot).start()

                write(jj, slot).start()

        write(n_chunks - 1, 1).wait()

    return gather_kernel(table, idx3)


def _sc_scatter(rows, idx, n_out):
    n_src, width = rows.shape
    n_idx = idx.shape[0]
    per_worker = n_idx // SC_WORKERS
    n_chunks = per_worker // GATHER_ROWS
    assert per_worker * SC_WORKERS == n_idx and n_chunks * GATHER_ROWS == per_worker and n_chunks % 2 == 0
    assert n_src % per_worker == 0
    idx3 = idx.reshape(SC_WORKERS, n_chunks, GATHER_ROWS)
    mesh = plsc.VectorSubcoreMesh(core_axis_name="core", subcore_axis_name="subcore")

    @functools.partial(
        pl.kernel, mesh=mesh,
        out_type=jax.ShapeDtypeStruct((n_out, width), rows.dtype),
        scratch_types=[pltpu.VMEM((n_chunks, GATHER_ROWS), jnp.int32),
                       pltpu.VMEM((2, GATHER_ROWS, width), rows.dtype),
                       pltpu.SemaphoreType.DMA((2,)),
                       pltpu.SemaphoreType.DMA((2,))])
    def scatter_kernel(rows_hbm, idx_hbm, out_hbm, idx_v, rows_v, rsem, wsem):
        wid = lax.axis_index("subcore") * SC_CORES + lax.axis_index("core")
        base = lax.rem(wid * per_worker, n_src)
        pltpu.sync_copy(idx_hbm.at[wid], idx_v)

        def read(j, slot):
            src = rows_hbm.at[pl.ds(pl.multiple_of(base + j * GATHER_ROWS, GATHER_ROWS), GATHER_ROWS)]
            return pltpu.make_async_copy(src, rows_v.at[slot], rsem.at[slot])

        def write(j, slot):
            return pltpu.make_async_copy(rows_v.at[slot], out_hbm.at[idx_v.at[j]], wsem.at[slot])

        read(0, 0).start()

        @pl.loop(0, n_chunks, step=2)
        def _(j):
            for slot in range(2):
                jj = j + slot
                read(jj, slot).wait()

                @pl.when(jj >= 1)
                def _():
                    write(jj - 1, 1 - slot).wait()

                @pl.when(jj + 1 < n_chunks)
                def _():
                    read(jj + 1, 1 - slot).start()

                write(jj, slot).start()

        write(n_chunks - 1, 1).wait()

    return scatter_kernel(rows, idx3)


def _combine_kernel(x1_ref, y0_ref, y1_ref, y2_ref, y3_ref, tw_ref, ada_ref, o_ref):
    tw = tw_ref[...]
    acc_lo = jnp.zeros((x1_ref.shape[0], PACKED), F32)
    acc_hi = jnp.zeros((x1_ref.shape[0], PACKED), F32)
    for j, y_ref in enumerate((y0_ref, y1_ref, y2_ref, y3_ref)):
        lo, hi = _unpack_halves(y_ref[...])
        acc_lo = acc_lo + tw[:, j:j + 1] * lo
        acc_hi = acc_hi + tw[:, j:j + 1] * hi
    o_ref[:, :PACKED] = x1_ref[:, :PACKED] + ada_ref[0, 5:6, :PACKED] * acc_lo
    o_ref[:, PACKED:] = x1_ref[:, PACKED:] + ada_ref[0, 5:6, PACKED:] * acc_hi


def _combine(x1, ys, tw, ada_g, seq, row_off):
    n = x1.shape[0]
    tm = ROW_BLOCK
    per_seq = seq // tm
    off = row_off // tm
    per_choice = tw.shape[0] // tm
    y_specs = [pl.BlockSpec((tm, PACKED), functools.partial(lambda i, j: (j * per_choice + off + i, 0), j=j))
               for j in range(TOP_K)]
    return pl.pallas_call(
        _combine_kernel,
        grid=(n // tm,),
        in_specs=[pl.BlockSpec((tm, D_MODEL), lambda i: (i, 0)),
                  *y_specs,
                  pl.BlockSpec((tm, LANES), lambda i: (i + off, 0)),
                  pl.BlockSpec((1, 6, D_MODEL), lambda i: (i // per_seq, 0, 0))],
        out_specs=pl.BlockSpec((tm, D_MODEL), lambda i: (i, 0)),
        out_shape=jax.ShapeDtypeStruct((n, D_MODEL), F32),
        compiler_params=_params("arbitrary"),
        name="combine",
    )(x1, ys, ys, ys, ys, tw, ada_g)


def _rope_tables(seq):
    half = HEAD_DIM // 2
    inv = ROPE_THETA ** (-jnp.arange(half, dtype=F32) / half)
    ang = jnp.arange(seq, dtype=F32)[:, None] * inv[None, :]
    cos, sin = jnp.cos(ang), jnp.sin(ang)
    cos_h = jnp.concatenate([cos, cos], axis=-1)
    sin_h = jnp.concatenate([-sin, sin], axis=-1)
    reps = DIFF_WIDTH // HEAD_DIM
    return jnp.tile(cos_h, (1, reps)), jnp.tile(sin_h, (1, reps))


def _na_bias_table(rpb):
    cols = jnp.arange(GRID_W, dtype=jnp.int32)
    c_start = jnp.clip(cols - NA_WIN_COLS // 2, 0, GRID_W - NA_WIN_COLS)
    col_mask = (cols[None, :] >= c_start[:, None]) & (cols[None, :] < c_start[:, None] + NA_WIN_COLS)
    col_idx = jnp.clip(cols[None, :] - cols[:, None], -(NA_WIN_COLS - 1), NA_WIN_COLS - 1) + NA_WIN_COLS - 1
    delta = jnp.arange(NA_WIN_ROWS, dtype=jnp.int32)
    j = jnp.arange(NA_WIN_ROWS, dtype=jnp.int32)
    row_idx = j[None, :] - delta[:, None] + NA_WIN_ROWS - 1
    row_hot = (row_idx[:, :, None] == jnp.arange(2 * NA_WIN_ROWS - 1, dtype=jnp.int32)).astype(F32)
    col_hot = (col_idx[:, :, None] == jnp.arange(2 * NA_WIN_COLS - 1, dtype=jnp.int32)).astype(F32)
    bias = jnp.einsum('djr,hrc,qkc->hdjqk', row_hot, rpb.astype(F32), col_hot,
                      precision=lax.Precision.HIGHEST)
    bias = jnp.where(col_mask[None, None, None], bias * LOG2E, NEG_INF)
    bias = bias.transpose(0, 1, 2, 4, 3).reshape(NA_HEADS, NA_WIN_ROWS, NA_WIN_ROWS * GRID_W, GRID_W)
    bias = bias.reshape(NA_HEADS // 2, 2, NA_WIN_ROWS, NA_WIN_ROWS * GRID_W, GRID_W)
    return jnp.concatenate([bias[:, 0], bias[:, 1]], axis=-1)


def _routing(top_idx, rank, counts, n):
    n_blocks = n * TOP_K // EXPERT_BLOCK + N_EXPERTS
    experts = jnp.arange(N_EXPERTS, dtype=jnp.int32)
    padded = (counts + EXPERT_BLOCK - 1) // EXPERT_BLOCK * EXPERT_BLOCK
    pad_end = jnp.cumsum(padded)
    pad_start = pad_end - padded
    start_of = jnp.sum(jnp.where(top_idx[:, :, None] == experts, pad_start, 0), axis=-1)
    dest = (start_of + rank).T.reshape(-1)
    block_lo = jnp.arange(n_blocks, dtype=jnp.int32) * EXPERT_BLOCK
    block_e = jnp.minimum(jnp.sum((pad_end[None, :] <= block_lo[:, None]).astype(jnp.int32), axis=1),
                          N_EXPERTS - 1).astype(jnp.int32)
    n_active = (pad_end[-1] // EXPERT_BLOCK).astype(jnp.int32).reshape(1)
    later = jnp.where((experts[None, :] > experts[:, None]) & (padded[None, :] > 0), experts[None, :], N_EXPERTS)
    next_nonempty = jnp.min(later, axis=1)
    next_nonempty = jnp.where(next_nonempty == N_EXPERTS, -1, next_nonempty)
    next_e = jnp.sum(jnp.where(block_e[:, None] == experts[None, :], next_nonempty[None, :], 0), axis=1)
    return dest, block_e, next_e.astype(jnp.int32), n_active, n_blocks * EXPERT_BLOCK


def kernel(x_prompt, x_sample, c_prompt, c_sample, w_ada, b_ada, g_attn_norm, w_qkv, na_q_norm, na_k_norm, na_rpb, diff_q_norm, diff_k_norm, lambda_q1, lambda_k1, lambda_q2, lambda_k2, diff_subln, w_o, g_ffn_norm, w_router, b_router, w_gate, b_gate, w_up, b_up, w_down, b_down):
    l = 0
    groups = [(x_prompt, c_prompt), (x_sample, c_sample)]
    nb = [x.shape[0] for x, _ in groups]

    ada_all = _ada(jnp.concatenate([c for _, c in groups], axis=0), w_ada[l], b_ada[l])
    ada_all = ada_all.reshape(sum(nb), 6, D_MODEL)

    w_qkv_bf = w_qkv[l].astype(BF16)
    w_o_bf = w_o[l].astype(BF16)
    scale = HEAD_DIM ** -0.5
    reps = NA_WIDTH // HEAD_DIM
    gains = jnp.stack([jnp.tile(na_q_norm[l], reps) * (scale * LOG2E),
                       jnp.tile(na_k_norm[l], reps),
                       jnp.tile(diff_q_norm[l], reps) * (scale * LOG2E),
                       jnp.tile(diff_k_norm[l], reps)]).astype(F32)
    head_id = jnp.arange(MXU_DIM, dtype=jnp.int32) // HEAD_DIM
    bd = (head_id[:, None] == head_id[None, :]).astype(BF16)
    bias_t = _na_bias_table(na_rpb[l])
    lam_vecs = jnp.stack([lambda_q1[l], lambda_k1[l], lambda_q2[l], lambda_k2[l]]).astype(F32)
    subln_col = (diff_subln[l].astype(F32) * (1.0 - LAMBDA_INIT)).reshape(LANES, 1)
    wr = w_router[l].astype(F32)
    wr_pad = jnp.zeros((D_MODEL, LANES), F32).at[:, :N_EXPERTS].set(wr)
    wr_hi = wr_pad.astype(BF16)
    wr_lo = (wr_pad - wr_hi.astype(F32)).astype(BF16)
    wr_lo = wr_hi.at[:, N_EXPERTS:2 * N_EXPERTS].set(wr_lo[:, :N_EXPERTS])
    br_pad = jnp.full((1, LANES), NEG_INF, F32).at[0, :N_EXPERTS].set(b_router[l].astype(F32))
    g_attn = g_attn_norm[l].reshape(1, D_MODEL).astype(F32)
    g_ffn = g_ffn_norm[l].reshape(1, D_MODEL).astype(F32)
    max_seq = max(x.shape[1] for x, _ in groups)
    cos_t, sin_t = _rope_tables(max_seq)

    rows = lax.broadcasted_iota(jnp.int32, (WO_BLOCK, WO_BLOCK), 0)
    cols = lax.broadcasted_iota(jnp.int32, (WO_BLOCK, WO_BLOCK), 1)
    tri = (cols < rows).astype(BF16)
    cnt0 = jnp.zeros((1, LANES), F32)

    bg = b_gate[l].reshape(N_EXPERTS, 1, D_MODEL).astype(F32)
    bu = b_up[l].reshape(N_EXPERTS, 1, D_MODEL).astype(F32)
    bdn = b_down[l].reshape(N_EXPERTS, 1, D_MODEL).astype(F32)
    ada_groups = [ada_all[:nb[0]], ada_all[nb[0]:]]

    order = sorted(range(len(groups)), key=lambda g: -groups[g][0].shape[1])
    staged = {}
    for g in order:
        x = groups[g][0]
        b, seq = x.shape[0], x.shape[1]
        n = b * seq
        x2d = x.reshape(n, D_MODEL)
        ada_g = ada_groups[g]
        qkv = _qkv(x2d, ada_g, g_attn, w_qkv_bf, gains, cos_t, sin_t, bd, seq)
        o_na = _na(qkv, bias_t, b, seq)
        o_df = _diff(qkv, lam_vecs, subln_col, b, seq)
        x1, h2, ti, tw, cnt = _wo(o_na, o_df, x2d, ada_g, w_o_bf, g_ffn, wr_hi, wr_lo, br_pad, tri, cnt0, seq)
        counts = cnt[0, :N_EXPERTS].astype(jnp.int32)
        dest, block_e, next_e, n_active, cap = _routing(ti[:, :TOP_K], ti[:, TOP_K:2 * TOP_K], counts, n)
        xs = _sc_scatter(h2, dest, cap)
        staged[g] = (x1, tw, ada_g, dest, block_e, next_e, n_active, xs, seq, b)

    sorted_out = {}
    for g in order:
        x1, tw, ada_g, dest, block_e, next_e, n_active, xs, seq, b = staged[g]
        sorted_out[g] = _experts(block_e, next_e, n_active, xs, w_gate[l], bg, w_up[l], bu, w_down[l], bdn)

    outs = [None] * len(groups)
    for g in order:
        x1, tw, ada_g, dest, block_e, next_e, n_active, xs, seq, b = staged[g]
        ys = _sc_gather(sorted_out[g], dest)
        outs[g] = _combine(x1, ys, tw, ada_g, seq, 0).reshape(b, seq, D_MODEL)
    return tuple(outs)
```

```python
import functools
import math

import jax
import jax.numpy as jnp
from jax import lax
from jax.experimental import pallas as pl
from jax.experimental.pallas import tpu as pltpu
from jax.experimental.pallas import tpu_sc as plsc

F32 = jnp.float32
BF16 = jnp.bfloat16
U32 = jnp.uint32

D_MODEL = 1024
HEAD_DIM = 64
NA_HEADS = 8
NA_WIDTH = 512
DIFF_HEADS = 4
DIFF_WIDTH = 512
QKV_COLS = 3072
GRID_W = 64
NA_WIN_ROWS = 8
NA_WIN_COLS = 16
ROPE_THETA = 10000.0
N_EXPERTS = 32
TOP_K = 4
SWIGLU_LIMIT = 7.0
SWIGLU_ALPHA = 1.702
EPS = 1e-5
NEG_INF = -1e30
LAMBDA_INIT = 0.8 - 0.6 * math.exp(-0.3 * 0)
LOG2E = 1.4426950408889634

LANES = 128
MXU_DIM = 256
VMEM_LIMIT = 56 * 1024 * 1024

ROW_BLOCK = 512
Q_BLOCK = 1024
EXPERT_BLOCK = 512
EXPERT_SUB = 256
WO_BLOCK = 1024
WO_SUB = 256
NA_ROWS_PER_TRIP = 16
VT_ROWS = LANES + 16
KV_CHUNK = 256
NORM_SLACK = 1.01
MAX_SAFE_BOUND = 60.0


PACKED = D_MODEL // 2
SC_CORES = 2
SC_SUBCORES = 16
SC_WORKERS = SC_CORES * SC_SUBCORES
GATHER_ROWS = 64


def _params(*sem):
    return pltpu.CompilerParams(dimension_semantics=sem, vmem_limit_bytes=VMEM_LIMIT)


def _pack_halves(x):
    w = x.shape[1] // 2
    bits = lax.bitcast_convert_type(x, U32)
    return (bits[:, :w] >> 16) | bits[:, w:]


def _col_reduce(x, op):
    while x.shape[0] >= 64:
        x = op(x.reshape(8, x.shape[0] // 8, x.shape[1]), axis=0)
    return op(x, axis=0, keepdims=True)


def _unpack_halves(word):
    lo = lax.bitcast_convert_type(word << 16, F32)
    hi = lax.bitcast_convert_type(word & jnp.uint32(0xFFFF0000), F32)
    return lo, hi


def _ada_kernel(c_ref, w_ref, b_ref, o_ref):
    c = c_ref[...]
    s = c * jax.nn.sigmoid(c)
    o_ref[...] = jnp.dot(s, w_ref[...], preferred_element_type=F32,
                         precision=lax.Precision.HIGHEST) + b_ref[...]


def _ada(c_all, w_ada, b_ada):
    nb = c_all.shape[0]
    n_out = w_ada.shape[1]
    blk = D_MODEL
    return pl.pallas_call(
        _ada_kernel,
        grid=(n_out // blk,),
        in_specs=[pl.BlockSpec((nb, D_MODEL), lambda j: (0, 0)),
                  pl.BlockSpec((D_MODEL, blk), lambda j: (0, j)),
                  pl.BlockSpec((1, blk), lambda j: (0, j))],
        out_specs=pl.BlockSpec((nb, blk), lambda j: (0, j)),
        out_shape=jax.ShapeDtypeStruct((nb, n_out), F32),
        compiler_params=_params("arbitrary"),
        name="ada",
    )(c_all, w_ada, b_ada.reshape(1, n_out))


def _head_sumsq(y, bd):
    sq = (y * y).astype(BF16)
    parts = [jnp.dot(sq[:, c:c + MXU_DIM], bd, preferred_element_type=F32)
             for c in range(0, y.shape[1], MXU_DIM)]
    return jnp.concatenate(parts, axis=1)


def _qkv_kernel(x_ref, ada_ref, g_ref, w_ref, gain_ref, cos_ref, sin_ref, bd_ref, o_ref):
    x = x_ref[...]
    ms = jnp.mean(x * x, axis=-1, keepdims=True)
    xn = x * lax.rsqrt(ms + EPS) * g_ref[...]
    sh = ada_ref[0, 0:1, :]
    sc = ada_ref[0, 1:2, :]
    h = (xn * (1.0 + sc) + sh).astype(BF16)
    bd = bd_ref[...]
    lane = lax.broadcasted_iota(jnp.int32, (x.shape[0], NA_WIDTH), 1)
    first_half = (lane & (HEAD_DIM // 2)) == 0
    for grp in range(6):
        cols = slice(grp * 512, (grp + 1) * 512)
        acc = jnp.dot(h, w_ref[:, cols], preferred_element_type=F32)
        if grp in (2, 5):
            o_ref[:, cols] = acc.astype(BF16)
            continue
        gi = {0: 0, 1: 1, 3: 2, 4: 3}[grp]
        ss = _head_sumsq(acc, bd)
        y = acc * lax.rsqrt(ss * (1.0 / HEAD_DIM) + EPS) * gain_ref[gi:gi + 1, :]
        if grp in (3, 4):
            partner = jnp.where(first_half,
                                pltpu.roll(y, NA_WIDTH - HEAD_DIM // 2, axis=1),
                                pltpu.roll(y, HEAD_DIM // 2, axis=1))
            y = y * cos_ref[...] + partner * sin_ref[...]
        o_ref[:, cols] = y.astype(BF16)


def _qkv(x2d, ada_g, g_attn, w_qkv_bf, gains, cos_t, sin_t, bd, seq):
    n = x2d.shape[0]
    tm = ROW_BLOCK
    per_seq = seq // tm
    return pl.pallas_call(
        _qkv_kernel,
        grid=(n // tm,),
        in_specs=[pl.BlockSpec((tm, D_MODEL), lambda i: (i, 0)),
                  pl.BlockSpec((1, 6, D_MODEL), lambda i: (i // per_seq, 0, 0)),
                  pl.BlockSpec((1, D_MODEL), lambda i: (0, 0)),
                  pl.BlockSpec((D_MODEL, QKV_COLS), lambda i: (0, 0)),
                  pl.BlockSpec((4, NA_WIDTH), lambda i: (0, 0)),
                  pl.BlockSpec((tm, DIFF_WIDTH), lambda i: (i % per_seq, 0)),
                  pl.BlockSpec((tm, DIFF_WIDTH), lambda i: (i % per_seq, 0)),
                  pl.BlockSpec((MXU_DIM, MXU_DIM), lambda i: (0, 0))],
        out_specs=pl.BlockSpec((tm, QKV_COLS), lambda i: (i, 0)),
        out_shape=jax.ShapeDtypeStruct((n, QKV_COLS), BF16),
        compiler_params=_params("arbitrary"),
        name="qkv",
    )(x2d, ada_g, g_attn, w_qkv_bf, gains, cos_t, sin_t, bd)


def _na_kernel(q_ref, k_ref, v_ref, bias_ref, o_ref, vaug_ref, *, rows):
    lane = lax.broadcasted_iota(jnp.int32, (GRID_W, LANES), 1)
    head0 = lane < HEAD_DIM
    win = NA_WIN_ROWS * GRID_W

    def window_start(r):
        return jnp.clip(r - NA_WIN_ROWS // 2, 0, rows - NA_WIN_ROWS)

    def scores(r):
        r_start = window_start(r)
        q = q_ref[pl.ds(pl.multiple_of(r * GRID_W, GRID_W), GRID_W), :]
        kw = k_ref[pl.ds(pl.multiple_of(r_start * GRID_W, GRID_W), win), :]
        zero = jnp.zeros_like(q)
        qm = jnp.concatenate([jnp.where(head0, q, zero), jnp.where(head0, zero, q)], axis=0)
        s = lax.dot_general(kw, qm, (((1,), (1,)), ((), ())), preferred_element_type=F32)
        return s + bias_ref[0, r - r_start]

    vaug_ref[:, :LANES] = v_ref[...]
    vaug_ref[:, LANES:] = jnp.ones((v_ref.shape[0], LANES), BF16)

    def finish(r, s):
        vw = vaug_ref[pl.ds(pl.multiple_of(window_start(r) * GRID_W, GRID_W), win), :]
        m = _col_reduce(s, jnp.max)
        p = jnp.exp2(s - m).astype(BF16)
        o2 = lax.dot_general(p, vw, (((0,), (0,)), ((), ())), preferred_element_type=F32)
        o2 = o2[:, :LANES] * (1.0 / o2[:, LANES:])
        o = jnp.where(head0, o2[:GRID_W], o2[GRID_W:])
        o_ref[pl.ds(pl.multiple_of(r * GRID_W, GRID_W), GRID_W), :] = o.astype(BF16)

    def body(i, carry):
        trip_rows = [i * NA_ROWS_PER_TRIP + u for u in range(NA_ROWS_PER_TRIP)]
        trip_scores = [scores(r) for r in trip_rows]
        for r, s in zip(trip_rows, trip_scores):
            finish(r, s)
        return carry

    lax.fori_loop(0, rows // NA_ROWS_PER_TRIP, body, 0)


def _na(qkv, bias_t, batch, seq):
    rows = seq // GRID_W
    n_pairs = NA_HEADS // 2
    return pl.pallas_call(
        functools.partial(_na_kernel, rows=rows),
        grid=(batch, n_pairs),
        in_specs=[pl.BlockSpec((seq, LANES), lambda b, hp: (b, hp)),
                  pl.BlockSpec((seq, LANES), lambda b, hp: (b, n_pairs + hp)),
                  pl.BlockSpec((seq, LANES), lambda b, hp: (b, 2 * n_pairs + hp)),
                  pl.BlockSpec((1, NA_WIN_ROWS, NA_WIN_ROWS * GRID_W, LANES), lambda b, hp: (hp, 0, 0, 0))],
        out_specs=pl.BlockSpec((seq, LANES), lambda b, hp: (b, hp)),
        out_shape=jax.ShapeDtypeStruct((batch * seq, NA_WIDTH), BF16),
        scratch_shapes=[pltpu.VMEM((seq, 2 * LANES), BF16)],
        compiler_params=_params("arbitrary", "arbitrary"),
        name="na_attn",
    )(qkv, qkv, qkv, bias_t)


def _diff_kernel(q_ref, k_ref, v_ref, lam_ref, g_ref, o_ref, vt_ref, kn_ref, oa_ref):
    @pl.when(pl.program_id(2) == 0)
    def _():
        vt_ref[:LANES, :] = v_ref[...].astype(F32).T.astype(BF16)
        ones_row = lax.broadcasted_iota(jnp.int32, (VT_ROWS - LANES, v_ref.shape[0]), 0) == 0
        vt_ref[LANES:, :] = jnp.where(ones_row, 1.0, 0.0).astype(BF16)
        kf = k_ref[...].astype(F32)
        d_id = lax.broadcasted_iota(jnp.int32, (LANES, LANES), 0) // HEAD_DIM
        c_id = lax.broadcasted_iota(jnp.int32, (LANES, LANES), 1)
        comp_sel = jnp.where(d_id == c_id, 1.0, 0.0).astype(BF16)
        kn2 = jnp.dot((kf * kf).astype(BF16), comp_sel, preferred_element_type=F32)
        kn_ref[...] = jnp.sqrt(_col_reduce(kn2, jnp.max)) * NORM_SLACK

    lq1 = lam_ref[0:1, :]
    lk1 = lam_ref[1:2, :]
    lq2 = lam_ref[2:3, :]
    lk2 = lam_ref[3:4, :]
    lam = (jnp.exp(jnp.sum(lq1 * lk1, axis=-1, keepdims=True))
           - jnp.exp(jnp.sum(lq2 * lk2, axis=-1, keepdims=True)) + LAMBDA_INIT)

    q = q_ref[...]
    lane = lax.broadcasted_iota(jnp.int32, q.shape, 1)
    zero = jnp.zeros_like(q)
    tq = q.shape[0]
    qcat = jnp.concatenate([jnp.where(lane < HEAD_DIM, q, zero), jnp.where(lane < HEAD_DIM, zero, q)], axis=0)
    def scores(c, chunk):
        kc = k_ref[c * chunk:(c + 1) * chunk, :]
        return lax.dot_general(kc, qcat, (((1,), (1,)), ((), ())), preferred_element_type=F32)

    def chunked(chunk, step):
        n_chunks = k_ref.shape[0] // chunk
        s_next = scores(0, chunk)
        state = None
        for c in range(n_chunks):
            s = s_next
            if c + 1 < n_chunks:
                s_next = scores(c + 1, chunk)
            state = step(c, s, vt_ref[:, c * chunk:(c + 1) * chunk], state)
        return state

    qf = qcat.astype(F32)
    ones8 = jnp.ones((8, LANES), BF16)
    qn2 = lax.dot_general(ones8, (qf * qf).astype(BF16), (((1,), (1,)), ((), ())), preferred_element_type=F32)
    col = lax.broadcasted_iota(jnp.int32, (1, 2 * tq), 1)
    bound = jnp.sqrt(qn2[0:1, :]) * NORM_SLACK * jnp.where(col < tq, kn_ref[0:1, 0:1], kn_ref[0:1, 1:2])
    in_range = jnp.max(bound) <= MAX_SAFE_BOUND

    @pl.when(in_range)
    def _():
        def step(c, s, vtc, acc):
            pv = jnp.dot(vtc, jnp.exp2(s - bound).astype(BF16), preferred_element_type=F32)
            return pv if c == 0 else acc + pv
        oa_ref[...] = chunked(KV_CHUNK, step)

    @pl.when(jnp.logical_not(in_range))
    def _():
        def step(c, s, vtc, state):
            mc = _col_reduce(s, jnp.max)
            m_new = mc if c == 0 else jnp.maximum(state[0], mc)
            pv = jnp.dot(vtc, jnp.exp2(s - m_new).astype(BF16), preferred_element_type=F32)
            return (m_new, pv if c == 0 else jnp.exp2(state[0] - m_new) * state[1] + pv)
        oa_ref[...] = chunked(KV_CHUNK, step)[1]

    oa = oa_ref[...]
    o0, l0 = oa[:LANES, :tq], oa[LANES:LANES + 1, :tq]
    o1, l1 = oa[:LANES, tq:], oa[LANES:LANES + 1, tq:]
    o = o0 * (1.0 / l0) - (lam / l1) * o1
    ms = jnp.mean(o * o, axis=0, keepdims=True)
    y = o * lax.rsqrt(ms + EPS) * g_ref[...]
    o_ref[...] = y.T.astype(BF16)


def _diff(qkv, lam_vecs, subln_col, batch, seq):
    tq = Q_BLOCK
    nq = seq // tq
    base = 3 * NA_WIDTH // LANES
    nh = DIFF_HEADS
    return pl.pallas_call(
        _diff_kernel,
        grid=(batch, nh, nq),
        in_specs=[pl.BlockSpec((tq, LANES), lambda b, h, i: (b * nq + i, base + h)),
                  pl.BlockSpec((seq, LANES), lambda b, h, i: (b, base + nh + h)),
                  pl.BlockSpec((seq, LANES), lambda b, h, i: (b, base + 2 * nh + h)),
                  pl.BlockSpec((4, HEAD_DIM), lambda b, h, i: (0, 0)),
                  pl.BlockSpec((LANES, 1), lambda b, h, i: (0, 0))],
        out_specs=pl.BlockSpec((tq, LANES), lambda b, h, i: (b * nq + i, h)),
        out_shape=jax.ShapeDtypeStruct((batch * seq, DIFF_WIDTH), BF16),
        scratch_shapes=[pltpu.VMEM((VT_ROWS, seq), BF16),
                        pltpu.VMEM((1, LANES), F32),
                        pltpu.VMEM((VT_ROWS, 2 * tq), F32)],
        compiler_params=_params("arbitrary", "arbitrary", "arbitrary"),
        name="diff_attn",
    )(qkv, qkv, qkv, lam_vecs, subln_col)


def _wo_kernel(ona_ref, odf_ref, x_ref, ada_ref, wo_ref, g_ref, wrh_ref, wrl_ref, br_ref, tri_ref, cnt0_ref,
               x1_ref, h2_ref, ti_ref, tw_ref, cnt_ref):
    gt1 = ada_ref[0, 2:3, :]
    sh2 = ada_ref[0, 3:4, :]
    sc2 = ada_ref[0, 4:5, :]
    lane = lax.broadcasted_iota(jnp.int32, (WO_SUB, LANES), 1).astype(F32)

    def mix_of(rows):
        return (jnp.dot(ona_ref[rows, :], wo_ref[:NA_WIDTH, :], preferred_element_type=F32)
                + jnp.dot(odf_ref[rows, :], wo_ref[NA_WIDTH:, :], preferred_element_type=F32))

    def route(rows, mix):
        x1 = x_ref[rows, :] + gt1 * mix
        x1_ref[rows, :] = x1
        ms = jnp.mean(x1 * x1, axis=-1, keepdims=True)
        h2 = x1 * lax.rsqrt(ms + EPS) * g_ref[...] * (1.0 + sc2) + sh2
        hi = h2.astype(BF16)
        h2_ref[rows, :] = _pack_halves(hi.astype(F32))
        lo = (h2 - hi.astype(F32)).astype(BF16)
        both = jnp.dot(hi, wrl_ref[...], preferred_element_type=F32)
        cur = (both + pltpu.roll(both, LANES - N_EXPERTS, axis=1)
               + jnp.dot(lo, wrh_ref[...], preferred_element_type=F32)) + br_ref[...]
        vals = []
        idxs = []
        for _ in range(TOP_K):
            m = jnp.max(cur, axis=-1, keepdims=True)
            idx = jnp.min(jnp.where(cur == m, lane, float(LANES)), axis=-1, keepdims=True)
            vals.append(m)
            idxs.append(idx)
            cur = jnp.where(lane == idx, -jnp.inf, cur)
        es = [jnp.exp(v - vals[0]) for v in vals]
        inv = 1.0 / (es[0] + es[1] + es[2] + es[3])
        sel = jnp.zeros((WO_SUB, LANES), F32)
        for j in range(TOP_K):
            sel = sel + jnp.where(lane == idxs[j], 1.0, 0.0)
        return idxs, [e * inv for e in es], sel

    subs = [pl.ds(r, WO_SUB) for r in range(0, x_ref.shape[0], WO_SUB)]
    routed = []
    mix_next = mix_of(subs[0])
    for j, rows in enumerate(subs):
        mix = mix_next
        if j + 1 < len(subs):
            mix_next = mix_of(subs[j + 1])
        routed.append(route(rows, mix))

    @pl.when(pl.program_id(0) == 0)
    def _():
        cnt_ref[...] = cnt0_ref[...]

    sel_all = jnp.concatenate([r[2] for r in routed], axis=0)
    before = jnp.dot(tri_ref[...], sel_all.astype(BF16), preferred_element_type=F32) + cnt_ref[...]
    cnt_ref[...] = cnt_ref[...] + jnp.sum(sel_all, axis=0, keepdims=True)

    for s, rows in enumerate(subs):
        idxs, wts, _ = routed[s]
        bef = before[s * WO_SUB:(s + 1) * WO_SUB]
        ti = jnp.zeros((WO_SUB, LANES), F32)
        tw = jnp.zeros((WO_SUB, LANES), F32)
        for j in range(TOP_K):
            rank = jnp.sum(jnp.where(lane == idxs[j], bef, 0.0), axis=-1, keepdims=True)
            ti = jnp.where(lane == float(j), idxs[j], ti)
            ti = jnp.where(lane == float(TOP_K + j), rank, ti)
            tw = jnp.where(lane == float(j), wts[j], tw)
        ti_ref[rows, :] = ti.astype(jnp.int32)
        tw_ref[rows, :] = tw


def _wo(o_na, o_df, x2d, ada_g, w_o_bf, g_ffn, wr_hi, wr_lo, br_pad, tri, cnt0, seq):
    n = x2d.shape[0]
    tm = WO_BLOCK
    per_seq = seq // tm
    row = lambda i: (i, 0)
    const = lambda i: (0, 0)
    return pl.pallas_call(
        _wo_kernel,
        grid=(n // tm,),
        in_specs=[pl.BlockSpec((tm, NA_WIDTH), row),
                  pl.BlockSpec((tm, DIFF_WIDTH), row),
                  pl.BlockSpec((tm, D_MODEL), row),
                  pl.BlockSpec((1, 6, D_MODEL), lambda i: (i // per_seq, 0, 0)),
                  pl.BlockSpec((D_MODEL, D_MODEL), const),
                  pl.BlockSpec((1, D_MODEL), const),
                  pl.BlockSpec((D_MODEL, LANES), const),
                  pl.BlockSpec((D_MODEL, LANES), const),
                  pl.BlockSpec((1, LANES), const),
                  pl.BlockSpec((tm, tm), const),
                  pl.BlockSpec((1, LANES), const)],
        out_specs=[pl.BlockSpec((tm, D_MODEL), row),
                   pl.BlockSpec((tm, PACKED), row),
                   pl.BlockSpec((tm, LANES), row),
                   pl.BlockSpec((tm, LANES), row),
                   pl.BlockSpec((1, LANES), const)],
        out_shape=[jax.ShapeDtypeStruct((n, D_MODEL), F32),
                   jax.ShapeDtypeStruct((n, PACKED), U32),
                   jax.ShapeDtypeStruct((n, LANES), jnp.int32),
                   jax.ShapeDtypeStruct((n, LANES), F32),
                   jax.ShapeDtypeStruct((1, LANES), F32)],
        compiler_params=_params("arbitrary"),
        name="wo_router",
    )(o_na, o_df, x2d, ada_g, w_o_bf, g_ffn, wr_hi, wr_lo, br_pad, tri, cnt0)


def _expert_kernel(be_ref, nxt_ref, na_ref, xs_ref, wg_hbm, bg_ref, wu_hbm, bu_ref, wd_hbm, bd_ref, o_ref,
                   w_f32, w_bf, sems, slot_ref):
    i = pl.program_id(0)
    active = i < na_ref[0]
    expert = be_ref[i]
    new_expert = jnp.logical_or(i == 0, expert != be_ref[jnp.maximum(i - 1, 0)])
    wg_bf, wu_bf, wd_bf = w_bf.at[0], w_bf.at[1], w_bf.at[2]

    def weight_copies(src_expert, slot):
        return [pltpu.make_async_copy(w_hbm.at[src_expert], w_f32.at[slot, j], sems.at[slot, j])
                for j, w_hbm in enumerate((wg_hbm, wu_hbm, wd_hbm))]

    @pl.when(i == 0)
    def _():
        slot_ref[0] = 0
        for cp in weight_copies(expert, 0):
            cp.start()

    @pl.when(jnp.logical_and(active, new_expert))
    def _():
        slot = slot_ref[0]
        nxt = nxt_ref[i]
        for s in range(2):
            @pl.when(slot == s)
            def _():
                @pl.when(nxt >= 0)
                def _():
                    for cp in weight_copies(nxt, 1 - s):
                        cp.start(priority=1)
                for j, cp in enumerate(weight_copies(expert, s)):
                    cp.wait()
                    w_bf[j] = w_f32[s, j].astype(BF16)
        slot_ref[0] = 1 - slot

    @pl.when(active)
    def _():
        def gate_up(rows):
            x_lo, x_hi = _unpack_halves(xs_ref[rows, :])
            x_lo = x_lo.astype(BF16)
            x_hi = x_hi.astype(BF16)

            def proj(w_bf):
                return (jnp.dot(x_lo, w_bf[:PACKED, :], preferred_element_type=F32)
                        + jnp.dot(x_hi, w_bf[PACKED:, :], preferred_element_type=F32))

            return proj(wg_bf), proj(wu_bf)

        def act_down(rows, gu):
            g = jnp.minimum(gu[0] + bg_ref[0], SWIGLU_LIMIT)
            u = jnp.clip(gu[1] + bu_ref[0], -SWIGLU_LIMIT, SWIGLU_LIMIT)
            act = g * jax.nn.sigmoid(SWIGLU_ALPHA * g) * (u + 1.0)
            out = jnp.dot(act.astype(BF16), wd_bf[...], preferred_element_type=F32) + bd_ref[0]
            o_ref[rows, :] = _pack_halves(out.astype(BF16).astype(F32))

        sub = [pl.ds(r, EXPERT_SUB) for r in range(0, EXPERT_BLOCK, EXPERT_SUB)]
        gu_next = gate_up(sub[0])
        for j, rows in enumerate(sub):
            gu = gu_next
            if j + 1 < len(sub):
                gu_next = gate_up(sub[j + 1])
            act_down(rows, gu)

    @pl.when(i >= na_ref[0])
    def _():
        o_ref[...] = jnp.zeros_like(o_ref)


def _experts(block_e, next_e, n_active, xs, wg, bg, wu, bu, wd, bd):
    cap = xs.shape[0]
    n_blocks = cap // EXPERT_BLOCK
    xmap = lambda i, be, nx, na: (jnp.minimum(i, na[0] - 1), 0)
    bmap = lambda i, be, nx, na: (be[i], 0, 0)
    hbm = pl.BlockSpec(memory_space=pl.ANY)
    grid_spec = pltpu.PrefetchScalarGridSpec(
        num_scalar_prefetch=3,
        grid=(n_blocks,),
        in_specs=[pl.BlockSpec((EXPERT_BLOCK, PACKED), xmap),
                  hbm, pl.BlockSpec((1, 1, D_MODEL), bmap),
                  hbm, pl.BlockSpec((1, 1, D_MODEL), bmap),
                  hbm, pl.BlockSpec((1, 1, D_MODEL), bmap)],
        out_specs=pl.BlockSpec((EXPERT_BLOCK, PACKED), lambda i, be, nx, na: (i, 0)),
        scratch_shapes=[pltpu.VMEM((2, 3, D_MODEL, D_MODEL), F32),
                        pltpu.VMEM((3, D_MODEL, D_MODEL), BF16),
                        pltpu.SemaphoreType.DMA((2, 3)),
                        pltpu.SMEM((1,), jnp.int32)],
    )
    return pl.pallas_call(
        _expert_kernel,
        grid_spec=grid_spec,
        out_shape=jax.ShapeDtypeStruct((cap, PACKED), U32),
        compiler_params=_params("arbitrary"),
        name="experts",
    )(block_e, next_e, n_active, xs, wg, bg, wu, bu, wd, bd)


def _sc_gather(table, idx):
    n_out = idx.shape[0]
    width = table.shape[1]
    per_worker = n_out // SC_WORKERS
    n_chunks = per_worker // GATHER_ROWS
    assert per_worker * SC_WORKERS == n_out and n_chunks * GATHER_ROWS == per_worker and n_chunks % 2 == 0
    idx3 = idx.reshape(SC_WORKERS, n_chunks, GATHER_ROWS)
    mesh = plsc.VectorSubcoreMesh(core_axis_name="core", subcore_axis_name="subcore")

    @functools.partial(
        pl.kernel, mesh=mesh,
        out_type=jax.ShapeDtypeStruct((n_out, width), table.dtype),
        scratch_types=[pltpu.VMEM((n_chunks, GATHER_ROWS), jnp.int32),
                       pltpu.VMEM((2, GATHER_ROWS, width), table.dtype),
                       pltpu.SemaphoreType.DMA((2,)),
                       pltpu.SemaphoreType.DMA((2,))])
    def gather_kernel(table_hbm, idx_hbm, out_hbm, idx_v, rows_v, gsem, wsem):
        wid = lax.axis_index("subcore") * SC_CORES + lax.axis_index("core")
        base = wid * per_worker
        pltpu.sync_copy(idx_hbm.at[wid], idx_v)

        def gather(j, slot):
            return pltpu.make_async_copy(table_hbm.at[idx_v.at[j]], rows_v.at[slot], gsem.at[slot])

        def write(j, slot):
            dst = out_hbm.at[pl.ds(pl.multiple_of(base + j * GATHER_ROWS, GATHER_ROWS), GATHER_ROWS)]
            return pltpu.make_async_copy(rows_v.at[slot], dst, wsem.at[slot])

        gather(0, 0).start()

        @pl.loop(0, n_chunks, step=2)
        def _(j):
            for slot in range(2):
                jj = j + slot
                gather(jj, slot).wait()

                @pl.when(jj >= 1)
                def _():
                    write(jj - 1, 1 - slot).wait()

                @pl.when(jj + 1 < n_chunks)
                def _():
                    gather(jj + 1, 1 - slot).start()

                write(jj, slot).start()

        write(n_chunks - 1, 1).wait()

    return gather_kernel(table, idx3)


def _sc_scatter(rows, idx, n_out):
    n_src, width = rows.shape
    n_idx = idx.shape[0]
    per_worker = n_idx // SC_WORKERS
    n_chunks = per_worker // GATHER_ROWS
    assert per_worker * SC_WORKERS == n_idx and n_chunks * GATHER_ROWS == per_worker and n_chunks % 2 == 0
    assert n_src % per_worker == 0
    idx3 = idx.reshape(SC_WORKERS, n_chunks, GATHER_ROWS)
    mesh = plsc.VectorSubcoreMesh(core_axis_name="core", subcore_axis_name="subcore")

    @functools.partial(
        pl.kernel, mesh=mesh,
        out_type=jax.ShapeDtypeStruct((n_out, width), rows.dtype),
        scratch_types=[pltpu.VMEM((n_chunks, GATHER_ROWS), jnp.int32),
                       pltpu.VMEM((2, GATHER_ROWS, width), rows.dtype),
                       pltpu.SemaphoreType.DMA((2,)),
                       pltpu.SemaphoreType.DMA((2,))])
    def scatter_kernel(rows_hbm, idx_hbm, out_hbm, idx_v, rows_v, rsem, wsem):
        wid = lax.axis_index("subcore") * SC_CORES + lax.axis_index("core")
        base = lax.rem(wid * per_worker, n_src)
        pltpu.sync_copy(idx_hbm.at[wid], idx_v)

        def read(j, slot):
            src = rows_hbm.at[pl.ds(pl.multiple_of(base + j * GATHER_ROWS, GATHER_ROWS), GATHER_ROWS)]
            return pltpu.make_async_copy(src, rows_v.at[slot], rsem.at[slot])

        def write(j, slot):
            return pltpu.make_async_copy(rows_v.at[slot], out_hbm.at[idx_v.at[j]], wsem.at[slot])

        read(0, 0).start()

        @pl.loop(0, n_chunks, step=2)
        def _(j):
            for slot in range(2):
                jj = j + slot
                read(jj, slot).wait()

                @pl.when(jj >= 1)
                def _():
                    write(jj - 1, 1 - slot).wait()

                @pl.when(jj + 1 < n_chunks)
                def _():
                    read(jj + 1, 1 - slot).start()

                write(jj, slot).start()

        write(n_chunks - 1, 1).wait()

    return scatter_kernel(rows, idx3)


def _combine_kernel(x1_ref, y0_ref, y1_ref, y2_ref, y3_ref, tw_ref, ada_ref, o_ref):
    tw = tw_ref[...]
    acc_lo = jnp.zeros((x1_ref.shape[0], PACKED), F32)
    acc_hi = jnp.zeros((x1_ref.shape[0], PACKED), F32)
    for j, y_ref in enumerate((y0_ref, y1_ref, y2_ref, y3_ref)):
        lo, hi = _unpack_halves(y_ref[...])
        acc_lo = acc_lo + tw[:, j:j + 1] * lo
        acc_hi = acc_hi + tw[:, j:j + 1] * hi
    o_ref[:, :PACKED] = x1_ref[:, :PACKED] + ada_ref[0, 5:6, :PACKED] * acc_lo
    o_ref[:, PACKED:] = x1_ref[:, PACKED:] + ada_ref[0, 5:6, PACKED:] * acc_hi


def _combine(x1, ys, tw, ada_g, seq, row_off):
    n = x1.shape[0]
    tm = ROW_BLOCK
    per_seq = seq // tm
    off = row_off // tm
    per_choice = tw.shape[0] // tm
    y_specs = [pl.BlockSpec((tm, PACKED), functools.partial(lambda i, j: (j * per_choice + off + i, 0), j=j))
               for j in range(TOP_K)]
    return pl.pallas_call(
        _combine_kernel,
        grid=(n // tm,),
        in_specs=[pl.BlockSpec((tm, D_MODEL), lambda i: (i, 0)),
                  *y_specs,
                  pl.BlockSpec((tm, LANES), lambda i: (i + off, 0)),
                  pl.BlockSpec((1, 6, D_MODEL), lambda i: (i // per_seq, 0, 0))],
        out_specs=pl.BlockSpec((tm, D_MODEL), lambda i: (i, 0)),
        out_shape=jax.ShapeDtypeStruct((n, D_MODEL), F32),
        compiler_params=_params("arbitrary"),
        name="combine",
    )(x1, ys, ys, ys, ys, tw, ada_g)


def _rope_tables(seq):
    half = HEAD_DIM // 2
    inv = ROPE_THETA ** (-jnp.arange(half, dtype=F32) / half)
    ang = jnp.arange(seq, dtype=F32)[:, None] * inv[None, :]
    cos, sin = jnp.cos(ang), jnp.sin(ang)
    cos_h = jnp.concatenate([cos, cos], axis=-1)
    sin_h = jnp.concatenate([-sin, sin], axis=-1)
    reps = DIFF_WIDTH // HEAD_DIM
    return jnp.tile(cos_h, (1, reps)), jnp.tile(sin_h, (1, reps))


def _na_bias_table(rpb):
    cols = jnp.arange(GRID_W, dtype=jnp.int32)
    c_start = jnp.clip(cols - NA_WIN_COLS // 2, 0, GRID_W - NA_WIN_COLS)
    col_mask = (cols[None, :] >= c_start[:, None]) & (cols[None, :] < c_start[:, None] + NA_WIN_COLS)
    col_idx = jnp.clip(cols[None, :] - cols[:, None], -(NA_WIN_COLS - 1), NA_WIN_COLS - 1) + NA_WIN_COLS - 1
    delta = jnp.arange(NA_WIN_ROWS, dtype=jnp.int32)
    j = jnp.arange(NA_WIN_ROWS, dtype=jnp.int32)
    row_idx = j[None, :] - delta[:, None] + NA_WIN_ROWS - 1
    row_hot = (row_idx[:, :, None] == jnp.arange(2 * NA_WIN_ROWS - 1, dtype=jnp.int32)).astype(F32)
    col_hot = (col_idx[:, :, None] == jnp.arange(2 * NA_WIN_COLS - 1, dtype=jnp.int32)).astype(F32)
    bias = jnp.einsum('djr,hrc,qkc->hdjqk', row_hot, rpb.astype(F32), col_hot,
                      precision=lax.Precision.HIGHEST)
    bias = jnp.where(col_mask[None, None, None], bias * LOG2E, NEG_INF)
    bias = bias.transpose(0, 1, 2, 4, 3).reshape(NA_HEADS, NA_WIN_ROWS, NA_WIN_ROWS * GRID_W, GRID_W)
    bias = bias.reshape(NA_HEADS // 2, 2, NA_WIN_ROWS, NA_WIN_ROWS * GRID_W, GRID_W)
    return jnp.concatenate([bias[:, 0], bias[:, 1]], axis=-1)


def _routing(top_idx, rank, counts, n):
    n_blocks = n * TOP_K // EXPERT_BLOCK + N_EXPERTS
    experts = jnp.arange(N_EXPERTS, dtype=jnp.int32)
    padded = (counts + EXPERT_BLOCK - 1) // EXPERT_BLOCK * EXPERT_BLOCK
    pad_end = jnp.cumsum(padded)
    pad_start = pad_end - padded
    start_of = jnp.sum(jnp.where(top_idx[:, :, None] == experts, pad_start, 0), axis=-1)
    dest = (start_of + rank).T.reshape(-1)
    block_lo = jnp.arange(n_blocks, dtype=jnp.int32) * EXPERT_BLOCK
    block_e = jnp.minimum(jnp.sum((pad_end[None, :] <= block_lo[:, None]).astype(jnp.int32), axis=1),
                          N_EXPERTS - 1).astype(jnp.int32)
    n_active = (pad_end[-1] // EXPERT_BLOCK).astype(jnp.int32).reshape(1)
    later = jnp.where((experts[None, :] > experts[:, None]) & (padded[None, :] > 0), experts[None, :], N_EXPERTS)
    next_nonempty = jnp.min(later, axis=1)
    next_nonempty = jnp.where(next_nonempty == N_EXPERTS, -1, next_nonempty)
    next_e = jnp.sum(jnp.where(block_e[:, None] == experts[None, :], next_nonempty[None, :], 0), axis=1)
    return dest, block_e, next_e.astype(jnp.int32), n_active, n_blocks * EXPERT_BLOCK


def kernel(x_prompt, x_sample, c_prompt, c_sample, w_ada, b_ada, g_attn_norm, w_qkv, na_q_norm, na_k_norm, na_rpb, diff_q_norm, diff_k_norm, lambda_q1, lambda_k1, lambda_q2, lambda_k2, diff_subln, w_o, g_ffn_norm, w_router, b_router, w_gate, b_gate, w_up, b_up, w_down, b_down):
    l = 0
    groups = [(x_prompt, c_prompt), (x_sample, c_sample)]
    nb = [x.shape[0] for x, _ in groups]

    ada_all = _ada(jnp.concatenate([c for _, c in groups], axis=0), w_ada[l], b_ada[l])
    ada_all = ada_all.reshape(sum(nb), 6, D_MODEL)

    w_qkv_bf = w_qkv[l].astype(BF16)
    w_o_bf = w_o[l].astype(BF16)
    scale = HEAD_DIM ** -0.5
    reps = NA_WIDTH // HEAD_DIM
    gains = jnp.stack([jnp.tile(na_q_norm[l], reps) * (scale * LOG2E),
                       jnp.tile(na_k_norm[l], reps),
                       jnp.tile(diff_q_norm[l], reps) * (scale * LOG2E),
                       jnp.tile(diff_k_norm[l], reps)]).astype(F32)
    head_id = jnp.arange(MXU_DIM, dtype=jnp.int32) // HEAD_DIM
    bd = (head_id[:, None] == head_id[None, :]).astype(BF16)
    bias_t = _na_bias_table(na_rpb[l])
    lam_vecs = jnp.stack([lambda_q1[l], lambda_k1[l], lambda_q2[l], lambda_k2[l]]).astype(F32)
    subln_col = (diff_subln[l].astype(F32) * (1.0 - LAMBDA_INIT)).reshape(LANES, 1)
    wr = w_router[l].astype(F32)
    wr_pad = jnp.zeros((D_MODEL, LANES), F32).at[:, :N_EXPERTS].set(wr)
    wr_hi = wr_pad.astype(BF16)
    wr_lo = (wr_pad - wr_hi.astype(F32)).astype(BF16)
    wr_lo = wr_hi.at[:, N_EXPERTS:2 * N_EXPERTS].set(wr_lo[:, :N_EXPERTS])
    br_pad = jnp.full((1, LANES), NEG_INF, F32).at[0, :N_EXPERTS].set(b_router[l].astype(F32))
    g_attn = g_attn_norm[l].reshape(1, D_MODEL).astype(F32)
    g_ffn = g_ffn_norm[l].reshape(1, D_MODEL).astype(F32)
    max_seq = max(x.shape[1] for x, _ in groups)
    cos_t, sin_t = _rope_tables(max_seq)

    rows = lax.broadcasted_iota(jnp.int32, (WO_BLOCK, WO_BLOCK), 0)
    cols = lax.broadcasted_iota(jnp.int32, (WO_BLOCK, WO_BLOCK), 1)
    tri = (cols < rows).astype(BF16)
    cnt0 = jnp.zeros((1, LANES), F32)

    bg = b_gate[l].reshape(N_EXPERTS, 1, D_MODEL).astype(F32)
    bu = b_up[l].reshape(N_EXPERTS, 1, D_MODEL).astype(F32)
    bdn = b_down[l].reshape(N_EXPERTS, 1, D_MODEL).astype(F32)
    ada_groups = [ada_all[:nb[0]], ada_all[nb[0]:]]

    order = sorted(range(len(groups)), key=lambda g: -groups[g][0].shape[1])
    staged = {}
    for g in order:
        x = groups[g][0]
        b, seq = x.shape[0], x.shape[1]
        n = b * seq
        x2d = x.reshape(n, D_MODEL)
        ada_g = ada_groups[g]
        qkv = _qkv(x2d, ada_g, g_attn, w_qkv_bf, gains, cos_t, sin_t, bd, seq)
        o_na = _na(qkv, bias_t, b, seq)
        o_df = _diff(qkv, lam_vecs, subln_col, b, seq)
        x1, h2, ti, tw, cnt = _wo(o_na, o_df, x2d, ada_g, w_o_bf, g_ffn, wr_hi, wr_lo, br_pad, tri, cnt0, seq)
        counts = cnt[0, :N_EXPERTS].astype(jnp.int32)
        dest, block_e, next_e, n_active, cap = _routing(ti[:, :TOP_K], ti[:, TOP_K:2 * TOP_K], counts, n)
        xs = _sc_scatter(h2, dest, cap)
        staged[g] = (x1, tw, ada_g, dest, block_e, next_e, n_active, xs, seq, b)

    sorted_out = {}
    for g in order:
        x1, tw, ada_g, dest, block_e, next_e, n_active, xs, seq, b = staged[g]
        sorted_out[g] = _experts(block_e, next_e, n_active, xs, w_gate[l], bg, w_up[l], bu, w_down[l], bdn)

    outs = [None] * len(groups)
    for g in order:
        x1, tw, ada_g, dest, block_e, next_e, n_active, xs, seq, b = staged[g]
        ys = _sc_gather(sorted_out[g], dest)
        outs[g] = _combine(x1, ys, tw, ada_g, seq, 0).reshape(b, seq, D_MODEL)
    return tuple(outs)
```

```python
import functools
import math

import jax
import jax.numpy as jnp
from jax import lax
from jax.experimental import pallas as pl
from jax.experimental.pallas import tpu as pltpu
from jax.experimental.pallas import tpu_sc as plsc

F32 = jnp.float32
BF16 = jnp.bfloat16
U32 = jnp.uint32

D_MODEL = 1024
HEAD_DIM = 64
NA_HEADS = 8
NA_WIDTH = 512
DIFF_HEADS = 4
DIFF_WIDTH = 512
QKV_COLS = 3072
GRID_W = 64
NA_WIN_ROWS = 8
NA_WIN_COLS = 16
ROPE_THETA = 10000.0
N_EXPERTS = 32
TOP_K = 4
SWIGLU_LIMIT = 7.0
SWIGLU_ALPHA = 1.702
EPS = 1e-5
NEG_INF = -1e30
LAMBDA_INIT = 0.8 - 0.6 * math.exp(-0.3 * 0)
LOG2E = 1.4426950408889634

LANES = 128
MXU_DIM = 256
VMEM_LIMIT = 56 * 1024 * 1024

ROW_BLOCK = 512
Q_BLOCK = 1024
EXPERT_BLOCK = 512
EXPERT_SUB = 256
WO_BLOCK = 1024
WO_SUB = 256
NA_ROWS_PER_TRIP = 16
VT_ROWS = LANES + 16
KV_CHUNK = 256
NORM_SLACK = 1.01
MAX_SAFE_BOUND = 60.0


PACKED = D_MODEL // 2
SC_CORES = 2
SC_SUBCORES = 16
SC_WORKERS = SC_CORES * SC_SUBCORES
GATHER_ROWS = 64


def _params(*sem):
    return pltpu.CompilerParams(dimension_semantics=sem, vmem_limit_bytes=VMEM_LIMIT)


def _pack_halves(x):
    w = x.shape[1] // 2
    bits = lax.bitcast_convert_type(x, U32)
    return (bits[:, :w] >> 16) | bits[:, w:]


def _col_reduce(x, op):
    while x.shape[0] >= 64:
        x = op(x.reshape(8, x.shape[0] // 8, x.shape[1]), axis=0)
    return op(x, axis=0, keepdims=True)


def _unpack_halves(word):
    lo = lax.bitcast_convert_type(word << 16, F32)
    hi = lax.bitcast_convert_type(word & jnp.uint32(0xFFFF0000), F32)
    return lo, hi


def _ada_kernel(c_ref, w_ref, b_ref, o_ref):
    c = c_ref[...]
    s = c * jax.nn.sigmoid(c)
    o_ref[...] = jnp.dot(s, w_ref[...], preferred_element_type=F32,
                         precision=lax.Precision.HIGHEST) + b_ref[...]


def _ada(c_all, w_ada, b_ada):
    nb = c_all.shape[0]
    n_out = w_ada.shape[1]
    blk = D_MODEL
    return pl.pallas_call(
        _ada_kernel,
        grid=(n_out // blk,),
        in_specs=[pl.BlockSpec((nb, D_MODEL), lambda j: (0, 0)),
                  pl.BlockSpec((D_MODEL, blk), lambda j: (0, j)),
                  pl.BlockSpec((1, blk), lambda j: (0, j))],
        out_specs=pl.BlockSpec((nb, blk), lambda j: (0, j)),
        out_shape=jax.ShapeDtypeStruct((nb, n_out), F32),
        compiler_params=_params("arbitrary"),
        name="ada",
    )(c_all, w_ada, b_ada.reshape(1, n_out))


def _head_sumsq(y, bd):
    sq = (y * y).astype(BF16)
    parts = [jnp.dot(sq[:, c:c + MXU_DIM], bd, preferred_element_type=F32)
             for c in range(0, y.shape[1], MXU_DIM)]
    return jnp.concatenate(parts, axis=1)


def _qkv_kernel(x_ref, ada_ref, g_ref, w_ref, gain_ref, cos_ref, sin_ref, bd_ref, o_ref):
    x = x_ref[...]
    ms = jnp.mean(x * x, axis=-1, keepdims=True)
    xn = x * lax.rsqrt(ms + EPS) * g_ref[...]
    sh = ada_ref[0, 0:1, :]
    sc = ada_ref[0, 1:2, :]
    h = (xn * (1.0 + sc) + sh).astype(BF16)
    bd = bd_ref[...]
    lane = lax.broadcasted_iota(jnp.int32, (x.shape[0], NA_WIDTH), 1)
    first_half = (lane & (HEAD_DIM // 2)) == 0
    for grp in range(6):
        cols = slice(grp * 512, (grp + 1) * 512)
        acc = jnp.dot(h, w_ref[:, cols], preferred_element_type=F32)
        if grp in (2, 5):
            o_ref[:, cols] = acc.astype(BF16)
            continue
        gi = {0: 0, 1: 1, 3: 2, 4: 3}[grp]
        ss = _head_sumsq(acc, bd)
        y = acc * lax.rsqrt(ss * (1.0 / HEAD_DIM) + EPS) * gain_ref[gi:gi + 1, :]
        if grp in (3, 4):
            partner = jnp.where(first_half,
                                pltpu.roll(y, NA_WIDTH - HEAD_DIM // 2, axis=1),
                                pltpu.roll(y, HEAD_DIM // 2, axis=1))
            reps = NA_WIDTH // LANES
            y = (y * jnp.concatenate([cos_ref[...]] * reps, axis=1)
                 + partner * jnp.concatenate([sin_ref[...]] * reps, axis=1))
        o_ref[:, cols] = y.astype(BF16)


def _qkv(x2d, ada_g, g_attn, w_qkv_bf, gains, cos_t, sin_t, bd, seq):
    n = x2d.shape[0]
    tm = ROW_BLOCK
    per_seq = seq // tm
    return pl.pallas_call(
        _qkv_kernel,
        grid=(n // tm,),
        in_specs=[pl.BlockSpec((tm, D_MODEL), lambda i: (i, 0)),
                  pl.BlockSpec((1, 6, D_MODEL), lambda i: (i // per_seq, 0, 0)),
                  pl.BlockSpec((1, D_MODEL), lambda i: (0, 0)),
                  pl.BlockSpec((D_MODEL, QKV_COLS), lambda i: (0, 0)),
                  pl.BlockSpec((4, NA_WIDTH), lambda i: (0, 0)),
                  pl.BlockSpec((tm, LANES), lambda i: (i % per_seq, 0)),
                  pl.BlockSpec((tm, LANES), lambda i: (i % per_seq, 0)),
                  pl.BlockSpec((MXU_DIM, MXU_DIM), lambda i: (0, 0))],
        out_specs=pl.BlockSpec((tm, QKV_COLS), lambda i: (i, 0)),
        out_shape=jax.ShapeDtypeStruct((n, QKV_COLS), BF16),
        compiler_params=_params("arbitrary"),
        name="qkv",
    )(x2d, ada_g, g_attn, w_qkv_bf, gains, cos_t, sin_t, bd)


def _na_kernel(q_ref, k_ref, v_ref, bias_ref, o_ref, vaug_ref, *, rows):
    lane = lax.broadcasted_iota(jnp.int32, (GRID_W, LANES), 1)
    head0 = lane < HEAD_DIM
    win = NA_WIN_ROWS * GRID_W

    def window_start(r):
        return jnp.clip(r - NA_WIN_ROWS // 2, 0, rows - NA_WIN_ROWS)

    def scores(r):
        r_start = window_start(r)
        q = q_ref[pl.ds(pl.multiple_of(r * GRID_W, GRID_W), GRID_W), :]
        kw = k_ref[pl.ds(pl.multiple_of(r_start * GRID_W, GRID_W), win), :]
        zero = jnp.zeros_like(q)
        qm = jnp.concatenate([jnp.where(head0, q, zero), jnp.where(head0, zero, q)], axis=0)
        s = lax.dot_general(kw, qm, (((1,), (1,)), ((), ())), preferred_element_type=F32)
        return s + bias_ref[0, r - r_start]

    vaug_ref[:, :LANES] = v_ref[...]
    vaug_ref[:, LANES:] = jnp.ones((v_ref.shape[0], LANES), BF16)

    def finish(r, s):
        vw = vaug_ref[pl.ds(pl.multiple_of(window_start(r) * GRID_W, GRID_W), win), :]
        m = _col_reduce(s, jnp.max)
        p = jnp.exp2(s - m).astype(BF16)
        o2 = lax.dot_general(p, vw, (((0,), (0,)), ((), ())), preferred_element_type=F32)
        o2 = o2[:, :LANES] * (1.0 / o2[:, LANES:])
        o = jnp.where(head0, o2[:GRID_W], o2[GRID_W:])
        o_ref[pl.ds(pl.multiple_of(r * GRID_W, GRID_W), GRID_W), :] = o.astype(BF16)

    def body(i, carry):
        trip_rows = [i * NA_ROWS_PER_TRIP + u for u in range(NA_ROWS_PER_TRIP)]
        trip_scores = [scores(r) for r in trip_rows]
        for r, s in zip(trip_rows, trip_scores):
            finish(r, s)
        return carry

    lax.fori_loop(0, rows // NA_ROWS_PER_TRIP, body, 0)


def _na(qkv, bias_t, batch, seq):
    rows = seq // GRID_W
    n_pairs = NA_HEADS // 2
    return pl.pallas_call(
        functools.partial(_na_kernel, rows=rows),
        grid=(batch, n_pairs),
        in_specs=[pl.BlockSpec((seq, LANES), lambda b, hp: (b, hp)),
                  pl.BlockSpec((seq, LANES), lambda b, hp: (b, n_pairs + hp)),
                  pl.BlockSpec((seq, LANES), lambda b, hp: (b, 2 * n_pairs + hp)),
                  pl.BlockSpec((1, NA_WIN_ROWS, NA_WIN_ROWS * GRID_W, LANES), lambda b, hp: (hp, 0, 0, 0))],
        out_specs=pl.BlockSpec((seq, LANES), lambda b, hp: (b, hp)),
        out_shape=jax.ShapeDtypeStruct((batch * seq, NA_WIDTH), BF16),
        scratch_shapes=[pltpu.VMEM((seq, 2 * LANES), BF16)],
        compiler_params=_params("arbitrary", "arbitrary"),
        name="na_attn",
    )(qkv, qkv, qkv, bias_t)


def _diff_kernel(q_ref, k_ref, v_ref, lam_ref, g_ref, o_ref, vt_ref, kn_ref, oa_ref):
    @pl.when(pl.program_id(2) == 0)
    def _():
        vt_ref[:LANES, :] = v_ref[...].astype(F32).T.astype(BF16)
        ones_row = lax.broadcasted_iota(jnp.int32, (VT_ROWS - LANES, v_ref.shape[0]), 0) == 0
        vt_ref[LANES:, :] = jnp.where(ones_row, 1.0, 0.0).astype(BF16)
        kf = k_ref[...].astype(F32)
        d_id = lax.broadcasted_iota(jnp.int32, (LANES, LANES), 0) // HEAD_DIM
        c_id = lax.broadcasted_iota(jnp.int32, (LANES, LANES), 1)
        comp_sel = jnp.where(d_id == c_id, 1.0, 0.0).astype(BF16)
        kn2 = jnp.dot((kf * kf).astype(BF16), comp_sel, preferred_element_type=F32)
        kn_ref[...] = jnp.sqrt(_col_reduce(kn2, jnp.max)) * NORM_SLACK

    lq1 = lam_ref[0:1, :]
    lk1 = lam_ref[1:2, :]
    lq2 = lam_ref[2:3, :]
    lk2 = lam_ref[3:4, :]
    lam = (jnp.exp(jnp.sum(lq1 * lk1, axis=-1, keepdims=True))
           - jnp.exp(jnp.sum(lq2 * lk2, axis=-1, keepdims=True)) + LAMBDA_INIT)

    q = q_ref[...]
    lane = lax.broadcasted_iota(jnp.int32, q.shape, 1)
    zero = jnp.zeros_like(q)
    tq = q.shape[0]
    qcat = jnp.concatenate([jnp.where(lane < HEAD_DIM, q, zero), jnp.where(lane < HEAD_DIM, zero, q)], axis=0)
    def scores(c, chunk):
        kc = k_ref[c * chunk:(c + 1) * chunk, :]
        return lax.dot_general(kc, qcat, (((1,), (1,)), ((), ())), preferred_element_type=F32)

    def chunked(chunk, step):
        n_chunks = k_ref.shape[0] // chunk
        s_next = scores(0, chunk)
        state = None
        for c in range(n_chunks):
            s = s_next
            if c + 1 < n_chunks:
                s_next = scores(c + 1, chunk)
            state = step(c, s, vt_ref[:, c * chunk:(c + 1) * chunk], state)
        return state

    qf = qcat.astype(F32)
    ones8 = jnp.ones((8, LANES), BF16)
    qn2 = lax.dot_general(ones8, (qf * qf).astype(BF16), (((1,), (1,)), ((), ())), preferred_element_type=F32)
    col = lax.broadcasted_iota(jnp.int32, (1, 2 * tq), 1)
    bound = jnp.sqrt(qn2[0:1, :]) * NORM_SLACK * jnp.where(col < tq, kn_ref[0:1, 0:1], kn_ref[0:1, 1:2])
    in_range = jnp.max(bound) <= MAX_SAFE_BOUND

    @pl.when(in_range)
    def _():
        def step(c, s, vtc, acc):
            pv = jnp.dot(vtc, jnp.exp2(s - bound).astype(BF16), preferred_element_type=F32)
            return pv if c == 0 else acc + pv
        oa_ref[...] = chunked(KV_CHUNK, step)

    @pl.when(jnp.logical_not(in_range))
    def _():
        def step(c, s, vtc, state):
            mc = _col_reduce(s, jnp.max)
            m_new = mc if c == 0 else jnp.maximum(state[0], mc)
            pv = jnp.dot(vtc, jnp.exp2(s - m_new).astype(BF16), preferred_element_type=F32)
            return (m_new, pv if c == 0 else jnp.exp2(state[0] - m_new) * state[1] + pv)
        oa_ref[...] = chunked(KV_CHUNK, step)[1]

    oa = oa_ref[...]
    o0, l0 = oa[:LANES, :tq], oa[LANES:LANES + 1, :tq]
    o1, l1 = oa[:LANES, tq:], oa[LANES:LANES + 1, tq:]
    o = o0 * (1.0 / l0) - (lam / l1) * o1
    ms = jnp.mean(o * o, axis=0, keepdims=True)
    y = o * lax.rsqrt(ms + EPS) * g_ref[...]
    o_ref[...] = y.T.astype(BF16)


def _diff(qkv, lam_vecs, subln_col, batch, seq):
    tq = Q_BLOCK
    nq = seq // tq
    base = 3 * NA_WIDTH // LANES
    nh = DIFF_HEADS
    return pl.pallas_call(
        _diff_kernel,
        grid=(batch, nh, nq),
        in_specs=[pl.BlockSpec((tq, LANES), lambda b, h, i: (b * nq + i, base + h)),
                  pl.BlockSpec((seq, LANES), lambda b, h, i: (b, base + nh + h)),
                  pl.BlockSpec((seq, LANES), lambda b, h, i: (b, base + 2 * nh + h)),
                  pl.BlockSpec((4, HEAD_DIM), lambda b, h, i: (0, 0)),
                  pl.BlockSpec((LANES, 1), lambda b, h, i: (0, 0))],
        out_specs=pl.BlockSpec((tq, LANES), lambda b, h, i: (b * nq + i, h)),
        out_shape=jax.ShapeDtypeStruct((batch * seq, DIFF_WIDTH), BF16),
        scratch_shapes=[pltpu.VMEM((VT_ROWS, seq), BF16),
                        pltpu.VMEM((1, LANES), F32),
                        pltpu.VMEM((VT_ROWS, 2 * tq), F32)],
        compiler_params=_params("arbitrary", "arbitrary", "arbitrary"),
        name="diff_attn",
    )(qkv, qkv, qkv, lam_vecs, subln_col)


def _wo_kernel(ona_ref, odf_ref, x_ref, ada_ref, wo_ref, g_ref, wrh_ref, wrl_ref, br_ref, tri_ref, cnt0_ref,
               x1_ref, h2_ref, ti_ref, tw_ref, cnt_ref):
    gt1 = ada_ref[0, 2:3, :]
    sh2 = ada_ref[0, 3:4, :]
    sc2 = ada_ref[0, 4:5, :]
    lane = lax.broadcasted_iota(jnp.int32, (WO_SUB, LANES), 1).astype(F32)

    def mix_of(rows):
        return (jnp.dot(ona_ref[rows, :], wo_ref[:NA_WIDTH, :], preferred_element_type=F32)
                + jnp.dot(odf_ref[rows, :], wo_ref[NA_WIDTH:, :], preferred_element_type=F32))

    def route(rows, mix):
        x1 = x_ref[rows, :] + gt1 * mix
        x1_ref[rows, :] = x1
        ms = jnp.mean(x1 * x1, axis=-1, keepdims=True)
        h2 = x1 * lax.rsqrt(ms + EPS) * g_ref[...] * (1.0 + sc2) + sh2
        hi = h2.astype(BF16)
        h2_ref[rows, :] = _pack_halves(hi.astype(F32))
        lo = (h2 - hi.astype(F32)).astype(BF16)
        both = jnp.dot(hi, wrl_ref[...], preferred_element_type=F32)
        cur = (both + pltpu.roll(both, LANES - N_EXPERTS, axis=1)
               + jnp.dot(lo, wrh_ref[...], preferred_element_type=F32)) + br_ref[...]
        vals = []
        idxs = []
        for _ in range(TOP_K):
            m = jnp.max(cur, axis=-1, keepdims=True)
            idx = jnp.min(jnp.where(cur == m, lane, float(LANES)), axis=-1, keepdims=True)
            vals.append(m)
            idxs.append(idx)
            cur = jnp.where(lane == idx, -jnp.inf, cur)
        es = [jnp.exp(v - vals[0]) for v in vals]
        inv = 1.0 / (es[0] + es[1] + es[2] + es[3])
        sel = jnp.zeros((WO_SUB, LANES), F32)
        for j in range(TOP_K):
            sel = sel + jnp.where(lane == idxs[j], 1.0, 0.0)
        return idxs, [e * inv for e in es], sel

    subs = [pl.ds(r, WO_SUB) for r in range(0, x_ref.shape[0], WO_SUB)]
    routed = []
    mix_next = mix_of(subs[0])
    for j, rows in enumerate(subs):
        mix = mix_next
        if j + 1 < len(subs):
            mix_next = mix_of(subs[j + 1])
        routed.append(route(rows, mix))

    @pl.when(pl.program_id(0) == 0)
    def _():
        cnt_ref[...] = cnt0_ref[...]

    sel_all = jnp.concatenate([r[2] for r in routed], axis=0)
    before = jnp.dot(tri_ref[...], sel_all.astype(BF16), preferred_element_type=F32) + cnt_ref[...]
    cnt_ref[...] = cnt_ref[...] + jnp.sum(sel_all, axis=0, keepdims=True)

    for s, rows in enumerate(subs):
        idxs, wts, _ = routed[s]
        bef = before[s * WO_SUB:(s + 1) * WO_SUB]
        ti = jnp.zeros((WO_SUB, LANES), F32)
        tw = jnp.zeros((WO_SUB, LANES), F32)
        for j in range(TOP_K):
            rank = jnp.sum(jnp.where(lane == idxs[j], bef, 0.0), axis=-1, keepdims=True)
            ti = jnp.where(lane == float(j), idxs[j], ti)
            ti = jnp.where(lane == float(TOP_K + j), rank, ti)
            tw = jnp.where(lane == float(j), wts[j], tw)
        ti_ref[rows, :] = ti.astype(jnp.int32)
        tw_ref[rows, :] = tw


def _wo(o_na, o_df, x2d, ada_g, w_o_bf, g_ffn, wr_hi, wr_lo, br_pad, tri, cnt0, seq):
    n = x2d.shape[0]
    tm = WO_BLOCK
    per_seq = seq // tm
    row = lambda i: (i, 0)
    const = lambda i: (0, 0)
    return pl.pallas_call(
        _wo_kernel,
        grid=(n // tm,),
        in_specs=[pl.BlockSpec((tm, NA_WIDTH), row),
                  pl.BlockSpec((tm, DIFF_WIDTH), row),
                  pl.BlockSpec((tm, D_MODEL), row),
                  pl.BlockSpec((1, 6, D_MODEL), lambda i: (i // per_seq, 0, 0)),
                  pl.BlockSpec((D_MODEL, D_MODEL), const),
                  pl.BlockSpec((1, D_MODEL), const),
                  pl.BlockSpec((D_MODEL, LANES), const),
                  pl.BlockSpec((D_MODEL, LANES), const),
                  pl.BlockSpec((1, LANES), const),
                  pl.BlockSpec((tm, tm), const),
                  pl.BlockSpec((1, LANES), const)],
        out_specs=[pl.BlockSpec((tm, D_MODEL), row),
                   pl.BlockSpec((tm, PACKED), row),
                   pl.BlockSpec((tm, LANES), row),
                   pl.BlockSpec((tm, LANES), row),
                   pl.BlockSpec((1, LANES), const)],
        out_shape=[jax.ShapeDtypeStruct((n, D_MODEL), F32),
                   jax.ShapeDtypeStruct((n, PACKED), U32),
                   jax.ShapeDtypeStruct((n, LANES), jnp.int32),
                   jax.ShapeDtypeStruct((n, LANES), F32),
                   jax.ShapeDtypeStruct((1, LANES), F32)],
        compiler_params=_params("arbitrary"),
        name="wo_router",
    )(o_na, o_df, x2d, ada_g, w_o_bf, g_ffn, wr_hi, wr_lo, br_pad, tri, cnt0)


def _expert_kernel(be_ref, nxt_ref, na_ref, xs_ref, wg_hbm, bg_ref, wu_hbm, bu_ref, wd_hbm, bd_ref, o_ref,
                   w_f32, w_bf, sems, slot_ref):
    i = pl.program_id(0)
    active = i < na_ref[0]
    expert = be_ref[i]
    new_expert = jnp.logical_or(i == 0, expert != be_ref[jnp.maximum(i - 1, 0)])
    wg_bf, wu_bf, wd_bf = w_bf.at[0], w_bf.at[1], w_bf.at[2]

    def weight_copies(src_expert, slot):
        return [pltpu.make_async_copy(w_hbm.at[src_expert], w_f32.at[slot, j], sems.at[slot, j])
                for j, w_hbm in enumerate((wg_hbm, wu_hbm, wd_hbm))]

    @pl.when(i == 0)
    def _():
        slot_ref[0] = 0
        for cp in weight_copies(expert, 0):
            cp.start()

    @pl.when(jnp.logical_and(active, new_expert))
    def _():
        slot = slot_ref[0]
        nxt = nxt_ref[i]
        for s in range(2):
            @pl.when(slot == s)
            def _():
                @pl.when(nxt >= 0)
                def _():
                    for cp in weight_copies(nxt, 1 - s):
                        cp.start(priority=1)
                for j, cp in enumerate(weight_copies(expert, s)):
                    cp.wait()
                    w_bf[j] = w_f32[s, j].astype(BF16)
        slot_ref[0] = 1 - slot

    @pl.when(active)
    def _():
        def gate_up(rows):
            x_lo, x_hi = _unpack_halves(xs_ref[rows, :])
            x_lo = x_lo.astype(BF16)
            x_hi = x_hi.astype(BF16)

            def proj(w_bf):
                return (jnp.dot(x_lo, w_bf[:PACKED, :], preferred_element_type=F32)
                        + jnp.dot(x_hi, w_bf[PACKED:, :], preferred_element_type=F32))

            return proj(wg_bf), proj(wu_bf)

        def act_down(rows, gu):
            g = jnp.minimum(gu[0] + bg_ref[0], SWIGLU_LIMIT)
            u = jnp.clip(gu[1] + bu_ref[0], -SWIGLU_LIMIT, SWIGLU_LIMIT)
            act = g * jax.nn.sigmoid(SWIGLU_ALPHA * g) * (u + 1.0)
            out = jnp.dot(act.astype(BF16), wd_bf[...], preferred_element_type=F32) + bd_ref[0]
            o_ref[rows, :] = _pack_halves(out.astype(BF16).astype(F32))

        sub = [pl.ds(r, EXPERT_SUB) for r in range(0, EXPERT_BLOCK, EXPERT_SUB)]
        gu_next = gate_up(sub[0])
        for j, rows in enumerate(sub):
            gu = gu_next
            if j + 1 < len(sub):
                gu_next = gate_up(sub[j + 1])
            act_down(rows, gu)

    @pl.when(i >= na_ref[0])
    def _():
        o_ref[...] = jnp.zeros_like(o_ref)


def _experts(block_e, next_e, n_active, xs, wg, bg, wu, bu, wd, bd):
    cap = xs.shape[0]
    n_blocks = cap // EXPERT_BLOCK
    xmap = lambda i, be, nx, na: (jnp.minimum(i, na[0] - 1), 0)
    bmap = lambda i, be, nx, na: (be[i], 0, 0)
    hbm = pl.BlockSpec(memory_space=pl.ANY)
    grid_spec = pltpu.PrefetchScalarGridSpec(
        num_scalar_prefetch=3,
        grid=(n_blocks,),
        in_specs=[pl.BlockSpec((EXPERT_BLOCK, PACKED), xmap),
                  hbm, pl.BlockSpec((1, 1, D_MODEL), bmap),
                  hbm, pl.BlockSpec((1, 1, D_MODEL), bmap),
                  hbm, pl.BlockSpec((1, 1, D_MODEL), bmap)],
        out_specs=pl.BlockSpec((EXPERT_BLOCK, PACKED), lambda i, be, nx, na: (i, 0)),
        scratch_shapes=[pltpu.VMEM((2, 3, D_MODEL, D_MODEL), F32),
                        pltpu.VMEM((3, D_MODEL, D_MODEL), BF16),
                        pltpu.SemaphoreType.DMA((2, 3)),
                        pltpu.SMEM((1,), jnp.int32)],
    )
    return pl.pallas_call(
        _expert_kernel,
        grid_spec=grid_spec,
        out_shape=jax.ShapeDtypeStruct((cap, PACKED), U32),
        compiler_params=_params("arbitrary"),
        name="experts",
    )(block_e, next_e, n_active, xs, wg, bg, wu, bu, wd, bd)


def _sc_gather(table, idx):
    n_out = idx.shape[0]
    width = table.shape[1]
    per_worker = n_out // SC_WORKERS
    n_chunks = per_worker // GATHER_ROWS
    assert per_worker * SC_WORKERS == n_out and n_chunks * GATHER_ROWS == per_worker and n_chunks % 2 == 0
    idx3 = idx.reshape(SC_WORKERS, n_chunks, GATHER_ROWS)
    mesh = plsc.VectorSubcoreMesh(core_axis_name="core", subcore_axis_name="subcore")

    @functools.partial(
        pl.kernel, mesh=mesh,
        out_type=jax.ShapeDtypeStruct((n_out, width), table.dtype),
        scratch_types=[pltpu.VMEM((n_chunks, GATHER_ROWS), jnp.int32),
                       pltpu.VMEM((2, GATHER_ROWS, width), table.dtype),
                       pltpu.SemaphoreType.DMA((2,)),
                       pltpu.SemaphoreType.DMA((2,))])
    def gather_kernel(table_hbm, idx_hbm, out_hbm, idx_v, rows_v, gsem, wsem):
        wid = lax.axis_index("subcore") * SC_CORES + lax.axis_index("core")
        base = wid * per_worker
        pltpu.sync_copy(idx_hbm.at[wid], idx_v)

        def gather(j, slot):
            return pltpu.make_async_copy(table_hbm.at[idx_v.at[j]], rows_v.at[slot], gsem.at[slot])

        def write(j, slot):
            dst = out_hbm.at[pl.ds(pl.multiple_of(base + j * GATHER_ROWS, GATHER_ROWS), GATHER_ROWS)]
            return pltpu.make_async_copy(rows_v.at[slot], dst, wsem.at[slot])

        gather(0, 0).start()

        @pl.loop(0, n_chunks, step=2)
        def _(j):
            for slot in range(2):
                jj = j + slot
                gather(jj, slot).wait()

                @pl.when(jj >= 1)
                def _():
                    write(jj - 1, 1 - slot).wait()

                @pl.when(jj + 1 < n_chunks)
                def _():
                    gather(jj + 1, 1 - slot).start()

                write(jj, slot).start()

        write(n_chunks - 1, 1).wait()

    return gather_kernel(table, idx3)


def _sc_scatter(rows, idx, n_out):
    n_src, width = rows.shape
    n_idx = idx.shape[0]
    per_worker = n_idx // SC_WORKERS
    n_chunks = per_worker // GATHER_ROWS
    assert per_worker * SC_WORKERS == n_idx and n_chunks * GATHER_ROWS == per_worker and n_chunks % 2 == 0
    assert n_src % per_worker == 0
    idx3 = idx.reshape(SC_WORKERS, n_chunks, GATHER_ROWS)
    mesh = plsc.VectorSubcoreMesh(core_axis_name="core", subcore_axis_name="subcore")

    @functools.partial(
        pl.kernel, mesh=mesh,
        out_type=jax.ShapeDtypeStruct((n_out, width), rows.dtype),
        scratch_types=[pltpu.VMEM((n_chunks, GATHER_ROWS), jnp.int32),
                       pltpu.VMEM((2, GATHER_ROWS, width), rows.dtype),
                       pltpu.SemaphoreType.DMA((2,)),
                       pltpu.SemaphoreType.DMA((2,))])
    def scatter_kernel(rows_hbm, idx_hbm, out_hbm, idx_v, rows_v, rsem, wsem):
        wid = lax.axis_index("subcore") * SC_CORES + lax.axis_index("core")
        base = lax.rem(wid * per_worker, n_src)
        pltpu.sync_copy(idx_hbm.at[wid], idx_v)

        def read(j, slot):
            src = rows_hbm.at[pl.ds(pl.multiple_of(base + j * GATHER_ROWS, GATHER_ROWS), GATHER_ROWS)]
            return pltpu.make_async_copy(src, rows_v.at[slot], rsem.at[slot])

        def write(j, slot):
            return pltpu.make_async_copy(rows_v.at[slot], out_hbm.at[idx_v.at[j]], wsem.at[slot])

        read(0, 0).start()

        @pl.loop(0, n_chunks, step=2)
        def _(j):
            for slot in range(2):
                jj = j + slot
                read(jj, slot).wait()

                @pl.when(jj >= 1)
                def _():
                    write(jj - 1, 1 - slot).wait()

                @pl.when(jj + 1 < n_chunks)
                def _():
                    read(jj + 1, 1 - slot).start()

                write(jj, slot).start()

        write(n_chunks - 1, 1).wait()

    return scatter_kernel(rows, idx3)


def _combine_kernel(x1_ref, y0_ref, y1_ref, y2_ref, y3_ref, tw_ref, ada_ref, o_ref):
    tw = tw_ref[...]
    acc_lo = jnp.zeros((x1_ref.shape[0], PACKED), F32)
    acc_hi = jnp.zeros((x1_ref.shape[0], PACKED), F32)
    for j, y_ref in enumerate((y0_ref, y1_ref, y2_ref, y3_ref)):
        lo, hi = _unpack_halves(y_ref[...])
        acc_lo = acc_lo + tw[:, j:j + 1] * lo
        acc_hi = acc_hi + tw[:, j:j + 1] * hi
    o_ref[:, :PACKED] = x1_ref[:, :PACKED] + ada_ref[0, 5:6, :PACKED] * acc_lo
    o_ref[:, PACKED:] = x1_ref[:, PACKED:] + ada_ref[0, 5:6, PACKED:] * acc_hi


def _combine(x1, ys, tw, ada_g, seq, row_off):
    n = x1.shape[0]
    tm = ROW_BLOCK
    per_seq = seq // tm
    off = row_off // tm
    per_choice = tw.shape[0] // tm
    y_specs = [pl.BlockSpec((tm, PACKED), functools.partial(lambda i, j: (j * per_choice + off + i, 0), j=j))
               for j in range(TOP_K)]
    return pl.pallas_call(
        _combine_kernel,
        grid=(n // tm,),
        in_specs=[pl.BlockSpec((tm, D_MODEL), lambda i: (i, 0)),
                  *y_specs,
                  pl.BlockSpec((tm, LANES), lambda i: (i + off, 0)),
                  pl.BlockSpec((1, 6, D_MODEL), lambda i: (i // per_seq, 0, 0))],
        out_specs=pl.BlockSpec((tm, D_MODEL), lambda i: (i, 0)),
        out_shape=jax.ShapeDtypeStruct((n, D_MODEL), F32),
        compiler_params=_params("arbitrary"),
        name="combine",
    )(x1, ys, ys, ys, ys, tw, ada_g)


def _rope_tables(seq):
    half = HEAD_DIM // 2
    inv = ROPE_THETA ** (-jnp.arange(half, dtype=F32) / half)
    ang = jnp.arange(seq, dtype=F32)[:, None] * inv[None, :]
    cos, sin = jnp.cos(ang), jnp.sin(ang)
    cos_h = jnp.concatenate([cos, cos], axis=-1)
    sin_h = jnp.concatenate([-sin, sin], axis=-1)
    reps = LANES // HEAD_DIM
    return jnp.tile(cos_h, (1, reps)), jnp.tile(sin_h, (1, reps))


def _na_bias_table(rpb):
    cols = jnp.arange(GRID_W, dtype=jnp.int32)
    c_start = jnp.clip(cols - NA_WIN_COLS // 2, 0, GRID_W - NA_WIN_COLS)
    col_mask = (cols[None, :] >= c_start[:, None]) & (cols[None, :] < c_start[:, None] + NA_WIN_COLS)
    col_idx = jnp.clip(cols[None, :] - cols[:, None], -(NA_WIN_COLS - 1), NA_WIN_COLS - 1) + NA_WIN_COLS - 1
    delta = jnp.arange(NA_WIN_ROWS, dtype=jnp.int32)
    j = jnp.arange(NA_WIN_ROWS, dtype=jnp.int32)
    row_idx = j[None, :] - delta[:, None] + NA_WIN_ROWS - 1
    row_hot = (row_idx[:, :, None] == jnp.arange(2 * NA_WIN_ROWS - 1, dtype=jnp.int32)).astype(F32)
    col_hot = (col_idx[:, :, None] == jnp.arange(2 * NA_WIN_COLS - 1, dtype=jnp.int32)).astype(F32)
    rpb_pairs = rpb.astype(F32).reshape(NA_HEADS // 2, 2, 2 * NA_WIN_ROWS - 1, 2 * NA_WIN_COLS - 1)
    bias = jnp.einsum('djr,phrc,qkc->pdjkhq', row_hot, rpb_pairs, col_hot,
                      precision=lax.Precision.HIGHEST)
    bias = jnp.where(col_mask.T[None, None, None, :, None, :], bias * LOG2E, NEG_INF)
    return bias.reshape(NA_HEADS // 2, NA_WIN_ROWS, NA_WIN_ROWS * GRID_W, LANES)


def _routing(top_idx, rank, counts, n):
    n_blocks = n * TOP_K // EXPERT_BLOCK + N_EXPERTS
    experts = jnp.arange(N_EXPERTS, dtype=jnp.int32)
    padded = (counts + EXPERT_BLOCK - 1) // EXPERT_BLOCK * EXPERT_BLOCK
    pad_end = jnp.cumsum(padded)
    pad_start = pad_end - padded
    start_of = jnp.sum(jnp.where(top_idx[:, :, None] == experts, pad_start, 0), axis=-1)
    dest = (start_of + rank).T.reshape(-1)
    block_lo = jnp.arange(n_blocks, dtype=jnp.int32) * EXPERT_BLOCK
    block_e = jnp.minimum(jnp.sum((pad_end[None, :] <= block_lo[:, None]).astype(jnp.int32), axis=1),
                          N_EXPERTS - 1).astype(jnp.int32)
    n_active = (pad_end[-1] // EXPERT_BLOCK).astype(jnp.int32).reshape(1)
    later = jnp.where((experts[None, :] > experts[:, None]) & (padded[None, :] > 0), experts[None, :], N_EXPERTS)
    next_nonempty = jnp.min(later, axis=1)
    next_nonempty = jnp.where(next_nonempty == N_EXPERTS, -1, next_nonempty)
    next_e = jnp.sum(jnp.where(block_e[:, None] == experts[None, :], next_nonempty[None, :], 0), axis=1)
    return dest, block_e, next_e.astype(jnp.int32), n_active, n_blocks * EXPERT_BLOCK


def kernel(x_prompt, x_sample, c_prompt, c_sample, w_ada, b_ada, g_attn_norm, w_qkv, na_q_norm, na_k_norm, na_rpb, diff_q_norm, diff_k_norm, lambda_q1, lambda_k1, lambda_q2, lambda_k2, diff_subln, w_o, g_ffn_norm, w_router, b_router, w_gate, b_gate, w_up, b_up, w_down, b_down):
    l = 0
    groups = [(x_prompt, c_prompt), (x_sample, c_sample)]
    nb = [x.shape[0] for x, _ in groups]

    ada_all = _ada(jnp.concatenate([c for _, c in groups], axis=0), w_ada[l], b_ada[l])
    ada_all = ada_all.reshape(sum(nb), 6, D_MODEL)

    w_qkv_bf = w_qkv[l].astype(BF16)
    w_o_bf = w_o[l].astype(BF16)
    scale = HEAD_DIM ** -0.5
    reps = NA_WIDTH // HEAD_DIM
    gains = jnp.stack([jnp.tile(na_q_norm[l], reps) * (scale * LOG2E),
                       jnp.tile(na_k_norm[l], reps),
                       jnp.tile(diff_q_norm[l], reps) * (scale * LOG2E),
                       jnp.tile(diff_k_norm[l], reps)]).astype(F32)
    head_id = jnp.arange(MXU_DIM, dtype=jnp.int32) // HEAD_DIM
    bd = (head_id[:, None] == head_id[None, :]).astype(BF16)
    bias_t = _na_bias_table(na_rpb[l])
    lam_vecs = jnp.stack([lambda_q1[l], lambda_k1[l], lambda_q2[l], lambda_k2[l]]).astype(F32)
    subln_col = (diff_subln[l].astype(F32) * (1.0 - LAMBDA_INIT)).reshape(LANES, 1)
    wr = w_router[l].astype(F32)
    wr_pad = jnp.zeros((D_MODEL, LANES), F32).at[:, :N_EXPERTS].set(wr)
    wr_hi = wr_pad.astype(BF16)
    wr_lo = (wr_pad - wr_hi.astype(F32)).astype(BF16)
    wr_lo = wr_hi.at[:, N_EXPERTS:2 * N_EXPERTS].set(wr_lo[:, :N_EXPERTS])
    br_pad = jnp.full((1, LANES), NEG_INF, F32).at[0, :N_EXPERTS].set(b_router[l].astype(F32))
    g_attn = g_attn_norm[l].reshape(1, D_MODEL).astype(F32)
    g_ffn = g_ffn_norm[l].reshape(1, D_MODEL).astype(F32)
    max_seq = max(x.shape[1] for x, _ in groups)
    cos_t, sin_t = _rope_tables(max_seq)

    rows = lax.broadcasted_iota(jnp.int32, (WO_BLOCK, WO_BLOCK), 0)
    cols = lax.broadcasted_iota(jnp.int32, (WO_BLOCK, WO_BLOCK), 1)
    tri = (cols < rows).astype(BF16)
    cnt0 = jnp.zeros((1, LANES), F32)

    bg = b_gate[l].reshape(N_EXPERTS, 1, D_MODEL).astype(F32)
    bu = b_up[l].reshape(N_EXPERTS, 1, D_MODEL).astype(F32)
    bdn = b_down[l].reshape(N_EXPERTS, 1, D_MODEL).astype(F32)
    ada_groups = [ada_all[:nb[0]], ada_all[nb[0]:]]

    order = sorted(range(len(groups)), key=lambda g: -groups[g][0].shape[1])
    staged = {}
    for g in order:
        x = groups[g][0]
        b, seq = x.shape[0], x.shape[1]
        n = b * seq
        x2d = x.reshape(n, D_MODEL)
        ada_g = ada_groups[g]
        qkv = _qkv(x2d, ada_g, g_attn, w_qkv_bf, gains, cos_t, sin_t, bd, seq)
        o_na = _na(qkv, bias_t, b, seq)
        o_df = _diff(qkv, lam_vecs, subln_col, b, seq)
        x1, h2, ti, tw, cnt = _wo(o_na, o_df, x2d, ada_g, w_o_bf, g_ffn, wr_hi, wr_lo, br_pad, tri, cnt0, seq)
        counts = cnt[0, :N_EXPERTS].astype(jnp.int32)
        dest, block_e, next_e, n_active, cap = _routing(ti[:, :TOP_K], ti[:, TOP_K:2 * TOP_K], counts, n)
        xs = _sc_scatter(h2, dest, cap)
        staged[g] = (x1, tw, ada_g, dest, block_e, next_e, n_active, xs, seq, b)

    sorted_out = {}
    for g in order:
        x1, tw, ada_g, dest, block_e, next_e, n_active, xs, seq, b = staged[g]
        sorted_out[g] = _experts(block_e, next_e, n_active, xs, w_gate[l], bg, w_up[l], bu, w_down[l], bdn)

    outs = [None] * len(groups)
    for g in order:
        x1, tw, ada_g, dest, block_e, next_e, n_active, xs, seq, b = staged[g]
        ys = _sc_gather(sorted_out[g], dest)
        outs[g] = _combine(x1, ys, tw, ada_g, seq, 0).reshape(b, seq, D_MODEL)
    return tuple(outs)
```

```python
import functools
import math

import jax
import jax.numpy as jnp
from jax import lax
from jax.experimental import pallas as pl
from jax.experimental.pallas import tpu as pltpu
from jax.experimental.pallas import tpu_sc as plsc

F32 = jnp.float32
BF16 = jnp.bfloat16
U32 = jnp.uint32

D_MODEL = 1024
HEAD_DIM = 64
NA_HEADS = 8
NA_WIDTH = 512
DIFF_HEADS = 4
DIFF_WIDTH = 512
QKV_COLS = 3072
GRID_W = 64
NA_WIN_ROWS = 8
NA_WIN_COLS = 16
ROPE_THETA = 10000.0
N_EXPERTS = 32
TOP_K = 4
SWIGLU_LIMIT = 7.0
SWIGLU_ALPHA = 1.702
EPS = 1e-5
NEG_INF = -1e30
LAMBDA_INIT = 0.8 - 0.6 * math.exp(-0.3 * 0)
LOG2E = 1.4426950408889634

LANES = 128
MXU_DIM = 256
VMEM_LIMIT = 56 * 1024 * 1024

ROW_BLOCK = 512
Q_BLOCK = 1024
EXPERT_BLOCK = 1024
EXPERT_HALF = 512
EXPERT_SUB = 256
WO_BLOCK = 1024
WO_SUB = 256
NA_ROWS_PER_TRIP = 16
VT_ROWS = LANES + 16
KV_CHUNK = 256
NORM_SLACK = 1.01
MAX_SAFE_BOUND = 60.0


PACKED = D_MODEL // 2
SC_CORES = 2
SC_SUBCORES = 16
SC_WORKERS = SC_CORES * SC_SUBCORES
GATHER_ROWS = 64


def _params(*sem):
    return pltpu.CompilerParams(dimension_semantics=sem, vmem_limit_bytes=VMEM_LIMIT)


def _pack_halves(x):
    w = x.shape[1] // 2
    bits = lax.bitcast_convert_type(x, U32)
    return (bits[:, :w] >> 16) | bits[:, w:]


def _col_reduce(x, op):
    while x.shape[0] >= 64:
        x = op(x.reshape(8, x.shape[0] // 8, x.shape[1]), axis=0)
    return op(x, axis=0, keepdims=True)


def _unpack_halves(word):
    lo = lax.bitcast_convert_type(word << 16, F32)
    hi = lax.bitcast_convert_type(word & jnp.uint32(0xFFFF0000), F32)
    return lo, hi


def _ada_kernel(c_ref, w_ref, b_ref, o_ref):
    c = c_ref[...]
    s = c * jax.nn.sigmoid(c)
    o_ref[...] = jnp.dot(s, w_ref[...], preferred_element_type=F32,
                         precision=lax.Precision.HIGHEST) + b_ref[...]


def _ada(c_all, w_ada, b_ada):
    nb = c_all.shape[0]
    n_out = w_ada.shape[1]
    blk = D_MODEL
    return pl.pallas_call(
        _ada_kernel,
        grid=(n_out // blk,),
        in_specs=[pl.BlockSpec((nb, D_MODEL), lambda j: (0, 0)),
                  pl.BlockSpec((D_MODEL, blk), lambda j: (0, j)),
                  pl.BlockSpec((1, blk), lambda j: (0, j))],
        out_specs=pl.BlockSpec((nb, blk), lambda j: (0, j)),
        out_shape=jax.ShapeDtypeStruct((nb, n_out), F32),
        compiler_params=_params("arbitrary"),
        name="ada",
    )(c_all, w_ada, b_ada.reshape(1, n_out))


def _head_sumsq(y, bd):
    sq = (y * y).astype(BF16)
    parts = [jnp.dot(sq[:, c:c + MXU_DIM], bd, preferred_element_type=F32)
             for c in range(0, y.shape[1], MXU_DIM)]
    return jnp.concatenate(parts, axis=1)


def _qkv_kernel(x_ref, ada_ref, g_ref, w_ref, gain_ref, cos_ref, sin_ref, bd_ref, o_ref):
    x = x_ref[...]
    ms = jnp.mean(x * x, axis=-1, keepdims=True)
    xn = x * lax.rsqrt(ms + EPS) * g_ref[...]
    sh = ada_ref[0, 0:1, :]
    sc = ada_ref[0, 1:2, :]
    h = (xn * (1.0 + sc) + sh).astype(BF16)
    bd = bd_ref[...]
    lane = lax.broadcasted_iota(jnp.int32, (x.shape[0], NA_WIDTH), 1)
    first_half = (lane & (HEAD_DIM // 2)) == 0
    for grp in range(6):
        cols = slice(grp * 512, (grp + 1) * 512)
        acc = jnp.dot(h, w_ref[:, cols], preferred_element_type=F32)
        if grp in (2, 5):
            o_ref[:, cols] = acc.astype(BF16)
            continue
        gi = {0: 0, 1: 1, 3: 2, 4: 3}[grp]
        ss = _head_sumsq(acc, bd)
        y = acc * lax.rsqrt(ss * (1.0 / HEAD_DIM) + EPS) * gain_ref[gi:gi + 1, :]
        if grp in (3, 4):
            partner = jnp.where(first_half,
                                pltpu.roll(y, NA_WIDTH - HEAD_DIM // 2, axis=1),
                                pltpu.roll(y, HEAD_DIM // 2, axis=1))
            reps = NA_WIDTH // LANES
            y = (y * jnp.concatenate([cos_ref[...]] * reps, axis=1)
                 + partner * jnp.concatenate([sin_ref[...]] * reps, axis=1))
        o_ref[:, cols] = y.astype(BF16)


def _qkv(x2d, ada_g, g_attn, w_qkv_bf, gains, cos_t, sin_t, bd, seq):
    n = x2d.shape[0]
    tm = ROW_BLOCK
    per_seq = seq // tm
    return pl.pallas_call(
        _qkv_kernel,
        grid=(n // tm,),
        in_specs=[pl.BlockSpec((tm, D_MODEL), lambda i: (i, 0)),
                  pl.BlockSpec((1, 6, D_MODEL), lambda i: (i // per_seq, 0, 0)),
                  pl.BlockSpec((1, D_MODEL), lambda i: (0, 0)),
                  pl.BlockSpec((D_MODEL, QKV_COLS), lambda i: (0, 0)),
                  pl.BlockSpec((4, NA_WIDTH), lambda i: (0, 0)),
                  pl.BlockSpec((tm, LANES), lambda i: (i % per_seq, 0)),
                  pl.BlockSpec((tm, LANES), lambda i: (i % per_seq, 0)),
                  pl.BlockSpec((MXU_DIM, MXU_DIM), lambda i: (0, 0))],
        out_specs=pl.BlockSpec((tm, QKV_COLS), lambda i: (i, 0)),
        out_shape=jax.ShapeDtypeStruct((n, QKV_COLS), BF16),
        compiler_params=_params("arbitrary"),
        name="qkv",
    )(x2d, ada_g, g_attn, w_qkv_bf, gains, cos_t, sin_t, bd)


def _na_kernel(q_ref, k_ref, v_ref, bias_ref, o_ref, vaug_ref, *, rows):
    lane = lax.broadcasted_iota(jnp.int32, (GRID_W, LANES), 1)
    head0 = lane < HEAD_DIM
    win = NA_WIN_ROWS * GRID_W

    def window_start(r):
        return jnp.clip(r - NA_WIN_ROWS // 2, 0, rows - NA_WIN_ROWS)

    def scores(r):
        r_start = window_start(r)
        q = q_ref[pl.ds(pl.multiple_of(r * GRID_W, GRID_W), GRID_W), :]
        kw = k_ref[pl.ds(pl.multiple_of(r_start * GRID_W, GRID_W), win), :]
        zero = jnp.zeros_like(q)
        qm = jnp.concatenate([jnp.where(head0, q, zero), jnp.where(head0, zero, q)], axis=0)
        s = lax.dot_general(kw, qm, (((1,), (1,)), ((), ())), preferred_element_type=F32)
        return s + bias_ref[0, r - r_start]

    vaug_ref[:, :LANES] = v_ref[...]
    vaug_ref[:, LANES:] = jnp.ones((v_ref.shape[0], LANES), BF16)

    def finish(r, s):
        vw = vaug_ref[pl.ds(pl.multiple_of(window_start(r) * GRID_W, GRID_W), win), :]
        m = _col_reduce(s, jnp.max)
        p = jnp.exp2(s - m).astype(BF16)
        o2 = lax.dot_general(p, vw, (((0,), (0,)), ((), ())), preferred_element_type=F32)
        o2 = o2[:, :LANES] * (1.0 / o2[:, LANES:])
        o = jnp.where(head0, o2[:GRID_W], o2[GRID_W:])
        o_ref[pl.ds(pl.multiple_of(r * GRID_W, GRID_W), GRID_W), :] = o.astype(BF16)

    def body(i, carry):
        trip_rows = [i * NA_ROWS_PER_TRIP + u for u in range(NA_ROWS_PER_TRIP)]
        trip_scores = [scores(r) for r in trip_rows]
        for r, s in zip(trip_rows, trip_scores):
            finish(r, s)
        return carry

    lax.fori_loop(0, rows // NA_ROWS_PER_TRIP, body, 0)


def _na(qkv, bias_t, batch, seq):
    rows = seq // GRID_W
    n_pairs = NA_HEADS // 2
    return pl.pallas_call(
        functools.partial(_na_kernel, rows=rows),
        grid=(batch, n_pairs),
        in_specs=[pl.BlockSpec((seq, LANES), lambda b, hp: (b, hp)),
                  pl.BlockSpec((seq, LANES), lambda b, hp: (b, n_pairs + hp)),
                  pl.BlockSpec((seq, LANES), lambda b, hp: (b, 2 * n_pairs + hp)),
                  pl.BlockSpec((1, NA_WIN_ROWS, NA_WIN_ROWS * GRID_W, LANES), lambda b, hp: (hp, 0, 0, 0))],
        out_specs=pl.BlockSpec((seq, LANES), lambda b, hp: (b, hp)),
        out_shape=jax.ShapeDtypeStruct((batch * seq, NA_WIDTH), BF16),
        scratch_shapes=[pltpu.VMEM((seq, 2 * LANES), BF16)],
        compiler_params=_params("arbitrary", "arbitrary"),
        name="na_attn",
    )(qkv, qkv, qkv, bias_t)


def _diff_kernel(q_ref, k_ref, v_ref, lam_ref, g_ref, o_ref, vt_ref, kn_ref, oa_ref):
    @pl.when(pl.program_id(2) == 0)
    def _():
        vt_ref[:LANES, :] = v_ref[...].astype(F32).T.astype(BF16)
        ones_row = lax.broadcasted_iota(jnp.int32, (VT_ROWS - LANES, v_ref.shape[0]), 0) == 0
        vt_ref[LANES:, :] = jnp.where(ones_row, 1.0, 0.0).astype(BF16)
        kf = k_ref[...].astype(F32)
        d_id = lax.broadcasted_iota(jnp.int32, (LANES, LANES), 0) // HEAD_DIM
        c_id = lax.broadcasted_iota(jnp.int32, (LANES, LANES), 1)
        comp_sel = jnp.where(d_id == c_id, 1.0, 0.0).astype(BF16)
        kn2 = jnp.dot((kf * kf).astype(BF16), comp_sel, preferred_element_type=F32)
        kn_ref[...] = jnp.sqrt(_col_reduce(kn2, jnp.max)) * NORM_SLACK

    lq1 = lam_ref[0:1, :]
    lk1 = lam_ref[1:2, :]
    lq2 = lam_ref[2:3, :]
    lk2 = lam_ref[3:4, :]
    lam = (jnp.exp(jnp.sum(lq1 * lk1, axis=-1, keepdims=True))
           - jnp.exp(jnp.sum(lq2 * lk2, axis=-1, keepdims=True)) + LAMBDA_INIT)

    q = q_ref[...]
    lane = lax.broadcasted_iota(jnp.int32, q.shape, 1)
    zero = jnp.zeros_like(q)
    tq = q.shape[0]
    qcat = jnp.concatenate([jnp.where(lane < HEAD_DIM, q, zero), jnp.where(lane < HEAD_DIM, zero, q)], axis=0)
    def scores(c, chunk):
        kc = k_ref[c * chunk:(c + 1) * chunk, :]
        return lax.dot_general(kc, qcat, (((1,), (1,)), ((), ())), preferred_element_type=F32)

    def chunked(chunk, step):
        n_chunks = k_ref.shape[0] // chunk
        s_next = scores(0, chunk)
        state = None
        for c in range(n_chunks):
            s = s_next
            if c + 1 < n_chunks:
                s_next = scores(c + 1, chunk)
            state = step(c, s, vt_ref[:, c * chunk:(c + 1) * chunk], state)
        return state

    qf = qcat.astype(F32)
    ones8 = jnp.ones((8, LANES), BF16)
    qn2 = lax.dot_general(ones8, (qf * qf).astype(BF16), (((1,), (1,)), ((), ())), preferred_element_type=F32)
    col = lax.broadcasted_iota(jnp.int32, (1, 2 * tq), 1)
    bound = jnp.sqrt(qn2[0:1, :]) * NORM_SLACK * jnp.where(col < tq, kn_ref[0:1, 0:1], kn_ref[0:1, 1:2])
    in_range = jnp.max(bound) <= MAX_SAFE_BOUND

    @pl.when(in_range)
    def _():
        def step(c, s, vtc, acc):
            pv = jnp.dot(vtc, jnp.exp2(s - bound).astype(BF16), preferred_element_type=F32)
            return pv if c == 0 else acc + pv
        oa_ref[...] = chunked(KV_CHUNK, step)

    @pl.when(jnp.logical_not(in_range))
    def _():
        def step(c, s, vtc, state):
            mc = _col_reduce(s, jnp.max)
            m_new = mc if c == 0 else jnp.maximum(state[0], mc)
            pv = jnp.dot(vtc, jnp.exp2(s - m_new).astype(BF16), preferred_element_type=F32)
            return (m_new, pv if c == 0 else jnp.exp2(state[0] - m_new) * state[1] + pv)
        oa_ref[...] = chunked(KV_CHUNK, step)[1]

    oa = oa_ref[...]
    o0, l0 = oa[:LANES, :tq], oa[LANES:LANES + 1, :tq]
    o1, l1 = oa[:LANES, tq:], oa[LANES:LANES + 1, tq:]
    o = o0 * (1.0 / l0) - (lam / l1) * o1
    ms = jnp.mean(o * o, axis=0, keepdims=True)
    y = o * lax.rsqrt(ms + EPS) * g_ref[...]
    o_ref[...] = y.T.astype(BF16)


def _diff(qkv, lam_vecs, subln_col, batch, seq):
    tq = Q_BLOCK
    nq = seq // tq
    base = 3 * NA_WIDTH // LANES
    nh = DIFF_HEADS
    return pl.pallas_call(
        _diff_kernel,
        grid=(batch, nh, nq),
        in_specs=[pl.BlockSpec((tq, LANES), lambda b, h, i: (b * nq + i, base + h)),
                  pl.BlockSpec((seq, LANES), lambda b, h, i: (b, base + nh + h)),
                  pl.BlockSpec((seq, LANES), lambda b, h, i: (b, base + 2 * nh + h)),
                  pl.BlockSpec((4, HEAD_DIM), lambda b, h, i: (0, 0)),
                  pl.BlockSpec((LANES, 1), lambda b, h, i: (0, 0))],
        out_specs=pl.BlockSpec((tq, LANES), lambda b, h, i: (b * nq + i, h)),
        out_shape=jax.ShapeDtypeStruct((batch * seq, DIFF_WIDTH), BF16),
        scratch_shapes=[pltpu.VMEM((VT_ROWS, seq), BF16),
                        pltpu.VMEM((1, LANES), F32),
                        pltpu.VMEM((VT_ROWS, 2 * tq), F32)],
        compiler_params=_params("arbitrary", "arbitrary", "arbitrary"),
        name="diff_attn",
    )(qkv, qkv, qkv, lam_vecs, subln_col)


def _wo_kernel(ona_ref, odf_ref, x_ref, ada_ref, wo_ref, g_ref, wrh_ref, wrl_ref, br_ref, tri_ref, cnt0_ref,
               x1_ref, h2_ref, ti_ref, tw_ref, cnt_ref):
    gt1 = ada_ref[0, 2:3, :]
    sh2 = ada_ref[0, 3:4, :]
    sc2 = ada_ref[0, 4:5, :]
    lane = lax.broadcasted_iota(jnp.int32, (WO_SUB, LANES), 1).astype(F32)

    def mix_of(rows):
        return (jnp.dot(ona_ref[rows, :], wo_ref[:NA_WIDTH, :], preferred_element_type=F32)
                + jnp.dot(odf_ref[rows, :], wo_ref[NA_WIDTH:, :], preferred_element_type=F32))

    def route(rows, mix):
        x1 = x_ref[rows, :] + gt1 * mix
        x1_ref[rows, :] = x1
        ms = jnp.mean(x1 * x1, axis=-1, keepdims=True)
        h2 = x1 * lax.rsqrt(ms + EPS) * g_ref[...] * (1.0 + sc2) + sh2
        hi = h2.astype(BF16)
        h2_ref[rows, :] = _pack_halves(hi.astype(F32))
        lo = (h2 - hi.astype(F32)).astype(BF16)
        both = jnp.dot(hi, wrl_ref[...], preferred_element_type=F32)
        cur = (both + pltpu.roll(both, LANES - N_EXPERTS, axis=1)
               + jnp.dot(lo, wrh_ref[...], preferred_element_type=F32)) + br_ref[...]
        vals = []
        idxs = []
        for _ in range(TOP_K):
            m = jnp.max(cur, axis=-1, keepdims=True)
            idx = jnp.min(jnp.where(cur == m, lane, float(LANES)), axis=-1, keepdims=True)
            vals.append(m)
            idxs.append(idx)
            cur = jnp.where(lane == idx, -jnp.inf, cur)
        es = [jnp.exp(v - vals[0]) for v in vals]
        inv = 1.0 / (es[0] + es[1] + es[2] + es[3])
        sel = jnp.zeros((WO_SUB, LANES), F32)
        for j in range(TOP_K):
            sel = sel + jnp.where(lane == idxs[j], 1.0, 0.0)
        return idxs, [e * inv for e in es], sel

    subs = [pl.ds(r, WO_SUB) for r in range(0, x_ref.shape[0], WO_SUB)]
    routed = []
    mix_next = mix_of(subs[0])
    for j, rows in enumerate(subs):
        mix = mix_next
        if j + 1 < len(subs):
            mix_next = mix_of(subs[j + 1])
        routed.append(route(rows, mix))

    @pl.when(pl.program_id(0) == 0)
    def _():
        cnt_ref[...] = cnt0_ref[...]

    sel_all = jnp.concatenate([r[2] for r in routed], axis=0)
    before = jnp.dot(tri_ref[...], sel_all.astype(BF16), preferred_element_type=F32) + cnt_ref[...]
    cnt_ref[...] = cnt_ref[...] + jnp.sum(sel_all, axis=0, keepdims=True)

    for s, rows in enumerate(subs):
        idxs, wts, _ = routed[s]
        bef = before[s * WO_SUB:(s + 1) * WO_SUB]
        ti = jnp.zeros((WO_SUB, LANES), F32)
        tw = jnp.zeros((WO_SUB, LANES), F32)
        for j in range(TOP_K):
            rank = jnp.sum(jnp.where(lane == idxs[j], bef, 0.0), axis=-1, keepdims=True)
            ti = jnp.where(lane == float(j), idxs[j], ti)
            ti = jnp.where(lane == float(TOP_K + j), rank, ti)
            tw = jnp.where(lane == float(j), wts[j], tw)
        ti_ref[rows, :] = ti.astype(jnp.int32)
        tw_ref[rows, :] = tw


def _wo(o_na, o_df, x2d, ada_g, w_o_bf, g_ffn, wr_hi, wr_lo, br_pad, tri, cnt0, seq):
    n = x2d.shape[0]
    tm = WO_BLOCK
    per_seq = seq // tm
    row = lambda i: (i, 0)
    const = lambda i: (0, 0)
    return pl.pallas_call(
        _wo_kernel,
        grid=(n // tm,),
        in_specs=[pl.BlockSpec((tm, NA_WIDTH), row),
                  pl.BlockSpec((tm, DIFF_WIDTH), row),
                  pl.BlockSpec((tm, D_MODEL), row),
                  pl.BlockSpec((1, 6, D_MODEL), lambda i: (i // per_seq, 0, 0)),
                  pl.BlockSpec((D_MODEL, D_MODEL), const),
                  pl.BlockSpec((1, D_MODEL), const),
                  pl.BlockSpec((D_MODEL, LANES), const),
                  pl.BlockSpec((D_MODEL, LANES), const),
                  pl.BlockSpec((1, LANES), const),
                  pl.BlockSpec((tm, tm), const),
                  pl.BlockSpec((1, LANES), const)],
        out_specs=[pl.BlockSpec((tm, D_MODEL), row),
                   pl.BlockSpec((tm, PACKED), row),
                   pl.BlockSpec((tm, LANES), row),
                   pl.BlockSpec((tm, LANES), row),
                   pl.BlockSpec((1, LANES), const)],
        out_shape=[jax.ShapeDtypeStruct((n, D_MODEL), F32),
                   jax.ShapeDtypeStruct((n, PACKED), U32),
                   jax.ShapeDtypeStruct((n, LANES), jnp.int32),
                   jax.ShapeDtypeStruct((n, LANES), F32),
                   jax.ShapeDtypeStruct((1, LANES), F32)],
        compiler_params=_params("arbitrary"),
        name="wo_router",
    )(o_na, o_df, x2d, ada_g, w_o_bf, g_ffn, wr_hi, wr_lo, br_pad, tri, cnt0)


def _expert_kernel(be_ref, nxt_ref, h2_ref, na_ref, xs_ref, wg_hbm, bg_ref, wu_hbm, bu_ref, wd_hbm, bd_ref, o_ref,
                   w_f32, w_bf, sems, slot_ref):
    i = pl.program_id(0)
    active = i < na_ref[0]
    expert = be_ref[i]
    new_expert = jnp.logical_or(i == 0, expert != be_ref[jnp.maximum(i - 1, 0)])
    wg_bf, wu_bf, wd_bf = w_bf.at[0], w_bf.at[1], w_bf.at[2]

    def weight_copies(src_expert, slot):
        return [pltpu.make_async_copy(w_hbm.at[src_expert], w_f32.at[slot, j], sems.at[slot, j])
                for j, w_hbm in enumerate((wg_hbm, wu_hbm, wd_hbm))]

    @pl.when(i == 0)
    def _():
        slot_ref[0] = 0
        for cp in weight_copies(expert, 0):
            cp.start()

    @pl.when(jnp.logical_and(active, new_expert))
    def _():
        slot = slot_ref[0]
        nxt = nxt_ref[i]
        for s in range(2):
            @pl.when(slot == s)
            def _():
                @pl.when(nxt >= 0)
                def _():
                    for cp in weight_copies(nxt, 1 - s):
                        cp.start(priority=1)
                for j, cp in enumerate(weight_copies(expert, s)):
                    cp.wait()
                    w_bf[j] = w_f32[s, j].astype(BF16)
        slot_ref[0] = 1 - slot

    def run_half(first_row):
        def gate_up(rows):
            x_lo, x_hi = _unpack_halves(xs_ref[rows, :])
            x_lo = x_lo.astype(BF16)
            x_hi = x_hi.astype(BF16)

            def proj(w_bf):
                return (jnp.dot(x_lo, w_bf[:PACKED, :], preferred_element_type=F32)
                        + jnp.dot(x_hi, w_bf[PACKED:, :], preferred_element_type=F32))

            return proj(wg_bf), proj(wu_bf)

        def act_down(rows, gu):
            g = jnp.minimum(gu[0] + bg_ref[0], SWIGLU_LIMIT)
            u = jnp.clip(gu[1] + bu_ref[0], -SWIGLU_LIMIT, SWIGLU_LIMIT)
            act = g * jax.nn.sigmoid(SWIGLU_ALPHA * g) * (u + 1.0)
            out = jnp.dot(act.astype(BF16), wd_bf[...], preferred_element_type=F32) + bd_ref[0]
            o_ref[rows, :] = _pack_halves(out.astype(BF16).astype(F32))

        sub = [pl.ds(first_row + r, EXPERT_SUB) for r in range(0, EXPERT_HALF, EXPERT_SUB)]
        gu_next = gate_up(sub[0])
        for j, rows in enumerate(sub):
            gu = gu_next
            if j + 1 < len(sub):
                gu_next = gate_up(sub[j + 1])
            act_down(rows, gu)

    @pl.when(active)
    def _():
        run_half(0)

    second = jnp.logical_and(active, h2_ref[i] > 0)

    @pl.when(second)
    def _():
        run_half(EXPERT_HALF)

    @pl.when(jnp.logical_not(active))
    def _():
        o_ref[:EXPERT_HALF, :] = jnp.zeros((EXPERT_HALF, PACKED), U32)

    @pl.when(jnp.logical_not(second))
    def _():
        o_ref[EXPERT_HALF:, :] = jnp.zeros((EXPERT_HALF, PACKED), U32)


def _experts(block_e, next_e, second_half, n_active, xs, wg, bg, wu, bu, wd, bd):
    cap = xs.shape[0]
    n_blocks = cap // EXPERT_BLOCK
    xmap = lambda i, be, nx, h2, na: (jnp.minimum(i, na[0] - 1), 0)
    bmap = lambda i, be, nx, h2, na: (be[i], 0, 0)
    hbm = pl.BlockSpec(memory_space=pl.ANY)
    grid_spec = pltpu.PrefetchScalarGridSpec(
        num_scalar_prefetch=4,
        grid=(n_blocks,),
        in_specs=[pl.BlockSpec((EXPERT_BLOCK, PACKED), xmap),
                  hbm, pl.BlockSpec((1, 1, D_MODEL), bmap),
                  hbm, pl.BlockSpec((1, 1, D_MODEL), bmap),
                  hbm, pl.BlockSpec((1, 1, D_MODEL), bmap)],
        out_specs=pl.BlockSpec((EXPERT_BLOCK, PACKED), lambda i, be, nx, h2, na: (i, 0)),
        scratch_shapes=[pltpu.VMEM((2, 3, D_MODEL, D_MODEL), F32),
                        pltpu.VMEM((3, D_MODEL, D_MODEL), BF16),
                        pltpu.SemaphoreType.DMA((2, 3)),
                        pltpu.SMEM((1,), jnp.int32)],
    )
    return pl.pallas_call(
        _expert_kernel,
        grid_spec=grid_spec,
        out_shape=jax.ShapeDtypeStruct((cap, PACKED), U32),
        compiler_params=_params("arbitrary"),
        name="experts",
    )(block_e, next_e, second_half, n_active, xs, wg, bg, wu, bu, wd, bd)


def _sc_gather(table, idx):
    n_out = idx.shape[0]
    width = table.shape[1]
    per_worker = n_out // SC_WORKERS
    n_chunks = per_worker // GATHER_ROWS
    assert per_worker * SC_WORKERS == n_out and n_chunks * GATHER_ROWS == per_worker and n_chunks % 2 == 0
    idx3 = idx.reshape(SC_WORKERS, n_chunks, GATHER_ROWS)
    mesh = plsc.VectorSubcoreMesh(core_axis_name="core", subcore_axis_name="subcore")

    @functools.partial(
        pl.kernel, mesh=mesh,
        out_type=jax.ShapeDtypeStruct((n_out, width), table.dtype),
        scratch_types=[pltpu.VMEM((n_chunks, GATHER_ROWS), jnp.int32),
                       pltpu.VMEM((2, GATHER_ROWS, width), table.dtype),
                       pltpu.SemaphoreType.DMA((2,)),
                       pltpu.SemaphoreType.DMA((2,))])
    def gather_kernel(table_hbm, idx_hbm, out_hbm, idx_v, rows_v, gsem, wsem):
        wid = lax.axis_index("subcore") * SC_CORES + lax.axis_index("core")
        base = wid * per_worker
        pltpu.sync_copy(idx_hbm.at[wid], idx_v)

        def gather(j, slot):
            return pltpu.make_async_copy(table_hbm.at[idx_v.at[j]], rows_v.at[slot], gsem.at[slot])

        def write(j, slot):
            dst = out_hbm.at[pl.ds(pl.multiple_of(base + j * GATHER_ROWS, GATHER_ROWS), GATHER_ROWS)]
            return pltpu.make_async_copy(rows_v.at[slot], dst, wsem.at[slot])

        gather(0, 0).start()

        @pl.loop(0, n_chunks, step=2)
        def _(j):
            for slot in range(2):
                jj = j + slot
                gather(jj, slot).wait()

                @pl.when(jj >= 1)
                def _():
                    write(jj - 1, 1 - slot).wait()

                @pl.when(jj + 1 < n_chunks)
                def _():
                    gather(jj + 1, 1 - slot).start()

                write(jj, slot).start()

        write(n_chunks - 1, 1).wait()

    return gather_kernel(table, idx3)


def _sc_scatter(rows, idx, n_out):
    n_src, width = rows.shape
    n_idx = idx.shape[0]
    per_worker = n_idx // SC_WORKERS
    n_chunks = per_worker // GATHER_ROWS
    assert per_worker * SC_WORKERS == n_idx and n_chunks * GATHER_ROWS == per_worker and n_chunks % 2 == 0
    assert n_src % per_worker == 0
    idx3 = idx.reshape(SC_WORKERS, n_chunks, GATHER_ROWS)
    mesh = plsc.VectorSubcoreMesh(core_axis_name="core", subcore_axis_name="subcore")

    @functools.partial(
        pl.kernel, mesh=mesh,
        out_type=jax.ShapeDtypeStruct((n_out, width), rows.dtype),
        scratch_types=[pltpu.VMEM((n_chunks, GATHER_ROWS), jnp.int32),
                       pltpu.VMEM((2, GATHER_ROWS, width), rows.dtype),
                       pltpu.SemaphoreType.DMA((2,)),
                       pltpu.SemaphoreType.DMA((2,))])
    def scatter_kernel(rows_hbm, idx_hbm, out_hbm, idx_v, rows_v, rsem, wsem):
        wid = lax.axis_index("subcore") * SC_CORES + lax.axis_index("core")
        base = lax.rem(wid * per_worker, n_src)
        pltpu.sync_copy(idx_hbm.at[wid], idx_v)

        def read(j, slot):
            src = rows_hbm.at[pl.ds(pl.multiple_of(base + j * GATHER_ROWS, GATHER_ROWS), GATHER_ROWS)]
            return pltpu.make_async_copy(src, rows_v.at[slot], rsem.at[slot])

        def write(j, slot):
            return pltpu.make_async_copy(rows_v.at[slot], out_hbm.at[idx_v.at[j]], wsem.at[slot])

        read(0, 0).start()

        @pl.loop(0, n_chunks, step=2)
        def _(j):
            for slot in range(2):
                jj = j + slot
                read(jj, slot).wait()

                @pl.when(jj >= 1)
                def _():
                    write(jj - 1, 1 - slot).wait()

                @pl.when(jj + 1 < n_chunks)
                def _():
                    read(jj + 1, 1 - slot).start()

                write(jj, slot).start()

        write(n_chunks - 1, 1).wait()

    return scatter_kernel(rows, idx3)


def _combine_kernel(x1_ref, y0_ref, y1_ref, y2_ref, y3_ref, tw_ref, ada_ref, o_ref):
    tw = tw_ref[...]
    acc_lo = jnp.zeros((x1_ref.shape[0], PACKED), F32)
    acc_hi = jnp.zeros((x1_ref.shape[0], PACKED), F32)
    for j, y_ref in enumerate((y0_ref, y1_ref, y2_ref, y3_ref)):
        lo, hi = _unpack_halves(y_ref[...])
        acc_lo = acc_lo + tw[:, j:j + 1] * lo
        acc_hi = acc_hi + tw[:, j:j + 1] * hi
    o_ref[:, :PACKED] = x1_ref[:, :PACKED] + ada_ref[0, 5:6, :PACKED] * acc_lo
    o_ref[:, PACKED:] = x1_ref[:, PACKED:] + ada_ref[0, 5:6, PACKED:] * acc_hi


def _combine(x1, ys, tw, ada_g, seq, row_off):
    n = x1.shape[0]
    tm = ROW_BLOCK
    per_seq = seq // tm
    off = row_off // tm
    per_choice = tw.shape[0] // tm
    y_specs = [pl.BlockSpec((tm, PACKED), functools.partial(lambda i, j: (j * per_choice + off + i, 0), j=j))
               for j in range(TOP_K)]
    return pl.pallas_call(
        _combine_kernel,
        grid=(n // tm,),
        in_specs=[pl.BlockSpec((tm, D_MODEL), lambda i: (i, 0)),
                  *y_specs,
                  pl.BlockSpec((tm, LANES), lambda i: (i + off, 0)),
                  pl.BlockSpec((1, 6, D_MODEL), lambda i: (i // per_seq, 0, 0))],
        out_specs=pl.BlockSpec((tm, D_MODEL), lambda i: (i, 0)),
        out_shape=jax.ShapeDtypeStruct((n, D_MODEL), F32),
        compiler_params=_params("arbitrary"),
        name="combine",
    )(x1, ys, ys, ys, ys, tw, ada_g)


def _rope_tables(seq):
    half = HEAD_DIM // 2
    inv = ROPE_THETA ** (-jnp.arange(half, dtype=F32) / half)
    ang = jnp.arange(seq, dtype=F32)[:, None] * inv[None, :]
    cos, sin = jnp.cos(ang), jnp.sin(ang)
    cos_h = jnp.concatenate([cos, cos], axis=-1)
    sin_h = jnp.concatenate([-sin, sin], axis=-1)
    reps = LANES // HEAD_DIM
    return jnp.tile(cos_h, (1, reps)), jnp.tile(sin_h, (1, reps))


def _na_bias_table(rpb):
    cols = jnp.arange(GRID_W, dtype=jnp.int32)
    c_start = jnp.clip(cols - NA_WIN_COLS // 2, 0, GRID_W - NA_WIN_COLS)
    col_mask = (cols[None, :] >= c_start[:, None]) & (cols[None, :] < c_start[:, None] + NA_WIN_COLS)
    col_idx = jnp.clip(cols[None, :] - cols[:, None], -(NA_WIN_COLS - 1), NA_WIN_COLS - 1) + NA_WIN_COLS - 1
    delta = jnp.arange(NA_WIN_ROWS, dtype=jnp.int32)
    j = jnp.arange(NA_WIN_ROWS, dtype=jnp.int32)
    row_idx = j[None, :] - delta[:, None] + NA_WIN_ROWS - 1
    row_hot = (row_idx[:, :, None] == jnp.arange(2 * NA_WIN_ROWS - 1, dtype=jnp.int32)).astype(F32)
    col_hot = (col_idx[:, :, None] == jnp.arange(2 * NA_WIN_COLS - 1, dtype=jnp.int32)).astype(F32)
    rpb_pairs = rpb.astype(F32).reshape(NA_HEADS // 2, 2, 2 * NA_WIN_ROWS - 1, 2 * NA_WIN_COLS - 1)
    pair_hot = jnp.eye(2, dtype=F32)
    lane_hot = jnp.einsum('qkc,hx->khqxc', col_hot, pair_hot).reshape(GRID_W, LANES, 2, 2 * NA_WIN_COLS - 1)
    bias = jnp.einsum('djr,pxrc,klxc->pdjkl', row_hot, rpb_pairs, lane_hot,
                      precision=lax.Precision.HIGHEST)
    lane_mask = jnp.concatenate([col_mask.T, col_mask.T], axis=1)
    bias = jnp.where(lane_mask[None, None, None], bias * LOG2E, NEG_INF)
    return bias.reshape(NA_HEADS // 2, NA_WIN_ROWS, NA_WIN_ROWS * GRID_W, LANES)


def _routing(top_idx, rank, counts, n):
    n_blocks = n * TOP_K // EXPERT_BLOCK + N_EXPERTS
    experts = jnp.arange(N_EXPERTS, dtype=jnp.int32)
    padded = (counts + EXPERT_BLOCK - 1) // EXPERT_BLOCK * EXPERT_BLOCK
    pad_end = jnp.cumsum(padded)
    pad_start = pad_end - padded
    start_of = jnp.sum(jnp.where(top_idx[:, :, None] == experts, pad_start, 0), axis=-1)
    dest = (start_of + rank).T.reshape(-1)
    block_lo = jnp.arange(n_blocks, dtype=jnp.int32) * EXPERT_BLOCK
    block_e = jnp.minimum(jnp.sum((pad_end[None, :] <= block_lo[:, None]).astype(jnp.int32), axis=1),
                          N_EXPERTS - 1).astype(jnp.int32)
    n_active = (pad_end[-1] // EXPERT_BLOCK).astype(jnp.int32).reshape(1)
    later = jnp.where((experts[None, :] > experts[:, None]) & (padded[None, :] > 0), experts[None, :], N_EXPERTS)
    next_nonempty = jnp.min(later, axis=1)
    next_nonempty = jnp.where(next_nonempty == N_EXPERTS, -1, next_nonempty)
    own = block_e[:, None] == experts[None, :]
    next_e = jnp.sum(jnp.where(own, next_nonempty[None, :], 0), axis=1)
    real_end = jnp.sum(jnp.where(own, (pad_start + counts)[None, :], 0), axis=1)
    second_half = (real_end - block_lo > EXPERT_HALF).astype(jnp.int32)
    return dest, block_e, next_e.astype(jnp.int32), second_half, n_active, n_blocks * EXPERT_BLOCK


def kernel(x_prompt, x_sample, c_prompt, c_sample, w_ada, b_ada, g_attn_norm, w_qkv, na_q_norm, na_k_norm, na_rpb, diff_q_norm, diff_k_norm, lambda_q1, lambda_k1, lambda_q2, lambda_k2, diff_subln, w_o, g_ffn_norm, w_router, b_router, w_gate, b_gate, w_up, b_up, w_down, b_down):
    l = 0
    groups = [(x_prompt, c_prompt), (x_sample, c_sample)]
    nb = [x.shape[0] for x, _ in groups]

    ada_all = _ada(jnp.concatenate([c for _, c in groups], axis=0), w_ada[l], b_ada[l])
    ada_all = ada_all.reshape(sum(nb), 6, D_MODEL)

    w_qkv_bf = w_qkv[l].astype(BF16)
    w_o_bf = w_o[l].astype(BF16)
    scale = HEAD_DIM ** -0.5
    reps = NA_WIDTH // HEAD_DIM
    gains = jnp.stack([jnp.tile(na_q_norm[l], reps) * (scale * LOG2E),
                       jnp.tile(na_k_norm[l], reps),
                       jnp.tile(diff_q_norm[l], reps) * (scale * LOG2E),
                       jnp.tile(diff_k_norm[l], reps)]).astype(F32)
    head_id = jnp.arange(MXU_DIM, dtype=jnp.int32) // HEAD_DIM
    bd = (head_id[:, None] == head_id[None, :]).astype(BF16)
    bias_t = _na_bias_table(na_rpb[l])
    lam_vecs = jnp.stack([lambda_q1[l], lambda_k1[l], lambda_q2[l], lambda_k2[l]]).astype(F32)
    subln_col = (diff_subln[l].astype(F32) * (1.0 - LAMBDA_INIT)).reshape(LANES, 1)
    wr = w_router[l].astype(F32)
    wr_pad = jnp.zeros((D_MODEL, LANES), F32).at[:, :N_EXPERTS].set(wr)
    wr_hi = wr_pad.astype(BF16)
    wr_lo = (wr_pad - wr_hi.astype(F32)).astype(BF16)
    wr_lo = wr_hi.at[:, N_EXPERTS:2 * N_EXPERTS].set(wr_lo[:, :N_EXPERTS])
    br_pad = jnp.full((1, LANES), NEG_INF, F32).at[0, :N_EXPERTS].set(b_router[l].astype(F32))
    g_attn = g_attn_norm[l].reshape(1, D_MODEL).astype(F32)
    g_ffn = g_ffn_norm[l].reshape(1, D_MODEL).astype(F32)
    max_seq = max(x.shape[1] for x, _ in groups)
    cos_t, sin_t = _rope_tables(max_seq)

    rows = lax.broadcasted_iota(jnp.int32, (WO_BLOCK, WO_BLOCK), 0)
    cols = lax.broadcasted_iota(jnp.int32, (WO_BLOCK, WO_BLOCK), 1)
    tri = (cols < rows).astype(BF16)
    cnt0 = jnp.zeros((1, LANES), F32)

    bg = b_gate[l].reshape(N_EXPERTS, 1, D_MODEL).astype(F32)
    bu = b_up[l].reshape(N_EXPERTS, 1, D_MODEL).astype(F32)
    bdn = b_down[l].reshape(N_EXPERTS, 1, D_MODEL).astype(F32)
    ada_groups = [ada_all[:nb[0]], ada_all[nb[0]:]]

    order = sorted(range(len(groups)), key=lambda g: -groups[g][0].shape[1])
    staged = {}
    for g in order:
        x = groups[g][0]
        b, seq = x.shape[0], x.shape[1]
        n = b * seq
        x2d = x.reshape(n, D_MODEL)
        ada_g = ada_groups[g]
        qkv = _qkv(x2d, ada_g, g_attn, w_qkv_bf, gains, cos_t, sin_t, bd, seq)
        o_na = _na(qkv, bias_t, b, seq)
        o_df = _diff(qkv, lam_vecs, subln_col, b, seq)
        x1, h2, ti, tw, cnt = _wo(o_na, o_df, x2d, ada_g, w_o_bf, g_ffn, wr_hi, wr_lo, br_pad, tri, cnt0, seq)
        counts = cnt[0, :N_EXPERTS].astype(jnp.int32)
        dest, *blocks, cap = _routing(ti[:, :TOP_K], ti[:, TOP_K:2 * TOP_K], counts, n)
        xs = _sc_scatter(h2, dest, cap)
        staged[g] = (x1, tw, ada_g, dest, blocks, xs, seq, b)

    sorted_out = {}
    for g in order:
        blocks, xs = staged[g][4], staged[g][5]
        sorted_out[g] = _experts(*blocks, xs, w_gate[l], bg, w_up[l], bu, w_down[l], bdn)

    outs = [None] * len(groups)
    for g in order:
        x1, tw, ada_g, dest, blocks, xs, seq, b = staged[g]
        ys = _sc_gather(sorted_out[g], dest)
        outs[g] = _combine(x1, ys, tw, ada_g, seq, 0).reshape(b, seq, D_MODEL)
    return tuple(outs)
```

```python
import functools
import math

import jax
import jax.numpy as jnp
from jax import lax
from jax.experimental import pallas as pl
from jax.experimental.pallas import tpu as pltpu
from jax.experimental.pallas import tpu_sc as plsc

F32 = jnp.float32
BF16 = jnp.bfloat16
U32 = jnp.uint32

D_MODEL = 1024
HEAD_DIM = 64
NA_HEADS = 8
NA_WIDTH = 512
DIFF_HEADS = 4
DIFF_WIDTH = 512
QKV_COLS = 3072
GRID_W = 64
NA_WIN_ROWS = 8
NA_WIN_COLS = 16
ROPE_THETA = 10000.0
N_EXPERTS = 32
TOP_K = 4
SWIGLU_LIMIT = 7.0
SWIGLU_ALPHA = 1.702
EPS = 1e-5
NEG_INF = -1e30
LAMBDA_INIT = 0.8 - 0.6 * math.exp(-0.3 * 0)
LOG2E = 1.4426950408889634

LANES = 128
MXU_DIM = 256
VMEM_LIMIT = 56 * 1024 * 1024

ROW_BLOCK = 512
Q_BLOCK = 1024
EXPERT_BLOCK = 1024
EXPERT_HALF = 512
EXPERT_SUB = 256
WO_BLOCK = 1024
WO_SUB = 256
NA_ROWS_PER_TRIP = 16
VT_ROWS = LANES + 16
KV_CHUNK = 256
NORM_SLACK = 1.01
MAX_SAFE_BOUND = 60.0


PACKED = D_MODEL // 2
SC_CORES = 2
SC_SUBCORES = 16
SC_WORKERS = SC_CORES * SC_SUBCORES
GATHER_ROWS = 64


def _params(*sem):
    return pltpu.CompilerParams(dimension_semantics=sem, vmem_limit_bytes=VMEM_LIMIT)


def _pack_halves(x):
    w = x.shape[1] // 2
    bits = lax.bitcast_convert_type(x, U32)
    return (bits[:, :w] >> 16) | bits[:, w:]


def _col_reduce(x, op):
    while x.shape[0] >= 64:
        x = op(x.reshape(8, x.shape[0] // 8, x.shape[1]), axis=0)
    return op(x, axis=0, keepdims=True)


def _unpack_halves(word):
    lo = lax.bitcast_convert_type(word << 16, F32)
    hi = lax.bitcast_convert_type(word & jnp.uint32(0xFFFF0000), F32)
    return lo, hi


def _ada_kernel(c_ref, w_ref, b_ref, o_ref):
    c = c_ref[...]
    s = c * jax.nn.sigmoid(c)
    o_ref[...] = jnp.dot(s, w_ref[...], preferred_element_type=F32,
                         precision=lax.Precision.HIGHEST) + b_ref[...]


def _ada(c_all, w_ada, b_ada):
    nb = c_all.shape[0]
    n_out = w_ada.shape[1]
    blk = D_MODEL
    return pl.pallas_call(
        _ada_kernel,
        grid=(n_out // blk,),
        in_specs=[pl.BlockSpec((nb, D_MODEL), lambda j: (0, 0)),
                  pl.BlockSpec((D_MODEL, blk), lambda j: (0, j)),
                  pl.BlockSpec((1, blk), lambda j: (0, j))],
        out_specs=pl.BlockSpec((nb, blk), lambda j: (0, j)),
        out_shape=jax.ShapeDtypeStruct((nb, n_out), F32),
        compiler_params=_params("arbitrary"),
        name="ada",
    )(c_all, w_ada, b_ada.reshape(1, n_out))


def _head_sumsq(y, bd):
    sq = (y * y).astype(BF16)
    parts = [jnp.dot(sq[:, c:c + MXU_DIM], bd, preferred_element_type=F32)
             for c in range(0, y.shape[1], MXU_DIM)]
    return jnp.concatenate(parts, axis=1)


def _qkv_kernel(x_ref, ada_ref, g_ref, w_ref, gain_ref, cos_ref, sin_ref, bd_ref, o_ref):
    x = x_ref[...]
    ms = jnp.mean(x * x, axis=-1, keepdims=True)
    xn = x * lax.rsqrt(ms + EPS) * g_ref[...]
    sh = ada_ref[0, 0:1, :]
    sc = ada_ref[0, 1:2, :]
    h = (xn * (1.0 + sc) + sh).astype(BF16)
    bd = bd_ref[...]
    lane = lax.broadcasted_iota(jnp.int32, (x.shape[0], NA_WIDTH), 1)
    first_half = (lane & (HEAD_DIM // 2)) == 0
    for grp in range(6):
        cols = slice(grp * 512, (grp + 1) * 512)
        acc = jnp.dot(h, w_ref[:, cols], preferred_element_type=F32)
        if grp in (2, 5):
            o_ref[:, cols] = acc.astype(BF16)
            continue
        gi = {0: 0, 1: 1, 3: 2, 4: 3}[grp]
        ss = _head_sumsq(acc, bd)
        y = acc * lax.rsqrt(ss * (1.0 / HEAD_DIM) + EPS) * gain_ref[gi:gi + 1, :]
        if grp in (3, 4):
            partner = jnp.where(first_half,
                                pltpu.roll(y, NA_WIDTH - HEAD_DIM // 2, axis=1),
                                pltpu.roll(y, HEAD_DIM // 2, axis=1))
            reps = NA_WIDTH // LANES
            y = (y * jnp.concatenate([cos_ref[...]] * reps, axis=1)
                 + partner * jnp.concatenate([sin_ref[...]] * reps, axis=1))
        o_ref[:, cols] = y.astype(BF16)


def _qkv(x2d, ada_g, g_attn, w_qkv_bf, gains, cos_t, sin_t, bd, seq):
    n = x2d.shape[0]
    tm = ROW_BLOCK
    per_seq = seq // tm
    return pl.pallas_call(
        _qkv_kernel,
        grid=(n // tm,),
        in_specs=[pl.BlockSpec((tm, D_MODEL), lambda i: (i, 0)),
                  pl.BlockSpec((1, 6, D_MODEL), lambda i: (i // per_seq, 0, 0)),
                  pl.BlockSpec((1, D_MODEL), lambda i: (0, 0)),
                  pl.BlockSpec((D_MODEL, QKV_COLS), lambda i: (0, 0)),
                  pl.BlockSpec((4, NA_WIDTH), lambda i: (0, 0)),
                  pl.BlockSpec((tm, LANES), lambda i: (i % per_seq, 0)),
                  pl.BlockSpec((tm, LANES), lambda i: (i % per_seq, 0)),
                  pl.BlockSpec((MXU_DIM, MXU_DIM), lambda i: (0, 0))],
        out_specs=pl.BlockSpec((tm, QKV_COLS), lambda i: (i, 0)),
        out_shape=jax.ShapeDtypeStruct((n, QKV_COLS), BF16),
        compiler_params=_params("arbitrary"),
        name="qkv",
    )(x2d, ada_g, g_attn, w_qkv_bf, gains, cos_t, sin_t, bd)


def _na_kernel(q_ref, k_ref, v_ref, bias_ref, o_ref, vaug_ref, *, rows):
    lane = lax.broadcasted_iota(jnp.int32, (GRID_W, LANES), 1)
    head0 = lane < HEAD_DIM
    win = NA_WIN_ROWS * GRID_W

    def window_start(r):
        return jnp.clip(r - NA_WIN_ROWS // 2, 0, rows - NA_WIN_ROWS)

    def scores(r):
        r_start = window_start(r)
        q = q_ref[pl.ds(pl.multiple_of(r * GRID_W, GRID_W), GRID_W), :]
        kw = k_ref[pl.ds(pl.multiple_of(r_start * GRID_W, GRID_W), win), :]
        zero = jnp.zeros_like(q)
        qm = jnp.concatenate([jnp.where(head0, q, zero), jnp.where(head0, zero, q)], axis=0)
        s = lax.dot_general(kw, qm, (((1,), (1,)), ((), ())), preferred_element_type=F32)
        return s + bias_ref[0, r - r_start]

    vaug_ref[:, :LANES] = v_ref[...]
    vaug_ref[:, LANES:] = jnp.ones((v_ref.shape[0], LANES), BF16)

    def finish(r, s):
        vw = vaug_ref[pl.ds(pl.multiple_of(window_start(r) * GRID_W, GRID_W), win), :]
        m = _col_reduce(s, jnp.max)
        p = jnp.exp2(s - m).astype(BF16)
        o2 = lax.dot_general(p, vw, (((0,), (0,)), ((), ())), preferred_element_type=F32)
        o2 = o2[:, :LANES] * (1.0 / o2[:, LANES:])
        o = jnp.where(head0, o2[:GRID_W], o2[GRID_W:])
        o_ref[pl.ds(pl.multiple_of(r * GRID_W, GRID_W), GRID_W), :] = o.astype(BF16)

    def body(i, carry):
        trip_rows = [i * NA_ROWS_PER_TRIP + u for u in range(NA_ROWS_PER_TRIP)]
        trip_scores = [scores(r) for r in trip_rows]
        for r, s in zip(trip_rows, trip_scores):
            finish(r, s)
        return carry

    lax.fori_loop(0, rows // NA_ROWS_PER_TRIP, body, 0)


def _na(qkv, bias_t, batch, seq):
    rows = seq // GRID_W
    n_pairs = NA_HEADS // 2
    return pl.pallas_call(
        functools.partial(_na_kernel, rows=rows),
        grid=(batch, n_pairs),
        in_specs=[pl.BlockSpec((seq, LANES), lambda b, hp: (b, hp)),
                  pl.BlockSpec((seq, LANES), lambda b, hp: (b, n_pairs + hp)),
                  pl.BlockSpec((seq, LANES), lambda b, hp: (b, 2 * n_pairs + hp)),
                  pl.BlockSpec((1, NA_WIN_ROWS, NA_WIN_ROWS * GRID_W, LANES), lambda b, hp: (hp, 0, 0, 0))],
        out_specs=pl.BlockSpec((seq, LANES), lambda b, hp: (b, hp)),
        out_shape=jax.ShapeDtypeStruct((batch * seq, NA_WIDTH), BF16),
        scratch_shapes=[pltpu.VMEM((seq, 2 * LANES), BF16)],
        compiler_params=_params("arbitrary", "arbitrary"),
        name="na_attn",
    )(qkv, qkv, qkv, bias_t)


def _diff_kernel(q_ref, k_ref, v_ref, lam_ref, g_ref, o_ref, vt_ref, kn_ref, oa_ref):
    @pl.when(pl.program_id(2) == 0)
    def _():
        vt_ref[:LANES, :] = v_ref[...].astype(F32).T.astype(BF16)
        ones_row = lax.broadcasted_iota(jnp.int32, (VT_ROWS - LANES, v_ref.shape[0]), 0) == 0
        vt_ref[LANES:, :] = jnp.where(ones_row, 1.0, 0.0).astype(BF16)
        kf = k_ref[...].astype(F32)
        d_id = lax.broadcasted_iota(jnp.int32, (LANES, LANES), 0) // HEAD_DIM
        c_id = lax.broadcasted_iota(jnp.int32, (LANES, LANES), 1)
        comp_sel = jnp.where(d_id == c_id, 1.0, 0.0).astype(BF16)
        kn2 = jnp.dot((kf * kf).astype(BF16), comp_sel, preferred_element_type=F32)
        kn_ref[...] = jnp.sqrt(_col_reduce(kn2, jnp.max)) * NORM_SLACK

    lq1 = lam_ref[0:1, :]
    lk1 = lam_ref[1:2, :]
    lq2 = lam_ref[2:3, :]
    lk2 = lam_ref[3:4, :]
    lam = (jnp.exp(jnp.sum(lq1 * lk1, axis=-1, keepdims=True))
           - jnp.exp(jnp.sum(lq2 * lk2, axis=-1, keepdims=True)) + LAMBDA_INIT)

    q = q_ref[...]
    lane = lax.broadcasted_iota(jnp.int32, q.shape, 1)
    zero = jnp.zeros_like(q)
    tq = q.shape[0]
    qcat = jnp.concatenate([jnp.where(lane < HEAD_DIM, q, zero), jnp.where(lane < HEAD_DIM, zero, q)], axis=0)
    def scores(c, chunk):
        kc = k_ref[c * chunk:(c + 1) * chunk, :]
        return lax.dot_general(kc, qcat, (((1,), (1,)), ((), ())), preferred_element_type=F32)

    def chunked(chunk, step):
        n_chunks = k_ref.shape[0] // chunk
        s_next = scores(0, chunk)
        state = None
        for c in range(n_chunks):
            s = s_next
            if c + 1 < n_chunks:
                s_next = scores(c + 1, chunk)
            state = step(c, s, vt_ref[:, c * chunk:(c + 1) * chunk], state)
        return state

    qf = qcat.astype(F32)
    ones8 = jnp.ones((8, LANES), BF16)
    qn2 = lax.dot_general(ones8, (qf * qf).astype(BF16), (((1,), (1,)), ((), ())), preferred_element_type=F32)
    col = lax.broadcasted_iota(jnp.int32, (1, 2 * tq), 1)
    bound = jnp.sqrt(qn2[0:1, :]) * NORM_SLACK * jnp.where(col < tq, kn_ref[0:1, 0:1], kn_ref[0:1, 1:2])
    in_range = jnp.max(bound) <= MAX_SAFE_BOUND

    @pl.when(in_range)
    def _():
        def step(c, s, vtc, acc):
            pv = jnp.dot(vtc, jnp.exp2(s - bound).astype(BF16), preferred_element_type=F32)
            return pv if c == 0 else acc + pv
        oa_ref[...] = chunked(KV_CHUNK, step)

    @pl.when(jnp.logical_not(in_range))
    def _():
        def step(c, s, vtc, state):
            mc = _col_reduce(s, jnp.max)
            m_new = mc if c == 0 else jnp.maximum(state[0], mc)
            pv = jnp.dot(vtc, jnp.exp2(s - m_new).astype(BF16), preferred_element_type=F32)
            return (m_new, pv if c == 0 else jnp.exp2(state[0] - m_new) * state[1] + pv)
        oa_ref[...] = chunked(KV_CHUNK, step)[1]

    oa = oa_ref[...]
    o0, l0 = oa[:LANES, :tq], oa[LANES:LANES + 1, :tq]
    o1, l1 = oa[:LANES, tq:], oa[LANES:LANES + 1, tq:]
    o = o0 * (1.0 / l0) - (lam / l1) * o1
    ms = jnp.mean(o * o, axis=0, keepdims=True)
    y = o * lax.rsqrt(ms + EPS) * g_ref[...]
    o_ref[...] = y.T.astype(BF16)


def _diff(qkv, lam_vecs, subln_col, batch, seq):
    tq = Q_BLOCK
    nq = seq // tq
    base = 3 * NA_WIDTH // LANES
    nh = DIFF_HEADS
    return pl.pallas_call(
        _diff_kernel,
        grid=(batch, nh, nq),
        in_specs=[pl.BlockSpec((tq, LANES), lambda b, h, i: (b * nq + i, base + h)),
                  pl.BlockSpec((seq, LANES), lambda b, h, i: (b, base + nh + h)),
                  pl.BlockSpec((seq, LANES), lambda b, h, i: (b, base + 2 * nh + h)),
                  pl.BlockSpec((4, HEAD_DIM), lambda b, h, i: (0, 0)),
                  pl.BlockSpec((LANES, 1), lambda b, h, i: (0, 0))],
        out_specs=pl.BlockSpec((tq, LANES), lambda b, h, i: (b * nq + i, h)),
        out_shape=jax.ShapeDtypeStruct((batch * seq, DIFF_WIDTH), BF16),
        scratch_shapes=[pltpu.VMEM((VT_ROWS, seq), BF16),
                        pltpu.VMEM((1, LANES), F32),
                        pltpu.VMEM((VT_ROWS, 2 * tq), F32)],
        compiler_params=_params("arbitrary", "arbitrary", "arbitrary"),
        name="diff_attn",
    )(qkv, qkv, qkv, lam_vecs, subln_col)


def _wo_kernel(ona_ref, odf_ref, x_ref, ada_ref, wo_ref, g_ref, wrh_ref, wrl_ref, br_ref, tri_ref, cnt0_ref,
               x1_ref, h2_ref, ti_ref, tw_ref, cnt_ref):
    gt1 = ada_ref[0, 2:3, :]
    sh2 = ada_ref[0, 3:4, :]
    sc2 = ada_ref[0, 4:5, :]
    lane = lax.broadcasted_iota(jnp.int32, (WO_SUB, LANES), 1).astype(F32)

    def mix_of(rows):
        return (jnp.dot(ona_ref[rows, :], wo_ref[:NA_WIDTH, :], preferred_element_type=F32)
                + jnp.dot(odf_ref[rows, :], wo_ref[NA_WIDTH:, :], preferred_element_type=F32))

    def route(rows, mix):
        x1 = x_ref[rows, :] + gt1 * mix
        x1_ref[rows, :] = x1
        ms = jnp.mean(x1 * x1, axis=-1, keepdims=True)
        h2 = x1 * lax.rsqrt(ms + EPS) * g_ref[...] * (1.0 + sc2) + sh2
        hi = h2.astype(BF16)
        h2_ref[rows, :] = _pack_halves(hi.astype(F32))
        lo = (h2 - hi.astype(F32)).astype(BF16)
        both = jnp.dot(hi, wrl_ref[...], preferred_element_type=F32)
        cur = (both + pltpu.roll(both, LANES - N_EXPERTS, axis=1)
               + jnp.dot(lo, wrh_ref[...], preferred_element_type=F32)) + br_ref[...]
        vals = []
        idxs = []
        for _ in range(TOP_K):
            m = jnp.max(cur, axis=-1, keepdims=True)
            idx = jnp.min(jnp.where(cur == m, lane, float(LANES)), axis=-1, keepdims=True)
            vals.append(m)
            idxs.append(idx)
            cur = jnp.where(lane == idx, -jnp.inf, cur)
        es = [jnp.exp(v - vals[0]) for v in vals]
        inv = 1.0 / (es[0] + es[1] + es[2] + es[3])
        sel = jnp.zeros((WO_SUB, LANES), F32)
        for j in range(TOP_K):
            sel = sel + jnp.where(lane == idxs[j], 1.0, 0.0)
        return idxs, [e * inv for e in es], sel

    subs = [pl.ds(r, WO_SUB) for r in range(0, x_ref.shape[0], WO_SUB)]
    routed = []
    mix_next = mix_of(subs[0])
    for j, rows in enumerate(subs):
        mix = mix_next
        if j + 1 < len(subs):
            mix_next = mix_of(subs[j + 1])
        routed.append(route(rows, mix))

    @pl.when(pl.program_id(0) == 0)
    def _():
        cnt_ref[...] = cnt0_ref[...]

    sel_all = jnp.concatenate([r[2] for r in routed], axis=0)
    before = jnp.dot(tri_ref[...], sel_all.astype(BF16), preferred_element_type=F32) + cnt_ref[...]
    cnt_ref[...] = cnt_ref[...] + jnp.sum(sel_all, axis=0, keepdims=True)

    for s, rows in enumerate(subs):
        idxs, wts, _ = routed[s]
        bef = before[s * WO_SUB:(s + 1) * WO_SUB]
        ti = jnp.zeros((WO_SUB, LANES), F32)
        tw = jnp.zeros((WO_SUB, LANES), F32)
        for j in range(TOP_K):
            rank = jnp.sum(jnp.where(lane == idxs[j], bef, 0.0), axis=-1, keepdims=True)
            ti = jnp.where(lane == float(j), idxs[j], ti)
            ti = jnp.where(lane == float(TOP_K + j), rank, ti)
            tw = jnp.where(lane == float(j), wts[j], tw)
        ti_ref[rows, :] = ti.astype(jnp.int32)
        tw_ref[rows, :] = tw


def _wo(o_na, o_df, x2d, ada_g, w_o_bf, g_ffn, wr_hi, wr_lo, br_pad, tri, cnt0, seq):
    n = x2d.shape[0]
    tm = WO_BLOCK
    per_seq = seq // tm
    row = lambda i: (i, 0)
    const = lambda i: (0, 0)
    return pl.pallas_call(
        _wo_kernel,
        grid=(n // tm,),
        in_specs=[pl.BlockSpec((tm, NA_WIDTH), row),
                  pl.BlockSpec((tm, DIFF_WIDTH), row),
                  pl.BlockSpec((tm, D_MODEL), row),
                  pl.BlockSpec((1, 6, D_MODEL), lambda i: (i // per_seq, 0, 0)),
                  pl.BlockSpec((D_MODEL, D_MODEL), const),
                  pl.BlockSpec((1, D_MODEL), const),
                  pl.BlockSpec((D_MODEL, LANES), const),
                  pl.BlockSpec((D_MODEL, LANES), const),
                  pl.BlockSpec((1, LANES), const),
                  pl.BlockSpec((tm, tm), const),
                  pl.BlockSpec((1, LANES), const)],
        out_specs=[pl.BlockSpec((tm, D_MODEL), row),
                   pl.BlockSpec((tm, PACKED), row),
                   pl.BlockSpec((tm, LANES), row),
                   pl.BlockSpec((tm, LANES), row),
                   pl.BlockSpec((1, LANES), const)],
        out_shape=[jax.ShapeDtypeStruct((n, D_MODEL), F32),
                   jax.ShapeDtypeStruct((n, PACKED), U32),
                   jax.ShapeDtypeStruct((n, LANES), jnp.int32),
                   jax.ShapeDtypeStruct((n, LANES), F32),
                   jax.ShapeDtypeStruct((1, LANES), F32)],
        compiler_params=_params("arbitrary"),
        name="wo_router",
    )(o_na, o_df, x2d, ada_g, w_o_bf, g_ffn, wr_hi, wr_lo, br_pad, tri, cnt0)


def _expert_kernel(be_ref, nxt_ref, fill_ref, na_ref, xs_ref, wg_hbm, bg_ref, wu_hbm, bu_ref, wd_hbm, bd_ref, o_ref,
                   w_f32, w_bf, sems, slot_ref):
    i = pl.program_id(0)
    active = i < na_ref[0]
    expert = be_ref[i]
    new_expert = jnp.logical_or(i == 0, expert != be_ref[jnp.maximum(i - 1, 0)])
    wg_bf, wu_bf, wd_bf = w_bf.at[0], w_bf.at[1], w_bf.at[2]

    def weight_copies(src_expert, slot):
        return [pltpu.make_async_copy(w_hbm.at[src_expert], w_f32.at[slot, j], sems.at[slot, j])
                for j, w_hbm in enumerate((wg_hbm, wu_hbm, wd_hbm))]

    @pl.when(i == 0)
    def _():
        slot_ref[0] = 0
        for cp in weight_copies(expert, 0):
            cp.start()

    @pl.when(jnp.logical_and(active, new_expert))
    def _():
        slot = slot_ref[0]
        nxt = nxt_ref[i]
        for s in range(2):
            @pl.when(slot == s)
            def _():
                @pl.when(nxt >= 0)
                def _():
                    for cp in weight_copies(nxt, 1 - s):
                        cp.start(priority=1)
                for j, cp in enumerate(weight_copies(expert, s)):
                    cp.wait()
                    w_bf[j] = w_f32[s, j].astype(BF16)
        slot_ref[0] = 1 - slot

    def run_half(first_row, n_sub):
        def gate_up(rows):
            x_lo, x_hi = _unpack_halves(xs_ref[rows, :])
            x_lo = x_lo.astype(BF16)
            x_hi = x_hi.astype(BF16)

            def proj(w_bf):
                return (jnp.dot(x_lo, w_bf[:PACKED, :], preferred_element_type=F32)
                        + jnp.dot(x_hi, w_bf[PACKED:, :], preferred_element_type=F32))

            return proj(wg_bf), proj(wu_bf)

        def act_down(rows, gu):
            g = jnp.minimum(gu[0] + bg_ref[0], SWIGLU_LIMIT)
            u = jnp.clip(gu[1] + bu_ref[0], -SWIGLU_LIMIT, SWIGLU_LIMIT)
            act = g * jax.nn.sigmoid(SWIGLU_ALPHA * g) * (u + 1.0)
            out = jnp.dot(act.astype(BF16), wd_bf[...], preferred_element_type=F32) + bd_ref[0]
            o_ref[rows, :] = _pack_halves(out.astype(BF16).astype(F32))

        sub = [pl.ds(first_row + r * EXPERT_SUB, EXPERT_SUB) for r in range(n_sub)]
        gu_next = gate_up(sub[0])
        for j, rows in enumerate(sub):
            gu = gu_next
            if j + 1 < len(sub):
                gu_next = gate_up(sub[j + 1])
            act_down(rows, gu)

    fill = jnp.where(active, fill_ref[i], 0)
    subs_per_half = EXPERT_HALF // EXPERT_SUB

    @pl.when(fill < 2 * subs_per_half)
    def _():
        o_ref[...] = jnp.zeros_like(o_ref)

    for half in range(2):
        for n_sub in range(1, subs_per_half + 1):
            here = fill - half * subs_per_half
            cond = (here >= n_sub) if n_sub == subs_per_half else (here == n_sub)

            @pl.when(cond)
            def _(half=half, n_sub=n_sub):
                run_half(half * EXPERT_HALF, n_sub)


def _experts(block_e, next_e, fill, n_active, xs, wg, bg, wu, bu, wd, bd):
    cap = xs.shape[0]
    n_blocks = cap // EXPERT_BLOCK
    xmap = lambda i, be, nx, h2, na: (jnp.minimum(i, na[0] - 1), 0)
    bmap = lambda i, be, nx, h2, na: (be[i], 0, 0)
    hbm = pl.BlockSpec(memory_space=pl.ANY)
    grid_spec = pltpu.PrefetchScalarGridSpec(
        num_scalar_prefetch=4,
        grid=(n_blocks,),
        in_specs=[pl.BlockSpec((EXPERT_BLOCK, PACKED), xmap),
                  hbm, pl.BlockSpec((1, 1, D_MODEL), bmap),
                  hbm, pl.BlockSpec((1, 1, D_MODEL), bmap),
                  hbm, pl.BlockSpec((1, 1, D_MODEL), bmap)],
        out_specs=pl.BlockSpec((EXPERT_BLOCK, PACKED), lambda i, be, nx, h2, na: (i, 0)),
        scratch_shapes=[pltpu.VMEM((2, 3, D_MODEL, D_MODEL), F32),
                        pltpu.VMEM((3, D_MODEL, D_MODEL), BF16),
                        pltpu.SemaphoreType.DMA((2, 3)),
                        pltpu.SMEM((1,), jnp.int32)],
    )
    return pl.pallas_call(
        _expert_kernel,
        grid_spec=grid_spec,
        out_shape=jax.ShapeDtypeStruct((cap, PACKED), U32),
        compiler_params=_params("arbitrary"),
        name="experts",
    )(block_e, next_e, fill, n_active, xs, wg, bg, wu, bu, wd, bd)


def _sc_gather(table, idx):
    n_out = idx.shape[0]
    width = table.shape[1]
    per_worker = n_out // SC_WORKERS
    n_chunks = per_worker // GATHER_ROWS
    assert per_worker * SC_WORKERS == n_out and n_chunks * GATHER_ROWS == per_worker and n_chunks % 2 == 0
    idx3 = idx.reshape(SC_WORKERS, n_chunks, GATHER_ROWS)
    mesh = plsc.VectorSubcoreMesh(core_axis_name="core", subcore_axis_name="subcore")

    @functools.partial(
        pl.kernel, mesh=mesh,
        out_type=jax.ShapeDtypeStruct((n_out, width), table.dtype),
        scratch_types=[pltpu.VMEM((n_chunks, GATHER_ROWS), jnp.int32),
                       pltpu.VMEM((2, GATHER_ROWS, width), table.dtype),
                       pltpu.SemaphoreType.DMA((2,)),
                       pltpu.SemaphoreType.DMA((2,))])
    def gather_kernel(table_hbm, idx_hbm, out_hbm, idx_v, rows_v, gsem, wsem):
        wid = lax.axis_index("subcore") * SC_CORES + lax.axis_index("core")
        base = wid * per_worker
        pltpu.sync_copy(idx_hbm.at[wid], idx_v)

        def gather(j, slot):
            return pltpu.make_async_copy(table_hbm.at[idx_v.at[j]], rows_v.at[slot], gsem.at[slot])

        def write(j, slot):
            dst = out_hbm.at[pl.ds(pl.multiple_of(base + j * GATHER_ROWS, GATHER_ROWS), GATHER_ROWS)]
            return pltpu.make_async_copy(rows_v.at[slot], dst, wsem.at[slot])

        gather(0, 0).start()

        @pl.loop(0, n_chunks, step=2)
        def _(j):
            for slot in range(2):
                jj = j + slot
                gather(jj, slot).wait()

                @pl.when(jj >= 1)
                def _():
                    write(jj - 1, 1 - slot).wait()

                @pl.when(jj + 1 < n_chunks)
                def _():
                    gather(jj + 1, 1 - slot).start()

                write(jj, slot).start()

        write(n_chunks - 1, 1).wait()

    return gather_kernel(table, idx3)


def _sc_scatter(rows, idx, n_out):
    n_src, width = rows.shape
    n_idx = idx.shape[0]
    per_worker = n_idx // SC_WORKERS
    n_chunks = per_worker // GATHER_ROWS
    assert per_worker * SC_WORKERS == n_idx and n_chunks * GATHER_ROWS == per_worker and n_chunks % 2 == 0
    assert n_src % per_worker == 0
    idx3 = idx.reshape(SC_WORKERS, n_chunks, GATHER_ROWS)
    mesh = plsc.VectorSubcoreMesh(core_axis_name="core", subcore_axis_name="subcore")

    @functools.partial(
        pl.kernel, mesh=mesh,
        out_type=jax.ShapeDtypeStruct((n_out, width), rows.dtype),
        scratch_types=[pltpu.VMEM((n_chunks, GATHER_ROWS), jnp.int32),
                       pltpu.VMEM((2, GATHER_ROWS, width), rows.dtype),
                       pltpu.SemaphoreType.DMA((2,)),
                       pltpu.SemaphoreType.DMA((2,))])
    def scatter_kernel(rows_hbm, idx_hbm, out_hbm, idx_v, rows_v, rsem, wsem):
        wid = lax.axis_index("subcore") * SC_CORES + lax.axis_index("core")
        base = lax.rem(wid * per_worker, n_src)
        pltpu.sync_copy(idx_hbm.at[wid], idx_v)

        def read(j, slot):
            src = rows_hbm.at[pl.ds(pl.multiple_of(base + j * GATHER_ROWS, GATHER_ROWS), GATHER_ROWS)]
            return pltpu.make_async_copy(src, rows_v.at[slot], rsem.at[slot])

        def write(j, slot):
            return pltpu.make_async_copy(rows_v.at[slot], out_hbm.at[idx_v.at[j]], wsem.at[slot])

        read(0, 0).start()

        @pl.loop(0, n_chunks, step=2)
        def _(j):
            for slot in range(2):
                jj = j + slot
                read(jj, slot).wait()

                @pl.when(jj >= 1)
                def _():
                    write(jj - 1, 1 - slot).wait()

                @pl.when(jj + 1 < n_chunks)
                def _():
                    read(jj + 1, 1 - slot).start()

                write(jj, slot).start()

        write(n_chunks - 1, 1).wait()

    return scatter_kernel(rows, idx3)


def _combine_kernel(x1_ref, y0_ref, y1_ref, y2_ref, y3_ref, tw_ref, ada_ref, o_ref):
    tw = tw_ref[...]
    acc_lo = jnp.zeros((x1_ref.shape[0], PACKED), F32)
    acc_hi = jnp.zeros((x1_ref.shape[0], PACKED), F32)
    for j, y_ref in enumerate((y0_ref, y1_ref, y2_ref, y3_ref)):
        lo, hi = _unpack_halves(y_ref[...])
        acc_lo = acc_lo + tw[:, j:j + 1] * lo
        acc_hi = acc_hi + tw[:, j:j + 1] * hi
    o_ref[:, :PACKED] = x1_ref[:, :PACKED] + ada_ref[0, 5:6, :PACKED] * acc_lo
    o_ref[:, PACKED:] = x1_ref[:, PACKED:] + ada_ref[0, 5:6, PACKED:] * acc_hi


def _combine(x1, ys, tw, ada_g, seq, row_off):
    n = x1.shape[0]
    tm = ROW_BLOCK
    per_seq = seq // tm
    off = row_off // tm
    per_choice = tw.shape[0] // tm
    y_specs = [pl.BlockSpec((tm, PACKED), functools.partial(lambda i, j: (j * per_choice + off + i, 0), j=j))
               for j in range(TOP_K)]
    return pl.pallas_call(
        _combine_kernel,
        grid=(n // tm,),
        in_specs=[pl.BlockSpec((tm, D_MODEL), lambda i: (i, 0)),
                  *y_specs,
                  pl.BlockSpec((tm, LANES), lambda i: (i + off, 0)),
                  pl.BlockSpec((1, 6, D_MODEL), lambda i: (i // per_seq, 0, 0))],
        out_specs=pl.BlockSpec((tm, D_MODEL), lambda i: (i, 0)),
        out_shape=jax.ShapeDtypeStruct((n, D_MODEL), F32),
        compiler_params=_params("arbitrary"),
        name="combine",
    )(x1, ys, ys, ys, ys, tw, ada_g)


def _rope_tables(seq):
    half = HEAD_DIM // 2
    inv = ROPE_THETA ** (-jnp.arange(half, dtype=F32) / half)
    ang = jnp.arange(seq, dtype=F32)[:, None] * inv[None, :]
    cos, sin = jnp.cos(ang), jnp.sin(ang)
    cos_h = jnp.concatenate([cos, cos], axis=-1)
    sin_h = jnp.concatenate([-sin, sin], axis=-1)
    reps = LANES // HEAD_DIM
    return jnp.tile(cos_h, (1, reps)), jnp.tile(sin_h, (1, reps))


def _na_bias_table(rpb):
    cols = jnp.arange(GRID_W, dtype=jnp.int32)
    c_start = jnp.clip(cols - NA_WIN_COLS // 2, 0, GRID_W - NA_WIN_COLS)
    col_mask = (cols[None, :] >= c_start[:, None]) & (cols[None, :] < c_start[:, None] + NA_WIN_COLS)
    col_idx = jnp.clip(cols[None, :] - cols[:, None], -(NA_WIN_COLS - 1), NA_WIN_COLS - 1) + NA_WIN_COLS - 1
    delta = jnp.arange(NA_WIN_ROWS, dtype=jnp.int32)
    j = jnp.arange(NA_WIN_ROWS, dtype=jnp.int32)
    row_idx = j[None, :] - delta[:, None] + NA_WIN_ROWS - 1
    row_hot = (row_idx[:, :, None] == jnp.arange(2 * NA_WIN_ROWS - 1, dtype=jnp.int32)).astype(F32)
    col_hot = (col_idx[:, :, None] == jnp.arange(2 * NA_WIN_COLS - 1, dtype=jnp.int32)).astype(F32)
    rpb_pairs = rpb.astype(F32).reshape(NA_HEADS // 2, 2, 2 * NA_WIN_ROWS - 1, 2 * NA_WIN_COLS - 1)
    pair_hot = jnp.eye(2, dtype=F32)
    lane_hot = jnp.einsum('qkc,hx->khqxc', col_hot, pair_hot).reshape(GRID_W, LANES, 2, 2 * NA_WIN_COLS - 1)
    bias = jnp.einsum('djr,pxrc,klxc->pdjkl', row_hot, rpb_pairs, lane_hot,
                      precision=lax.Precision.HIGHEST)
    lane_mask = jnp.concatenate([col_mask.T, col_mask.T], axis=1)
    bias = jnp.where(lane_mask[None, None, None], bias * LOG2E, NEG_INF)
    return bias.reshape(NA_HEADS // 2, NA_WIN_ROWS, NA_WIN_ROWS * GRID_W, LANES)


def _routing(top_idx, rank, counts, n):
    n_blocks = n * TOP_K // EXPERT_BLOCK + N_EXPERTS
    experts = jnp.arange(N_EXPERTS, dtype=jnp.int32)
    padded = (counts + EXPERT_BLOCK - 1) // EXPERT_BLOCK * EXPERT_BLOCK
    pad_end = jnp.cumsum(padded)
    pad_start = pad_end - padded
    start_of = jnp.sum(jnp.where(top_idx[None] == experts[:, None, None], pad_start[:, None, None], 0), axis=0)
    dest = (start_of + rank).reshape(-1)
    block_lo = jnp.arange(n_blocks, dtype=jnp.int32) * EXPERT_BLOCK
    block_e = jnp.minimum(jnp.sum((pad_end[None, :] <= block_lo[:, None]).astype(jnp.int32), axis=1),
                          N_EXPERTS - 1).astype(jnp.int32)
    n_active = (pad_end[-1] // EXPERT_BLOCK).astype(jnp.int32).reshape(1)
    later = jnp.where((experts[None, :] > experts[:, None]) & (padded[None, :] > 0), experts[None, :], N_EXPERTS)
    next_nonempty = jnp.min(later, axis=1)
    next_nonempty = jnp.where(next_nonempty == N_EXPERTS, -1, next_nonempty)
    own = block_e[:, None] == experts[None, :]
    next_e = jnp.sum(jnp.where(own, next_nonempty[None, :], 0), axis=1)
    real_end = jnp.sum(jnp.where(own, (pad_start + counts)[None, :], 0), axis=1)
    fill = jnp.clip((real_end - block_lo + EXPERT_SUB - 1) // EXPERT_SUB, 0, EXPERT_BLOCK // EXPERT_SUB)
    return dest, block_e, next_e.astype(jnp.int32), fill.astype(jnp.int32), n_active, n_blocks * EXPERT_BLOCK


def kernel(x_prompt, x_sample, c_prompt, c_sample, w_ada, b_ada, g_attn_norm, w_qkv, na_q_norm, na_k_norm, na_rpb, diff_q_norm, diff_k_norm, lambda_q1, lambda_k1, lambda_q2, lambda_k2, diff_subln, w_o, g_ffn_norm, w_router, b_router, w_gate, b_gate, w_up, b_up, w_down, b_down):
    l = 0
    groups = [(x_prompt, c_prompt), (x_sample, c_sample)]
    nb = [x.shape[0] for x, _ in groups]

    ada_all = _ada(jnp.concatenate([c for _, c in groups], axis=0), w_ada[l], b_ada[l])
    ada_all = ada_all.reshape(sum(nb), 6, D_MODEL)

    w_qkv_bf = w_qkv[l].astype(BF16)
    w_o_bf = w_o[l].astype(BF16)
    scale = HEAD_DIM ** -0.5
    reps = NA_WIDTH // HEAD_DIM
    gains = jnp.stack([jnp.tile(na_q_norm[l], reps) * (scale * LOG2E),
                       jnp.tile(na_k_norm[l], reps),
                       jnp.tile(diff_q_norm[l], reps) * (scale * LOG2E),
                       jnp.tile(diff_k_norm[l], reps)]).astype(F32)
    head_id = jnp.arange(MXU_DIM, dtype=jnp.int32) // HEAD_DIM
    bd = (head_id[:, None] == head_id[None, :]).astype(BF16)
    bias_t = _na_bias_table(na_rpb[l])
    lam_vecs = jnp.stack([lambda_q1[l], lambda_k1[l], lambda_q2[l], lambda_k2[l]]).astype(F32)
    subln_col = (diff_subln[l].astype(F32) * (1.0 - LAMBDA_INIT)).reshape(LANES, 1)
    wr = w_router[l].astype(F32)
    wr_pad = jnp.zeros((D_MODEL, LANES), F32).at[:, :N_EXPERTS].set(wr)
    wr_hi = wr_pad.astype(BF16)
    wr_lo = (wr_pad - wr_hi.astype(F32)).astype(BF16)
    wr_lo = wr_hi.at[:, N_EXPERTS:2 * N_EXPERTS].set(wr_lo[:, :N_EXPERTS])
    br_pad = jnp.full((1, LANES), NEG_INF, F32).at[0, :N_EXPERTS].set(b_router[l].astype(F32))
    g_attn = g_attn_norm[l].reshape(1, D_MODEL).astype(F32)
    g_ffn = g_ffn_norm[l].reshape(1, D_MODEL).astype(F32)
    max_seq = max(x.shape[1] for x, _ in groups)
    cos_t, sin_t = _rope_tables(max_seq)

    rows = lax.broadcasted_iota(jnp.int32, (WO_BLOCK, WO_BLOCK), 0)
    cols = lax.broadcasted_iota(jnp.int32, (WO_BLOCK, WO_BLOCK), 1)
    tri = (cols < rows).astype(BF16)
    cnt0 = jnp.zeros((1, LANES), F32)

    bg = b_gate[l].reshape(N_EXPERTS, 1, D_MODEL).astype(F32)
    bu = b_up[l].reshape(N_EXPERTS, 1, D_MODEL).astype(F32)
    bdn = b_down[l].reshape(N_EXPERTS, 1, D_MODEL).astype(F32)
    ada_groups = [ada_all[:nb[0]], ada_all[nb[0]:]]

    order = sorted(range(len(groups)), key=lambda g: -groups[g][0].shape[1])
    staged = {}
    for g in order:
        x = groups[g][0]
        b, seq = x.shape[0], x.shape[1]
        n = b * seq
        x2d = x.reshape(n, D_MODEL)
        ada_g = ada_groups[g]
        qkv = _qkv(x2d, ada_g, g_attn, w_qkv_bf, gains, cos_t, sin_t, bd, seq)
        o_na = _na(qkv, bias_t, b, seq)
        o_df = _diff(qkv, lam_vecs, subln_col, b, seq)
        x1, h2, ti, tw, cnt = _wo(o_na, o_df, x2d, ada_g, w_o_bf, g_ffn, wr_hi, wr_lo, br_pad, tri, cnt0, seq)
        counts = cnt[0, :N_EXPERTS].astype(jnp.int32)
        ti_t = ti[:, :2 * TOP_K].T
        dest, *blocks, cap = _routing(ti_t[:TOP_K], ti_t[TOP_K:], counts, n)
        xs = _sc_scatter(h2, dest, cap)
        staged[g] = (x1, tw, ada_g, dest, blocks, xs, seq, b)

    sorted_out = {}
    for g in order:
        blocks, xs = staged[g][4], staged[g][5]
        sorted_out[g] = _experts(*blocks, xs, w_gate[l], bg, w_up[l], bu, w_down[l], bdn)

    outs = [None] * len(groups)
    for g in order:
        x1, tw, ada_g, dest, blocks, xs, seq, b = staged[g]
        ys = _sc_gather(sorted_out[g], dest)
        outs[g] = _combine(x1, ys, tw, ada_g, seq, 0).reshape(b, seq, D_MODEL)
    return tuple(outs)
```

```python
import functools
import math

import jax
import jax.numpy as jnp
from jax import lax
from jax.experimental import pallas as pl
from jax.experimental.pallas import tpu as pltpu
from jax.experimental.pallas import tpu_sc as plsc

F32 = jnp.float32
BF16 = jnp.bfloat16
U32 = jnp.uint32

D_MODEL = 1024
HEAD_DIM = 64
NA_HEADS = 8
NA_WIDTH = 512
DIFF_HEADS = 4
DIFF_WIDTH = 512
QKV_COLS = 3072
GRID_W = 64
NA_WIN_ROWS = 8
NA_WIN_COLS = 16
ROPE_THETA = 10000.0
N_EXPERTS = 32
TOP_K = 4
SWIGLU_LIMIT = 7.0
SWIGLU_ALPHA = 1.702
EPS = 1e-5
NEG_INF = -1e30
LAMBDA_INIT = 0.8 - 0.6 * math.exp(-0.3 * 0)
LOG2E = 1.4426950408889634

LANES = 128
MXU_DIM = 256
VMEM_LIMIT = 56 * 1024 * 1024

ROW_BLOCK = 512
Q_BLOCK = 1024
EXPERT_BLOCK = 1024
EXPERT_HALF = 512
EXPERT_SUB = 256
WO_BLOCK = 1024
WO_SUB = 256
NA_ROWS_PER_TRIP = 16
VT_ROWS = LANES + 16
KV_CHUNK = 256
NORM_SLACK = 1.01
MAX_SAFE_BOUND = 60.0


PACKED = D_MODEL // 2
SC_CORES = 2
SC_SUBCORES = 16
SC_WORKERS = SC_CORES * SC_SUBCORES
GATHER_ROWS = 64


def _params(*sem):
    return pltpu.CompilerParams(dimension_semantics=sem, vmem_limit_bytes=VMEM_LIMIT)


def _pack_halves(x):
    w = x.shape[1] // 2
    bits = lax.bitcast_convert_type(x, U32)
    return (bits[:, :w] >> 16) | bits[:, w:]


def _col_reduce(x, op):
    while x.shape[0] >= 64:
        x = op(x.reshape(8, x.shape[0] // 8, x.shape[1]), axis=0)
    return op(x, axis=0, keepdims=True)


def _unpack_halves(word):
    lo = lax.bitcast_convert_type(word << 16, F32)
    hi = lax.bitcast_convert_type(word & jnp.uint32(0xFFFF0000), F32)
    return lo, hi


def _ada_kernel(c_ref, w_ref, b_ref, o_ref):
    c = c_ref[...]
    s = c * jax.nn.sigmoid(c)
    o_ref[...] = jnp.dot(s, w_ref[...], preferred_element_type=F32,
                         precision=lax.Precision.HIGHEST) + b_ref[...]


def _ada(c_all, w_ada, b_ada):
    nb = c_all.shape[0]
    n_out = w_ada.shape[1]
    blk = D_MODEL
    return pl.pallas_call(
        _ada_kernel,
        grid=(n_out // blk,),
        in_specs=[pl.BlockSpec((nb, D_MODEL), lambda j: (0, 0)),
                  pl.BlockSpec((D_MODEL, blk), lambda j: (0, j)),
                  pl.BlockSpec((1, blk), lambda j: (0, j))],
        out_specs=pl.BlockSpec((nb, blk), lambda j: (0, j)),
        out_shape=jax.ShapeDtypeStruct((nb, n_out), F32),
        compiler_params=_params("arbitrary"),
        name="ada",
    )(c_all, w_ada, b_ada.reshape(1, n_out))


def _head_sumsq(y, bd):
    sq = (y * y).astype(BF16)
    parts = [jnp.dot(sq[:, c:c + MXU_DIM], bd, preferred_element_type=F32)
             for c in range(0, y.shape[1], MXU_DIM)]
    return jnp.concatenate(parts, axis=1)


def _qkv_kernel(x_ref, ada_ref, g_ref, w_ref, gain_ref, cos_ref, sin_ref, bd_ref, o_ref):
    x = x_ref[...]
    ms = jnp.mean(x * x, axis=-1, keepdims=True)
    xn = x * lax.rsqrt(ms + EPS) * g_ref[...]
    sh = ada_ref[0, 0:1, :]
    sc = ada_ref[0, 1:2, :]
    h = (xn * (1.0 + sc) + sh).astype(BF16)
    bd = bd_ref[...]
    lane = lax.broadcasted_iota(jnp.int32, (x.shape[0], NA_WIDTH), 1)
    first_half = (lane & (HEAD_DIM // 2)) == 0
    for grp in range(6):
        cols = slice(grp * 512, (grp + 1) * 512)
        acc = jnp.dot(h, w_ref[:, cols], preferred_element_type=F32)
        if grp in (2, 5):
            o_ref[:, cols] = acc.astype(BF16)
            continue
        gi = {0: 0, 1: 1, 3: 2, 4: 3}[grp]
        ss = _head_sumsq(acc, bd)
        y = acc * lax.rsqrt(ss * (1.0 / HEAD_DIM) + EPS) * gain_ref[gi:gi + 1, :]
        if grp in (3, 4):
            partner = jnp.where(first_half,
                                pltpu.roll(y, NA_WIDTH - HEAD_DIM // 2, axis=1),
                                pltpu.roll(y, HEAD_DIM // 2, axis=1))
            reps = NA_WIDTH // LANES
            y = (y * jnp.concatenate([cos_ref[...]] * reps, axis=1)
                 + partner * jnp.concatenate([sin_ref[...]] * reps, axis=1))
        o_ref[:, cols] = y.astype(BF16)


def _qkv(x2d, ada_g, g_attn, w_qkv_bf, gains, cos_t, sin_t, bd, seq):
    n = x2d.shape[0]
    tm = ROW_BLOCK
    per_seq = seq // tm
    return pl.pallas_call(
        _qkv_kernel,
        grid=(n // tm,),
        in_specs=[pl.BlockSpec((tm, D_MODEL), lambda i: (i, 0)),
                  pl.BlockSpec((1, 6, D_MODEL), lambda i: (i // per_seq, 0, 0)),
                  pl.BlockSpec((1, D_MODEL), lambda i: (0, 0)),
                  pl.BlockSpec((D_MODEL, QKV_COLS), lambda i: (0, 0)),
                  pl.BlockSpec((4, NA_WIDTH), lambda i: (0, 0)),
                  pl.BlockSpec((tm, LANES), lambda i: (i % per_seq, 0)),
                  pl.BlockSpec((tm, LANES), lambda i: (i % per_seq, 0)),
                  pl.BlockSpec((MXU_DIM, MXU_DIM), lambda i: (0, 0))],
        out_specs=pl.BlockSpec((tm, QKV_COLS), lambda i: (i, 0)),
        out_shape=jax.ShapeDtypeStruct((n, QKV_COLS), BF16),
        compiler_params=_params("arbitrary"),
        name="qkv",
    )(x2d, ada_g, g_attn, w_qkv_bf, gains, cos_t, sin_t, bd)


def _na_kernel(q_ref, k_ref, v_ref, bias_ref, o_ref, vaug_ref, *, rows):
    lane = lax.broadcasted_iota(jnp.int32, (GRID_W, LANES), 1)
    head0 = lane < HEAD_DIM
    win = NA_WIN_ROWS * GRID_W

    def window_start(r):
        return jnp.clip(r - NA_WIN_ROWS // 2, 0, rows - NA_WIN_ROWS)

    def scores(r):
        r_start = window_start(r)
        q = q_ref[pl.ds(pl.multiple_of(r * GRID_W, GRID_W), GRID_W), :]
        kw = k_ref[pl.ds(pl.multiple_of(r_start * GRID_W, GRID_W), win), :]
        zero = jnp.zeros_like(q)
        qm = jnp.concatenate([jnp.where(head0, q, zero), jnp.where(head0, zero, q)], axis=0)
        s = lax.dot_general(kw, qm, (((1,), (1,)), ((), ())), preferred_element_type=F32)
        return s + bias_ref[0, r - r_start]

    vaug_ref[:, :LANES] = v_ref[...]
    vaug_ref[:, LANES:] = jnp.ones((v_ref.shape[0], LANES), BF16)

    def finish(r, s):
        vw = vaug_ref[pl.ds(pl.multiple_of(window_start(r) * GRID_W, GRID_W), win), :]
        m = _col_reduce(s, jnp.max)
        p = jnp.exp2(s - m).astype(BF16)
        o2 = lax.dot_general(p, vw, (((0,), (0,)), ((), ())), preferred_element_type=F32)
        o2 = o2[:, :LANES] * (1.0 / o2[:, LANES:])
        o = jnp.where(head0, o2[:GRID_W], o2[GRID_W:])
        o_ref[pl.ds(pl.multiple_of(r * GRID_W, GRID_W), GRID_W), :] = o.astype(BF16)

    def body(i, carry):
        trip_rows = [i * NA_ROWS_PER_TRIP + u for u in range(NA_ROWS_PER_TRIP)]
        trip_scores = [scores(r) for r in trip_rows]
        for r, s in zip(trip_rows, trip_scores):
            finish(r, s)
        return carry

    lax.fori_loop(0, rows // NA_ROWS_PER_TRIP, body, 0)


def _na(qkv, bias_t, batch, seq):
    rows = seq // GRID_W
    n_pairs = NA_HEADS // 2
    return pl.pallas_call(
        functools.partial(_na_kernel, rows=rows),
        grid=(batch, n_pairs),
        in_specs=[pl.BlockSpec((seq, LANES), lambda b, hp: (b, hp)),
                  pl.BlockSpec((seq, LANES), lambda b, hp: (b, n_pairs + hp)),
                  pl.BlockSpec((seq, LANES), lambda b, hp: (b, 2 * n_pairs + hp)),
                  pl.BlockSpec((1, NA_WIN_ROWS, NA_WIN_ROWS * GRID_W, LANES), lambda b, hp: (hp, 0, 0, 0))],
        out_specs=pl.BlockSpec((seq, LANES), lambda b, hp: (b, hp)),
        out_shape=jax.ShapeDtypeStruct((batch * seq, NA_WIDTH), BF16),
        scratch_shapes=[pltpu.VMEM((seq, 2 * LANES), BF16)],
        compiler_params=_params("arbitrary", "arbitrary"),
        name="na_attn",
    )(qkv, qkv, qkv, bias_t)


def _diff_kernel(q_ref, k_ref, v_ref, lam_ref, g_ref, o_ref, vt_ref, kn_ref, oa_ref):
    @pl.when(pl.program_id(2) == 0)
    def _():
        vt_ref[:LANES, :] = v_ref[...].astype(F32).T.astype(BF16)
        ones_row = lax.broadcasted_iota(jnp.int32, (VT_ROWS - LANES, v_ref.shape[0]), 0) == 0
        vt_ref[LANES:, :] = jnp.where(ones_row, 1.0, 0.0).astype(BF16)
        kf = k_ref[...].astype(F32)
        d_id = lax.broadcasted_iota(jnp.int32, (LANES, LANES), 0) // HEAD_DIM
        c_id = lax.broadcasted_iota(jnp.int32, (LANES, LANES), 1)
        comp_sel = jnp.where(d_id == c_id, 1.0, 0.0).astype(BF16)
        kn2 = jnp.dot((kf * kf).astype(BF16), comp_sel, preferred_element_type=F32)
        kn_ref[...] = jnp.sqrt(_col_reduce(kn2, jnp.max)) * NORM_SLACK

    lq1 = lam_ref[0:1, :]
    lk1 = lam_ref[1:2, :]
    lq2 = lam_ref[2:3, :]
    lk2 = lam_ref[3:4, :]
    lam = (jnp.exp(jnp.sum(lq1 * lk1, axis=-1, keepdims=True))
           - jnp.exp(jnp.sum(lq2 * lk2, axis=-1, keepdims=True)) + LAMBDA_INIT)

    q = q_ref[...]
    lane = lax.broadcasted_iota(jnp.int32, q.shape, 1)
    zero = jnp.zeros_like(q)
    tq = q.shape[0]
    qcat = jnp.concatenate([jnp.where(lane < HEAD_DIM, q, zero), jnp.where(lane < HEAD_DIM, zero, q)], axis=0)
    def scores(c, chunk):
        kc = k_ref[c * chunk:(c + 1) * chunk, :]
        return lax.dot_general(kc, qcat, (((1,), (1,)), ((), ())), preferred_element_type=F32)

    def chunked(chunk, step):
        n_chunks = k_ref.shape[0] // chunk
        s_next = scores(0, chunk)
        state = None
        for c in range(n_chunks):
            s = s_next
            if c + 1 < n_chunks:
                s_next = scores(c + 1, chunk)
            state = step(c, s, vt_ref[:, c * chunk:(c + 1) * chunk], state)
        return state

    qf = qcat.astype(F32)
    ones8 = jnp.ones((8, LANES), BF16)
    qn2 = lax.dot_general(ones8, (qf * qf).astype(BF16), (((1,), (1,)), ((), ())), preferred_element_type=F32)
    col = lax.broadcasted_iota(jnp.int32, (1, 2 * tq), 1)
    bound = jnp.sqrt(qn2[0:1, :]) * NORM_SLACK * jnp.where(col < tq, kn_ref[0:1, 0:1], kn_ref[0:1, 1:2])
    in_range = jnp.max(bound) <= MAX_SAFE_BOUND

    @pl.when(in_range)
    def _():
        def step(c, s, vtc, acc):
            pv = jnp.dot(vtc, jnp.exp2(s - bound).astype(BF16), preferred_element_type=F32)
            return pv if c == 0 else acc + pv
        oa_ref[...] = chunked(KV_CHUNK, step)

    @pl.when(jnp.logical_not(in_range))
    def _():
        def step(c, s, vtc, state):
            mc = _col_reduce(s, jnp.max)
            m_new = mc if c == 0 else jnp.maximum(state[0], mc)
            pv = jnp.dot(vtc, jnp.exp2(s - m_new).astype(BF16), preferred_element_type=F32)
            return (m_new, pv if c == 0 else jnp.exp2(state[0] - m_new) * state[1] + pv)
        oa_ref[...] = chunked(KV_CHUNK, step)[1]

    oa = oa_ref[...]
    o0, l0 = oa[:LANES, :tq], oa[LANES:LANES + 1, :tq]
    o1, l1 = oa[:LANES, tq:], oa[LANES:LANES + 1, tq:]
    o = o0 * (1.0 / l0) - (lam / l1) * o1
    ms = jnp.mean(o * o, axis=0, keepdims=True)
    y = o * lax.rsqrt(ms + EPS) * g_ref[...]
    o_ref[...] = y.T.astype(BF16)


def _diff(qkv, lam_vecs, subln_col, batch, seq):
    tq = Q_BLOCK
    nq = seq // tq
    base = 3 * NA_WIDTH // LANES
    nh = DIFF_HEADS
    return pl.pallas_call(
        _diff_kernel,
        grid=(batch, nh, nq),
        in_specs=[pl.BlockSpec((tq, LANES), lambda b, h, i: (b * nq + i, base + h)),
                  pl.BlockSpec((seq, LANES), lambda b, h, i: (b, base + nh + h)),
                  pl.BlockSpec((seq, LANES), lambda b, h, i: (b, base + 2 * nh + h)),
                  pl.BlockSpec((4, HEAD_DIM), lambda b, h, i: (0, 0)),
                  pl.BlockSpec((LANES, 1), lambda b, h, i: (0, 0))],
        out_specs=pl.BlockSpec((tq, LANES), lambda b, h, i: (b * nq + i, h)),
        out_shape=jax.ShapeDtypeStruct((batch * seq, DIFF_WIDTH), BF16),
        scratch_shapes=[pltpu.VMEM((VT_ROWS, seq), BF16),
                        pltpu.VMEM((1, LANES), F32),
                        pltpu.VMEM((VT_ROWS, 2 * tq), F32)],
        compiler_params=_params("arbitrary", "arbitrary", "arbitrary"),
        name="diff_attn",
    )(qkv, qkv, qkv, lam_vecs, subln_col)


def _wo_kernel(ona_ref, odf_ref, x_ref, ada_ref, wo_ref, g_ref, wrh_ref, wrl_ref, br_ref, tri_ref, cnt0_ref,
               x1_ref, h2_ref, ti_ref, tw_ref, cnt_ref):
    gt1 = ada_ref[0, 2:3, :]
    sh2 = ada_ref[0, 3:4, :]
    sc2 = ada_ref[0, 4:5, :]
    lane = lax.broadcasted_iota(jnp.int32, (WO_SUB, LANES), 1).astype(F32)

    def mix_of(rows):
        return (jnp.dot(ona_ref[rows, :], wo_ref[:NA_WIDTH, :], preferred_element_type=F32)
                + jnp.dot(odf_ref[rows, :], wo_ref[NA_WIDTH:, :], preferred_element_type=F32))

    def route(rows, mix):
        x1 = x_ref[rows, :] + gt1 * mix
        x1_ref[rows, :] = x1
        ms = jnp.mean(x1 * x1, axis=-1, keepdims=True)
        h2 = x1 * lax.rsqrt(ms + EPS) * g_ref[...] * (1.0 + sc2) + sh2
        hi = h2.astype(BF16)
        h2_ref[rows, :] = _pack_halves(hi.astype(F32))
        lo = (h2 - hi.astype(F32)).astype(BF16)
        both = jnp.dot(hi, wrl_ref[...], preferred_element_type=F32)
        cur = (both + pltpu.roll(both, LANES - N_EXPERTS, axis=1)
               + jnp.dot(lo, wrh_ref[...], preferred_element_type=F32)) + br_ref[...]
        vals = []
        idxs = []
        for _ in range(TOP_K):
            m = jnp.max(cur, axis=-1, keepdims=True)
            idx = jnp.min(jnp.where(cur == m, lane, float(LANES)), axis=-1, keepdims=True)
            vals.append(m)
            idxs.append(idx)
            cur = jnp.where(lane == idx, -jnp.inf, cur)
        es = [jnp.exp(v - vals[0]) for v in vals]
        inv = 1.0 / (es[0] + es[1] + es[2] + es[3])
        sel = jnp.zeros((WO_SUB, LANES), F32)
        for j in range(TOP_K):
            sel = sel + jnp.where(lane == idxs[j], 1.0, 0.0)
        return idxs, [e * inv for e in es], sel

    subs = [pl.ds(r, WO_SUB) for r in range(0, x_ref.shape[0], WO_SUB)]
    routed = []
    mix_next = mix_of(subs[0])
    for j, rows in enumerate(subs):
        mix = mix_next
        if j + 1 < len(subs):
            mix_next = mix_of(subs[j + 1])
        routed.append(route(rows, mix))

    @pl.when(pl.program_id(0) == 0)
    def _():
        cnt_ref[...] = cnt0_ref[...]

    sel_all = jnp.concatenate([r[2] for r in routed], axis=0)
    before = jnp.dot(tri_ref[...], sel_all.astype(BF16), preferred_element_type=F32) + cnt_ref[...]
    cnt_ref[...] = cnt_ref[...] + jnp.sum(sel_all, axis=0, keepdims=True)

    for s, rows in enumerate(subs):
        idxs, wts, _ = routed[s]
        bef = before[s * WO_SUB:(s + 1) * WO_SUB]
        ti = jnp.zeros((WO_SUB, LANES), F32)
        tw = jnp.zeros((WO_SUB, LANES), F32)
        for j in range(TOP_K):
            rank = jnp.sum(jnp.where(lane == idxs[j], bef, 0.0), axis=-1, keepdims=True)
            ti = jnp.where(lane == float(j), idxs[j], ti)
            ti = jnp.where(lane == float(TOP_K + j), rank, ti)
            tw = jnp.where(lane == float(j), wts[j], tw)
        ti_ref[:, s * WO_SUB:(s + 1) * WO_SUB] = ti.T[:2 * TOP_K, :].astype(jnp.int32)
        tw_ref[rows, :] = tw


def _wo(o_na, o_df, x2d, ada_g, w_o_bf, g_ffn, wr_hi, wr_lo, br_pad, tri, cnt0, seq):
    n = x2d.shape[0]
    tm = WO_BLOCK
    per_seq = seq // tm
    row = lambda i: (i, 0)
    const = lambda i: (0, 0)
    return pl.pallas_call(
        _wo_kernel,
        grid=(n // tm,),
        in_specs=[pl.BlockSpec((tm, NA_WIDTH), row),
                  pl.BlockSpec((tm, DIFF_WIDTH), row),
                  pl.BlockSpec((tm, D_MODEL), row),
                  pl.BlockSpec((1, 6, D_MODEL), lambda i: (i // per_seq, 0, 0)),
                  pl.BlockSpec((D_MODEL, D_MODEL), const),
                  pl.BlockSpec((1, D_MODEL), const),
                  pl.BlockSpec((D_MODEL, LANES), const),
                  pl.BlockSpec((D_MODEL, LANES), const),
                  pl.BlockSpec((1, LANES), const),
                  pl.BlockSpec((tm, tm), const),
                  pl.BlockSpec((1, LANES), const)],
        out_specs=[pl.BlockSpec((tm, D_MODEL), row),
                   pl.BlockSpec((tm, PACKED), row),
                   pl.BlockSpec((2 * TOP_K, tm), lambda i: (0, i)),
                   pl.BlockSpec((tm, LANES), row),
                   pl.BlockSpec((1, LANES), const)],
        out_shape=[jax.ShapeDtypeStruct((n, D_MODEL), F32),
                   jax.ShapeDtypeStruct((n, PACKED), U32),
                   jax.ShapeDtypeStruct((2 * TOP_K, n), jnp.int32),
                   jax.ShapeDtypeStruct((n, LANES), F32),
                   jax.ShapeDtypeStruct((1, LANES), F32)],
        compiler_params=_params("arbitrary"),
        name="wo_router",
    )(o_na, o_df, x2d, ada_g, w_o_bf, g_ffn, wr_hi, wr_lo, br_pad, tri, cnt0)


def _expert_kernel(be_ref, nxt_ref, fill_ref, na_ref, xs_ref, wg_hbm, bg_ref, wu_hbm, bu_ref, wd_hbm, bd_ref, o_ref,
                   w_f32, w_bf, sems, slot_ref):
    i = pl.program_id(0)
    active = i < na_ref[0]
    expert = be_ref[i]
    new_expert = jnp.logical_or(i == 0, expert != be_ref[jnp.maximum(i - 1, 0)])
    wg_bf, wu_bf, wd_bf = w_bf.at[0], w_bf.at[1], w_bf.at[2]

    def weight_copies(src_expert, slot):
        return [pltpu.make_async_copy(w_hbm.at[src_expert], w_f32.at[slot, j], sems.at[slot, j])
                for j, w_hbm in enumerate((wg_hbm, wu_hbm, wd_hbm))]

    @pl.when(i == 0)
    def _():
        slot_ref[0] = 0
        for cp in weight_copies(expert, 0):
            cp.start()

    @pl.when(jnp.logical_and(active, new_expert))
    def _():
        slot = slot_ref[0]
        nxt = nxt_ref[i]
        for s in range(2):
            @pl.when(slot == s)
            def _():
                @pl.when(nxt >= 0)
                def _():
                    for cp in weight_copies(nxt, 1 - s):
                        cp.start(priority=1)
                for j, cp in enumerate(weight_copies(expert, s)):
                    cp.wait()
                    w_bf[j] = w_f32[s, j].astype(BF16)
        slot_ref[0] = 1 - slot

    def run_half(first_row, n_sub):
        def gate_up(rows):
            x_lo, x_hi = _unpack_halves(xs_ref[rows, :])
            x_lo = x_lo.astype(BF16)
            x_hi = x_hi.astype(BF16)

            def proj(w_bf):
                return (jnp.dot(x_lo, w_bf[:PACKED, :], preferred_element_type=F32)
                        + jnp.dot(x_hi, w_bf[PACKED:, :], preferred_element_type=F32))

            return proj(wg_bf), proj(wu_bf)

        def act_down(rows, gu):
            g = jnp.minimum(gu[0] + bg_ref[0], SWIGLU_LIMIT)
            u = jnp.clip(gu[1] + bu_ref[0], -SWIGLU_LIMIT, SWIGLU_LIMIT)
            act = g * jax.nn.sigmoid(SWIGLU_ALPHA * g) * (u + 1.0)
            out = jnp.dot(act.astype(BF16), wd_bf[...], preferred_element_type=F32) + bd_ref[0]
            o_ref[rows, :] = _pack_halves(out.astype(BF16).astype(F32))

        sub = [pl.ds(first_row + r * EXPERT_SUB, EXPERT_SUB) for r in range(n_sub)]
        gu_next = gate_up(sub[0])
        for j, rows in enumerate(sub):
            gu = gu_next
            if j + 1 < len(sub):
                gu_next = gate_up(sub[j + 1])
            act_down(rows, gu)

    fill = jnp.where(active, fill_ref[i], 0)
    subs_per_half = EXPERT_HALF // EXPERT_SUB

    @pl.when(fill < 2 * subs_per_half)
    def _():
        o_ref[...] = jnp.zeros_like(o_ref)

    for half in range(2):
        for n_sub in range(1, subs_per_half + 1):
            here = fill - half * subs_per_half
            cond = (here >= n_sub) if n_sub == subs_per_half else (here == n_sub)

            @pl.when(cond)
            def _(half=half, n_sub=n_sub):
                run_half(half * EXPERT_HALF, n_sub)


def _experts(block_e, next_e, fill, n_active, xs, wg, bg, wu, bu, wd, bd):
    cap = xs.shape[0]
    n_blocks = cap // EXPERT_BLOCK
    xmap = lambda i, be, nx, h2, na: (jnp.minimum(i, na[0] - 1), 0)
    bmap = lambda i, be, nx, h2, na: (be[i], 0, 0)
    hbm = pl.BlockSpec(memory_space=pl.ANY)
    grid_spec = pltpu.PrefetchScalarGridSpec(
        num_scalar_prefetch=4,
        grid=(n_blocks,),
        in_specs=[pl.BlockSpec((EXPERT_BLOCK, PACKED), xmap),
                  hbm, pl.BlockSpec((1, 1, D_MODEL), bmap),
                  hbm, pl.BlockSpec((1, 1, D_MODEL), bmap),
                  hbm, pl.BlockSpec((1, 1, D_MODEL), bmap)],
        out_specs=pl.BlockSpec((EXPERT_BLOCK, PACKED), lambda i, be, nx, h2, na: (i, 0)),
        scratch_shapes=[pltpu.VMEM((2, 3, D_MODEL, D_MODEL), F32),
                        pltpu.VMEM((3, D_MODEL, D_MODEL), BF16),
                        pltpu.SemaphoreType.DMA((2, 3)),
                        pltpu.SMEM((1,), jnp.int32)],
    )
    return pl.pallas_call(
        _expert_kernel,
        grid_spec=grid_spec,
        out_shape=jax.ShapeDtypeStruct((cap, PACKED), U32),
        compiler_params=_params("arbitrary"),
        name="experts",
    )(block_e, next_e, fill, n_active, xs, wg, bg, wu, bu, wd, bd)


def _sc_gather(table, idx):
    n_out = idx.shape[0]
    width = table.shape[1]
    per_worker = n_out // SC_WORKERS
    n_chunks = per_worker // GATHER_ROWS
    assert per_worker * SC_WORKERS == n_out and n_chunks * GATHER_ROWS == per_worker and n_chunks % 2 == 0
    idx3 = idx.reshape(SC_WORKERS, n_chunks, GATHER_ROWS)
    mesh = plsc.VectorSubcoreMesh(core_axis_name="core", subcore_axis_name="subcore")

    @functools.partial(
        pl.kernel, mesh=mesh,
        out_type=jax.ShapeDtypeStruct((n_out, width), table.dtype),
        scratch_types=[pltpu.VMEM((n_chunks, GATHER_ROWS), jnp.int32),
                       pltpu.VMEM((2, GATHER_ROWS, width), table.dtype),
                       pltpu.SemaphoreType.DMA((2,)),
                       pltpu.SemaphoreType.DMA((2,))])
    def gather_kernel(table_hbm, idx_hbm, out_hbm, idx_v, rows_v, gsem, wsem):
        wid = lax.axis_index("subcore") * SC_CORES + lax.axis_index("core")
        base = wid * per_worker
        pltpu.sync_copy(idx_hbm.at[wid], idx_v)

        def gather(j, slot):
            return pltpu.make_async_copy(table_hbm.at[idx_v.at[j]], rows_v.at[slot], gsem.at[slot])

        def write(j, slot):
            dst = out_hbm.at[pl.ds(pl.multiple_of(base + j * GATHER_ROWS, GATHER_ROWS), GATHER_ROWS)]
            return pltpu.make_async_copy(rows_v.at[slot], dst, wsem.at[slot])

        gather(0, 0).start()

        @pl.loop(0, n_chunks, step=2)
        def _(j):
            for slot in range(2):
                jj = j + slot
                gather(jj, slot).wait()

                @pl.when(jj >= 1)
                def _():
                    write(jj - 1, 1 - slot).wait()

                @pl.when(jj + 1 < n_chunks)
                def _():
                    gather(jj + 1, 1 - slot).start()

                write(jj, slot).start()

        write(n_chunks - 1, 1).wait()

    return gather_kernel(table, idx3)


def _sc_scatter(rows, idx, n_out):
    n_src, width = rows.shape
    n_idx = idx.shape[0]
    per_worker = n_idx // SC_WORKERS
    n_chunks = per_worker // GATHER_ROWS
    assert per_worker * SC_WORKERS == n_idx and n_chunks * GATHER_ROWS == per_worker and n_chunks % 2 == 0
    assert n_src % per_worker == 0
    idx3 = idx.reshape(SC_WORKERS, n_chunks, GATHER_ROWS)
    mesh = plsc.VectorSubcoreMesh(core_axis_name="core", subcore_axis_name="subcore")

    @functools.partial(
        pl.kernel, mesh=mesh,
        out_type=jax.ShapeDtypeStruct((n_out, width), rows.dtype),
        scratch_types=[pltpu.VMEM((n_chunks, GATHER_ROWS), jnp.int32),
                       pltpu.VMEM((2, GATHER_ROWS, width), rows.dtype),
                       pltpu.SemaphoreType.DMA((2,)),
                       pltpu.SemaphoreType.DMA((2,))])
    def scatter_kernel(rows_hbm, idx_hbm, out_hbm, idx_v, rows_v, rsem, wsem):
        wid = lax.axis_index("subcore") * SC_CORES + lax.axis_index("core")
        base = lax.rem(wid * per_worker, n_src)
        pltpu.sync_copy(idx_hbm.at[wid], idx_v)

        def read(j, slot):
            src = rows_hbm.at[pl.ds(pl.multiple_of(base + j * GATHER_ROWS, GATHER_ROWS), GATHER_ROWS)]
            return pltpu.make_async_copy(src, rows_v.at[slot], rsem.at[slot])

        def write(j, slot):
            return pltpu.make_async_copy(rows_v.at[slot], out_hbm.at[idx_v.at[j]], wsem.at[slot])

        read(0, 0).start()

        @pl.loop(0, n_chunks, step=2)
        def _(j):
            for slot in range(2):
                jj = j + slot
                read(jj, slot).wait()

                @pl.when(jj >= 1)
                def _():
                    write(jj - 1, 1 - slot).wait()

                @pl.when(jj + 1 < n_chunks)
                def _():
                    read(jj + 1, 1 - slot).start()

                write(jj, slot).start()

        write(n_chunks - 1, 1).wait()

    return scatter_kernel(rows, idx3)


def _combine_kernel(x1_ref, y0_ref, y1_ref, y2_ref, y3_ref, tw_ref, ada_ref, o_ref):
    tw = tw_ref[...]
    acc_lo = jnp.zeros((x1_ref.shape[0], PACKED), F32)
    acc_hi = jnp.zeros((x1_ref.shape[0], PACKED), F32)
    for j, y_ref in enumerate((y0_ref, y1_ref, y2_ref, y3_ref)):
        lo, hi = _unpack_halves(y_ref[...])
        acc_lo = acc_lo + tw[:, j:j + 1] * lo
        acc_hi = acc_hi + tw[:, j:j + 1] * hi
    o_ref[:, :PACKED] = x1_ref[:, :PACKED] + ada_ref[0, 5:6, :PACKED] * acc_lo
    o_ref[:, PACKED:] = x1_ref[:, PACKED:] + ada_ref[0, 5:6, PACKED:] * acc_hi


def _combine(x1, ys, tw, ada_g, seq, row_off):
    n = x1.shape[0]
    tm = ROW_BLOCK
    per_seq = seq // tm
    off = row_off // tm
    per_choice = tw.shape[0] // tm
    y_specs = [pl.BlockSpec((tm, PACKED), functools.partial(lambda i, j: (j * per_choice + off + i, 0), j=j))
               for j in range(TOP_K)]
    return pl.pallas_call(
        _combine_kernel,
        grid=(n // tm,),
        in_specs=[pl.BlockSpec((tm, D_MODEL), lambda i: (i, 0)),
                  *y_specs,
                  pl.BlockSpec((tm, LANES), lambda i: (i + off, 0)),
                  pl.BlockSpec((1, 6, D_MODEL), lambda i: (i // per_seq, 0, 0))],
        out_specs=pl.BlockSpec((tm, D_MODEL), lambda i: (i, 0)),
        out_shape=jax.ShapeDtypeStruct((n, D_MODEL), F32),
        compiler_params=_params("arbitrary"),
        name="combine",
    )(x1, ys, ys, ys, ys, tw, ada_g)


def _rope_tables(seq):
    half = HEAD_DIM // 2
    inv = ROPE_THETA ** (-jnp.arange(half, dtype=F32) / half)
    ang = jnp.arange(seq, dtype=F32)[:, None] * inv[None, :]
    cos, sin = jnp.cos(ang), jnp.sin(ang)
    cos_h = jnp.concatenate([cos, cos], axis=-1)
    sin_h = jnp.concatenate([-sin, sin], axis=-1)
    reps = LANES // HEAD_DIM
    return jnp.tile(cos_h, (1, reps)), jnp.tile(sin_h, (1, reps))


def _na_bias_table(rpb):
    cols = jnp.arange(GRID_W, dtype=jnp.int32)
    c_start = jnp.clip(cols - NA_WIN_COLS // 2, 0, GRID_W - NA_WIN_COLS)
    col_mask = (cols[None, :] >= c_start[:, None]) & (cols[None, :] < c_start[:, None] + NA_WIN_COLS)
    col_idx = jnp.clip(cols[None, :] - cols[:, None], -(NA_WIN_COLS - 1), NA_WIN_COLS - 1) + NA_WIN_COLS - 1
    delta = jnp.arange(NA_WIN_ROWS, dtype=jnp.int32)
    j = jnp.arange(NA_WIN_ROWS, dtype=jnp.int32)
    row_idx = j[None, :] - delta[:, None] + NA_WIN_ROWS - 1
    row_hot = (row_idx[:, :, None] == jnp.arange(2 * NA_WIN_ROWS - 1, dtype=jnp.int32)).astype(F32)
    col_hot = (col_idx[:, :, None] == jnp.arange(2 * NA_WIN_COLS - 1, dtype=jnp.int32)).astype(F32)
    rpb_pairs = rpb.astype(F32).reshape(NA_HEADS // 2, 2, 2 * NA_WIN_ROWS - 1, 2 * NA_WIN_COLS - 1)
    pair_hot = jnp.eye(2, dtype=F32)
    lane_hot = jnp.einsum('qkc,hx->khqxc', col_hot, pair_hot).reshape(GRID_W, LANES, 2, 2 * NA_WIN_COLS - 1)
    bias = jnp.einsum('djr,pxrc,klxc->pdjkl', row_hot, rpb_pairs, lane_hot,
                      precision=lax.Precision.HIGHEST)
    lane_mask = jnp.concatenate([col_mask.T, col_mask.T], axis=1)
    bias = jnp.where(lane_mask[None, None, None], bias * LOG2E, NEG_INF)
    return bias.reshape(NA_HEADS // 2, NA_WIN_ROWS, NA_WIN_ROWS * GRID_W, LANES)


def _routing(top_idx, rank, counts, n):
    n_blocks = n * TOP_K // EXPERT_BLOCK + N_EXPERTS
    experts = jnp.arange(N_EXPERTS, dtype=jnp.int32)
    padded = (counts + EXPERT_BLOCK - 1) // EXPERT_BLOCK * EXPERT_BLOCK
    pad_end = jnp.cumsum(padded)
    pad_start = pad_end - padded
    start_of = jnp.sum(jnp.where(top_idx[None] == experts[:, None, None], pad_start[:, None, None], 0), axis=0)
    dest = (start_of + rank).reshape(-1)
    block_lo = jnp.arange(n_blocks, dtype=jnp.int32) * EXPERT_BLOCK
    block_e = jnp.minimum(jnp.sum((pad_end[None, :] <= block_lo[:, None]).astype(jnp.int32), axis=1),
                          N_EXPERTS - 1).astype(jnp.int32)
    n_active = (pad_end[-1] // EXPERT_BLOCK).astype(jnp.int32).reshape(1)
    later = jnp.where((experts[None, :] > experts[:, None]) & (padded[None, :] > 0), experts[None, :], N_EXPERTS)
    next_nonempty = jnp.min(later, axis=1)
    next_nonempty = jnp.where(next_nonempty == N_EXPERTS, -1, next_nonempty)
    own = block_e[:, None] == experts[None, :]
    next_e = jnp.sum(jnp.where(own, next_nonempty[None, :], 0), axis=1)
    real_end = jnp.sum(jnp.where(own, (pad_start + counts)[None, :], 0), axis=1)
    fill = jnp.clip((real_end - block_lo + EXPERT_SUB - 1) // EXPERT_SUB, 0, EXPERT_BLOCK // EXPERT_SUB)
    return dest, block_e, next_e.astype(jnp.int32), fill.astype(jnp.int32), n_active, n_blocks * EXPERT_BLOCK


def kernel(x_prompt, x_sample, c_prompt, c_sample, w_ada, b_ada, g_attn_norm, w_qkv, na_q_norm, na_k_norm, na_rpb, diff_q_norm, diff_k_norm, lambda_q1, lambda_k1, lambda_q2, lambda_k2, diff_subln, w_o, g_ffn_norm, w_router, b_router, w_gate, b_gate, w_up, b_up, w_down, b_down):
    l = 0
    groups = [(x_prompt, c_prompt), (x_sample, c_sample)]
    nb = [x.shape[0] for x, _ in groups]

    ada_all = _ada(jnp.concatenate([c for _, c in groups], axis=0), w_ada[l], b_ada[l])
    ada_all = ada_all.reshape(sum(nb), 6, D_MODEL)

    w_qkv_bf = w_qkv[l].astype(BF16)
    w_o_bf = w_o[l].astype(BF16)
    scale = HEAD_DIM ** -0.5
    reps = NA_WIDTH // HEAD_DIM
    gains = jnp.stack([jnp.tile(na_q_norm[l], reps) * (scale * LOG2E),
                       jnp.tile(na_k_norm[l], reps),
                       jnp.tile(diff_q_norm[l], reps) * (scale * LOG2E),
                       jnp.tile(diff_k_norm[l], reps)]).astype(F32)
    head_id = jnp.arange(MXU_DIM, dtype=jnp.int32) // HEAD_DIM
    bd = (head_id[:, None] == head_id[None, :]).astype(BF16)
    bias_t = _na_bias_table(na_rpb[l])
    lam_vecs = jnp.stack([lambda_q1[l], lambda_k1[l], lambda_q2[l], lambda_k2[l]]).astype(F32)
    subln_col = (diff_subln[l].astype(F32) * (1.0 - LAMBDA_INIT)).reshape(LANES, 1)
    wr = w_router[l].astype(F32)
    wr_pad = jnp.zeros((D_MODEL, LANES), F32).at[:, :N_EXPERTS].set(wr)
    wr_hi = wr_pad.astype(BF16)
    wr_lo = (wr_pad - wr_hi.astype(F32)).astype(BF16)
    wr_lo = wr_hi.at[:, N_EXPERTS:2 * N_EXPERTS].set(wr_lo[:, :N_EXPERTS])
    br_pad = jnp.full((1, LANES), NEG_INF, F32).at[0, :N_EXPERTS].set(b_router[l].astype(F32))
    g_attn = g_attn_norm[l].reshape(1, D_MODEL).astype(F32)
    g_ffn = g_ffn_norm[l].reshape(1, D_MODEL).astype(F32)
    max_seq = max(x.shape[1] for x, _ in groups)
    cos_t, sin_t = _rope_tables(max_seq)

    rows = lax.broadcasted_iota(jnp.int32, (WO_BLOCK, WO_BLOCK), 0)
    cols = lax.broadcasted_iota(jnp.int32, (WO_BLOCK, WO_BLOCK), 1)
    tri = (cols < rows).astype(BF16)
    cnt0 = jnp.zeros((1, LANES), F32)

    bg = b_gate[l].reshape(N_EXPERTS, 1, D_MODEL).astype(F32)
    bu = b_up[l].reshape(N_EXPERTS, 1, D_MODEL).astype(F32)
    bdn = b_down[l].reshape(N_EXPERTS, 1, D_MODEL).astype(F32)
    ada_groups = [ada_all[:nb[0]], ada_all[nb[0]:]]

    order = sorted(range(len(groups)), key=lambda g: -groups[g][0].shape[1])
    staged = {}
    for g in order:
        x = groups[g][0]
        b, seq = x.shape[0], x.shape[1]
        n = b * seq
        x2d = x.reshape(n, D_MODEL)
        ada_g = ada_groups[g]
        qkv = _qkv(x2d, ada_g, g_attn, w_qkv_bf, gains, cos_t, sin_t, bd, seq)
        o_na = _na(qkv, bias_t, b, seq)
        o_df = _diff(qkv, lam_vecs, subln_col, b, seq)
        x1, h2, ti, tw, cnt = _wo(o_na, o_df, x2d, ada_g, w_o_bf, g_ffn, wr_hi, wr_lo, br_pad, tri, cnt0, seq)
        counts = cnt[0, :N_EXPERTS].astype(jnp.int32)
        dest, *blocks, cap = _routing(ti[:TOP_K], ti[TOP_K:], counts, n)
        xs = _sc_scatter(h2, dest, cap)
        staged[g] = (x1, tw, ada_g, dest, blocks, xs, seq, b)

    sorted_out = {}
    for g in order:
        blocks, xs = staged[g][4], staged[g][5]
        sorted_out[g] = _experts(*blocks, xs, w_gate[l], bg, w_up[l], bu, w_down[l], bdn)

    outs = [None] * len(groups)
    for g in order:
        x1, tw, ada_g, dest, blocks, xs, seq, b = staged[g]
        ys = _sc_gather(sorted_out[g], dest)
        outs[g] = _combine(x1, ys, tw, ada_g, seq, 0).reshape(b, seq, D_MODEL)
    return tuple(outs)
```

```python
import functools
import math

import jax
import jax.numpy as jnp
from jax import lax
from jax.experimental import pallas as pl
from jax.experimental.pallas import tpu as pltpu
from jax.experimental.pallas import tpu_sc as plsc

F32 = jnp.float32
BF16 = jnp.bfloat16
U32 = jnp.uint32

D_MODEL = 1024
HEAD_DIM = 64
NA_HEADS = 8
NA_WIDTH = 512
DIFF_HEADS = 4
DIFF_WIDTH = 512
QKV_COLS = 3072
GRID_W = 64
NA_WIN_ROWS = 8
NA_WIN_COLS = 16
ROPE_THETA = 10000.0
N_EXPERTS = 32
TOP_K = 4
SWIGLU_LIMIT = 7.0
SWIGLU_ALPHA = 1.702
EPS = 1e-5
NEG_INF = -1e30
LAMBDA_INIT = 0.8 - 0.6 * math.exp(-0.3 * 0)
LOG2E = 1.4426950408889634

LANES = 128
MXU_DIM = 256
VMEM_LIMIT = 56 * 1024 * 1024

ROW_BLOCK = 512
QKV_BLOCK = 1024
Q_BLOCK = 1024
EXPERT_BLOCK = 1024
EXPERT_HALF = 512
EXPERT_SUB = 256
WO_BLOCK = 1024
WO_SUB = 256
NA_ROWS_PER_TRIP = 16
VT_ROWS = LANES + 16
KV_CHUNK = 256
NORM_SLACK = 1.01
MAX_SAFE_BOUND = 60.0


PACKED = D_MODEL // 2
SC_CORES = 2
SC_SUBCORES = 16
SC_WORKERS = SC_CORES * SC_SUBCORES
GATHER_ROWS = 64


def _params(*sem):
    return pltpu.CompilerParams(dimension_semantics=sem, vmem_limit_bytes=VMEM_LIMIT)


def _pack_halves(x):
    w = x.shape[1] // 2
    bits = lax.bitcast_convert_type(x, U32)
    return (bits[:, :w] >> 16) | bits[:, w:]


def _col_reduce(x, op):
    while x.shape[0] >= 64:
        x = op(x.reshape(8, x.shape[0] // 8, x.shape[1]), axis=0)
    return op(x, axis=0, keepdims=True)


def _unpack_halves(word):
    lo = lax.bitcast_convert_type(word << 16, F32)
    hi = lax.bitcast_convert_type(word & jnp.uint32(0xFFFF0000), F32)
    return lo, hi


def _ada_kernel(c_ref, w_ref, b_ref, o_ref):
    c = c_ref[...]
    s = c * jax.nn.sigmoid(c)
    o_ref[...] = jnp.dot(s, w_ref[...], preferred_element_type=F32,
                         precision=lax.Precision.HIGHEST) + b_ref[...]


def _ada(c_all, w_ada, b_ada):
    nb = c_all.shape[0]
    n_out = w_ada.shape[1]
    blk = D_MODEL
    return pl.pallas_call(
        _ada_kernel,
        grid=(n_out // blk,),
        in_specs=[pl.BlockSpec((nb, D_MODEL), lambda j: (0, 0)),
                  pl.BlockSpec((D_MODEL, blk), lambda j: (0, j)),
                  pl.BlockSpec((1, blk), lambda j: (0, j))],
        out_specs=pl.BlockSpec((nb, blk), lambda j: (0, j)),
        out_shape=jax.ShapeDtypeStruct((nb, n_out), F32),
        compiler_params=_params("arbitrary"),
        name="ada",
    )(c_all, w_ada, b_ada.reshape(1, n_out))


def _head_sumsq(y, bd):
    sq = (y * y).astype(BF16)
    parts = [jnp.dot(sq[:, c:c + MXU_DIM], bd, preferred_element_type=F32)
             for c in range(0, y.shape[1], MXU_DIM)]
    return jnp.concatenate(parts, axis=1)


def _qkv_kernel(x_ref, ada_ref, g_ref, w_ref, gain_ref, cos_ref, sin_ref, bd_ref, o_ref):
    x = x_ref[...]
    ms = jnp.mean(x * x, axis=-1, keepdims=True)
    xn = x * lax.rsqrt(ms + EPS) * g_ref[...]
    sh = ada_ref[0, 0:1, :]
    sc = ada_ref[0, 1:2, :]
    h = (xn * (1.0 + sc) + sh).astype(BF16)
    bd = bd_ref[...]
    lane = lax.broadcasted_iota(jnp.int32, (x.shape[0], NA_WIDTH), 1)
    first_half = (lane & (HEAD_DIM // 2)) == 0
    for grp in range(6):
        cols = slice(grp * 512, (grp + 1) * 512)
        acc = jnp.dot(h, w_ref[:, cols], preferred_element_type=F32)
        if grp in (2, 5):
            o_ref[:, cols] = acc.astype(BF16)
            continue
        gi = {0: 0, 1: 1, 3: 2, 4: 3}[grp]
        ss = _head_sumsq(acc, bd)
        y = acc * lax.rsqrt(ss * (1.0 / HEAD_DIM) + EPS) * gain_ref[gi:gi + 1, :]
        if grp in (3, 4):
            partner = jnp.where(first_half,
                                pltpu.roll(y, NA_WIDTH - HEAD_DIM // 2, axis=1),
                                pltpu.roll(y, HEAD_DIM // 2, axis=1))
            reps = NA_WIDTH // LANES
            y = (y * jnp.concatenate([cos_ref[...]] * reps, axis=1)
                 + partner * jnp.concatenate([sin_ref[...]] * reps, axis=1))
        o_ref[:, cols] = y.astype(BF16)


def _qkv(x2d, ada_g, g_attn, w_qkv_bf, gains, cos_t, sin_t, bd, seq):
    n = x2d.shape[0]
    tm = QKV_BLOCK
    per_seq = seq // tm
    return pl.pallas_call(
        _qkv_kernel,
        grid=(n // tm,),
        in_specs=[pl.BlockSpec((tm, D_MODEL), lambda i: (i, 0)),
                  pl.BlockSpec((1, 6, D_MODEL), lambda i: (i // per_seq, 0, 0)),
                  pl.BlockSpec((1, D_MODEL), lambda i: (0, 0)),
                  pl.BlockSpec((D_MODEL, QKV_COLS), lambda i: (0, 0)),
                  pl.BlockSpec((4, NA_WIDTH), lambda i: (0, 0)),
                  pl.BlockSpec((tm, LANES), lambda i: (i % per_seq, 0)),
                  pl.BlockSpec((tm, LANES), lambda i: (i % per_seq, 0)),
                  pl.BlockSpec((MXU_DIM, MXU_DIM), lambda i: (0, 0))],
        out_specs=pl.BlockSpec((tm, QKV_COLS), lambda i: (i, 0)),
        out_shape=jax.ShapeDtypeStruct((n, QKV_COLS), BF16),
        compiler_params=_params("arbitrary"),
        name="qkv",
    )(x2d, ada_g, g_attn, w_qkv_bf, gains, cos_t, sin_t, bd)


def _na_kernel(q_ref, k_ref, v_ref, bias_ref, o_ref, vaug_ref, *, rows):
    lane = lax.broadcasted_iota(jnp.int32, (GRID_W, LANES), 1)
    head0 = lane < HEAD_DIM
    win = NA_WIN_ROWS * GRID_W

    def window_start(r):
        return jnp.clip(r - NA_WIN_ROWS // 2, 0, rows - NA_WIN_ROWS)

    def scores(r):
        r_start = window_start(r)
        q = q_ref[pl.ds(pl.multiple_of(r * GRID_W, GRID_W), GRID_W), :]
        kw = k_ref[pl.ds(pl.multiple_of(r_start * GRID_W, GRID_W), win), :]
        zero = jnp.zeros_like(q)
        qm = jnp.concatenate([jnp.where(head0, q, zero), jnp.where(head0, zero, q)], axis=0)
        s = lax.dot_general(kw, qm, (((1,), (1,)), ((), ())), preferred_element_type=F32)
        return s + bias_ref[0, r - r_start]

    vaug_ref[:, :LANES] = v_ref[...]
    vaug_ref[:, LANES:] = jnp.ones((v_ref.shape[0], LANES), BF16)

    def finish(r, s):
        vw = vaug_ref[pl.ds(pl.multiple_of(window_start(r) * GRID_W, GRID_W), win), :]
        m = _col_reduce(s, jnp.max)
        p = jnp.exp2(s - m).astype(BF16)
        o2 = lax.dot_general(p, vw, (((0,), (0,)), ((), ())), preferred_element_type=F32)
        o2 = o2[:, :LANES] * (1.0 / o2[:, LANES:])
        o = jnp.where(head0, o2[:GRID_W], o2[GRID_W:])
        o_ref[pl.ds(pl.multiple_of(r * GRID_W, GRID_W), GRID_W), :] = o.astype(BF16)

    def body(i, carry):
        trip_rows = [i * NA_ROWS_PER_TRIP + u for u in range(NA_ROWS_PER_TRIP)]
        trip_scores = [scores(r) for r in trip_rows]
        for r, s in zip(trip_rows, trip_scores):
            finish(r, s)
        return carry

    lax.fori_loop(0, rows // NA_ROWS_PER_TRIP, body, 0)


def _na(qkv, bias_t, batch, seq):
    rows = seq // GRID_W
    n_pairs = NA_HEADS // 2
    return pl.pallas_call(
        functools.partial(_na_kernel, rows=rows),
        grid=(batch, n_pairs),
        in_specs=[pl.BlockSpec((seq, LANES), lambda b, hp: (b, hp)),
                  pl.BlockSpec((seq, LANES), lambda b, hp: (b, n_pairs + hp)),
                  pl.BlockSpec((seq, LANES), lambda b, hp: (b, 2 * n_pairs + hp)),
                  pl.BlockSpec((1, NA_WIN_ROWS, NA_WIN_ROWS * GRID_W, LANES), lambda b, hp: (hp, 0, 0, 0))],
        out_specs=pl.BlockSpec((seq, LANES), lambda b, hp: (b, hp)),
        out_shape=jax.ShapeDtypeStruct((batch * seq, NA_WIDTH), BF16),
        scratch_shapes=[pltpu.VMEM((seq, 2 * LANES), BF16)],
        compiler_params=_params("arbitrary", "arbitrary"),
        name="na_attn",
    )(qkv, qkv, qkv, bias_t)


def _diff_kernel(q_ref, k_ref, v_ref, lam_ref, g_ref, o_ref, vt_ref, kn_ref, oa_ref):
    @pl.when(pl.program_id(2) == 0)
    def _():
        vt_ref[:LANES, :] = v_ref[...].astype(F32).T.astype(BF16)
        ones_row = lax.broadcasted_iota(jnp.int32, (VT_ROWS - LANES, v_ref.shape[0]), 0) == 0
        vt_ref[LANES:, :] = jnp.where(ones_row, 1.0, 0.0).astype(BF16)
        kf = k_ref[...].astype(F32)
        d_id = lax.broadcasted_iota(jnp.int32, (LANES, LANES), 0) // HEAD_DIM
        c_id = lax.broadcasted_iota(jnp.int32, (LANES, LANES), 1)
        comp_sel = jnp.where(d_id == c_id, 1.0, 0.0).astype(BF16)
        kn2 = jnp.dot((kf * kf).astype(BF16), comp_sel, preferred_element_type=F32)
        kn_ref[...] = jnp.sqrt(_col_reduce(kn2, jnp.max)) * NORM_SLACK

    lq1 = lam_ref[0:1, :]
    lk1 = lam_ref[1:2, :]
    lq2 = lam_ref[2:3, :]
    lk2 = lam_ref[3:4, :]
    lam = (jnp.exp(jnp.sum(lq1 * lk1, axis=-1, keepdims=True))
           - jnp.exp(jnp.sum(lq2 * lk2, axis=-1, keepdims=True)) + LAMBDA_INIT)

    q = q_ref[...]
    lane = lax.broadcasted_iota(jnp.int32, q.shape, 1)
    zero = jnp.zeros_like(q)
    tq = q.shape[0]
    qcat = jnp.concatenate([jnp.where(lane < HEAD_DIM, q, zero), jnp.where(lane < HEAD_DIM, zero, q)], axis=0)
    def scores(c, chunk):
        kc = k_ref[c * chunk:(c + 1) * chunk, :]
        return lax.dot_general(kc, qcat, (((1,), (1,)), ((), ())), preferred_element_type=F32)

    def chunked(chunk, step):
        n_chunks = k_ref.shape[0] // chunk
        s_next = scores(0, chunk)
        state = None
        for c in range(n_chunks):
            s = s_next
            if c + 1 < n_chunks:
                s_next = scores(c + 1, chunk)
            state = step(c, s, vt_ref[:, c * chunk:(c + 1) * chunk], state)
        return state

    qf = qcat.astype(F32)
    ones8 = jnp.ones((8, LANES), BF16)
    qn2 = lax.dot_general(ones8, (qf * qf).astype(BF16), (((1,), (1,)), ((), ())), preferred_element_type=F32)
    col = lax.broadcasted_iota(jnp.int32, (1, 2 * tq), 1)
    bound = jnp.sqrt(qn2[0:1, :]) * NORM_SLACK * jnp.where(col < tq, kn_ref[0:1, 0:1], kn_ref[0:1, 1:2])
    in_range = jnp.max(bound) <= MAX_SAFE_BOUND

    @pl.when(in_range)
    def _():
        def step(c, s, vtc, acc):
            pv = jnp.dot(vtc, jnp.exp2(s - bound).astype(BF16), preferred_element_type=F32)
            return pv if c == 0 else acc + pv
        oa_ref[...] = chunked(KV_CHUNK, step)

    @pl.when(jnp.logical_not(in_range))
    def _():
        def step(c, s, vtc, state):
            mc = _col_reduce(s, jnp.max)
            m_new = mc if c == 0 else jnp.maximum(state[0], mc)
            pv = jnp.dot(vtc, jnp.exp2(s - m_new).astype(BF16), preferred_element_type=F32)
            return (m_new, pv if c == 0 else jnp.exp2(state[0] - m_new) * state[1] + pv)
        oa_ref[...] = chunked(KV_CHUNK, step)[1]

    oa = oa_ref[...]
    o0, l0 = oa[:LANES, :tq], oa[LANES:LANES + 1, :tq]
    o1, l1 = oa[:LANES, tq:], oa[LANES:LANES + 1, tq:]
    o = o0 * (1.0 / l0) - (lam / l1) * o1
    ms = jnp.mean(o * o, axis=0, keepdims=True)
    y = o * lax.rsqrt(ms + EPS) * g_ref[...]
    o_ref[...] = y.T.astype(BF16)


def _diff(qkv, lam_vecs, subln_col, batch, seq):
    tq = Q_BLOCK
    nq = seq // tq
    base = 3 * NA_WIDTH // LANES
    nh = DIFF_HEADS
    return pl.pallas_call(
        _diff_kernel,
        grid=(batch, nh, nq),
        in_specs=[pl.BlockSpec((tq, LANES), lambda b, h, i: (b * nq + i, base + h)),
                  pl.BlockSpec((seq, LANES), lambda b, h, i: (b, base + nh + h)),
                  pl.BlockSpec((seq, LANES), lambda b, h, i: (b, base + 2 * nh + h)),
                  pl.BlockSpec((4, HEAD_DIM), lambda b, h, i: (0, 0)),
                  pl.BlockSpec((LANES, 1), lambda b, h, i: (0, 0))],
        out_specs=pl.BlockSpec((tq, LANES), lambda b, h, i: (b * nq + i, h)),
        out_shape=jax.ShapeDtypeStruct((batch * seq, DIFF_WIDTH), BF16),
        scratch_shapes=[pltpu.VMEM((VT_ROWS, seq), BF16),
                        pltpu.VMEM((1, LANES), F32),
                        pltpu.VMEM((VT_ROWS, 2 * tq), F32)],
        compiler_params=_params("arbitrary", "arbitrary", "arbitrary"),
        name="diff_attn",
    )(qkv, qkv, qkv, lam_vecs, subln_col)


def _wo_kernel(ona_ref, odf_ref, x_ref, ada_ref, wo_ref, g_ref, wrh_ref, wrl_ref, br_ref, tri_ref, cnt0_ref,
               x1_ref, h2_ref, ti_ref, tw_ref, cnt_ref):
    gt1 = ada_ref[0, 2:3, :]
    sh2 = ada_ref[0, 3:4, :]
    sc2 = ada_ref[0, 4:5, :]
    lane = lax.broadcasted_iota(jnp.int32, (WO_SUB, LANES), 1).astype(F32)

    def mix_of(rows):
        return (jnp.dot(ona_ref[rows, :], wo_ref[:NA_WIDTH, :], preferred_element_type=F32)
                + jnp.dot(odf_ref[rows, :], wo_ref[NA_WIDTH:, :], preferred_element_type=F32))

    def route(rows, mix):
        x1 = x_ref[rows, :] + gt1 * mix
        x1_ref[rows, :] = x1
        ms = jnp.mean(x1 * x1, axis=-1, keepdims=True)
        h2 = x1 * lax.rsqrt(ms + EPS) * g_ref[...] * (1.0 + sc2) + sh2
        hi = h2.astype(BF16)
        h2_ref[rows, :] = _pack_halves(hi.astype(F32))
        lo = (h2 - hi.astype(F32)).astype(BF16)
        both = jnp.dot(hi, wrl_ref[...], preferred_element_type=F32)
        cur = (both + pltpu.roll(both, LANES - N_EXPERTS, axis=1)
               + jnp.dot(lo, wrh_ref[...], preferred_element_type=F32)) + br_ref[...]
        vals = []
        idxs = []
        for _ in range(TOP_K):
            m = jnp.max(cur, axis=-1, keepdims=True)
            idx = jnp.min(jnp.where(cur == m, lane, float(LANES)), axis=-1, keepdims=True)
            vals.append(m)
            idxs.append(idx)
            cur = jnp.where(lane == idx, -jnp.inf, cur)
        es = [jnp.exp(v - vals[0]) for v in vals]
        inv = 1.0 / (es[0] + es[1] + es[2] + es[3])
        sel = jnp.zeros((WO_SUB, LANES), F32)
        for j in range(TOP_K):
            sel = sel + jnp.where(lane == idxs[j], 1.0, 0.0)
        return idxs, [e * inv for e in es], sel

    subs = [pl.ds(r, WO_SUB) for r in range(0, x_ref.shape[0], WO_SUB)]
    routed = []
    mix_next = mix_of(subs[0])
    for j, rows in enumerate(subs):
        mix = mix_next
        if j + 1 < len(subs):
            mix_next = mix_of(subs[j + 1])
        routed.append(route(rows, mix))

    @pl.when(pl.program_id(0) == 0)
    def _():
        cnt_ref[...] = cnt0_ref[...]

    sel_all = jnp.concatenate([r[2] for r in routed], axis=0)
    before = jnp.dot(tri_ref[...], sel_all.astype(BF16), preferred_element_type=F32) + cnt_ref[...]
    cnt_ref[...] = cnt_ref[...] + jnp.sum(sel_all, axis=0, keepdims=True)

    for s, rows in enumerate(subs):
        idxs, wts, _ = routed[s]
        bef = before[s * WO_SUB:(s + 1) * WO_SUB]
        ti = jnp.zeros((WO_SUB, LANES), F32)
        tw = jnp.zeros((WO_SUB, LANES), F32)
        for j in range(TOP_K):
            rank = jnp.sum(jnp.where(lane == idxs[j], bef, 0.0), axis=-1, keepdims=True)
            ti = jnp.where(lane == float(j), idxs[j], ti)
            ti = jnp.where(lane == float(TOP_K + j), rank, ti)
            tw = jnp.where(lane == float(j), wts[j], tw)
        ti_ref[:, s * WO_SUB:(s + 1) * WO_SUB] = ti.T[:2 * TOP_K, :].astype(jnp.int32)
        tw_ref[rows, :] = tw


def _wo(o_na, o_df, x2d, ada_g, w_o_bf, g_ffn, wr_hi, wr_lo, br_pad, tri, cnt0, seq):
    n = x2d.shape[0]
    tm = WO_BLOCK
    per_seq = seq // tm
    row = lambda i: (i, 0)
    const = lambda i: (0, 0)
    return pl.pallas_call(
        _wo_kernel,
        grid=(n // tm,),
        in_specs=[pl.BlockSpec((tm, NA_WIDTH), row),
                  pl.BlockSpec((tm, DIFF_WIDTH), row),
                  pl.BlockSpec((tm, D_MODEL), row),
                  pl.BlockSpec((1, 6, D_MODEL), lambda i: (i // per_seq, 0, 0)),
                  pl.BlockSpec((D_MODEL, D_MODEL), const),
                  pl.BlockSpec((1, D_MODEL), const),
                  pl.BlockSpec((D_MODEL, LANES), const),
                  pl.BlockSpec((D_MODEL, LANES), const),
                  pl.BlockSpec((1, LANES), const),
                  pl.BlockSpec((tm, tm), const),
                  pl.BlockSpec((1, LANES), const)],
        out_specs=[pl.BlockSpec((tm, D_MODEL), row),
                   pl.BlockSpec((tm, PACKED), row),
                   pl.BlockSpec((2 * TOP_K, tm), lambda i: (0, i)),
                   pl.BlockSpec((tm, LANES), row),
                   pl.BlockSpec((1, LANES), const)],
        out_shape=[jax.ShapeDtypeStruct((n, D_MODEL), F32),
                   jax.ShapeDtypeStruct((n, PACKED), U32),
                   jax.ShapeDtypeStruct((2 * TOP_K, n), jnp.int32),
                   jax.ShapeDtypeStruct((n, LANES), F32),
                   jax.ShapeDtypeStruct((1, LANES), F32)],
        compiler_params=_params("arbitrary"),
        name="wo_router",
    )(o_na, o_df, x2d, ada_g, w_o_bf, g_ffn, wr_hi, wr_lo, br_pad, tri, cnt0)


def _expert_kernel(be_ref, nxt_ref, fill_ref, na_ref, xs_ref, wg_hbm, bg_ref, wu_hbm, bu_ref, wd_hbm, bd_ref, o_ref,
                   w_f32, w_bf, sems, slot_ref):
    i = pl.program_id(0)
    active = i < na_ref[0]
    expert = be_ref[i]
    new_expert = jnp.logical_or(i == 0, expert != be_ref[jnp.maximum(i - 1, 0)])
    wg_bf, wu_bf, wd_bf = w_bf.at[0], w_bf.at[1], w_bf.at[2]

    def weight_copies(src_expert, slot):
        return [pltpu.make_async_copy(w_hbm.at[src_expert], w_f32.at[slot, j], sems.at[slot, j])
                for j, w_hbm in enumerate((wg_hbm, wu_hbm, wd_hbm))]

    @pl.when(i == 0)
    def _():
        slot_ref[0] = 0
        for cp in weight_copies(expert, 0):
            cp.start()

    @pl.when(jnp.logical_and(active, new_expert))
    def _():
        slot = slot_ref[0]
        nxt = nxt_ref[i]
        for s in range(2):
            @pl.when(slot == s)
            def _():
                @pl.when(nxt >= 0)
                def _():
                    for cp in weight_copies(nxt, 1 - s):
                        cp.start(priority=1)
                for j, cp in enumerate(weight_copies(expert, s)):
                    cp.wait()
                    w_bf[j] = w_f32[s, j].astype(BF16)
        slot_ref[0] = 1 - slot

    def run_half(first_row, n_sub):
        def gate_up(rows):
            x_lo, x_hi = _unpack_halves(xs_ref[rows, :])
            x_lo = x_lo.astype(BF16)
            x_hi = x_hi.astype(BF16)

            def proj(w_bf):
                return (jnp.dot(x_lo, w_bf[:PACKED, :], preferred_element_type=F32)
                        + jnp.dot(x_hi, w_bf[PACKED:, :], preferred_element_type=F32))

            return proj(wg_bf), proj(wu_bf)

        def act_down(rows, gu):
            g = jnp.minimum(gu[0] + bg_ref[0], SWIGLU_LIMIT)
            u = jnp.clip(gu[1] + bu_ref[0], -SWIGLU_LIMIT, SWIGLU_LIMIT)
            act = g * jax.nn.sigmoid(SWIGLU_ALPHA * g) * (u + 1.0)
            out = jnp.dot(act.astype(BF16), wd_bf[...], preferred_element_type=F32) + bd_ref[0]
            o_ref[rows, :] = _pack_halves(out.astype(BF16).astype(F32))

        sub = [pl.ds(first_row + r * EXPERT_SUB, EXPERT_SUB) for r in range(n_sub)]
        gu_next = gate_up(sub[0])
        for j, rows in enumerate(sub):
            gu = gu_next
            if j + 1 < len(sub):
                gu_next = gate_up(sub[j + 1])
            act_down(rows, gu)

    fill = jnp.where(active, fill_ref[i], 0)
    subs_per_half = EXPERT_HALF // EXPERT_SUB

    @pl.when(fill < 2 * subs_per_half)
    def _():
        o_ref[...] = jnp.zeros_like(o_ref)

    for half in range(2):
        for n_sub in range(1, subs_per_half + 1):
            here = fill - half * subs_per_half
            cond = (here >= n_sub) if n_sub == subs_per_half else (here == n_sub)

            @pl.when(cond)
            def _(half=half, n_sub=n_sub):
                run_half(half * EXPERT_HALF, n_sub)


def _experts(block_e, next_e, fill, n_active, xs, wg, bg, wu, bu, wd, bd):
    cap = xs.shape[0]
    n_blocks = cap // EXPERT_BLOCK
    xmap = lambda i, be, nx, fl, na: (jnp.minimum(i, na[0] - 1), 0)
    bmap = lambda i, be, nx, fl, na: (be[i], 0, 0)
    hbm = pl.BlockSpec(memory_space=pl.ANY)
    grid_spec = pltpu.PrefetchScalarGridSpec(
        num_scalar_prefetch=4,
        grid=(n_blocks,),
        in_specs=[pl.BlockSpec((EXPERT_BLOCK, PACKED), xmap),
                  hbm, pl.BlockSpec((1, 1, D_MODEL), bmap),
                  hbm, pl.BlockSpec((1, 1, D_MODEL), bmap),
                  hbm, pl.BlockSpec((1, 1, D_MODEL), bmap)],
        out_specs=pl.BlockSpec((EXPERT_BLOCK, PACKED), lambda i, be, nx, fl, na: (i, 0)),
        scratch_shapes=[pltpu.VMEM((2, 3, D_MODEL, D_MODEL), F32),
                        pltpu.VMEM((3, D_MODEL, D_MODEL), BF16),
                        pltpu.SemaphoreType.DMA((2, 3)),
                        pltpu.SMEM((1,), jnp.int32)],
    )
    return pl.pallas_call(
        _expert_kernel,
        grid_spec=grid_spec,
        out_shape=jax.ShapeDtypeStruct((cap, PACKED), U32),
        compiler_params=_params("arbitrary"),
        name="experts",
    )(block_e, next_e, fill, n_active, xs, wg, bg, wu, bu, wd, bd)


def _sc_gather(table, idx):
    n_out = idx.shape[0]
    width = table.shape[1]
    per_worker = n_out // SC_WORKERS
    n_chunks = per_worker // GATHER_ROWS
    assert per_worker * SC_WORKERS == n_out and n_chunks * GATHER_ROWS == per_worker and n_chunks % 2 == 0
    idx3 = idx.reshape(SC_WORKERS, n_chunks, GATHER_ROWS)
    mesh = plsc.VectorSubcoreMesh(core_axis_name="core", subcore_axis_name="subcore")

    @functools.partial(
        pl.kernel, mesh=mesh,
        out_type=jax.ShapeDtypeStruct((n_out, width), table.dtype),
        scratch_types=[pltpu.VMEM((n_chunks, GATHER_ROWS), jnp.int32),
                       pltpu.VMEM((2, GATHER_ROWS, width), table.dtype),
                       pltpu.SemaphoreType.DMA((2,)),
                       pltpu.SemaphoreType.DMA((2,))])
    def gather_kernel(table_hbm, idx_hbm, out_hbm, idx_v, rows_v, gsem, wsem):
        wid = lax.axis_index("subcore") * SC_CORES + lax.axis_index("core")
        base = wid * per_worker
        pltpu.sync_copy(idx_hbm.at[wid], idx_v)

        def gather(j, slot):
            return pltpu.make_async_copy(table_hbm.at[idx_v.at[j]], rows_v.at[slot], gsem.at[slot])

        def write(j, slot):
            dst = out_hbm.at[pl.ds(pl.multiple_of(base + j * GATHER_ROWS, GATHER_ROWS), GATHER_ROWS)]
            return pltpu.make_async_copy(rows_v.at[slot], dst, wsem.at[slot])

        gather(0, 0).start()

        @pl.loop(0, n_chunks, step=2)
        def _(j):
            for slot in range(2):
                jj = j + slot
                gather(jj, slot).wait()

                @pl.when(jj >= 1)
                def _():
                    write(jj - 1, 1 - slot).wait()

                @pl.when(jj + 1 < n_chunks)
                def _():
                    gather(jj + 1, 1 - slot).start()

                write(jj, slot).start()

        write(n_chunks - 1, 1).wait()

    return gather_kernel(table, idx3)


def _sc_scatter(rows, idx, n_out):
    n_src, width = rows.shape
    n_idx = idx.shape[0]
    per_worker = n_idx // SC_WORKERS
    n_chunks = per_worker // GATHER_ROWS
    assert per_worker * SC_WORKERS == n_idx and n_chunks * GATHER_ROWS == per_worker and n_chunks % 2 == 0
    assert n_src % per_worker == 0
    idx3 = idx.reshape(SC_WORKERS, n_chunks, GATHER_ROWS)
    mesh = plsc.VectorSubcoreMesh(core_axis_name="core", subcore_axis_name="subcore")

    @functools.partial(
        pl.kernel, mesh=mesh,
        out_type=jax.ShapeDtypeStruct((n_out, width), rows.dtype),
        scratch_types=[pltpu.VMEM((n_chunks, GATHER_ROWS), jnp.int32),
                       pltpu.VMEM((2, GATHER_ROWS, width), rows.dtype),
                       pltpu.SemaphoreType.DMA((2,)),
                       pltpu.SemaphoreType.DMA((2,))])
    def scatter_kernel(rows_hbm, idx_hbm, out_hbm, idx_v, rows_v, rsem, wsem):
        wid = lax.axis_index("subcore") * SC_CORES + lax.axis_index("core")
        base = lax.rem(wid * per_worker, n_src)
        pltpu.sync_copy(idx_hbm.at[wid], idx_v)

        def read(j, slot):
            src = rows_hbm.at[pl.ds(pl.multiple_of(base + j * GATHER_ROWS, GATHER_ROWS), GATHER_ROWS)]
            return pltpu.make_async_copy(src, rows_v.at[slot], rsem.at[slot])

        def write(j, slot):
            return pltpu.make_async_copy(rows_v.at[slot], out_hbm.at[idx_v.at[j]], wsem.at[slot])

        read(0, 0).start()

        @pl.loop(0, n_chunks, step=2)
        def _(j):
            for slot in range(2):
                jj = j + slot
                read(jj, slot).wait()

                @pl.when(jj >= 1)
                def _():
                    write(jj - 1, 1 - slot).wait()

                @pl.when(jj + 1 < n_chunks)
                def _():
                    read(jj + 1, 1 - slot).start()

                write(jj, slot).start()

        write(n_chunks - 1, 1).wait()

    return scatter_kernel(rows, idx3)


def _combine_kernel(x1_ref, y0_ref, y1_ref, y2_ref, y3_ref, tw_ref, ada_ref, o_ref):
    tw = tw_ref[...]
    acc_lo = jnp.zeros((x1_ref.shape[0], PACKED), F32)
    acc_hi = jnp.zeros((x1_ref.shape[0], PACKED), F32)
    for j, y_ref in enumerate((y0_ref, y1_ref, y2_ref, y3_ref)):
        lo, hi = _unpack_halves(y_ref[...])
        acc_lo = acc_lo + tw[:, j:j + 1] * lo
        acc_hi = acc_hi + tw[:, j:j + 1] * hi
    o_ref[:, :PACKED] = x1_ref[:, :PACKED] + ada_ref[0, 5:6, :PACKED] * acc_lo
    o_ref[:, PACKED:] = x1_ref[:, PACKED:] + ada_ref[0, 5:6, PACKED:] * acc_hi


def _combine(x1, ys, tw, ada_g, seq, row_off):
    n = x1.shape[0]
    tm = ROW_BLOCK
    per_seq = seq // tm
    off = row_off // tm
    per_choice = tw.shape[0] // tm
    y_specs = [pl.BlockSpec((tm, PACKED), functools.partial(lambda i, j: (j * per_choice + off + i, 0), j=j))
               for j in range(TOP_K)]
    return pl.pallas_call(
        _combine_kernel,
        grid=(n // tm,),
        in_specs=[pl.BlockSpec((tm, D_MODEL), lambda i: (i, 0)),
                  *y_specs,
                  pl.BlockSpec((tm, LANES), lambda i: (i + off, 0)),
                  pl.BlockSpec((1, 6, D_MODEL), lambda i: (i // per_seq, 0, 0))],
        out_specs=pl.BlockSpec((tm, D_MODEL), lambda i: (i, 0)),
        out_shape=jax.ShapeDtypeStruct((n, D_MODEL), F32),
        compiler_params=_params("arbitrary"),
        name="combine",
    )(x1, ys, ys, ys, ys, tw, ada_g)


def _rope_tables(seq):
    half = HEAD_DIM // 2
    inv = ROPE_THETA ** (-jnp.arange(half, dtype=F32) / half)
    ang = jnp.arange(seq, dtype=F32)[:, None] * inv[None, :]
    cos, sin = jnp.cos(ang), jnp.sin(ang)
    cos_h = jnp.concatenate([cos, cos], axis=-1)
    sin_h = jnp.concatenate([-sin, sin], axis=-1)
    reps = LANES // HEAD_DIM
    return jnp.tile(cos_h, (1, reps)), jnp.tile(sin_h, (1, reps))


def _na_bias_table(rpb):
    cols = jnp.arange(GRID_W, dtype=jnp.int32)
    c_start = jnp.clip(cols - NA_WIN_COLS // 2, 0, GRID_W - NA_WIN_COLS)
    col_mask = (cols[None, :] >= c_start[:, None]) & (cols[None, :] < c_start[:, None] + NA_WIN_COLS)
    col_idx = jnp.clip(cols[None, :] - cols[:, None], -(NA_WIN_COLS - 1), NA_WIN_COLS - 1) + NA_WIN_COLS - 1
    delta = jnp.arange(NA_WIN_ROWS, dtype=jnp.int32)
    j = jnp.arange(NA_WIN_ROWS, dtype=jnp.int32)
    row_idx = j[None, :] - delta[:, None] + NA_WIN_ROWS - 1
    row_hot = (row_idx[:, :, None] == jnp.arange(2 * NA_WIN_ROWS - 1, dtype=jnp.int32)).astype(F32)
    col_hot = (col_idx[:, :, None] == jnp.arange(2 * NA_WIN_COLS - 1, dtype=jnp.int32)).astype(F32)
    rpb_pairs = rpb.astype(F32).reshape(NA_HEADS // 2, 2, 2 * NA_WIN_ROWS - 1, 2 * NA_WIN_COLS - 1)
    pair_hot = jnp.eye(2, dtype=F32)
    lane_hot = jnp.einsum('qkc,hx->khqxc', col_hot, pair_hot).reshape(GRID_W, LANES, 2, 2 * NA_WIN_COLS - 1)
    bias = jnp.einsum('djr,pxrc,klxc->pdjkl', row_hot, rpb_pairs, lane_hot,
                      precision=lax.Precision.HIGHEST)
    lane_mask = jnp.concatenate([col_mask.T, col_mask.T], axis=1)
    bias = jnp.where(lane_mask[None, None, None], bias * LOG2E, NEG_INF)
    return bias.reshape(NA_HEADS // 2, NA_WIN_ROWS, NA_WIN_ROWS * GRID_W, LANES)


def _routing(top_idx, rank, counts, n):
    n_blocks = n * TOP_K // EXPERT_BLOCK + N_EXPERTS
    experts = jnp.arange(N_EXPERTS, dtype=jnp.int32)
    padded = (counts + EXPERT_BLOCK - 1) // EXPERT_BLOCK * EXPERT_BLOCK
    pad_end = jnp.cumsum(padded)
    pad_start = pad_end - padded
    start_of = jnp.sum(jnp.where(top_idx[None] == experts[:, None, None], pad_start[:, None, None], 0), axis=0)
    dest = (start_of + rank).reshape(-1)
    block_lo = jnp.arange(n_blocks, dtype=jnp.int32) * EXPERT_BLOCK
    block_e = jnp.minimum(jnp.sum((pad_end[None, :] <= block_lo[:, None]).astype(jnp.int32), axis=1),
                          N_EXPERTS - 1).astype(jnp.int32)
    n_active = (pad_end[-1] // EXPERT_BLOCK).astype(jnp.int32).reshape(1)
    later = jnp.where((experts[None, :] > experts[:, None]) & (padded[None, :] > 0), experts[None, :], N_EXPERTS)
    next_nonempty = jnp.min(later, axis=1)
    next_nonempty = jnp.where(next_nonempty == N_EXPERTS, -1, next_nonempty)
    own = block_e[:, None] == experts[None, :]
    next_e = jnp.sum(jnp.where(own, next_nonempty[None, :], 0), axis=1)
    real_end = jnp.sum(jnp.where(own, (pad_start + counts)[None, :], 0), axis=1)
    fill = jnp.clip((real_end - block_lo + EXPERT_SUB - 1) // EXPERT_SUB, 0, EXPERT_BLOCK // EXPERT_SUB)
    return dest, block_e, next_e.astype(jnp.int32), fill.astype(jnp.int32), n_active, n_blocks * EXPERT_BLOCK


def kernel(x_prompt, x_sample, c_prompt, c_sample, w_ada, b_ada, g_attn_norm, w_qkv, na_q_norm, na_k_norm, na_rpb, diff_q_norm, diff_k_norm, lambda_q1, lambda_k1, lambda_q2, lambda_k2, diff_subln, w_o, g_ffn_norm, w_router, b_router, w_gate, b_gate, w_up, b_up, w_down, b_down):
    l = 0
    groups = [(x_prompt, c_prompt), (x_sample, c_sample)]
    nb = [x.shape[0] for x, _ in groups]

    ada_all = _ada(jnp.concatenate([c for _, c in groups], axis=0), w_ada[l], b_ada[l])
    ada_all = ada_all.reshape(sum(nb), 6, D_MODEL)

    w_qkv_bf = w_qkv[l].astype(BF16)
    w_o_bf = w_o[l].astype(BF16)
    scale = HEAD_DIM ** -0.5
    reps = NA_WIDTH // HEAD_DIM
    gains = jnp.stack([jnp.tile(na_q_norm[l], reps) * (scale * LOG2E),
                       jnp.tile(na_k_norm[l], reps),
                       jnp.tile(diff_q_norm[l], reps) * (scale * LOG2E),
                       jnp.tile(diff_k_norm[l], reps)]).astype(F32)
    head_id = jnp.arange(MXU_DIM, dtype=jnp.int32) // HEAD_DIM
    bd = (head_id[:, None] == head_id[None, :]).astype(BF16)
    bias_t = _na_bias_table(na_rpb[l])
    lam_vecs = jnp.stack([lambda_q1[l], lambda_k1[l], lambda_q2[l], lambda_k2[l]]).astype(F32)
    subln_col = (diff_subln[l].astype(F32) * (1.0 - LAMBDA_INIT)).reshape(LANES, 1)
    wr = w_router[l].astype(F32)
    wr_pad = jnp.zeros((D_MODEL, LANES), F32).at[:, :N_EXPERTS].set(wr)
    wr_hi = wr_pad.astype(BF16)
    wr_lo = (wr_pad - wr_hi.astype(F32)).astype(BF16)
    wr_lo = wr_hi.at[:, N_EXPERTS:2 * N_EXPERTS].set(wr_lo[:, :N_EXPERTS])
    br_pad = jnp.full((1, LANES), NEG_INF, F32).at[0, :N_EXPERTS].set(b_router[l].astype(F32))
    g_attn = g_attn_norm[l].reshape(1, D_MODEL).astype(F32)
    g_ffn = g_ffn_norm[l].reshape(1, D_MODEL).astype(F32)
    max_seq = max(x.shape[1] for x, _ in groups)
    cos_t, sin_t = _rope_tables(max_seq)

    rows = lax.broadcasted_iota(jnp.int32, (WO_BLOCK, WO_BLOCK), 0)
    cols = lax.broadcasted_iota(jnp.int32, (WO_BLOCK, WO_BLOCK), 1)
    tri = (cols < rows).astype(BF16)
    cnt0 = jnp.zeros((1, LANES), F32)

    bg = b_gate[l].reshape(N_EXPERTS, 1, D_MODEL).astype(F32)
    bu = b_up[l].reshape(N_EXPERTS, 1, D_MODEL).astype(F32)
    bdn = b_down[l].reshape(N_EXPERTS, 1, D_MODEL).astype(F32)
    ada_groups = [ada_all[:nb[0]], ada_all[nb[0]:]]

    order = sorted(range(len(groups)), key=lambda g: -groups[g][0].shape[1])
    staged = {}
    for g in order:
        x = groups[g][0]
        b, seq = x.shape[0], x.shape[1]
        n = b * seq
        x2d = x.reshape(n, D_MODEL)
        ada_g = ada_groups[g]
        qkv = _qkv(x2d, ada_g, g_attn, w_qkv_bf, gains, cos_t, sin_t, bd, seq)
        o_na = _na(qkv, bias_t, b, seq)
        o_df = _diff(qkv, lam_vecs, subln_col, b, seq)
        x1, h2, ti, tw, cnt = _wo(o_na, o_df, x2d, ada_g, w_o_bf, g_ffn, wr_hi, wr_lo, br_pad, tri, cnt0, seq)
        counts = cnt[0, :N_EXPERTS].astype(jnp.int32)
        dest, *blocks, cap = _routing(ti[:TOP_K], ti[TOP_K:], counts, n)
        xs = _sc_scatter(h2, dest, cap)
        staged[g] = (x1, tw, ada_g, dest, blocks, xs, seq, b)

    sorted_out = {}
    for g in order:
        blocks, xs = staged[g][4], staged[g][5]
        sorted_out[g] = _experts(*blocks, xs, w_gate[l], bg, w_up[l], bu, w_down[l], bdn)

    outs = [None] * len(groups)
    for g in order:
        x1, tw, ada_g, dest, blocks, xs, seq, b = staged[g]
        ys = _sc_gather(sorted_out[g], dest)
        outs[g] = _combine(x1, ys, tw, ada_g, seq, 0).reshape(b, seq, D_MODEL)
    return tuple(outs)
```

```python
import functools
import math

import jax
import jax.numpy as jnp
from jax import lax
from jax.experimental import pallas as pl
from jax.experimental.pallas import tpu as pltpu
from jax.experimental.pallas import tpu_sc as plsc

F32 = jnp.float32
BF16 = jnp.bfloat16
U32 = jnp.uint32

D_MODEL = 1024
HEAD_DIM = 64
NA_HEADS = 8
NA_WIDTH = 512
DIFF_HEADS = 4
DIFF_WIDTH = 512
QKV_COLS = 3072
GRID_W = 64
NA_WIN_ROWS = 8
NA_WIN_COLS = 16
ROPE_THETA = 10000.0
N_EXPERTS = 32
TOP_K = 4
SWIGLU_LIMIT = 7.0
SWIGLU_ALPHA = 1.702
EPS = 1e-5
NEG_INF = -1e30
LAMBDA_INIT = 0.8 - 0.6 * math.exp(-0.3 * 0)
LOG2E = 1.4426950408889634

LANES = 128
MXU_DIM = 256
VMEM_LIMIT = 56 * 1024 * 1024

ROW_BLOCK = 1024
QKV_BLOCK = 1024
Q_BLOCK = 1024
EXPERT_BLOCK = 1024
EXPERT_HALF = 512
EXPERT_SUB = 256
WO_BLOCK = 1024
WO_SUB = 256
NA_ROWS_PER_TRIP = 16
VT_ROWS = LANES + 16
KV_CHUNK = 256
NORM_SLACK = 1.01
MAX_SAFE_BOUND = 60.0


PACKED = D_MODEL // 2
SC_CORES = 2
SC_SUBCORES = 16
SC_WORKERS = SC_CORES * SC_SUBCORES
GATHER_ROWS = 64


def _params(*sem):
    return pltpu.CompilerParams(dimension_semantics=sem, vmem_limit_bytes=VMEM_LIMIT)


def _pack_halves(x):
    w = x.shape[1] // 2
    bits = lax.bitcast_convert_type(x, U32)
    return (bits[:, :w] >> 16) | bits[:, w:]


def _col_reduce(x, op):
    while x.shape[0] >= 64:
        x = op(x.reshape(8, x.shape[0] // 8, x.shape[1]), axis=0)
    return op(x, axis=0, keepdims=True)


def _unpack_halves(word):
    lo = lax.bitcast_convert_type(word << 16, F32)
    hi = lax.bitcast_convert_type(word & jnp.uint32(0xFFFF0000), F32)
    return lo, hi


def _ada_kernel(c_ref, w_ref, b_ref, o_ref):
    c = c_ref[...]
    s = c * jax.nn.sigmoid(c)
    o_ref[...] = jnp.dot(s, w_ref[...], preferred_element_type=F32,
                         precision=lax.Precision.HIGHEST) + b_ref[...]


def _ada(c_all, w_ada, b_ada):
    nb = c_all.shape[0]
    n_out = w_ada.shape[1]
    blk = D_MODEL
    return pl.pallas_call(
        _ada_kernel,
        grid=(n_out // blk,),
        in_specs=[pl.BlockSpec((nb, D_MODEL), lambda j: (0, 0)),
                  pl.BlockSpec((D_MODEL, blk), lambda j: (0, j)),
                  pl.BlockSpec((1, blk), lambda j: (0, j))],
        out_specs=pl.BlockSpec((nb, blk), lambda j: (0, j)),
        out_shape=jax.ShapeDtypeStruct((nb, n_out), F32),
        compiler_params=_params("arbitrary"),
        name="ada",
    )(c_all, w_ada, b_ada.reshape(1, n_out))


def _head_sumsq(y, bd):
    sq = (y * y).astype(BF16)
    parts = [jnp.dot(sq[:, c:c + MXU_DIM], bd, preferred_element_type=F32)
             for c in range(0, y.shape[1], MXU_DIM)]
    return jnp.concatenate(parts, axis=1)


def _qkv_kernel(x_ref, ada_ref, g_ref, w_ref, gain_ref, cos_ref, sin_ref, bd_ref, o_ref):
    x = x_ref[...]
    ms = jnp.mean(x * x, axis=-1, keepdims=True)
    xn = x * lax.rsqrt(ms + EPS) * g_ref[...]
    sh = ada_ref[0, 0:1, :]
    sc = ada_ref[0, 1:2, :]
    h = (xn * (1.0 + sc) + sh).astype(BF16)
    bd = bd_ref[...]
    lane = lax.broadcasted_iota(jnp.int32, (x.shape[0], NA_WIDTH), 1)
    first_half = (lane & (HEAD_DIM // 2)) == 0
    for grp in range(6):
        cols = slice(grp * 512, (grp + 1) * 512)
        acc = jnp.dot(h, w_ref[:, cols], preferred_element_type=F32)
        if grp in (2, 5):
            o_ref[:, cols] = acc.astype(BF16)
            continue
        gi = {0: 0, 1: 1, 3: 2, 4: 3}[grp]
        ss = _head_sumsq(acc, bd)
        y = acc * lax.rsqrt(ss * (1.0 / HEAD_DIM) + EPS) * gain_ref[gi:gi + 1, :]
        if grp in (3, 4):
            partner = jnp.where(first_half,
                                pltpu.roll(y, NA_WIDTH - HEAD_DIM // 2, axis=1),
                                pltpu.roll(y, HEAD_DIM // 2, axis=1))
            reps = NA_WIDTH // LANES
            y = (y * jnp.concatenate([cos_ref[...]] * reps, axis=1)
                 + partner * jnp.concatenate([sin_ref[...]] * reps, axis=1))
        o_ref[:, cols] = y.astype(BF16)


def _qkv(x2d, ada_g, g_attn, w_qkv_bf, gains, cos_t, sin_t, bd, seq):
    n = x2d.shape[0]
    tm = QKV_BLOCK
    per_seq = seq // tm
    return pl.pallas_call(
        _qkv_kernel,
        grid=(n // tm,),
        in_specs=[pl.BlockSpec((tm, D_MODEL), lambda i: (i, 0)),
                  pl.BlockSpec((1, 6, D_MODEL), lambda i: (i // per_seq, 0, 0)),
                  pl.BlockSpec((1, D_MODEL), lambda i: (0, 0)),
                  pl.BlockSpec((D_MODEL, QKV_COLS), lambda i: (0, 0)),
                  pl.BlockSpec((4, NA_WIDTH), lambda i: (0, 0)),
                  pl.BlockSpec((tm, LANES), lambda i: (i % per_seq, 0)),
                  pl.BlockSpec((tm, LANES), lambda i: (i % per_seq, 0)),
                  pl.BlockSpec((MXU_DIM, MXU_DIM), lambda i: (0, 0))],
        out_specs=pl.BlockSpec((tm, QKV_COLS), lambda i: (i, 0)),
        out_shape=jax.ShapeDtypeStruct((n, QKV_COLS), BF16),
        compiler_params=_params("arbitrary"),
        name="qkv",
    )(x2d, ada_g, g_attn, w_qkv_bf, gains, cos_t, sin_t, bd)


def _na_kernel(q_ref, k_ref, v_ref, bias_ref, o_ref, vaug_ref, *, rows):
    lane = lax.broadcasted_iota(jnp.int32, (GRID_W, LANES), 1)
    head0 = lane < HEAD_DIM
    win = NA_WIN_ROWS * GRID_W

    def window_start(r):
        return jnp.clip(r - NA_WIN_ROWS // 2, 0, rows - NA_WIN_ROWS)

    def scores(r):
        r_start = window_start(r)
        q = q_ref[pl.ds(pl.multiple_of(r * GRID_W, GRID_W), GRID_W), :]
        kw = k_ref[pl.ds(pl.multiple_of(r_start * GRID_W, GRID_W), win), :]
        zero = jnp.zeros_like(q)
        qm = jnp.concatenate([jnp.where(head0, q, zero), jnp.where(head0, zero, q)], axis=0)
        s = lax.dot_general(kw, qm, (((1,), (1,)), ((), ())), preferred_element_type=F32)
        return s + bias_ref[0, r - r_start]

    vaug_ref[:, :LANES] = v_ref[...]
    vaug_ref[:, LANES:] = jnp.ones((v_ref.shape[0], LANES), BF16)

    def finish(r, s):
        vw = vaug_ref[pl.ds(pl.multiple_of(window_start(r) * GRID_W, GRID_W), win), :]
        m = _col_reduce(s, jnp.max)
        p = jnp.exp2(s - m).astype(BF16)
        o2 = lax.dot_general(p, vw, (((0,), (0,)), ((), ())), preferred_element_type=F32)
        o2 = o2[:, :LANES] * (1.0 / o2[:, LANES:])
        o = jnp.where(head0, o2[:GRID_W], o2[GRID_W:])
        o_ref[pl.ds(pl.multiple_of(r * GRID_W, GRID_W), GRID_W), :] = o.astype(BF16)

    def body(i, carry):
        trip_rows = [i * NA_ROWS_PER_TRIP + u for u in range(NA_ROWS_PER_TRIP)]
        trip_scores = [scores(r) for r in trip_rows]
        for r, s in zip(trip_rows, trip_scores):
            finish(r, s)
        return carry

    lax.fori_loop(0, rows // NA_ROWS_PER_TRIP, body, 0)


def _na(qkv, bias_t, batch, seq):
    rows = seq // GRID_W
    n_pairs = NA_HEADS // 2
    return pl.pallas_call(
        functools.partial(_na_kernel, rows=rows),
        grid=(batch, n_pairs),
        in_specs=[pl.BlockSpec((seq, LANES), lambda b, hp: (b, hp)),
                  pl.BlockSpec((seq, LANES), lambda b, hp: (b, n_pairs + hp)),
                  pl.BlockSpec((seq, LANES), lambda b, hp: (b, 2 * n_pairs + hp)),
                  pl.BlockSpec((1, NA_WIN_ROWS, NA_WIN_ROWS * GRID_W, LANES), lambda b, hp: (hp, 0, 0, 0))],
        out_specs=pl.BlockSpec((seq, LANES), lambda b, hp: (b, hp)),
        out_shape=jax.ShapeDtypeStruct((batch * seq, NA_WIDTH), BF16),
        scratch_shapes=[pltpu.VMEM((seq, 2 * LANES), BF16)],
        compiler_params=_params("arbitrary", "arbitrary"),
        name="na_attn",
    )(qkv, qkv, qkv, bias_t)


def _diff_kernel(q_ref, k_ref, v_ref, lam_ref, g_ref, o_ref, vt_ref, kn_ref, oa_ref):
    @pl.when(pl.program_id(2) == 0)
    def _():
        vt_ref[:LANES, :] = v_ref[...].astype(F32).T.astype(BF16)
        ones_row = lax.broadcasted_iota(jnp.int32, (VT_ROWS - LANES, v_ref.shape[0]), 0) == 0
        vt_ref[LANES:, :] = jnp.where(ones_row, 1.0, 0.0).astype(BF16)
        kf = k_ref[...].astype(F32)
        d_id = lax.broadcasted_iota(jnp.int32, (LANES, LANES), 0) // HEAD_DIM
        c_id = lax.broadcasted_iota(jnp.int32, (LANES, LANES), 1)
        comp_sel = jnp.where(d_id == c_id, 1.0, 0.0).astype(BF16)
        kn2 = jnp.dot((kf * kf).astype(BF16), comp_sel, preferred_element_type=F32)
        kn_ref[...] = jnp.sqrt(_col_reduce(kn2, jnp.max)) * NORM_SLACK

    lq1 = lam_ref[0:1, :]
    lk1 = lam_ref[1:2, :]
    lq2 = lam_ref[2:3, :]
    lk2 = lam_ref[3:4, :]
    lam = (jnp.exp(jnp.sum(lq1 * lk1, axis=-1, keepdims=True))
           - jnp.exp(jnp.sum(lq2 * lk2, axis=-1, keepdims=True)) + LAMBDA_INIT)

    q = q_ref[...]
    lane = lax.broadcasted_iota(jnp.int32, q.shape, 1)
    zero = jnp.zeros_like(q)
    tq = q.shape[0]
    qcat = jnp.concatenate([jnp.where(lane < HEAD_DIM, q, zero), jnp.where(lane < HEAD_DIM, zero, q)], axis=0)
    def scores(c, chunk):
        kc = k_ref[c * chunk:(c + 1) * chunk, :]
        return lax.dot_general(kc, qcat, (((1,), (1,)), ((), ())), preferred_element_type=F32)

    def chunked(chunk, step):
        n_chunks = k_ref.shape[0] // chunk
        s_next = scores(0, chunk)
        state = None
        for c in range(n_chunks):
            s = s_next
            if c + 1 < n_chunks:
                s_next = scores(c + 1, chunk)
            state = step(c, s, vt_ref[:, c * chunk:(c + 1) * chunk], state)
        return state

    qf = qcat.astype(F32)
    ones8 = jnp.ones((8, LANES), BF16)
    qn2 = lax.dot_general(ones8, (qf * qf).astype(BF16), (((1,), (1,)), ((), ())), preferred_element_type=F32)
    col = lax.broadcasted_iota(jnp.int32, (1, 2 * tq), 1)
    bound = jnp.sqrt(qn2[0:1, :]) * NORM_SLACK * jnp.where(col < tq, kn_ref[0:1, 0:1], kn_ref[0:1, 1:2])
    in_range = jnp.max(bound) <= MAX_SAFE_BOUND

    @pl.when(in_range)
    def _():
        def step(c, s, vtc, acc):
            pv = jnp.dot(vtc, jnp.exp2(s - bound).astype(BF16), preferred_element_type=F32)
            return pv if c == 0 else acc + pv
        oa_ref[...] = chunked(KV_CHUNK, step)

    @pl.when(jnp.logical_not(in_range))
    def _():
        def step(c, s, vtc, state):
            mc = _col_reduce(s, jnp.max)
            m_new = mc if c == 0 else jnp.maximum(state[0], mc)
            pv = jnp.dot(vtc, jnp.exp2(s - m_new).astype(BF16), preferred_element_type=F32)
            return (m_new, pv if c == 0 else jnp.exp2(state[0] - m_new) * state[1] + pv)
        oa_ref[...] = chunked(KV_CHUNK, step)[1]

    oa = oa_ref[...]
    o0, l0 = oa[:LANES, :tq], oa[LANES:LANES + 1, :tq]
    o1, l1 = oa[:LANES, tq:], oa[LANES:LANES + 1, tq:]
    o = o0 * (1.0 / l0) - (lam / l1) * o1
    ms = jnp.mean(o * o, axis=0, keepdims=True)
    y = o * lax.rsqrt(ms + EPS) * g_ref[...]
    o_ref[...] = y.T.astype(BF16)


def _diff(qkv, lam_vecs, subln_col, batch, seq):
    tq = Q_BLOCK
    nq = seq // tq
    base = 3 * NA_WIDTH // LANES
    nh = DIFF_HEADS
    return pl.pallas_call(
        _diff_kernel,
        grid=(batch, nh, nq),
        in_specs=[pl.BlockSpec((tq, LANES), lambda b, h, i: (b * nq + i, base + h)),
                  pl.BlockSpec((seq, LANES), lambda b, h, i: (b, base + nh + h)),
                  pl.BlockSpec((seq, LANES), lambda b, h, i: (b, base + 2 * nh + h)),
                  pl.BlockSpec((4, HEAD_DIM), lambda b, h, i: (0, 0)),
                  pl.BlockSpec((LANES, 1), lambda b, h, i: (0, 0))],
        out_specs=pl.BlockSpec((tq, LANES), lambda b, h, i: (b * nq + i, h)),
        out_shape=jax.ShapeDtypeStruct((batch * seq, DIFF_WIDTH), BF16),
        scratch_shapes=[pltpu.VMEM((VT_ROWS, seq), BF16),
                        pltpu.VMEM((1, LANES), F32),
                        pltpu.VMEM((VT_ROWS, 2 * tq), F32)],
        compiler_params=_params("arbitrary", "arbitrary", "arbitrary"),
        name="diff_attn",
    )(qkv, qkv, qkv, lam_vecs, subln_col)


def _wo_kernel(ona_ref, odf_ref, x_ref, ada_ref, wo_ref, g_ref, wrh_ref, wrl_ref, br_ref, tri_ref, cnt0_ref,
               x1_ref, h2_ref, ti_ref, tw_ref, cnt_ref):
    gt1 = ada_ref[0, 2:3, :]
    sh2 = ada_ref[0, 3:4, :]
    sc2 = ada_ref[0, 4:5, :]
    lane = lax.broadcasted_iota(jnp.int32, (WO_SUB, LANES), 1).astype(F32)

    def mix_of(rows):
        return (jnp.dot(ona_ref[rows, :], wo_ref[:NA_WIDTH, :], preferred_element_type=F32)
                + jnp.dot(odf_ref[rows, :], wo_ref[NA_WIDTH:, :], preferred_element_type=F32))

    def route(rows, mix):
        x1 = x_ref[rows, :] + gt1 * mix
        x1_ref[rows, :] = x1
        ms = jnp.mean(x1 * x1, axis=-1, keepdims=True)
        h2 = x1 * lax.rsqrt(ms + EPS) * g_ref[...] * (1.0 + sc2) + sh2
        hi = h2.astype(BF16)
        h2_ref[rows, :] = _pack_halves(hi.astype(F32))
        lo = (h2 - hi.astype(F32)).astype(BF16)
        both = jnp.dot(hi, wrl_ref[...], preferred_element_type=F32)
        cur = (both + pltpu.roll(both, LANES - N_EXPERTS, axis=1)
               + jnp.dot(lo, wrh_ref[...], preferred_element_type=F32)) + br_ref[...]
        vals = []
        idxs = []
        for _ in range(TOP_K):
            m = jnp.max(cur, axis=-1, keepdims=True)
            idx = jnp.min(jnp.where(cur == m, lane, float(LANES)), axis=-1, keepdims=True)
            vals.append(m)
            idxs.append(idx)
            cur = jnp.where(lane == idx, -jnp.inf, cur)
        es = [jnp.exp(v - vals[0]) for v in vals]
        inv = 1.0 / (es[0] + es[1] + es[2] + es[3])
        sel = jnp.zeros((WO_SUB, LANES), F32)
        for j in range(TOP_K):
            sel = sel + jnp.where(lane == idxs[j], 1.0, 0.0)
        return idxs, [e * inv for e in es], sel

    subs = [pl.ds(r, WO_SUB) for r in range(0, x_ref.shape[0], WO_SUB)]
    routed = []
    mix_next = mix_of(subs[0])
    for j, rows in enumerate(subs):
        mix = mix_next
        if j + 1 < len(subs):
            mix_next = mix_of(subs[j + 1])
        routed.append(route(rows, mix))

    @pl.when(pl.program_id(0) == 0)
    def _():
        cnt_ref[...] = cnt0_ref[...]

    sel_all = jnp.concatenate([r[2] for r in routed], axis=0)
    before = jnp.dot(tri_ref[...], sel_all.astype(BF16), preferred_element_type=F32) + cnt_ref[...]
    cnt_ref[...] = cnt_ref[...] + jnp.sum(sel_all, axis=0, keepdims=True)

    for s, rows in enumerate(subs):
        idxs, wts, _ = routed[s]
        bef = before[s * WO_SUB:(s + 1) * WO_SUB]
        ti = jnp.zeros((WO_SUB, LANES), F32)
        tw = jnp.zeros((WO_SUB, LANES), F32)
        for j in range(TOP_K):
            rank = jnp.sum(jnp.where(lane == idxs[j], bef, 0.0), axis=-1, keepdims=True)
            ti = jnp.where(lane == float(j), idxs[j], ti)
            ti = jnp.where(lane == float(TOP_K + j), rank, ti)
            tw = jnp.where(lane == float(j), wts[j], tw)
        ti_ref[:, s * WO_SUB:(s + 1) * WO_SUB] = ti.T[:2 * TOP_K, :].astype(jnp.int32)
        tw_ref[rows, :] = tw


def _wo(o_na, o_df, x2d, ada_g, w_o_bf, g_ffn, wr_hi, wr_lo, br_pad, tri, cnt0, seq):
    n = x2d.shape[0]
    tm = WO_BLOCK
    per_seq = seq // tm
    row = lambda i: (i, 0)
    const = lambda i: (0, 0)
    return pl.pallas_call(
        _wo_kernel,
        grid=(n // tm,),
        in_specs=[pl.BlockSpec((tm, NA_WIDTH), row),
                  pl.BlockSpec((tm, DIFF_WIDTH), row),
                  pl.BlockSpec((tm, D_MODEL), row),
                  pl.BlockSpec((1, 6, D_MODEL), lambda i: (i // per_seq, 0, 0)),
                  pl.BlockSpec((D_MODEL, D_MODEL), const),
                  pl.BlockSpec((1, D_MODEL), const),
                  pl.BlockSpec((D_MODEL, LANES), const),
                  pl.BlockSpec((D_MODEL, LANES), const),
                  pl.BlockSpec((1, LANES), const),
                  pl.BlockSpec((tm, tm), const),
                  pl.BlockSpec((1, LANES), const)],
        out_specs=[pl.BlockSpec((tm, D_MODEL), row),
                   pl.BlockSpec((tm, PACKED), row),
                   pl.BlockSpec((2 * TOP_K, tm), lambda i: (0, i)),
                   pl.BlockSpec((tm, LANES), row),
                   pl.BlockSpec((1, LANES), const)],
        out_shape=[jax.ShapeDtypeStruct((n, D_MODEL), F32),
                   jax.ShapeDtypeStruct((n, PACKED), U32),
                   jax.ShapeDtypeStruct((2 * TOP_K, n), jnp.int32),
                   jax.ShapeDtypeStruct((n, LANES), F32),
                   jax.ShapeDtypeStruct((1, LANES), F32)],
        compiler_params=_params("arbitrary"),
        name="wo_router",
    )(o_na, o_df, x2d, ada_g, w_o_bf, g_ffn, wr_hi, wr_lo, br_pad, tri, cnt0)


def _expert_kernel(be_ref, nxt_ref, fill_ref, na_ref, xs_ref, wg_hbm, bg_ref, wu_hbm, bu_ref, wd_hbm, bd_ref, o_ref,
                   w_f32, w_bf, sems, slot_ref):
    i = pl.program_id(0)
    active = i < na_ref[0]
    expert = be_ref[i]
    new_expert = jnp.logical_or(i == 0, expert != be_ref[jnp.maximum(i - 1, 0)])
    wg_bf, wu_bf, wd_bf = w_bf.at[0], w_bf.at[1], w_bf.at[2]

    def weight_copies(src_expert, slot):
        return [pltpu.make_async_copy(w_hbm.at[src_expert], w_f32.at[slot, j], sems.at[slot, j])
                for j, w_hbm in enumerate((wg_hbm, wu_hbm, wd_hbm))]

    @pl.when(i == 0)
    def _():
        slot_ref[0] = 0
        for cp in weight_copies(expert, 0):
            cp.start()

    @pl.when(jnp.logical_and(active, new_expert))
    def _():
        slot = slot_ref[0]
        nxt = nxt_ref[i]
        for s in range(2):
            @pl.when(slot == s)
            def _():
                @pl.when(nxt >= 0)
                def _():
                    for cp in weight_copies(nxt, 1 - s):
                        cp.start(priority=1)
                for j, cp in enumerate(weight_copies(expert, s)):
                    cp.wait()
                    w_bf[j] = w_f32[s, j].astype(BF16)
        slot_ref[0] = 1 - slot

    def run_half(first_row, n_sub):
        def gate_up(rows):
            x_lo, x_hi = _unpack_halves(xs_ref[rows, :])
            x_lo = x_lo.astype(BF16)
            x_hi = x_hi.astype(BF16)

            def proj(w_bf):
                return (jnp.dot(x_lo, w_bf[:PACKED, :], preferred_element_type=F32)
                        + jnp.dot(x_hi, w_bf[PACKED:, :], preferred_element_type=F32))

            return proj(wg_bf), proj(wu_bf)

        def act_down(rows, gu):
            g = jnp.minimum(gu[0] + bg_ref[0], SWIGLU_LIMIT)
            u = jnp.clip(gu[1] + bu_ref[0], -SWIGLU_LIMIT, SWIGLU_LIMIT)
            act = g * jax.nn.sigmoid(SWIGLU_ALPHA * g) * (u + 1.0)
            out = jnp.dot(act.astype(BF16), wd_bf[...], preferred_element_type=F32) + bd_ref[0]
            o_ref[rows, :] = _pack_halves(out.astype(BF16).astype(F32))

        sub = [pl.ds(first_row + r * EXPERT_SUB, EXPERT_SUB) for r in range(n_sub)]
        gu_next = gate_up(sub[0])
        for j, rows in enumerate(sub):
            gu = gu_next
            if j + 1 < len(sub):
                gu_next = gate_up(sub[j + 1])
            act_down(rows, gu)

    fill = jnp.where(active, fill_ref[i], 0)
    subs_per_half = EXPERT_HALF // EXPERT_SUB

    @pl.when(fill < 2 * subs_per_half)
    def _():
        o_ref[...] = jnp.zeros_like(o_ref)

    for half in range(2):
        for n_sub in range(1, subs_per_half + 1):
            here = fill - half * subs_per_half
            cond = (here >= n_sub) if n_sub == subs_per_half else (here == n_sub)

            @pl.when(cond)
            def _(half=half, n_sub=n_sub):
                run_half(half * EXPERT_HALF, n_sub)


def _experts(block_e, next_e, fill, n_active, xs, wg, bg, wu, bu, wd, bd):
    cap = xs.shape[0]
    n_blocks = cap // EXPERT_BLOCK
    xmap = lambda i, be, nx, fl, na: (jnp.minimum(i, na[0] - 1), 0)
    bmap = lambda i, be, nx, fl, na: (be[i], 0, 0)
    hbm = pl.BlockSpec(memory_space=pl.ANY)
    grid_spec = pltpu.PrefetchScalarGridSpec(
        num_scalar_prefetch=4,
        grid=(n_blocks,),
        in_specs=[pl.BlockSpec((EXPERT_BLOCK, PACKED), xmap),
                  hbm, pl.BlockSpec((1, 1, D_MODEL), bmap),
                  hbm, pl.BlockSpec((1, 1, D_MODEL), bmap),
                  hbm, pl.BlockSpec((1, 1, D_MODEL), bmap)],
        out_specs=pl.BlockSpec((EXPERT_BLOCK, PACKED), lambda i, be, nx, fl, na: (i, 0)),
        scratch_shapes=[pltpu.VMEM((2, 3, D_MODEL, D_MODEL), F32),
                        pltpu.VMEM((3, D_MODEL, D_MODEL), BF16),
                        pltpu.SemaphoreType.DMA((2, 3)),
                        pltpu.SMEM((1,), jnp.int32)],
    )
    return pl.pallas_call(
        _expert_kernel,
        grid_spec=grid_spec,
        out_shape=jax.ShapeDtypeStruct((cap, PACKED), U32),
        compiler_params=_params("arbitrary"),
        name="experts",
    )(block_e, next_e, fill, n_active, xs, wg, bg, wu, bu, wd, bd)


def _sc_gather(table, idx):
    n_out = idx.shape[0]
    width = table.shape[1]
    per_worker = n_out // SC_WORKERS
    n_chunks = per_worker // GATHER_ROWS
    assert per_worker * SC_WORKERS == n_out and n_chunks * GATHER_ROWS == per_worker and n_chunks % 2 == 0
    idx3 = idx.reshape(SC_WORKERS, n_chunks, GATHER_ROWS)
    mesh = plsc.VectorSubcoreMesh(core_axis_name="core", subcore_axis_name="subcore")

    @functools.partial(
        pl.kernel, mesh=mesh,
        out_type=jax.ShapeDtypeStruct((n_out, width), table.dtype),
        scratch_types=[pltpu.VMEM((n_chunks, GATHER_ROWS), jnp.int32),
                       pltpu.VMEM((2, GATHER_ROWS, width), table.dtype),
                       pltpu.SemaphoreType.DMA((2,)),
                       pltpu.SemaphoreType.DMA((2,))])
    def gather_kernel(table_hbm, idx_hbm, out_hbm, idx_v, rows_v, gsem, wsem):
        wid = lax.axis_index("subcore") * SC_CORES + lax.axis_index("core")
        base = wid * per_worker
        pltpu.sync_copy(idx_hbm.at[wid], idx_v)

        def gather(j, slot):
            return pltpu.make_async_copy(table_hbm.at[idx_v.at[j]], rows_v.at[slot], gsem.at[slot])

        def write(j, slot):
            dst = out_hbm.at[pl.ds(pl.multiple_of(base + j * GATHER_ROWS, GATHER_ROWS), GATHER_ROWS)]
            return pltpu.make_async_copy(rows_v.at[slot], dst, wsem.at[slot])

        gather(0, 0).start()

        @pl.loop(0, n_chunks, step=2)
        def _(j):
            for slot in range(2):
                jj = j + slot
                gather(jj, slot).wait()

                @pl.when(jj >= 1)
                def _():
                    write(jj - 1, 1 - slot).wait()

                @pl.when(jj + 1 < n_chunks)
                def _():
                    gather(jj + 1, 1 - slot).start()

                write(jj, slot).start()

        write(n_chunks - 1, 1).wait()

    return gather_kernel(table, idx3)


def _sc_scatter(rows, idx, n_out):
    n_src, width = rows.shape
    n_idx = idx.shape[0]
    per_worker = n_idx // SC_WORKERS
    n_chunks = per_worker // GATHER_ROWS
    assert per_worker * SC_WORKERS == n_idx and n_chunks * GATHER_ROWS == per_worker and n_chunks % 2 == 0
    assert n_src % per_worker == 0
    idx3 = idx.reshape(SC_WORKERS, n_chunks, GATHER_ROWS)
    mesh = plsc.VectorSubcoreMesh(core_axis_name="core", subcore_axis_name="subcore")

    @functools.partial(
        pl.kernel, mesh=mesh,
        out_type=jax.ShapeDtypeStruct((n_out, width), rows.dtype),
        scratch_types=[pltpu.VMEM((n_chunks, GATHER_ROWS), jnp.int32),
                       pltpu.VMEM((2, GATHER_ROWS, width), rows.dtype),
                       pltpu.SemaphoreType.DMA((2,)),
                       pltpu.SemaphoreType.DMA((2,))])
    def scatter_kernel(rows_hbm, idx_hbm, out_hbm, idx_v, rows_v, rsem, wsem):
        wid = lax.axis_index("subcore") * SC_CORES + lax.axis_index("core")
        base = lax.rem(wid * per_worker, n_src)
        pltpu.sync_copy(idx_hbm.at[wid], idx_v)

        def read(j, slot):
            src = rows_hbm.at[pl.ds(pl.multiple_of(base + j * GATHER_ROWS, GATHER_ROWS), GATHER_ROWS)]
            return pltpu.make_async_copy(src, rows_v.at[slot], rsem.at[slot])

        def write(j, slot):
            return pltpu.make_async_copy(rows_v.at[slot], out_hbm.at[idx_v.at[j]], wsem.at[slot])

        read(0, 0).start()

        @pl.loop(0, n_chunks, step=2)
        def _(j):
            for slot in range(2):
                jj = j + slot
                read(jj, slot).wait()

                @pl.when(jj >= 1)
                def _():
                    write(jj - 1, 1 - slot).wait()

                @pl.when(jj + 1 < n_chunks)
                def _():
                    read(jj + 1, 1 - slot).start()

                write(jj, slot).start()

        write(n_chunks - 1, 1).wait()

    return scatter_kernel(rows, idx3)


def _combine_kernel(x1_ref, y0_ref, y1_ref, y2_ref, y3_ref, tw_ref, ada_ref, o_ref):
    tw = tw_ref[...]
    acc_lo = jnp.zeros((x1_ref.shape[0], PACKED), F32)
    acc_hi = jnp.zeros((x1_ref.shape[0], PACKED), F32)
    for j, y_ref in enumerate((y0_ref, y1_ref, y2_ref, y3_ref)):
        lo, hi = _unpack_halves(y_ref[...])
        acc_lo = acc_lo + tw[:, j:j + 1] * lo
        acc_hi = acc_hi + tw[:, j:j + 1] * hi
    o_ref[:, :PACKED] = x1_ref[:, :PACKED] + ada_ref[0, 5:6, :PACKED] * acc_lo
    o_ref[:, PACKED:] = x1_ref[:, PACKED:] + ada_ref[0, 5:6, PACKED:] * acc_hi


def _combine(x1, ys, tw, ada_g, seq, row_off):
    n = x1.shape[0]
    tm = ROW_BLOCK
    per_seq = seq // tm
    off = row_off // tm
    per_choice = tw.shape[0] // tm
    y_specs = [pl.BlockSpec((tm, PACKED), functools.partial(lambda i, j: (j * per_choice + off + i, 0), j=j))
               for j in range(TOP_K)]
    return pl.pallas_call(
        _combine_kernel,
        grid=(n // tm,),
        in_specs=[pl.BlockSpec((tm, D_MODEL), lambda i: (i, 0)),
                  *y_specs,
                  pl.BlockSpec((tm, LANES), lambda i: (i + off, 0)),
                  pl.BlockSpec((1, 6, D_MODEL), lambda i: (i // per_seq, 0, 0))],
        out_specs=pl.BlockSpec((tm, D_MODEL), lambda i: (i, 0)),
        out_shape=jax.ShapeDtypeStruct((n, D_MODEL), F32),
        compiler_params=_params("arbitrary"),
        name="combine",
    )(x1, ys, ys, ys, ys, tw, ada_g)


def _rope_tables(seq):
    half = HEAD_DIM // 2
    inv = ROPE_THETA ** (-jnp.arange(half, dtype=F32) / half)
    ang = jnp.arange(seq, dtype=F32)[:, None] * inv[None, :]
    cos, sin = jnp.cos(ang), jnp.sin(ang)
    cos_h = jnp.concatenate([cos, cos], axis=-1)
    sin_h = jnp.concatenate([-sin, sin], axis=-1)
    reps = LANES // HEAD_DIM
    return jnp.tile(cos_h, (1, reps)), jnp.tile(sin_h, (1, reps))


def _na_bias_table(rpb):
    cols = jnp.arange(GRID_W, dtype=jnp.int32)
    c_start = jnp.clip(cols - NA_WIN_COLS // 2, 0, GRID_W - NA_WIN_COLS)
    col_mask = (cols[None, :] >= c_start[:, None]) & (cols[None, :] < c_start[:, None] + NA_WIN_COLS)
    col_idx = jnp.clip(cols[None, :] - cols[:, None], -(NA_WIN_COLS - 1), NA_WIN_COLS - 1) + NA_WIN_COLS - 1
    delta = jnp.arange(NA_WIN_ROWS, dtype=jnp.int32)
    j = jnp.arange(NA_WIN_ROWS, dtype=jnp.int32)
    row_idx = j[None, :] - delta[:, None] + NA_WIN_ROWS - 1
    row_hot = (row_idx[:, :, None] == jnp.arange(2 * NA_WIN_ROWS - 1, dtype=jnp.int32)).astype(F32)
    col_hot = (col_idx[:, :, None] == jnp.arange(2 * NA_WIN_COLS - 1, dtype=jnp.int32)).astype(F32)
    rpb_pairs = rpb.astype(F32).reshape(NA_HEADS // 2, 2, 2 * NA_WIN_ROWS - 1, 2 * NA_WIN_COLS - 1)
    pair_hot = jnp.eye(2, dtype=F32)
    lane_hot = jnp.einsum('qkc,hx->khqxc', col_hot, pair_hot).reshape(GRID_W, LANES, 2, 2 * NA_WIN_COLS - 1)
    bias = jnp.einsum('djr,pxrc,klxc->pdjkl', row_hot, rpb_pairs, lane_hot,
                      precision=lax.Precision.HIGHEST)
    lane_mask = jnp.concatenate([col_mask.T, col_mask.T], axis=1)
    bias = jnp.where(lane_mask[None, None, None], bias * LOG2E, NEG_INF)
    return bias.reshape(NA_HEADS // 2, NA_WIN_ROWS, NA_WIN_ROWS * GRID_W, LANES)


def _routing(top_idx, rank, counts, n):
    n_blocks = n * TOP_K // EXPERT_BLOCK + N_EXPERTS
    experts = jnp.arange(N_EXPERTS, dtype=jnp.int32)
    padded = (counts + EXPERT_BLOCK - 1) // EXPERT_BLOCK * EXPERT_BLOCK
    pad_end = jnp.cumsum(padded)
    pad_start = pad_end - padded
    start_of = jnp.sum(jnp.where(top_idx[None] == experts[:, None, None], pad_start[:, None, None], 0), axis=0)
    dest = (start_of + rank).reshape(-1)
    block_lo = jnp.arange(n_blocks, dtype=jnp.int32) * EXPERT_BLOCK
    block_e = jnp.minimum(jnp.sum((pad_end[None, :] <= block_lo[:, None]).astype(jnp.int32), axis=1),
                          N_EXPERTS - 1).astype(jnp.int32)
    n_active = (pad_end[-1] // EXPERT_BLOCK).astype(jnp.int32).reshape(1)
    later = jnp.where((experts[None, :] > experts[:, None]) & (padded[None, :] > 0), experts[None, :], N_EXPERTS)
    next_nonempty = jnp.min(later, axis=1)
    next_nonempty = jnp.where(next_nonempty == N_EXPERTS, -1, next_nonempty)
    own = block_e[:, None] == experts[None, :]
    next_e = jnp.sum(jnp.where(own, next_nonempty[None, :], 0), axis=1)
    real_end = jnp.sum(jnp.where(own, (pad_start + counts)[None, :], 0), axis=1)
    fill = jnp.clip((real_end - block_lo + EXPERT_SUB - 1) // EXPERT_SUB, 0, EXPERT_BLOCK // EXPERT_SUB)
    return dest, block_e, next_e.astype(jnp.int32), fill.astype(jnp.int32), n_active, n_blocks * EXPERT_BLOCK


def kernel(x_prompt, x_sample, c_prompt, c_sample, w_ada, b_ada, g_attn_norm, w_qkv, na_q_norm, na_k_norm, na_rpb, diff_q_norm, diff_k_norm, lambda_q1, lambda_k1, lambda_q2, lambda_k2, diff_subln, w_o, g_ffn_norm, w_router, b_router, w_gate, b_gate, w_up, b_up, w_down, b_down):
    l = 0
    groups = [(x_prompt, c_prompt), (x_sample, c_sample)]
    nb = [x.shape[0] for x, _ in groups]

    ada_all = _ada(jnp.concatenate([c for _, c in groups], axis=0), w_ada[l], b_ada[l])
    ada_all = ada_all.reshape(sum(nb), 6, D_MODEL)

    w_qkv_bf = w_qkv[l].astype(BF16)
    w_o_bf = w_o[l].astype(BF16)
    scale = HEAD_DIM ** -0.5
    reps = NA_WIDTH // HEAD_DIM
    gains = jnp.stack([jnp.tile(na_q_norm[l], reps) * (scale * LOG2E),
                       jnp.tile(na_k_norm[l], reps),
                       jnp.tile(diff_q_norm[l], reps) * (scale * LOG2E),
                       jnp.tile(diff_k_norm[l], reps)]).astype(F32)
    head_id = jnp.arange(MXU_DIM, dtype=jnp.int32) // HEAD_DIM
    bd = (head_id[:, None] == head_id[None, :]).astype(BF16)
    bias_t = _na_bias_table(na_rpb[l])
    lam_vecs = jnp.stack([lambda_q1[l], lambda_k1[l], lambda_q2[l], lambda_k2[l]]).astype(F32)
    subln_col = (diff_subln[l].astype(F32) * (1.0 - LAMBDA_INIT)).reshape(LANES, 1)
    wr = w_router[l].astype(F32)
    wr_pad = jnp.zeros((D_MODEL, LANES), F32).at[:, :N_EXPERTS].set(wr)
    wr_hi = wr_pad.astype(BF16)
    wr_lo = (wr_pad - wr_hi.astype(F32)).astype(BF16)
    wr_lo = wr_hi.at[:, N_EXPERTS:2 * N_EXPERTS].set(wr_lo[:, :N_EXPERTS])
    br_pad = jnp.full((1, LANES), NEG_INF, F32).at[0, :N_EXPERTS].set(b_router[l].astype(F32))
    g_attn = g_attn_norm[l].reshape(1, D_MODEL).astype(F32)
    g_ffn = g_ffn_norm[l].reshape(1, D_MODEL).astype(F32)
    max_seq = max(x.shape[1] for x, _ in groups)
    cos_t, sin_t = _rope_tables(max_seq)

    rows = lax.broadcasted_iota(jnp.int32, (WO_BLOCK, WO_BLOCK), 0)
    cols = lax.broadcasted_iota(jnp.int32, (WO_BLOCK, WO_BLOCK), 1)
    tri = (cols < rows).astype(BF16)
    cnt0 = jnp.zeros((1, LANES), F32)

    bg = b_gate[l].reshape(N_EXPERTS, 1, D_MODEL).astype(F32)
    bu = b_up[l].reshape(N_EXPERTS, 1, D_MODEL).astype(F32)
    bdn = b_down[l].reshape(N_EXPERTS, 1, D_MODEL).astype(F32)
    ada_groups = [ada_all[:nb[0]], ada_all[nb[0]:]]

    order = sorted(range(len(groups)), key=lambda g: -groups[g][0].shape[1])
    staged = {}
    for g in order:
        x = groups[g][0]
        b, seq = x.shape[0], x.shape[1]
        n = b * seq
        x2d = x.reshape(n, D_MODEL)
        ada_g = ada_groups[g]
        qkv = _qkv(x2d, ada_g, g_attn, w_qkv_bf, gains, cos_t, sin_t, bd, seq)
        o_na = _na(qkv, bias_t, b, seq)
        o_df = _diff(qkv, lam_vecs, subln_col, b, seq)
        x1, h2, ti, tw, cnt = _wo(o_na, o_df, x2d, ada_g, w_o_bf, g_ffn, wr_hi, wr_lo, br_pad, tri, cnt0, seq)
        counts = cnt[0, :N_EXPERTS].astype(jnp.int32)
        dest, *blocks, cap = _routing(ti[:TOP_K], ti[TOP_K:], counts, n)
        xs = _sc_scatter(h2, dest, cap)
        staged[g] = (x1, tw, ada_g, dest, blocks, xs, seq, b)

    sorted_out = {}
    for g in order:
        blocks, xs = staged[g][4], staged[g][5]
        sorted_out[g] = _experts(*blocks, xs, w_gate[l], bg, w_up[l], bu, w_down[l], bdn)

    outs = [None] * len(groups)
    for g in order:
        x1, tw, ada_g, dest, blocks, xs, seq, b = staged[g]
        ys = _sc_gather(sorted_out[g], dest)
        outs[g] = _combine(x1, ys, tw, ada_g, seq, 0).reshape(b, seq, D_MODEL)
    return tuple(outs)
```

```python
import functools
import math

import jax
import jax.numpy as jnp
from jax import lax
from jax.experimental import pallas as pl
from jax.experimental.pallas import tpu as pltpu
from jax.experimental.pallas import tpu_sc as plsc

F32 = jnp.float32
BF16 = jnp.bfloat16
U32 = jnp.uint32

D_MODEL = 1024
HEAD_DIM = 64
NA_HEADS = 8
NA_WIDTH = 512
DIFF_HEADS = 4
DIFF_WIDTH = 512
QKV_COLS = 3072
GRID_W = 64
NA_WIN_ROWS = 8
NA_WIN_COLS = 16
ROPE_THETA = 10000.0
N_EXPERTS = 32
TOP_K = 4
SWIGLU_LIMIT = 7.0
SWIGLU_ALPHA = 1.702
EPS = 1e-5
NEG_INF = -1e30
LAMBDA_INIT = 0.8 - 0.6 * math.exp(-0.3 * 0)
LOG2E = 1.4426950408889634

LANES = 128
MXU_DIM = 256
VMEM_LIMIT = 56 * 1024 * 1024

ROW_BLOCK = 1024
QKV_BLOCK = 1024
Q_BLOCK = 1024
EXPERT_BLOCK = 1024
EXPERT_HALF = 512
EXPERT_SUB = 256
WO_BLOCK = 1024
WO_SUB = 256
NA_ROWS_PER_TRIP = 16
VT_ROWS = LANES + 16
KV_CHUNK = 256
NORM_SLACK = 1.01
MAX_SAFE_BOUND = 60.0


PACKED = D_MODEL // 2
SC_CORES = 2
SC_SUBCORES = 16
SC_WORKERS = SC_CORES * SC_SUBCORES
GATHER_ROWS = 64


def _params(*sem):
    return pltpu.CompilerParams(dimension_semantics=sem, vmem_limit_bytes=VMEM_LIMIT)


def _pack_halves(x):
    w = x.shape[1] // 2
    bits = lax.bitcast_convert_type(x, U32)
    return (bits[:, :w] >> 16) | bits[:, w:]


def _col_reduce(x, op):
    while x.shape[0] >= 64:
        x = op(x.reshape(8, x.shape[0] // 8, x.shape[1]), axis=0)
    return op(x, axis=0, keepdims=True)


def _unpack_halves(word):
    lo = lax.bitcast_convert_type(word << 16, F32)
    hi = lax.bitcast_convert_type(word & jnp.uint32(0xFFFF0000), F32)
    return lo, hi


def _ada_kernel(c_ref, w_ref, b_ref, o_ref):
    c = c_ref[...]
    s = c * jax.nn.sigmoid(c)
    o_ref[...] = jnp.dot(s, w_ref[...], preferred_element_type=F32,
                         precision=lax.Precision.HIGHEST) + b_ref[...]


def _ada(c_all, w_ada, b_ada):
    nb = c_all.shape[0]
    n_out = w_ada.shape[1]
    blk = D_MODEL
    return pl.pallas_call(
        _ada_kernel,
        grid=(n_out // blk,),
        in_specs=[pl.BlockSpec((nb, D_MODEL), lambda j: (0, 0)),
                  pl.BlockSpec((D_MODEL, blk), lambda j: (0, j)),
                  pl.BlockSpec((1, blk), lambda j: (0, j))],
        out_specs=pl.BlockSpec((nb, blk), lambda j: (0, j)),
        out_shape=jax.ShapeDtypeStruct((nb, n_out), F32),
        compiler_params=_params("arbitrary"),
        name="ada",
    )(c_all, w_ada, b_ada.reshape(1, n_out))


def _head_sumsq(y, bd):
    sq = (y * y).astype(BF16)
    parts = [jnp.dot(sq[:, c:c + MXU_DIM], bd, preferred_element_type=F32)
             for c in range(0, y.shape[1], MXU_DIM)]
    return jnp.concatenate(parts, axis=1)


def _qkv_kernel(x_ref, ada_ref, g_ref, w_ref, gain_ref, cos_ref, sin_ref, bd_ref, o_ref):
    x = x_ref[...]
    ms = jnp.mean(x * x, axis=-1, keepdims=True)
    xn = x * lax.rsqrt(ms + EPS) * g_ref[...]
    sh = ada_ref[0, 0:1, :]
    sc = ada_ref[0, 1:2, :]
    h = (xn * (1.0 + sc) + sh).astype(BF16)
    bd = bd_ref[...]
    lane = lax.broadcasted_iota(jnp.int32, (x.shape[0], NA_WIDTH), 1)
    first_half = (lane & (HEAD_DIM // 2)) == 0
    for grp in range(6):
        cols = slice(grp * 512, (grp + 1) * 512)
        acc = jnp.dot(h, w_ref[:, cols], preferred_element_type=F32)
        if grp in (2, 5):
            o_ref[:, cols] = acc.astype(BF16)
            continue
        gi = {0: 0, 1: 1, 3: 2, 4: 3}[grp]
        ss = _head_sumsq(acc, bd)
        y = acc * lax.rsqrt(ss * (1.0 / HEAD_DIM) + EPS) * gain_ref[gi:gi + 1, :]
        if grp in (3, 4):
            partner = jnp.where(first_half,
                                pltpu.roll(y, NA_WIDTH - HEAD_DIM // 2, axis=1),
                                pltpu.roll(y, HEAD_DIM // 2, axis=1))
            reps = NA_WIDTH // LANES
            y = (y * jnp.concatenate([cos_ref[...]] * reps, axis=1)
                 + partner * jnp.concatenate([sin_ref[...]] * reps, axis=1))
        o_ref[:, cols] = y.astype(BF16)


def _qkv(x2d, ada_g, g_attn, w_qkv_bf, gains, cos_t, sin_t, bd, seq):
    n = x2d.shape[0]
    tm = QKV_BLOCK
    per_seq = seq // tm
    return pl.pallas_call(
        _qkv_kernel,
        grid=(n // tm,),
        in_specs=[pl.BlockSpec((tm, D_MODEL), lambda i: (i, 0)),
                  pl.BlockSpec((1, 6, D_MODEL), lambda i: (i // per_seq, 0, 0)),
                  pl.BlockSpec((1, D_MODEL), lambda i: (0, 0)),
                  pl.BlockSpec((D_MODEL, QKV_COLS), lambda i: (0, 0)),
                  pl.BlockSpec((4, NA_WIDTH), lambda i: (0, 0)),
                  pl.BlockSpec((tm, LANES), lambda i: (i % per_seq, 0)),
                  pl.BlockSpec((tm, LANES), lambda i: (i % per_seq, 0)),
                  pl.BlockSpec((MXU_DIM, MXU_DIM), lambda i: (0, 0))],
        out_specs=pl.BlockSpec((tm, QKV_COLS), lambda i: (i, 0)),
        out_shape=jax.ShapeDtypeStruct((n, QKV_COLS), BF16),
        compiler_params=_params("arbitrary"),
        name="qkv",
    )(x2d, ada_g, g_attn, w_qkv_bf, gains, cos_t, sin_t, bd)


def _na_kernel(q_ref, k_ref, v_ref, bias_ref, o_ref, vaug_ref, biasb_ref, *, rows):
    lane = lax.broadcasted_iota(jnp.int32, (GRID_W, LANES), 1)
    head0 = lane < HEAD_DIM
    win = NA_WIN_ROWS * GRID_W

    def window_start(r):
        return jnp.clip(r - NA_WIN_ROWS // 2, 0, rows - NA_WIN_ROWS)

    def scores(r, bias_table):
        r_start = window_start(r)
        q = q_ref[pl.ds(pl.multiple_of(r * GRID_W, GRID_W), GRID_W), :]
        kw = k_ref[pl.ds(pl.multiple_of(r_start * GRID_W, GRID_W), win), :]
        zero = jnp.zeros_like(q)
        qm = jnp.concatenate([jnp.where(head0, q, zero), jnp.where(head0, zero, q)], axis=0)
        s = lax.dot_general(kw, qm, (((1,), (1,)), ((), ())), preferred_element_type=F32)
        return s + bias_table[r - r_start]

    vaug_ref[:, :LANES] = v_ref[...]
    vaug_ref[:, LANES:] = jnp.ones((v_ref.shape[0], LANES), BF16)

    def finish(r, s, shifted):
        vw = vaug_ref[pl.ds(pl.multiple_of(window_start(r) * GRID_W, GRID_W), win), :]
        p = jnp.exp2(s if shifted else s - _col_reduce(s, jnp.max)).astype(BF16)
        o2 = lax.dot_general(p, vw, (((0,), (0,)), ((), ())), preferred_element_type=F32)
        o2 = o2[:, :LANES] * (1.0 / o2[:, LANES:])
        o = jnp.where(head0, o2[:GRID_W], o2[GRID_W:])
        o_ref[pl.ds(pl.multiple_of(r * GRID_W, GRID_W), GRID_W), :] = o.astype(BF16)

    def run(bias_table, shifted):
        def body(i, carry):
            trip_rows = [i * NA_ROWS_PER_TRIP + u for u in range(NA_ROWS_PER_TRIP)]
            trip_scores = [scores(r, bias_table) for r in trip_rows]
            for r, s in zip(trip_rows, trip_scores):
                finish(r, s, shifted)
            return carry

        lax.fori_loop(0, rows // NA_ROWS_PER_TRIP, body, 0)

    d_head = lax.broadcasted_iota(jnp.int32, (LANES, LANES), 0) // HEAD_DIM
    c_head = lax.broadcasted_iota(jnp.int32, (LANES, LANES), 1) // HEAD_DIM
    head_sel = jnp.where(d_head == c_head, 1.0, 0.0).astype(BF16)

    def max_norm(ref):
        x = ref[...].astype(F32)
        n2 = jnp.dot((x * x).astype(BF16), head_sel, preferred_element_type=F32)
        return jnp.sqrt(_col_reduce(n2, jnp.max)) * NORM_SLACK

    qk = max_norm(q_ref) * max_norm(k_ref)
    flat_bias = bias_ref[0].reshape(NA_WIN_ROWS * win, LANES)
    bias_max = _col_reduce(flat_bias, jnp.max)
    bias_min = _col_reduce(jnp.where(flat_bias > 0.5 * NEG_INF, flat_bias, -NEG_INF), jnp.min)
    in_range = jnp.max(2.0 * qk + (bias_max - bias_min)) <= 2.0 * MAX_SAFE_BOUND

    @pl.when(in_range)
    def _():
        biasb_ref[...] = bias_ref[0] - (qk + bias_max)
        run(biasb_ref, True)

    @pl.when(jnp.logical_not(in_range))
    def _():
        run(bias_ref.at[0], False)


def _na(qkv, bias_t, batch, seq):
    rows = seq // GRID_W
    n_pairs = NA_HEADS // 2
    return pl.pallas_call(
        functools.partial(_na_kernel, rows=rows),
        grid=(batch, n_pairs),
        in_specs=[pl.BlockSpec((seq, LANES), lambda b, hp: (b, hp)),
                  pl.BlockSpec((seq, LANES), lambda b, hp: (b, n_pairs + hp)),
                  pl.BlockSpec((seq, LANES), lambda b, hp: (b, 2 * n_pairs + hp)),
                  pl.BlockSpec((1, NA_WIN_ROWS, NA_WIN_ROWS * GRID_W, LANES), lambda b, hp: (hp, 0, 0, 0))],
        out_specs=pl.BlockSpec((seq, LANES), lambda b, hp: (b, hp)),
        out_shape=jax.ShapeDtypeStruct((batch * seq, NA_WIDTH), BF16),
        scratch_shapes=[pltpu.VMEM((seq, 2 * LANES), BF16),
                        pltpu.VMEM((NA_WIN_ROWS, NA_WIN_ROWS * GRID_W, LANES), F32)],
        compiler_params=_params("arbitrary", "arbitrary"),
        name="na_attn",
    )(qkv, qkv, qkv, bias_t)


def _diff_kernel(q_ref, k_ref, v_ref, lam_ref, g_ref, o_ref, vt_ref, kn_ref, oa_ref):
    @pl.when(pl.program_id(2) == 0)
    def _():
        vt_ref[:LANES, :] = v_ref[...].astype(F32).T.astype(BF16)
        ones_row = lax.broadcasted_iota(jnp.int32, (VT_ROWS - LANES, v_ref.shape[0]), 0) == 0
        vt_ref[LANES:, :] = jnp.where(ones_row, 1.0, 0.0).astype(BF16)
        kf = k_ref[...].astype(F32)
        d_id = lax.broadcasted_iota(jnp.int32, (LANES, LANES), 0) // HEAD_DIM
        c_id = lax.broadcasted_iota(jnp.int32, (LANES, LANES), 1)
        comp_sel = jnp.where(d_id == c_id, 1.0, 0.0).astype(BF16)
        kn2 = jnp.dot((kf * kf).astype(BF16), comp_sel, preferred_element_type=F32)
        kn_ref[...] = jnp.sqrt(_col_reduce(kn2, jnp.max)) * NORM_SLACK

    lq1 = lam_ref[0:1, :]
    lk1 = lam_ref[1:2, :]
    lq2 = lam_ref[2:3, :]
    lk2 = lam_ref[3:4, :]
    lam = (jnp.exp(jnp.sum(lq1 * lk1, axis=-1, keepdims=True))
           - jnp.exp(jnp.sum(lq2 * lk2, axis=-1, keepdims=True)) + LAMBDA_INIT)

    q = q_ref[...]
    lane = lax.broadcasted_iota(jnp.int32, q.shape, 1)
    zero = jnp.zeros_like(q)
    tq = q.shape[0]
    qcat = jnp.concatenate([jnp.where(lane < HEAD_DIM, q, zero), jnp.where(lane < HEAD_DIM, zero, q)], axis=0)
    def scores(c, chunk):
        kc = k_ref[c * chunk:(c + 1) * chunk, :]
        return lax.dot_general(kc, qcat, (((1,), (1,)), ((), ())), preferred_element_type=F32)

    def chunked(chunk, step):
        n_chunks = k_ref.shape[0] // chunk
        s_next = scores(0, chunk)
        state = None
        for c in range(n_chunks):
            s = s_next
            if c + 1 < n_chunks:
                s_next = scores(c + 1, chunk)
            state = step(c, s, vt_ref[:, c * chunk:(c + 1) * chunk], state)
        return state

    qf = qcat.astype(F32)
    ones8 = jnp.ones((8, LANES), BF16)
    qn2 = lax.dot_general(ones8, (qf * qf).astype(BF16), (((1,), (1,)), ((), ())), preferred_element_type=F32)
    col = lax.broadcasted_iota(jnp.int32, (1, 2 * tq), 1)
    bound = jnp.sqrt(qn2[0:1, :]) * NORM_SLACK * jnp.where(col < tq, kn_ref[0:1, 0:1], kn_ref[0:1, 1:2])
    in_range = jnp.max(bound) <= MAX_SAFE_BOUND

    @pl.when(in_range)
    def _():
        def step(c, s, vtc, acc):
            pv = jnp.dot(vtc, jnp.exp2(s - bound).astype(BF16), preferred_element_type=F32)
            return pv if c == 0 else acc + pv
        oa_ref[...] = chunked(KV_CHUNK, step)

    @pl.when(jnp.logical_not(in_range))
    def _():
        def step(c, s, vtc, state):
            mc = _col_reduce(s, jnp.max)
            m_new = mc if c == 0 else jnp.maximum(state[0], mc)
            pv = jnp.dot(vtc, jnp.exp2(s - m_new).astype(BF16), preferred_element_type=F32)
            return (m_new, pv if c == 0 else jnp.exp2(state[0] - m_new) * state[1] + pv)
        oa_ref[...] = chunked(KV_CHUNK, step)[1]

    oa = oa_ref[...]
    o0, l0 = oa[:LANES, :tq], oa[LANES:LANES + 1, :tq]
    o1, l1 = oa[:LANES, tq:], oa[LANES:LANES + 1, tq:]
    o = o0 * (1.0 / l0) - (lam / l1) * o1
    ms = jnp.mean(o * o, axis=0, keepdims=True)
    y = o * lax.rsqrt(ms + EPS) * g_ref[...]
    o_ref[...] = y.T.astype(BF16)


def _diff(qkv, lam_vecs, subln_col, batch, seq):
    tq = Q_BLOCK
    nq = seq // tq
    base = 3 * NA_WIDTH // LANES
    nh = DIFF_HEADS
    return pl.pallas_call(
        _diff_kernel,
        grid=(batch, nh, nq),
        in_specs=[pl.BlockSpec((tq, LANES), lambda b, h, i: (b * nq + i, base + h)),
                  pl.BlockSpec((seq, LANES), lambda b, h, i: (b, base + nh + h)),
                  pl.BlockSpec((seq, LANES), lambda b, h, i: (b, base + 2 * nh + h)),
                  pl.BlockSpec((4, HEAD_DIM), lambda b, h, i: (0, 0)),
                  pl.BlockSpec((LANES, 1), lambda b, h, i: (0, 0))],
        out_specs=pl.BlockSpec((tq, LANES), lambda b, h, i: (b * nq + i, h)),
        out_shape=jax.ShapeDtypeStruct((batch * seq, DIFF_WIDTH), BF16),
        scratch_shapes=[pltpu.VMEM((VT_ROWS, seq), BF16),
                        pltpu.VMEM((1, LANES), F32),
                        pltpu.VMEM((VT_ROWS, 2 * tq), F32)],
        compiler_params=_params("arbitrary", "arbitrary", "arbitrary"),
        name="diff_attn",
    )(qkv, qkv, qkv, lam_vecs, subln_col)


def _wo_kernel(ona_ref, odf_ref, x_ref, ada_ref, wo_ref, g_ref, wrh_ref, wrl_ref, br_ref, tri_ref, cnt0_ref,
               x1_ref, h2_ref, ti_ref, tw_ref, cnt_ref):
    gt1 = ada_ref[0, 2:3, :]
    sh2 = ada_ref[0, 3:4, :]
    sc2 = ada_ref[0, 4:5, :]
    lane = lax.broadcasted_iota(jnp.int32, (WO_SUB, LANES), 1).astype(F32)

    def mix_of(rows):
        return (jnp.dot(ona_ref[rows, :], wo_ref[:NA_WIDTH, :], preferred_element_type=F32)
                + jnp.dot(odf_ref[rows, :], wo_ref[NA_WIDTH:, :], preferred_element_type=F32))

    def route(rows, mix):
        x1 = x_ref[rows, :] + gt1 * mix
        x1_ref[rows, :] = x1
        ms = jnp.mean(x1 * x1, axis=-1, keepdims=True)
        h2 = x1 * lax.rsqrt(ms + EPS) * g_ref[...] * (1.0 + sc2) + sh2
        hi = h2.astype(BF16)
        h2_ref[rows, :] = _pack_halves(hi.astype(F32))
        lo = (h2 - hi.astype(F32)).astype(BF16)
        both = jnp.dot(hi, wrl_ref[...], preferred_element_type=F32)
        cur = (both + pltpu.roll(both, LANES - N_EXPERTS, axis=1)
               + jnp.dot(lo, wrh_ref[...], preferred_element_type=F32)) + br_ref[...]
        vals = []
        idxs = []
        for _ in range(TOP_K):
            m = jnp.max(cur, axis=-1, keepdims=True)
            idx = jnp.min(jnp.where(cur == m, lane, float(LANES)), axis=-1, keepdims=True)
            vals.append(m)
            idxs.append(idx)
            cur = jnp.where(lane == idx, -jnp.inf, cur)
        es = [jnp.exp(v - vals[0]) for v in vals]
        inv = 1.0 / (es[0] + es[1] + es[2] + es[3])
        sel = jnp.zeros((WO_SUB, LANES), F32)
        for j in range(TOP_K):
            sel = sel + jnp.where(lane == idxs[j], 1.0, 0.0)
        return idxs, [e * inv for e in es], sel

    subs = [pl.ds(r, WO_SUB) for r in range(0, x_ref.shape[0], WO_SUB)]
    routed = []
    mix_next = mix_of(subs[0])
    for j, rows in enumerate(subs):
        mix = mix_next
        if j + 1 < len(subs):
            mix_next = mix_of(subs[j + 1])
        routed.append(route(rows, mix))

    @pl.when(pl.program_id(0) == 0)
    def _():
        cnt_ref[...] = cnt0_ref[...]

    sel_all = jnp.concatenate([r[2] for r in routed], axis=0)
    before = jnp.dot(tri_ref[...], sel_all.astype(BF16), preferred_element_type=F32) + cnt_ref[...]
    cnt_ref[...] = cnt_ref[...] + jnp.sum(sel_all, axis=0, keepdims=True)

    for s, rows in enumerate(subs):
        idxs, wts, _ = routed[s]
        bef = before[s * WO_SUB:(s + 1) * WO_SUB]
        ti = jnp.zeros((WO_SUB, LANES), F32)
        tw = jnp.zeros((WO_SUB, LANES), F32)
        for j in range(TOP_K):
            rank = jnp.sum(jnp.where(lane == idxs[j], bef, 0.0), axis=-1, keepdims=True)
            ti = jnp.where(lane == float(j), idxs[j], ti)
            ti = jnp.where(lane == float(TOP_K + j), rank, ti)
            tw = jnp.where(lane == float(j), wts[j], tw)
        ti_ref[:, s * WO_SUB:(s + 1) * WO_SUB] = ti.T[:2 * TOP_K, :].astype(jnp.int32)
        tw_ref[rows, :] = tw


def _wo(o_na, o_df, x2d, ada_g, w_o_bf, g_ffn, wr_hi, wr_lo, br_pad, tri, cnt0, seq):
    n = x2d.shape[0]
    tm = WO_BLOCK
    per_seq = seq // tm
    row = lambda i: (i, 0)
    const = lambda i: (0, 0)
    return pl.pallas_call(
        _wo_kernel,
        grid=(n // tm,),
        in_specs=[pl.BlockSpec((tm, NA_WIDTH), row),
                  pl.BlockSpec((tm, DIFF_WIDTH), row),
                  pl.BlockSpec((tm, D_MODEL), row),
                  pl.BlockSpec((1, 6, D_MODEL), lambda i: (i // per_seq, 0, 0)),
                  pl.BlockSpec((D_MODEL, D_MODEL), const),
                  pl.BlockSpec((1, D_MODEL), const),
                  pl.BlockSpec((D_MODEL, LANES), const),
                  pl.BlockSpec((D_MODEL, LANES), const),
                  pl.BlockSpec((1, LANES), const),
                  pl.BlockSpec((tm, tm), const),
                  pl.BlockSpec((1, LANES), const)],
        out_specs=[pl.BlockSpec((tm, D_MODEL), row),
                   pl.BlockSpec((tm, PACKED), row),
                   pl.BlockSpec((2 * TOP_K, tm), lambda i: (0, i)),
                   pl.BlockSpec((tm, LANES), row),
                   pl.BlockSpec((1, LANES), const)],
        out_shape=[jax.ShapeDtypeStruct((n, D_MODEL), F32),
                   jax.ShapeDtypeStruct((n, PACKED), U32),
                   jax.ShapeDtypeStruct((2 * TOP_K, n), jnp.int32),
                   jax.ShapeDtypeStruct((n, LANES), F32),
                   jax.ShapeDtypeStruct((1, LANES), F32)],
        compiler_params=_params("arbitrary"),
        name="wo_router",
    )(o_na, o_df, x2d, ada_g, w_o_bf, g_ffn, wr_hi, wr_lo, br_pad, tri, cnt0)


def _expert_kernel(be_ref, nxt_ref, fill_ref, na_ref, xs_ref, wg_hbm, bg_ref, wu_hbm, bu_ref, wd_hbm, bd_ref, o_ref,
                   w_f32, w_bf, sems, slot_ref):
    i = pl.program_id(0)
    active = i < na_ref[0]
    expert = be_ref[i]
    new_expert = jnp.logical_or(i == 0, expert != be_ref[jnp.maximum(i - 1, 0)])
    wg_bf, wu_bf, wd_bf = w_bf.at[0], w_bf.at[1], w_bf.at[2]

    def weight_copies(src_expert, slot):
        return [pltpu.make_async_copy(w_hbm.at[src_expert], w_f32.at[slot, j], sems.at[slot, j])
                for j, w_hbm in enumerate((wg_hbm, wu_hbm, wd_hbm))]

    @pl.when(i == 0)
    def _():
        slot_ref[0] = 0
        for cp in weight_copies(expert, 0):
            cp.start()

    @pl.when(jnp.logical_and(active, new_expert))
    def _():
        slot = slot_ref[0]
        nxt = nxt_ref[i]
        for s in range(2):
            @pl.when(slot == s)
            def _():
                @pl.when(nxt >= 0)
                def _():
                    for cp in weight_copies(nxt, 1 - s):
                        cp.start(priority=1)
                for j, cp in enumerate(weight_copies(expert, s)):
                    cp.wait()
                    w_bf[j] = w_f32[s, j].astype(BF16)
        slot_ref[0] = 1 - slot

    def run_half(first_row, n_sub):
        def gate_up(rows):
            x_lo, x_hi = _unpack_halves(xs_ref[rows, :])
            x_lo = x_lo.astype(BF16)
            x_hi = x_hi.astype(BF16)

            def proj(w_bf):
                return (jnp.dot(x_lo, w_bf[:PACKED, :], preferred_element_type=F32)
                        + jnp.dot(x_hi, w_bf[PACKED:, :], preferred_element_type=F32))

            return proj(wg_bf), proj(wu_bf)

        def act_down(rows, gu):
            g = jnp.minimum(gu[0] + bg_ref[0], SWIGLU_LIMIT)
            u = jnp.clip(gu[1] + bu_ref[0], -SWIGLU_LIMIT, SWIGLU_LIMIT)
            act = g * jax.nn.sigmoid(SWIGLU_ALPHA * g) * (u + 1.0)
            out = jnp.dot(act.astype(BF16), wd_bf[...], preferred_element_type=F32) + bd_ref[0]
            o_ref[rows, :] = _pack_halves(out.astype(BF16).astype(F32))

        sub = [pl.ds(first_row + r * EXPERT_SUB, EXPERT_SUB) for r in range(n_sub)]
        gu_next = gate_up(sub[0])
        for j, rows in enumerate(sub):
            gu = gu_next
            if j + 1 < len(sub):
                gu_next = gate_up(sub[j + 1])
            act_down(rows, gu)

    fill = jnp.where(active, fill_ref[i], 0)
    subs_per_half = EXPERT_HALF // EXPERT_SUB

    @pl.when(fill < 2 * subs_per_half)
    def _():
        o_ref[...] = jnp.zeros_like(o_ref)

    for half in range(2):
        for n_sub in range(1, subs_per_half + 1):
            here = fill - half * subs_per_half
            cond = (here >= n_sub) if n_sub == subs_per_half else (here == n_sub)

            @pl.when(cond)
            def _(half=half, n_sub=n_sub):
                run_half(half * EXPERT_HALF, n_sub)


def _experts(block_e, next_e, fill, n_active, xs, wg, bg, wu, bu, wd, bd):
    cap = xs.shape[0]
    n_blocks = cap // EXPERT_BLOCK
    xmap = lambda i, be, nx, fl, na: (jnp.minimum(i, na[0] - 1), 0)
    bmap = lambda i, be, nx, fl, na: (be[i], 0, 0)
    hbm = pl.BlockSpec(memory_space=pl.ANY)
    grid_spec = pltpu.PrefetchScalarGridSpec(
        num_scalar_prefetch=4,
        grid=(n_blocks,),
        in_specs=[pl.BlockSpec((EXPERT_BLOCK, PACKED), xmap),
                  hbm, pl.BlockSpec((1, 1, D_MODEL), bmap),
                  hbm, pl.BlockSpec((1, 1, D_MODEL), bmap),
                  hbm, pl.BlockSpec((1, 1, D_MODEL), bmap)],
        out_specs=pl.BlockSpec((EXPERT_BLOCK, PACKED), lambda i, be, nx, fl, na: (i, 0)),
        scratch_shapes=[pltpu.VMEM((2, 3, D_MODEL, D_MODEL), F32),
                        pltpu.VMEM((3, D_MODEL, D_MODEL), BF16),
                        pltpu.SemaphoreType.DMA((2, 3)),
                        pltpu.SMEM((1,), jnp.int32)],
    )
    return pl.pallas_call(
        _expert_kernel,
        grid_spec=grid_spec,
        out_shape=jax.ShapeDtypeStruct((cap, PACKED), U32),
        compiler_params=_params("arbitrary"),
        name="experts",
    )(block_e, next_e, fill, n_active, xs, wg, bg, wu, bu, wd, bd)


def _sc_gather(table, idx):
    n_out = idx.shape[0]
    width = table.shape[1]
    per_worker = n_out // SC_WORKERS
    n_chunks = per_worker // GATHER_ROWS
    assert per_worker * SC_WORKERS == n_out and n_chunks * GATHER_ROWS == per_worker and n_chunks % 2 == 0
    idx3 = idx.reshape(SC_WORKERS, n_chunks, GATHER_ROWS)
    mesh = plsc.VectorSubcoreMesh(core_axis_name="core", subcore_axis_name="subcore")

    @functools.partial(
        pl.kernel, mesh=mesh,
        out_type=jax.ShapeDtypeStruct((n_out, width), table.dtype),
        scratch_types=[pltpu.VMEM((n_chunks, GATHER_ROWS), jnp.int32),
                       pltpu.VMEM((2, GATHER_ROWS, width), table.dtype),
                       pltpu.SemaphoreType.DMA((2,)),
                       pltpu.SemaphoreType.DMA((2,))])
    def gather_kernel(table_hbm, idx_hbm, out_hbm, idx_v, rows_v, gsem, wsem):
        wid = lax.axis_index("subcore") * SC_CORES + lax.axis_index("core")
        base = wid * per_worker
        pltpu.sync_copy(idx_hbm.at[wid], idx_v)

        def gather(j, slot):
            return pltpu.make_async_copy(table_hbm.at[idx_v.at[j]], rows_v.at[slot], gsem.at[slot])

        def write(j, slot):
            dst = out_hbm.at[pl.ds(pl.multiple_of(base + j * GATHER_ROWS, GATHER_ROWS), GATHER_ROWS)]
            return pltpu.make_async_copy(rows_v.at[slot], dst, wsem.at[slot])

        gather(0, 0).start()

        @pl.loop(0, n_chunks, step=2)
        def _(j):
            for slot in range(2):
                jj = j + slot
                gather(jj, slot).wait()

                @pl.when(jj >= 1)
                def _():
                    write(jj - 1, 1 - slot).wait()

                @pl.when(jj + 1 < n_chunks)
                def _():
                    gather(jj + 1, 1 - slot).start()

                write(jj, slot).start()

        write(n_chunks - 1, 1).wait()

    return gather_kernel(table, idx3)


def _sc_scatter(rows, idx, n_out):
    n_src, width = rows.shape
    n_idx = idx.shape[0]
    per_worker = n_idx // SC_WORKERS
    n_chunks = per_worker // GATHER_ROWS
    assert per_worker * SC_WORKERS == n_idx and n_chunks * GATHER_ROWS == per_worker and n_chunks % 2 == 0
    assert n_src % per_worker == 0
    idx3 = idx.reshape(SC_WORKERS, n_chunks, GATHER_ROWS)
    mesh = plsc.VectorSubcoreMesh(core_axis_name="core", subcore_axis_name="subcore")

    @functools.partial(
        pl.kernel, mesh=mesh,
        out_type=jax.ShapeDtypeStruct((n_out, width), rows.dtype),
        scratch_types=[pltpu.VMEM((n_chunks, GATHER_ROWS), jnp.int32),
                       pltpu.VMEM((2, GATHER_ROWS, width), rows.dtype),
                       pltpu.SemaphoreType.DMA((2,)),
                       pltpu.SemaphoreType.DMA((2,))])
    def scatter_kernel(rows_hbm, idx_hbm, out_hbm, idx_v, rows_v, rsem, wsem):
        wid = lax.axis_index("subcore") * SC_CORES + lax.axis_index("core")
        base = lax.rem(wid * per_worker, n_src)
        pltpu.sync_copy(idx_hbm.at[wid], idx_v)

        def read(j, slot):
            src = rows_hbm.at[pl.ds(pl.multiple_of(base + j * GATHER_ROWS, GATHER_ROWS), GATHER_ROWS)]
            return pltpu.make_async_copy(src, rows_v.at[slot], rsem.at[slot])

        def write(j, slot):
            return pltpu.make_async_copy(rows_v.at[slot], out_hbm.at[idx_v.at[j]], wsem.at[slot])

        read(0, 0).start()

        @pl.loop(0, n_chunks, step=2)
        def _(j):
            for slot in range(2):
                jj = j + slot
                read(jj, slot).wait()

                @pl.when(jj >= 1)
                def _():
                    write(jj - 1, 1 - slot).wait()

                @pl.when(jj + 1 < n_chunks)
                def _():
                    read(jj + 1, 1 - slot).start()

                write(jj, slot).start()

        write(n_chunks - 1, 1).wait()

    return scatter_kernel(rows, idx3)


def _combine_kernel(x1_ref, y0_ref, y1_ref, y2_ref, y3_ref, tw_ref, ada_ref, o_ref):
    tw = tw_ref[...]
    acc_lo = jnp.zeros((x1_ref.shape[0], PACKED), F32)
    acc_hi = jnp.zeros((x1_ref.shape[0], PACKED), F32)
    for j, y_ref in enumerate((y0_ref, y1_ref, y2_ref, y3_ref)):
        lo, hi = _unpack_halves(y_ref[...])
        acc_lo = acc_lo + tw[:, j:j + 1] * lo
        acc_hi = acc_hi + tw[:, j:j + 1] * hi
    o_ref[:, :PACKED] = x1_ref[:, :PACKED] + ada_ref[0, 5:6, :PACKED] * acc_lo
    o_ref[:, PACKED:] = x1_ref[:, PACKED:] + ada_ref[0, 5:6, PACKED:] * acc_hi


def _combine(x1, ys, tw, ada_g, seq, row_off):
    n = x1.shape[0]
    tm = ROW_BLOCK
    per_seq = seq // tm
    off = row_off // tm
    per_choice = tw.shape[0] // tm
    y_specs = [pl.BlockSpec((tm, PACKED), functools.partial(lambda i, j: (j * per_choice + off + i, 0), j=j))
               for j in range(TOP_K)]
    return pl.pallas_call(
        _combine_kernel,
        grid=(n // tm,),
        in_specs=[pl.BlockSpec((tm, D_MODEL), lambda i: (i, 0)),
                  *y_specs,
                  pl.BlockSpec((tm, LANES), lambda i: (i + off, 0)),
                  pl.BlockSpec((1, 6, D_MODEL), lambda i: (i // per_seq, 0, 0))],
        out_specs=pl.BlockSpec((tm, D_MODEL), lambda i: (i, 0)),
        out_shape=jax.ShapeDtypeStruct((n, D_MODEL), F32),
        compiler_params=_params("arbitrary"),
        name="combine",
    )(x1, ys, ys, ys, ys, tw, ada_g)


def _rope_tables(seq):
    half = HEAD_DIM // 2
    inv = ROPE_THETA ** (-jnp.arange(half, dtype=F32) / half)
    ang = jnp.arange(seq, dtype=F32)[:, None] * inv[None, :]
    cos, sin = jnp.cos(ang), jnp.sin(ang)
    cos_h = jnp.concatenate([cos, cos], axis=-1)
    sin_h = jnp.concatenate([-sin, sin], axis=-1)
    reps = LANES // HEAD_DIM
    return jnp.tile(cos_h, (1, reps)), jnp.tile(sin_h, (1, reps))


def _na_bias_table(rpb):
    cols = jnp.arange(GRID_W, dtype=jnp.int32)
    c_start = jnp.clip(cols - NA_WIN_COLS // 2, 0, GRID_W - NA_WIN_COLS)
    col_mask = (cols[None, :] >= c_start[:, None]) & (cols[None, :] < c_start[:, None] + NA_WIN_COLS)
    col_idx = jnp.clip(cols[None, :] - cols[:, None], -(NA_WIN_COLS - 1), NA_WIN_COLS - 1) + NA_WIN_COLS - 1
    delta = jnp.arange(NA_WIN_ROWS, dtype=jnp.int32)
    j = jnp.arange(NA_WIN_ROWS, dtype=jnp.int32)
    row_idx = j[None, :] - delta[:, None] + NA_WIN_ROWS - 1
    row_hot = (row_idx[:, :, None] == jnp.arange(2 * NA_WIN_ROWS - 1, dtype=jnp.int32)).astype(F32)
    col_hot = (col_idx[:, :, None] == jnp.arange(2 * NA_WIN_COLS - 1, dtype=jnp.int32)).astype(F32)
    rpb_pairs = rpb.astype(F32).reshape(NA_HEADS // 2, 2, 2 * NA_WIN_ROWS - 1, 2 * NA_WIN_COLS - 1)
    pair_hot = jnp.eye(2, dtype=F32)
    lane_hot = jnp.einsum('qkc,hx->khqxc', col_hot, pair_hot).reshape(GRID_W, LANES, 2, 2 * NA_WIN_COLS - 1)
    bias = jnp.einsum('djr,pxrc,klxc->pdjkl', row_hot, rpb_pairs, lane_hot,
                      precision=lax.Precision.HIGHEST)
    lane_mask = jnp.concatenate([col_mask.T, col_mask.T], axis=1)
    bias = jnp.where(lane_mask[None, None, None], bias * LOG2E, NEG_INF)
    return bias.reshape(NA_HEADS // 2, NA_WIN_ROWS, NA_WIN_ROWS * GRID_W, LANES)


def _routing(top_idx, rank, counts, n):
    n_blocks = n * TOP_K // EXPERT_BLOCK + N_EXPERTS
    experts = jnp.arange(N_EXPERTS, dtype=jnp.int32)
    padded = (counts + EXPERT_BLOCK - 1) // EXPERT_BLOCK * EXPERT_BLOCK
    pad_end = jnp.cumsum(padded)
    pad_start = pad_end - padded
    start_of = jnp.sum(jnp.where(top_idx[None] == experts[:, None, None], pad_start[:, None, None], 0), axis=0)
    dest = (start_of + rank).reshape(-1)
    block_lo = jnp.arange(n_blocks, dtype=jnp.int32) * EXPERT_BLOCK
    block_e = jnp.minimum(jnp.sum((pad_end[None, :] <= block_lo[:, None]).astype(jnp.int32), axis=1),
                          N_EXPERTS - 1).astype(jnp.int32)
    n_active = (pad_end[-1] // EXPERT_BLOCK).astype(jnp.int32).reshape(1)
    later = jnp.where((experts[None, :] > experts[:, None]) & (padded[None, :] > 0), experts[None, :], N_EXPERTS)
    next_nonempty = jnp.min(later, axis=1)
    next_nonempty = jnp.where(next_nonempty == N_EXPERTS, -1, next_nonempty)
    own = block_e[:, None] == experts[None, :]
    next_e = jnp.sum(jnp.where(own, next_nonempty[None, :], 0), axis=1)
    real_end = jnp.sum(jnp.where(own, (pad_start + counts)[None, :], 0), axis=1)
    fill = jnp.clip((real_end - block_lo + EXPERT_SUB - 1) // EXPERT_SUB, 0, EXPERT_BLOCK // EXPERT_SUB)
    return dest, block_e, next_e.astype(jnp.int32), fill.astype(jnp.int32), n_active, n_blocks * EXPERT_BLOCK


def kernel(x_prompt, x_sample, c_prompt, c_sample, w_ada, b_ada, g_attn_norm, w_qkv, na_q_norm, na_k_norm, na_rpb, diff_q_norm, diff_k_norm, lambda_q1, lambda_k1, lambda_q2, lambda_k2, diff_subln, w_o, g_ffn_norm, w_router, b_router, w_gate, b_gate, w_up, b_up, w_down, b_down):
    l = 0
    groups = [(x_prompt, c_prompt), (x_sample, c_sample)]
    nb = [x.shape[0] for x, _ in groups]

    ada_all = _ada(jnp.concatenate([c for _, c in groups], axis=0), w_ada[l], b_ada[l])
    ada_all = ada_all.reshape(sum(nb), 6, D_MODEL)

    w_qkv_bf = w_qkv[l].astype(BF16)
    w_o_bf = w_o[l].astype(BF16)
    scale = HEAD_DIM ** -0.5
    reps = NA_WIDTH // HEAD_DIM
    gains = jnp.stack([jnp.tile(na_q_norm[l], reps) * (scale * LOG2E),
                       jnp.tile(na_k_norm[l], reps),
                       jnp.tile(diff_q_norm[l], reps) * (scale * LOG2E),
                       jnp.tile(diff_k_norm[l], reps)]).astype(F32)
    head_id = jnp.arange(MXU_DIM, dtype=jnp.int32) // HEAD_DIM
    bd = (head_id[:, None] == head_id[None, :]).astype(BF16)
    bias_t = _na_bias_table(na_rpb[l])
    lam_vecs = jnp.stack([lambda_q1[l], lambda_k1[l], lambda_q2[l], lambda_k2[l]]).astype(F32)
    subln_col = (diff_subln[l].astype(F32) * (1.0 - LAMBDA_INIT)).reshape(LANES, 1)
    wr = w_router[l].astype(F32)
    wr_pad = jnp.zeros((D_MODEL, LANES), F32).at[:, :N_EXPERTS].set(wr)
    wr_hi = wr_pad.astype(BF16)
    wr_lo = (wr_pad - wr_hi.astype(F32)).astype(BF16)
    wr_lo = wr_hi.at[:, N_EXPERTS:2 * N_EXPERTS].set(wr_lo[:, :N_EXPERTS])
    br_pad = jnp.full((1, LANES), NEG_INF, F32).at[0, :N_EXPERTS].set(b_router[l].astype(F32))
    g_attn = g_attn_norm[l].reshape(1, D_MODEL).astype(F32)
    g_ffn = g_ffn_norm[l].reshape(1, D_MODEL).astype(F32)
    max_seq = max(x.shape[1] for x, _ in groups)
    cos_t, sin_t = _rope_tables(max_seq)

    rows = lax.broadcasted_iota(jnp.int32, (WO_BLOCK, WO_BLOCK), 0)
    cols = lax.broadcasted_iota(jnp.int32, (WO_BLOCK, WO_BLOCK), 1)
    tri = (cols < rows).astype(BF16)
    cnt0 = jnp.zeros((1, LANES), F32)

    bg = b_gate[l].reshape(N_EXPERTS, 1, D_MODEL).astype(F32)
    bu = b_up[l].reshape(N_EXPERTS, 1, D_MODEL).astype(F32)
    bdn = b_down[l].reshape(N_EXPERTS, 1, D_MODEL).astype(F32)
    ada_groups = [ada_all[:nb[0]], ada_all[nb[0]:]]

    order = sorted(range(len(groups)), key=lambda g: -groups[g][0].shape[1])
    staged = {}
    for g in order:
        x = groups[g][0]
        b, seq = x.shape[0], x.shape[1]
        n = b * seq
        x2d = x.reshape(n, D_MODEL)
        ada_g = ada_groups[g]
        qkv = _qkv(x2d, ada_g, g_attn, w_qkv_bf, gains, cos_t, sin_t, bd, seq)
        o_na = _na(qkv, bias_t, b, seq)
        o_df = _diff(qkv, lam_vecs, subln_col, b, seq)
        x1, h2, ti, tw, cnt = _wo(o_na, o_df, x2d, ada_g, w_o_bf, g_ffn, wr_hi, wr_lo, br_pad, tri, cnt0, seq)
        counts = cnt[0, :N_EXPERTS].astype(jnp.int32)
        dest, *blocks, cap = _routing(ti[:TOP_K], ti[TOP_K:], counts, n)
        xs = _sc_scatter(h2, dest, cap)
        staged[g] = (x1, tw, ada_g, dest, blocks, xs, seq, b)

    sorted_out = {}
    for g in order:
        blocks, xs = staged[g][4], staged[g][5]
        sorted_out[g] = _experts(*blocks, xs, w_gate[l], bg, w_up[l], bu, w_down[l], bdn)

    outs = [None] * len(groups)
    for g in order:
        x1, tw, ada_g, dest, blocks, xs, seq, b = staged[g]
        ys = _sc_gather(sorted_out[g], dest)
        outs[g] = _combine(x1, ys, tw, ada_g, seq, 0).reshape(b, seq, D_MODEL)
    return tuple(outs)
```

```python
import functools
import math

import jax
import jax.numpy as jnp
from jax import lax
from jax.experimental import pallas as pl
from jax.experimental.pallas import tpu as pltpu
from jax.experimental.pallas import tpu_sc as plsc

F32 = jnp.float32
BF16 = jnp.bfloat16
U32 = jnp.uint32

D_MODEL = 1024
HEAD_DIM = 64
NA_HEADS = 8
NA_WIDTH = 512
DIFF_HEADS = 4
DIFF_WIDTH = 512
QKV_COLS = 3072
GRID_W = 64
NA_WIN_ROWS = 8
NA_WIN_COLS = 16
ROPE_THETA = 10000.0
N_EXPERTS = 32
TOP_K = 4
SWIGLU_LIMIT = 7.0
SWIGLU_ALPHA = 1.702
EPS = 1e-5
NEG_INF = -1e30
LAMBDA_INIT = 0.8 - 0.6 * math.exp(-0.3 * 0)
LOG2E = 1.4426950408889634

LANES = 128
MXU_DIM = 256
VMEM_LIMIT = 56 * 1024 * 1024

ROW_BLOCK = 512
QKV_BLOCK = 1024
Q_BLOCK = 1024
EXPERT_BLOCK = 1024
EXPERT_HALF = 512
EXPERT_SUB = 256
WO_BLOCK = 1024
WO_SUB = 256
NA_ROWS_PER_TRIP = 16
VT_ROWS = LANES + 16
KV_CHUNK = 256
NORM_SLACK = 1.01
MAX_SAFE_BOUND = 60.0


PACKED = D_MODEL // 2
SC_CORES = 2
SC_SUBCORES = 16
SC_WORKERS = SC_CORES * SC_SUBCORES
GATHER_ROWS = 64


def _params(*sem):
    return pltpu.CompilerParams(dimension_semantics=sem, vmem_limit_bytes=VMEM_LIMIT)


def _pack_halves(x):
    w = x.shape[1] // 2
    bits = lax.bitcast_convert_type(x, U32)
    return (bits[:, :w] >> 16) | bits[:, w:]


def _col_reduce(x, op):
    while x.shape[0] >= 64:
        x = op(x.reshape(8, x.shape[0] // 8, x.shape[1]), axis=0)
    return op(x, axis=0, keepdims=True)


def _unpack_halves(word):
    lo = lax.bitcast_convert_type(word << 16, F32)
    hi = lax.bitcast_convert_type(word & jnp.uint32(0xFFFF0000), F32)
    return lo, hi


def _ada_kernel(c_ref, w_ref, b_ref, o_ref):
    c = c_ref[...]
    s = c * jax.nn.sigmoid(c)
    o_ref[...] = jnp.dot(s, w_ref[...], preferred_element_type=F32,
                         precision=lax.Precision.HIGHEST) + b_ref[...]


def _ada(c_all, w_ada, b_ada):
    nb = c_all.shape[0]
    n_out = w_ada.shape[1]
    blk = D_MODEL
    return pl.pallas_call(
        _ada_kernel,
        grid=(n_out // blk,),
        in_specs=[pl.BlockSpec((nb, D_MODEL), lambda j: (0, 0)),
                  pl.BlockSpec((D_MODEL, blk), lambda j: (0, j)),
                  pl.BlockSpec((1, blk), lambda j: (0, j))],
        out_specs=pl.BlockSpec((nb, blk), lambda j: (0, j)),
        out_shape=jax.ShapeDtypeStruct((nb, n_out), F32),
        compiler_params=_params("arbitrary"),
        name="ada",
    )(c_all, w_ada, b_ada.reshape(1, n_out))


def _head_sumsq(y, bd):
    sq = (y * y).astype(BF16)
    parts = [jnp.dot(sq[:, c:c + MXU_DIM], bd, preferred_element_type=F32)
             for c in range(0, y.shape[1], MXU_DIM)]
    return jnp.concatenate(parts, axis=1)


def _qkv_kernel(x_ref, ada_ref, g_ref, w_ref, gain_ref, cos_ref, sin_ref, bd_ref, o_ref):
    x = x_ref[...]
    ms = jnp.mean(x * x, axis=-1, keepdims=True)
    xn = x * lax.rsqrt(ms + EPS) * g_ref[...]
    sh = ada_ref[0, 0:1, :]
    sc = ada_ref[0, 1:2, :]
    h = (xn * (1.0 + sc) + sh).astype(BF16)
    bd = bd_ref[...]
    lane = lax.broadcasted_iota(jnp.int32, (x.shape[0], NA_WIDTH), 1)
    first_half = (lane & (HEAD_DIM // 2)) == 0
    for grp in range(6):
        cols = slice(grp * 512, (grp + 1) * 512)
        acc = jnp.dot(h, w_ref[:, cols], preferred_element_type=F32)
        if grp in (2, 5):
            o_ref[:, cols] = acc.astype(BF16)
            continue
        gi = {0: 0, 1: 1, 3: 2, 4: 3}[grp]
        ss = _head_sumsq(acc, bd)
        y = acc * lax.rsqrt(ss * (1.0 / HEAD_DIM) + EPS) * gain_ref[gi:gi + 1, :]
        if grp in (3, 4):
            partner = jnp.where(first_half,
                                pltpu.roll(y, NA_WIDTH - HEAD_DIM // 2, axis=1),
                                pltpu.roll(y, HEAD_DIM // 2, axis=1))
            reps = NA_WIDTH // LANES
            y = (y * jnp.concatenate([cos_ref[...]] * reps, axis=1)
                 + partner * jnp.concatenate([sin_ref[...]] * reps, axis=1))
        o_ref[:, cols] = y.astype(BF16)


def _qkv(x2d, ada_g, g_attn, w_qkv_bf, gains, cos_t, sin_t, bd, seq):
    n = x2d.shape[0]
    tm = QKV_BLOCK
    per_seq = seq // tm
    return pl.pallas_call(
        _qkv_kernel,
        grid=(n // tm,),
        in_specs=[pl.BlockSpec((tm, D_MODEL), lambda i: (i, 0)),
                  pl.BlockSpec((1, 6, D_MODEL), lambda i: (i // per_seq, 0, 0)),
                  pl.BlockSpec((1, D_MODEL), lambda i: (0, 0)),
                  pl.BlockSpec((D_MODEL, QKV_COLS), lambda i: (0, 0)),
                  pl.BlockSpec((4, NA_WIDTH), lambda i: (0, 0)),
                  pl.BlockSpec((tm, LANES), lambda i: (i % per_seq, 0)),
                  pl.BlockSpec((tm, LANES), lambda i: (i % per_seq, 0)),
                  pl.BlockSpec((MXU_DIM, MXU_DIM), lambda i: (0, 0))],
        out_specs=pl.BlockSpec((tm, QKV_COLS), lambda i: (i, 0)),
        out_shape=jax.ShapeDtypeStruct((n, QKV_COLS), BF16),
        compiler_params=_params("arbitrary"),
        name="qkv",
    )(x2d, ada_g, g_attn, w_qkv_bf, gains, cos_t, sin_t, bd)


def _na_kernel(q_ref, k_ref, v_ref, bias_ref, o_ref, vaug_ref, *, rows):
    lane = lax.broadcasted_iota(jnp.int32, (GRID_W, LANES), 1)
    head0 = lane < HEAD_DIM
    win = NA_WIN_ROWS * GRID_W

    def window_start(r):
        return jnp.clip(r - NA_WIN_ROWS // 2, 0, rows - NA_WIN_ROWS)

    def scores(r):
        r_start = window_start(r)
        q = q_ref[pl.ds(pl.multiple_of(r * GRID_W, GRID_W), GRID_W), :]
        kw = k_ref[pl.ds(pl.multiple_of(r_start * GRID_W, GRID_W), win), :]
        zero = jnp.zeros_like(q)
        qm = jnp.concatenate([jnp.where(head0, q, zero), jnp.where(head0, zero, q)], axis=0)
        s = lax.dot_general(kw, qm, (((1,), (1,)), ((), ())), preferred_element_type=F32)
        return s + bias_ref[0, r - r_start]

    vaug_ref[:, :LANES] = v_ref[...]
    vaug_ref[:, LANES:] = jnp.ones((v_ref.shape[0], LANES), BF16)

    def finish(r, s):
        vw = vaug_ref[pl.ds(pl.multiple_of(window_start(r) * GRID_W, GRID_W), win), :]
        m = _col_reduce(s, jnp.max)
        p = jnp.exp2(s - m).astype(BF16)
        o2 = lax.dot_general(p, vw, (((0,), (0,)), ((), ())), preferred_element_type=F32)
        o2 = o2[:, :LANES] * (1.0 / o2[:, LANES:])
        o = jnp.where(head0, o2[:GRID_W], o2[GRID_W:])
        o_ref[pl.ds(pl.multiple_of(r * GRID_W, GRID_W), GRID_W), :] = o.astype(BF16)

    def body(i, carry):
        trip_rows = [i * NA_ROWS_PER_TRIP + u for u in range(NA_ROWS_PER_TRIP)]
        trip_scores = [scores(r) for r in trip_rows]
        for r, s in zip(trip_rows, trip_scores):
            finish(r, s)
        return carry

    lax.fori_loop(0, rows // NA_ROWS_PER_TRIP, body, 0)


def _na(qkv, bias_t, batch, seq):
    rows = seq // GRID_W
    n_pairs = NA_HEADS // 2
    return pl.pallas_call(
        functools.partial(_na_kernel, rows=rows),
        grid=(batch, n_pairs),
        in_specs=[pl.BlockSpec((seq, LANES), lambda b, hp: (b, hp)),
                  pl.BlockSpec((seq, LANES), lambda b, hp: (b, n_pairs + hp)),
                  pl.BlockSpec((seq, LANES), lambda b, hp: (b, 2 * n_pairs + hp)),
                  pl.BlockSpec((1, NA_WIN_ROWS, NA_WIN_ROWS * GRID_W, LANES), lambda b, hp: (hp, 0, 0, 0))],
        out_specs=pl.BlockSpec((seq, LANES), lambda b, hp: (b, hp)),
        out_shape=jax.ShapeDtypeStruct((batch * seq, NA_WIDTH), BF16),
        scratch_shapes=[pltpu.VMEM((seq, 2 * LANES), BF16)],
        compiler_params=_params("arbitrary", "arbitrary"),
        name="na_attn",
    )(qkv, qkv, qkv, bias_t)


def _diff_kernel(q_ref, k_ref, v_ref, lam_ref, g_ref, o_ref, vt_ref, kn_ref, oa_ref):
    @pl.when(pl.program_id(2) == 0)
    def _():
        vt_ref[:LANES, :] = v_ref[...].astype(F32).T.astype(BF16)
        ones_row = lax.broadcasted_iota(jnp.int32, (VT_ROWS - LANES, v_ref.shape[0]), 0) == 0
        vt_ref[LANES:, :] = jnp.where(ones_row, 1.0, 0.0).astype(BF16)
        kf = k_ref[...].astype(F32)
        d_id = lax.broadcasted_iota(jnp.int32, (LANES, LANES), 0) // HEAD_DIM
        c_id = lax.broadcasted_iota(jnp.int32, (LANES, LANES), 1)
        comp_sel = jnp.where(d_id == c_id, 1.0, 0.0).astype(BF16)
        kn2 = jnp.dot((kf * kf).astype(BF16), comp_sel, preferred_element_type=F32)
        kn_ref[...] = jnp.sqrt(_col_reduce(kn2, jnp.max)) * NORM_SLACK

    lq1 = lam_ref[0:1, :]
    lk1 = lam_ref[1:2, :]
    lq2 = lam_ref[2:3, :]
    lk2 = lam_ref[3:4, :]
    lam = (jnp.exp(jnp.sum(lq1 * lk1, axis=-1, keepdims=True))
           - jnp.exp(jnp.sum(lq2 * lk2, axis=-1, keepdims=True)) + LAMBDA_INIT)

    q = q_ref[...]
    lane = lax.broadcasted_iota(jnp.int32, q.shape, 1)
    zero = jnp.zeros_like(q)
    tq = q.shape[0]
    qcat = jnp.concatenate([jnp.where(lane < HEAD_DIM, q, zero), jnp.where(lane < HEAD_DIM, zero, q)], axis=0)
    def scores(c, chunk):
        kc = k_ref[c * chunk:(c + 1) * chunk, :]
        return lax.dot_general(kc, qcat, (((1,), (1,)), ((), ())), preferred_element_type=F32)

    def chunked(chunk, step):
        n_chunks = k_ref.shape[0] // chunk
        s_next = scores(0, chunk)
        state = None
        for c in range(n_chunks):
            s = s_next
            if c + 1 < n_chunks:
                s_next = scores(c + 1, chunk)
            state = step(c, s, vt_ref[:, c * chunk:(c + 1) * chunk], state)
        return state

    qf = qcat.astype(F32)
    ones8 = jnp.ones((8, LANES), BF16)
    qn2 = lax.dot_general(ones8, (qf * qf).astype(BF16), (((1,), (1,)), ((), ())), preferred_element_type=F32)
    col = lax.broadcasted_iota(jnp.int32, (1, 2 * tq), 1)
    bound = jnp.sqrt(qn2[0:1, :]) * NORM_SLACK * jnp.where(col < tq, kn_ref[0:1, 0:1], kn_ref[0:1, 1:2])
    in_range = jnp.max(bound) <= MAX_SAFE_BOUND

    @pl.when(in_range)
    def _():
        def step(c, s, vtc, acc):
            pv = jnp.dot(vtc, jnp.exp2(s - bound).astype(BF16), preferred_element_type=F32)
            return pv if c == 0 else acc + pv
        oa_ref[...] = chunked(KV_CHUNK, step)

    @pl.when(jnp.logical_not(in_range))
    def _():
        def step(c, s, vtc, state):
            mc = _col_reduce(s, jnp.max)
            m_new = mc if c == 0 else jnp.maximum(state[0], mc)
            pv = jnp.dot(vtc, jnp.exp2(s - m_new).astype(BF16), preferred_element_type=F32)
            return (m_new, pv if c == 0 else jnp.exp2(state[0] - m_new) * state[1] + pv)
        oa_ref[...] = chunked(KV_CHUNK, step)[1]

    oa = oa_ref[...]
    o0, l0 = oa[:LANES, :tq], oa[LANES:LANES + 1, :tq]
    o1, l1 = oa[:LANES, tq:], oa[LANES:LANES + 1, tq:]
    o = o0 * (1.0 / l0) - (lam / l1) * o1
    ms = jnp.mean(o * o, axis=0, keepdims=True)
    y = o * lax.rsqrt(ms + EPS) * g_ref[...]
    o_ref[...] = y.T.astype(BF16)


def _diff(qkv, lam_vecs, subln_col, batch, seq):
    tq = Q_BLOCK
    nq = seq // tq
    base = 3 * NA_WIDTH // LANES
    nh = DIFF_HEADS
    return pl.pallas_call(
        _diff_kernel,
        grid=(batch, nh, nq),
        in_specs=[pl.BlockSpec((tq, LANES), lambda b, h, i: (b * nq + i, base + h)),
                  pl.BlockSpec((seq, LANES), lambda b, h, i: (b, base + nh + h)),
                  pl.BlockSpec((seq, LANES), lambda b, h, i: (b, base + 2 * nh + h)),
                  pl.BlockSpec((4, HEAD_DIM), lambda b, h, i: (0, 0)),
                  pl.BlockSpec((LANES, 1), lambda b, h, i: (0, 0))],
        out_specs=pl.BlockSpec((tq, LANES), lambda b, h, i: (b * nq + i, h)),
        out_shape=jax.ShapeDtypeStruct((batch * seq, DIFF_WIDTH), BF16),
        scratch_shapes=[pltpu.VMEM((VT_ROWS, seq), BF16),
                        pltpu.VMEM((1, LANES), F32),
                        pltpu.VMEM((VT_ROWS, 2 * tq), F32)],
        compiler_params=_params("arbitrary", "arbitrary", "arbitrary"),
        name="diff_attn",
    )(qkv, qkv, qkv, lam_vecs, subln_col)


def _wo_kernel(ona_ref, odf_ref, x_ref, ada_ref, wo_ref, g_ref, wrh_ref, wrl_ref, br_ref, tri_ref, cnt0_ref,
               x1_ref, h2_ref, ti_ref, tw_ref, cnt_ref):
    gt1 = ada_ref[0, 2:3, :]
    sh2 = ada_ref[0, 3:4, :]
    sc2 = ada_ref[0, 4:5, :]
    lane = lax.broadcasted_iota(jnp.int32, (WO_SUB, LANES), 1).astype(F32)

    def mix_of(rows):
        return (jnp.dot(ona_ref[rows, :], wo_ref[:NA_WIDTH, :], preferred_element_type=F32)
                + jnp.dot(odf_ref[rows, :], wo_ref[NA_WIDTH:, :], preferred_element_type=F32))

    def route(rows, mix):
        x1 = x_ref[rows, :] + gt1 * mix
        x1_ref[rows, :] = x1
        ms = jnp.mean(x1 * x1, axis=-1, keepdims=True)
        h2 = x1 * lax.rsqrt(ms + EPS) * g_ref[...] * (1.0 + sc2) + sh2
        hi = h2.astype(BF16)
        h2_ref[rows, :] = _pack_halves(hi.astype(F32))
        lo = (h2 - hi.astype(F32)).astype(BF16)
        both = jnp.dot(hi, wrl_ref[...], preferred_element_type=F32)
        cur = (both + pltpu.roll(both, LANES - N_EXPERTS, axis=1)
               + jnp.dot(lo, wrh_ref[...], preferred_element_type=F32)) + br_ref[...]
        vals = []
        idxs = []
        for _ in range(TOP_K):
            m = jnp.max(cur, axis=-1, keepdims=True)
            idx = jnp.min(jnp.where(cur == m, lane, float(LANES)), axis=-1, keepdims=True)
            vals.append(m)
            idxs.append(idx)
            cur = jnp.where(lane == idx, -jnp.inf, cur)
        es = [jnp.exp(v - vals[0]) for v in vals]
        inv = 1.0 / (es[0] + es[1] + es[2] + es[3])
        sel = jnp.zeros((WO_SUB, LANES), F32)
        for j in range(TOP_K):
            sel = sel + jnp.where(lane == idxs[j], 1.0, 0.0)
        return idxs, [e * inv for e in es], sel

    subs = [pl.ds(r, WO_SUB) for r in range(0, x_ref.shape[0], WO_SUB)]
    routed = []
    mix_next = mix_of(subs[0])
    for j, rows in enumerate(subs):
        mix = mix_next
        if j + 1 < len(subs):
            mix_next = mix_of(subs[j + 1])
        routed.append(route(rows, mix))

    @pl.when(pl.program_id(0) == 0)
    def _():
        cnt_ref[...] = cnt0_ref[...]

    sel_all = jnp.concatenate([r[2] for r in routed], axis=0)
    before = jnp.dot(tri_ref[...], sel_all.astype(BF16), preferred_element_type=F32) + cnt_ref[...]
    cnt_ref[...] = cnt_ref[...] + jnp.sum(sel_all, axis=0, keepdims=True)

    for s, rows in enumerate(subs):
        idxs, wts, _ = routed[s]
        bef = before[s * WO_SUB:(s + 1) * WO_SUB]
        ti = jnp.zeros((WO_SUB, LANES), F32)
        tw = jnp.zeros((WO_SUB, LANES), F32)
        for j in range(TOP_K):
            rank = jnp.sum(jnp.where(lane == idxs[j], bef, 0.0), axis=-1, keepdims=True)
            ti = jnp.where(lane == float(j), idxs[j], ti)
            ti = jnp.where(lane == float(TOP_K + j), rank, ti)
            tw = jnp.where(lane == float(j), wts[j], tw)
        ti_ref[:, s * WO_SUB:(s + 1) * WO_SUB] = ti.T[:2 * TOP_K, :].astype(jnp.int32)
        tw_ref[rows, :] = tw


def _wo(o_na, o_df, x2d, ada_g, w_o_bf, g_ffn, wr_hi, wr_lo, br_pad, tri, cnt0, seq):
    n = x2d.shape[0]
    tm = WO_BLOCK
    per_seq = seq // tm
    row = lambda i: (i, 0)
    const = lambda i: (0, 0)
    return pl.pallas_call(
        _wo_kernel,
        grid=(n // tm,),
        in_specs=[pl.BlockSpec((tm, NA_WIDTH), row),
                  pl.BlockSpec((tm, DIFF_WIDTH), row),
                  pl.BlockSpec((tm, D_MODEL), row),
                  pl.BlockSpec((1, 6, D_MODEL), lambda i: (i // per_seq, 0, 0)),
                  pl.BlockSpec((D_MODEL, D_MODEL), const),
                  pl.BlockSpec((1, D_MODEL), const),
                  pl.BlockSpec((D_MODEL, LANES), const),
                  pl.BlockSpec((D_MODEL, LANES), const),
                  pl.BlockSpec((1, LANES), const),
                  pl.BlockSpec((tm, tm), const),
                  pl.BlockSpec((1, LANES), const)],
        out_specs=[pl.BlockSpec((tm, D_MODEL), row),
                   pl.BlockSpec((tm, PACKED), row),
                   pl.BlockSpec((2 * TOP_K, tm), lambda i: (0, i)),
                   pl.BlockSpec((tm, LANES), row),
                   pl.BlockSpec((1, LANES), const)],
        out_shape=[jax.ShapeDtypeStruct((n, D_MODEL), F32),
                   jax.ShapeDtypeStruct((n, PACKED), U32),
                   jax.ShapeDtypeStruct((2 * TOP_K, n), jnp.int32),
                   jax.ShapeDtypeStruct((n, LANES), F32),
                   jax.ShapeDtypeStruct((1, LANES), F32)],
        compiler_params=_params("arbitrary"),
        name="wo_router",
    )(o_na, o_df, x2d, ada_g, w_o_bf, g_ffn, wr_hi, wr_lo, br_pad, tri, cnt0)


def _expert_kernel(be_ref, nxt_ref, fill_ref, na_ref, xs_ref, wg_hbm, bg_ref, wu_hbm, bu_ref, wd_hbm, bd_ref, o_ref,
                   w_f32, w_bf, sems, slot_ref):
    i = pl.program_id(0)
    active = i < na_ref[0]
    expert = be_ref[i]
    new_expert = jnp.logical_or(i == 0, expert != be_ref[jnp.maximum(i - 1, 0)])
    wg_bf, wu_bf, wd_bf = w_bf.at[0], w_bf.at[1], w_bf.at[2]

    def weight_copies(src_expert, slot):
        return [pltpu.make_async_copy(w_hbm.at[src_expert], w_f32.at[slot, j], sems.at[slot, j])
                for j, w_hbm in enumerate((wg_hbm, wu_hbm, wd_hbm))]

    @pl.when(i == 0)
    def _():
        slot_ref[0] = 0
        for cp in weight_copies(expert, 0):
            cp.start()

    @pl.when(jnp.logical_and(active, new_expert))
    def _():
        slot = slot_ref[0]
        nxt = nxt_ref[i]
        for s in range(2):
            @pl.when(slot == s)
            def _():
                @pl.when(nxt >= 0)
                def _():
                    for cp in weight_copies(nxt, 1 - s):
                        cp.start(priority=1)
                for j, cp in enumerate(weight_copies(expert, s)):
                    cp.wait()
                    w_bf[j] = w_f32[s, j].astype(BF16)
        slot_ref[0] = 1 - slot

    def run_half(first_row, n_sub):
        def gate_up(rows):
            x_lo, x_hi = _unpack_halves(xs_ref[rows, :])
            x_lo = x_lo.astype(BF16)
            x_hi = x_hi.astype(BF16)

            def proj(w_bf):
                return (jnp.dot(x_lo, w_bf[:PACKED, :], preferred_element_type=F32)
                        + jnp.dot(x_hi, w_bf[PACKED:, :], preferred_element_type=F32))

            return proj(wg_bf), proj(wu_bf)

        def act_down(rows, gu):
            g = jnp.minimum(gu[0] + bg_ref[0], SWIGLU_LIMIT)
            u = jnp.clip(gu[1] + bu_ref[0], -SWIGLU_LIMIT, SWIGLU_LIMIT)
            act = g * jax.nn.sigmoid(SWIGLU_ALPHA * g) * (u + 1.0)
            out = jnp.dot(act.astype(BF16), wd_bf[...], preferred_element_type=F32) + bd_ref[0]
            o_ref[rows, :] = _pack_halves(out.astype(BF16).astype(F32))

        sub = [pl.ds(first_row + r * EXPERT_SUB, EXPERT_SUB) for r in range(n_sub)]
        gu_next = gate_up(sub[0])
        for j, rows in enumerate(sub):
            gu = gu_next
            if j + 1 < len(sub):
                gu_next = gate_up(sub[j + 1])
            act_down(rows, gu)

    fill = jnp.where(active, fill_ref[i], 0)
    subs_per_half = EXPERT_HALF // EXPERT_SUB

    @pl.when(fill < 2 * subs_per_half)
    def _():
        o_ref[...] = jnp.zeros_like(o_ref)

    full = fill == 2 * subs_per_half

    @pl.when(full)
    def _():
        run_half(0, 2 * subs_per_half)

    for half in range(2):
        for n_sub in range(1, subs_per_half + 1):
            here = fill - half * subs_per_half
            cond = (here >= n_sub) if n_sub == subs_per_half else (here == n_sub)

            @pl.when(jnp.logical_and(cond, jnp.logical_not(full)))
            def _(half=half, n_sub=n_sub):
                run_half(half * EXPERT_HALF, n_sub)


def _experts(block_e, next_e, fill, n_active, xs, wg, bg, wu, bu, wd, bd):
    cap = xs.shape[0]
    n_blocks = cap // EXPERT_BLOCK
    xmap = lambda i, be, nx, fl, na: (jnp.minimum(i, na[0] - 1), 0)
    bmap = lambda i, be, nx, fl, na: (be[i], 0, 0)
    hbm = pl.BlockSpec(memory_space=pl.ANY)
    grid_spec = pltpu.PrefetchScalarGridSpec(
        num_scalar_prefetch=4,
        grid=(n_blocks,),
        in_specs=[pl.BlockSpec((EXPERT_BLOCK, PACKED), xmap),
                  hbm, pl.BlockSpec((1, 1, D_MODEL), bmap),
                  hbm, pl.BlockSpec((1, 1, D_MODEL), bmap),
                  hbm, pl.BlockSpec((1, 1, D_MODEL), bmap)],
        out_specs=pl.BlockSpec((EXPERT_BLOCK, PACKED), lambda i, be, nx, fl, na: (i, 0)),
        scratch_shapes=[pltpu.VMEM((2, 3, D_MODEL, D_MODEL), F32),
                        pltpu.VMEM((3, D_MODEL, D_MODEL), BF16),
                        pltpu.SemaphoreType.DMA((2, 3)),
                        pltpu.SMEM((1,), jnp.int32)],
    )
    return pl.pallas_call(
        _expert_kernel,
        grid_spec=grid_spec,
        out_shape=jax.ShapeDtypeStruct((cap, PACKED), U32),
        compiler_params=_params("arbitrary"),
        name="experts",
    )(block_e, next_e, fill, n_active, xs, wg, bg, wu, bu, wd, bd)


def _sc_gather(table, idx):
    n_out = idx.shape[0]
    width = table.shape[1]
    per_worker = n_out // SC_WORKERS
    n_chunks = per_worker // GATHER_ROWS
    assert per_worker * SC_WORKERS == n_out and n_chunks * GATHER_ROWS == per_worker and n_chunks % 2 == 0
    idx3 = idx.reshape(SC_WORKERS, n_chunks, GATHER_ROWS)
    mesh = plsc.VectorSubcoreMesh(core_axis_name="core", subcore_axis_name="subcore")

    @functools.partial(
        pl.kernel, mesh=mesh,
        out_type=jax.ShapeDtypeStruct((n_out, width), table.dtype),
        scratch_types=[pltpu.VMEM((n_chunks, GATHER_ROWS), jnp.int32),
                       pltpu.VMEM((2, GATHER_ROWS, width), table.dtype),
                       pltpu.SemaphoreType.DMA((2,)),
                       pltpu.SemaphoreType.DMA((2,))])
    def gather_kernel(table_hbm, idx_hbm, out_hbm, idx_v, rows_v, gsem, wsem):
        wid = lax.axis_index("subcore") * SC_CORES + lax.axis_index("core")
        base = wid * per_worker
        pltpu.sync_copy(idx_hbm.at[wid], idx_v)

        def gather(j, slot):
            return pltpu.make_async_copy(table_hbm.at[idx_v.at[j]], rows_v.at[slot], gsem.at[slot])

        def write(j, slot):
            dst = out_hbm.at[pl.ds(pl.multiple_of(base + j * GATHER_ROWS, GATHER_ROWS), GATHER_ROWS)]
            return pltpu.make_async_copy(rows_v.at[slot], dst, wsem.at[slot])

        gather(0, 0).start()

        @pl.loop(0, n_chunks, step=2)
        def _(j):
            for slot in range(2):
                jj = j + slot
                gather(jj, slot).wait()

                @pl.when(jj >= 1)
                def _():
                    write(jj - 1, 1 - slot).wait()

                @pl.when(jj + 1 < n_chunks)
                def _():
                    gather(jj + 1, 1 - slot).start()

                write(jj, slot).start()

        write(n_chunks - 1, 1).wait()

    return gather_kernel(table, idx3)


def _sc_scatter(rows, idx, n_out):
    n_src, width = rows.shape
    n_idx = idx.shape[0]
    per_worker = n_idx // SC_WORKERS
    n_chunks = per_worker // GATHER_ROWS
    assert per_worker * SC_WORKERS == n_idx and n_chunks * GATHER_ROWS == per_worker and n_chunks % 2 == 0
    assert n_src % per_worker == 0
    idx3 = idx.reshape(SC_WORKERS, n_chunks, GATHER_ROWS)
    mesh = plsc.VectorSubcoreMesh(core_axis_name="core", subcore_axis_name="subcore")

    @functools.partial(
        pl.kernel, mesh=mesh,
        out_type=jax.ShapeDtypeStruct((n_out, width), rows.dtype),
        scratch_types=[pltpu.VMEM((n_chunks, GATHER_ROWS), jnp.int32),
                       pltpu.VMEM((2, GATHER_ROWS, width), rows.dtype),
                       pltpu.SemaphoreType.DMA((2,)),
                       pltpu.SemaphoreType.DMA((2,))])
    def scatter_kernel(rows_hbm, idx_hbm, out_hbm, idx_v, rows_v, rsem, wsem):
        wid = lax.axis_index("subcore") * SC_CORES + lax.axis_index("core")
        base = lax.rem(wid * per_worker, n_src)
        pltpu.sync_copy(idx_hbm.at[wid], idx_v)

        def read(j, slot):
            src = rows_hbm.at[pl.ds(pl.multiple_of(base + j * GATHER_ROWS, GATHER_ROWS), GATHER_ROWS)]
            return pltpu.make_async_copy(src, rows_v.at[slot], rsem.at[slot])

        def write(j, slot):
            return pltpu.make_async_copy(rows_v.at[slot], out_hbm.at[idx_v.at[j]], wsem.at[slot])

        read(0, 0).start()

        @pl.loop(0, n_chunks, step=2)
        def _(j):
            for slot in range(2):
                jj = j + slot
                read(jj, slot).wait()

                @pl.when(jj >= 1)
                def _():
                    write(jj - 1, 1 - slot).wait()

                @pl.when(jj + 1 < n_chunks)
                def _():
                    read(jj + 1, 1 - slot).start()

                write(jj, slot).start()

        write(n_chunks - 1, 1).wait()

    return scatter_kernel(rows, idx3)


def _combine_kernel(x1_ref, y0_ref, y1_ref, y2_ref, y3_ref, tw_ref, ada_ref, o_ref):
    tw = tw_ref[...]
    acc_lo = jnp.zeros((x1_ref.shape[0], PACKED), F32)
    acc_hi = jnp.zeros((x1_ref.shape[0], PACKED), F32)
    for j, y_ref in enumerate((y0_ref, y1_ref, y2_ref, y3_ref)):
        lo, hi = _unpack_halves(y_ref[...])
        acc_lo = acc_lo + tw[:, j:j + 1] * lo
        acc_hi = acc_hi + tw[:, j:j + 1] * hi
    o_ref[:, :PACKED] = x1_ref[:, :PACKED] + ada_ref[0, 5:6, :PACKED] * acc_lo
    o_ref[:, PACKED:] = x1_ref[:, PACKED:] + ada_ref[0, 5:6, PACKED:] * acc_hi


def _combine(x1, ys, tw, ada_g, seq, row_off):
    n = x1.shape[0]
    tm = ROW_BLOCK
    per_seq = seq // tm
    off = row_off // tm
    per_choice = tw.shape[0] // tm
    y_specs = [pl.BlockSpec((tm, PACKED), functools.partial(lambda i, j: (j * per_choice + off + i, 0), j=j))
               for j in range(TOP_K)]
    return pl.pallas_call(
        _combine_kernel,
        grid=(n // tm,),
        in_specs=[pl.BlockSpec((tm, D_MODEL), lambda i: (i, 0)),
                  *y_specs,
                  pl.BlockSpec((tm, LANES), lambda i: (i + off, 0)),
                  pl.BlockSpec((1, 6, D_MODEL), lambda i: (i // per_seq, 0, 0))],
        out_specs=pl.BlockSpec((tm, D_MODEL), lambda i: (i, 0)),
        out_shape=jax.ShapeDtypeStruct((n, D_MODEL), F32),
        compiler_params=_params("arbitrary"),
        name="combine",
    )(x1, ys, ys, ys, ys, tw, ada_g)


def _rope_tables(seq):
    half = HEAD_DIM // 2
    inv = ROPE_THETA ** (-jnp.arange(half, dtype=F32) / half)
    ang = jnp.arange(seq, dtype=F32)[:, None] * inv[None, :]
    cos, sin = jnp.cos(ang), jnp.sin(ang)
    cos_h = jnp.concatenate([cos, cos], axis=-1)
    sin_h = jnp.concatenate([-sin, sin], axis=-1)
    reps = LANES // HEAD_DIM
    return jnp.tile(cos_h, (1, reps)), jnp.tile(sin_h, (1, reps))


def _na_bias_table(rpb):
    cols = jnp.arange(GRID_W, dtype=jnp.int32)
    c_start = jnp.clip(cols - NA_WIN_COLS // 2, 0, GRID_W - NA_WIN_COLS)
    col_mask = (cols[None, :] >= c_start[:, None]) & (cols[None, :] < c_start[:, None] + NA_WIN_COLS)
    col_idx = jnp.clip(cols[None, :] - cols[:, None], -(NA_WIN_COLS - 1), NA_WIN_COLS - 1) + NA_WIN_COLS - 1
    delta = jnp.arange(NA_WIN_ROWS, dtype=jnp.int32)
    j = jnp.arange(NA_WIN_ROWS, dtype=jnp.int32)
    row_idx = j[None, :] - delta[:, None] + NA_WIN_ROWS - 1
    row_hot = (row_idx[:, :, None] == jnp.arange(2 * NA_WIN_ROWS - 1, dtype=jnp.int32)).astype(F32)
    col_hot = (col_idx[:, :, None] == jnp.arange(2 * NA_WIN_COLS - 1, dtype=jnp.int32)).astype(F32)
    rpb_pairs = rpb.astype(F32).reshape(NA_HEADS // 2, 2, 2 * NA_WIN_ROWS - 1, 2 * NA_WIN_COLS - 1)
    pair_hot = jnp.eye(2, dtype=F32)
    lane_hot = jnp.einsum('qkc,hx->khqxc', col_hot, pair_hot).reshape(GRID_W, LANES, 2, 2 * NA_WIN_COLS - 1)
    bias = jnp.einsum('djr,pxrc,klxc->pdjkl', row_hot, rpb_pairs, lane_hot,
                      precision=lax.Precision.HIGHEST)
    lane_mask = jnp.concatenate([col_mask.T, col_mask.T], axis=1)
    bias = jnp.where(lane_mask[None, None, None], bias * LOG2E, NEG_INF)
    return bias.reshape(NA_HEADS // 2, NA_WIN_ROWS, NA_WIN_ROWS * GRID_W, LANES)


def _routing(top_idx, rank, counts, n):
    n_blocks = n * TOP_K // EXPERT_BLOCK + N_EXPERTS
    experts = jnp.arange(N_EXPERTS, dtype=jnp.int32)
    padded = (counts + EXPERT_BLOCK - 1) // EXPERT_BLOCK * EXPERT_BLOCK
    pad_end = jnp.cumsum(padded)
    pad_start = pad_end - padded
    start_of = jnp.sum(jnp.where(top_idx[None] == experts[:, None, None], pad_start[:, None, None], 0), axis=0)
    dest = (start_of + rank).reshape(-1)
    block_lo = jnp.arange(n_blocks, dtype=jnp.int32) * EXPERT_BLOCK
    block_e = jnp.minimum(jnp.sum((pad_end[None, :] <= block_lo[:, None]).astype(jnp.int32), axis=1),
                          N_EXPERTS - 1).astype(jnp.int32)
    n_active = (pad_end[-1] // EXPERT_BLOCK).astype(jnp.int32).reshape(1)
    later = jnp.where((experts[None, :] > experts[:, None]) & (padded[None, :] > 0), experts[None, :], N_EXPERTS)
    next_nonempty = jnp.min(later, axis=1)
    next_nonempty = jnp.where(next_nonempty == N_EXPERTS, -1, next_nonempty)
    own = block_e[:, None] == experts[None, :]
    next_e = jnp.sum(jnp.where(own, next_nonempty[None, :], 0), axis=1)
    real_end = jnp.sum(jnp.where(own, (pad_start + counts)[None, :], 0), axis=1)
    fill = jnp.clip((real_end - block_lo + EXPERT_SUB - 1) // EXPERT_SUB, 0, EXPERT_BLOCK // EXPERT_SUB)
    return dest, block_e, next_e.astype(jnp.int32), fill.astype(jnp.int32), n_active, n_blocks * EXPERT_BLOCK


def kernel(x_prompt, x_sample, c_prompt, c_sample, w_ada, b_ada, g_attn_norm, w_qkv, na_q_norm, na_k_norm, na_rpb, diff_q_norm, diff_k_norm, lambda_q1, lambda_k1, lambda_q2, lambda_k2, diff_subln, w_o, g_ffn_norm, w_router, b_router, w_gate, b_gate, w_up, b_up, w_down, b_down):
    l = 0
    groups = [(x_prompt, c_prompt), (x_sample, c_sample)]
    nb = [x.shape[0] for x, _ in groups]

    ada_all = _ada(jnp.concatenate([c for _, c in groups], axis=0), w_ada[l], b_ada[l])
    ada_all = ada_all.reshape(sum(nb), 6, D_MODEL)

    w_qkv_bf = w_qkv[l].astype(BF16)
    w_o_bf = w_o[l].astype(BF16)
    scale = HEAD_DIM ** -0.5
    reps = NA_WIDTH // HEAD_DIM
    gains = jnp.stack([jnp.tile(na_q_norm[l], reps) * (scale * LOG2E),
                       jnp.tile(na_k_norm[l], reps),
                       jnp.tile(diff_q_norm[l], reps) * (scale * LOG2E),
                       jnp.tile(diff_k_norm[l], reps)]).astype(F32)
    head_id = jnp.arange(MXU_DIM, dtype=jnp.int32) // HEAD_DIM
    bd = (head_id[:, None] == head_id[None, :]).astype(BF16)
    bias_t = _na_bias_table(na_rpb[l])
    lam_vecs = jnp.stack([lambda_q1[l], lambda_k1[l], lambda_q2[l], lambda_k2[l]]).astype(F32)
    subln_col = (diff_subln[l].astype(F32) * (1.0 - LAMBDA_INIT)).reshape(LANES, 1)
    wr = w_router[l].astype(F32)
    wr_pad = jnp.zeros((D_MODEL, LANES), F32).at[:, :N_EXPERTS].set(wr)
    wr_hi = wr_pad.astype(BF16)
    wr_lo = (wr_pad - wr_hi.astype(F32)).astype(BF16)
    wr_lo = wr_hi.at[:, N_EXPERTS:2 * N_EXPERTS].set(wr_lo[:, :N_EXPERTS])
    br_pad = jnp.full((1, LANES), NEG_INF, F32).at[0, :N_EXPERTS].set(b_router[l].astype(F32))
    g_attn = g_attn_norm[l].reshape(1, D_MODEL).astype(F32)
    g_ffn = g_ffn_norm[l].reshape(1, D_MODEL).astype(F32)
    max_seq = max(x.shape[1] for x, _ in groups)
    cos_t, sin_t = _rope_tables(max_seq)

    rows = lax.broadcasted_iota(jnp.int32, (WO_BLOCK, WO_BLOCK), 0)
    cols = lax.broadcasted_iota(jnp.int32, (WO_BLOCK, WO_BLOCK), 1)
    tri = (cols < rows).astype(BF16)
    cnt0 = jnp.zeros((1, LANES), F32)

    bg = b_gate[l].reshape(N_EXPERTS, 1, D_MODEL).astype(F32)
    bu = b_up[l].reshape(N_EXPERTS, 1, D_MODEL).astype(F32)
    bdn = b_down[l].reshape(N_EXPERTS, 1, D_MODEL).astype(F32)
    ada_groups = [ada_all[:nb[0]], ada_all[nb[0]:]]

    order = sorted(range(len(groups)), key=lambda g: -groups[g][0].shape[1])
    staged = {}
    for g in order:
        x = groups[g][0]
        b, seq = x.shape[0], x.shape[1]
        n = b * seq
        x2d = x.reshape(n, D_MODEL)
        ada_g = ada_groups[g]
        qkv = _qkv(x2d, ada_g, g_attn, w_qkv_bf, gains, cos_t, sin_t, bd, seq)
        o_na = _na(qkv, bias_t, b, seq)
        o_df = _diff(qkv, lam_vecs, subln_col, b, seq)
        x1, h2, ti, tw, cnt = _wo(o_na, o_df, x2d, ada_g, w_o_bf, g_ffn, wr_hi, wr_lo, br_pad, tri, cnt0, seq)
        counts = cnt[0, :N_EXPERTS].astype(jnp.int32)
        dest, *blocks, cap = _routing(ti[:TOP_K], ti[TOP_K:], counts, n)
        xs = _sc_scatter(h2, dest, cap)
        staged[g] = (x1, tw, ada_g, dest, blocks, xs, seq, b)

    sorted_out = {}
    for g in order:
        blocks, xs = staged[g][4], staged[g][5]
        sorted_out[g] = _experts(*blocks, xs, w_gate[l], bg, w_up[l], bu, w_down[l], bdn)

    outs = [None] * len(groups)
    for g in order:
        x1, tw, ada_g, dest, blocks, xs, seq, b = staged[g]
        ys = _sc_gather(sorted_out[g], dest)
        outs[g] = _combine(x1, ys, tw, ada_g, seq, 0).reshape(b, seq, D_MODEL)
    return tuple(outs)
```
